```python
import math
import jax, jax.numpy as jnp
from jax import lax
import numpy as np

D_MODEL = 1024
BATCH = 8
SEQ = 16384
DEPTH = 2

NORM_EPS = 1e-6
CONV_K = 4
ATTN_HEADS = 8
ATTN_HEAD_DIM = 64
ATTN_WIDTH = ATTN_HEADS * ATTN_HEAD_DIM
DILATED_PATTERNS = ((128, 1), (512, 4), (2048, 16))
ATTN_BLOCK = 128
LRU_WIDTH = D_MODEL // 2
LRU_BLOCKS = 8
LRU_C = 8.0
AB_IN = 3 * ATTN_WIDTH + 2 * LRU_WIDTH
DN_HEADS = 8
DN_HEAD_DIM = 128
DN_WIDTH = DN_HEADS * DN_HEAD_DIM
DN_CHUNK = 64
DN_IN = 4 * DN_WIDTH + 2 * DN_HEADS
XA_HEADS = 4
XA_HEAD_DIM = D_MODEL // XA_HEADS
N_MEM = 256
D_FF = ((8 * D_MODEL // 3 + 127) // 128) * 128

kernel_name = "hybrid_dilated_attn_rglru_gdn_macaron"


def rmsnorm(x, g):
    xf = x.astype(jnp.float32)
    y = xf * lax.rsqrt(jnp.mean(xf * xf, axis=-1, keepdims=True) + NORM_EPS)
    return (y * g.astype(jnp.float32)).astype(x.dtype)


def swiglu(h, w_in, w_out):
    gate, up = jnp.split(h @ w_in, 2, axis=-1)
    return (jax.nn.silu(gate) * up) @ w_out


def causal_dwconv(x, w):
    C = x.shape[-1]
    return lax.conv_general_dilated(
        x, w[:, None, :].astype(x.dtype), window_strides=(1,),
        padding=((w.shape[0] - 1, 0),), dimension_numbers=('NWC', 'WIO', 'NWC'),
        feature_group_count=C)


def dilated_window_attention(q, k, v, dilation, n_back):
    B, S, H, hd = q.shape
    L = S // dilation
    Bd = B * dilation
    Lp = -(-L // ATTN_BLOCK) * ATTN_BLOCK
    nb = Lp // ATTN_BLOCK

    def split(t):
        t = t.reshape(B, L, dilation, H, hd).transpose(0, 2, 1, 3, 4).reshape(Bd, L, H, hd)
        return jnp.pad(t, ((0, 0), (0, Lp - L), (0, 0), (0, 0))).reshape(Bd, nb, ATTN_BLOCK, H, hd)

    qb, kb, vb = split(q), split(k), split(v)

    def with_prev(t):
        prev = jnp.concatenate([jnp.zeros_like(t[:, :1]), t[:, :-1]], axis=1)
        return jnp.concatenate([prev, t], axis=2)

    kk, vv = with_prev(kb), with_prev(vb)
    s = jnp.einsum('bnqhd,bnkhd->bnhqk', qb, kk).astype(jnp.float32) * (hd ** -0.5)
    qi = jnp.arange(ATTN_BLOCK)[:, None]
    kj = jnp.arange(2 * ATTN_BLOCK)[None, :]
    dist = qi + ATTN_BLOCK - kj
    band = (dist >= 0) & (dist <= n_back)
    not_first = (jnp.arange(nb) > 0)[:, None, None] | (kj >= ATTN_BLOCK)[None]
    valid = band[None] & not_first
    s = jnp.where(valid[None, :, None], s, -jnp.inf)
    m = jnp.max(s, axis=-1, keepdims=True)
    p = jnp.exp(s - m)
    den = jnp.sum(p, axis=-1)
    o = jnp.einsum('bnhqk,bnkhd->bnqhd', p, vv.astype(jnp.float32))
    den_t = jnp.moveaxis(den, 2, 3)
    o = o / den_t[..., None]
    lse = jnp.moveaxis(m[..., 0], 2, 3) + jnp.log(den_t)
    o = o.reshape(Bd, Lp, H, hd)[:, :L].reshape(B, dilation, L, H, hd).transpose(0, 2, 1, 3, 4)
    lse = lse.reshape(Bd, Lp, H)[:, :L].reshape(B, dilation, L, H).transpose(0, 2, 1, 3)
    return o.reshape(B, S, H, hd), lse.reshape(B, S, H)


def linear_scan(a, b):
    def comb(l, r):
        return (l[0] * r[0], r[0] * l[1] + r[1])
    _, h = lax.associative_scan(comb, (a, b), axis=1)
    return h


def attn_lru_mixer(h, w_in, conv_w, conv_b, w_a, b_a, w_x, b_x, lam, w_out):
    B, S, _ = h.shape
    A = ATTN_WIDTH
    q, k, v, xr, gr = jnp.split(h @ w_in, [A, 2 * A, 3 * A, 3 * A + LRU_WIDTH], axis=-1)
    shp = (B, S, ATTN_HEADS, ATTN_HEAD_DIM)
    q, k, v = q.reshape(shp), k.reshape(shp), v.reshape(shp)
    outs, lses = [], []
    for window, dil in DILATED_PATTERNS:
        o, l = dilated_window_attention(q, k, v, dil, window // dil)
        outs.append(o)
        lses.append(l)
    wts = jax.nn.softmax(jnp.stack(lses, 0), axis=0)
    attn = jnp.einsum('gbsh,gbshd->bshd', wts, jnp.stack(outs, 0))
    attn = attn.reshape(B, S, A).astype(h.dtype)
    xc = (causal_dwconv(xr, conv_w) + conv_b).astype(jnp.float32)
    xb = xc.reshape(B, S, LRU_BLOCKS, LRU_WIDTH // LRU_BLOCKS)
    r = jax.nn.sigmoid(jnp.einsum('bsnj,njk->bsnk', xb, w_a.astype(jnp.float32)).reshape(B, S, LRU_WIDTH)
                       + b_a.astype(jnp.float32))
    i = jax.nn.sigmoid(jnp.einsum('bsnj,njk->bsnk', xb, w_x.astype(jnp.float32)).reshape(B, S, LRU_WIDTH)
                       + b_x.astype(jnp.float32))
    log_a = -LRU_C * r * jax.nn.softplus(-lam.astype(jnp.float32))
    a = jnp.exp(log_a)
    mult = jnp.sqrt(-jnp.expm1(2.0 * log_a))
    hs = linear_scan(a, mult * i * xc)
    y = hs.astype(h.dtype) * jax.nn.gelu(gr)
    return jnp.concatenate([attn, y], axis=-1) @ w_out


def l2norm(t):
    return t * lax.rsqrt(jnp.sum(t * t, axis=-1, keepdims=True) + 1e-6)


def gated_delta_rule_chunked(q, k, v, g, beta):
    B, S, H, dk = q.shape
    dv = v.shape[-1]
    C = DN_CHUNK
    N = S // C

    def ch(t):
        t = jnp.moveaxis(t, 2, 1)
        return t.reshape((B, H, N, C) + t.shape[3:])

    q = ch(q * (dk ** -0.5))
    k, v, g, beta = ch(k), ch(v), ch(g), ch(beta)
    gcum = jnp.cumsum(g, axis=-1)
    tril = jnp.tril(jnp.ones((C, C), bool))
    strict = jnp.tril(jnp.ones((C, C), bool), -1)
    decay = jnp.exp(jnp.where(tril, gcum[..., :, None] - gcum[..., None, :], -jnp.inf))
    kb = k * beta[..., None]
    kkt = jnp.einsum('bhnid,bhnjd->bhnij', kb, k) * decay
    amat = jnp.eye(C, dtype=jnp.float32) + jnp.where(strict, kkt, 0.0)
    rhs = jnp.concatenate([v * beta[..., None], kb * jnp.exp(gcum)[..., None]], axis=-1)
    sol = lax.linalg.triangular_solve(amat, rhs, left_side=True, lower=True, unit_diagonal=True)
    u, w = sol[..., :dv], sol[..., dv:]
    qk = jnp.einsum('bhnid,bhnjd->bhnij', q, k) * decay

    def step(state, inp):
        q_i, k_i, u_i, w_i, qk_i, g_i = inp
        v_new = u_i - jnp.einsum('bhcd,bhde->bhce', w_i, state)
        o = (jnp.einsum('bhcd,bhde->bhce', q_i * jnp.exp(g_i)[..., None], state)
             + jnp.einsum('bhij,bhje->bhie', qk_i, v_new))
        g_last = g_i[..., -1]
        state = (state * jnp.exp(g_last)[..., None, None]
                 + jnp.einsum('bhcd,bhce->bhde', k_i * jnp.exp(g_last[..., None] - g_i)[..., None], v_new))
        return state, o

    xs = tuple(jnp.moveaxis(t, 2, 0) for t in (q, k, u, w, qk, gcum))
    state0 = jnp.zeros((B, H, dk, dv), jnp.float32)
    _, o = lax.scan(step, state0, xs)
    o = jnp.transpose(o, (1, 0, 3, 2, 4)).reshape(B, S, H, dv)
    return o


def deltanet_mixer(h, w_in, conv_w, a_log, dt_bias, o_norm, w_out):
    B, S, _ = h.shape
    W = DN_WIDTH
    qkv, z, a, b = jnp.split(h @ w_in, [3 * W, 4 * W, 4 * W + DN_HEADS], axis=-1)
    qkv = jax.nn.silu(causal_dwconv(qkv, conv_w)).astype(jnp.float32)
    q, k, v = jnp.split(qkv, 3, axis=-1)
    shp = (B, S, DN_HEADS, DN_HEAD_DIM)
    q, k, v = l2norm(q.reshape(shp)), l2norm(k.reshape(shp)), v.reshape(shp)
    beta = jax.nn.sigmoid(b.astype(jnp.float32))
    g = -jnp.exp(a_log.astype(jnp.float32)) * jax.nn.softplus(a.astype(jnp.float32) + dt_bias.astype(jnp.float32))
    o = gated_delta_rule_chunked(q, k, v, g, beta)
    o = rmsnorm(o, o_norm) * jax.nn.silu(z.reshape(shp).astype(jnp.float32))
    return o.reshape(B, S, W).astype(h.dtype) @ w_out


def cross_attention(h, mem_h, wq, wkv, wo):
    B, S, _ = h.shape
    M = mem_h.shape[1]
    q = (h @ wq).reshape(B, S, XA_HEADS, XA_HEAD_DIM)
    k, v = jnp.split(mem_h @ wkv, 2, axis=-1)
    k = k.reshape(B, M, XA_HEADS, XA_HEAD_DIM)
    v = v.reshape(B, M, XA_HEADS, XA_HEAD_DIM)
    s = jnp.einsum('bshd,bmhd->bhsm', q, k).astype(jnp.float32) * (XA_HEAD_DIM ** -0.5)
    p = jax.nn.softmax(s, axis=-1).astype(v.dtype)
    o = jnp.einsum('bhsm,bmhd->bshd', p, v).reshape(B, S, D_MODEL)
    return o @ wo


def _fwd_setup_inputs(seed: int = 0) -> dict:
    key = jax.random.key(seed)
    keys = iter(jax.random.split(key, 64))
    n_even = (DEPTH + 1) // 2
    n_odd = DEPTH // 2
    f32 = jnp.float32

    def nrm(shape, fan_in):
        return jax.random.normal(next(keys), shape, f32) * (fan_in ** -0.5)

    def gain(shape):
        return 1.0 + 0.02 * jax.random.normal(next(keys), shape, f32)

    def small(shape):
        return 0.01 * jax.random.normal(next(keys), shape, f32)

    x = jax.random.normal(next(keys), (BATCH, SEQ, D_MODEL), f32)
    mem = jax.random.normal(next(keys), (BATCH, N_MEM, D_MODEL), f32)
    blk = LRU_WIDTH // LRU_BLOCKS
    a0 = jax.random.uniform(next(keys), (n_even, LRU_WIDTH), f32, 0.9, 0.999)
    a_base = a0 ** (1.0 / LRU_C)
    lam = jnp.log(a_base) - jnp.log1p(-a_base)
    dt = jnp.exp(jax.random.uniform(next(keys), (n_odd, DN_HEADS), f32, math.log(1e-3), math.log(0.1)))
    dt_bias = dt + jnp.log(-jnp.expm1(-dt))
    a_log = jnp.log(jax.random.uniform(next(keys), (n_odd, DN_HEADS), f32, 1.0, 16.0))
    return {
        "x": x,
        "mem": mem,
        "ffn1_norm": gain((DEPTH, D_MODEL)),
        "ffn1_w_in": nrm((DEPTH, D_MODEL, 2 * D_FF), D_MODEL),
        "ffn1_w_out": nrm((DEPTH, D_FF, D_MODEL), D_FF),
        "mix_norm": gain((DEPTH, D_MODEL)),
        "xa_norm": gain((DEPTH, D_MODEL)),
        "xa_mem_norm": gain((DEPTH, D_MODEL)),
        "xa_wq": nrm((DEPTH, D_MODEL, D_MODEL), D_MODEL),
        "xa_wkv": nrm((DEPTH, D_MODEL, 2 * D_MODEL), D_MODEL),
        "xa_wo": nrm((DEPTH, D_MODEL, D_MODEL), D_MODEL),
        "ffn2_norm": gain((DEPTH, D_MODEL)),
        "ffn2_w_in": nrm((DEPTH, D_MODEL, 2 * D_FF), D_MODEL),
        "ffn2_w_out": nrm((DEPTH, D_FF, D_MODEL), D_FF),
        "ab_w_in": nrm((n_even, D_MODEL, AB_IN), D_MODEL),
        "lru_conv_w": nrm((n_even, CONV_K, LRU_WIDTH), CONV_K),
        "lru_conv_b": small((n_even, LRU_WIDTH)),
        "lru_w_a": nrm((n_even, LRU_BLOCKS, blk, blk), blk),
        "lru_b_a": small((n_even, LRU_WIDTH)),
        "lru_w_x": nrm((n_even, LRU_BLOCKS, blk, blk), blk),
        "lru_b_x": small((n_even, LRU_WIDTH)),
        "lru_lambda": lam,
        "ab_w_out": nrm((n_even, ATTN_WIDTH + LRU_WIDTH, D_MODEL), ATTN_WIDTH + LRU_WIDTH),
        "dn_w_in": nrm((n_odd, D_MODEL, DN_IN), D_MODEL),
        "dn_conv_w": nrm((n_odd, CONV_K, 3 * DN_WIDTH), CONV_K),
        "dn_a_log": a_log,
        "dn_dt_bias": dt_bias,
        "dn_o_norm": gain((n_odd, DN_HEAD_DIM)),
        "dn_w_out": nrm((n_odd, DN_WIDTH, D_MODEL), DN_WIDTH),
        "final_norm": gain((D_MODEL,)),
    }


def _fwd_reference(x, mem, ffn1_norm, ffn1_w_in, ffn1_w_out, mix_norm, xa_norm, xa_mem_norm,
              xa_wq, xa_wkv, xa_wo, ffn2_norm, ffn2_w_in, ffn2_w_out,
              ab_w_in, lru_conv_w, lru_conv_b, lru_w_a, lru_b_a, lru_w_x, lru_b_x, lru_lambda, ab_w_out,
              dn_w_in, dn_conv_w, dn_a_log, dn_dt_bias, dn_o_norm, dn_w_out, final_norm):
    for layer in range(DEPTH):
        x = x + 0.5 * swiglu(rmsnorm(x, ffn1_norm[layer]), ffn1_w_in[layer], ffn1_w_out[layer])
        h = rmsnorm(x, mix_norm[layer])
        j = layer // 2
        if layer % 2 == 0:
            x = x + attn_lru_mixer(h, ab_w_in[j], lru_conv_w[j], lru_conv_b[j], lru_w_a[j], lru_b_a[j],
                                   lru_w_x[j], lru_b_x[j], lru_lambda[j], ab_w_out[j])
        else:
            x = x + deltanet_mixer(h, dn_w_in[j], dn_conv_w[j], dn_a_log[j], dn_dt_bias[j],
                                   dn_o_norm[j], dn_w_out[j])
        x = x + cross_attention(rmsnorm(x, xa_norm[layer]), rmsnorm(mem, xa_mem_norm[layer]),
                                xa_wq[layer], xa_wkv[layer], xa_wo[layer])
        x = x + 0.5 * swiglu(rmsnorm(x, ffn2_norm[layer]), ffn2_w_in[layer], ffn2_w_out[layer])
    return rmsnorm(x, final_norm)


import jax as _jax
import jax.numpy as _jnp

TWIN_FORMAT = 'train_step'
FWD_PARAMS = ['x', 'mem', 'ffn1_norm', 'ffn1_w_in', 'ffn1_w_out', 'mix_norm', 'xa_norm', 'xa_mem_norm', 'xa_wq', 'xa_wkv', 'xa_wo', 'ffn2_norm', 'ffn2_w_in', 'ffn2_w_out', 'ab_w_in', 'lru_conv_w', 'lru_conv_b', 'lru_w_a', 'lru_b_a', 'lru_w_x', 'lru_b_x', 'lru_lambda', 'ab_w_out', 'dn_w_in', 'dn_conv_w', 'dn_a_log', 'dn_dt_bias', 'dn_o_norm', 'dn_w_out', 'final_norm']
TWIN_WEIGHTS = ['ffn1_norm', 'ffn1_w_in', 'ffn1_w_out', 'mix_norm', 'xa_norm', 'xa_mem_norm', 'xa_wq', 'xa_wkv', 'xa_wo', 'ffn2_norm', 'ffn2_w_in', 'ffn2_w_out', 'ab_w_in', 'lru_conv_w', 'lru_conv_b', 'lru_w_a', 'lru_b_a', 'lru_w_x', 'lru_b_x', 'lru_lambda', 'ab_w_out', 'dn_w_in', 'dn_conv_w', 'dn_a_log', 'dn_dt_bias', 'dn_o_norm', 'dn_w_out', 'final_norm']
TWIN_DIFF_INPUT = 'x'
TWIN_INPUTS = ['x', 'mem', 'ffn1_norm', 'ffn1_w_in', 'ffn1_w_out', 'mix_norm', 'xa_norm', 'xa_mem_norm', 'xa_wq', 'xa_wkv', 'xa_wo', 'ffn2_norm', 'ffn2_w_in', 'ffn2_w_out', 'ab_w_in', 'lru_conv_w', 'lru_conv_b', 'lru_w_a', 'lru_b_a', 'lru_w_x', 'lru_b_x', 'lru_lambda', 'ab_w_out', 'dn_w_in', 'dn_conv_w', 'dn_a_log', 'dn_dt_bias', 'dn_o_norm', 'dn_w_out', 'final_norm', 'loss_target', 'm_ffn1_norm', 'm_ffn1_w_in', 'm_ffn1_w_out', 'm_mix_norm', 'm_xa_norm', 'm_xa_mem_norm', 'm_xa_wq', 'm_xa_wkv', 'm_xa_wo', 'm_ffn2_norm', 'm_ffn2_w_in', 'm_ffn2_w_out', 'm_ab_w_in', 'm_lru_conv_w', 'm_lru_conv_b', 'm_lru_w_a', 'm_lru_b_a', 'm_lru_w_x', 'm_lru_b_x', 'm_lru_lambda', 'm_ab_w_out', 'm_dn_w_in', 'm_dn_conv_w', 'm_dn_a_log', 'm_dn_dt_bias', 'm_dn_o_norm', 'm_dn_w_out', 'm_final_norm', 'v_ffn1_norm', 'v_ffn1_w_in', 'v_ffn1_w_out', 'v_mix_norm', 'v_xa_norm', 'v_xa_mem_norm', 'v_xa_wq', 'v_xa_wkv', 'v_xa_wo', 'v_ffn2_norm', 'v_ffn2_w_in', 'v_ffn2_w_out', 'v_ab_w_in', 'v_lru_conv_w', 'v_lru_conv_b', 'v_lru_w_a', 'v_lru_b_a', 'v_lru_w_x', 'v_lru_b_x', 'v_lru_lambda', 'v_ab_w_out', 'v_dn_w_in', 'v_dn_conv_w', 'v_dn_a_log', 'v_dn_dt_bias', 'v_dn_o_norm', 'v_dn_w_out', 'v_final_norm']
TWIN_OUTPUTS = ['loss', 'grad_x', 'grad_ffn1_norm', 'grad_ffn1_w_in', 'grad_ffn1_w_out', 'grad_mix_norm', 'grad_xa_norm', 'grad_xa_mem_norm', 'grad_xa_wq', 'grad_xa_wkv', 'grad_xa_wo', 'grad_ffn2_norm', 'grad_ffn2_w_in', 'grad_ffn2_w_out', 'grad_ab_w_in', 'grad_lru_conv_w', 'grad_lru_conv_b', 'grad_lru_w_a', 'grad_lru_b_a', 'grad_lru_w_x', 'grad_lru_b_x', 'grad_lru_lambda', 'grad_ab_w_out', 'grad_dn_w_in', 'grad_dn_conv_w', 'grad_dn_a_log', 'grad_dn_dt_bias', 'grad_dn_o_norm', 'grad_dn_w_out', 'grad_final_norm', 'delta_ffn1_norm', 'delta_ffn1_w_in', 'delta_ffn1_w_out', 'delta_mix_norm', 'delta_xa_norm', 'delta_xa_mem_norm', 'delta_xa_wq', 'delta_xa_wkv', 'delta_xa_wo', 'delta_ffn2_norm', 'delta_ffn2_w_in', 'delta_ffn2_w_out', 'delta_ab_w_in', 'delta_lru_conv_w', 'delta_lru_conv_b', 'delta_lru_w_a', 'delta_lru_b_a', 'delta_lru_w_x', 'delta_lru_b_x', 'delta_lru_lambda', 'delta_ab_w_out', 'delta_dn_w_in', 'delta_dn_conv_w', 'delta_dn_a_log', 'delta_dn_dt_bias', 'delta_dn_o_norm', 'delta_dn_w_out', 'delta_final_norm', 'new_m_ffn1_norm', 'new_m_ffn1_w_in', 'new_m_ffn1_w_out', 'new_m_mix_norm', 'new_m_xa_norm', 'new_m_xa_mem_norm', 'new_m_xa_wq', 'new_m_xa_wkv', 'new_m_xa_wo', 'new_m_ffn2_norm', 'new_m_ffn2_w_in', 'new_m_ffn2_w_out', 'new_m_ab_w_in', 'new_m_lru_conv_w', 'new_m_lru_conv_b', 'new_m_lru_w_a', 'new_m_lru_b_a', 'new_m_lru_w_x', 'new_m_lru_b_x', 'new_m_lru_lambda', 'new_m_ab_w_out', 'new_m_dn_w_in', 'new_m_dn_conv_w', 'new_m_dn_a_log', 'new_m_dn_dt_bias', 'new_m_dn_o_norm', 'new_m_dn_w_out', 'new_m_final_norm', 'new_v_ffn1_norm', 'new_v_ffn1_w_in', 'new_v_ffn1_w_out', 'new_v_mix_norm', 'new_v_xa_norm', 'new_v_xa_mem_norm', 'new_v_xa_wq', 'new_v_xa_wkv', 'new_v_xa_wo', 'new_v_ffn2_norm', 'new_v_ffn2_w_in', 'new_v_ffn2_w_out', 'new_v_ab_w_in', 'new_v_lru_conv_w', 'new_v_lru_conv_b', 'new_v_lru_w_a', 'new_v_lru_b_a', 'new_v_lru_w_x', 'new_v_lru_b_x', 'new_v_lru_lambda', 'new_v_ab_w_out', 'new_v_dn_w_in', 'new_v_dn_conv_w', 'new_v_dn_a_log', 'new_v_dn_dt_bias', 'new_v_dn_o_norm', 'new_v_dn_w_out', 'new_v_final_norm']
TWIN_LEAF_KINDS = {'loss': 'loss', 'grad_x': 'grad_x', 'grad_ffn1_norm': 'grad_w', 'grad_ffn1_w_in': 'grad_w', 'grad_ffn1_w_out': 'grad_w', 'grad_mix_norm': 'grad_w', 'grad_xa_norm': 'grad_w', 'grad_xa_mem_norm': 'grad_w', 'grad_xa_wq': 'grad_w', 'grad_xa_wkv': 'grad_w', 'grad_xa_wo': 'grad_w', 'grad_ffn2_norm': 'grad_w', 'grad_ffn2_w_in': 'grad_w', 'grad_ffn2_w_out': 'grad_w', 'grad_ab_w_in': 'grad_w', 'grad_lru_conv_w': 'grad_w', 'grad_lru_conv_b': 'grad_w', 'grad_lru_w_a': 'grad_w', 'grad_lru_b_a': 'grad_w', 'grad_lru_w_x': 'grad_w', 'grad_lru_b_x': 'grad_w', 'grad_lru_lambda': 'grad_w', 'grad_ab_w_out': 'grad_w', 'grad_dn_w_in': 'grad_w', 'grad_dn_conv_w': 'grad_w', 'grad_dn_a_log': 'grad_w', 'grad_dn_dt_bias': 'grad_w', 'grad_dn_o_norm': 'grad_w', 'grad_dn_w_out': 'grad_w', 'grad_final_norm': 'grad_w', 'delta_ffn1_norm': 'delta_w', 'delta_ffn1_w_in': 'delta_w', 'delta_ffn1_w_out': 'delta_w', 'delta_mix_norm': 'delta_w', 'delta_xa_norm': 'delta_w', 'delta_xa_mem_norm': 'delta_w', 'delta_xa_wq': 'delta_w', 'delta_xa_wkv': 'delta_w', 'delta_xa_wo': 'delta_w', 'delta_ffn2_norm': 'delta_w', 'delta_ffn2_w_in': 'delta_w', 'delta_ffn2_w_out': 'delta_w', 'delta_ab_w_in': 'delta_w', 'delta_lru_conv_w': 'delta_w', 'delta_lru_conv_b': 'delta_w', 'delta_lru_w_a': 'delta_w', 'delta_lru_b_a': 'delta_w', 'delta_lru_w_x': 'delta_w', 'delta_lru_b_x': 'delta_w', 'delta_lru_lambda': 'delta_w', 'delta_ab_w_out': 'delta_w', 'delta_dn_w_in': 'delta_w', 'delta_dn_conv_w': 'delta_w', 'delta_dn_a_log': 'delta_w', 'delta_dn_dt_bias': 'delta_w', 'delta_dn_o_norm': 'delta_w', 'delta_dn_w_out': 'delta_w', 'delta_final_norm': 'delta_w', 'new_m_ffn1_norm': 'new_m', 'new_m_ffn1_w_in': 'new_m', 'new_m_ffn1_w_out': 'new_m', 'new_m_mix_norm': 'new_m', 'new_m_xa_norm': 'new_m', 'new_m_xa_mem_norm': 'new_m', 'new_m_xa_wq': 'new_m', 'new_m_xa_wkv': 'new_m', 'new_m_xa_wo': 'new_m', 'new_m_ffn2_norm': 'new_m', 'new_m_ffn2_w_in': 'new_m', 'new_m_ffn2_w_out': 'new_m', 'new_m_ab_w_in': 'new_m', 'new_m_lru_conv_w': 'new_m', 'new_m_lru_conv_b': 'new_m', 'new_m_lru_w_a': 'new_m', 'new_m_lru_b_a': 'new_m', 'new_m_lru_w_x': 'new_m', 'new_m_lru_b_x': 'new_m', 'new_m_lru_lambda': 'new_m', 'new_m_ab_w_out': 'new_m', 'new_m_dn_w_in': 'new_m', 'new_m_dn_conv_w': 'new_m', 'new_m_dn_a_log': 'new_m', 'new_m_dn_dt_bias': 'new_m', 'new_m_dn_o_norm': 'new_m', 'new_m_dn_w_out': 'new_m', 'new_m_final_norm': 'new_m', 'new_v_ffn1_norm': 'new_v', 'new_v_ffn1_w_in': 'new_v', 'new_v_ffn1_w_out': 'new_v', 'new_v_mix_norm': 'new_v', 'new_v_xa_norm': 'new_v', 'new_v_xa_mem_norm': 'new_v', 'new_v_xa_wq': 'new_v', 'new_v_xa_wkv': 'new_v', 'new_v_xa_wo': 'new_v', 'new_v_ffn2_norm': 'new_v', 'new_v_ffn2_w_in': 'new_v', 'new_v_ffn2_w_out': 'new_v', 'new_v_ab_w_in': 'new_v', 'new_v_lru_conv_w': 'new_v', 'new_v_lru_conv_b': 'new_v', 'new_v_lru_w_a': 'new_v', 'new_v_lru_b_a': 'new_v', 'new_v_lru_w_x': 'new_v', 'new_v_lru_b_x': 'new_v', 'new_v_lru_lambda': 'new_v', 'new_v_ab_w_out': 'new_v', 'new_v_dn_w_in': 'new_v', 'new_v_dn_conv_w': 'new_v', 'new_v_dn_a_log': 'new_v', 'new_v_dn_dt_bias': 'new_v', 'new_v_dn_o_norm': 'new_v', 'new_v_dn_w_out': 'new_v', 'new_v_final_norm': 'new_v'}


def _forward(args):
    return _fwd_reference(*[args[k] for k in FWD_PARAMS])


def _output_shape():
    def fwd():
        inp = _fwd_setup_inputs(0)
        return _fwd_reference(*[inp[k] for k in FWD_PARAMS])
    out = _jax.eval_shape(fwd)
    return out.shape, out.dtype

N_MICROBATCH = 1
ADAM_LR = 0.001
ADAM_B1 = 0.9
ADAM_B2 = 0.999
ADAM_EPS = 1e-08
ADAM_WD = 0.01
ADAM_STEP = 10
PER_EXAMPLE_BATCH_AXIS = {'x': 0, 'mem': 0, 'loss_target': 0}
SHARED_INPUTS = []
_WEIGHT_DTYPES = {'ffn1_norm': _jnp.float32, 'ffn1_w_in': _jnp.float32, 'ffn1_w_out': _jnp.float32, 'mix_norm': _jnp.float32, 'xa_norm': _jnp.float32, 'xa_mem_norm': _jnp.float32, 'xa_wq': _jnp.float32, 'xa_wkv': _jnp.float32, 'xa_wo': _jnp.float32, 'ffn2_norm': _jnp.float32, 'ffn2_w_in': _jnp.float32, 'ffn2_w_out': _jnp.float32, 'ab_w_in': _jnp.float32, 'lru_conv_w': _jnp.float32, 'lru_conv_b': _jnp.float32, 'lru_w_a': _jnp.float32, 'lru_b_a': _jnp.float32, 'lru_w_x': _jnp.float32, 'lru_b_x': _jnp.float32, 'lru_lambda': _jnp.float32, 'ab_w_out': _jnp.float32, 'dn_w_in': _jnp.float32, 'dn_conv_w': _jnp.float32, 'dn_a_log': _jnp.float32, 'dn_dt_bias': _jnp.float32, 'dn_o_norm': _jnp.float32, 'dn_w_out': _jnp.float32, 'final_norm': _jnp.float32}
MOMENT_SCALE = {'ffn1_norm': 1.781375e-01, 'ffn1_w_in': 7.527434e-02, 'ffn1_w_out': 1.227497e-01, 'mix_norm': 2.423385e-01, 'xa_norm': 3.802921e-02, 'xa_mem_norm': 5.526805e-02, 'xa_wq': 3.686826e-02, 'xa_wkv': 3.764760e-02, 'xa_wo': 3.825731e-02, 'ffn2_norm': 1.458146e-01, 'ffn2_w_in': 6.071379e-02, 'ffn2_w_out': 9.909988e-02, 'ab_w_in': 1.187326e-01, 'lru_conv_w': 2.289494e-01, 'lru_conv_b': 2.712228e+00, 'lru_w_a': 1.205521e-01, 'lru_b_a': 6.958435e-02, 'lru_w_x': 2.182214e-01, 'lru_b_x': 8.334248e-02, 'lru_lambda': 8.855793e-02, 'ab_w_out': 1.480622e-01, 'dn_w_in': 1.389709e-01, 'dn_conv_w': 1.338689e-01, 'dn_a_log': 7.336156e-01, 'dn_dt_bias': 6.776122e-01, 'dn_o_norm': 4.745402e-01, 'dn_w_out': 1.637326e-01, 'final_norm': 1.280013e+02}


def _to_microbatches(a, axis):
    t = _jnp.moveaxis(a, axis, 0)
    t = t.reshape((N_MICROBATCH, t.shape[0] // N_MICROBATCH) + t.shape[1:])
    return _jnp.moveaxis(t, 1, axis + 1)


def setup_inputs(seed: int = 0) -> dict:
    inp = _fwd_setup_inputs(seed)
    key = _jax.random.fold_in(_jax.random.key(seed), 7919)
    shape, _ = _output_shape()
    out = dict(inp)
    out["loss_target"] = _jax.random.normal(_jax.random.fold_in(key, 0), shape, _jnp.float32)
    for i, name in enumerate(TWIN_WEIGHTS):
        w = inp[name].astype(_jnp.float32)
        if MOMENT_SCALE is None:
            s = _jnp.sqrt(_jnp.mean(_jnp.square(w)) + 1e-30)
        else:
            s = MOMENT_SCALE[name]
        km, kv = _jax.random.split(_jax.random.fold_in(key, i + 1))
        out[name] = w
        out["m_" + name] = s * _jax.random.normal(km, w.shape, _jnp.float32)
        out["v_" + name] = (s * s) * _jax.random.uniform(kv, w.shape, _jnp.float32, 0.5, 1.5)
    if N_MICROBATCH > 1:
        for name, axis in PER_EXAMPLE_BATCH_AXIS.items():
            out[name] = _to_microbatches(out[name], axis)
    return {'x': out['x'], 'mem': out['mem'], 'ffn1_norm': out['ffn1_norm'], 'ffn1_w_in': out['ffn1_w_in'], 'ffn1_w_out': out['ffn1_w_out'], 'mix_norm': out['mix_norm'], 'xa_norm': out['xa_norm'], 'xa_mem_norm': out['xa_mem_norm'], 'xa_wq': out['xa_wq'], 'xa_wkv': out['xa_wkv'], 'xa_wo': out['xa_wo'], 'ffn2_norm': out['ffn2_norm'], 'ffn2_w_in': out['ffn2_w_in'], 'ffn2_w_out': out['ffn2_w_out'], 'ab_w_in': out['ab_w_in'], 'lru_conv_w': out['lru_conv_w'], 'lru_conv_b': out['lru_conv_b'], 'lru_w_a': out['lru_w_a'], 'lru_b_a': out['lru_b_a'], 'lru_w_x': out['lru_w_x'], 'lru_b_x': out['lru_b_x'], 'lru_lambda': out['lru_lambda'], 'ab_w_out': out['ab_w_out'], 'dn_w_in': out['dn_w_in'], 'dn_conv_w': out['dn_conv_w'], 'dn_a_log': out['dn_a_log'], 'dn_dt_bias': out['dn_dt_bias'], 'dn_o_norm': out['dn_o_norm'], 'dn_w_out': out['dn_w_out'], 'final_norm': out['final_norm'], 'loss_target': out['loss_target'], 'm_ffn1_norm': out['m_ffn1_norm'], 'm_ffn1_w_in': out['m_ffn1_w_in'], 'm_ffn1_w_out': out['m_ffn1_w_out'], 'm_mix_norm': out['m_mix_norm'], 'm_xa_norm': out['m_xa_norm'], 'm_xa_mem_norm': out['m_xa_mem_norm'], 'm_xa_wq': out['m_xa_wq'], 'm_xa_wkv': out['m_xa_wkv'], 'm_xa_wo': out['m_xa_wo'], 'm_ffn2_norm': out['m_ffn2_norm'], 'm_ffn2_w_in': out['m_ffn2_w_in'], 'm_ffn2_w_out': out['m_ffn2_w_out'], 'm_ab_w_in': out['m_ab_w_in'], 'm_lru_conv_w': out['m_lru_conv_w'], 'm_lru_conv_b': out['m_lru_conv_b'], 'm_lru_w_a': out['m_lru_w_a'], 'm_lru_b_a': out['m_lru_b_a'], 'm_lru_w_x': out['m_lru_w_x'], 'm_lru_b_x': out['m_lru_b_x'], 'm_lru_lambda': out['m_lru_lambda'], 'm_ab_w_out': out['m_ab_w_out'], 'm_dn_w_in': out['m_dn_w_in'], 'm_dn_conv_w': out['m_dn_conv_w'], 'm_dn_a_log': out['m_dn_a_log'], 'm_dn_dt_bias': out['m_dn_dt_bias'], 'm_dn_o_norm': out['m_dn_o_norm'], 'm_dn_w_out': out['m_dn_w_out'], 'm_final_norm': out['m_final_norm'], 'v_ffn1_norm': out['v_ffn1_norm'], 'v_ffn1_w_in': out['v_ffn1_w_in'], 'v_ffn1_w_out': out['v_ffn1_w_out'], 'v_mix_norm': out['v_mix_norm'], 'v_xa_norm': out['v_xa_norm'], 'v_xa_mem_norm': out['v_xa_mem_norm'], 'v_xa_wq': out['v_xa_wq'], 'v_xa_wkv': out['v_xa_wkv'], 'v_xa_wo': out['v_xa_wo'], 'v_ffn2_norm': out['v_ffn2_norm'], 'v_ffn2_w_in': out['v_ffn2_w_in'], 'v_ffn2_w_out': out['v_ffn2_w_out'], 'v_ab_w_in': out['v_ab_w_in'], 'v_lru_conv_w': out['v_lru_conv_w'], 'v_lru_conv_b': out['v_lru_conv_b'], 'v_lru_w_a': out['v_lru_w_a'], 'v_lru_b_a': out['v_lru_b_a'], 'v_lru_w_x': out['v_lru_w_x'], 'v_lru_b_x': out['v_lru_b_x'], 'v_lru_lambda': out['v_lru_lambda'], 'v_ab_w_out': out['v_ab_w_out'], 'v_dn_w_in': out['v_dn_w_in'], 'v_dn_conv_w': out['v_dn_conv_w'], 'v_dn_a_log': out['v_dn_a_log'], 'v_dn_dt_bias': out['v_dn_dt_bias'], 'v_dn_o_norm': out['v_dn_o_norm'], 'v_dn_w_out': out['v_dn_w_out'], 'v_final_norm': out['v_final_norm']}


def _loss(weights, diff, rest, loss_target):
    with _jax.named_scope("forward"):
        args = {**rest, TWIN_DIFF_INPUT: diff, **{k: w.astype(_WEIGHT_DTYPES[k]) for k, w in weights.items()}}
        y = _forward(args)
    with _jax.named_scope("loss_head"):
        err = _jnp.square(y.astype(_jnp.float32) - loss_target)
        return 0.5 * _jnp.sum(_jnp.mean(err, axis=-1)) if err.ndim else 0.5 * err


def _adamw(w, g, m, v):
    m = ADAM_B1 * m + (1.0 - ADAM_B1) * g
    v = ADAM_B2 * v + (1.0 - ADAM_B2) * _jnp.square(g)
    m_hat = m / (1.0 - ADAM_B1 ** ADAM_STEP)
    v_hat = v / (1.0 - ADAM_B2 ** ADAM_STEP)
    delta = -ADAM_LR * (m_hat / (_jnp.sqrt(v_hat) + ADAM_EPS) + ADAM_WD * w)
    return delta, m, v


def reference(x, mem, ffn1_norm, ffn1_w_in, ffn1_w_out, mix_norm, xa_norm, xa_mem_norm, xa_wq, xa_wkv, xa_wo, ffn2_norm, ffn2_w_in, ffn2_w_out, ab_w_in, lru_conv_w, lru_conv_b, lru_w_a, lru_b_a, lru_w_x, lru_b_x, lru_lambda, ab_w_out, dn_w_in, dn_conv_w, dn_a_log, dn_dt_bias, dn_o_norm, dn_w_out, final_norm, loss_target, m_ffn1_norm, m_ffn1_w_in, m_ffn1_w_out, m_mix_norm, m_xa_norm, m_xa_mem_norm, m_xa_wq, m_xa_wkv, m_xa_wo, m_ffn2_norm, m_ffn2_w_in, m_ffn2_w_out, m_ab_w_in, m_lru_conv_w, m_lru_conv_b, m_lru_w_a, m_lru_b_a, m_lru_w_x, m_lru_b_x, m_lru_lambda, m_ab_w_out, m_dn_w_in, m_dn_conv_w, m_dn_a_log, m_dn_dt_bias, m_dn_o_norm, m_dn_w_out, m_final_norm, v_ffn1_norm, v_ffn1_w_in, v_ffn1_w_out, v_mix_norm, v_xa_norm, v_xa_mem_norm, v_xa_wq, v_xa_wkv, v_xa_wo, v_ffn2_norm, v_ffn2_w_in, v_ffn2_w_out, v_ab_w_in, v_lru_conv_w, v_lru_conv_b, v_lru_w_a, v_lru_b_a, v_lru_w_x, v_lru_b_x, v_lru_lambda, v_ab_w_out, v_dn_w_in, v_dn_conv_w, v_dn_a_log, v_dn_dt_bias, v_dn_o_norm, v_dn_w_out, v_final_norm):
    given = dict(x=x, mem=mem, ffn1_norm=ffn1_norm, ffn1_w_in=ffn1_w_in, ffn1_w_out=ffn1_w_out, mix_norm=mix_norm, xa_norm=xa_norm, xa_mem_norm=xa_mem_norm, xa_wq=xa_wq, xa_wkv=xa_wkv, xa_wo=xa_wo, ffn2_norm=ffn2_norm, ffn2_w_in=ffn2_w_in, ffn2_w_out=ffn2_w_out, ab_w_in=ab_w_in, lru_conv_w=lru_conv_w, lru_conv_b=lru_conv_b, lru_w_a=lru_w_a, lru_b_a=lru_b_a, lru_w_x=lru_w_x, lru_b_x=lru_b_x, lru_lambda=lru_lambda, ab_w_out=ab_w_out, dn_w_in=dn_w_in, dn_conv_w=dn_conv_w, dn_a_log=dn_a_log, dn_dt_bias=dn_dt_bias, dn_o_norm=dn_o_norm, dn_w_out=dn_w_out, final_norm=final_norm, loss_target=loss_target, m_ffn1_norm=m_ffn1_norm, m_ffn1_w_in=m_ffn1_w_in, m_ffn1_w_out=m_ffn1_w_out, m_mix_norm=m_mix_norm, m_xa_norm=m_xa_norm, m_xa_mem_norm=m_xa_mem_norm, m_xa_wq=m_xa_wq, m_xa_wkv=m_xa_wkv, m_xa_wo=m_xa_wo, m_ffn2_norm=m_ffn2_norm, m_ffn2_w_in=m_ffn2_w_in, m_ffn2_w_out=m_ffn2_w_out, m_ab_w_in=m_ab_w_in, m_lru_conv_w=m_lru_conv_w, m_lru_conv_b=m_lru_conv_b, m_lru_w_a=m_lru_w_a, m_lru_b_a=m_lru_b_a, m_lru_w_x=m_lru_w_x, m_lru_b_x=m_lru_b_x, m_lru_lambda=m_lru_lambda, m_ab_w_out=m_ab_w_out, m_dn_w_in=m_dn_w_in, m_dn_conv_w=m_dn_conv_w, m_dn_a_log=m_dn_a_log, m_dn_dt_bias=m_dn_dt_bias, m_dn_o_norm=m_dn_o_norm, m_dn_w_out=m_dn_w_out, m_final_norm=m_final_norm, v_ffn1_norm=v_ffn1_norm, v_ffn1_w_in=v_ffn1_w_in, v_ffn1_w_out=v_ffn1_w_out, v_mix_norm=v_mix_norm, v_xa_norm=v_xa_norm, v_xa_mem_norm=v_xa_mem_norm, v_xa_wq=v_xa_wq, v_xa_wkv=v_xa_wkv, v_xa_wo=v_xa_wo, v_ffn2_norm=v_ffn2_norm, v_ffn2_w_in=v_ffn2_w_in, v_ffn2_w_out=v_ffn2_w_out, v_ab_w_in=v_ab_w_in, v_lru_conv_w=v_lru_conv_w, v_lru_conv_b=v_lru_conv_b, v_lru_w_a=v_lru_w_a, v_lru_b_a=v_lru_b_a, v_lru_w_x=v_lru_w_x, v_lru_b_x=v_lru_b_x, v_lru_lambda=v_lru_lambda, v_ab_w_out=v_ab_w_out, v_dn_w_in=v_dn_w_in, v_dn_conv_w=v_dn_conv_w, v_dn_a_log=v_dn_a_log, v_dn_dt_bias=v_dn_dt_bias, v_dn_o_norm=v_dn_o_norm, v_dn_w_out=v_dn_w_out, v_final_norm=v_final_norm)
    weights = {n: given[n] for n in TWIN_WEIGHTS}
    shared = {n: given[n] for n in SHARED_INPUTS}
    per_example = {n: given[n] for n in ['x', 'mem']}
    grad_fn = _jax.value_and_grad(_loss, argnums=(0, 1))

    def one_microbatch(ex, loss_target):
        ex = dict(ex)
        diff = ex.pop(TWIN_DIFF_INPUT)
        return grad_fn(weights, diff, {**shared, **ex}, loss_target)

    if N_MICROBATCH == 1:
        loss, (grad_w, grad_x) = one_microbatch(per_example, given["loss_target"])
    else:
        def body(carry, xs):
            loss_sum, grad_sum = carry
            l_k, (gw_k, gx_k) = one_microbatch(xs[0], xs[1])
            with _jax.named_scope("update"):
                return (loss_sum + l_k, _jax.tree.map(_jnp.add, grad_sum, gw_k)), gx_k

        init = (_jnp.zeros((), _jnp.float32), _jax.tree.map(_jnp.zeros_like, weights))
        (loss, grad_w), grad_x = _jax.lax.scan(body, init, (per_example, given["loss_target"]))
    with _jax.named_scope("update"):
        delta_w, new_m, new_v = {}, {}, {}
        for n in TWIN_WEIGHTS:
            delta_w[n], new_m[n], new_v[n] = _adamw(weights[n], grad_w[n], given["m_" + n], given["v_" + n])
    return (loss, grad_x, *[grad_w[n] for n in TWIN_WEIGHTS], *[delta_w[n] for n in TWIN_WEIGHTS],
            *[new_m[n] for n in TWIN_WEIGHTS], *[new_v[n] for n in TWIN_WEIGHTS])
```

```python
import math

import jax
import jax.numpy as jnp
from jax import lax
from jax.experimental import pallas as pl
from jax.experimental.pallas import tpu as pltpu

F32, BF16 = jnp.float32, jnp.bfloat16
MESH = pl.DeviceIdType.MESH
N_DEV = 8
V7X_VMEM_LIMIT = 56 << 20
LANES = 128
PACK_COLS = 1024
PACK_ROWS = 16
HI = lax.Precision.HIGHEST
NEG = -1e30

NORM_EPS = 1e-6
CONV_K = 4
ATTN_HEADS = 8
DILATED_PATTERNS = ((128, 1), (512, 4), (2048, 16))
ATTN_BLOCK = 128
LRU_BLOCKS = 8
LRU_C = 8.0
DN_HEADS = 8
DN_CHUNK = 64
XA_HEADS = 4
ADAM_LR, ADAM_B1, ADAM_B2, ADAM_EPS, ADAM_WD, ADAM_STEP = 0.001, 0.9, 0.999, 1e-08, 0.01, 10

WEIGHTS = ['ffn1_norm', 'ffn1_w_in', 'ffn1_w_out', 'mix_norm', 'xa_norm', 'xa_mem_norm', 'xa_wq', 'xa_wkv', 'xa_wo',
           'ffn2_norm', 'ffn2_w_in', 'ffn2_w_out', 'ab_w_in', 'lru_conv_w', 'lru_conv_b', 'lru_w_a', 'lru_b_a',
           'lru_w_x', 'lru_b_x', 'lru_lambda', 'ab_w_out', 'dn_w_in', 'dn_conv_w', 'dn_a_log', 'dn_dt_bias',
           'dn_o_norm', 'dn_w_out', 'final_norm']
SHARD_AXIS = {'ffn1_w_in': 2, 'ffn1_w_out': 1, 'xa_wq': 1, 'xa_wkv': 2, 'xa_wo': 1, 'ffn2_w_in': 2, 'ffn2_w_out': 1,
              'ab_w_in': 2, 'lru_conv_w': 2, 'ab_w_out': 1, 'dn_w_in': 2, 'dn_conv_w': 2, 'dn_w_out': 1}
GATHER_F32 = ('lru_conv_w', 'dn_conv_w')


def _params(sem=None):
    return pltpu.CompilerParams(dimension_semantics=sem, vmem_limit_bytes=V7X_VMEM_LIMIT)


def _tile(n, pref, mult):
    best = None
    t = mult
    while t <= min(n, pref):
        if n % t == 0:
            best = t
        t += mult
    return n if best is None else best


def _dot(a, b, ca, cb, prec=None):
    return lax.dot_general(a, b, (((ca,), (cb,)), ((), ())), preferred_element_type=F32, precision=prec)


def _mm(a, b, *, name, ta=False, tb=False, out_dtype=F32, res=None, scale=1.0, tm=1024, tn=1024, tk=1024):
    m, kdim = (a.shape[1], a.shape[0]) if ta else a.shape
    n = b.shape[0] if tb else b.shape[1]
    assert (b.shape[1] if tb else b.shape[0]) == kdim
    tm = _tile(m, tm, LANES if ta else 16)
    tn = _tile(n, tn, LANES)
    tk = _tile(kdim, tk, LANES)
    nk = kdim // tk
    a_spec = pl.BlockSpec((tk, tm), lambda i, j, k: (k, i)) if ta else pl.BlockSpec((tm, tk), lambda i, j, k: (i, k))
    b_spec = pl.BlockSpec((tn, tk), lambda i, j, k: (j, k)) if tb else pl.BlockSpec((tk, tn), lambda i, j, k: (k, j))
    o_spec = pl.BlockSpec((tm, tn), lambda i, j, k: (i, j))
    ca, cb = (0 if ta else 1), (1 if tb else 0)
    has_res = res is not None

    def finish(acc, r_ref, o_ref):
        y = acc if scale == 1.0 else acc * scale
        if has_res:
            y = y + r_ref[...]
        o_ref[...] = y.astype(out_dtype)

    def body(*refs):
        a_ref, b_ref = refs[0], refs[1]
        r_ref = refs[2] if has_res else None
        o_ref = refs[2 + has_res]
        p = _dot(a_ref[...].astype(BF16), b_ref[...].astype(BF16), ca, cb)
        if nk == 1:
            finish(p, r_ref, o_ref)
            return
        acc = refs[3 + has_res]
        k = pl.program_id(2)

        @pl.when(k == 0)
        def _():
            acc[...] = p

        @pl.when(k > 0)
        def _():
            acc[...] += p

        @pl.when(k == nk - 1)
        def _():
            finish(acc[...], r_ref, o_ref)

    ins, specs = [a, b], [a_spec, b_spec]
    if has_res:
        ins.append(res)
        specs.append(o_spec)
    return pl.pallas_call(
        body, name=name, grid=(m // tm, n // tn, nk), in_specs=specs, out_specs=o_spec,
        out_shape=jax.ShapeDtypeStruct((m, n), out_dtype),
        scratch_shapes=[] if nk == 1 else [pltpu.VMEM((tm, tn), F32)],
        compiler_params=_params(("parallel", "parallel", "arbitrary")),
    )(*ins)


def _rms_fwd(x, g, *, name, out_dtype):
    r, d = x.shape
    tr = _tile(r, 512, 16)

    def body(x_ref, g_ref, o_ref):
        xv = x_ref[...]
        rs = lax.rsqrt(jnp.mean(xv * xv, axis=-1, keepdims=True) + NORM_EPS)
        o_ref[...] = (xv * rs * g_ref[...]).astype(out_dtype)

    return pl.pallas_call(
        body, name=name, grid=(r // tr,),
        in_specs=[pl.BlockSpec((tr, d), lambda i: (i, 0)), pl.BlockSpec((1, d), lambda i: (0, 0))],
        out_specs=pl.BlockSpec((tr, d), lambda i: (i, 0)),
        out_shape=jax.ShapeDtypeStruct((r, d), out_dtype), compiler_params=_params(("parallel",)),
    )(x, g.reshape(1, d))


def _rms_bwd(x, g, dh, dres, *, name):
    r, d = x.shape
    tr = _tile(r, 512, 8)
    has_res = dres is not None

    def body(*refs):
        x_ref, g_ref, dh_ref = refs[:3]
        r_ref = refs[3] if has_res else None
        dx_ref, dg_ref = refs[3 + has_res], refs[4 + has_res]
        xv = x_ref[...]
        rs = lax.rsqrt(jnp.mean(xv * xv, axis=-1, keepdims=True) + NORM_EPS)
        xh = xv * rs
        dhv = dh_ref[...]
        dgh = dhv * g_ref[...]
        dx = rs * (dgh - xh * jnp.mean(dgh * xh, axis=-1, keepdims=True))
        if has_res:
            dx = dx + r_ref[...]
        dx_ref[...] = dx
        part = jnp.sum(dhv * xh, axis=0, keepdims=True)

        @pl.when(pl.program_id(0) == 0)
        def _():
            dg_ref[...] = part

        @pl.when(pl.program_id(0) > 0)
        def _():
            dg_ref[...] += part

    row = pl.BlockSpec((tr, d), lambda i: (i, 0))
    vec = pl.BlockSpec((1, d), lambda i: (0, 0))
    ins, specs = [x, g.reshape(1, d), dh], [row, vec, row]
    if has_res:
        ins.append(dres)
        specs.append(row)
    dx, dg = pl.pallas_call(
        body, name=name, grid=(r // tr,), in_specs=specs, out_specs=[row, vec],
        out_shape=[jax.ShapeDtypeStruct((r, d), F32), jax.ShapeDtypeStruct((1, d), F32)],
        compiler_params=_params(("arbitrary",)),
    )(*ins)
    return dx, dg.reshape(d)


def _swiglu_fwd(u, *, name):
    t, f2 = u.shape
    f = f2 // 2
    tr = _tile(t, 256, 16)

    def body(u_ref, a_ref):
        gate, up = u_ref[:, :f], u_ref[:, f:]
        a_ref[...] = (gate * jax.nn.sigmoid(gate) * up).astype(BF16)

    return pl.pallas_call(
        body, name=name, grid=(t // tr,), in_specs=[pl.BlockSpec((tr, f2), lambda i: (i, 0))],
        out_specs=pl.BlockSpec((tr, f), lambda i: (i, 0)), out_shape=jax.ShapeDtypeStruct((t, f), BF16),
        compiler_params=_params(("parallel",)),
    )(u)


def _swiglu_bwd(u, da, *, name):
    t, f2 = u.shape
    f = f2 // 2
    tr = _tile(t, 256, 16)

    def body(u_ref, da_ref, du_ref):
        gate, up = u_ref[:, :f], u_ref[:, f:]
        s = jax.nn.sigmoid(gate)
        dav = da_ref[...]
        du_ref[:, :f] = (dav * up * (s * (1.0 + gate * (1.0 - s)))).astype(BF16)
        du_ref[:, f:] = (dav * gate * s).astype(BF16)

    return pl.pallas_call(
        body, name=name, grid=(t // tr,),
        in_specs=[pl.BlockSpec((tr, f2), lambda i: (i, 0)), pl.BlockSpec((tr, f), lambda i: (i, 0))],
        out_specs=pl.BlockSpec((tr, f2), lambda i: (i, 0)), out_shape=jax.ShapeDtypeStruct((t, f2), BF16),
        compiler_params=_params(("parallel",)),
    )(u, da)


def _final_loss(x, g, target, *, name):
    r, d = x.shape
    tr = _tile(r, 512, 8)

    def body(x_ref, g_ref, t_ref, sq_ref, dx_ref, dg_ref):
        xv = x_ref[...]
        gv = g_ref[...]
        rs = lax.rsqrt(jnp.mean(xv * xv, axis=-1, keepdims=True) + NORM_EPS)
        xh = xv * rs
        err = xh * gv - t_ref[...]
        dy = err * (1.0 / d)
        dgh = dy * gv
        dx_ref[...] = rs * (dgh - xh * jnp.mean(dgh * xh, axis=-1, keepdims=True))
        sq = jnp.sum(err * err, axis=0, keepdims=True)
        part = jnp.sum(dy * xh, axis=0, keepdims=True)

        @pl.when(pl.program_id(0) == 0)
        def _():
            sq_ref[...] = sq
            dg_ref[...] = part

        @pl.when(pl.program_id(0) > 0)
        def _():
            sq_ref[...] += sq
            dg_ref[...] += part

    row = pl.BlockSpec((tr, d), lambda i: (i, 0))
    vec = pl.BlockSpec((1, d), lambda i: (0, 0))
    sq, dx, dg = pl.pallas_call(
        body, name=name, grid=(r // tr,), in_specs=[row, vec, row], out_specs=[vec, row, vec],
        out_shape=[jax.ShapeDtypeStruct((1, d), F32), jax.ShapeDtypeStruct((r, d), F32),
                   jax.ShapeDtypeStruct((1, d), F32)],
        compiler_params=_params(("arbitrary",)),
    )(x, g.reshape(1, d), target)
    return sq.reshape(d), dx, dg.reshape(d)


def _scan(a, b, *, name):
    t, w = a.shape
    tb = _tile(t, 1024, 8)

    def body(a_ref, b_ref, h_ref, carry):
        @pl.when(pl.program_id(0) == 0)
        def _():
            carry[...] = jnp.zeros_like(carry)

        row = lax.broadcasted_iota(jnp.int32, (8, w), 0)

        def group(i, c):
            r0 = pl.multiple_of(i * 8, 8)
            av, bv = a_ref[pl.ds(r0, 8), :], b_ref[pl.ds(r0, 8), :]
            for s in (1, 2, 4):
                keep = row >= s
                bv = jnp.where(keep, bv + av * pltpu.roll(bv, s, axis=0), bv)
                av = jnp.where(keep, av * pltpu.roll(av, s, axis=0), av)
            hv = bv + av * c
            h_ref[pl.ds(r0, 8), :] = hv
            return hv[7:8, :]

        carry[0:1, :] = lax.fori_loop(0, tb // 8, group, carry[0:1, :])

    blk = pl.BlockSpec((tb, w), lambda i: (i, 0))
    return pl.pallas_call(
        body, name=name, grid=(t // tb,), in_specs=[blk, blk], out_specs=blk,
        out_shape=jax.ShapeDtypeStruct((t, w), F32), scratch_shapes=[pltpu.VMEM((8, w), F32)],
        compiler_params=_params(("arbitrary",)),
    )(a, b)


def _band_masks(n):
    qi = lax.broadcasted_iota(jnp.int32, (ATTN_BLOCK, ATTN_BLOCK), 0)
    kj = lax.broadcasted_iota(jnp.int32, (ATTN_BLOCK, ATTN_BLOCK), 1)
    return kj <= qi, jnp.logical_and(kj >= qi, n > 0)


def _dattn_fwd(q, k, v, dil, *, name):
    t, w = q.shape
    hd = w // ATTN_HEADS
    seq = t // dil
    assert seq % ATTN_BLOCK == 0
    nb = seq // ATTN_BLOCK
    scale = hd ** -0.5

    def body(q_ref, kp_ref, kc_ref, vp_ref, vc_ref, o_ref, lse_ref):
        mc, mp = _band_masks(pl.program_id(1))
        for h in range(ATTN_HEADS):
            sl = slice(h * hd, (h + 1) * hd)
            qh = (q_ref[:, sl] * scale).astype(BF16)
            sc = jnp.where(mc, _dot(qh, kc_ref[:, sl].astype(BF16), 1, 1), NEG)
            sp = jnp.where(mp, _dot(qh, kp_ref[:, sl].astype(BF16), 1, 1), NEG)
            m = jnp.maximum(jnp.max(sc, axis=-1, keepdims=True), jnp.max(sp, axis=-1, keepdims=True))
            pc, pp = jnp.exp(sc - m), jnp.exp(sp - m)
            den = jnp.sum(pc, axis=-1, keepdims=True) + jnp.sum(pp, axis=-1, keepdims=True)
            o = _dot(pc.astype(BF16), vc_ref[:, sl].astype(BF16), 1, 0) + _dot(pp.astype(BF16), vp_ref[:, sl].astype(BF16), 1, 0)
            o_ref[:, sl] = o / den
            lse_ref[:, sl] = jnp.broadcast_to(m + jnp.log(den), (ATTN_BLOCK, hd))

    cur = pl.BlockSpec((ATTN_BLOCK, w), lambda r, n: (n, r))
    prev = pl.BlockSpec((ATTN_BLOCK, w), lambda r, n: (jnp.maximum(n - 1, 0), r))
    view = lambda z: z.reshape(seq, dil * w)
    o, lse = pl.pallas_call(
        body, name=name, grid=(dil, nb), in_specs=[cur, prev, cur, prev, cur], out_specs=[cur, cur],
        out_shape=[jax.ShapeDtypeStruct((seq, dil * w), F32)] * 2, compiler_params=_params(("parallel", "parallel")),
    )(view(q), view(k), view(k), view(v), view(v))
    return o.reshape(t, w), lse.reshape(t, w)


def _dattn_bwd(q, k, v, do, lse, delta, dil, *, name):
    t, w = q.shape
    hd = w // ATTN_HEADS
    seq = t // dil
    nb = seq // ATTN_BLOCK
    scale = hd ** -0.5

    def body(qc_ref, qn_ref, doc_ref, don_ref, lc_ref, ln_ref, dc_ref, dn_ref, kp_ref, kc_ref, vp_ref, vc_ref,
             dq_ref, dk_ref, dv_ref):
        n = pl.program_id(1)
        mc, mp = _band_masks(n)
        _, mx = _band_masks(jnp.where(n + 1 < nb, 1, 0))
        for h in range(ATTN_HEADS):
            sl = slice(h * hd, (h + 1) * hd)
            one = slice(h * hd, h * hd + 1)
            qc, qn = (qc_ref[:, sl] * scale).astype(BF16), (qn_ref[:, sl] * scale).astype(BF16)
            kc, kp = kc_ref[:, sl].astype(BF16), kp_ref[:, sl].astype(BF16)
            vc, vp = vc_ref[:, sl].astype(BF16), vp_ref[:, sl].astype(BF16)
            doc, don = doc_ref[:, sl].astype(BF16), don_ref[:, sl].astype(BF16)
            p_c = jnp.exp(jnp.where(mc, _dot(qc, kc, 1, 1), NEG) - lc_ref[:, one])
            p_p = jnp.exp(jnp.where(mp, _dot(qc, kp, 1, 1), NEG) - lc_ref[:, one])
            p_x = jnp.exp(jnp.where(mx, _dot(qn, kc, 1, 1), NEG) - ln_ref[:, one])
            ds_c = (p_c * (_dot(doc, vc, 1, 1) - dc_ref[:, one])).astype(BF16)
            ds_p = (p_p * (_dot(doc, vp, 1, 1) - dc_ref[:, one])).astype(BF16)
            ds_x = (p_x * (_dot(don, vc, 1, 1) - dn_ref[:, one])).astype(BF16)
            dq_ref[:, sl] = (_dot(ds_c, kc, 1, 0) + _dot(ds_p, kp, 1, 0)) * scale
            dk_ref[:, sl] = _dot(ds_c, qc, 0, 0) + _dot(ds_x, qn, 0, 0)
            dv_ref[:, sl] = _dot(p_c.astype(BF16), doc, 0, 0) + _dot(p_x.astype(BF16), don, 0, 0)

    cur = pl.BlockSpec((ATTN_BLOCK, w), lambda r, n: (n, r))
    prev = pl.BlockSpec((ATTN_BLOCK, w), lambda r, n: (jnp.maximum(n - 1, 0), r))
    nxt = pl.BlockSpec((ATTN_BLOCK, w), lambda r, n: (jnp.minimum(n + 1, nb - 1), r))
    view = lambda z: z.reshape(seq, dil * w)
    q, k, v, do, lse, delta = (view(z) for z in (q, k, v, do, lse, delta))
    dq, dk, dv = pl.pallas_call(
        body, name=name, grid=(dil, nb),
        in_specs=[cur, nxt, cur, nxt, cur, nxt, cur, nxt, prev, cur, prev, cur], out_specs=[cur, cur, cur],
        out_shape=[jax.ShapeDtypeStruct((seq, dil * w), F32)] * 3, compiler_params=_params(("parallel", "parallel")),
    )(q, q, do, do, lse, lse, delta, delta, k, k, v, v)
    return dq.reshape(t, w), dk.reshape(t, w), dv.reshape(t, w)


def _xattn_fwd(q, k, v, *, name):
    t, w = q.shape
    nm = k.shape[0]
    hd = w // XA_HEADS
    scale = hd ** -0.5
    tq = _tile(t, 512, 8)

    def body(q_ref, k_ref, v_ref, o_ref):
        for h in range(XA_HEADS):
            sl = slice(h * hd, (h + 1) * hd)
            s = _dot((q_ref[:, sl] * scale).astype(BF16), k_ref[:, sl].astype(BF16), 1, 1)
            p = jnp.exp(s - jnp.max(s, axis=-1, keepdims=True))
            p = p / jnp.sum(p, axis=-1, keepdims=True)
            o_ref[:, sl] = _dot(p.astype(BF16), v_ref[:, sl].astype(BF16), 1, 0)

    qs = pl.BlockSpec((tq, w), lambda i: (i, 0))
    ks = pl.BlockSpec((nm, w), lambda i: (0, 0))
    return pl.pallas_call(
        body, name=name, grid=(t // tq,), in_specs=[qs, ks, ks], out_specs=qs,
        out_shape=jax.ShapeDtypeStruct((t, w), F32), compiler_params=_params(("parallel",)),
    )(q, k, v)


def _xattn_bwd(q, k, v, do, *, name):
    t, w = q.shape
    nm = k.shape[0]
    hd = w // XA_HEADS
    scale = hd ** -0.5
    tq = _tile(t, 512, 8)

    def body(q_ref, k_ref, v_ref, do_ref, dq_ref, dk_ref, dv_ref):
        first = pl.program_id(0) == 0
        for h in range(XA_HEADS):
            sl = slice(h * hd, (h + 1) * hd)
            qh = (q_ref[:, sl] * scale).astype(BF16)
            kh, vh, doh = k_ref[:, sl].astype(BF16), v_ref[:, sl].astype(BF16), do_ref[:, sl].astype(BF16)
            s = _dot(qh, kh, 1, 1)
            p = jnp.exp(s - jnp.max(s, axis=-1, keepdims=True))
            p = p / jnp.sum(p, axis=-1, keepdims=True)
            dp = _dot(doh, vh, 1, 1)
            ds = (p * (dp - jnp.sum(p * dp, axis=-1, keepdims=True))).astype(BF16)
            dq_ref[:, sl] = _dot(ds, kh, 1, 0) * scale
            dk = _dot(ds, qh, 0, 0)
            dv = _dot(p.astype(BF16), doh, 0, 0)

            @pl.when(first)
            def _():
                dk_ref[:, sl] = dk
                dv_ref[:, sl] = dv

            @pl.when(jnp.logical_not(first))
            def _():
                dk_ref[:, sl] += dk
                dv_ref[:, sl] += dv

    qs = pl.BlockSpec((tq, w), lambda i: (i, 0))
    ks = pl.BlockSpec((nm, w), lambda i: (0, 0))
    return pl.pallas_call(
        body, name=name, grid=(t // tq,), in_specs=[qs, ks, ks, qs], out_specs=[qs, ks, ks],
        out_shape=[jax.ShapeDtypeStruct((t, w), F32), jax.ShapeDtypeStruct((nm, w), F32),
                   jax.ShapeDtypeStruct((nm, w), F32)],
        compiler_params=_params(("arbitrary",)),
    )(q, k, v, do)


def _dn_chunk_terms(qh, kh, gcc, gcr, bh):
    c = DN_CHUNK
    row = lax.broadcasted_iota(jnp.int32, (c, c), 0)
    col = lax.broadcasted_iota(jnp.int32, (c, c), 1)
    decay = jnp.exp(jnp.where(row >= col, gcc - gcr, NEG))
    kb = kh * bh
    kkt = _dot(kb, kh, 1, 1, HI)
    qkt = _dot(qh, kh, 1, 1, HI)
    return row, col, decay, kb, kkt, qkt


def _unit_lower_inverse(nm, row, col):
    eye = (row == col).astype(F32)
    inv = eye - nm
    p = nm
    for _ in range(int(math.log2(DN_CHUNK)) - 1):
        p = _dot(p, p, 1, 0, HI)
        inv = inv + _dot(inv, p, 1, 0, HI)
    return inv


def _dn_fwd(q, k, v, gc, gcr, beta, *, name):
    t, w = q.shape
    hd = w // DN_HEADS
    c = DN_CHUNK
    nch = t // c

    def body(q_ref, k_ref, v_ref, gc_ref, gcr_ref, b_ref, o_ref, s_ref, inv_ref, state):
        @pl.when(pl.program_id(0) == 0)
        def _():
            state[...] = jnp.zeros_like(state)

        for h in range(DN_HEADS):
            sl = slice(h * hd, (h + 1) * hd)
            qh, kh, vh = q_ref[:, sl], k_ref[:, sl], v_ref[:, sl]
            gcc, gcr_h, bh = gc_ref[:, h:h + 1], gcr_ref[0, h:h + 1, :], b_ref[:, h:h + 1]
            row, col, decay, kb, kkt, qkt = _dn_chunk_terms(qh, kh, gcc, gcr_h, bh)
            inv = _unit_lower_inverse(jnp.where(row > col, kkt * decay, 0.0), row, col)
            e = jnp.exp(gcc)
            u = _dot(inv, vh * bh, 1, 0, HI)
            wm = _dot(inv, kb * e, 1, 0, HI)
            sh = state[h]
            v_new = u - _dot(wm, sh, 1, 0, HI)
            o_ref[:, sl] = _dot(qh * e, sh, 1, 0, HI) + _dot(qkt * decay, v_new, 1, 0, HI)
            s_ref[0, h] = sh
            inv_ref[0, h] = inv
            gl = gcc[c - 1:c, :]
            state[h] = sh * jnp.exp(gl) + _dot(kh * jnp.exp(gl - gcc), v_new, 0, 0, HI)

    rows = pl.BlockSpec((c, w), lambda n: (n, 0))
    cols = pl.BlockSpec((c, DN_HEADS), lambda n: (n, 0))
    rowg = pl.BlockSpec((1, DN_HEADS, c), lambda n: (n, 0, 0))
    return pl.pallas_call(
        body, name=name, grid=(nch,), in_specs=[rows, rows, rows, cols, rowg, cols],
        out_specs=[rows, pl.BlockSpec((1, DN_HEADS, hd, hd), lambda n: (n, 0, 0, 0)),
                   pl.BlockSpec((1, DN_HEADS, c, c), lambda n: (n, 0, 0, 0))],
        out_shape=[jax.ShapeDtypeStruct((t, w), F32), jax.ShapeDtypeStruct((nch, DN_HEADS, hd, hd), F32),
                   jax.ShapeDtypeStruct((nch, DN_HEADS, c, c), F32)],
        scratch_shapes=[pltpu.VMEM((DN_HEADS, hd, hd), F32)], compiler_params=_params(("arbitrary",)),
    )(q, k, v, gc, gcr, beta)


def _dn_bwd(q, k, v, gc, gcr, beta, states, invs, do, *, name):
    t, w = q.shape
    hd = w // DN_HEADS
    c = DN_CHUNK
    nch = t // c

    def body(q_ref, k_ref, v_ref, gc_ref, gcr_ref, b_ref, s_ref, inv_ref, do_ref,
             dq_ref, dk_ref, dv_ref, dgc_ref, dgr_ref, db_ref, dstate):
        @pl.when(pl.program_id(0) == 0)
        def _():
            dstate[...] = jnp.zeros_like(dstate)

        for h in range(DN_HEADS):
            sl = slice(h * hd, (h + 1) * hd)
            qh, kh, vh, doh = q_ref[:, sl], k_ref[:, sl], v_ref[:, sl], do_ref[:, sl]
            gcc, gcr_h, bh = gc_ref[:, h:h + 1], gcr_ref[0, h:h + 1, :], b_ref[:, h:h + 1]
            row, col, decay, kb, kkt, qkt = _dn_chunk_terms(qh, kh, gcc, gcr_h, bh)
            inv, sh, dsn = inv_ref[0, h], s_ref[0, h], dstate[h]
            e = jnp.exp(gcc)
            gl = gcc[c - 1:c, :]
            el, r = jnp.exp(gl), jnp.exp(gl - gcc)
            u = _dot(inv, vh * bh, 1, 0, HI)
            wm = _dot(inv, kb * e, 1, 0, HI)
            v_new = u - _dot(wm, sh, 1, 0, HI)
            qk = qkt * decay
            kr = kh * r
            qe = qh * e
            d_qe = _dot(doh, sh, 1, 1, HI)
            d_qk = _dot(doh, v_new, 1, 1, HI)
            d_vnew = _dot(qk, doh, 0, 0, HI) + _dot(kr, dsn, 1, 0, HI)
            d_el = jnp.sum(jnp.sum(dsn * sh, axis=1, keepdims=True), axis=0, keepdims=True)
            d_kr = _dot(v_new, dsn, 1, 1, HI)
            d_w = -_dot(d_vnew, sh, 1, 1, HI)
            dstate[h] = dsn * el + _dot(qe, doh, 0, 0, HI) - _dot(wm, d_vnew, 0, 0, HI)
            d_ru = _dot(inv, d_vnew, 0, 0, HI)
            d_rw = _dot(inv, d_w, 0, 0, HI)
            d_a = -(_dot(d_ru, u, 1, 1, HI) + _dot(d_rw, wm, 1, 1, HI))
            dv_ref[:, sl] = d_ru * bh
            d_kb = d_rw * e
            d_e = jnp.sum(d_rw * kb, axis=1, keepdims=True) + jnp.sum(d_qe * qh, axis=1, keepdims=True)
            d_n = jnp.where(row > col, d_a, 0.0)
            d_m = d_n * decay
            d_p = d_qk * decay
            d_kb = d_kb + _dot(d_m, kh, 1, 0, HI)
            dk = _dot(d_m, kb, 0, 0, HI) + _dot(d_p, qh, 0, 0, HI)
            dq_ref[:, sl] = _dot(d_p, kh, 1, 0, HI) + d_qe * e
            dd = (d_n * kkt + d_qk * qkt) * decay
            d_r = jnp.sum(d_kr * kh, axis=1, keepdims=True)
            d_gl = d_el * el + jnp.sum(d_r * r, axis=0, keepdims=True)
            last = lax.broadcasted_iota(jnp.int32, (c, 1), 0) == c - 1
            dgc_ref[:, h:h + 1] = (jnp.sum(dd, axis=1, keepdims=True) + d_e * e - d_r * r + jnp.where(last, d_gl, 0.0))
            dgr_ref[0, h:h + 1, :] = -jnp.sum(dd, axis=0, keepdims=True)
            dk_ref[:, sl] = dk + d_kr * r + d_kb * bh
            db_ref[:, h:h + 1] = jnp.sum(d_ru * vh, axis=1, keepdims=True) + jnp.sum(d_kb * kh, axis=1, keepdims=True)

    rev = lambda n: nch - 1 - n
    rows = pl.BlockSpec((c, w), lambda n: (rev(n), 0))
    cols = pl.BlockSpec((c, DN_HEADS), lambda n: (rev(n), 0))
    rowg = pl.BlockSpec((1, DN_HEADS, c), lambda n: (rev(n), 0, 0))
    st = pl.BlockSpec((1, DN_HEADS, hd, hd), lambda n: (rev(n), 0, 0, 0))
    iv = pl.BlockSpec((1, DN_HEADS, c, c), lambda n: (rev(n), 0, 0, 0))
    return pl.pallas_call(
        body, name=name, grid=(nch,), in_specs=[rows, rows, rows, cols, rowg, cols, st, iv, rows],
        out_specs=[rows, rows, rows, cols, rowg, cols],
        out_shape=[jax.ShapeDtypeStruct((t, w), F32)] * 3
        + [jax.ShapeDtypeStruct((t, DN_HEADS), F32), jax.ShapeDtypeStruct((nch, DN_HEADS, c), F32),
           jax.ShapeDtypeStruct((t, DN_HEADS), F32)],
        scratch_shapes=[pltpu.VMEM((DN_HEADS, hd, hd), F32)], compiler_params=_params(("arbitrary",)),
    )(q, k, v, gc, gcr, beta, states, invs, do)


def _my_id():
    return 4 * lax.axis_index("x") + 2 * lax.axis_index("y") + lax.axis_index("c")


def _peer(k):
    x, y, c = lax.axis_index("x"), lax.axis_index("y"), lax.axis_index("c")
    flip = lambda v, bit: 1 - v if bit else v
    return flip(x, k & 4), flip(y, k & 2), flip(c, k & 1)


def _peer_id(k):
    x, y, c = _peer(k)
    return 4 * x + 2 * y + c


def _exchange(src, *, name, scatter):
    rows = src.shape[-2:]

    def body(src_ref, out_ref, send_sems, recv_sems, local_sem):
        me = _my_id()
        mine = src_ref.at[me] if scatter else src_ref
        local = pltpu.make_async_copy(mine, out_ref.at[me], local_sem)
        local.start()
        sends = []
        for k in range(1, N_DEV):
            cp = pltpu.make_async_remote_copy(
                src_ref=src_ref.at[_peer_id(k)] if scatter else src_ref, dst_ref=out_ref.at[me],
                send_sem=send_sems.at[k - 1], recv_sem=recv_sems.at[k - 1], device_id=_peer(k), device_id_type=MESH)
            cp.start()
            sends.append(cp)
        for k in range(1, N_DEV):
            pltpu.make_async_remote_copy(
                src_ref=mine, dst_ref=out_ref.at[_peer_id(k)], send_sem=send_sems.at[k - 1],
                recv_sem=recv_sems.at[k - 1], device_id=_peer(k), device_id_type=MESH).wait_recv()
        for cp in sends:
            cp.wait_send()
        local.wait()

    return pl.pallas_call(
        body, name=name, out_shape=jax.ShapeDtypeStruct((N_DEV,) + rows, src.dtype),
        in_specs=[pl.BlockSpec(memory_space=pl.ANY)], out_specs=pl.BlockSpec(memory_space=pl.ANY),
        scratch_shapes=[pltpu.SemaphoreType.DMA((N_DEV - 1,)), pltpu.SemaphoreType.DMA((N_DEV - 1,)),
                        pltpu.SemaphoreType.DMA(())],
    )(src)


def _sum_adamw(parts, w, m, v, *, name):
    _, r, c = parts.shape
    tr = _tile(r, 256, 8)
    c1, c2 = 1.0 - ADAM_B1 ** ADAM_STEP, 1.0 - ADAM_B2 ** ADAM_STEP

    def body(p_ref, w_ref, m_ref, v_ref, g_ref, d_ref, nm_ref, nv_ref):
        g = p_ref[0]
        for s in range(1, N_DEV):
            g = g + p_ref[s]
        nm = ADAM_B1 * m_ref[...] + (1.0 - ADAM_B1) * g
        nv = ADAM_B2 * v_ref[...] + (1.0 - ADAM_B2) * (g * g)
        g_ref[...] = g
        nm_ref[...] = nm
        nv_ref[...] = nv
        d_ref[...] = -ADAM_LR * ((nm / c1) / (jnp.sqrt(nv / c2) + ADAM_EPS) + ADAM_WD * w_ref[...])

    blk = pl.BlockSpec((tr, c), lambda i: (i, 0))
    return pl.pallas_call(
        body, name=name, grid=(r // tr,), in_specs=[pl.BlockSpec((N_DEV, tr, c), lambda i: (0, i, 0)), blk, blk, blk],
        out_specs=[blk] * 4, out_shape=[jax.ShapeDtypeStruct((r, c), F32)] * 4, compiler_params=_params(("parallel",)),
    )(parts, w, m, v)


def _pack_rows(n):
    return -(-n // (PACK_COLS * PACK_ROWS)) * PACK_ROWS


def _pack(blocks):
    parts, spans, at = [], [], 0
    for blk, n_lead in blocks:
        lead = blk.shape[:n_lead]
        n = math.prod(blk.shape[n_lead:])
        rows = _pack_rows(n)
        flat = blk.reshape(lead + (n,))
        flat = jnp.pad(flat, [(0, 0)] * n_lead + [(0, rows * PACK_COLS - n)])
        parts.append(flat.reshape(lead + (rows, PACK_COLS)))
        spans.append((at, rows))
        at += rows
    return jnp.concatenate(parts, axis=-2), spans


def _unpack(buf, span, shape):
    at, rows = span
    lead = buf.shape[:-2]
    flat = lax.slice_in_dim(buf, at, at + rows, axis=buf.ndim - 2).reshape(lead + (rows * PACK_COLS,))
    return lax.slice_in_dim(flat, 0, math.prod(shape), axis=len(lead)).reshape(lead + tuple(shape))


def _join_shards(g, axis):
    g = jnp.moveaxis(g, 0, axis)
    return g.reshape(g.shape[:axis] + (g.shape[axis] * g.shape[axis + 1],) + g.shape[axis + 2:])


def _split_shards(full, axis):
    s = full.shape
    g = full.reshape(s[:axis] + (N_DEV, s[axis] // N_DEV) + s[axis + 1:])
    return jnp.moveaxis(g, axis, 0)


def _ffn_fwd(x, g, w_in, w_out, tag):
    h = _rms_fwd(x, g, name=tag + "_norm", out_dtype=BF16)
    u = _mm(h, w_in, name=tag + "_in", tn=1408)
    a = _swiglu_fwd(u, name=tag + "_act")
    return _mm(a, w_out, name=tag + "_out", res=x, scale=0.5, tk=2816), (x, h, u, a)


def _ffn_bwd(saved, g, w_in, w_out, dxo, tag):
    x, h, u, a = saved
    d_w_out = _mm(a, dxo, name=tag + "_dwout", ta=True, scale=0.5, tm=1408)
    da = _mm(dxo, w_out, name=tag + "_da", tb=True, scale=0.5, tn=1408)
    du = _swiglu_bwd(u, da, name=tag + "_dact")
    d_w_in = _mm(h, du, name=tag + "_dwin", ta=True, tn=1408)
    dh = _mm(du, w_in, name=tag + "_dh", tb=True, tk=1408)
    dx, dg = _rms_bwd(x, g, dh, dxo, name=tag + "_dnorm")
    return dx, dg, d_w_in, d_w_out


def _conv(x, w):
    t = x.shape[0]
    xp = jnp.pad(x, ((CONV_K - 1, 0), (0, 0)))
    return sum(w[k][None, :] * lax.slice_in_dim(xp, k, k + t, axis=0) for k in range(CONV_K))


def _lru_gates(xr, conv_w, conv_b, w_a, b_a, w_x, b_x, lam):
    t, width = xr.shape
    xc = _conv(xr, conv_w) + conv_b
    xb = xc.reshape(t, LRU_BLOCKS, width // LRU_BLOCKS)
    r = jax.nn.sigmoid(jnp.einsum('snj,njk->snk', xb, w_a).reshape(t, width) + b_a)
    i = jax.nn.sigmoid(jnp.einsum('snj,njk->snk', xb, w_x).reshape(t, width) + b_x)
    log_a = -LRU_C * r * jax.nn.softplus(-lam)
    return jnp.exp(log_a), jnp.sqrt(-jnp.expm1(2.0 * log_a)) * i * xc


def _head_sum(z, heads):
    t, w = z.shape
    s = jnp.sum(z.reshape(t, heads, w // heads), axis=-1, keepdims=True)
    return jnp.broadcast_to(s, (t, heads, w // heads)).reshape(t, w)


def _attn_lru_fwd(x, p, tag):
    aw = ATTN_HEADS * 64
    h = _rms_fwd(x, p['mix_norm'], name=tag + "_norm", out_dtype=BF16)
    proj = _mm(h, p['ab_w_in'], name=tag + "_in", tn=1280)
    q, k, v, xr, gr = (proj[:, i * aw:(i + 1) * aw] for i in range(5))
    outs, lses = [], []
    for window, dil in DILATED_PATTERNS:
        assert window // dil == ATTN_BLOCK
        o, l = _dattn_fwd(q, k, v, dil, name=f"{tag}_attn{dil}")
        outs.append(o)
        lses.append(l)
    lse_all = jax.nn.logsumexp(jnp.stack(lses, 0), axis=0)
    attn = sum(jnp.exp(l - lse_all) * o for l, o in zip(lses, outs))
    gate_args = (xr, p['lru_conv_w'], p['lru_conv_b'], p['lru_w_a'], p['lru_b_a'], p['lru_w_x'], p['lru_b_x'], p['lru_lambda'])
    (a, b), gates_vjp = jax.vjp(_lru_gates, *gate_args)
    hs = _scan(a, b, name=tag + "_scan")
    y, out_vjp = jax.vjp(lambda hs_, gr_: hs_ * jax.nn.gelu(gr_), hs, gr)
    cat = jnp.concatenate([attn, y], axis=-1).astype(BF16)
    xo = _mm(cat, p['ab_w_out'], name=tag + "_out", res=x)
    return xo, (x, h, q, k, v, attn, lse_all, a, hs, gates_vjp, out_vjp, cat)


def _attn_lru_bwd(saved, p, dxo, tag):
    x, h, q, k, v, attn, lse_all, a, hs, gates_vjp, out_vjp, cat = saved
    aw = attn.shape[1]
    g = {'ab_w_out': _mm(cat, dxo, name=tag + "_dwout", ta=True)}
    dcat = _mm(dxo, p['ab_w_out'], name=tag + "_dcat", tb=True)
    dattn, dy = dcat[:, :aw], dcat[:, aw:]
    dhs, dgr = out_vjp(dy)
    a_next = jnp.concatenate([a[1:], jnp.zeros_like(a[:1])], axis=0)
    dtot = jnp.flip(_scan(jnp.flip(a_next, 0), jnp.flip(dhs, 0), name=tag + "_dscan"), 0)
    h_prev = jnp.concatenate([jnp.zeros_like(hs[:1]), hs[:-1]], axis=0)
    dxr, g['lru_conv_w'], g['lru_conv_b'], g['lru_w_a'], g['lru_b_a'], g['lru_w_x'], g['lru_b_x'], g['lru_lambda'] = \
        gates_vjp((dtot * h_prev, dtot))
    delta = _head_sum(dattn * attn, ATTN_HEADS)
    dq = dk = dv = 0.0
    for _, dil in DILATED_PATTERNS:
        dq_, dk_, dv_ = _dattn_bwd(q, k, v, dattn, lse_all, delta, dil, name=f"{tag}_dattn{dil}")
        dq, dk, dv = dq + dq_, dk + dk_, dv + dv_
    dproj = jnp.concatenate([dq, dk, dv, dxr, dgr], axis=-1).astype(BF16)
    g['ab_w_in'] = _mm(h, dproj, name=tag + "_dwin", ta=True, tn=1280)
    dh = _mm(dproj, p['ab_w_in'], name=tag + "_dh", tb=True, tk=1280)
    dx, g['mix_norm'] = _rms_bwd(x, p['mix_norm'], dh, dxo, name=tag + "_dnorm")
    return dx, g


def _dn_prepare(qkv, a, b, conv_w, a_log, dt_bias):
    t = qkv.shape[0]
    w = qkv.shape[1] // 3
    hd = w // DN_HEADS
    qkv = jax.nn.silu(_conv(qkv, conv_w))
    q, k, v = qkv[:, :w], qkv[:, w:2 * w], qkv[:, 2 * w:]

    def l2(z):
        z = z.reshape(t, DN_HEADS, hd)
        return (z * lax.rsqrt(jnp.sum(z * z, axis=-1, keepdims=True) + 1e-6)).reshape(t, w)

    beta = jax.nn.sigmoid(b)
    g = -jnp.exp(a_log) * jax.nn.softplus(a + dt_bias)
    gc = jnp.cumsum(g.reshape(t // DN_CHUNK, DN_CHUNK, DN_HEADS), axis=1)
    return l2(q) * hd ** -0.5, l2(k), v, gc.reshape(t, DN_HEADS), jnp.swapaxes(gc, 1, 2), beta


def _dn_gate(o, z, o_norm):
    t, w = o.shape
    hd = w // DN_HEADS
    o = o.reshape(t, DN_HEADS, hd)
    o = o * lax.rsqrt(jnp.mean(o * o, axis=-1, keepdims=True) + NORM_EPS) * o_norm
    return (o * jax.nn.silu(z.reshape(t, DN_HEADS, hd))).reshape(t, w)


def _dn_in_width(w):
    return -(-(4 * w + 2 * DN_HEADS) // LANES) * LANES


def _deltanet_fwd(x, p, tag):
    w = p['dn_w_out'].shape[0]
    h = _rms_fwd(x, p['mix_norm'], name=tag + "_norm", out_dtype=BF16)
    proj = _mm(h, p['dn_w_in'], name=tag + "_in", tn=1408)
    qkv, z = proj[:, :3 * w], proj[:, 3 * w:4 * w]
    a, b = proj[:, 4 * w:4 * w + DN_HEADS], proj[:, 4 * w + DN_HEADS:4 * w + 2 * DN_HEADS]
    prep, prep_vjp = jax.vjp(_dn_prepare, qkv, a, b, p['dn_conv_w'], p['dn_a_log'], p['dn_dt_bias'])
    o, states, invs = _dn_fwd(*prep, name=tag + "_rule")
    og, gate_vjp = jax.vjp(_dn_gate, o, z, p['dn_o_norm'])
    og = og.astype(BF16)
    xo = _mm(og, p['dn_w_out'], name=tag + "_out", res=x)
    return xo, (x, h, prep, states, invs, prep_vjp, gate_vjp, og)


def _deltanet_bwd(saved, p, dxo, tag):
    x, h, prep, states, invs, prep_vjp, gate_vjp, og = saved
    g = {'dn_w_out': _mm(og, dxo, name=tag + "_dwout", ta=True)}
    dog = _mm(dxo, p['dn_w_out'], name=tag + "_dog", tb=True)
    do, dz, g['dn_o_norm'] = gate_vjp(dog)
    dq, dk, dv, dgc, dgr, dbeta = _dn_bwd(*prep, states, invs, do, name=tag + "_drule")
    dqkv, da, db, g['dn_conv_w'], g['dn_a_log'], g['dn_dt_bias'] = prep_vjp((dq, dk, dv, dgc, dgr, dbeta))
    t = x.shape[0]
    pad = jnp.zeros((t, p['dn_w_in'].shape[1] - dqkv.shape[1] - dz.shape[1] - 2 * DN_HEADS), F32)
    dproj = jnp.concatenate([dqkv, dz, da, db, pad], axis=-1).astype(BF16)
    g['dn_w_in'] = _mm(h, dproj, name=tag + "_dwin", ta=True, tn=1408)
    dh = _mm(dproj, p['dn_w_in'], name=tag + "_dh", tb=True, tk=1408)
    dx, g['mix_norm'] = _rms_bwd(x, p['mix_norm'], dh, dxo, name=tag + "_dnorm")
    return dx, g


def _xattn_block_fwd(x, mem, p, tag):
    w = x.shape[1]
    h = _rms_fwd(x, p['xa_norm'], name=tag + "_norm", out_dtype=BF16)
    mh = _rms_fwd(mem, p['xa_mem_norm'], name=tag + "_mnorm", out_dtype=BF16)
    q = _mm(h, p['xa_wq'], name=tag + "_q")
    kv = _mm(mh, p['xa_wkv'], name=tag + "_kv")
    k, v = kv[:, :w], kv[:, w:]
    o = _xattn_fwd(q, k, v, name=tag + "_attn").astype(BF16)
    xo = _mm(o, p['xa_wo'], name=tag + "_out", res=x)
    return xo, (x, h, mh, q, k, v, o)


def _xattn_block_bwd(saved, mem, p, dxo, tag):
    x, h, mh, q, k, v, o = saved
    g = {'xa_wo': _mm(o, dxo, name=tag + "_dwo", ta=True)}
    do = _mm(dxo, p['xa_wo'], name=tag + "_do", tb=True)
    dq, dk, dv = _xattn_bwd(q, k, v, do, name=tag + "_dattn")
    dq = dq.astype(BF16)
    dkv = jnp.concatenate([dk, dv], axis=-1).astype(BF16)
    g['xa_wq'] = _mm(h, dq, name=tag + "_dwq", ta=True)
    g['xa_wkv'] = _mm(mh, dkv, name=tag + "_dwkv", ta=True)
    dmh = _mm(dkv, p['xa_wkv'], name=tag + "_dmh", tb=True)
    _, g['xa_mem_norm'] = _rms_bwd(mem, p['xa_mem_norm'], dmh, None, name=tag + "_dmnorm")
    dh = _mm(dq, p['xa_wq'], name=tag + "_dh", tb=True)
    dx, g['xa_norm'] = _rms_bwd(x, p['xa_norm'], dh, dxo, name=tag + "_dnorm")
    return dx, g


def _layer_params(full, layer):
    p = {n: full[n][layer] for n in ('ffn1_norm', 'ffn1_w_in', 'ffn1_w_out', 'mix_norm', 'xa_norm', 'xa_mem_norm',
                                     'xa_wq', 'xa_wkv', 'xa_wo', 'ffn2_norm', 'ffn2_w_in', 'ffn2_w_out')}
    mixer = ('ab_w_in', 'lru_conv_w', 'lru_conv_b', 'lru_w_a', 'lru_b_a', 'lru_w_x', 'lru_b_x', 'lru_lambda', 'ab_w_out') \
        if layer % 2 == 0 else ('dn_w_in', 'dn_conv_w', 'dn_a_log', 'dn_dt_bias', 'dn_o_norm', 'dn_w_out')
    p.update({n: full[n][layer // 2] for n in mixer})
    return p


def _step(x, mem, target, full, depth):
    saved = []
    for layer in range(depth):
        p = _layer_params(full, layer)
        tag = f"l{layer}"
        x, s1 = _ffn_fwd(x, p['ffn1_norm'], p['ffn1_w_in'], p['ffn1_w_out'], tag + "_ffn1")
        x, s2 = (_attn_lru_fwd if layer % 2 == 0 else _deltanet_fwd)(x, p, tag + "_mix")
        x, s3 = _xattn_block_fwd(x, mem, p, tag + "_xa")
        x, s4 = _ffn_fwd(x, p['ffn2_norm'], p['ffn2_w_in'], p['ffn2_w_out'], tag + "_ffn2")
        saved.append((p, s1, s2, s3, s4))
    sq, dx, d_final = _final_loss(x, full['final_norm'], target, name="final_loss")
    per_layer = []
    for layer in reversed(range(depth)):
        p, s1, s2, s3, s4 = saved[layer]
        tag = f"l{layer}"
        g = {}
        dx, g['ffn2_norm'], g['ffn2_w_in'], g['ffn2_w_out'] = _ffn_bwd(s4, p['ffn2_norm'], p['ffn2_w_in'], p['ffn2_w_out'], dx, tag + "_ffn2")
        dx, gx = _xattn_block_bwd(s3, mem, p, dx, tag + "_xa")
        dx, gm = (_attn_lru_bwd if layer % 2 == 0 else _deltanet_bwd)(s2, p, dx, tag + "_mix")
        dx, g['ffn1_norm'], g['ffn1_w_in'], g['ffn1_w_out'] = _ffn_bwd(s1, p['ffn1_norm'], p['ffn1_w_in'], p['ffn1_w_out'], dx, tag + "_ffn1")
        g.update(gx)
        g.update(gm)
        per_layer.insert(0, g)
    grads = {'final_norm': d_final}
    for n in WEIGHTS[:-1]:
        grads[n] = jnp.stack([g[n] for g in per_layer if n in g], axis=0)
    return sq, dx, grads


def kernel(x, mem, ffn1_norm, ffn1_w_in, ffn1_w_out, mix_norm, xa_norm, xa_mem_norm, xa_wq, xa_wkv, xa_wo, ffn2_norm, ffn2_w_in, ffn2_w_out, ab_w_in, lru_conv_w, lru_conv_b, lru_w_a, lru_b_a, lru_w_x, lru_b_x, lru_lambda, ab_w_out, dn_w_in, dn_conv_w, dn_a_log, dn_dt_bias, dn_o_norm, dn_w_out, final_norm, loss_target, m_ffn1_norm, m_ffn1_w_in, m_ffn1_w_out, m_mix_norm, m_xa_norm, m_xa_mem_norm, m_xa_wq, m_xa_wkv, m_xa_wo, m_ffn2_norm, m_ffn2_w_in, m_ffn2_w_out, m_ab_w_in, m_lru_conv_w, m_lru_conv_b, m_lru_w_a, m_lru_b_a, m_lru_w_x, m_lru_b_x, m_lru_lambda, m_ab_w_out, m_dn_w_in, m_dn_conv_w, m_dn_a_log, m_dn_dt_bias, m_dn_o_norm, m_dn_w_out, m_final_norm, v_ffn1_norm, v_ffn1_w_in, v_ffn1_w_out, v_mix_norm, v_xa_norm, v_xa_mem_norm, v_xa_wq, v_xa_wkv, v_xa_wo, v_ffn2_norm, v_ffn2_w_in, v_ffn2_w_out, v_ab_w_in, v_lru_conv_w, v_lru_conv_b, v_lru_w_a, v_lru_b_a, v_lru_w_x, v_lru_b_x, v_lru_lambda, v_ab_w_out, v_dn_w_in, v_dn_conv_w, v_dn_a_log, v_dn_dt_bias, v_dn_o_norm, v_dn_w_out, v_final_norm):
    args = dict(locals())
    local = {n: args[n] for n in WEIGHTS}
    depth = ffn1_norm.shape[0]

    big = [n for n in WEIGHTS if n in SHARD_AXIS and n not in GATHER_F32]
    send16, spans16 = _pack([(local[n].astype(BF16), 0) for n in big])
    send32, spans32 = _pack([(local[n], 0) for n in GATHER_F32])
    got16 = _exchange(send16, name="gather_matrices", scatter=False)
    got32 = _exchange(send32, name="gather_filters", scatter=False)
    full = dict(local)
    for names, got, spans in ((big, got16, spans16), (GATHER_F32, got32, spans32)):
        for n, span in zip(names, spans):
            full[n] = _join_shards(_unpack(got, span, local[n].shape), SHARD_AXIS[n] + 1 - 1)
    dn_cols = full['dn_w_in'].shape[2]
    full['dn_w_in'] = jnp.pad(full['dn_w_in'], ((0, 0), (0, 0), (0, _dn_in_width(full['dn_w_out'].shape[1]) - dn_cols)))

    sq, dx, grads = _step(x[0], mem[0], loss_target[0], full, depth)
    grads['dn_w_in'] = grads['dn_w_in'][:, :, :dn_cols]
    loss = lax.psum(0.5 * jnp.sum(sq) / x.shape[2], ("x", "y", "c"))

    contrib = [(_split_shards(grads[n], SHARD_AXIS[n]) if n in SHARD_AXIS
                else jnp.broadcast_to(grads[n], (N_DEV,) + grads[n].shape), 1) for n in WEIGHTS]
    send, spans = _pack(contrib)
    parts = _exchange(send, name="scatter_grads", scatter=True)
    w_pack, _ = _pack([(local[n], 0) for n in WEIGHTS])
    m_pack, _ = _pack([(args["m_" + n], 0) for n in WEIGHTS])
    v_pack, _ = _pack([(args["v_" + n], 0) for n in WEIGHTS])
    outs = _sum_adamw(parts, w_pack, m_pack, v_pack, name="sum_adamw")
    grad_w, delta_w, new_m, new_v = ([_unpack(o, span, local[n].shape) for n, span in zip(WEIGHTS, spans)] for o in outs)
    return (loss, dx[None], *grad_w, *delta_w, *new_m, *new_v)
```

```python
import math

import jax
import jax.numpy as jnp
from jax import lax
from jax.experimental import pallas as pl
from jax.experimental.pallas import tpu as pltpu

F32, BF16 = jnp.float32, jnp.bfloat16
MESH = pl.DeviceIdType.MESH
N_DEV = 8
V7X_VMEM_LIMIT = 56 << 20
LANES = 128
PACK_COLS = 1024
PACK_ROWS = 16
HI = lax.Precision.HIGHEST
NEG = -1e30

NORM_EPS = 1e-6
CONV_K = 4
ATTN_HEADS = 8
DILATED_PATTERNS = ((128, 1), (512, 4), (2048, 16))
ATTN_BLOCK = 128
LRU_BLOCKS = 8
LRU_C = 8.0
DN_HEADS = 8
DN_CHUNK = 64
XA_HEADS = 4
ADAM_LR, ADAM_B1, ADAM_B2, ADAM_EPS, ADAM_WD, ADAM_STEP = 0.001, 0.9, 0.999, 1e-08, 0.01, 10

WEIGHTS = ['ffn1_norm', 'ffn1_w_in', 'ffn1_w_out', 'mix_norm', 'xa_norm', 'xa_mem_norm', 'xa_wq', 'xa_wkv', 'xa_wo',
           'ffn2_norm', 'ffn2_w_in', 'ffn2_w_out', 'ab_w_in', 'lru_conv_w', 'lru_conv_b', 'lru_w_a', 'lru_b_a',
           'lru_w_x', 'lru_b_x', 'lru_lambda', 'ab_w_out', 'dn_w_in', 'dn_conv_w', 'dn_a_log', 'dn_dt_bias',
           'dn_o_norm', 'dn_w_out', 'final_norm']
SHARD_AXIS = {'ffn1_w_in': 2, 'ffn1_w_out': 1, 'xa_wq': 1, 'xa_wkv': 2, 'xa_wo': 1, 'ffn2_w_in': 2, 'ffn2_w_out': 1,
              'ab_w_in': 2, 'lru_conv_w': 2, 'ab_w_out': 1, 'dn_w_in': 2, 'dn_conv_w': 2, 'dn_w_out': 1}
GATHER_F32 = ('lru_conv_w', 'dn_conv_w')


def _params(sem=None):
    return pltpu.CompilerParams(dimension_semantics=sem, vmem_limit_bytes=V7X_VMEM_LIMIT)


def _tile(n, pref, mult):
    best = None
    t = mult
    while t <= min(n, pref):
        if n % t == 0:
            best = t
        t += mult
    return n if best is None else best


def _dot(a, b, ca, cb, prec=None):
    return lax.dot_general(a, b, (((ca,), (cb,)), ((), ())), preferred_element_type=F32, precision=prec)


def _mm(a, b, *, name, ta=False, tb=False, out_dtype=F32, res=None, scale=1.0, tm=1024, tn=1024, tk=1024):
    m, kdim = (a.shape[1], a.shape[0]) if ta else a.shape
    n = b.shape[0] if tb else b.shape[1]
    assert (b.shape[1] if tb else b.shape[0]) == kdim
    tm = _tile(m, tm, LANES if ta else 16)
    tn = _tile(n, tn, LANES)
    tk = _tile(kdim, tk, LANES)
    nk = kdim // tk
    a_spec = pl.BlockSpec((tk, tm), lambda i, j, k: (k, i)) if ta else pl.BlockSpec((tm, tk), lambda i, j, k: (i, k))
    b_spec = pl.BlockSpec((tn, tk), lambda i, j, k: (j, k)) if tb else pl.BlockSpec((tk, tn), lambda i, j, k: (k, j))
    o_spec = pl.BlockSpec((tm, tn), lambda i, j, k: (i, j))
    ca, cb = (0 if ta else 1), (1 if tb else 0)
    has_res = res is not None

    def finish(acc, r_ref, o_ref):
        y = acc if scale == 1.0 else acc * scale
        if has_res:
            y = y + r_ref[...]
        o_ref[...] = y.astype(out_dtype)

    def body(*refs):
        a_ref, b_ref = refs[0], refs[1]
        r_ref = refs[2] if has_res else None
        o_ref = refs[2 + has_res]
        p = _dot(a_ref[...].astype(BF16), b_ref[...].astype(BF16), ca, cb)
        if nk == 1:
            finish(p, r_ref, o_ref)
            return
        acc = refs[3 + has_res]
        k = pl.program_id(2)

        @pl.when(k == 0)
        def _():
            acc[...] = p

        @pl.when(k > 0)
        def _():
            acc[...] += p

        @pl.when(k == nk - 1)
        def _():
            finish(acc[...], r_ref, o_ref)

    ins, specs = [a, b], [a_spec, b_spec]
    if has_res:
        ins.append(res)
        specs.append(o_spec)
    return pl.pallas_call(
        body, name=name, grid=(m // tm, n // tn, nk), in_specs=specs, out_specs=o_spec,
        out_shape=jax.ShapeDtypeStruct((m, n), out_dtype),
        scratch_shapes=[] if nk == 1 else [pltpu.VMEM((tm, tn), F32)],
        compiler_params=_params(("parallel", "parallel", "arbitrary")),
    )(*ins)


def _rms_fwd(x, g, *, name, out_dtype):
    r, d = x.shape
    tr = _tile(r, 512, 16)

    def body(x_ref, g_ref, o_ref):
        xv = x_ref[...]
        rs = lax.rsqrt(jnp.mean(xv * xv, axis=-1, keepdims=True) + NORM_EPS)
        o_ref[...] = (xv * rs * g_ref[...]).astype(out_dtype)

    return pl.pallas_call(
        body, name=name, grid=(r // tr,),
        in_specs=[pl.BlockSpec((tr, d), lambda i: (i, 0)), pl.BlockSpec((1, d), lambda i: (0, 0))],
        out_specs=pl.BlockSpec((tr, d), lambda i: (i, 0)),
        out_shape=jax.ShapeDtypeStruct((r, d), out_dtype), compiler_params=_params(("parallel",)),
    )(x, g.reshape(1, d))


def _rms_bwd(x, g, dh, dres, *, name):
    r, d = x.shape
    tr = _tile(r, 512, 8)
    has_res = dres is not None

    def body(*refs):
        x_ref, g_ref, dh_ref = refs[:3]
        r_ref = refs[3] if has_res else None
        dx_ref, dg_ref = refs[3 + has_res], refs[4 + has_res]
        xv = x_ref[...]
        rs = lax.rsqrt(jnp.mean(xv * xv, axis=-1, keepdims=True) + NORM_EPS)
        xh = xv * rs
        dhv = dh_ref[...]
        dgh = dhv * g_ref[...]
        dx = rs * (dgh - xh * jnp.mean(dgh * xh, axis=-1, keepdims=True))
        if has_res:
            dx = dx + r_ref[...]
        dx_ref[...] = dx
        part = jnp.sum(dhv * xh, axis=0, keepdims=True)

        @pl.when(pl.program_id(0) == 0)
        def _():
            dg_ref[...] = part

        @pl.when(pl.program_id(0) > 0)
        def _():
            dg_ref[...] += part

    row = pl.BlockSpec((tr, d), lambda i: (i, 0))
    vec = pl.BlockSpec((1, d), lambda i: (0, 0))
    ins, specs = [x, g.reshape(1, d), dh], [row, vec, row]
    if has_res:
        ins.append(dres)
        specs.append(row)
    dx, dg = pl.pallas_call(
        body, name=name, grid=(r // tr,), in_specs=specs, out_specs=[row, vec],
        out_shape=[jax.ShapeDtypeStruct((r, d), F32), jax.ShapeDtypeStruct((1, d), F32)],
        compiler_params=_params(("arbitrary",)),
    )(*ins)
    return dx, dg.reshape(d)


def _swiglu_fwd(u, *, name):
    t, f2 = u.shape
    f = f2 // 2
    tr = _tile(t, 256, 16)

    def body(u_ref, a_ref):
        gate, up = u_ref[:, :f], u_ref[:, f:]
        a_ref[...] = (gate * jax.nn.sigmoid(gate) * up).astype(BF16)

    return pl.pallas_call(
        body, name=name, grid=(t // tr,), in_specs=[pl.BlockSpec((tr, f2), lambda i: (i, 0))],
        out_specs=pl.BlockSpec((tr, f), lambda i: (i, 0)), out_shape=jax.ShapeDtypeStruct((t, f), BF16),
        compiler_params=_params(("parallel",)),
    )(u)


def _swiglu_bwd(u, da, *, name):
    t, f2 = u.shape
    f = f2 // 2
    tr = _tile(t, 256, 16)

    def body(u_ref, da_ref, du_ref):
        gate, up = u_ref[:, :f], u_ref[:, f:]
        s = jax.nn.sigmoid(gate)
        dav = da_ref[...]
        du_ref[:, :f] = (dav * up * (s * (1.0 + gate * (1.0 - s)))).astype(BF16)
        du_ref[:, f:] = (dav * gate * s).astype(BF16)

    return pl.pallas_call(
        body, name=name, grid=(t // tr,),
        in_specs=[pl.BlockSpec((tr, f2), lambda i: (i, 0)), pl.BlockSpec((tr, f), lambda i: (i, 0))],
        out_specs=pl.BlockSpec((tr, f2), lambda i: (i, 0)), out_shape=jax.ShapeDtypeStruct((t, f2), BF16),
        compiler_params=_params(("parallel",)),
    )(u, da)


def _final_loss(x, g, target, *, name):
    r, d = x.shape
    tr = _tile(r, 512, 8)

    def body(x_ref, g_ref, t_ref, sq_ref, dx_ref, dg_ref):
        xv = x_ref[...]
        gv = g_ref[...]
        rs = lax.rsqrt(jnp.mean(xv * xv, axis=-1, keepdims=True) + NORM_EPS)
        xh = xv * rs
        err = xh * gv - t_ref[...]
        dy = err * (1.0 / d)
        dgh = dy * gv
        dx_ref[...] = rs * (dgh - xh * jnp.mean(dgh * xh, axis=-1, keepdims=True))
        sq = jnp.sum(err * err, axis=0, keepdims=True)
        part = jnp.sum(dy * xh, axis=0, keepdims=True)

        @pl.when(pl.program_id(0) == 0)
        def _():
            sq_ref[...] = sq
            dg_ref[...] = part

        @pl.when(pl.program_id(0) > 0)
        def _():
            sq_ref[...] += sq
            dg_ref[...] += part

    row = pl.BlockSpec((tr, d), lambda i: (i, 0))
    vec = pl.BlockSpec((1, d), lambda i: (0, 0))
    sq, dx, dg = pl.pallas_call(
        body, name=name, grid=(r // tr,), in_specs=[row, vec, row], out_specs=[vec, row, vec],
        out_shape=[jax.ShapeDtypeStruct((1, d), F32), jax.ShapeDtypeStruct((r, d), F32),
                   jax.ShapeDtypeStruct((1, d), F32)],
        compiler_params=_params(("arbitrary",)),
    )(x, g.reshape(1, d), target)
    return sq.reshape(d), dx, dg.reshape(d)


def _scan(a, b, *, name, reverse=False):
    t, w = a.shape
    tb = _tile(t, 1024, 8)
    nblk, ngrp = t // tb, tb // 8

    def body(a_ref, b_ref, h_ref, carry):
        @pl.when(pl.program_id(0) == 0)
        def _():
            carry[...] = jnp.zeros_like(carry)

        row = lax.broadcasted_iota(jnp.int32, (8, w), 0)

        def group(i, c):
            r0 = pl.multiple_of((ngrp - 1 - i if reverse else i) * 8, 8)
            av, bv = a_ref[pl.ds(r0, 8), :], b_ref[pl.ds(r0, 8), :]
            for s in (1, 2, 4):
                keep = row < 8 - s if reverse else row >= s
                shift = 8 - s if reverse else s
                bv = jnp.where(keep, bv + av * pltpu.roll(bv, shift, axis=0), bv)
                av = jnp.where(keep, av * pltpu.roll(av, shift, axis=0), av)
            hv = bv + av * c
            h_ref[pl.ds(r0, 8), :] = hv
            return hv[0:1, :] if reverse else hv[7:8, :]

        carry[0:1, :] = lax.fori_loop(0, ngrp, group, carry[0:1, :])

    blk = pl.BlockSpec((tb, w), lambda i: (nblk - 1 - i if reverse else i, 0))
    return pl.pallas_call(
        body, name=name, grid=(t // tb,), in_specs=[blk, blk], out_specs=blk,
        out_shape=jax.ShapeDtypeStruct((t, w), F32), scratch_shapes=[pltpu.VMEM((8, w), F32)],
        compiler_params=_params(("arbitrary",)),
    )(a, b)


def _row_specs(t, tb, width, col):
    per = tb // 8
    main = pl.BlockSpec((tb, width), lambda i: (i, col))
    before = pl.BlockSpec((8, width), lambda i: (jnp.maximum(i * per - 1, 0), col))
    after = pl.BlockSpec((8, width), lambda i: (jnp.minimum((i + 1) * per, t // 8 - 1), col))
    return main, before, after


def _with_rows_before(x_ref, before_ref):
    return jnp.concatenate([jnp.where(pl.program_id(0) > 0, before_ref[...], 0.0), x_ref[...]], axis=0)


def _tap(xe, s):
    return xe[8:] if s == 0 else pltpu.roll(xe, s, axis=0)[8:]


def _conv_rows(xe, w_ref):
    return sum(_tap(xe, CONV_K - 1 - k) * w_ref[k:k + 1, :] for k in range(CONV_K))


def _conv_bwd(dy, x, col, w, *, name):
    t, width = dy.shape
    tb = _tile(t, 256, 8)
    nblk = t // tb

    def body(dy_ref, dy_after_ref, x_ref, x_before_ref, w_ref, dx_ref, dw_ref):
        i = pl.program_id(0)
        dyv = dy_ref[...]
        dye = jnp.concatenate([dyv, jnp.where(i < nblk - 1, dy_after_ref[...], 0.0)], axis=0)
        dx = dyv * w_ref[CONV_K - 1:CONV_K, :]
        for s in range(1, CONV_K):
            dx = dx + pltpu.roll(dye, tb + 8 - s, axis=0)[:tb] * w_ref[CONV_K - 1 - s:CONV_K - s, :]
        dx_ref[...] = dx.astype(BF16)
        xe = _with_rows_before(x_ref, x_before_ref)

        @pl.when(i == 0)
        def _():
            dw_ref[...] = jnp.zeros_like(dw_ref)

        for k in range(CONV_K):
            dw_ref[k:k + 1, :] += jnp.sum(dyv * _tap(xe, CONV_K - 1 - k), axis=0, keepdims=True)

    main, _, after = _row_specs(t, tb, width, 0)
    xmain, xbefore, _ = _row_specs(t, tb, width, col)
    dx, dw = pl.pallas_call(
        body, name=name, grid=(nblk,),
        in_specs=[main, after, xmain, xbefore, pl.BlockSpec((CONV_K, width), lambda i: (0, 0))],
        out_specs=[main, pl.BlockSpec((8, width), lambda i: (0, 0))],
        out_shape=[jax.ShapeDtypeStruct((t, width), BF16), jax.ShapeDtypeStruct((8, width), F32)],
        compiler_params=_params(("arbitrary",)),
    )(dy, dy, x, x, w)
    return dx, dw[:CONV_K]


def _expm1(x):
    small = x * (1.0 + x * (0.5 + x * (1.0 / 6.0 + x * (1.0 / 24.0 + x * (1.0 / 120.0 + x * (1.0 / 720.0))))))
    return jnp.where(jnp.abs(x) < 0.1, small, jnp.exp(x) - 1.0)


def _lru_gate_terms(xc, wa_ref, ba_ref, wx_ref, bx_ref, sp_ref):
    xb = xc.astype(BF16)
    r = jax.nn.sigmoid(_dot(xb, wa_ref[...], 1, 0) + ba_ref[...])
    i = jax.nn.sigmoid(_dot(xb, wx_ref[...], 1, 0) + bx_ref[...])
    log_a = -r * sp_ref[...]
    return r, i, jnp.exp(log_a), jnp.sqrt(-_expm1(2.0 * log_a))


def _lru_gates_fwd(proj, col, conv_w, conv_b, wa, ba, wx, bx, sp, *, name):
    t = proj.shape[0]
    width = conv_w.shape[1]
    tb = _tile(t, 512, 8)

    def body(x_ref, x_before_ref, cw_ref, cb_ref, wa_ref, ba_ref, wx_ref, bx_ref, sp_ref, a_ref, b_ref, xc_ref):
        xc = _conv_rows(_with_rows_before(x_ref, x_before_ref), cw_ref) + cb_ref[...]
        r, i, a, mult = _lru_gate_terms(xc, wa_ref, ba_ref, wx_ref, bx_ref, sp_ref)
        a_ref[...] = a
        b_ref[...] = mult * i * xc
        xc_ref[...] = xc

    main, before, _ = _row_specs(t, tb, width, col)
    out = pl.BlockSpec((tb, width), lambda i: (i, 0))
    vec = pl.BlockSpec((1, width), lambda i: (0, 0))
    mat = pl.BlockSpec((width, width), lambda i: (0, 0))
    return pl.pallas_call(
        body, name=name, grid=(t // tb,),
        in_specs=[main, before, pl.BlockSpec((CONV_K, width), lambda i: (0, 0)), vec, mat, vec, mat, vec, vec],
        out_specs=[out] * 3, out_shape=[jax.ShapeDtypeStruct((t, width), F32)] * 3,
        compiler_params=_params(("parallel",)),
    )(proj, proj, conv_w, conv_b.reshape(1, -1), wa, ba.reshape(1, -1), wx, bx.reshape(1, -1), sp.reshape(1, -1))


def _lru_gates_bwd(xc, dtot, h_prev, wa, ba, wx, bx, sp, *, name):
    t, width = xc.shape
    tb = _tile(t, 512, 8)

    def body(xc_ref, dt_ref, hp_ref, wa_ref, ba_ref, wx_ref, bx_ref, sp_ref, dxc_ref, dwa_ref, dwx_ref, vec_ref):
        xc = xc_ref[...]
        r, i, a, mult = _lru_gate_terms(xc, wa_ref, ba_ref, wx_ref, bx_ref, sp_ref)
        db = dt_ref[...]
        d_la = db * hp_ref[...] * a - db * i * xc * (a * a / mult)
        d_pa = (-d_la * sp_ref[...]) * r * (1.0 - r)
        d_pi = db * mult * xc * i * (1.0 - i)
        dab, dib = d_pa.astype(BF16), d_pi.astype(BF16)
        dxc = db * mult * i + _dot(dab, wa_ref[...], 1, 1) + _dot(dib, wx_ref[...], 1, 1)
        dxc_ref[...] = dxc
        xb = xc.astype(BF16)
        rows = [jnp.sum(z, axis=0, keepdims=True) for z in (d_pa, d_pi, -d_la * r, dxc)]

        @pl.when(pl.program_id(0) == 0)
        def _():
            dwa_ref[...] = jnp.zeros_like(dwa_ref)
            dwx_ref[...] = jnp.zeros_like(dwx_ref)
            vec_ref[...] = jnp.zeros_like(vec_ref)

        dwa_ref[...] += _dot(xb, dab, 0, 0)
        dwx_ref[...] += _dot(xb, dib, 0, 0)
        for j, z in enumerate(rows):
            vec_ref[j:j + 1, :] += z

    blk = pl.BlockSpec((tb, width), lambda i: (i, 0))
    vec = pl.BlockSpec((1, width), lambda i: (0, 0))
    mat = pl.BlockSpec((width, width), lambda i: (0, 0))
    dxc, dwa, dwx, vecs = pl.pallas_call(
        body, name=name, grid=(t // tb,), in_specs=[blk, blk, blk, mat, vec, mat, vec, vec],
        out_specs=[blk, mat, mat, pl.BlockSpec((8, width), lambda i: (0, 0))],
        out_shape=[jax.ShapeDtypeStruct((t, width), F32), jax.ShapeDtypeStruct((width, width), F32),
                   jax.ShapeDtypeStruct((width, width), F32), jax.ShapeDtypeStruct((8, width), F32)],
        compiler_params=_params(("arbitrary",)),
    )(xc, dtot, h_prev, wa, ba.reshape(1, -1), wx, bx.reshape(1, -1), sp.reshape(1, -1))
    return dxc, dwa, dwx, vecs[:4]


GELU_C = math.sqrt(2.0 / math.pi)


def _gelu_terms(x):
    th = jnp.tanh(GELU_C * (x + 0.044715 * x * x * x))
    return 0.5 * x * (1.0 + th), 0.5 * (1.0 + th) + 0.5 * x * (1.0 - th * th) * GELU_C * (1.0 + 3 * 0.044715 * x * x)


def _mix_join_fwd(outs, lses, hs, proj, gr_col, *, name):
    t, w = hs.shape
    tb = _tile(t, 512, 16)
    n = len(outs)

    def body(*refs):
        o_refs, l_refs = refs[:n], refs[n:2 * n]
        hs_ref, gr_ref, cat_ref, attn_ref, lse_ref = refs[2 * n:]
        ls = [r[...] for r in l_refs]
        m = ls[0]
        for l in ls[1:]:
            m = jnp.maximum(m, l)
        ws = [jnp.exp(l - m) for l in ls]
        den = sum(ws)
        attn = sum(wt * r[...] for wt, r in zip(ws, o_refs)) / den
        attn_ref[...] = attn
        lse_ref[...] = m + jnp.log(den)
        cat_ref[:, :w] = attn.astype(BF16)
        cat_ref[:, w:] = (hs_ref[...] * _gelu_terms(gr_ref[...])[0]).astype(BF16)

    blk = pl.BlockSpec((tb, w), lambda i: (i, 0))
    return pl.pallas_call(
        body, name=name, grid=(t // tb,),
        in_specs=[blk] * (2 * n + 1) + [pl.BlockSpec((tb, w), lambda i: (i, gr_col))],
        out_specs=[pl.BlockSpec((tb, 2 * w), lambda i: (i, 0)), blk, blk],
        out_shape=[jax.ShapeDtypeStruct((t, 2 * w), BF16), jax.ShapeDtypeStruct((t, w), F32),
                   jax.ShapeDtypeStruct((t, w), F32)],
        compiler_params=_params(("parallel",)),
    )(*outs, *lses, hs, proj)


def _mix_join_bwd(dcat, attn, hs, proj, gr_col, *, name):
    t, w = hs.shape
    hd = w // ATTN_HEADS
    tb = _tile(t, 512, 16)

    def body(dcat_ref, attn_ref, hs_ref, gr_ref, delta_ref, dhs_ref, dgr_ref):
        for h in range(ATTN_HEADS):
            sl = slice(h * hd, (h + 1) * hd)
            d = jnp.sum(dcat_ref[:, sl] * attn_ref[:, sl], axis=-1, keepdims=True)
            delta_ref[:, sl] = jnp.broadcast_to(d, (tb, hd))
        dy = dcat_ref[:, w:]
        g, dg = _gelu_terms(gr_ref[...])
        dhs_ref[...] = dy * g
        dgr_ref[...] = (dy * hs_ref[...] * dg).astype(BF16)

    blk = pl.BlockSpec((tb, w), lambda i: (i, 0))
    return pl.pallas_call(
        body, name=name, grid=(t // tb,),
        in_specs=[pl.BlockSpec((tb, 2 * w), lambda i: (i, 0)), blk, blk, pl.BlockSpec((tb, w), lambda i: (i, gr_col))],
        out_specs=[blk, blk, blk],
        out_shape=[jax.ShapeDtypeStruct((t, w), F32), jax.ShapeDtypeStruct((t, w), F32),
                   jax.ShapeDtypeStruct((t, w), BF16)],
        compiler_params=_params(("parallel",)),
    )(dcat, attn, hs, proj)


def _silu_terms(x):
    s = jax.nn.sigmoid(x)
    return x * s, s * (1.0 + x * (1.0 - s))


def _dn_prep_fwd(proj, conv_w, *, name):
    t = proj.shape[0]
    w3 = conv_w.shape[1]
    w = w3 // 3
    hd = w // DN_HEADS
    tb = _tile(t, 256, 8)

    def body(x_ref, x_before_ref, cw_ref, q_ref, k_ref, v_ref, c_ref):
        c = _conv_rows(_with_rows_before(x_ref, x_before_ref), cw_ref)
        c_ref[...] = c
        s = _silu_terms(c)[0]
        v_ref[...] = s[:, 2 * w:]
        for part, ref, scale in ((0, q_ref, hd ** -0.5), (1, k_ref, 1.0)):
            for h in range(DN_HEADS):
                z = s[:, part * w + h * hd:part * w + (h + 1) * hd]
                ref[:, h * hd:(h + 1) * hd] = z * (lax.rsqrt(jnp.sum(z * z, axis=-1, keepdims=True) + 1e-6) * scale)

    main, before, _ = _row_specs(t, tb, w3, 0)
    out = pl.BlockSpec((tb, w), lambda i: (i, 0))
    return pl.pallas_call(
        body, name=name, grid=(t // tb,), in_specs=[main, before, pl.BlockSpec((CONV_K, w3), lambda i: (0, 0))],
        out_specs=[out, out, out, pl.BlockSpec((tb, w3), lambda i: (i, 0))],
        out_shape=[jax.ShapeDtypeStruct((t, w), F32)] * 3 + [jax.ShapeDtypeStruct((t, w3), F32)],
        compiler_params=_params(("parallel",)),
    )(proj, proj, conv_w)


def _dn_prep_bwd(c, dq, dk, dv, *, name):
    t, w3 = c.shape
    w = w3 // 3
    hd = w // DN_HEADS
    tb = _tile(t, 256, 8)

    def body(c_ref, dq_ref, dk_ref, dv_ref, dc_ref):
        cv = c_ref[...]
        s, ds = _silu_terms(cv)
        dc_ref[:, 2 * w:] = dv_ref[...] * ds[:, 2 * w:]
        for part, ref, scale in ((0, dq_ref, hd ** -0.5), (1, dk_ref, 1.0)):
            for h in range(DN_HEADS):
                cols = slice(part * w + h * hd, part * w + (h + 1) * hd)
                z = s[:, cols]
                rn = lax.rsqrt(jnp.sum(z * z, axis=-1, keepdims=True) + 1e-6)
                y = z * rn
                dy = ref[:, h * hd:(h + 1) * hd] * scale
                dc_ref[:, cols] = rn * (dy - y * jnp.sum(dy * y, axis=-1, keepdims=True)) * ds[:, cols]

    blk = pl.BlockSpec((tb, w), lambda i: (i, 0))
    wide = pl.BlockSpec((tb, w3), lambda i: (i, 0))
    return pl.pallas_call(
        body, name=name, grid=(t // tb,), in_specs=[wide, blk, blk, blk], out_specs=wide,
        out_shape=jax.ShapeDtypeStruct((t, w3), F32), compiler_params=_params(("parallel",)),
    )(c, dq, dk, dv)


def _dn_gate_fwd(o, proj, z_col, o_norm, *, name):
    t, w = o.shape
    hd = w // DN_HEADS
    tb = _tile(t, 512, 16)

    def body(o_ref, z_ref, g_ref, y_ref):
        for h in range(DN_HEADS):
            sl = slice(h * hd, (h + 1) * hd)
            ov = o_ref[:, sl]
            rn = lax.rsqrt(jnp.mean(ov * ov, axis=-1, keepdims=True) + NORM_EPS)
            y_ref[:, sl] = (ov * rn * g_ref[...] * _silu_terms(z_ref[:, sl])[0]).astype(BF16)

    blk = pl.BlockSpec((tb, w), lambda i: (i, 0))
    return pl.pallas_call(
        body, name=name, grid=(t // tb,),
        in_specs=[blk, pl.BlockSpec((tb, w), lambda i: (i, z_col)), pl.BlockSpec((1, hd), lambda i: (0, 0))],
        out_specs=blk, out_shape=jax.ShapeDtypeStruct((t, w), BF16), compiler_params=_params(("parallel",)),
    )(o, proj, o_norm.reshape(1, hd))


def _dn_gate_bwd(o, proj, z_col, o_norm, dy, *, name):
    t, w = o.shape
    hd = w // DN_HEADS
    tb = _tile(t, 512, 16)

    def body(o_ref, z_ref, g_ref, dy_ref, do_ref, dz_ref, dg_ref):
        gv = g_ref[...]
        dg = jnp.zeros((1, hd), F32)
        for h in range(DN_HEADS):
            sl = slice(h * hd, (h + 1) * hd)
            ov, dyv = o_ref[:, sl], dy_ref[:, sl]
            sz, dsz = _silu_terms(z_ref[:, sl])
            rn = lax.rsqrt(jnp.mean(ov * ov, axis=-1, keepdims=True) + NORM_EPS)
            nv = ov * rn
            dz_ref[:, sl] = (dyv * nv * gv * dsz).astype(BF16)
            dn = dyv * gv * sz
            do_ref[:, sl] = rn * (dn - nv * jnp.mean(dn * nv, axis=-1, keepdims=True))
            dg = dg + jnp.sum(dyv * nv * sz, axis=0, keepdims=True)

        @pl.when(pl.program_id(0) == 0)
        def _():
            dg_ref[...] = dg

        @pl.when(pl.program_id(0) > 0)
        def _():
            dg_ref[...] += dg

    blk = pl.BlockSpec((tb, w), lambda i: (i, 0))
    vec = pl.BlockSpec((1, hd), lambda i: (0, 0))
    do, dz, dg = pl.pallas_call(
        body, name=name, grid=(t // tb,), in_specs=[blk, pl.BlockSpec((tb, w), lambda i: (i, z_col)), vec, blk],
        out_specs=[blk, blk, vec],
        out_shape=[jax.ShapeDtypeStruct((t, w), F32), jax.ShapeDtypeStruct((t, w), BF16),
                   jax.ShapeDtypeStruct((1, hd), F32)],
        compiler_params=_params(("arbitrary",)),
    )(o, proj, o_norm.reshape(1, hd), dy)
    return do, dz, dg.reshape(hd)


def _band_masks(n):
    qi = lax.broadcasted_iota(jnp.int32, (ATTN_BLOCK, ATTN_BLOCK), 0)
    kj = lax.broadcasted_iota(jnp.int32, (ATTN_BLOCK, ATTN_BLOCK), 1)
    return kj <= qi, jnp.logical_and(kj >= qi, n > 0)


def _dattn_fwd(q, k, v, dil, *, name):
    t, w = q.shape
    hd = w // ATTN_HEADS
    seq = t // dil
    assert seq % ATTN_BLOCK == 0
    nb = seq // ATTN_BLOCK
    scale = hd ** -0.5

    def body(q_ref, kp_ref, kc_ref, vp_ref, vc_ref, o_ref, lse_ref):
        mc, mp = _band_masks(pl.program_id(1))
        for h in range(ATTN_HEADS):
            sl = slice(h * hd, (h + 1) * hd)
            qh = (q_ref[:, sl] * scale).astype(BF16)
            sc = jnp.where(mc, _dot(qh, kc_ref[:, sl].astype(BF16), 1, 1), NEG)
            sp = jnp.where(mp, _dot(qh, kp_ref[:, sl].astype(BF16), 1, 1), NEG)
            m = jnp.maximum(jnp.max(sc, axis=-1, keepdims=True), jnp.max(sp, axis=-1, keepdims=True))
            pc, pp = jnp.exp(sc - m), jnp.exp(sp - m)
            den = jnp.sum(pc, axis=-1, keepdims=True) + jnp.sum(pp, axis=-1, keepdims=True)
            o = _dot(pc.astype(BF16), vc_ref[:, sl].astype(BF16), 1, 0) + _dot(pp.astype(BF16), vp_ref[:, sl].astype(BF16), 1, 0)
            o_ref[:, sl] = o / den
            lse_ref[:, sl] = jnp.broadcast_to(m + jnp.log(den), (ATTN_BLOCK, hd))

    cur = pl.BlockSpec((ATTN_BLOCK, w), lambda r, n: (n, r))
    prev = pl.BlockSpec((ATTN_BLOCK, w), lambda r, n: (jnp.maximum(n - 1, 0), r))
    view = lambda z: z.reshape(seq, dil * w)
    o, lse = pl.pallas_call(
        body, name=name, grid=(dil, nb), in_specs=[cur, prev, cur, prev, cur], out_specs=[cur, cur],
        out_shape=[jax.ShapeDtypeStruct((seq, dil * w), F32)] * 2, compiler_params=_params(("parallel", "parallel")),
    )(view(q), view(k), view(k), view(v), view(v))
    return o.reshape(t, w), lse.reshape(t, w)


def _dattn_bwd(q, k, v, do, lse, delta, dil, *, name):
    t, w = q.shape
    hd = w // ATTN_HEADS
    seq = t // dil
    nb = seq // ATTN_BLOCK
    scale = hd ** -0.5

    def body(qc_ref, qn_ref, doc_ref, don_ref, lc_ref, ln_ref, dc_ref, dn_ref, kp_ref, kc_ref, vp_ref, vc_ref,
             dq_ref, dk_ref, dv_ref):
        n = pl.program_id(1)
        mc, mp = _band_masks(n)
        _, mx = _band_masks(jnp.where(n + 1 < nb, 1, 0))
        for h in range(ATTN_HEADS):
            sl = slice(h * hd, (h + 1) * hd)
            one = slice(h * hd, h * hd + 1)
            qc, qn = (qc_ref[:, sl] * scale).astype(BF16), (qn_ref[:, sl] * scale).astype(BF16)
            kc, kp = kc_ref[:, sl].astype(BF16), kp_ref[:, sl].astype(BF16)
            vc, vp = vc_ref[:, sl].astype(BF16), vp_ref[:, sl].astype(BF16)
            doc, don = doc_ref[:, sl].astype(BF16), don_ref[:, sl].astype(BF16)
            p_c = jnp.exp(jnp.where(mc, _dot(qc, kc, 1, 1), NEG) - lc_ref[:, one])
            p_p = jnp.exp(jnp.where(mp, _dot(qc, kp, 1, 1), NEG) - lc_ref[:, one])
            p_x = jnp.exp(jnp.where(mx, _dot(qn, kc, 1, 1), NEG) - ln_ref[:, one])
            ds_c = (p_c * (_dot(doc, vc, 1, 1) - dc_ref[:, one])).astype(BF16)
            ds_p = (p_p * (_dot(doc, vp, 1, 1) - dc_ref[:, one])).astype(BF16)
            ds_x = (p_x * (_dot(don, vc, 1, 1) - dn_ref[:, one])).astype(BF16)
            dq_ref[:, sl] = (_dot(ds_c, kc, 1, 0) + _dot(ds_p, kp, 1, 0)) * scale
            dk_ref[:, sl] = _dot(ds_c, qc, 0, 0) + _dot(ds_x, qn, 0, 0)
            dv_ref[:, sl] = _dot(p_c.astype(BF16), doc, 0, 0) + _dot(p_x.astype(BF16), don, 0, 0)

    cur = pl.BlockSpec((ATTN_BLOCK, w), lambda r, n: (n, r))
    prev = pl.BlockSpec((ATTN_BLOCK, w), lambda r, n: (jnp.maximum(n - 1, 0), r))
    nxt = pl.BlockSpec((ATTN_BLOCK, w), lambda r, n: (jnp.minimum(n + 1, nb - 1), r))
    view = lambda z: z.reshape(seq, dil * w)
    q, k, v, do, lse, delta = (view(z) for z in (q, k, v, do, lse, delta))
    dq, dk, dv = pl.pallas_call(
        body, name=name, grid=(dil, nb),
        in_specs=[cur, nxt, cur, nxt, cur, nxt, cur, nxt, prev, cur, prev, cur], out_specs=[cur, cur, cur],
        out_shape=[jax.ShapeDtypeStruct((seq, dil * w), F32)] * 3, compiler_params=_params(("parallel", "parallel")),
    )(q, q, do, do, lse, lse, delta, delta, k, k, v, v)
    return dq.reshape(t, w), dk.reshape(t, w), dv.reshape(t, w)


def _xattn_fwd(q, k, v, *, name):
    t, w = q.shape
    nm = k.shape[0]
    hd = w // XA_HEADS
    scale = hd ** -0.5
    tq = _tile(t, 512, 8)

    def body(q_ref, k_ref, v_ref, o_ref):
        for h in range(XA_HEADS):
            sl = slice(h * hd, (h + 1) * hd)
            s = _dot((q_ref[:, sl] * scale).astype(BF16), k_ref[:, sl].astype(BF16), 1, 1)
            p = jnp.exp(s - jnp.max(s, axis=-1, keepdims=True))
            p = p / jnp.sum(p, axis=-1, keepdims=True)
            o_ref[:, sl] = _dot(p.astype(BF16), v_ref[:, sl].astype(BF16), 1, 0)

    qs = pl.BlockSpec((tq, w), lambda i: (i, 0))
    ks = pl.BlockSpec((nm, w), lambda i: (0, 0))
    return pl.pallas_call(
        body, name=name, grid=(t // tq,), in_specs=[qs, ks, ks], out_specs=qs,
        out_shape=jax.ShapeDtypeStruct((t, w), F32), compiler_params=_params(("parallel",)),
    )(q, k, v)


def _xattn_bwd(q, k, v, do, *, name):
    t, w = q.shape
    nm = k.shape[0]
    hd = w // XA_HEADS
    scale = hd ** -0.5
    tq = _tile(t, 512, 8)

    def body(q_ref, k_ref, v_ref, do_ref, dq_ref, dk_ref, dv_ref):
        first = pl.program_id(0) == 0
        for h in range(XA_HEADS):
            sl = slice(h * hd, (h + 1) * hd)
            qh = (q_ref[:, sl] * scale).astype(BF16)
            kh, vh, doh = k_ref[:, sl].astype(BF16), v_ref[:, sl].astype(BF16), do_ref[:, sl].astype(BF16)
            s = _dot(qh, kh, 1, 1)
            p = jnp.exp(s - jnp.max(s, axis=-1, keepdims=True))
            p = p / jnp.sum(p, axis=-1, keepdims=True)
            dp = _dot(doh, vh, 1, 1)
            ds = (p * (dp - jnp.sum(p * dp, axis=-1, keepdims=True))).astype(BF16)
            dq_ref[:, sl] = _dot(ds, kh, 1, 0) * scale
            dk = _dot(ds, qh, 0, 0)
            dv = _dot(p.astype(BF16), doh, 0, 0)

            @pl.when(first)
            def _():
                dk_ref[:, sl] = dk
                dv_ref[:, sl] = dv

            @pl.when(jnp.logical_not(first))
            def _():
                dk_ref[:, sl] += dk
                dv_ref[:, sl] += dv

    qs = pl.BlockSpec((tq, w), lambda i: (i, 0))
    ks = pl.BlockSpec((nm, w), lambda i: (0, 0))
    return pl.pallas_call(
        body, name=name, grid=(t // tq,), in_specs=[qs, ks, ks, qs], out_specs=[qs, ks, ks],
        out_shape=[jax.ShapeDtypeStruct((t, w), F32), jax.ShapeDtypeStruct((nm, w), F32),
                   jax.ShapeDtypeStruct((nm, w), F32)],
        compiler_params=_params(("arbitrary",)),
    )(q, k, v, do)


def _dn_chunk_terms(qh, kh, gcc, gcr, bh):
    c = DN_CHUNK
    row = lax.broadcasted_iota(jnp.int32, (c, c), 0)
    col = lax.broadcasted_iota(jnp.int32, (c, c), 1)
    decay = jnp.exp(jnp.where(row >= col, gcc - gcr, NEG))
    kb = kh * bh
    kkt = _dot(kb, kh, 1, 1, HI)
    qkt = _dot(qh, kh, 1, 1, HI)
    return row, col, decay, kb, kkt, qkt


def _unit_lower_inverse(nm, row, col):
    eye = (row == col).astype(F32)
    inv = eye - nm
    p = nm
    for _ in range(int(math.log2(DN_CHUNK)) - 1):
        p = _dot(p, p, 1, 0, HI)
        inv = inv + _dot(inv, p, 1, 0, HI)
    return inv


def _dn_fwd(q, k, v, gc, gcr, beta, *, name):
    t, w = q.shape
    hd = w // DN_HEADS
    c = DN_CHUNK
    nch = t // c

    def body(q_ref, k_ref, v_ref, gc_ref, gcr_ref, b_ref, o_ref, s_ref, inv_ref, state):
        @pl.when(pl.program_id(0) == 0)
        def _():
            state[...] = jnp.zeros_like(state)

        for h in range(DN_HEADS):
            sl = slice(h * hd, (h + 1) * hd)
            qh, kh, vh = q_ref[:, sl], k_ref[:, sl], v_ref[:, sl]
            gcc, gcr_h, bh = gc_ref[:, h:h + 1], gcr_ref[0, h:h + 1, :], b_ref[:, h:h + 1]
            row, col, decay, kb, kkt, qkt = _dn_chunk_terms(qh, kh, gcc, gcr_h, bh)
            inv = _unit_lower_inverse(jnp.where(row > col, kkt * decay, 0.0), row, col)
            e = jnp.exp(gcc)
            u = _dot(inv, vh * bh, 1, 0, HI)
            wm = _dot(inv, kb * e, 1, 0, HI)
            sh = state[h]
            v_new = u - _dot(wm, sh, 1, 0, HI)
            o_ref[:, sl] = _dot(qh * e, sh, 1, 0, HI) + _dot(qkt * decay, v_new, 1, 0, HI)
            s_ref[0, h] = sh
            inv_ref[0, h] = inv
            gl = gcc[c - 1:c, :]
            state[h] = sh * jnp.exp(gl) + _dot(kh * jnp.exp(gl - gcc), v_new, 0, 0, HI)

    rows = pl.BlockSpec((c, w), lambda n: (n, 0))
    cols = pl.BlockSpec((c, DN_HEADS), lambda n: (n, 0))
    rowg = pl.BlockSpec((1, DN_HEADS, c), lambda n: (n, 0, 0))
    return pl.pallas_call(
        body, name=name, grid=(nch,), in_specs=[rows, rows, rows, cols, rowg, cols],
        out_specs=[rows, pl.BlockSpec((1, DN_HEADS, hd, hd), lambda n: (n, 0, 0, 0)),
                   pl.BlockSpec((1, DN_HEADS, c, c), lambda n: (n, 0, 0, 0))],
        out_shape=[jax.ShapeDtypeStruct((t, w), F32), jax.ShapeDtypeStruct((nch, DN_HEADS, hd, hd), F32),
                   jax.ShapeDtypeStruct((nch, DN_HEADS, c, c), F32)],
        scratch_shapes=[pltpu.VMEM((DN_HEADS, hd, hd), F32)], compiler_params=_params(("arbitrary",)),
    )(q, k, v, gc, gcr, beta)


def _dn_bwd(q, k, v, gc, gcr, beta, states, invs, do, *, name):
    t, w = q.shape
    hd = w // DN_HEADS
    c = DN_CHUNK
    nch = t // c

    def body(q_ref, k_ref, v_ref, gc_ref, gcr_ref, b_ref, s_ref, inv_ref, do_ref,
             dq_ref, dk_ref, dv_ref, dgc_ref, dgr_ref, db_ref, dstate):
        @pl.when(pl.program_id(0) == 0)
        def _():
            dstate[...] = jnp.zeros_like(dstate)

        for h in range(DN_HEADS):
            sl = slice(h * hd, (h + 1) * hd)
            qh, kh, vh, doh = q_ref[:, sl], k_ref[:, sl], v_ref[:, sl], do_ref[:, sl]
            gcc, gcr_h, bh = gc_ref[:, h:h + 1], gcr_ref[0, h:h + 1, :], b_ref[:, h:h + 1]
            row, col, decay, kb, kkt, qkt = _dn_chunk_terms(qh, kh, gcc, gcr_h, bh)
            inv, sh, dsn = inv_ref[0, h], s_ref[0, h], dstate[h]
            e = jnp.exp(gcc)
            gl = gcc[c - 1:c, :]
            el, r = jnp.exp(gl), jnp.exp(gl - gcc)
            u = _dot(inv, vh * bh, 1, 0, HI)
            wm = _dot(inv, kb * e, 1, 0, HI)
            v_new = u - _dot(wm, sh, 1, 0, HI)
            qk = qkt * decay
            kr = kh * r
            qe = qh * e
            d_qe = _dot(doh, sh, 1, 1, HI)
            d_qk = _dot(doh, v_new, 1, 1, HI)
            d_vnew = _dot(qk, doh, 0, 0, HI) + _dot(kr, dsn, 1, 0, HI)
            d_el = jnp.sum(jnp.sum(dsn * sh, axis=1, keepdims=True), axis=0, keepdims=True)
            d_kr = _dot(v_new, dsn, 1, 1, HI)
            d_w = -_dot(d_vnew, sh, 1, 1, HI)
            dstate[h] = dsn * el + _dot(qe, doh, 0, 0, HI) - _dot(wm, d_vnew, 0, 0, HI)
            d_ru = _dot(inv, d_vnew, 0, 0, HI)
            d_rw = _dot(inv, d_w, 0, 0, HI)
            d_a = -(_dot(d_ru, u, 1, 1, HI) + _dot(d_rw, wm, 1, 1, HI))
            dv_ref[:, sl] = d_ru * bh
            d_kb = d_rw * e
            d_e = jnp.sum(d_rw * kb, axis=1, keepdims=True) + jnp.sum(d_qe * qh, axis=1, keepdims=True)
            d_n = jnp.where(row > col, d_a, 0.0)
            d_m = d_n * decay
            d_p = d_qk * decay
            d_kb = d_kb + _dot(d_m, kh, 1, 0, HI)
            dk = _dot(d_m, kb, 0, 0, HI) + _dot(d_p, qh, 0, 0, HI)
            dq_ref[:, sl] = _dot(d_p, kh, 1, 0, HI) + d_qe * e
            dd = (d_n * kkt + d_qk * qkt) * decay
            d_r = jnp.sum(d_kr * kh, axis=1, keepdims=True)
            d_gl = d_el * el + jnp.sum(d_r * r, axis=0, keepdims=True)
            last = lax.broadcasted_iota(jnp.int32, (c, 1), 0) == c - 1
            dgc_ref[:, h:h + 1] = (jnp.sum(dd, axis=1, keepdims=True) + d_e * e - d_r * r + jnp.where(last, d_gl, 0.0))
            dgr_ref[0, h:h + 1, :] = -jnp.sum(dd, axis=0, keepdims=True)
            dk_ref[:, sl] = dk + d_kr * r + d_kb * bh
            db_ref[:, h:h + 1] = jnp.sum(d_ru * vh, axis=1, keepdims=True) + jnp.sum(d_kb * kh, axis=1, keepdims=True)

    rev = lambda n: nch - 1 - n
    rows = pl.BlockSpec((c, w), lambda n: (rev(n), 0))
    cols = pl.BlockSpec((c, DN_HEADS), lambda n: (rev(n), 0))
    rowg = pl.BlockSpec((1, DN_HEADS, c), lambda n: (rev(n), 0, 0))
    st = pl.BlockSpec((1, DN_HEADS, hd, hd), lambda n: (rev(n), 0, 0, 0))
    iv = pl.BlockSpec((1, DN_HEADS, c, c), lambda n: (rev(n), 0, 0, 0))
    return pl.pallas_call(
        body, name=name, grid=(nch,), in_specs=[rows, rows, rows, cols, rowg, cols, st, iv, rows],
        out_specs=[rows, rows, rows, cols, rowg, cols],
        out_shape=[jax.ShapeDtypeStruct((t, w), F32)] * 3
        + [jax.ShapeDtypeStruct((t, DN_HEADS), F32), jax.ShapeDtypeStruct((nch, DN_HEADS, c), F32),
           jax.ShapeDtypeStruct((t, DN_HEADS), F32)],
        scratch_shapes=[pltpu.VMEM((DN_HEADS, hd, hd), F32)], compiler_params=_params(("arbitrary",)),
    )(q, k, v, gc, gcr, beta, states, invs, do)


def _dn_head_terms(q_ref, k_ref, gc_ref, gcr_ref, b_ref, h, hd):
    c = DN_CHUNK
    sl = slice(h * hd, (h + 1) * hd)
    qh, kh = q_ref[:, sl], k_ref[:, sl]
    gcc, gcr_h, bh = gc_ref[:, h:h + 1], gcr_ref[0, h:h + 1, :], b_ref[:, h:h + 1]
    row = lax.broadcasted_iota(jnp.int32, (c, c), 0)
    col = lax.broadcasted_iota(jnp.int32, (c, c), 1)
    decay = jnp.exp(jnp.where(row >= col, gcc - gcr_h, NEG))
    kb = kh * bh
    kkt = _dot(kb.astype(BF16), kh.astype(BF16), 1, 1)
    qkt = _dot(qh.astype(BF16), kh.astype(BF16), 1, 1)
    gl = gcc[c - 1:c, :]
    return dict(sl=sl, q=qh, k=kh, gcc=gcc, b=bh, row=row, col=col, decay=decay, kb=kb, kkt=kkt, qkt=qkt,
                e=jnp.exp(gcc), el=jnp.exp(gl), r=jnp.exp(gl - gcc))


def _dn_local_fwd(q, k, v, gc, gcr, beta, *, name):
    t, w = q.shape
    hd = w // DN_HEADS
    c = DN_CHUNK
    nch = t // c
    heads = range(DN_HEADS)

    def body(q_ref, k_ref, v_ref, gc_ref, gcr_ref, b_ref,
             u_ref, wq_ref, wqt_ref, kr_ref, krt_ref, qk_ref, qkt_ref, invt_ref):
        tm = [_dn_head_terms(q_ref, k_ref, gc_ref, gcr_ref, b_ref, h, hd) for h in heads]
        pw = [jnp.where(m['row'] > m['col'], m['kkt'] * m['decay'], 0.0) for m in tm]
        inv = [(m['row'] == m['col']).astype(F32) - p for m, p in zip(tm, pw)]
        for _ in range(int(math.log2(c)) - 1):
            pw = [_dot(p.astype(BF16), p.astype(BF16), 1, 0) for p in pw]
            inv = [i + _dot(i.astype(BF16), p.astype(BF16), 1, 0) for i, p in zip(inv, pw)]
        for h, m in zip(heads, tm):
            rhs = jnp.concatenate([v_ref[:, m['sl']] * m['b'], m['kb'] * m['e']], axis=1).astype(BF16)
            sol = _dot(inv[h].astype(BF16), rhs, 1, 0)
            u_ref[:, m['sl']] = sol[:, :hd]
            wq = jnp.concatenate([sol[:, hd:], m['q'] * m['e']], axis=0)
            kr = m['k'] * m['r']
            qk = m['qkt'] * m['decay']
            wq_ref[0, h], wqt_ref[0, h] = wq.astype(BF16), wq.T.astype(BF16)
            kr_ref[0, h], krt_ref[0, h] = kr.astype(BF16), kr.T.astype(BF16)
            qk_ref[0, h], qkt_ref[0, h] = qk.astype(BF16), qk.T.astype(BF16)
            invt_ref[0, h] = inv[h].T.astype(BF16)

    rows = pl.BlockSpec((c, w), lambda n: (n, 0))
    cols = pl.BlockSpec((c, DN_HEADS), lambda n: (n, 0))
    rowg = pl.BlockSpec((1, DN_HEADS, c), lambda n: (n, 0, 0))
    per = lambda a, b: (pl.BlockSpec((1, DN_HEADS, a, b), lambda n: (n, 0, 0, 0)),
                        jax.ShapeDtypeStruct((nch, DN_HEADS, a, b), BF16))
    outs = [(rows, jax.ShapeDtypeStruct((t, w), F32)), per(2 * c, hd), per(hd, 2 * c), per(c, hd), per(hd, c),
            per(c, c), per(c, c), per(c, c)]
    return pl.pallas_call(
        body, name=name, grid=(nch,), in_specs=[rows, rows, rows, cols, rowg, cols],
        out_specs=[o[0] for o in outs], out_shape=[o[1] for o in outs], compiler_params=_params(("parallel",)),
    )(q, k, v, gc, gcr, beta)


def _dn_state_fwd(u, wq, krt, qk, gc, *, name):
    t, w = u.shape
    hd = w // DN_HEADS
    c = DN_CHUNK
    nch = t // c

    def body(u_ref, wq_ref, krt_ref, qk_ref, gc_ref, o_ref, vn_ref, s_ref, state):
        @pl.when(pl.program_id(0) == 0)
        def _():
            state[...] = jnp.zeros_like(state)

        for h in range(DN_HEADS):
            sl = slice(h * hd, (h + 1) * hd)
            sh = state[h]
            s_ref[0, h] = sh
            ws = _dot(wq_ref[0, h], sh.astype(BF16), 1, 0)
            v_new = (u_ref[:, sl] - ws[:c]).astype(BF16)
            vn_ref[:, sl] = v_new
            o_ref[:, sl] = ws[c:] + _dot(qk_ref[0, h], v_new, 1, 0)
            state[h] = sh * jnp.exp(gc_ref[c - 1:c, h:h + 1]) + _dot(krt_ref[0, h], v_new, 1, 0)

    rows = pl.BlockSpec((c, w), lambda n: (n, 0))
    per = lambda a, b: pl.BlockSpec((1, DN_HEADS, a, b), lambda n: (n, 0, 0, 0))
    return pl.pallas_call(
        body, name=name, grid=(nch,),
        in_specs=[rows, per(2 * c, hd), per(hd, c), per(c, c), pl.BlockSpec((c, DN_HEADS), lambda n: (n, 0))],
        out_specs=[rows, rows, per(hd, hd)],
        out_shape=[jax.ShapeDtypeStruct((t, w), F32), jax.ShapeDtypeStruct((t, w), BF16),
                   jax.ShapeDtypeStruct((nch, DN_HEADS, hd, hd), F32)],
        scratch_shapes=[pltpu.VMEM((DN_HEADS, hd, hd), F32)], compiler_params=_params(("arbitrary",)),
    )(u, wq, krt, qk, gc)


def _dn_state_bwd(do, qkt, kr, vn, states, wqt, gc, *, name):
    t, w = do.shape
    hd = w // DN_HEADS
    c = DN_CHUNK
    nch = t // c

    def body(do_ref, qkt_ref, kr_ref, vn_ref, s_ref, wqt_ref, gc_ref, dvn_ref, dkr_ref, del_ref, dstate):
        @pl.when(pl.program_id(0) == 0)
        def _():
            dstate[...] = jnp.zeros_like(dstate)

        for h in range(DN_HEADS):
            sl = slice(h * hd, (h + 1) * hd)
            dsn = dstate[h]
            dsb, dob = dsn.astype(BF16), do_ref[:, sl].astype(BF16)
            d_vnew = (_dot(qkt_ref[0, h], dob, 1, 0) + _dot(kr_ref[0, h], dsb, 1, 0)).astype(BF16)
            dvn_ref[:, sl] = d_vnew
            dkr_ref[:, sl] = _dot(vn_ref[:, sl], dsb, 1, 1)
            d_el = jnp.sum(jnp.sum(dsn * s_ref[0, h], axis=1, keepdims=True), axis=0, keepdims=True)
            del_ref[0, h:h + 1, :] = jnp.broadcast_to(d_el, (1, LANES))
            dstate[h] = dsn * jnp.exp(gc_ref[c - 1:c, h:h + 1]) + _dot(
                wqt_ref[0, h], jnp.concatenate([-d_vnew, dob], axis=0), 1, 0)

    rev = lambda n: nch - 1 - n
    rows = pl.BlockSpec((c, w), lambda n: (rev(n), 0))
    per = lambda a, b: pl.BlockSpec((1, DN_HEADS, a, b), lambda n: (rev(n), 0, 0, 0))
    return pl.pallas_call(
        body, name=name, grid=(nch,),
        in_specs=[rows, per(c, c), per(c, hd), rows, per(hd, hd), per(hd, 2 * c),
                  pl.BlockSpec((c, DN_HEADS), lambda n: (rev(n), 0))],
        out_specs=[rows, rows, pl.BlockSpec((1, DN_HEADS, LANES), lambda n: (rev(n), 0, 0))],
        out_shape=[jax.ShapeDtypeStruct((t, w), BF16), jax.ShapeDtypeStruct((t, w), F32),
                   jax.ShapeDtypeStruct((nch, DN_HEADS, LANES), F32)],
        scratch_shapes=[pltpu.VMEM((DN_HEADS, hd, hd), F32)], compiler_params=_params(("arbitrary",)),
    )(do, qkt, kr, vn, states, wqt, gc)


def _dn_local_bwd(q, k, v, gc, gcr, beta, invt, u, wq, vn, states, do, dvn, dkr, d_el, *, name):
    t, w = q.shape
    hd = w // DN_HEADS
    c = DN_CHUNK
    nch = t // c

    def body(q_ref, k_ref, v_ref, gc_ref, gcr_ref, b_ref, invt_ref, u_ref, wq_ref, vn_ref, s_ref, do_ref, dvn_ref,
             dkr_ref, del_ref, dq_ref, dk_ref, dv_ref, dgc_ref, dgr_ref, db_ref):
        for h in range(DN_HEADS):
            m = _dn_head_terms(q_ref, k_ref, gc_ref, gcr_ref, b_ref, h, hd)
            sl, qh, kh, bh, e, r, decay = m['sl'], m['q'], m['k'], m['b'], m['e'], m['r'], m['decay']
            vh = v_ref[:, sl]
            dob, dvb = do_ref[:, sl].astype(BF16), dvn_ref[:, sl]
            ts = _dot(jnp.concatenate([dob, -dvb], axis=0), s_ref[0, h].astype(BF16), 1, 1)
            d_qe, d_w = ts[:c], ts[c:]
            d_qk = _dot(dob, vn_ref[:, sl], 1, 1)
            d_rhs = _dot(invt_ref[0, h], jnp.concatenate([dvb, d_w.astype(BF16)], axis=1), 1, 0)
            d_ru, d_rw = d_rhs[:, :hd], d_rhs[:, hd:]
            sol = jnp.concatenate([u_ref[:, sl].astype(BF16), wq_ref[0, h, :c, :]], axis=1)
            d_a = -_dot(d_rhs.astype(BF16), sol, 1, 1)
            dv_ref[:, sl] = d_ru * bh
            d_e = jnp.sum(d_rw * m['kb'], axis=1, keepdims=True) + jnp.sum(d_qe * qh, axis=1, keepdims=True)
            d_n = jnp.where(m['row'] > m['col'], d_a, 0.0)
            d_m, d_p = (d_n * decay).astype(BF16), (d_qk * decay).astype(BF16)
            tk = _dot(jnp.concatenate([d_m, d_p], axis=0), kh.astype(BF16), 1, 0)
            d_kb = d_rw * e + tk[:c]
            dq_ref[:, sl] = tk[c:] + d_qe * e
            dk = _dot(d_m, m['kb'].astype(BF16), 0, 0) + _dot(d_p, qh.astype(BF16), 0, 0)
            dd = (d_n * m['kkt'] + d_qk * m['qkt']) * decay
            d_kr = dkr_ref[:, sl]
            d_r = jnp.sum(d_kr * kh, axis=1, keepdims=True)
            d_gl = del_ref[0, h:h + 1, 0:1] * m['el'] + jnp.sum(d_r * r, axis=0, keepdims=True)
            last = lax.broadcasted_iota(jnp.int32, (c, 1), 0) == c - 1
            dgc_ref[:, h:h + 1] = jnp.sum(dd, axis=1, keepdims=True) + d_e * e - d_r * r + jnp.where(last, d_gl, 0.0)
            dgr_ref[0, h:h + 1, :] = -jnp.sum(dd, axis=0, keepdims=True)
            dk_ref[:, sl] = dk + d_kr * r + d_kb * bh
            db_ref[:, h:h + 1] = jnp.sum(d_ru * vh, axis=1, keepdims=True) + jnp.sum(d_kb * kh, axis=1, keepdims=True)

    rows = pl.BlockSpec((c, w), lambda n: (n, 0))
    cols = pl.BlockSpec((c, DN_HEADS), lambda n: (n, 0))
    rowg = pl.BlockSpec((1, DN_HEADS, c), lambda n: (n, 0, 0))
    per = lambda a, b: pl.BlockSpec((1, DN_HEADS, a, b), lambda n: (n, 0, 0, 0))
    return pl.pallas_call(
        body, name=name, grid=(nch,),
        in_specs=[rows, rows, rows, cols, rowg, cols, per(c, c), rows, per(2 * c, hd), rows, per(hd, hd), rows, rows,
                  rows, pl.BlockSpec((1, DN_HEADS, LANES), lambda n: (n, 0, 0))],
        out_specs=[rows, rows, rows, cols, rowg, cols],
        out_shape=[jax.ShapeDtypeStruct((t, w), F32)] * 3
        + [jax.ShapeDtypeStruct((t, DN_HEADS), F32), jax.ShapeDtypeStruct((nch, DN_HEADS, c), F32),
           jax.ShapeDtypeStruct((t, DN_HEADS), F32)],
        compiler_params=_params(("parallel",)),
    )(q, k, v, gc, gcr, beta, invt, u, wq, vn, states, do, dvn, dkr, d_el)


def _my_id():
    return 4 * lax.axis_index("x") + 2 * lax.axis_index("y") + lax.axis_index("c")


def _peer(k):
    x, y, c = lax.axis_index("x"), lax.axis_index("y"), lax.axis_index("c")
    flip = lambda v, bit: 1 - v if bit else v
    return flip(x, k & 4), flip(y, k & 2), flip(c, k & 1)


def _peer_id(k):
    x, y, c = _peer(k)
    return 4 * x + 2 * y + c


def _exchange(src, *, name, scatter):
    rows = src.shape[-2:]

    def body(src_ref, out_ref, send_sems, recv_sems, local_sem):
        me = _my_id()
        mine = src_ref.at[me] if scatter else src_ref
        local = pltpu.make_async_copy(mine, out_ref.at[me], local_sem)
        local.start()
        sends = []
        for k in range(1, N_DEV):
            cp = pltpu.make_async_remote_copy(
                src_ref=src_ref.at[_peer_id(k)] if scatter else src_ref, dst_ref=out_ref.at[me],
                send_sem=send_sems.at[k - 1], recv_sem=recv_sems.at[k - 1], device_id=_peer(k), device_id_type=MESH)
            cp.start()
            sends.append(cp)
        for k in range(1, N_DEV):
            pltpu.make_async_remote_copy(
                src_ref=mine, dst_ref=out_ref.at[_peer_id(k)], send_sem=send_sems.at[k - 1],
                recv_sem=recv_sems.at[k - 1], device_id=_peer(k), device_id_type=MESH).wait_recv()
        for cp in sends:
            cp.wait_send()
        local.wait()

    return pl.pallas_call(
        body, name=name, out_shape=jax.ShapeDtypeStruct((N_DEV,) + rows, src.dtype),
        in_specs=[pl.BlockSpec(memory_space=pl.ANY)], out_specs=pl.BlockSpec(memory_space=pl.ANY),
        scratch_shapes=[pltpu.SemaphoreType.DMA((N_DEV - 1,)), pltpu.SemaphoreType.DMA((N_DEV - 1,)),
                        pltpu.SemaphoreType.DMA(())],
    )(src)


def _sum_adamw(parts, w, m, v, *, name):
    _, r, c = parts.shape
    tr = _tile(r, 256, 8)
    c1, c2 = 1.0 - ADAM_B1 ** ADAM_STEP, 1.0 - ADAM_B2 ** ADAM_STEP

    def body(p_ref, w_ref, m_ref, v_ref, g_ref, d_ref, nm_ref, nv_ref):
        g = p_ref[0]
        for s in range(1, N_DEV):
            g = g + p_ref[s]
        nm = ADAM_B1 * m_ref[...] + (1.0 - ADAM_B1) * g
        nv = ADAM_B2 * v_ref[...] + (1.0 - ADAM_B2) * (g * g)
        g_ref[...] = g
        nm_ref[...] = nm
        nv_ref[...] = nv
        d_ref[...] = -ADAM_LR * ((nm / c1) / (jnp.sqrt(nv / c2) + ADAM_EPS) + ADAM_WD * w_ref[...])

    blk = pl.BlockSpec((tr, c), lambda i: (i, 0))
    return pl.pallas_call(
        body, name=name, grid=(r // tr,), in_specs=[pl.BlockSpec((N_DEV, tr, c), lambda i: (0, i, 0)), blk, blk, blk],
        out_specs=[blk] * 4, out_shape=[jax.ShapeDtypeStruct((r, c), F32)] * 4, compiler_params=_params(("parallel",)),
    )(parts, w, m, v)


def _pack_rows(n):
    return -(-n // (PACK_COLS * PACK_ROWS)) * PACK_ROWS


def _pack(blocks):
    parts, spans, at = [], [], 0
    for blk, n_lead in blocks:
        lead = blk.shape[:n_lead]
        n = math.prod(blk.shape[n_lead:])
        rows = _pack_rows(n)
        flat = blk.reshape(lead + (n,))
        flat = jnp.pad(flat, [(0, 0)] * n_lead + [(0, rows * PACK_COLS - n)])
        parts.append(flat.reshape(lead + (rows, PACK_COLS)))
        spans.append((at, rows))
        at += rows
    return jnp.concatenate(parts, axis=-2), spans


def _unpack(buf, span, shape):
    at, rows = span
    lead = buf.shape[:-2]
    flat = lax.slice_in_dim(buf, at, at + rows, axis=buf.ndim - 2).reshape(lead + (rows * PACK_COLS,))
    return lax.slice_in_dim(flat, 0, math.prod(shape), axis=len(lead)).reshape(lead + tuple(shape))


def _join_shards(g, axis):
    g = jnp.moveaxis(g, 0, axis)
    return g.reshape(g.shape[:axis] + (g.shape[axis] * g.shape[axis + 1],) + g.shape[axis + 2:])


def _split_shards(full, axis):
    s = full.shape
    g = full.reshape(s[:axis] + (N_DEV, s[axis] // N_DEV) + s[axis + 1:])
    return jnp.moveaxis(g, axis, 0)


def _ffn_fwd(x, g, w_in, w_out, tag):
    h = _rms_fwd(x, g, name=tag + "_norm", out_dtype=BF16)
    u = _mm(h, w_in, name=tag + "_in", tn=1408)
    a = _swiglu_fwd(u, name=tag + "_act")
    return _mm(a, w_out, name=tag + "_out", res=x, scale=0.5, tk=2816), (x, h, u, a)


def _ffn_bwd(saved, g, w_in, w_out, dxo, tag):
    x, h, u, a = saved
    d_w_out = _mm(a, dxo, name=tag + "_dwout", ta=True, scale=0.5, tm=1408)
    da = _mm(dxo, w_out, name=tag + "_da", tb=True, scale=0.5, tn=1408)
    du = _swiglu_bwd(u, da, name=tag + "_dact")
    d_w_in = _mm(h, du, name=tag + "_dwin", ta=True, tn=1408)
    dh = _mm(du, w_in, name=tag + "_dh", tb=True, tk=1408)
    dx, dg = _rms_bwd(x, g, dh, dxo, name=tag + "_dnorm")
    return dx, dg, d_w_in, d_w_out


def _block_diag(w):
    n, j, k = w.shape
    return (w[:, :, None, :] * jnp.eye(n, dtype=w.dtype)[:, None, :, None]).reshape(n * j, n * k)


def _diag_blocks(dense, n):
    j, k = dense.shape[0] // n, dense.shape[1] // n
    return jnp.stack([dense[i * j:(i + 1) * j, i * k:(i + 1) * k] for i in range(n)], axis=0)


def _attn_lru_fwd(x, p, tag):
    aw = ATTN_HEADS * 64
    h = _rms_fwd(x, p['mix_norm'], name=tag + "_norm", out_dtype=BF16)
    proj = _mm(h, p['ab_w_in'], name=tag + "_in", tn=1280)
    q, k, v = (proj[:, i * aw:(i + 1) * aw] for i in range(3))
    outs, lses = [], []
    for window, dil in DILATED_PATTERNS:
        assert window // dil == ATTN_BLOCK
        o, l = _dattn_fwd(q, k, v, dil, name=f"{tag}_attn{dil}")
        outs.append(o)
        lses.append(l)
    wa, wx = _block_diag(p['lru_w_a']).astype(BF16), _block_diag(p['lru_w_x']).astype(BF16)
    sp, sp_vjp = jax.vjp(lambda lam: LRU_C * jax.nn.softplus(-lam), p['lru_lambda'])
    a, b, xc = _lru_gates_fwd(proj, 3, p['lru_conv_w'], p['lru_conv_b'], wa, p['lru_b_a'], wx, p['lru_b_x'], sp,
                              name=tag + "_gates")
    hs = _scan(a, b, name=tag + "_scan")
    cat, attn, lse_all = _mix_join_fwd(outs, lses, hs, proj, 4, name=tag + "_join")
    xo = _mm(cat, p['ab_w_out'], name=tag + "_out", res=x)
    return xo, (x, h, proj, q, k, v, attn, lse_all, a, hs, xc, wa, wx, sp, sp_vjp, cat)


def _attn_lru_bwd(saved, p, dxo, tag):
    x, h, proj, q, k, v, attn, lse_all, a, hs, xc, wa, wx, sp, sp_vjp, cat = saved
    aw = attn.shape[1]
    g = {'ab_w_out': _mm(cat, dxo, name=tag + "_dwout", ta=True)}
    dcat = _mm(dxo, p['ab_w_out'], name=tag + "_dcat", tb=True)
    delta, dhs, dgr = _mix_join_bwd(dcat, attn, hs, proj, 4, name=tag + "_djoin")
    a_next = jnp.concatenate([a[1:], jnp.zeros_like(a[:1])], axis=0)
    dtot = _scan(a_next, dhs, name=tag + "_dscan", reverse=True)
    h_prev = jnp.concatenate([jnp.zeros_like(hs[:1]), hs[:-1]], axis=0)
    dxc, dwa, dwx, vecs = _lru_gates_bwd(xc, dtot, h_prev, wa, p['lru_b_a'], wx, p['lru_b_x'], sp, name=tag + "_dgates")
    dxr, g['lru_conv_w'] = _conv_bwd(dxc, proj, 3, p['lru_conv_w'], name=tag + "_dconv")
    g['lru_b_a'], g['lru_b_x'], g['lru_conv_b'] = vecs[0], vecs[1], vecs[3]
    g['lru_lambda'], = sp_vjp(vecs[2])
    g['lru_w_a'], g['lru_w_x'] = _diag_blocks(dwa, LRU_BLOCKS), _diag_blocks(dwx, LRU_BLOCKS)
    dattn = dcat[:, :aw]
    dq = dk = dv = 0.0
    for _, dil in DILATED_PATTERNS:
        dq_, dk_, dv_ = _dattn_bwd(q, k, v, dattn, lse_all, delta, dil, name=f"{tag}_dattn{dil}")
        dq, dk, dv = dq + dq_, dk + dk_, dv + dv_
    dproj = jnp.concatenate([dq.astype(BF16), dk.astype(BF16), dv.astype(BF16), dxr, dgr], axis=-1)
    g['ab_w_in'] = _mm(h, dproj, name=tag + "_dwin", ta=True, tn=1280)
    dh = _mm(dproj, p['ab_w_in'], name=tag + "_dh", tb=True, tk=1280)
    dx, g['mix_norm'] = _rms_bwd(x, p['mix_norm'], dh, dxo, name=tag + "_dnorm")
    return dx, g


def _dn_decay(a, b, a_log, dt_bias):
    t = a.shape[0]
    g = -jnp.exp(a_log) * jax.nn.softplus(a + dt_bias)
    gc = jnp.cumsum(g.reshape(t // DN_CHUNK, DN_CHUNK, DN_HEADS), axis=1)
    return gc.reshape(t, DN_HEADS), jnp.swapaxes(gc, 1, 2), jax.nn.sigmoid(b)


def _dn_in_width(w):
    return -(-(4 * w + 2 * DN_HEADS) // LANES) * LANES


def _deltanet_fwd(x, p, tag):
    w = p['dn_w_out'].shape[0]
    h = _rms_fwd(x, p['mix_norm'], name=tag + "_norm", out_dtype=BF16)
    proj = _mm(h, p['dn_w_in'], name=tag + "_in", tn=1408)
    q, k, v, c = _dn_prep_fwd(proj, p['dn_conv_w'], name=tag + "_prep")
    a, b = proj[:, 4 * w:4 * w + DN_HEADS], proj[:, 4 * w + DN_HEADS:4 * w + 2 * DN_HEADS]
    (gc, gcr, beta), decay_vjp = jax.vjp(_dn_decay, a, b, p['dn_a_log'], p['dn_dt_bias'])
    prep = (q, k, v, gc, gcr, beta)
    u, wq, wqt, kr, krt, qk, qkt, invt = _dn_local_fwd(*prep, name=tag + "_local")
    o, vn, states = _dn_state_fwd(u, wq, krt, qk, gc, name=tag + "_state")
    og = _dn_gate_fwd(o, proj, 3, p['dn_o_norm'], name=tag + "_gate")
    xo = _mm(og, p['dn_w_out'], name=tag + "_out", res=x)
    return xo, (x, h, proj, c, o, prep, (u, wq, wqt, kr, qkt, invt, vn, states), decay_vjp, og)


def _deltanet_bwd(saved, p, dxo, tag):
    x, h, proj, c, o, prep, (u, wq, wqt, kr, qkt, invt, vn, states), decay_vjp, og = saved
    g = {'dn_w_out': _mm(og, dxo, name=tag + "_dwout", ta=True)}
    dog = _mm(dxo, p['dn_w_out'], name=tag + "_dog", tb=True)
    do, dz, g['dn_o_norm'] = _dn_gate_bwd(o, proj, 3, p['dn_o_norm'], dog, name=tag + "_dgate")
    dvn, dkr, d_el = _dn_state_bwd(do, qkt, kr, vn, states, wqt, prep[3], name=tag + "_dstate")
    dq, dk, dv, dgc, dgr, dbeta = _dn_local_bwd(*prep, invt, u, wq, vn, states, do, dvn, dkr, d_el, name=tag + "_dlocal")
    dc = _dn_prep_bwd(c, dq, dk, dv, name=tag + "_dprep")
    dqkv, g['dn_conv_w'] = _conv_bwd(dc, proj, 0, p['dn_conv_w'], name=tag + "_dconv")
    da, db, g['dn_a_log'], g['dn_dt_bias'] = decay_vjp((dgc, dgr, dbeta))
    t = x.shape[0]
    pad = jnp.zeros((t, p['dn_w_in'].shape[1] - dqkv.shape[1] - dz.shape[1] - 2 * DN_HEADS), BF16)
    dproj = jnp.concatenate([dqkv, dz, da.astype(BF16), db.astype(BF16), pad], axis=-1)
    g['dn_w_in'] = _mm(h, dproj, name=tag + "_dwin", ta=True, tn=1408)
    dh = _mm(dproj, p['dn_w_in'], name=tag + "_dh", tb=True, tk=1408)
    dx, g['mix_norm'] = _rms_bwd(x, p['mix_norm'], dh, dxo, name=tag + "_dnorm")
    return dx, g


def _xattn_block_fwd(x, mem, p, tag):
    w = x.shape[1]
    h = _rms_fwd(x, p['xa_norm'], name=tag + "_norm", out_dtype=BF16)
    mh = _rms_fwd(mem, p['xa_mem_norm'], name=tag + "_mnorm", out_dtype=BF16)
    q = _mm(h, p['xa_wq'], name=tag + "_q")
    kv = _mm(mh, p['xa_wkv'], name=tag + "_kv")
    k, v = kv[:, :w], kv[:, w:]
    o = _xattn_fwd(q, k, v, name=tag + "_attn").astype(BF16)
    xo = _mm(o, p['xa_wo'], name=tag + "_out", res=x)
    return xo, (x, h, mh, q, k, v, o)


def _xattn_block_bwd(saved, mem, p, dxo, tag):
    x, h, mh, q, k, v, o = saved
    g = {'xa_wo': _mm(o, dxo, name=tag + "_dwo", ta=True)}
    do = _mm(dxo, p['xa_wo'], name=tag + "_do", tb=True)
    dq, dk, dv = _xattn_bwd(q, k, v, do, name=tag + "_dattn")
    dq = dq.astype(BF16)
    dkv = jnp.concatenate([dk, dv], axis=-1).astype(BF16)
    g['xa_wq'] = _mm(h, dq, name=tag + "_dwq", ta=True)
    g['xa_wkv'] = _mm(mh, dkv, name=tag + "_dwkv", ta=True)
    dmh = _mm(dkv, p['xa_wkv'], name=tag + "_dmh", tb=True)
    _, g['xa_mem_norm'] = _rms_bwd(mem, p['xa_mem_norm'], dmh, None, name=tag + "_dmnorm")
    dh = _mm(dq, p['xa_wq'], name=tag + "_dh", tb=True)
    dx, g['xa_norm'] = _rms_bwd(x, p['xa_norm'], dh, dxo, name=tag + "_dnorm")
    return dx, g


def _layer_params(full, layer):
    p = {n: full[n][layer] for n in ('ffn1_norm', 'ffn1_w_in', 'ffn1_w_out', 'mix_norm', 'xa_norm', 'xa_mem_norm',
                                     'xa_wq', 'xa_wkv', 'xa_wo', 'ffn2_norm', 'ffn2_w_in', 'ffn2_w_out')}
    mixer = ('ab_w_in', 'lru_conv_w', 'lru_conv_b', 'lru_w_a', 'lru_b_a', 'lru_w_x', 'lru_b_x', 'lru_lambda', 'ab_w_out') \
        if layer % 2 == 0 else ('dn_w_in', 'dn_conv_w', 'dn_a_log', 'dn_dt_bias', 'dn_o_norm', 'dn_w_out')
    p.update({n: full[n][layer // 2] for n in mixer})
    return p


def _step(x, mem, target, full, depth):
    saved = []
    for layer in range(depth):
        p = _layer_params(full, layer)
        tag = f"l{layer}"
        x, s1 = _ffn_fwd(x, p['ffn1_norm'], p['ffn1_w_in'], p['ffn1_w_out'], tag + "_ffn1")
        x, s2 = (_attn_lru_fwd if layer % 2 == 0 else _deltanet_fwd)(x, p, tag + "_mix")
        x, s3 = _xattn_block_fwd(x, mem, p, tag + "_xa")
        x, s4 = _ffn_fwd(x, p['ffn2_norm'], p['ffn2_w_in'], p['ffn2_w_out'], tag + "_ffn2")
        saved.append((p, s1, s2, s3, s4))
    sq, dx, d_final = _final_loss(x, full['final_norm'], target, name="final_loss")
    per_layer = []
    for layer in reversed(range(depth)):
        p, s1, s2, s3, s4 = saved[layer]
        tag = f"l{layer}"
        g = {}
        dx, g['ffn2_norm'], g['ffn2_w_in'], g['ffn2_w_out'] = _ffn_bwd(s4, p['ffn2_norm'], p['ffn2_w_in'], p['ffn2_w_out'], dx, tag + "_ffn2")
        dx, gx = _xattn_block_bwd(s3, mem, p, dx, tag + "_xa")
        dx, gm = (_attn_lru_bwd if layer % 2 == 0 else _deltanet_bwd)(s2, p, dx, tag + "_mix")
        dx, g['ffn1_norm'], g['ffn1_w_in'], g['ffn1_w_out'] = _ffn_bwd(s1, p['ffn1_norm'], p['ffn1_w_in'], p['ffn1_w_out'], dx, tag + "_ffn1")
        g.update(gx)
        g.update(gm)
        per_layer.insert(0, g)
    grads = {'final_norm': d_final}
    for n in WEIGHTS[:-1]:
        grads[n] = jnp.stack([g[n] for g in per_layer if n in g], axis=0)
    return sq, dx, grads


def kernel(x, mem, ffn1_norm, ffn1_w_in, ffn1_w_out, mix_norm, xa_norm, xa_mem_norm, xa_wq, xa_wkv, xa_wo, ffn2_norm, ffn2_w_in, ffn2_w_out, ab_w_in, lru_conv_w, lru_conv_b, lru_w_a, lru_b_a, lru_w_x, lru_b_x, lru_lambda, ab_w_out, dn_w_in, dn_conv_w, dn_a_log, dn_dt_bias, dn_o_norm, dn_w_out, final_norm, loss_target, m_ffn1_norm, m_ffn1_w_in, m_ffn1_w_out, m_mix_norm, m_xa_norm, m_xa_mem_norm, m_xa_wq, m_xa_wkv, m_xa_wo, m_ffn2_norm, m_ffn2_w_in, m_ffn2_w_out, m_ab_w_in, m_lru_conv_w, m_lru_conv_b, m_lru_w_a, m_lru_b_a, m_lru_w_x, m_lru_b_x, m_lru_lambda, m_ab_w_out, m_dn_w_in, m_dn_conv_w, m_dn_a_log, m_dn_dt_bias, m_dn_o_norm, m_dn_w_out, m_final_norm, v_ffn1_norm, v_ffn1_w_in, v_ffn1_w_out, v_mix_norm, v_xa_norm, v_xa_mem_norm, v_xa_wq, v_xa_wkv, v_xa_wo, v_ffn2_norm, v_ffn2_w_in, v_ffn2_w_out, v_ab_w_in, v_lru_conv_w, v_lru_conv_b, v_lru_w_a, v_lru_b_a, v_lru_w_x, v_lru_b_x, v_lru_lambda, v_ab_w_out, v_dn_w_in, v_dn_conv_w, v_dn_a_log, v_dn_dt_bias, v_dn_o_norm, v_dn_w_out, v_final_norm):
    args = dict(locals())
    local = {n: args[n] for n in WEIGHTS}
    depth = ffn1_norm.shape[0]

    big = [n for n in WEIGHTS if n in SHARD_AXIS and n not in GATHER_F32]
    send16, spans16 = _pack([(local[n].astype(BF16), 0) for n in big])
    send32, spans32 = _pack([(local[n], 0) for n in GATHER_F32])
    got16 = _exchange(send16, name="gather_matrices", scatter=False)
    got32 = _exchange(send32, name="gather_filters", scatter=False)
    full = dict(local)
    for names, got, spans in ((big, got16, spans16), (GATHER_F32, got32, spans32)):
        for n, span in zip(names, spans):
            full[n] = _join_shards(_unpack(got, span, local[n].shape), SHARD_AXIS[n] + 1 - 1)
    dn_cols = full['dn_w_in'].shape[2]
    full['dn_w_in'] = jnp.pad(full['dn_w_in'], ((0, 0), (0, 0), (0, _dn_in_width(full['dn_w_out'].shape[1]) - dn_cols)))

    sq, dx, grads = _step(x[0], mem[0], loss_target[0], full, depth)
    grads['dn_w_in'] = grads['dn_w_in'][:, :, :dn_cols]
    loss = lax.psum(0.5 * jnp.sum(sq) / x.shape[2], ("x", "y", "c"))

    contrib = [(_split_shards(grads[n], SHARD_AXIS[n]) if n in SHARD_AXIS
                else jnp.broadcast_to(grads[n], (N_DEV,) + grads[n].shape), 1) for n in WEIGHTS]
    send, spans = _pack(contrib)
    parts = _exchange(send, name="scatter_grads", scatter=True)
    w_pack, _ = _pack([(local[n], 0) for n in WEIGHTS])
    m_pack, _ = _pack([(args["m_" + n], 0) for n in WEIGHTS])
    v_pack, _ = _pack([(args["v_" + n], 0) for n in WEIGHTS])
    outs = _sum_adamw(parts, w_pack, m_pack, v_pack, name="sum_adamw")
    grad_w, delta_w, new_m, new_v = ([_unpack(o, span, local[n].shape) for n, span in zip(WEIGHTS, spans)] for o in outs)
    return (loss, dx[None], *grad_w, *delta_w, *new_m, *new_v)
```

```python
import math

import jax
import jax.numpy as jnp
from jax import lax
from jax.experimental import pallas as pl
from jax.experimental.pallas import tpu as pltpu

F32, BF16 = jnp.float32, jnp.bfloat16
MESH = pl.DeviceIdType.MESH
N_DEV = 8
V7X_VMEM_LIMIT = 56 << 20
LANES = 128
PACK_COLS = 1024
PACK_ROWS = 16
HI = lax.Precision.HIGHEST
NEG = -1e30

NORM_EPS = 1e-6
CONV_K = 4
ATTN_HEADS = 8
DILATED_PATTERNS = ((128, 1), (512, 4), (2048, 16))
ATTN_BLOCK = 128
LRU_BLOCKS = 8
LRU_C = 8.0
DN_HEADS = 8
DN_CHUNK = 64
XA_HEADS = 4
ADAM_LR, ADAM_B1, ADAM_B2, ADAM_EPS, ADAM_WD, ADAM_STEP = 0.001, 0.9, 0.999, 1e-08, 0.01, 10

WEIGHTS = ['ffn1_norm', 'ffn1_w_in', 'ffn1_w_out', 'mix_norm', 'xa_norm', 'xa_mem_norm', 'xa_wq', 'xa_wkv', 'xa_wo',
           'ffn2_norm', 'ffn2_w_in', 'ffn2_w_out', 'ab_w_in', 'lru_conv_w', 'lru_conv_b', 'lru_w_a', 'lru_b_a',
           'lru_w_x', 'lru_b_x', 'lru_lambda', 'ab_w_out', 'dn_w_in', 'dn_conv_w', 'dn_a_log', 'dn_dt_bias',
           'dn_o_norm', 'dn_w_out', 'final_norm']
SHARD_AXIS = {'ffn1_w_in': 2, 'ffn1_w_out': 1, 'xa_wq': 1, 'xa_wkv': 2, 'xa_wo': 1, 'ffn2_w_in': 2, 'ffn2_w_out': 1,
              'ab_w_in': 2, 'lru_conv_w': 2, 'ab_w_out': 1, 'dn_w_in': 2, 'dn_conv_w': 2, 'dn_w_out': 1}
GATHER_F32 = ('lru_conv_w', 'dn_conv_w')
TRANSPOSED = ('dn_w_in',)


def _params(sem=None):
    return pltpu.CompilerParams(dimension_semantics=sem, vmem_limit_bytes=V7X_VMEM_LIMIT)


def _tile(n, pref, mult):
    best = None
    t = mult
    while t <= min(n, pref):
        if n % t == 0:
            best = t
        t += mult
    return n if best is None else best


def _dot(a, b, ca, cb, prec=None):
    return lax.dot_general(a, b, (((ca,), (cb,)), ((), ())), preferred_element_type=F32, precision=prec)


def _mm(a, b, *, name, ta=False, tb=False, out_dtype=F32, res=None, scale=1.0, tm=1024, tn=1024, tk=1024):
    m, kdim = (a.shape[1], a.shape[0]) if ta else a.shape
    n = b.shape[0] if tb else b.shape[1]
    assert (b.shape[1] if tb else b.shape[0]) == kdim
    tm = _tile(m, tm, LANES if ta else 16)
    tn = _tile(n, tn, LANES)
    tk = _tile(kdim, tk, LANES)
    nk = kdim // tk
    a_spec = pl.BlockSpec((tk, tm), lambda i, j, k: (k, i)) if ta else pl.BlockSpec((tm, tk), lambda i, j, k: (i, k))
    b_spec = pl.BlockSpec((tn, tk), lambda i, j, k: (j, k)) if tb else pl.BlockSpec((tk, tn), lambda i, j, k: (k, j))
    o_spec = pl.BlockSpec((tm, tn), lambda i, j, k: (i, j))
    ca, cb = (0 if ta else 1), (1 if tb else 0)
    has_res = res is not None

    def finish(acc, r_ref, o_ref):
        y = acc if scale == 1.0 else acc * scale
        if has_res:
            y = y + r_ref[...]
        o_ref[...] = y.astype(out_dtype)

    def body(*refs):
        a_ref, b_ref = refs[0], refs[1]
        r_ref = refs[2] if has_res else None
        o_ref = refs[2 + has_res]
        p = _dot(a_ref[...].astype(BF16), b_ref[...].astype(BF16), ca, cb)
        if nk == 1:
            finish(p, r_ref, o_ref)
            return
        acc = refs[3 + has_res]
        k = pl.program_id(2)

        @pl.when(k == 0)
        def _():
            acc[...] = p

        @pl.when(k > 0)
        def _():
            acc[...] += p

        @pl.when(k == nk - 1)
        def _():
            finish(acc[...], r_ref, o_ref)

    ins, specs = [a, b], [a_spec, b_spec]
    if has_res:
        ins.append(res)
        specs.append(o_spec)
    return pl.pallas_call(
        body, name=name, grid=(m // tm, n // tn, nk), in_specs=specs, out_specs=o_spec,
        out_shape=jax.ShapeDtypeStruct((m, n), out_dtype),
        scratch_shapes=[] if nk == 1 else [pltpu.VMEM((tm, tn), F32)],
        compiler_params=_params(("parallel", "parallel", "arbitrary")),
    )(*ins)


def _rms_fwd(x, g, *, name, out_dtype):
    r, d = x.shape
    tr = _tile(r, 512, 16)

    def body(x_ref, g_ref, o_ref):
        xv = x_ref[...]
        rs = lax.rsqrt(jnp.mean(xv * xv, axis=-1, keepdims=True) + NORM_EPS)
        o_ref[...] = (xv * rs * g_ref[...]).astype(out_dtype)

    return pl.pallas_call(
        body, name=name, grid=(r // tr,),
        in_specs=[pl.BlockSpec((tr, d), lambda i: (i, 0)), pl.BlockSpec((1, d), lambda i: (0, 0))],
        out_specs=pl.BlockSpec((tr, d), lambda i: (i, 0)),
        out_shape=jax.ShapeDtypeStruct((r, d), out_dtype), compiler_params=_params(("parallel",)),
    )(x, g.reshape(1, d))


def _rms_bwd(x, g, dh, dres, *, name):
    r, d = x.shape
    tr = _tile(r, 512, 8)
    has_res = dres is not None

    def body(*refs):
        x_ref, g_ref, dh_ref = refs[:3]
        r_ref = refs[3] if has_res else None
        dx_ref, dg_ref = refs[3 + has_res], refs[4 + has_res]
        xv = x_ref[...]
        rs = lax.rsqrt(jnp.mean(xv * xv, axis=-1, keepdims=True) + NORM_EPS)
        xh = xv * rs
        dhv = dh_ref[...]
        dgh = dhv * g_ref[...]
        dx = rs * (dgh - xh * jnp.mean(dgh * xh, axis=-1, keepdims=True))
        if has_res:
            dx = dx + r_ref[...]
        dx_ref[...] = dx
        part = jnp.sum(dhv * xh, axis=0, keepdims=True)

        @pl.when(pl.program_id(0) == 0)
        def _():
            dg_ref[...] = part

        @pl.when(pl.program_id(0) > 0)
        def _():
            dg_ref[...] += part

    row = pl.BlockSpec((tr, d), lambda i: (i, 0))
    vec = pl.BlockSpec((1, d), lambda i: (0, 0))
    ins, specs = [x, g.reshape(1, d), dh], [row, vec, row]
    if has_res:
        ins.append(dres)
        specs.append(row)
    dx, dg = pl.pallas_call(
        body, name=name, grid=(r // tr,), in_specs=specs, out_specs=[row, vec],
        out_shape=[jax.ShapeDtypeStruct((r, d), F32), jax.ShapeDtypeStruct((1, d), F32)],
        compiler_params=_params(("arbitrary",)),
    )(*ins)
    return dx, dg.reshape(d)


def _mm_swiglu(h, w_in, *, name, tm=1024, tn=256):
    t, d = h.shape
    f = w_in.shape[1] // 2
    tm, tn = _tile(t, tm, 16), _tile(f, tn, LANES)
    nj = f // tn

    def body(h_ref, wg_ref, wu_ref, g_ref, u_ref, a_ref):
        hv = h_ref[...]
        gate, up = _dot(hv, wg_ref[...], 1, 0), _dot(hv, wu_ref[...], 1, 0)
        g_ref[...] = gate
        u_ref[...] = up
        a_ref[...] = (gate * jax.nn.sigmoid(gate) * up).astype(BF16)

    out = pl.BlockSpec((tm, tn), lambda i, j: (i, j))
    return pl.pallas_call(
        body, name=name, grid=(t // tm, nj),
        in_specs=[pl.BlockSpec((tm, d), lambda i, j: (i, 0)), pl.BlockSpec((d, tn), lambda i, j: (0, j)),
                  pl.BlockSpec((d, tn), lambda i, j: (0, j + nj))],
        out_specs=[out, out, out],
        out_shape=[jax.ShapeDtypeStruct((t, f), F32), jax.ShapeDtypeStruct((t, f), F32), jax.ShapeDtypeStruct((t, f), BF16)],
        compiler_params=_params(("parallel", "parallel")),
    )(h, w_in, w_in)


def _mm_dswiglu(dy, w_out, gate, up, *, name, scale, tm=1024, tn=256):
    t, d = dy.shape
    f = w_out.shape[0]
    tm, tn = _tile(t, tm, 16), _tile(f, tn, LANES)

    def body(dy_ref, w_ref, g_ref, u_ref, dg_ref, du_ref):
        da = _dot(dy_ref[...].astype(BF16), w_ref[...], 1, 1) * scale
        gv = g_ref[...]
        s = jax.nn.sigmoid(gv)
        dg_ref[...] = (da * u_ref[...] * (s * (1.0 + gv * (1.0 - s)))).astype(BF16)
        du_ref[...] = (da * gv * s).astype(BF16)

    blk = pl.BlockSpec((tm, tn), lambda i, j: (i, j))
    return pl.pallas_call(
        body, name=name, grid=(t // tm, f // tn),
        in_specs=[pl.BlockSpec((tm, d), lambda i, j: (i, 0)), pl.BlockSpec((tn, d), lambda i, j: (j, 0)), blk, blk],
        out_specs=[blk, blk], out_shape=[jax.ShapeDtypeStruct((t, f), BF16)] * 2,
        compiler_params=_params(("parallel", "parallel")),
    )(dy, w_out, gate, up)


def _final_loss(x, g, target, *, name):
    r, d = x.shape
    tr = _tile(r, 512, 8)

    def body(x_ref, g_ref, t_ref, sq_ref, dx_ref, dg_ref):
        xv = x_ref[...]
        gv = g_ref[...]
        rs = lax.rsqrt(jnp.mean(xv * xv, axis=-1, keepdims=True) + NORM_EPS)
        xh = xv * rs
        err = xh * gv - t_ref[...]
        dy = err * (1.0 / d)
        dgh = dy * gv
        dx_ref[...] = rs * (dgh - xh * jnp.mean(dgh * xh, axis=-1, keepdims=True))
        sq = jnp.sum(err * err, axis=0, keepdims=True)
        part = jnp.sum(dy * xh, axis=0, keepdims=True)

        @pl.when(pl.program_id(0) == 0)
        def _():
            sq_ref[...] = sq
            dg_ref[...] = part

        @pl.when(pl.program_id(0) > 0)
        def _():
            sq_ref[...] += sq
            dg_ref[...] += part

    row = pl.BlockSpec((tr, d), lambda i: (i, 0))
    vec = pl.BlockSpec((1, d), lambda i: (0, 0))
    sq, dx, dg = pl.pallas_call(
        body, name=name, grid=(r // tr,), in_specs=[row, vec, row], out_specs=[vec, row, vec],
        out_shape=[jax.ShapeDtypeStruct((1, d), F32), jax.ShapeDtypeStruct((r, d), F32),
                   jax.ShapeDtypeStruct((1, d), F32)],
        compiler_params=_params(("arbitrary",)),
    )(x, g.reshape(1, d), target)
    return sq.reshape(d), dx, dg.reshape(d)


def _scan(a, b, *, name, reverse=False):
    t, w = a.shape
    tb = _tile(t, 1024, 8)
    nblk, ngrp = t // tb, tb // 8

    def body(a_ref, b_ref, h_ref, carry):
        @pl.when(pl.program_id(0) == 0)
        def _():
            carry[...] = jnp.zeros_like(carry)

        row = lax.broadcasted_iota(jnp.int32, (8, w), 0)

        def group(i, c):
            r0 = pl.multiple_of((ngrp - 1 - i if reverse else i) * 8, 8)
            av, bv = a_ref[pl.ds(r0, 8), :], b_ref[pl.ds(r0, 8), :]
            for s in (1, 2, 4):
                keep = row < 8 - s if reverse else row >= s
                shift = 8 - s if reverse else s
                bv = jnp.where(keep, bv + av * pltpu.roll(bv, shift, axis=0), bv)
                av = jnp.where(keep, av * pltpu.roll(av, shift, axis=0), av)
            hv = bv + av * c
            h_ref[pl.ds(r0, 8), :] = hv
            return hv[0:1, :] if reverse else hv[7:8, :]

        carry[0:1, :] = lax.fori_loop(0, ngrp, group, carry[0:1, :])

    blk = pl.BlockSpec((tb, w), lambda i: (nblk - 1 - i if reverse else i, 0))
    return pl.pallas_call(
        body, name=name, grid=(t // tb,), in_specs=[blk, blk], out_specs=blk,
        out_shape=jax.ShapeDtypeStruct((t, w), F32), scratch_shapes=[pltpu.VMEM((8, w), F32)],
        compiler_params=_params(("arbitrary",)),
    )(a, b)


def _row_specs(t, tb, width, col):
    per = tb // 8
    main = pl.BlockSpec((tb, width), lambda i: (i, col))
    before = pl.BlockSpec((8, width), lambda i: (jnp.maximum(i * per - 1, 0), col))
    after = pl.BlockSpec((8, width), lambda i: (jnp.minimum((i + 1) * per, t // 8 - 1), col))
    return main, before, after


def _with_rows_before(x_ref, before_ref):
    return jnp.concatenate([jnp.where(pl.program_id(0) > 0, before_ref[...], 0.0), x_ref[...]], axis=0)


def _tap(xe, s):
    return xe[8:] if s == 0 else pltpu.roll(xe, s, axis=0)[8:]


def _conv_rows(xe, w_ref):
    return sum(_tap(xe, CONV_K - 1 - k) * w_ref[k:k + 1, :] for k in range(CONV_K))


def _conv_bwd(dy, x, col, w, *, name):
    t, width = dy.shape
    tb = _tile(t, 256, 8)
    nblk = t // tb

    def body(dy_ref, dy_after_ref, x_ref, x_before_ref, w_ref, dx_ref, dw_ref):
        i = pl.program_id(0)
        dyv = dy_ref[...]
        dye = jnp.concatenate([dyv, jnp.where(i < nblk - 1, dy_after_ref[...], 0.0)], axis=0)
        dx = dyv * w_ref[CONV_K - 1:CONV_K, :]
        for s in range(1, CONV_K):
            dx = dx + pltpu.roll(dye, tb + 8 - s, axis=0)[:tb] * w_ref[CONV_K - 1 - s:CONV_K - s, :]
        dx_ref[...] = dx.astype(BF16)
        xe = _with_rows_before(x_ref, x_before_ref)

        @pl.when(i == 0)
        def _():
            dw_ref[...] = jnp.zeros_like(dw_ref)

        for k in range(CONV_K):
            dw_ref[k:k + 1, :] += jnp.sum(dyv * _tap(xe, CONV_K - 1 - k), axis=0, keepdims=True)

    main, _, after = _row_specs(t, tb, width, 0)
    xmain, xbefore, _ = _row_specs(t, tb, width, col)
    dx, dw = pl.pallas_call(
        body, name=name, grid=(nblk,),
        in_specs=[main, after, xmain, xbefore, pl.BlockSpec((CONV_K, width), lambda i: (0, 0))],
        out_specs=[main, pl.BlockSpec((8, width), lambda i: (0, 0))],
        out_shape=[jax.ShapeDtypeStruct((t, width), BF16), jax.ShapeDtypeStruct((8, width), F32)],
        compiler_params=_params(("arbitrary",)),
    )(dy, dy, x, x, w)
    return dx, dw[:CONV_K]


def _expm1(x):
    small = x * (1.0 + x * (0.5 + x * (1.0 / 6.0 + x * (1.0 / 24.0 + x * (1.0 / 120.0 + x * (1.0 / 720.0))))))
    return jnp.where(jnp.abs(x) < 0.1, small, jnp.exp(x) - 1.0)


def _lru_gate_terms(xc, wa_ref, ba_ref, wx_ref, bx_ref, sp_ref):
    xb = xc.astype(BF16)
    r = jax.nn.sigmoid(_dot(xb, wa_ref[...], 1, 0) + ba_ref[...])
    i = jax.nn.sigmoid(_dot(xb, wx_ref[...], 1, 0) + bx_ref[...])
    log_a = -r * sp_ref[...]
    return r, i, jnp.exp(log_a), jnp.sqrt(-_expm1(2.0 * log_a))


def _lru_gates_fwd(proj, col, conv_w, conv_b, wa, ba, wx, bx, sp, *, name):
    t = proj.shape[0]
    width = conv_w.shape[1]
    tb = _tile(t, 512, 8)

    def body(x_ref, x_before_ref, cw_ref, cb_ref, wa_ref, ba_ref, wx_ref, bx_ref, sp_ref, a_ref, b_ref, xc_ref):
        xc = _conv_rows(_with_rows_before(x_ref, x_before_ref), cw_ref) + cb_ref[...]
        r, i, a, mult = _lru_gate_terms(xc, wa_ref, ba_ref, wx_ref, bx_ref, sp_ref)
        a_ref[...] = a
        b_ref[...] = mult * i * xc
        xc_ref[...] = xc

    main, before, _ = _row_specs(t, tb, width, col)
    out = pl.BlockSpec((tb, width), lambda i: (i, 0))
    vec = pl.BlockSpec((1, width), lambda i: (0, 0))
    mat = pl.BlockSpec((width, width), lambda i: (0, 0))
    return pl.pallas_call(
        body, name=name, grid=(t // tb,),
        in_specs=[main, before, pl.BlockSpec((CONV_K, width), lambda i: (0, 0)), vec, mat, vec, mat, vec, vec],
        out_specs=[out] * 3, out_shape=[jax.ShapeDtypeStruct((t, width), F32)] * 3,
        compiler_params=_params(("parallel",)),
    )(proj, proj, conv_w, conv_b.reshape(1, -1), wa, ba.reshape(1, -1), wx, bx.reshape(1, -1), sp.reshape(1, -1))


def _lru_gates_bwd(xc, dtot, h_prev, wa, ba, wx, bx, sp, *, name):
    t, width = xc.shape
    tb = _tile(t, 512, 8)

    def body(xc_ref, dt_ref, hp_ref, wa_ref, ba_ref, wx_ref, bx_ref, sp_ref, dxc_ref, dwa_ref, dwx_ref, vec_ref):
        xc = xc_ref[...]
        r, i, a, mult = _lru_gate_terms(xc, wa_ref, ba_ref, wx_ref, bx_ref, sp_ref)
        db = dt_ref[...]
        d_la = db * hp_ref[...] * a - db * i * xc * (a * a / mult)
        d_pa = (-d_la * sp_ref[...]) * r * (1.0 - r)
        d_pi = db * mult * xc * i * (1.0 - i)
        dab, dib = d_pa.astype(BF16), d_pi.astype(BF16)
        dxc = db * mult * i + _dot(dab, wa_ref[...], 1, 1) + _dot(dib, wx_ref[...], 1, 1)
        dxc_ref[...] = dxc
        xb = xc.astype(BF16)
        rows = [jnp.sum(z, axis=0, keepdims=True) for z in (d_pa, d_pi, -d_la * r, dxc)]

        @pl.when(pl.program_id(0) == 0)
        def _():
            dwa_ref[...] = jnp.zeros_like(dwa_ref)
            dwx_ref[...] = jnp.zeros_like(dwx_ref)
            vec_ref[...] = jnp.zeros_like(vec_ref)

        dwa_ref[...] += _dot(xb, dab, 0, 0)
        dwx_ref[...] += _dot(xb, dib, 0, 0)
        for j, z in enumerate(rows):
            vec_ref[j:j + 1, :] += z

    blk = pl.BlockSpec((tb, width), lambda i: (i, 0))
    vec = pl.BlockSpec((1, width), lambda i: (0, 0))
    mat = pl.BlockSpec((width, width), lambda i: (0, 0))
    dxc, dwa, dwx, vecs = pl.pallas_call(
        body, name=name, grid=(t // tb,), in_specs=[blk, blk, blk, mat, vec, mat, vec, vec],
        out_specs=[blk, mat, mat, pl.BlockSpec((8, width), lambda i: (0, 0))],
        out_shape=[jax.ShapeDtypeStruct((t, width), F32), jax.ShapeDtypeStruct((width, width), F32),
                   jax.ShapeDtypeStruct((width, width), F32), jax.ShapeDtypeStruct((8, width), F32)],
        compiler_params=_params(("arbitrary",)),
    )(xc, dtot, h_prev, wa, ba.reshape(1, -1), wx, bx.reshape(1, -1), sp.reshape(1, -1))
    return dxc, dwa, dwx, vecs[:4]


GELU_C = math.sqrt(2.0 / math.pi)


def _gelu_terms(x):
    th = jnp.tanh(GELU_C * (x + 0.044715 * x * x * x))
    return 0.5 * x * (1.0 + th), 0.5 * (1.0 + th) + 0.5 * x * (1.0 - th * th) * GELU_C * (1.0 + 3 * 0.044715 * x * x)


def _mix_join_fwd(outs, lses, hs, proj, gr_col, *, name):
    t, w = hs.shape
    tb = _tile(t, 512, 16)
    n = len(outs)

    def body(*refs):
        o_refs, l_refs = refs[:n], refs[n:2 * n]
        hs_ref, gr_ref, cat_ref, attn_ref, lse_ref = refs[2 * n:]
        ls = [r[...] for r in l_refs]
        m = ls[0]
        for l in ls[1:]:
            m = jnp.maximum(m, l)
        ws = [jnp.exp(l - m) for l in ls]
        den = sum(ws)
        attn = sum(wt * r[...] for wt, r in zip(ws, o_refs)) / den
        attn_ref[...] = attn
        lse_ref[...] = m + jnp.log(den)
        cat_ref[:, :w] = attn.astype(BF16)
        cat_ref[:, w:] = (hs_ref[...] * _gelu_terms(gr_ref[...])[0]).astype(BF16)

    blk = pl.BlockSpec((tb, w), lambda i: (i, 0))
    return pl.pallas_call(
        body, name=name, grid=(t // tb,),
        in_specs=[blk] * (2 * n + 1) + [pl.BlockSpec((tb, w), lambda i: (i, gr_col))],
        out_specs=[pl.BlockSpec((tb, 2 * w), lambda i: (i, 0)), blk, blk],
        out_shape=[jax.ShapeDtypeStruct((t, 2 * w), BF16), jax.ShapeDtypeStruct((t, w), F32),
                   jax.ShapeDtypeStruct((t, w), F32)],
        compiler_params=_params(("parallel",)),
    )(*outs, *lses, hs, proj)


def _mix_join_bwd(dcat, attn, hs, proj, gr_col, *, name):
    t, w = hs.shape
    hd = w // ATTN_HEADS
    tb = _tile(t, 512, 16)

    def body(dcat_ref, attn_ref, hs_ref, gr_ref, delta_ref, dhs_ref, dgr_ref):
        for h in range(ATTN_HEADS):
            sl = slice(h * hd, (h + 1) * hd)
            d = jnp.sum(dcat_ref[:, sl] * attn_ref[:, sl], axis=-1, keepdims=True)
            delta_ref[:, sl] = jnp.broadcast_to(d, (tb, hd))
        dy = dcat_ref[:, w:]
        g, dg = _gelu_terms(gr_ref[...])
        dhs_ref[...] = dy * g
        dgr_ref[...] = (dy * hs_ref[...] * dg).astype(BF16)

    blk = pl.BlockSpec((tb, w), lambda i: (i, 0))
    return pl.pallas_call(
        body, name=name, grid=(t // tb,),
        in_specs=[pl.BlockSpec((tb, 2 * w), lambda i: (i, 0)), blk, blk, pl.BlockSpec((tb, w), lambda i: (i, gr_col))],
        out_specs=[blk, blk, blk],
        out_shape=[jax.ShapeDtypeStruct((t, w), F32), jax.ShapeDtypeStruct((t, w), F32),
                   jax.ShapeDtypeStruct((t, w), BF16)],
        compiler_params=_params(("parallel",)),
    )(dcat, attn, hs, proj)


def _silu_terms(x):
    s = jax.nn.sigmoid(x)
    return x * s, s * (1.0 + x * (1.0 - s))


def _dn_prep_fwd(proj, conv_w, *, name):
    t = proj.shape[0]
    w3 = conv_w.shape[1]
    w = w3 // 3
    hd = w // DN_HEADS
    tb = _tile(t, 256, 8)

    def body(x_ref, x_before_ref, cw_ref, q_ref, k_ref, v_ref, c_ref):
        c = _conv_rows(_with_rows_before(x_ref, x_before_ref), cw_ref)
        c_ref[...] = c
        s = _silu_terms(c)[0]
        v_ref[...] = s[:, 2 * w:]
        for part, ref, scale in ((0, q_ref, hd ** -0.5), (1, k_ref, 1.0)):
            for h in range(DN_HEADS):
                z = s[:, part * w + h * hd:part * w + (h + 1) * hd]
                ref[:, h * hd:(h + 1) * hd] = z * (lax.rsqrt(jnp.sum(z * z, axis=-1, keepdims=True) + 1e-6) * scale)

    main, before, _ = _row_specs(t, tb, w3, 0)
    out = pl.BlockSpec((tb, w), lambda i: (i, 0))
    return pl.pallas_call(
        body, name=name, grid=(t // tb,), in_specs=[main, before, pl.BlockSpec((CONV_K, w3), lambda i: (0, 0))],
        out_specs=[out, out, out, pl.BlockSpec((tb, w3), lambda i: (i, 0))],
        out_shape=[jax.ShapeDtypeStruct((t, w), F32)] * 3 + [jax.ShapeDtypeStruct((t, w3), F32)],
        compiler_params=_params(("parallel",)),
    )(proj, proj, conv_w)


def _dn_prep_bwd(c, dq, dk, dv, *, name):
    t, w3 = c.shape
    w = w3 // 3
    hd = w // DN_HEADS
    tb = _tile(t, 256, 8)

    def body(c_ref, dq_ref, dk_ref, dv_ref, dc_ref):
        cv = c_ref[...]
        s, ds = _silu_terms(cv)
        dc_ref[:, 2 * w:] = dv_ref[...] * ds[:, 2 * w:]
        for part, ref, scale in ((0, dq_ref, hd ** -0.5), (1, dk_ref, 1.0)):
            for h in range(DN_HEADS):
                cols = slice(part * w + h * hd, part * w + (h + 1) * hd)
                z = s[:, cols]
                rn = lax.rsqrt(jnp.sum(z * z, axis=-1, keepdims=True) + 1e-6)
                y = z * rn
                dy = ref[:, h * hd:(h + 1) * hd] * scale
                dc_ref[:, cols] = rn * (dy - y * jnp.sum(dy * y, axis=-1, keepdims=True)) * ds[:, cols]

    blk = pl.BlockSpec((tb, w), lambda i: (i, 0))
    wide = pl.BlockSpec((tb, w3), lambda i: (i, 0))
    return pl.pallas_call(
        body, name=name, grid=(t // tb,), in_specs=[wide, blk, blk, blk], out_specs=wide,
        out_shape=jax.ShapeDtypeStruct((t, w3), F32), compiler_params=_params(("parallel",)),
    )(c, dq, dk, dv)


def _dn_gate_fwd(o, proj, z_col, o_norm, *, name):
    t, w = o.shape
    hd = w // DN_HEADS
    tb = _tile(t, 512, 16)

    def body(o_ref, z_ref, g_ref, y_ref):
        for h in range(DN_HEADS):
            sl = slice(h * hd, (h + 1) * hd)
            ov = o_ref[:, sl]
            rn = lax.rsqrt(jnp.mean(ov * ov, axis=-1, keepdims=True) + NORM_EPS)
            y_ref[:, sl] = (ov * rn * g_ref[...] * _silu_terms(z_ref[:, sl])[0]).astype(BF16)

    blk = pl.BlockSpec((tb, w), lambda i: (i, 0))
    return pl.pallas_call(
        body, name=name, grid=(t // tb,),
        in_specs=[blk, pl.BlockSpec((tb, w), lambda i: (i, z_col)), pl.BlockSpec((1, hd), lambda i: (0, 0))],
        out_specs=blk, out_shape=jax.ShapeDtypeStruct((t, w), BF16), compiler_params=_params(("parallel",)),
    )(o, proj, o_norm.reshape(1, hd))


def _dn_gate_bwd(o, proj, z_col, o_norm, dy, *, name):
    t, w = o.shape
    hd = w // DN_HEADS
    tb = _tile(t, 512, 16)

    def body(o_ref, z_ref, g_ref, dy_ref, do_ref, dz_ref, dg_ref):
        gv = g_ref[...]
        dg = jnp.zeros((1, hd), F32)
        for h in range(DN_HEADS):
            sl = slice(h * hd, (h + 1) * hd)
            ov, dyv = o_ref[:, sl], dy_ref[:, sl]
            sz, dsz = _silu_terms(z_ref[:, sl])
            rn = lax.rsqrt(jnp.mean(ov * ov, axis=-1, keepdims=True) + NORM_EPS)
            nv = ov * rn
            dz_ref[:, sl] = (dyv * nv * gv * dsz).astype(BF16)
            dn = dyv * gv * sz
            do_ref[:, sl] = rn * (dn - nv * jnp.mean(dn * nv, axis=-1, keepdims=True))
            dg = dg + jnp.sum(dyv * nv * sz, axis=0, keepdims=True)

        @pl.when(pl.program_id(0) == 0)
        def _():
            dg_ref[...] = dg

        @pl.when(pl.program_id(0) > 0)
        def _():
            dg_ref[...] += dg

    blk = pl.BlockSpec((tb, w), lambda i: (i, 0))
    vec = pl.BlockSpec((1, hd), lambda i: (0, 0))
    do, dz, dg = pl.pallas_call(
        body, name=name, grid=(t // tb,), in_specs=[blk, pl.BlockSpec((tb, w), lambda i: (i, z_col)), vec, blk],
        out_specs=[blk, blk, vec],
        out_shape=[jax.ShapeDtypeStruct((t, w), F32), jax.ShapeDtypeStruct((t, w), BF16),
                   jax.ShapeDtypeStruct((1, hd), F32)],
        compiler_params=_params(("arbitrary",)),
    )(o, proj, o_norm.reshape(1, hd), dy)
    return do, dz, dg.reshape(hd)


def _band_masks(n):
    qi = lax.broadcasted_iota(jnp.int32, (ATTN_BLOCK, ATTN_BLOCK), 0)
    kj = lax.broadcasted_iota(jnp.int32, (ATTN_BLOCK, ATTN_BLOCK), 1)
    return kj <= qi, jnp.logical_and(kj >= qi, n > 0)


def _dattn_fwd(q, k, v, dil, *, name):
    t, w = q.shape
    hd = w // ATTN_HEADS
    seq = t // dil
    assert seq % ATTN_BLOCK == 0
    nb = seq // ATTN_BLOCK
    scale = hd ** -0.5

    def body(q_ref, kp_ref, kc_ref, vp_ref, vc_ref, o_ref, lse_ref):
        mc, mp = _band_masks(pl.program_id(1))
        for h in range(ATTN_HEADS):
            sl = slice(h * hd, (h + 1) * hd)
            qh = (q_ref[:, sl] * scale).astype(BF16)
            sc = jnp.where(mc, _dot(qh, kc_ref[:, sl].astype(BF16), 1, 1), NEG)
            sp = jnp.where(mp, _dot(qh, kp_ref[:, sl].astype(BF16), 1, 1), NEG)
            m = jnp.maximum(jnp.max(sc, axis=-1, keepdims=True), jnp.max(sp, axis=-1, keepdims=True))
            pc, pp = jnp.exp(sc - m), jnp.exp(sp - m)
            den = jnp.sum(pc, axis=-1, keepdims=True) + jnp.sum(pp, axis=-1, keepdims=True)
            o = _dot(pc.astype(BF16), vc_ref[:, sl].astype(BF16), 1, 0) + _dot(pp.astype(BF16), vp_ref[:, sl].astype(BF16), 1, 0)
            o_ref[:, sl] = o / den
            lse_ref[:, sl] = jnp.broadcast_to(m + jnp.log(den), (ATTN_BLOCK, hd))

    cur = pl.BlockSpec((ATTN_BLOCK, w), lambda r, n: (n, r))
    prev = pl.BlockSpec((ATTN_BLOCK, w), lambda r, n: (jnp.maximum(n - 1, 0), r))
    view = lambda z: z.reshape(seq, dil * w)
    o, lse = pl.pallas_call(
        body, name=name, grid=(dil, nb), in_specs=[cur, prev, cur, prev, cur], out_specs=[cur, cur],
        out_shape=[jax.ShapeDtypeStruct((seq, dil * w), F32)] * 2, compiler_params=_params(("parallel", "parallel")),
    )(view(q), view(k), view(k), view(v), view(v))
    return o.reshape(t, w), lse.reshape(t, w)


def _dattn_bwd(q, k, v, do, lse, delta, dil, *, name):
    t, w = q.shape
    hd = w // ATTN_HEADS
    seq = t // dil
    nb = seq // ATTN_BLOCK
    scale = hd ** -0.5

    def body(qc_ref, qn_ref, doc_ref, don_ref, lc_ref, ln_ref, dc_ref, dn_ref, kp_ref, kc_ref, vp_ref, vc_ref,
             dq_ref, dk_ref, dv_ref):
        n = pl.program_id(1)
        mc, mp = _band_masks(n)
        _, mx = _band_masks(jnp.where(n + 1 < nb, 1, 0))
        for h in range(ATTN_HEADS):
            sl = slice(h * hd, (h + 1) * hd)
            one = slice(h * hd, h * hd + 1)
            qc, qn = (qc_ref[:, sl] * scale).astype(BF16), (qn_ref[:, sl] * scale).astype(BF16)
            kc, kp = kc_ref[:, sl].astype(BF16), kp_ref[:, sl].astype(BF16)
            vc, vp = vc_ref[:, sl].astype(BF16), vp_ref[:, sl].astype(BF16)
            doc, don = doc_ref[:, sl].astype(BF16), don_ref[:, sl].astype(BF16)
            p_c = jnp.exp(jnp.where(mc, _dot(qc, kc, 1, 1), NEG) - lc_ref[:, one])
            p_p = jnp.exp(jnp.where(mp, _dot(qc, kp, 1, 1), NEG) - lc_ref[:, one])
            p_x = jnp.exp(jnp.where(mx, _dot(qn, kc, 1, 1), NEG) - ln_ref[:, one])
            ds_c = (p_c * (_dot(doc, vc, 1, 1) - dc_ref[:, one])).astype(BF16)
            ds_p = (p_p * (_dot(doc, vp, 1, 1) - dc_ref[:, one])).astype(BF16)
            ds_x = (p_x * (_dot(don, vc, 1, 1) - dn_ref[:, one])).astype(BF16)
            dq_ref[:, sl] = (_dot(ds_c, kc, 1, 0) + _dot(ds_p, kp, 1, 0)) * scale
            dk_ref[:, sl] = _dot(ds_c, qc, 0, 0) + _dot(ds_x, qn, 0, 0)
            dv_ref[:, sl] = _dot(p_c.astype(BF16), doc, 0, 0) + _dot(p_x.astype(BF16), don, 0, 0)

    cur = pl.BlockSpec((ATTN_BLOCK, w), lambda r, n: (n, r))
    prev = pl.BlockSpec((ATTN_BLOCK, w), lambda r, n: (jnp.maximum(n - 1, 0), r))
    nxt = pl.BlockSpec((ATTN_BLOCK, w), lambda r, n: (jnp.minimum(n + 1, nb - 1), r))
    view = lambda z: z.reshape(seq, dil * w)
    q, k, v, do, lse, delta = (view(z) for z in (q, k, v, do, lse, delta))
    dq, dk, dv = pl.pallas_call(
        body, name=name, grid=(dil, nb),
        in_specs=[cur, nxt, cur, nxt, cur, nxt, cur, nxt, prev, cur, prev, cur], out_specs=[cur, cur, cur],
        out_shape=[jax.ShapeDtypeStruct((seq, dil * w), F32)] * 3, compiler_params=_params(("parallel", "parallel")),
    )(q, q, do, do, lse, lse, delta, delta, k, k, v, v)
    return dq.reshape(t, w), dk.reshape(t, w), dv.reshape(t, w)


def _xattn_fwd(q, k, v, *, name):
    t, w = q.shape
    nm = k.shape[0]
    hd = w // XA_HEADS
    scale = hd ** -0.5
    tq = _tile(t, 512, 8)

    def body(q_ref, k_ref, v_ref, o_ref):
        for h in range(XA_HEADS):
            sl = slice(h * hd, (h + 1) * hd)
            s = _dot((q_ref[:, sl] * scale).astype(BF16), k_ref[:, sl].astype(BF16), 1, 1)
            p = jnp.exp(s - jnp.max(s, axis=-1, keepdims=True))
            p = p / jnp.sum(p, axis=-1, keepdims=True)
            o_ref[:, sl] = _dot(p.astype(BF16), v_ref[:, sl].astype(BF16), 1, 0)

    qs = pl.BlockSpec((tq, w), lambda i: (i, 0))
    ks = pl.BlockSpec((nm, w), lambda i: (0, 0))
    return pl.pallas_call(
        body, name=name, grid=(t // tq,), in_specs=[qs, ks, ks], out_specs=qs,
        out_shape=jax.ShapeDtypeStruct((t, w), F32), compiler_params=_params(("parallel",)),
    )(q, k, v)


def _xattn_bwd(q, k, v, do, *, name):
    t, w = q.shape
    nm = k.shape[0]
    hd = w // XA_HEADS
    scale = hd ** -0.5
    tq = _tile(t, 512, 8)

    def body(q_ref, k_ref, v_ref, do_ref, dq_ref, dk_ref, dv_ref):
        first = pl.program_id(0) == 0
        for h in range(XA_HEADS):
            sl = slice(h * hd, (h + 1) * hd)
            qh = (q_ref[:, sl] * scale).astype(BF16)
            kh, vh, doh = k_ref[:, sl].astype(BF16), v_ref[:, sl].astype(BF16), do_ref[:, sl].astype(BF16)
            s = _dot(qh, kh, 1, 1)
            p = jnp.exp(s - jnp.max(s, axis=-1, keepdims=True))
            p = p / jnp.sum(p, axis=-1, keepdims=True)
            dp = _dot(doh, vh, 1, 1)
            ds = (p * (dp - jnp.sum(p * dp, axis=-1, keepdims=True))).astype(BF16)
            dq_ref[:, sl] = _dot(ds, kh, 1, 0) * scale
            dk = _dot(ds, qh, 0, 0)
            dv = _dot(p.astype(BF16), doh, 0, 0)

            @pl.when(first)
            def _():
                dk_ref[:, sl] = dk
                dv_ref[:, sl] = dv

            @pl.when(jnp.logical_not(first))
            def _():
                dk_ref[:, sl] += dk
                dv_ref[:, sl] += dv

    qs = pl.BlockSpec((tq, w), lambda i: (i, 0))
    ks = pl.BlockSpec((nm, w), lambda i: (0, 0))
    return pl.pallas_call(
        body, name=name, grid=(t // tq,), in_specs=[qs, ks, ks, qs], out_specs=[qs, ks, ks],
        out_shape=[jax.ShapeDtypeStruct((t, w), F32), jax.ShapeDtypeStruct((nm, w), F32),
                   jax.ShapeDtypeStruct((nm, w), F32)],
        compiler_params=_params(("arbitrary",)),
    )(q, k, v, do)


def _dn_chunk_terms(qh, kh, gcc, gcr, bh):
    c = DN_CHUNK
    row = lax.broadcasted_iota(jnp.int32, (c, c), 0)
    col = lax.broadcasted_iota(jnp.int32, (c, c), 1)
    decay = jnp.exp(jnp.where(row >= col, gcc - gcr, NEG))
    kb = kh * bh
    kkt = _dot(kb, kh, 1, 1, HI)
    qkt = _dot(qh, kh, 1, 1, HI)
    return row, col, decay, kb, kkt, qkt


def _unit_lower_inverse(nm, row, col):
    eye = (row == col).astype(F32)
    inv = eye - nm
    p = nm
    for _ in range(int(math.log2(DN_CHUNK)) - 1):
        p = _dot(p, p, 1, 0, HI)
        inv = inv + _dot(inv, p, 1, 0, HI)
    return inv


def _dn_fwd(q, k, v, gc, gcr, beta, *, name):
    t, w = q.shape
    hd = w // DN_HEADS
    c = DN_CHUNK
    nch = t // c

    def body(q_ref, k_ref, v_ref, gc_ref, gcr_ref, b_ref, o_ref, s_ref, inv_ref, state):
        @pl.when(pl.program_id(0) == 0)
        def _():
            state[...] = jnp.zeros_like(state)

        for h in range(DN_HEADS):
            sl = slice(h * hd, (h + 1) * hd)
            qh, kh, vh = q_ref[:, sl], k_ref[:, sl], v_ref[:, sl]
            gcc, gcr_h, bh = gc_ref[:, h:h + 1], gcr_ref[0, h:h + 1, :], b_ref[:, h:h + 1]
            row, col, decay, kb, kkt, qkt = _dn_chunk_terms(qh, kh, gcc, gcr_h, bh)
            inv = _unit_lower_inverse(jnp.where(row > col, kkt * decay, 0.0), row, col)
            e = jnp.exp(gcc)
            u = _dot(inv, vh * bh, 1, 0, HI)
            wm = _dot(inv, kb * e, 1, 0, HI)
            sh = state[h]
            v_new = u - _dot(wm, sh, 1, 0, HI)
            o_ref[:, sl] = _dot(qh * e, sh, 1, 0, HI) + _dot(qkt * decay, v_new, 1, 0, HI)
            s_ref[0, h] = sh
            inv_ref[0, h] = inv
            gl = gcc[c - 1:c, :]
            state[h] = sh * jnp.exp(gl) + _dot(kh * jnp.exp(gl - gcc), v_new, 0, 0, HI)

    rows = pl.BlockSpec((c, w), lambda n: (n, 0))
    cols = pl.BlockSpec((c, DN_HEADS), lambda n: (n, 0))
    rowg = pl.BlockSpec((1, DN_HEADS, c), lambda n: (n, 0, 0))
    return pl.pallas_call(
        body, name=name, grid=(nch,), in_specs=[rows, rows, rows, cols, rowg, cols],
        out_specs=[rows, pl.BlockSpec((1, DN_HEADS, hd, hd), lambda n: (n, 0, 0, 0)),
                   pl.BlockSpec((1, DN_HEADS, c, c), lambda n: (n, 0, 0, 0))],
        out_shape=[jax.ShapeDtypeStruct((t, w), F32), jax.ShapeDtypeStruct((nch, DN_HEADS, hd, hd), F32),
                   jax.ShapeDtypeStruct((nch, DN_HEADS, c, c), F32)],
        scratch_shapes=[pltpu.VMEM((DN_HEADS, hd, hd), F32)], compiler_params=_params(("arbitrary",)),
    )(q, k, v, gc, gcr, beta)


def _dn_bwd(q, k, v, gc, gcr, beta, states, invs, do, *, name):
    t, w = q.shape
    hd = w // DN_HEADS
    c = DN_CHUNK
    nch = t // c

    def body(q_ref, k_ref, v_ref, gc_ref, gcr_ref, b_ref, s_ref, inv_ref, do_ref,
             dq_ref, dk_ref, dv_ref, dgc_ref, dgr_ref, db_ref, dstate):
        @pl.when(pl.program_id(0) == 0)
        def _():
            dstate[...] = jnp.zeros_like(dstate)

        for h in range(DN_HEADS):
            sl = slice(h * hd, (h + 1) * hd)
            qh, kh, vh, doh = q_ref[:, sl], k_ref[:, sl], v_ref[:, sl], do_ref[:, sl]
            gcc, gcr_h, bh = gc_ref[:, h:h + 1], gcr_ref[0, h:h + 1, :], b_ref[:, h:h + 1]
            row, col, decay, kb, kkt, qkt = _dn_chunk_terms(qh, kh, gcc, gcr_h, bh)
            inv, sh, dsn = inv_ref[0, h], s_ref[0, h], dstate[h]
            e = jnp.exp(gcc)
            gl = gcc[c - 1:c, :]
            el, r = jnp.exp(gl), jnp.exp(gl - gcc)
            u = _dot(inv, vh * bh, 1, 0, HI)
            wm = _dot(inv, kb * e, 1, 0, HI)
            v_new = u - _dot(wm, sh, 1, 0, HI)
            qk = qkt * decay
            kr = kh * r
            qe = qh * e
            d_qe = _dot(doh, sh, 1, 1, HI)
            d_qk = _dot(doh, v_new, 1, 1, HI)
            d_vnew = _dot(qk, doh, 0, 0, HI) + _dot(kr, dsn, 1, 0, HI)
            d_el = jnp.sum(jnp.sum(dsn * sh, axis=1, keepdims=True), axis=0, keepdims=True)
            d_kr = _dot(v_new, dsn, 1, 1, HI)
            d_w = -_dot(d_vnew, sh, 1, 1, HI)
            dstate[h] = dsn * el + _dot(qe, doh, 0, 0, HI) - _dot(wm, d_vnew, 0, 0, HI)
            d_ru = _dot(inv, d_vnew, 0, 0, HI)
            d_rw = _dot(inv, d_w, 0, 0, HI)
            d_a = -(_dot(d_ru, u, 1, 1, HI) + _dot(d_rw, wm, 1, 1, HI))
            dv_ref[:, sl] = d_ru * bh
            d_kb = d_rw * e
            d_e = jnp.sum(d_rw * kb, axis=1, keepdims=True) + jnp.sum(d_qe * qh, axis=1, keepdims=True)
            d_n = jnp.where(row > col, d_a, 0.0)
            d_m = d_n * decay
            d_p = d_qk * decay
            d_kb = d_kb + _dot(d_m, kh, 1, 0, HI)
            dk = _dot(d_m, kb, 0, 0, HI) + _dot(d_p, qh, 0, 0, HI)
            dq_ref[:, sl] = _dot(d_p, kh, 1, 0, HI) + d_qe * e
            dd = (d_n * kkt + d_qk * qkt) * decay
            d_r = jnp.sum(d_kr * kh, axis=1, keepdims=True)
            d_gl = d_el * el + jnp.sum(d_r * r, axis=0, keepdims=True)
            last = lax.broadcasted_iota(jnp.int32, (c, 1), 0) == c - 1
            dgc_ref[:, h:h + 1] = (jnp.sum(dd, axis=1, keepdims=True) + d_e * e - d_r * r + jnp.where(last, d_gl, 0.0))
            dgr_ref[0, h:h + 1, :] = -jnp.sum(dd, axis=0, keepdims=True)
            dk_ref[:, sl] = dk + d_kr * r + d_kb * bh
            db_ref[:, h:h + 1] = jnp.sum(d_ru * vh, axis=1, keepdims=True) + jnp.sum(d_kb * kh, axis=1, keepdims=True)

    rev = lambda n: nch - 1 - n
    rows = pl.BlockSpec((c, w), lambda n: (rev(n), 0))
    cols = pl.BlockSpec((c, DN_HEADS), lambda n: (rev(n), 0))
    rowg = pl.BlockSpec((1, DN_HEADS, c), lambda n: (rev(n), 0, 0))
    st = pl.BlockSpec((1, DN_HEADS, hd, hd), lambda n: (rev(n), 0, 0, 0))
    iv = pl.BlockSpec((1, DN_HEADS, c, c), lambda n: (rev(n), 0, 0, 0))
    return pl.pallas_call(
        body, name=name, grid=(nch,), in_specs=[rows, rows, rows, cols, rowg, cols, st, iv, rows],
        out_specs=[rows, rows, rows, cols, rowg, cols],
        out_shape=[jax.ShapeDtypeStruct((t, w), F32)] * 3
        + [jax.ShapeDtypeStruct((t, DN_HEADS), F32), jax.ShapeDtypeStruct((nch, DN_HEADS, c), F32),
           jax.ShapeDtypeStruct((t, DN_HEADS), F32)],
        scratch_shapes=[pltpu.VMEM((DN_HEADS, hd, hd), F32)], compiler_params=_params(("arbitrary",)),
    )(q, k, v, gc, gcr, beta, states, invs, do)


def _dn_head_terms(q_ref, k_ref, gc_ref, gcr_ref, b_ref, h, hd):
    c = DN_CHUNK
    sl = slice(h * hd, (h + 1) * hd)
    qh, kh = q_ref[:, sl], k_ref[:, sl]
    gcc, gcr_h, bh = gc_ref[:, h:h + 1], gcr_ref[0, h:h + 1, :], b_ref[:, h:h + 1]
    row = lax.broadcasted_iota(jnp.int32, (c, c), 0)
    col = lax.broadcasted_iota(jnp.int32, (c, c), 1)
    decay = jnp.exp(jnp.where(row >= col, gcc - gcr_h, NEG))
    kb = kh * bh
    kkt = _dot(kb.astype(BF16), kh.astype(BF16), 1, 1)
    qkt = _dot(qh.astype(BF16), kh.astype(BF16), 1, 1)
    gl = gcc[c - 1:c, :]
    return dict(sl=sl, q=qh, k=kh, gcc=gcc, b=bh, row=row, col=col, decay=decay, kb=kb, kkt=kkt, qkt=qkt,
                e=jnp.exp(gcc), el=jnp.exp(gl), r=jnp.exp(gl - gcc))


def _dn_local_fwd(q, k, v, gc, gcr, beta, *, name):
    t, w = q.shape
    hd = w // DN_HEADS
    c = DN_CHUNK
    nch = t // c
    heads = range(DN_HEADS)

    def body(q_ref, k_ref, v_ref, gc_ref, gcr_ref, b_ref,
             u_ref, wq_ref, wqt_ref, kr_ref, krt_ref, qk_ref, qkt_ref, invt_ref):
        tm = [_dn_head_terms(q_ref, k_ref, gc_ref, gcr_ref, b_ref, h, hd) for h in heads]
        pw = [jnp.where(m['row'] > m['col'], m['kkt'] * m['decay'], 0.0) for m in tm]
        inv = [(m['row'] == m['col']).astype(F32) - p for m, p in zip(tm, pw)]
        for _ in range(int(math.log2(c)) - 1):
            pw = [_dot(p.astype(BF16), p.astype(BF16), 1, 0) for p in pw]
            inv = [i + _dot(i.astype(BF16), p.astype(BF16), 1, 0) for i, p in zip(inv, pw)]
        for h, m in zip(heads, tm):
            rhs = jnp.concatenate([v_ref[:, m['sl']] * m['b'], m['kb'] * m['e']], axis=1).astype(BF16)
            sol = _dot(inv[h].astype(BF16), rhs, 1, 0)
            u_ref[:, m['sl']] = sol[:, :hd]
            wq = jnp.concatenate([sol[:, hd:], m['q'] * m['e']], axis=0)
            kr = m['k'] * m['r']
            qk = m['qkt'] * m['decay']
            wq_ref[0, h], wqt_ref[0, h] = wq.astype(BF16), wq.T.astype(BF16)
            kr_ref[0, h], krt_ref[0, h] = kr.astype(BF16), kr.T.astype(BF16)
            qk_ref[0, h], qkt_ref[0, h] = qk.astype(BF16), qk.T.astype(BF16)
            invt_ref[0, h] = inv[h].T.astype(BF16)

    rows = pl.BlockSpec((c, w), lambda n: (n, 0))
    cols = pl.BlockSpec((c, DN_HEADS), lambda n: (n, 0))
    rowg = pl.BlockSpec((1, DN_HEADS, c), lambda n: (n, 0, 0))
    per = lambda a, b: (pl.BlockSpec((1, DN_HEADS, a, b), lambda n: (n, 0, 0, 0)),
                        jax.ShapeDtypeStruct((nch, DN_HEADS, a, b), BF16))
    outs = [(rows, jax.ShapeDtypeStruct((t, w), F32)), per(2 * c, hd), per(hd, 2 * c), per(c, hd), per(hd, c),
            per(c, c), per(c, c), per(c, c)]
    return pl.pallas_call(
        body, name=name, grid=(nch,), in_specs=[rows, rows, rows, cols, rowg, cols],
        out_specs=[o[0] for o in outs], out_shape=[o[1] for o in outs], compiler_params=_params(("parallel",)),
    )(q, k, v, gc, gcr, beta)


def _dn_state_fwd(u, wq, krt, qk, gc, *, name):
    t, w = u.shape
    hd = w // DN_HEADS
    c = DN_CHUNK
    nch = t // c

    def body(u_ref, wq_ref, krt_ref, qk_ref, gc_ref, o_ref, vn_ref, s_ref, state):
        @pl.when(pl.program_id(0) == 0)
        def _():
            state[...] = jnp.zeros_like(state)

        for h in range(DN_HEADS):
            sl = slice(h * hd, (h + 1) * hd)
            sh = state[h]
            s_ref[0, h] = sh
            ws = _dot(wq_ref[0, h], sh.astype(BF16), 1, 0)
            v_new = (u_ref[:, sl] - ws[:c]).astype(BF16)
            vn_ref[:, sl] = v_new
            o_ref[:, sl] = ws[c:] + _dot(qk_ref[0, h], v_new, 1, 0)
            state[h] = sh * jnp.exp(gc_ref[c - 1:c, h:h + 1]) + _dot(krt_ref[0, h], v_new, 1, 0)

    rows = pl.BlockSpec((c, w), lambda n: (n, 0))
    per = lambda a, b: pl.BlockSpec((1, DN_HEADS, a, b), lambda n: (n, 0, 0, 0))
    return pl.pallas_call(
        body, name=name, grid=(nch,),
        in_specs=[rows, per(2 * c, hd), per(hd, c), per(c, c), pl.BlockSpec((c, DN_HEADS), lambda n: (n, 0))],
        out_specs=[rows, rows, per(hd, hd)],
        out_shape=[jax.ShapeDtypeStruct((t, w), F32), jax.ShapeDtypeStruct((t, w), BF16),
                   jax.ShapeDtypeStruct((nch, DN_HEADS, hd, hd), F32)],
        scratch_shapes=[pltpu.VMEM((DN_HEADS, hd, hd), F32)], compiler_params=_params(("arbitrary",)),
    )(u, wq, krt, qk, gc)


def _dn_state_bwd(do, qkt, kr, vn, states, wqt, gc, *, name):
    t, w = do.shape
    hd = w // DN_HEADS
    c = DN_CHUNK
    nch = t // c

    def body(do_ref, qkt_ref, kr_ref, vn_ref, s_ref, wqt_ref, gc_ref, dvn_ref, dkr_ref, del_ref, dstate):
        @pl.when(pl.program_id(0) == 0)
        def _():
            dstate[...] = jnp.zeros_like(dstate)

        for h in range(DN_HEADS):
            sl = slice(h * hd, (h + 1) * hd)
            dsn = dstate[h]
            dsb, dob = dsn.astype(BF16), do_ref[:, sl].astype(BF16)
            d_vnew = (_dot(qkt_ref[0, h], dob, 1, 0) + _dot(kr_ref[0, h], dsb, 1, 0)).astype(BF16)
            dvn_ref[:, sl] = d_vnew
            dkr_ref[:, sl] = _dot(vn_ref[:, sl], dsb, 1, 1)
            d_el = jnp.sum(jnp.sum(dsn * s_ref[0, h], axis=1, keepdims=True), axis=0, keepdims=True)
            del_ref[0, h:h + 1, :] = jnp.broadcast_to(d_el, (1, LANES))
            dstate[h] = dsn * jnp.exp(gc_ref[c - 1:c, h:h + 1]) + _dot(
                wqt_ref[0, h], jnp.concatenate([-d_vnew, dob], axis=0), 1, 0)

    rev = lambda n: nch - 1 - n
    rows = pl.BlockSpec((c, w), lambda n: (rev(n), 0))
    per = lambda a, b: pl.BlockSpec((1, DN_HEADS, a, b), lambda n: (rev(n), 0, 0, 0))
    return pl.pallas_call(
        body, name=name, grid=(nch,),
        in_specs=[rows, per(c, c), per(c, hd), rows, per(hd, hd), per(hd, 2 * c),
                  pl.BlockSpec((c, DN_HEADS), lambda n: (rev(n), 0))],
        out_specs=[rows, rows, pl.BlockSpec((1, DN_HEADS, LANES), lambda n: (rev(n), 0, 0))],
        out_shape=[jax.ShapeDtypeStruct((t, w), BF16), jax.ShapeDtypeStruct((t, w), F32),
                   jax.ShapeDtypeStruct((nch, DN_HEADS, LANES), F32)],
        scratch_shapes=[pltpu.VMEM((DN_HEADS, hd, hd), F32)], compiler_params=_params(("arbitrary",)),
    )(do, qkt, kr, vn, states, wqt, gc)


def _dn_local_bwd(q, k, v, gc, gcr, beta, invt, u, wq, vn, states, do, dvn, dkr, d_el, *, name):
    t, w = q.shape
    hd = w // DN_HEADS
    c = DN_CHUNK
    nch = t // c

    def body(q_ref, k_ref, v_ref, gc_ref, gcr_ref, b_ref, invt_ref, u_ref, wq_ref, vn_ref, s_ref, do_ref, dvn_ref,
             dkr_ref, del_ref, dq_ref, dk_ref, dv_ref, dgc_ref, dgr_ref, db_ref):
        for h in range(DN_HEADS):
            m = _dn_head_terms(q_ref, k_ref, gc_ref, gcr_ref, b_ref, h, hd)
            sl, qh, kh, bh, e, r, decay = m['sl'], m['q'], m['k'], m['b'], m['e'], m['r'], m['decay']
            vh = v_ref[:, sl]
            dob, dvb = do_ref[:, sl].astype(BF16), dvn_ref[:, sl]
            ts = _dot(jnp.concatenate([dob, -dvb], axis=0), s_ref[0, h].astype(BF16), 1, 1)
            d_qe, d_w = ts[:c], ts[c:]
            d_qk = _dot(dob, vn_ref[:, sl], 1, 1)
            d_rhs = _dot(invt_ref[0, h], jnp.concatenate([dvb, d_w.astype(BF16)], axis=1), 1, 0)
            d_ru, d_rw = d_rhs[:, :hd], d_rhs[:, hd:]
            sol = jnp.concatenate([u_ref[:, sl].astype(BF16), wq_ref[0, h, :c, :]], axis=1)
            d_a = -_dot(d_rhs.astype(BF16), sol, 1, 1)
            dv_ref[:, sl] = d_ru * bh
            d_e = jnp.sum(d_rw * m['kb'], axis=1, keepdims=True) + jnp.sum(d_qe * qh, axis=1, keepdims=True)
            d_n = jnp.where(m['row'] > m['col'], d_a, 0.0)
            d_m, d_p = (d_n * decay).astype(BF16), (d_qk * decay).astype(BF16)
            tk = _dot(jnp.concatenate([d_m, d_p], axis=0), kh.astype(BF16), 1, 0)
            d_kb = d_rw * e + tk[:c]
            dq_ref[:, sl] = tk[c:] + d_qe * e
            dk = _dot(d_m, m['kb'].astype(BF16), 0, 0) + _dot(d_p, qh.astype(BF16), 0, 0)
            dd = (d_n * m['kkt'] + d_qk * m['qkt']) * decay
            d_kr = dkr_ref[:, sl]
            d_r = jnp.sum(d_kr * kh, axis=1, keepdims=True)
            d_gl = del_ref[0, h:h + 1, 0:1] * m['el'] + jnp.sum(d_r * r, axis=0, keepdims=True)
            last = lax.broadcasted_iota(jnp.int32, (c, 1), 0) == c - 1
            dgc_ref[:, h:h + 1] = jnp.sum(dd, axis=1, keepdims=True) + d_e * e - d_r * r + jnp.where(last, d_gl, 0.0)
            dgr_ref[0, h:h + 1, :] = -jnp.sum(dd, axis=0, keepdims=True)
            dk_ref[:, sl] = dk + d_kr * r + d_kb * bh
            db_ref[:, h:h + 1] = jnp.sum(d_ru * vh, axis=1, keepdims=True) + jnp.sum(d_kb * kh, axis=1, keepdims=True)

    rows = pl.BlockSpec((c, w), lambda n: (n, 0))
    cols = pl.BlockSpec((c, DN_HEADS), lambda n: (n, 0))
    rowg = pl.BlockSpec((1, DN_HEADS, c), lambda n: (n, 0, 0))
    per = lambda a, b: pl.BlockSpec((1, DN_HEADS, a, b), lambda n: (n, 0, 0, 0))
    return pl.pallas_call(
        body, name=name, grid=(nch,),
        in_specs=[rows, rows, rows, cols, rowg, cols, per(c, c), rows, per(2 * c, hd), rows, per(hd, hd), rows, rows,
                  rows, pl.BlockSpec((1, DN_HEADS, LANES), lambda n: (n, 0, 0))],
        out_specs=[rows, rows, rows, cols, rowg, cols],
        out_shape=[jax.ShapeDtypeStruct((t, w), F32)] * 3
        + [jax.ShapeDtypeStruct((t, DN_HEADS), F32), jax.ShapeDtypeStruct((nch, DN_HEADS, c), F32),
           jax.ShapeDtypeStruct((t, DN_HEADS), F32)],
        compiler_params=_params(("parallel",)),
    )(q, k, v, gc, gcr, beta, invt, u, wq, vn, states, do, dvn, dkr, d_el)


def _other_chips():
    x, y = lax.axis_index("x"), lax.axis_index("y")
    return [(1 - x, y), (x, 1 - y), (1 - x, 1 - y)]


def _gather(src, *, name):
    def body(src_ref, out_ref, send_sems, recv_sems, local_sem):
        x, y, c = lax.axis_index("x"), lax.axis_index("y"), lax.axis_index("c")
        me, sibling, chips = (x, y, c), (x, y, 1 - c), _other_chips()
        slot = lambda px, py, pc: out_ref.at[4 * px + 2 * py + pc]

        def copy(k, block, to, own=False):
            return pltpu.make_async_remote_copy(
                src_ref=src_ref if own else slot(*block), dst_ref=slot(*block), send_sem=send_sems.at[k],
                recv_sem=recv_sems.at[k], device_id=to, device_id_type=MESH)

        local = pltpu.make_async_copy(src_ref, slot(*me), local_sem)
        local.start()
        first = [copy(0, me, sibling, own=True)] + [copy(1 + j, me, (*chip, c), own=True) for j, chip in enumerate(chips)]
        for cp in first:
            cp.start()
        passed = [copy(4 + j, (*chip, c), sibling) for j, chip in enumerate(chips)]
        for j, chip in enumerate(chips):
            copy(1 + j, (*chip, c), me).wait_recv()
            passed[j].start()
        copy(0, sibling, me).wait_recv()
        for j, chip in enumerate(chips):
            copy(4 + j, (*chip, 1 - c), me).wait_recv()
        for cp in first + passed:
            cp.wait_send()
        local.wait()

    return pl.pallas_call(
        body, name=name, out_shape=jax.ShapeDtypeStruct((N_DEV,) + src.shape, src.dtype),
        in_specs=[pl.BlockSpec(memory_space=pl.ANY)], out_specs=pl.BlockSpec(memory_space=pl.ANY),
        scratch_shapes=[pltpu.SemaphoreType.DMA((N_DEV - 1,)), pltpu.SemaphoreType.DMA((N_DEV - 1,)),
                        pltpu.SemaphoreType.DMA(())],
    )(src)


def _swap_with_sibling(src, *, name):
    def body(src_ref, out_ref, send_sem, recv_sem):
        x, y, c = lax.axis_index("x"), lax.axis_index("y"), lax.axis_index("c")
        cp = pltpu.make_async_remote_copy(src_ref=src_ref.at[1 - c], dst_ref=out_ref, send_sem=send_sem,
                                          recv_sem=recv_sem, device_id=(x, y, 1 - c), device_id_type=MESH)
        cp.start()
        cp.wait()

    return pl.pallas_call(
        body, name=name, out_shape=jax.ShapeDtypeStruct(src.shape[1:], src.dtype),
        in_specs=[pl.BlockSpec(memory_space=pl.ANY)], out_specs=pl.BlockSpec(memory_space=pl.ANY),
        scratch_shapes=[pltpu.SemaphoreType.DMA(()), pltpu.SemaphoreType.DMA(())],
    )(src)


def _add_own_half(src, got, *, name):
    _, n, r, c = src.shape
    tr = _tile(r, 512, 8)

    def body(s0_ref, s1_ref, got_ref, o_ref):
        own = jnp.where(lax.axis_index("c") == 0, s0_ref[0, 0], s1_ref[0, 0])
        o_ref[0] = own + got_ref[0]

    half = lambda h: pl.BlockSpec((1, 1, tr, c), lambda j, i: (h, j, i, 0))
    blk = pl.BlockSpec((1, tr, c), lambda j, i: (j, i, 0))
    return pl.pallas_call(
        body, name=name, grid=(n, r // tr), in_specs=[half(0), half(1), blk], out_specs=blk,
        out_shape=jax.ShapeDtypeStruct((n, r, c), src.dtype), compiler_params=_params(("parallel", "parallel")),
    )(src, src, got)


def _swap_between_chips(src, *, name):
    def body(src_ref, out_ref, send_sems, recv_sems, local_sem):
        x, y, c = lax.axis_index("x"), lax.axis_index("y"), lax.axis_index("c")
        mine = 2 * x + y
        local = pltpu.make_async_copy(src_ref.at[mine], out_ref.at[mine], local_sem)
        local.start()
        copies = [pltpu.make_async_remote_copy(
            src_ref=src_ref.at[2 * px + py], dst_ref=out_ref.at[mine], send_sem=send_sems.at[j],
            recv_sem=recv_sems.at[j], device_id=(px, py, c), device_id_type=MESH) for j, (px, py) in enumerate(_other_chips())]
        for cp in copies:
            cp.start()
        for j, (px, py) in enumerate(_other_chips()):
            pltpu.make_async_remote_copy(
                src_ref=src_ref.at[mine], dst_ref=out_ref.at[2 * px + py], send_sem=send_sems.at[j],
                recv_sem=recv_sems.at[j], device_id=(px, py, c), device_id_type=MESH).wait_recv()
        for cp in copies:
            cp.wait_send()
        local.wait()

    return pl.pallas_call(
        body, name=name, out_shape=jax.ShapeDtypeStruct(src.shape, src.dtype),
        in_specs=[pl.BlockSpec(memory_space=pl.ANY)], out_specs=pl.BlockSpec(memory_space=pl.ANY),
        scratch_shapes=[pltpu.SemaphoreType.DMA((3,)), pltpu.SemaphoreType.DMA((3,)), pltpu.SemaphoreType.DMA(())],
    )(src)


def _sum_adamw(parts, w, m, v, *, name):
    n_parts, r, c = parts.shape
    tr = _tile(r, 256, 8)
    c1, c2 = 1.0 - ADAM_B1 ** ADAM_STEP, 1.0 - ADAM_B2 ** ADAM_STEP

    def body(p_ref, w_ref, m_ref, v_ref, g_ref, d_ref, nm_ref, nv_ref):
        g = p_ref[0]
        for s in range(1, n_parts):
            g = g + p_ref[s]
        nm = ADAM_B1 * m_ref[...] + (1.0 - ADAM_B1) * g
        nv = ADAM_B2 * v_ref[...] + (1.0 - ADAM_B2) * (g * g)
        g_ref[...] = g
        nm_ref[...] = nm
        nv_ref[...] = nv
        d_ref[...] = -ADAM_LR * ((nm / c1) / (jnp.sqrt(nv / c2) + ADAM_EPS) + ADAM_WD * w_ref[...])

    blk = pl.BlockSpec((tr, c), lambda i: (i, 0))
    return pl.pallas_call(
        body, name=name, grid=(r // tr,), in_specs=[pl.BlockSpec((n_parts, tr, c), lambda i: (0, i, 0)), blk, blk, blk],
        out_specs=[blk] * 4, out_shape=[jax.ShapeDtypeStruct((r, c), F32)] * 4, compiler_params=_params(("parallel",)),
    )(parts, w, m, v)


def _pack_rows(n):
    return -(-n // (PACK_COLS * PACK_ROWS)) * PACK_ROWS


def _pack(blocks):
    parts, spans, at = [], [], 0
    for blk, n_lead in blocks:
        lead = blk.shape[:n_lead]
        n = math.prod(blk.shape[n_lead:])
        rows = _pack_rows(n)
        flat = blk.reshape(lead + (n,))
        flat = jnp.pad(flat, [(0, 0)] * n_lead + [(0, rows * PACK_COLS - n)])
        parts.append(flat.reshape(lead + (rows, PACK_COLS)))
        spans.append((at, rows))
        at += rows
    return jnp.concatenate(parts, axis=-2), spans


def _unpack(buf, span, shape):
    at, rows = span
    lead = buf.shape[:-2]
    flat = lax.slice_in_dim(buf, at, at + rows, axis=buf.ndim - 2).reshape(lead + (rows * PACK_COLS,))
    return lax.slice_in_dim(flat, 0, math.prod(shape), axis=len(lead)).reshape(lead + tuple(shape))


def _join_shards(g, axis):
    g = jnp.moveaxis(g, 0, axis)
    return g.reshape(g.shape[:axis] + (g.shape[axis] * g.shape[axis + 1],) + g.shape[axis + 2:])


def _split_shards(full, axis):
    s = full.shape
    g = full.reshape(s[:axis] + (N_DEV, s[axis] // N_DEV) + s[axis + 1:])
    return jnp.moveaxis(g, axis, 0)


def _ffn_fwd(x, g, w_in, w_out, tag):
    h = _rms_fwd(x, g, name=tag + "_norm", out_dtype=BF16)
    gate, up, a = _mm_swiglu(h, w_in, name=tag + "_in")
    return _mm(a, w_out, name=tag + "_out", res=x, scale=0.5, tk=2816), (x, h, gate, up, a)


def _ffn_bwd(saved, g, w_in, w_out, dxo, tag):
    x, h, gate, up, a = saved
    f = w_out.shape[0]
    d_w_out = _mm(a, dxo, name=tag + "_dwout", ta=True, scale=0.5, tm=1408)
    dgate, dup = _mm_dswiglu(dxo, w_out, gate, up, name=tag + "_da", scale=0.5)
    d_w_in = jnp.concatenate([_mm(h, dgate, name=tag + "_dwin_g", ta=True, tn=1408),
                              _mm(h, dup, name=tag + "_dwin_u", ta=True, tn=1408)], axis=1)
    dh = _mm(dgate, w_in[:, :f], name=tag + "_dh_g", tb=True, tk=1408)
    dh = _mm(dup, w_in[:, f:], name=tag + "_dh_u", tb=True, tk=1408, res=dh)
    dx, dg = _rms_bwd(x, g, dh, dxo, name=tag + "_dnorm")
    return dx, dg, d_w_in, d_w_out


def _block_diag(w):
    n, j, k = w.shape
    return (w[:, :, None, :] * jnp.eye(n, dtype=w.dtype)[:, None, :, None]).reshape(n * j, n * k)


def _diag_blocks(dense, n):
    j, k = dense.shape[0] // n, dense.shape[1] // n
    return jnp.stack([dense[i * j:(i + 1) * j, i * k:(i + 1) * k] for i in range(n)], axis=0)


def _attn_lru_fwd(x, p, tag):
    aw = ATTN_HEADS * 64
    h = _rms_fwd(x, p['mix_norm'], name=tag + "_norm", out_dtype=BF16)
    proj = _mm(h, p['ab_w_in'], name=tag + "_in", tn=1280)
    q, k, v = (proj[:, i * aw:(i + 1) * aw] for i in range(3))
    outs, lses = [], []
    for window, dil in DILATED_PATTERNS:
        assert window // dil == ATTN_BLOCK
        o, l = _dattn_fwd(q, k, v, dil, name=f"{tag}_attn{dil}")
        outs.append(o)
        lses.append(l)
    wa, wx = _block_diag(p['lru_w_a']).astype(BF16), _block_diag(p['lru_w_x']).astype(BF16)
    sp, sp_vjp = jax.vjp(lambda lam: LRU_C * jax.nn.softplus(-lam), p['lru_lambda'])
    a, b, xc = _lru_gates_fwd(proj, 3, p['lru_conv_w'], p['lru_conv_b'], wa, p['lru_b_a'], wx, p['lru_b_x'], sp,
                              name=tag + "_gates")
    hs = _scan(a, b, name=tag + "_scan")
    cat, attn, lse_all = _mix_join_fwd(outs, lses, hs, proj, 4, name=tag + "_join")
    xo = _mm(cat, p['ab_w_out'], name=tag + "_out", res=x)
    return xo, (x, h, proj, q, k, v, attn, lse_all, a, hs, xc, wa, wx, sp, sp_vjp, cat)


def _attn_lru_bwd(saved, p, dxo, tag):
    x, h, proj, q, k, v, attn, lse_all, a, hs, xc, wa, wx, sp, sp_vjp, cat = saved
    aw = attn.shape[1]
    g = {'ab_w_out': _mm(cat, dxo, name=tag + "_dwout", ta=True)}
    dcat = _mm(dxo, p['ab_w_out'], name=tag + "_dcat", tb=True)
    delta, dhs, dgr = _mix_join_bwd(dcat, attn, hs, proj, 4, name=tag + "_djoin")
    a_next = jnp.concatenate([a[1:], jnp.zeros_like(a[:1])], axis=0)
    dtot = _scan(a_next, dhs, name=tag + "_dscan", reverse=True)
    h_prev = jnp.concatenate([jnp.zeros_like(hs[:1]), hs[:-1]], axis=0)
    dxc, dwa, dwx, vecs = _lru_gates_bwd(xc, dtot, h_prev, wa, p['lru_b_a'], wx, p['lru_b_x'], sp, name=tag + "_dgates")
    dxr, g['lru_conv_w'] = _conv_bwd(dxc, proj, 3, p['lru_conv_w'], name=tag + "_dconv")
    g['lru_b_a'], g['lru_b_x'], g['lru_conv_b'] = vecs[0], vecs[1], vecs[3]
    g['lru_lambda'], = sp_vjp(vecs[2])
    g['lru_w_a'], g['lru_w_x'] = _diag_blocks(dwa, LRU_BLOCKS), _diag_blocks(dwx, LRU_BLOCKS)
    dattn = dcat[:, :aw]
    dq = dk = dv = 0.0
    for _, dil in DILATED_PATTERNS:
        dq_, dk_, dv_ = _dattn_bwd(q, k, v, dattn, lse_all, delta, dil, name=f"{tag}_dattn{dil}")
        dq, dk, dv = dq + dq_, dk + dk_, dv + dv_
    dproj = jnp.concatenate([dq.astype(BF16), dk.astype(BF16), dv.astype(BF16), dxr, dgr], axis=-1)
    g['ab_w_in'] = _mm(h, dproj, name=tag + "_dwin", ta=True, tn=1280)
    dh = _mm(dproj, p['ab_w_in'], name=tag + "_dh", tb=True, tk=1280)
    dx, g['mix_norm'] = _rms_bwd(x, p['mix_norm'], dh, dxo, name=tag + "_dnorm")
    return dx, g


def _dn_decay(a, b, a_log, dt_bias):
    t = a.shape[0]
    g = -jnp.exp(a_log) * jax.nn.softplus(a + dt_bias)
    gc = jnp.cumsum(g.reshape(t // DN_CHUNK, DN_CHUNK, DN_HEADS), axis=1)
    return gc.reshape(t, DN_HEADS), jnp.swapaxes(gc, 1, 2), jax.nn.sigmoid(b)


def _dn_in_width(w):
    return -(-(4 * w + 2 * DN_HEADS) // LANES) * LANES


def _deltanet_fwd(x, p, tag):
    w = p['dn_w_out'].shape[0]
    h = _rms_fwd(x, p['mix_norm'], name=tag + "_norm", out_dtype=BF16)
    proj = _mm(h, p['dn_w_in'], name=tag + "_in", tb=True, tn=1408)
    q, k, v, c = _dn_prep_fwd(proj, p['dn_conv_w'], name=tag + "_prep")
    a, b = proj[:, 4 * w:4 * w + DN_HEADS], proj[:, 4 * w + DN_HEADS:4 * w + 2 * DN_HEADS]
    (gc, gcr, beta), decay_vjp = jax.vjp(_dn_decay, a, b, p['dn_a_log'], p['dn_dt_bias'])
    prep = (q, k, v, gc, gcr, beta)
    u, wq, wqt, kr, krt, qk, qkt, invt = _dn_local_fwd(*prep, name=tag + "_local")
    o, vn, states = _dn_state_fwd(u, wq, krt, qk, gc, name=tag + "_state")
    og = _dn_gate_fwd(o, proj, 3, p['dn_o_norm'], name=tag + "_gate")
    xo = _mm(og, p['dn_w_out'], name=tag + "_out", res=x)
    return xo, (x, h, proj, c, o, prep, (u, wq, wqt, kr, qkt, invt, vn, states), decay_vjp, og)


def _deltanet_bwd(saved, p, dxo, tag):
    x, h, proj, c, o, prep, (u, wq, wqt, kr, qkt, invt, vn, states), decay_vjp, og = saved
    g = {'dn_w_out': _mm(og, dxo, name=tag + "_dwout", ta=True)}
    dog = _mm(dxo, p['dn_w_out'], name=tag + "_dog", tb=True)
    do, dz, g['dn_o_norm'] = _dn_gate_bwd(o, proj, 3, p['dn_o_norm'], dog, name=tag + "_dgate")
    dvn, dkr, d_el = _dn_state_bwd(do, qkt, kr, vn, states, wqt, prep[3], name=tag + "_dstate")
    dq, dk, dv, dgc, dgr, dbeta = _dn_local_bwd(*prep, invt, u, wq, vn, states, do, dvn, dkr, d_el, name=tag + "_dlocal")
    dc = _dn_prep_bwd(c, dq, dk, dv, name=tag + "_dprep")
    dqkv, g['dn_conv_w'] = _conv_bwd(dc, proj, 0, p['dn_conv_w'], name=tag + "_dconv")
    da, db, g['dn_a_log'], g['dn_dt_bias'] = decay_vjp((dgc, dgr, dbeta))
    t = x.shape[0]
    pad = jnp.zeros((t, p['dn_w_in'].shape[0] - dqkv.shape[1] - dz.shape[1] - 2 * DN_HEADS), BF16)
    dproj = jnp.concatenate([dqkv, dz, da.astype(BF16), db.astype(BF16), pad], axis=-1)
    g['dn_w_in'] = _mm(dproj, h, name=tag + "_dwin", ta=True, tm=1408)
    dh = _mm(dproj, p['dn_w_in'], name=tag + "_dh", tk=1408)
    dx, g['mix_norm'] = _rms_bwd(x, p['mix_norm'], dh, dxo, name=tag + "_dnorm")
    return dx, g


def _xattn_block_fwd(x, mem, p, tag):
    w = x.shape[1]
    h = _rms_fwd(x, p['xa_norm'], name=tag + "_norm", out_dtype=BF16)
    mh = _rms_fwd(mem, p['xa_mem_norm'], name=tag + "_mnorm", out_dtype=BF16)
    q = _mm(h, p['xa_wq'], name=tag + "_q")
    kv = _mm(mh, p['xa_wkv'], name=tag + "_kv")
    k, v = kv[:, :w], kv[:, w:]
    o = _xattn_fwd(q, k, v, name=tag + "_attn").astype(BF16)
    xo = _mm(o, p['xa_wo'], name=tag + "_out", res=x)
    return xo, (x, h, mh, q, k, v, o)


def _xattn_block_bwd(saved, mem, p, dxo, tag):
    x, h, mh, q, k, v, o = saved
    g = {'xa_wo': _mm(o, dxo, name=tag + "_dwo", ta=True)}
    do = _mm(dxo, p['xa_wo'], name=tag + "_do", tb=True)
    dq, dk, dv = _xattn_bwd(q, k, v, do, name=tag + "_dattn")
    dq = dq.astype(BF16)
    dkv = jnp.concatenate([dk, dv], axis=-1).astype(BF16)
    g['xa_wq'] = _mm(h, dq, name=tag + "_dwq", ta=True)
    g['xa_wkv'] = _mm(mh, dkv, name=tag + "_dwkv", ta=True)
    dmh = _mm(dkv, p['xa_wkv'], name=tag + "_dmh", tb=True)
    _, g['xa_mem_norm'] = _rms_bwd(mem, p['xa_mem_norm'], dmh, None, name=tag + "_dmnorm")
    dh = _mm(dq, p['xa_wq'], name=tag + "_dh", tb=True)
    dx, g['xa_norm'] = _rms_bwd(x, p['xa_norm'], dh, dxo, name=tag + "_dnorm")
    return dx, g


def _layer_params(full, layer):
    p = {n: full[n][layer] for n in ('ffn1_norm', 'ffn1_w_in', 'ffn1_w_out', 'mix_norm', 'xa_norm', 'xa_mem_norm',
                                     'xa_wq', 'xa_wkv', 'xa_wo', 'ffn2_norm', 'ffn2_w_in', 'ffn2_w_out')}
    mixer = ('ab_w_in', 'lru_conv_w', 'lru_conv_b', 'lru_w_a', 'lru_b_a', 'lru_w_x', 'lru_b_x', 'lru_lambda', 'ab_w_out') \
        if layer % 2 == 0 else ('dn_w_in', 'dn_conv_w', 'dn_a_log', 'dn_dt_bias', 'dn_o_norm', 'dn_w_out')
    p.update({n: full[n][layer // 2] for n in mixer})
    return p


def _step(x, mem, target, full, depth):
    saved = []
    for layer in range(depth):
        p = _layer_params(full, layer)
        tag = f"l{layer}"
        x, s1 = _ffn_fwd(x, p['ffn1_norm'], p['ffn1_w_in'], p['ffn1_w_out'], tag + "_ffn1")
        x, s2 = (_attn_lru_fwd if layer % 2 == 0 else _deltanet_fwd)(x, p, tag + "_mix")
        x, s3 = _xattn_block_fwd(x, mem, p, tag + "_xa")
        x, s4 = _ffn_fwd(x, p['ffn2_norm'], p['ffn2_w_in'], p['ffn2_w_out'], tag + "_ffn2")
        saved.append((p, s1, s2, s3, s4))
    sq, dx, d_final = _final_loss(x, full['final_norm'], target, name="final_loss")
    per_layer = []
    for layer in reversed(range(depth)):
        p, s1, s2, s3, s4 = saved[layer]
        tag = f"l{layer}"
        g = {}
        dx, g['ffn2_norm'], g['ffn2_w_in'], g['ffn2_w_out'] = _ffn_bwd(s4, p['ffn2_norm'], p['ffn2_w_in'], p['ffn2_w_out'], dx, tag + "_ffn2")
        dx, gx = _xattn_block_bwd(s3, mem, p, dx, tag + "_xa")
        dx, gm = (_attn_lru_bwd if layer % 2 == 0 else _deltanet_bwd)(s2, p, dx, tag + "_mix")
        dx, g['ffn1_norm'], g['ffn1_w_in'], g['ffn1_w_out'] = _ffn_bwd(s1, p['ffn1_norm'], p['ffn1_w_in'], p['ffn1_w_out'], dx, tag + "_ffn1")
        g.update(gx)
        g.update(gm)
        per_layer.insert(0, g)
    grads = {'final_norm': d_final}
    for n in WEIGHTS[:-1]:
        grads[n] = jnp.stack([g[n] for g in per_layer if n in g], axis=0)
    return sq, dx, grads


def kernel(x, mem, ffn1_norm, ffn1_w_in, ffn1_w_out, mix_norm, xa_norm, xa_mem_norm, xa_wq, xa_wkv, xa_wo, ffn2_norm, ffn2_w_in, ffn2_w_out, ab_w_in, lru_conv_w, lru_conv_b, lru_w_a, lru_b_a, lru_w_x, lru_b_x, lru_lambda, ab_w_out, dn_w_in, dn_conv_w, dn_a_log, dn_dt_bias, dn_o_norm, dn_w_out, final_norm, loss_target, m_ffn1_norm, m_ffn1_w_in, m_ffn1_w_out, m_mix_norm, m_xa_norm, m_xa_mem_norm, m_xa_wq, m_xa_wkv, m_xa_wo, m_ffn2_norm, m_ffn2_w_in, m_ffn2_w_out, m_ab_w_in, m_lru_conv_w, m_lru_conv_b, m_lru_w_a, m_lru_b_a, m_lru_w_x, m_lru_b_x, m_lru_lambda, m_ab_w_out, m_dn_w_in, m_dn_conv_w, m_dn_a_log, m_dn_dt_bias, m_dn_o_norm, m_dn_w_out, m_final_norm, v_ffn1_norm, v_ffn1_w_in, v_ffn1_w_out, v_mix_norm, v_xa_norm, v_xa_mem_norm, v_xa_wq, v_xa_wkv, v_xa_wo, v_ffn2_norm, v_ffn2_w_in, v_ffn2_w_out, v_ab_w_in, v_lru_conv_w, v_lru_conv_b, v_lru_w_a, v_lru_b_a, v_lru_w_x, v_lru_b_x, v_lru_lambda, v_ab_w_out, v_dn_w_in, v_dn_conv_w, v_dn_a_log, v_dn_dt_bias, v_dn_o_norm, v_dn_w_out, v_final_norm):
    args = dict(locals())
    flip = lambda n, a: jnp.swapaxes(a, 1, 2) if n in TRANSPOSED else a
    axis = lambda n: 3 - SHARD_AXIS[n] if n in TRANSPOSED else SHARD_AXIS[n]
    local = {n: flip(n, args[n]) for n in WEIGHTS}
    depth = ffn1_norm.shape[0]

    big = [n for n in WEIGHTS if n in SHARD_AXIS and n not in GATHER_F32]
    send16, spans16 = _pack([(local[n].astype(BF16), 0) for n in big])
    send32, spans32 = _pack([(local[n], 0) for n in GATHER_F32])
    got16 = _gather(send16, name="gather_matrices")
    got32 = _gather(send32, name="gather_filters")
    full = dict(local)
    for names, got, spans in ((big, got16, spans16), (GATHER_F32, got32, spans32)):
        for n, span in zip(names, spans):
            full[n] = _join_shards(_unpack(got, span, local[n].shape), axis(n))
    dn_rows = full['dn_w_in'].shape[1]
    full['dn_w_in'] = jnp.pad(full['dn_w_in'], ((0, 0), (0, _dn_in_width(full['dn_w_out'].shape[1]) - dn_rows), (0, 0)))

    sq, dx, grads = _step(x[0], mem[0], loss_target[0], full, depth)
    grads['dn_w_in'] = grads['dn_w_in'][:, :dn_rows]
    loss = lax.psum(0.5 * jnp.sum(sq) / x.shape[2], ("x", "y", "c"))

    by_core = lambda z: jnp.swapaxes(z.reshape((N_DEV // 2, 2) + z.shape[1:]), 0, 1)
    contrib = [(by_core(_split_shards(grads[n], axis(n)) if n in SHARD_AXIS
                        else jnp.broadcast_to(grads[n], (N_DEV,) + grads[n].shape)), 2) for n in WEIGHTS]
    send, spans = _pack(contrib)
    got = _swap_with_sibling(send, name="grads_to_sibling")
    chip_sum = _add_own_half(send, got, name="grads_chip_sum")
    parts = _swap_between_chips(chip_sum, name="grads_between_chips")
    w_pack, _ = _pack([(local[n], 0) for n in WEIGHTS])
    m_pack, _ = _pack([(flip(n, args["m_" + n]), 0) for n in WEIGHTS])
    v_pack, _ = _pack([(flip(n, args["v_" + n]), 0) for n in WEIGHTS])
    outs = _sum_adamw(parts, w_pack, m_pack, v_pack, name="sum_adamw")
    grad_w, delta_w, new_m, new_v = ([flip(n, _unpack(o, span, local[n].shape)) for n, span in zip(WEIGHTS, spans)]
                                     for o in outs)
    return (loss, dx[None], *grad_w, *delta_w, *new_m, *new_v)
```

```python
import math

import jax
import jax.numpy as jnp
from jax import lax
from jax.experimental import pallas as pl
from jax.experimental.pallas import tpu as pltpu

F32, BF16 = jnp.float32, jnp.bfloat16
MESH = pl.DeviceIdType.MESH
N_DEV = 8
V7X_VMEM_LIMIT = 56 << 20
LANES = 128
PACK_COLS = 1024
PACK_ROWS = 16
HI = lax.Precision.HIGHEST
NEG = -1e30

NORM_EPS = 1e-6
CONV_K = 4
ATTN_HEADS = 8
DILATED_PATTERNS = ((128, 1), (512, 4), (2048, 16))
ATTN_BLOCK = 128
LRU_BLOCKS = 8
LRU_C = 8.0
DN_HEADS = 8
DN_CHUNK = 64
XA_HEADS = 4
ADAM_LR, ADAM_B1, ADAM_B2, ADAM_EPS, ADAM_WD, ADAM_STEP = 0.001, 0.9, 0.999, 1e-08, 0.01, 10

WEIGHTS = ['ffn1_norm', 'ffn1_w_in', 'ffn1_w_out', 'mix_norm', 'xa_norm', 'xa_mem_norm', 'xa_wq', 'xa_wkv', 'xa_wo',
           'ffn2_norm', 'ffn2_w_in', 'ffn2_w_out', 'ab_w_in', 'lru_conv_w', 'lru_conv_b', 'lru_w_a', 'lru_b_a',
           'lru_w_x', 'lru_b_x', 'lru_lambda', 'ab_w_out', 'dn_w_in', 'dn_conv_w', 'dn_a_log', 'dn_dt_bias',
           'dn_o_norm', 'dn_w_out', 'final_norm']
SHARD_AXIS = {'ffn1_w_in': 2, 'ffn1_w_out': 1, 'xa_wq': 1, 'xa_wkv': 2, 'xa_wo': 1, 'ffn2_w_in': 2, 'ffn2_w_out': 1,
              'ab_w_in': 2, 'lru_conv_w': 2, 'ab_w_out': 1, 'dn_w_in': 2, 'dn_conv_w': 2, 'dn_w_out': 1}
GATHER_F32 = ('lru_conv_w', 'dn_conv_w')
TRANSPOSED = ('ffn1_w_in', 'xa_wkv', 'ffn2_w_in', 'ab_w_in', 'dn_w_in')


def _params(sem=None):
    return pltpu.CompilerParams(dimension_semantics=sem, vmem_limit_bytes=V7X_VMEM_LIMIT)


def _tile(n, pref, mult):
    best = None
    t = mult
    while t <= min(n, pref):
        if n % t == 0:
            best = t
        t += mult
    return n if best is None else best


def _dot(a, b, ca, cb, prec=None):
    return lax.dot_general(a, b, (((ca,), (cb,)), ((), ())), preferred_element_type=F32, precision=prec)


def _mm(a, b, *, name, ta=False, tb=False, out_dtype=F32, res=None, scale=1.0, tm=1024, tn=1024, tk=1024):
    m, kdim = (a.shape[1], a.shape[0]) if ta else a.shape
    n = b.shape[0] if tb else b.shape[1]
    assert (b.shape[1] if tb else b.shape[0]) == kdim
    tm = _tile(m, tm, LANES if ta else 16)
    tn = _tile(n, tn, LANES)
    tk = _tile(kdim, tk, LANES)
    nk = kdim // tk
    a_spec = pl.BlockSpec((tk, tm), lambda i, j, k: (k, i)) if ta else pl.BlockSpec((tm, tk), lambda i, j, k: (i, k))
    b_spec = pl.BlockSpec((tn, tk), lambda i, j, k: (j, k)) if tb else pl.BlockSpec((tk, tn), lambda i, j, k: (k, j))
    o_spec = pl.BlockSpec((tm, tn), lambda i, j, k: (i, j))
    ca, cb = (0 if ta else 1), (1 if tb else 0)
    has_res = res is not None

    def finish(acc, r_ref, o_ref):
        y = acc if scale == 1.0 else acc * scale
        if has_res:
            y = y + r_ref[...]
        o_ref[...] = y.astype(out_dtype)

    def body(*refs):
        a_ref, b_ref = refs[0], refs[1]
        r_ref = refs[2] if has_res else None
        o_ref = refs[2 + has_res]
        p = _dot(a_ref[...].astype(BF16), b_ref[...].astype(BF16), ca, cb)
        if nk == 1:
            finish(p, r_ref, o_ref)
            return
        acc = refs[3 + has_res]
        k = pl.program_id(2)

        @pl.when(k == 0)
        def _():
            acc[...] = p

        @pl.when(k > 0)
        def _():
            acc[...] += p

        @pl.when(k == nk - 1)
        def _():
            finish(acc[...], r_ref, o_ref)

    ins, specs = [a, b], [a_spec, b_spec]
    if has_res:
        ins.append(res)
        specs.append(o_spec)
    return pl.pallas_call(
        body, name=name, grid=(m // tm, n // tn, nk), in_specs=specs, out_specs=o_spec,
        out_shape=jax.ShapeDtypeStruct((m, n), out_dtype),
        scratch_shapes=[] if nk == 1 else [pltpu.VMEM((tm, tn), F32)],
        compiler_params=_params(("parallel", "parallel", "arbitrary")),
    )(*ins)


def _rms_fwd(x, g, *, name, out_dtype):
    r, d = x.shape
    tr = _tile(r, 512, 16)

    def body(x_ref, g_ref, o_ref):
        xv = x_ref[...]
        rs = lax.rsqrt(jnp.mean(xv * xv, axis=-1, keepdims=True) + NORM_EPS)
        o_ref[...] = (xv * rs * g_ref[...]).astype(out_dtype)

    return pl.pallas_call(
        body, name=name, grid=(r // tr,),
        in_specs=[pl.BlockSpec((tr, d), lambda i: (i, 0)), pl.BlockSpec((1, d), lambda i: (0, 0))],
        out_specs=pl.BlockSpec((tr, d), lambda i: (i, 0)),
        out_shape=jax.ShapeDtypeStruct((r, d), out_dtype), compiler_params=_params(("parallel",)),
    )(x, g.reshape(1, d))


def _rms_bwd(x, g, dh, dres, *, name):
    r, d = x.shape
    tr = _tile(r, 512, 8)
    has_res = dres is not None

    def body(*refs):
        x_ref, g_ref, dh_ref = refs[:3]
        r_ref = refs[3] if has_res else None
        dx_ref, dg_ref = refs[3 + has_res], refs[4 + has_res]
        xv = x_ref[...]
        rs = lax.rsqrt(jnp.mean(xv * xv, axis=-1, keepdims=True) + NORM_EPS)
        xh = xv * rs
        dhv = dh_ref[...]
        dgh = dhv * g_ref[...]
        dx = rs * (dgh - xh * jnp.mean(dgh * xh, axis=-1, keepdims=True))
        if has_res:
            dx = dx + r_ref[...]
        dx_ref[...] = dx
        part = jnp.sum(dhv * xh, axis=0, keepdims=True)

        @pl.when(pl.program_id(0) == 0)
        def _():
            dg_ref[...] = part

        @pl.when(pl.program_id(0) > 0)
        def _():
            dg_ref[...] += part

    row = pl.BlockSpec((tr, d), lambda i: (i, 0))
    vec = pl.BlockSpec((1, d), lambda i: (0, 0))
    ins, specs = [x, g.reshape(1, d), dh], [row, vec, row]
    if has_res:
        ins.append(dres)
        specs.append(row)
    dx, dg = pl.pallas_call(
        body, name=name, grid=(r // tr,), in_specs=specs, out_specs=[row, vec],
        out_shape=[jax.ShapeDtypeStruct((r, d), F32), jax.ShapeDtypeStruct((1, d), F32)],
        compiler_params=_params(("arbitrary",)),
    )(*ins)
    return dx, dg.reshape(d)


def _mm_swiglu(h, w_in, *, name, tm=1024, tn=256):
    t, d = h.shape
    f = w_in.shape[0] // 2
    tm, tn = _tile(t, tm, 16), _tile(f, tn, LANES)
    nj = f // tn

    def body(h_ref, wg_ref, wu_ref, g_ref, u_ref, a_ref):
        hv = h_ref[...]
        gate, up = _dot(hv, wg_ref[...], 1, 1), _dot(hv, wu_ref[...], 1, 1)
        g_ref[...] = gate.astype(BF16)
        u_ref[...] = up.astype(BF16)
        a_ref[...] = (gate * jax.nn.sigmoid(gate) * up).astype(BF16)

    out = pl.BlockSpec((tm, tn), lambda i, j: (i, j))
    return pl.pallas_call(
        body, name=name, grid=(t // tm, nj),
        in_specs=[pl.BlockSpec((tm, d), lambda i, j: (i, 0)), pl.BlockSpec((tn, d), lambda i, j: (j, 0)),
                  pl.BlockSpec((tn, d), lambda i, j: (j + nj, 0))],
        out_specs=[out, out, out], out_shape=[jax.ShapeDtypeStruct((t, f), BF16)] * 3,
        compiler_params=_params(("parallel", "parallel")),
    )(h, w_in, w_in)


def _mm_dswiglu(dy, w_out, gate, up, *, name, scale, tm=1024, tn=256):
    t, d = dy.shape
    f = w_out.shape[0]
    tm, tn = _tile(t, tm, 16), _tile(f, tn, LANES)

    def body(dy_ref, w_ref, g_ref, u_ref, dg_ref, du_ref):
        da = _dot(dy_ref[...].astype(BF16), w_ref[...], 1, 1) * scale
        gv = g_ref[...].astype(F32)
        s = jax.nn.sigmoid(gv)
        dg_ref[...] = (da * u_ref[...].astype(F32) * (s * (1.0 + gv * (1.0 - s)))).astype(BF16)
        du_ref[...] = (da * gv * s).astype(BF16)

    blk = pl.BlockSpec((tm, tn), lambda i, j: (i, j))
    return pl.pallas_call(
        body, name=name, grid=(t // tm, f // tn),
        in_specs=[pl.BlockSpec((tm, d), lambda i, j: (i, 0)), pl.BlockSpec((tn, d), lambda i, j: (j, 0)), blk, blk],
        out_specs=[blk, blk], out_shape=[jax.ShapeDtypeStruct((t, f), BF16)] * 2,
        compiler_params=_params(("parallel", "parallel")),
    )(dy, w_out, gate, up)


def _final_loss(x, g, target, *, name):
    r, d = x.shape
    tr = _tile(r, 512, 8)

    def body(x_ref, g_ref, t_ref, sq_ref, dx_ref, dg_ref):
        xv = x_ref[...]
        gv = g_ref[...]
        rs = lax.rsqrt(jnp.mean(xv * xv, axis=-1, keepdims=True) + NORM_EPS)
        xh = xv * rs
        err = xh * gv - t_ref[...]
        dy = err * (1.0 / d)
        dgh = dy * gv
        dx_ref[...] = rs * (dgh - xh * jnp.mean(dgh * xh, axis=-1, keepdims=True))
        sq = jnp.sum(err * err, axis=0, keepdims=True)
        part = jnp.sum(dy * xh, axis=0, keepdims=True)

        @pl.when(pl.program_id(0) == 0)
        def _():
            sq_ref[...] = sq
            dg_ref[...] = part

        @pl.when(pl.program_id(0) > 0)
        def _():
            sq_ref[...] += sq
            dg_ref[...] += part

    row = pl.BlockSpec((tr, d), lambda i: (i, 0))
    vec = pl.BlockSpec((1, d), lambda i: (0, 0))
    sq, dx, dg = pl.pallas_call(
        body, name=name, grid=(r // tr,), in_specs=[row, vec, row], out_specs=[vec, row, vec],
        out_shape=[jax.ShapeDtypeStruct((1, d), F32), jax.ShapeDtypeStruct((r, d), F32),
                   jax.ShapeDtypeStruct((1, d), F32)],
        compiler_params=_params(("arbitrary",)),
    )(x, g.reshape(1, d), target)
    return sq.reshape(d), dx, dg.reshape(d)


def _scan(a, b, *, name, reverse=False):
    t, w = a.shape
    tb = _tile(t, 1024, 8)
    nblk, ngrp = t // tb, tb // 8

    def body(a_ref, b_ref, h_ref, carry):
        @pl.when(pl.program_id(0) == 0)
        def _():
            carry[...] = jnp.zeros_like(carry)

        row = lax.broadcasted_iota(jnp.int32, (8, w), 0)

        def group(i, c):
            r0 = pl.multiple_of((ngrp - 1 - i if reverse else i) * 8, 8)
            av, bv = a_ref[pl.ds(r0, 8), :], b_ref[pl.ds(r0, 8), :]
            for s in (1, 2, 4):
                keep = row < 8 - s if reverse else row >= s
                shift = 8 - s if reverse else s
                bv = jnp.where(keep, bv + av * pltpu.roll(bv, shift, axis=0), bv)
                av = jnp.where(keep, av * pltpu.roll(av, shift, axis=0), av)
            hv = bv + av * c
            h_ref[pl.ds(r0, 8), :] = hv
            return hv[0:1, :] if reverse else hv[7:8, :]

        carry[0:1, :] = lax.fori_loop(0, ngrp, group, carry[0:1, :])

    blk = pl.BlockSpec((tb, w), lambda i: (nblk - 1 - i if reverse else i, 0))
    return pl.pallas_call(
        body, name=name, grid=(t // tb,), in_specs=[blk, blk], out_specs=blk,
        out_shape=jax.ShapeDtypeStruct((t, w), F32), scratch_shapes=[pltpu.VMEM((8, w), F32)],
        compiler_params=_params(("arbitrary",)),
    )(a, b)


def _row_specs(t, tb, width, col):
    per = tb // 8
    main = pl.BlockSpec((tb, width), lambda i: (i, col))
    before = pl.BlockSpec((8, width), lambda i: (jnp.maximum(i * per - 1, 0), col))
    after = pl.BlockSpec((8, width), lambda i: (jnp.minimum((i + 1) * per, t // 8 - 1), col))
    return main, before, after


def _with_rows_before(x_ref, before_ref):
    return jnp.concatenate([jnp.where(pl.program_id(0) > 0, before_ref[...], 0.0), x_ref[...]], axis=0)


def _tap(xe, s):
    return xe[8:] if s == 0 else pltpu.roll(xe, s, axis=0)[8:]


def _conv_rows(xe, w_ref):
    return sum(_tap(xe, CONV_K - 1 - k) * w_ref[k:k + 1, :] for k in range(CONV_K))


def _conv_bwd(dy, x, col, w, *, name):
    t, width = dy.shape
    tb = _tile(t, 256, 8)
    nblk = t // tb

    def body(dy_ref, dy_after_ref, x_ref, x_before_ref, w_ref, dx_ref, dw_ref):
        i = pl.program_id(0)
        dyv = dy_ref[...]
        dye = jnp.concatenate([dyv, jnp.where(i < nblk - 1, dy_after_ref[...], 0.0)], axis=0)
        dx = dyv * w_ref[CONV_K - 1:CONV_K, :]
        for s in range(1, CONV_K):
            dx = dx + pltpu.roll(dye, tb + 8 - s, axis=0)[:tb] * w_ref[CONV_K - 1 - s:CONV_K - s, :]
        dx_ref[...] = dx.astype(BF16)
        xe = _with_rows_before(x_ref, x_before_ref)

        @pl.when(i == 0)
        def _():
            dw_ref[...] = jnp.zeros_like(dw_ref)

        for k in range(CONV_K):
            dw_ref[k:k + 1, :] += jnp.sum(dyv * _tap(xe, CONV_K - 1 - k), axis=0, keepdims=True)

    main, _, after = _row_specs(t, tb, width, 0)
    xmain, xbefore, _ = _row_specs(t, tb, width, col)
    dx, dw = pl.pallas_call(
        body, name=name, grid=(nblk,),
        in_specs=[main, after, xmain, xbefore, pl.BlockSpec((CONV_K, width), lambda i: (0, 0))],
        out_specs=[main, pl.BlockSpec((8, width), lambda i: (0, 0))],
        out_shape=[jax.ShapeDtypeStruct((t, width), BF16), jax.ShapeDtypeStruct((8, width), F32)],
        compiler_params=_params(("arbitrary",)),
    )(dy, dy, x, x, w)
    return dx, dw[:CONV_K]


def _expm1(x):
    small = x * (1.0 + x * (0.5 + x * (1.0 / 6.0 + x * (1.0 / 24.0 + x * (1.0 / 120.0 + x * (1.0 / 720.0))))))
    return jnp.where(jnp.abs(x) < 0.1, small, jnp.exp(x) - 1.0)


def _lru_gate_terms(xc, wa_ref, ba_ref, wx_ref, bx_ref, sp_ref):
    xb = xc.astype(BF16)
    r = jax.nn.sigmoid(_dot(xb, wa_ref[...], 1, 0) + ba_ref[...])
    i = jax.nn.sigmoid(_dot(xb, wx_ref[...], 1, 0) + bx_ref[...])
    log_a = -r * sp_ref[...]
    return r, i, jnp.exp(log_a), jnp.sqrt(-_expm1(2.0 * log_a))


def _lru_gates_fwd(proj, col, conv_w, conv_b, wa, ba, wx, bx, sp, *, name):
    t = proj.shape[0]
    width = conv_w.shape[1]
    tb = _tile(t, 512, 8)

    def body(x_ref, x_before_ref, cw_ref, cb_ref, wa_ref, ba_ref, wx_ref, bx_ref, sp_ref, a_ref, b_ref, xc_ref):
        xc = _conv_rows(_with_rows_before(x_ref, x_before_ref), cw_ref) + cb_ref[...]
        r, i, a, mult = _lru_gate_terms(xc, wa_ref, ba_ref, wx_ref, bx_ref, sp_ref)
        a_ref[...] = a
        b_ref[...] = mult * i * xc
        xc_ref[...] = xc

    main, before, _ = _row_specs(t, tb, width, col)
    out = pl.BlockSpec((tb, width), lambda i: (i, 0))
    vec = pl.BlockSpec((1, width), lambda i: (0, 0))
    mat = pl.BlockSpec((width, width), lambda i: (0, 0))
    return pl.pallas_call(
        body, name=name, grid=(t // tb,),
        in_specs=[main, before, pl.BlockSpec((CONV_K, width), lambda i: (0, 0)), vec, mat, vec, mat, vec, vec],
        out_specs=[out] * 3, out_shape=[jax.ShapeDtypeStruct((t, width), F32)] * 3,
        compiler_params=_params(("parallel",)),
    )(proj, proj, conv_w, conv_b.reshape(1, -1), wa, ba.reshape(1, -1), wx, bx.reshape(1, -1), sp.reshape(1, -1))


def _lru_gates_bwd(xc, dtot, h_prev, wa, ba, wx, bx, sp, *, name):
    t, width = xc.shape
    tb = _tile(t, 512, 8)

    def body(xc_ref, dt_ref, hp_ref, wa_ref, ba_ref, wx_ref, bx_ref, sp_ref, dxc_ref, dwa_ref, dwx_ref, vec_ref):
        xc = xc_ref[...]
        r, i, a, mult = _lru_gate_terms(xc, wa_ref, ba_ref, wx_ref, bx_ref, sp_ref)
        db = dt_ref[...]
        d_la = db * hp_ref[...] * a - db * i * xc * (a * a / mult)
        d_pa = (-d_la * sp_ref[...]) * r * (1.0 - r)
        d_pi = db * mult * xc * i * (1.0 - i)
        dab, dib = d_pa.astype(BF16), d_pi.astype(BF16)
        dxc = db * mult * i + _dot(dab, wa_ref[...], 1, 1) + _dot(dib, wx_ref[...], 1, 1)
        dxc_ref[...] = dxc
        xb = xc.astype(BF16)
        rows = [jnp.sum(z, axis=0, keepdims=True) for z in (d_pa, d_pi, -d_la * r, dxc)]

        @pl.when(pl.program_id(0) == 0)
        def _():
            dwa_ref[...] = jnp.zeros_like(dwa_ref)
            dwx_ref[...] = jnp.zeros_like(dwx_ref)
            vec_ref[...] = jnp.zeros_like(vec_ref)

        dwa_ref[...] += _dot(xb, dab, 0, 0)
        dwx_ref[...] += _dot(xb, dib, 0, 0)
        for j, z in enumerate(rows):
            vec_ref[j:j + 1, :] += z

    blk = pl.BlockSpec((tb, width), lambda i: (i, 0))
    vec = pl.BlockSpec((1, width), lambda i: (0, 0))
    mat = pl.BlockSpec((width, width), lambda i: (0, 0))
    dxc, dwa, dwx, vecs = pl.pallas_call(
        body, name=name, grid=(t // tb,), in_specs=[blk, blk, blk, mat, vec, mat, vec, vec],
        out_specs=[blk, mat, mat, pl.BlockSpec((8, width), lambda i: (0, 0))],
        out_shape=[jax.ShapeDtypeStruct((t, width), F32), jax.ShapeDtypeStruct((width, width), F32),
                   jax.ShapeDtypeStruct((width, width), F32), jax.ShapeDtypeStruct((8, width), F32)],
        compiler_params=_params(("arbitrary",)),
    )(xc, dtot, h_prev, wa, ba.reshape(1, -1), wx, bx.reshape(1, -1), sp.reshape(1, -1))
    return dxc, dwa, dwx, vecs[:4]


GELU_C = math.sqrt(2.0 / math.pi)


def _gelu_terms(x):
    th = jnp.tanh(GELU_C * (x + 0.044715 * x * x * x))
    return 0.5 * x * (1.0 + th), 0.5 * (1.0 + th) + 0.5 * x * (1.0 - th * th) * GELU_C * (1.0 + 3 * 0.044715 * x * x)


def _mix_join_fwd(outs, lses, hs, proj, gr_col, *, name):
    t, w = hs.shape
    tb = _tile(t, 512, 16)
    n = len(outs)

    def body(*refs):
        o_refs, l_refs = refs[:n], refs[n:2 * n]
        hs_ref, gr_ref, cat_ref, attn_ref, lse_ref = refs[2 * n:]
        ls = [r[...] for r in l_refs]
        m = ls[0]
        for l in ls[1:]:
            m = jnp.maximum(m, l)
        ws = [jnp.exp(l - m) for l in ls]
        den = sum(ws)
        attn = sum(wt * r[...] for wt, r in zip(ws, o_refs)) / den
        attn_ref[...] = attn
        lse_ref[...] = m + jnp.log(den)
        cat_ref[:, :w] = attn.astype(BF16)
        cat_ref[:, w:] = (hs_ref[...] * _gelu_terms(gr_ref[...])[0]).astype(BF16)

    blk = pl.BlockSpec((tb, w), lambda i: (i, 0))
    return pl.pallas_call(
        body, name=name, grid=(t // tb,),
        in_specs=[blk] * (2 * n + 1) + [pl.BlockSpec((tb, w), lambda i: (i, gr_col))],
        out_specs=[pl.BlockSpec((tb, 2 * w), lambda i: (i, 0)), blk, blk],
        out_shape=[jax.ShapeDtypeStruct((t, 2 * w), BF16), jax.ShapeDtypeStruct((t, w), F32),
                   jax.ShapeDtypeStruct((t, w), F32)],
        compiler_params=_params(("parallel",)),
    )(*outs, *lses, hs, proj)


def _mix_join_bwd(dcat, attn, hs, proj, gr_col, *, name):
    t, w = hs.shape
    hd = w // ATTN_HEADS
    tb = _tile(t, 512, 16)

    def body(dcat_ref, attn_ref, hs_ref, gr_ref, delta_ref, dhs_ref, dgr_ref):
        for h in range(ATTN_HEADS):
            sl = slice(h * hd, (h + 1) * hd)
            d = jnp.sum(dcat_ref[:, sl] * attn_ref[:, sl], axis=-1, keepdims=True)
            delta_ref[:, sl] = jnp.broadcast_to(d, (tb, hd))
        dy = dcat_ref[:, w:]
        g, dg = _gelu_terms(gr_ref[...])
        dhs_ref[...] = dy * g
        dgr_ref[...] = (dy * hs_ref[...] * dg).astype(BF16)

    blk = pl.BlockSpec((tb, w), lambda i: (i, 0))
    return pl.pallas_call(
        body, name=name, grid=(t // tb,),
        in_specs=[pl.BlockSpec((tb, 2 * w), lambda i: (i, 0)), blk, blk, pl.BlockSpec((tb, w), lambda i: (i, gr_col))],
        out_specs=[blk, blk, blk],
        out_shape=[jax.ShapeDtypeStruct((t, w), F32), jax.ShapeDtypeStruct((t, w), F32),
                   jax.ShapeDtypeStruct((t, w), BF16)],
        compiler_params=_params(("parallel",)),
    )(dcat, attn, hs, proj)


def _silu_terms(x):
    s = jax.nn.sigmoid(x)
    return x * s, s * (1.0 + x * (1.0 - s))


def _dn_prep_fwd(proj, conv_w, *, name):
    t = proj.shape[0]
    w3 = conv_w.shape[1]
    w = w3 // 3
    hd = w // DN_HEADS
    tb = _tile(t, 256, 8)

    def body(x_ref, x_before_ref, cw_ref, q_ref, k_ref, v_ref, c_ref):
        c = _conv_rows(_with_rows_before(x_ref, x_before_ref), cw_ref)
        c_ref[...] = c
        s = _silu_terms(c)[0]
        v_ref[...] = s[:, 2 * w:]
        for part, ref, scale in ((0, q_ref, hd ** -0.5), (1, k_ref, 1.0)):
            for h in range(DN_HEADS):
                z = s[:, part * w + h * hd:part * w + (h + 1) * hd]
                ref[:, h * hd:(h + 1) * hd] = z * (lax.rsqrt(jnp.sum(z * z, axis=-1, keepdims=True) + 1e-6) * scale)

    main, before, _ = _row_specs(t, tb, w3, 0)
    out = pl.BlockSpec((tb, w), lambda i: (i, 0))
    return pl.pallas_call(
        body, name=name, grid=(t // tb,), in_specs=[main, before, pl.BlockSpec((CONV_K, w3), lambda i: (0, 0))],
        out_specs=[out, out, out, pl.BlockSpec((tb, w3), lambda i: (i, 0))],
        out_shape=[jax.ShapeDtypeStruct((t, w), F32)] * 3 + [jax.ShapeDtypeStruct((t, w3), F32)],
        compiler_params=_params(("parallel",)),
    )(proj, proj, conv_w)


def _dn_prep_bwd(c, dq, dk, dv, *, name):
    t, w3 = c.shape
    w = w3 // 3
    hd = w // DN_HEADS
    tb = _tile(t, 256, 8)

    def body(c_ref, dq_ref, dk_ref, dv_ref, dc_ref):
        cv = c_ref[...]
        s, ds = _silu_terms(cv)
        dc_ref[:, 2 * w:] = dv_ref[...] * ds[:, 2 * w:]
        for part, ref, scale in ((0, dq_ref, hd ** -0.5), (1, dk_ref, 1.0)):
            for h in range(DN_HEADS):
                cols = slice(part * w + h * hd, part * w + (h + 1) * hd)
                z = s[:, cols]
                rn = lax.rsqrt(jnp.sum(z * z, axis=-1, keepdims=True) + 1e-6)
                y = z * rn
                dy = ref[:, h * hd:(h + 1) * hd] * scale
                dc_ref[:, cols] = rn * (dy - y * jnp.sum(dy * y, axis=-1, keepdims=True)) * ds[:, cols]

    blk = pl.BlockSpec((tb, w), lambda i: (i, 0))
    wide = pl.BlockSpec((tb, w3), lambda i: (i, 0))
    return pl.pallas_call(
        body, name=name, grid=(t // tb,), in_specs=[wide, blk, blk, blk], out_specs=wide,
        out_shape=jax.ShapeDtypeStruct((t, w3), F32), compiler_params=_params(("parallel",)),
    )(c, dq, dk, dv)


def _dn_gate_fwd(o, proj, z_col, o_norm, *, name):
    t, w = o.shape
    hd = w // DN_HEADS
    tb = _tile(t, 512, 16)

    def body(o_ref, z_ref, g_ref, y_ref):
        for h in range(DN_HEADS):
            sl = slice(h * hd, (h + 1) * hd)
            ov = o_ref[:, sl]
            rn = lax.rsqrt(jnp.mean(ov * ov, axis=-1, keepdims=True) + NORM_EPS)
            y_ref[:, sl] = (ov * rn * g_ref[...] * _silu_terms(z_ref[:, sl])[0]).astype(BF16)

    blk = pl.BlockSpec((tb, w), lambda i: (i, 0))
    return pl.pallas_call(
        body, name=name, grid=(t // tb,),
        in_specs=[blk, pl.BlockSpec((tb, w), lambda i: (i, z_col)), pl.BlockSpec((1, hd), lambda i: (0, 0))],
        out_specs=blk, out_shape=jax.ShapeDtypeStruct((t, w), BF16), compiler_params=_params(("parallel",)),
    )(o, proj, o_norm.reshape(1, hd))


def _dn_gate_bwd(o, proj, z_col, o_norm, dy, *, name):
    t, w = o.shape
    hd = w // DN_HEADS
    tb = _tile(t, 512, 16)

    def body(o_ref, z_ref, g_ref, dy_ref, do_ref, dz_ref, dg_ref):
        gv = g_ref[...]
        dg = jnp.zeros((1, hd), F32)
        for h in range(DN_HEADS):
            sl = slice(h * hd, (h + 1) * hd)
            ov, dyv = o_ref[:, sl], dy_ref[:, sl]
            sz, dsz = _silu_terms(z_ref[:, sl])
            rn = lax.rsqrt(jnp.mean(ov * ov, axis=-1, keepdims=True) + NORM_EPS)
            nv = ov * rn
            dz_ref[:, sl] = (dyv * nv * gv * dsz).astype(BF16)
            dn = dyv * gv * sz
            do_ref[:, sl] = rn * (dn - nv * jnp.mean(dn * nv, axis=-1, keepdims=True))
            dg = dg + jnp.sum(dyv * nv * sz, axis=0, keepdims=True)

        @pl.when(pl.program_id(0) == 0)
        def _():
            dg_ref[...] = dg

        @pl.when(pl.program_id(0) > 0)
        def _():
            dg_ref[...] += dg

    blk = pl.BlockSpec((tb, w), lambda i: (i, 0))
    vec = pl.BlockSpec((1, hd), lambda i: (0, 0))
    do, dz, dg = pl.pallas_call(
        body, name=name, grid=(t // tb,), in_specs=[blk, pl.BlockSpec((tb, w), lambda i: (i, z_col)), vec, blk],
        out_specs=[blk, blk, vec],
        out_shape=[jax.ShapeDtypeStruct((t, w), F32), jax.ShapeDtypeStruct((t, w), BF16),
                   jax.ShapeDtypeStruct((1, hd), F32)],
        compiler_params=_params(("arbitrary",)),
    )(o, proj, o_norm.reshape(1, hd), dy)
    return do, dz, dg.reshape(hd)


def _band_masks(n):
    qi = lax.broadcasted_iota(jnp.int32, (ATTN_BLOCK, ATTN_BLOCK), 0)
    kj = lax.broadcasted_iota(jnp.int32, (ATTN_BLOCK, ATTN_BLOCK), 1)
    return kj <= qi, jnp.logical_and(kj >= qi, n > 0)


def _dattn_fwd(q, k, v, dil, *, name):
    t, w = q.shape
    hd = w // ATTN_HEADS
    seq = t // dil
    assert seq % ATTN_BLOCK == 0
    nb = seq // ATTN_BLOCK
    scale = hd ** -0.5

    def body(q_ref, kp_ref, kc_ref, vp_ref, vc_ref, o_ref, lse_ref):
        mc, mp = _band_masks(pl.program_id(1))
        heads = range(ATTN_HEADS)
        sls = [slice(h * hd, (h + 1) * hd) for h in heads]
        qs = [(q_ref[:, sl] * scale).astype(BF16) for sl in sls]
        scs = [jnp.where(mc, _dot(qs[h], kc_ref[:, sls[h]].astype(BF16), 1, 1), NEG) for h in heads]
        sps = [jnp.where(mp, _dot(qs[h], kp_ref[:, sls[h]].astype(BF16), 1, 1), NEG) for h in heads]
        ms = [jnp.maximum(jnp.max(sc, axis=-1, keepdims=True), jnp.max(sp, axis=-1, keepdims=True))
              for sc, sp in zip(scs, sps)]
        pcs = [jnp.exp(sc - m) for sc, m in zip(scs, ms)]
        pps = [jnp.exp(sp - m) for sp, m in zip(sps, ms)]
        dens = [jnp.sum(pc, axis=-1, keepdims=True) + jnp.sum(pp, axis=-1, keepdims=True) for pc, pp in zip(pcs, pps)]
        outs = [_dot(pcs[h].astype(BF16), vc_ref[:, sls[h]].astype(BF16), 1, 0)
                + _dot(pps[h].astype(BF16), vp_ref[:, sls[h]].astype(BF16), 1, 0) for h in heads]
        for h in heads:
            o_ref[:, sls[h]] = outs[h] / dens[h]
            lse_ref[:, sls[h]] = jnp.broadcast_to(ms[h] + jnp.log(dens[h]), (ATTN_BLOCK, hd))

    cur = pl.BlockSpec((ATTN_BLOCK, w), lambda r, n: (n, r))
    prev = pl.BlockSpec((ATTN_BLOCK, w), lambda r, n: (jnp.maximum(n - 1, 0), r))
    view = lambda z: z.reshape(seq, dil * w)
    o, lse = pl.pallas_call(
        body, name=name, grid=(dil, nb), in_specs=[cur, prev, cur, prev, cur], out_specs=[cur, cur],
        out_shape=[jax.ShapeDtypeStruct((seq, dil * w), F32)] * 2, compiler_params=_params(("parallel", "parallel")),
    )(view(q), view(k), view(k), view(v), view(v))
    return o.reshape(t, w), lse.reshape(t, w)


def _dattn_bwd(q, k, v, do, lse, delta, dil, *, name):
    t, w = q.shape
    hd = w // ATTN_HEADS
    seq = t // dil
    nb = seq // ATTN_BLOCK
    scale = hd ** -0.5

    def body(qc_ref, qn_ref, doc_ref, don_ref, lc_ref, ln_ref, dc_ref, dn_ref, kp_ref, kc_ref, vp_ref, vc_ref,
             dq_ref, dk_ref, dv_ref):
        n = pl.program_id(1)
        mc, mp = _band_masks(n)
        _, mx = _band_masks(jnp.where(n + 1 < nb, 1, 0))
        heads = range(ATTN_HEADS)
        sls = [slice(h * hd, (h + 1) * hd) for h in heads]
        ones = [slice(h * hd, h * hd + 1) for h in heads]
        bf = lambda ref, scl=None: [(ref[:, sl] if scl is None else ref[:, sl] * scl).astype(BF16) for sl in sls]
        qc, qn, kc, kp = bf(qc_ref, scale), bf(qn_ref, scale), bf(kc_ref), bf(kp_ref)
        vc, vp, doc, don = bf(vc_ref), bf(vp_ref), bf(doc_ref), bf(don_ref)
        p_c = [jnp.exp(jnp.where(mc, _dot(qc[h], kc[h], 1, 1), NEG) - lc_ref[:, ones[h]]) for h in heads]
        p_p = [jnp.exp(jnp.where(mp, _dot(qc[h], kp[h], 1, 1), NEG) - lc_ref[:, ones[h]]) for h in heads]
        p_x = [jnp.exp(jnp.where(mx, _dot(qn[h], kc[h], 1, 1), NEG) - ln_ref[:, ones[h]]) for h in heads]
        ds_c = [(p_c[h] * (_dot(doc[h], vc[h], 1, 1) - dc_ref[:, ones[h]])).astype(BF16) for h in heads]
        ds_p = [(p_p[h] * (_dot(doc[h], vp[h], 1, 1) - dc_ref[:, ones[h]])).astype(BF16) for h in heads]
        ds_x = [(p_x[h] * (_dot(don[h], vc[h], 1, 1) - dn_ref[:, ones[h]])).astype(BF16) for h in heads]
        for h in heads:
            dq_ref[:, sls[h]] = (_dot(ds_c[h], kc[h], 1, 0) + _dot(ds_p[h], kp[h], 1, 0)) * scale
            dk_ref[:, sls[h]] = _dot(ds_c[h], qc[h], 0, 0) + _dot(ds_x[h], qn[h], 0, 0)
            dv_ref[:, sls[h]] = _dot(p_c[h].astype(BF16), doc[h], 0, 0) + _dot(p_x[h].astype(BF16), don[h], 0, 0)

    cur = pl.BlockSpec((ATTN_BLOCK, w), lambda r, n: (n, r))
    prev = pl.BlockSpec((ATTN_BLOCK, w), lambda r, n: (jnp.maximum(n - 1, 0), r))
    nxt = pl.BlockSpec((ATTN_BLOCK, w), lambda r, n: (jnp.minimum(n + 1, nb - 1), r))
    view = lambda z: z.reshape(seq, dil * w)
    q, k, v, do, lse, delta = (view(z) for z in (q, k, v, do, lse, delta))
    dq, dk, dv = pl.pallas_call(
        body, name=name, grid=(dil, nb),
        in_specs=[cur, nxt, cur, nxt, cur, nxt, cur, nxt, prev, cur, prev, cur], out_specs=[cur, cur, cur],
        out_shape=[jax.ShapeDtypeStruct((seq, dil * w), F32)] * 3, compiler_params=_params(("parallel", "parallel")),
    )(q, q, do, do, lse, lse, delta, delta, k, k, v, v)
    return dq.reshape(t, w), dk.reshape(t, w), dv.reshape(t, w)


def _xattn_fwd(q, k, v, *, name):
    t, w = q.shape
    nm = k.shape[0]
    hd = w // XA_HEADS
    scale = hd ** -0.5
    tq = _tile(t, 512, 8)

    def body(q_ref, k_ref, v_ref, o_ref):
        for h in range(XA_HEADS):
            sl = slice(h * hd, (h + 1) * hd)
            s = _dot((q_ref[:, sl] * scale).astype(BF16), k_ref[:, sl].astype(BF16), 1, 1)
            p = jnp.exp(s - jnp.max(s, axis=-1, keepdims=True))
            p = p / jnp.sum(p, axis=-1, keepdims=True)
            o_ref[:, sl] = _dot(p.astype(BF16), v_ref[:, sl].astype(BF16), 1, 0)

    qs = pl.BlockSpec((tq, w), lambda i: (i, 0))
    ks = pl.BlockSpec((nm, w), lambda i: (0, 0))
    return pl.pallas_call(
        body, name=name, grid=(t // tq,), in_specs=[qs, ks, ks], out_specs=qs,
        out_shape=jax.ShapeDtypeStruct((t, w), F32), compiler_params=_params(("parallel",)),
    )(q, k, v)


def _xattn_bwd(q, k, v, do, *, name):
    t, w = q.shape
    nm = k.shape[0]
    hd = w // XA_HEADS
    scale = hd ** -0.5
    tq = _tile(t, 512, 8)

    def body(q_ref, k_ref, v_ref, do_ref, dq_ref, dk_ref, dv_ref):
        first = pl.program_id(0) == 0
        for h in range(XA_HEADS):
            sl = slice(h * hd, (h + 1) * hd)
            qh = (q_ref[:, sl] * scale).astype(BF16)
            kh, vh, doh = k_ref[:, sl].astype(BF16), v_ref[:, sl].astype(BF16), do_ref[:, sl].astype(BF16)
            s = _dot(qh, kh, 1, 1)
            p = jnp.exp(s - jnp.max(s, axis=-1, keepdims=True))
            p = p / jnp.sum(p, axis=-1, keepdims=True)
            dp = _dot(doh, vh, 1, 1)
            ds = (p * (dp - jnp.sum(p * dp, axis=-1, keepdims=True))).astype(BF16)
            dq_ref[:, sl] = _dot(ds, kh, 1, 0) * scale
            dk = _dot(ds, qh, 0, 0)
            dv = _dot(p.astype(BF16), doh, 0, 0)

            @pl.when(first)
            def _():
                dk_ref[:, sl] = dk
                dv_ref[:, sl] = dv

            @pl.when(jnp.logical_not(first))
            def _():
                dk_ref[:, sl] += dk
                dv_ref[:, sl] += dv

    qs = pl.BlockSpec((tq, w), lambda i: (i, 0))
    ks = pl.BlockSpec((nm, w), lambda i: (0, 0))
    return pl.pallas_call(
        body, name=name, grid=(t // tq,), in_specs=[qs, ks, ks, qs], out_specs=[qs, ks, ks],
        out_shape=[jax.ShapeDtypeStruct((t, w), F32), jax.ShapeDtypeStruct((nm, w), F32),
                   jax.ShapeDtypeStruct((nm, w), F32)],
        compiler_params=_params(("arbitrary",)),
    )(q, k, v, do)


def _dn_chunk_terms(qh, kh, gcc, gcr, bh):
    c = DN_CHUNK
    row = lax.broadcasted_iota(jnp.int32, (c, c), 0)
    col = lax.broadcasted_iota(jnp.int32, (c, c), 1)
    decay = jnp.exp(jnp.where(row >= col, gcc - gcr, NEG))
    kb = kh * bh
    kkt = _dot(kb, kh, 1, 1, HI)
    qkt = _dot(qh, kh, 1, 1, HI)
    return row, col, decay, kb, kkt, qkt


def _unit_lower_inverse(nm, row, col):
    eye = (row == col).astype(F32)
    inv = eye - nm
    p = nm
    for _ in range(int(math.log2(DN_CHUNK)) - 1):
        p = _dot(p, p, 1, 0, HI)
        inv = inv + _dot(inv, p, 1, 0, HI)
    return inv


def _dn_fwd(q, k, v, gc, gcr, beta, *, name):
    t, w = q.shape
    hd = w // DN_HEADS
    c = DN_CHUNK
    nch = t // c

    def body(q_ref, k_ref, v_ref, gc_ref, gcr_ref, b_ref, o_ref, s_ref, inv_ref, state):
        @pl.when(pl.program_id(0) == 0)
        def _():
            state[...] = jnp.zeros_like(state)

        for h in range(DN_HEADS):
            sl = slice(h * hd, (h + 1) * hd)
            qh, kh, vh = q_ref[:, sl], k_ref[:, sl], v_ref[:, sl]
            gcc, gcr_h, bh = gc_ref[:, h:h + 1], gcr_ref[0, h:h + 1, :], b_ref[:, h:h + 1]
            row, col, decay, kb, kkt, qkt = _dn_chunk_terms(qh, kh, gcc, gcr_h, bh)
            inv = _unit_lower_inverse(jnp.where(row > col, kkt * decay, 0.0), row, col)
            e = jnp.exp(gcc)
            u = _dot(inv, vh * bh, 1, 0, HI)
            wm = _dot(inv, kb * e, 1, 0, HI)
            sh = state[h]
            v_new = u - _dot(wm, sh, 1, 0, HI)
            o_ref[:, sl] = _dot(qh * e, sh, 1, 0, HI) + _dot(qkt * decay, v_new, 1, 0, HI)
            s_ref[0, h] = sh
            inv_ref[0, h] = inv
            gl = gcc[c - 1:c, :]
            state[h] = sh * jnp.exp(gl) + _dot(kh * jnp.exp(gl - gcc), v_new, 0, 0, HI)

    rows = pl.BlockSpec((c, w), lambda n: (n, 0))
    cols = pl.BlockSpec((c, DN_HEADS), lambda n: (n, 0))
    rowg = pl.BlockSpec((1, DN_HEADS, c), lambda n: (n, 0, 0))
    return pl.pallas_call(
        body, name=name, grid=(nch,), in_specs=[rows, rows, rows, cols, rowg, cols],
        out_specs=[rows, pl.BlockSpec((1, DN_HEADS, hd, hd), lambda n: (n, 0, 0, 0)),
                   pl.BlockSpec((1, DN_HEADS, c, c), lambda n: (n, 0, 0, 0))],
        out_shape=[jax.ShapeDtypeStruct((t, w), F32), jax.ShapeDtypeStruct((nch, DN_HEADS, hd, hd), F32),
                   jax.ShapeDtypeStruct((nch, DN_HEADS, c, c), F32)],
        scratch_shapes=[pltpu.VMEM((DN_HEADS, hd, hd), F32)], compiler_params=_params(("arbitrary",)),
    )(q, k, v, gc, gcr, beta)


def _dn_bwd(q, k, v, gc, gcr, beta, states, invs, do, *, name):
    t, w = q.shape
    hd = w // DN_HEADS
    c = DN_CHUNK
    nch = t // c

    def body(q_ref, k_ref, v_ref, gc_ref, gcr_ref, b_ref, s_ref, inv_ref, do_ref,
             dq_ref, dk_ref, dv_ref, dgc_ref, dgr_ref, db_ref, dstate):
        @pl.when(pl.program_id(0) == 0)
        def _():
            dstate[...] = jnp.zeros_like(dstate)

        for h in range(DN_HEADS):
            sl = slice(h * hd, (h + 1) * hd)
            qh, kh, vh, doh = q_ref[:, sl], k_ref[:, sl], v_ref[:, sl], do_ref[:, sl]
            gcc, gcr_h, bh = gc_ref[:, h:h + 1], gcr_ref[0, h:h + 1, :], b_ref[:, h:h + 1]
            row, col, decay, kb, kkt, qkt = _dn_chunk_terms(qh, kh, gcc, gcr_h, bh)
            inv, sh, dsn = inv_ref[0, h], s_ref[0, h], dstate[h]
            e = jnp.exp(gcc)
            gl = gcc[c - 1:c, :]
            el, r = jnp.exp(gl), jnp.exp(gl - gcc)
            u = _dot(inv, vh * bh, 1, 0, HI)
            wm = _dot(inv, kb * e, 1, 0, HI)
            v_new = u - _dot(wm, sh, 1, 0, HI)
            qk = qkt * decay
            kr = kh * r
            qe = qh * e
            d_qe = _dot(doh, sh, 1, 1, HI)
            d_qk = _dot(doh, v_new, 1, 1, HI)
            d_vnew = _dot(qk, doh, 0, 0, HI) + _dot(kr, dsn, 1, 0, HI)
            d_el = jnp.sum(jnp.sum(dsn * sh, axis=1, keepdims=True), axis=0, keepdims=True)
            d_kr = _dot(v_new, dsn, 1, 1, HI)
            d_w = -_dot(d_vnew, sh, 1, 1, HI)
            dstate[h] = dsn * el + _dot(qe, doh, 0, 0, HI) - _dot(wm, d_vnew, 0, 0, HI)
            d_ru = _dot(inv, d_vnew, 0, 0, HI)
            d_rw = _dot(inv, d_w, 0, 0, HI)
            d_a = -(_dot(d_ru, u, 1, 1, HI) + _dot(d_rw, wm, 1, 1, HI))
            dv_ref[:, sl] = d_ru * bh
            d_kb = d_rw * e
            d_e = jnp.sum(d_rw * kb, axis=1, keepdims=True) + jnp.sum(d_qe * qh, axis=1, keepdims=True)
            d_n = jnp.where(row > col, d_a, 0.0)
            d_m = d_n * decay
            d_p = d_qk * decay
            d_kb = d_kb + _dot(d_m, kh, 1, 0, HI)
            dk = _dot(d_m, kb, 0, 0, HI) + _dot(d_p, qh, 0, 0, HI)
            dq_ref[:, sl] = _dot(d_p, kh, 1, 0, HI) + d_qe * e
            dd = (d_n * kkt + d_qk * qkt) * decay
            d_r = jnp.sum(d_kr * kh, axis=1, keepdims=True)
            d_gl = d_el * el + jnp.sum(d_r * r, axis=0, keepdims=True)
            last = lax.broadcasted_iota(jnp.int32, (c, 1), 0) == c - 1
            dgc_ref[:, h:h + 1] = (jnp.sum(dd, axis=1, keepdims=True) + d_e * e - d_r * r + jnp.where(last, d_gl, 0.0))
            dgr_ref[0, h:h + 1, :] = -jnp.sum(dd, axis=0, keepdims=True)
            dk_ref[:, sl] = dk + d_kr * r + d_kb * bh
            db_ref[:, h:h + 1] = jnp.sum(d_ru * vh, axis=1, keepdims=True) + jnp.sum(d_kb * kh, axis=1, keepdims=True)

    rev = lambda n: nch - 1 - n
    rows = pl.BlockSpec((c, w), lambda n: (rev(n), 0))
    cols = pl.BlockSpec((c, DN_HEADS), lambda n: (rev(n), 0))
    rowg = pl.BlockSpec((1, DN_HEADS, c), lambda n: (rev(n), 0, 0))
    st = pl.BlockSpec((1, DN_HEADS, hd, hd), lambda n: (rev(n), 0, 0, 0))
    iv = pl.BlockSpec((1, DN_HEADS, c, c), lambda n: (rev(n), 0, 0, 0))
    return pl.pallas_call(
        body, name=name, grid=(nch,), in_specs=[rows, rows, rows, cols, rowg, cols, st, iv, rows],
        out_specs=[rows, rows, rows, cols, rowg, cols],
        out_shape=[jax.ShapeDtypeStruct((t, w), F32)] * 3
        + [jax.ShapeDtypeStruct((t, DN_HEADS), F32), jax.ShapeDtypeStruct((nch, DN_HEADS, c), F32),
           jax.ShapeDtypeStruct((t, DN_HEADS), F32)],
        scratch_shapes=[pltpu.VMEM((DN_HEADS, hd, hd), F32)], compiler_params=_params(("arbitrary",)),
    )(q, k, v, gc, gcr, beta, states, invs, do)


def _dn_head_terms(q_ref, k_ref, gc_ref, gcr_ref, b_ref, h, hd):
    c = DN_CHUNK
    sl = slice(h * hd, (h + 1) * hd)
    qh, kh = q_ref[:, sl], k_ref[:, sl]
    gcc, gcr_h, bh = gc_ref[:, h:h + 1], gcr_ref[0, h:h + 1, :], b_ref[:, h:h + 1]
    row = lax.broadcasted_iota(jnp.int32, (c, c), 0)
    col = lax.broadcasted_iota(jnp.int32, (c, c), 1)
    decay = jnp.exp(jnp.where(row >= col, gcc - gcr_h, NEG))
    kb = kh * bh
    kkt = _dot(kb.astype(BF16), kh.astype(BF16), 1, 1)
    qkt = _dot(qh.astype(BF16), kh.astype(BF16), 1, 1)
    gl = gcc[c - 1:c, :]
    return dict(sl=sl, q=qh, k=kh, gcc=gcc, b=bh, row=row, col=col, decay=decay, kb=kb, kkt=kkt, qkt=qkt,
                e=jnp.exp(gcc), el=jnp.exp(gl), r=jnp.exp(gl - gcc))


def _dn_local_fwd(q, k, v, gc, gcr, beta, *, name):
    t, w = q.shape
    hd = w // DN_HEADS
    c = DN_CHUNK
    nch = t // c
    heads = range(DN_HEADS)

    def body(q_ref, k_ref, v_ref, gc_ref, gcr_ref, b_ref,
             u_ref, wq_ref, wqt_ref, kr_ref, krt_ref, qk_ref, qkt_ref, invt_ref):
        tm = [_dn_head_terms(q_ref, k_ref, gc_ref, gcr_ref, b_ref, h, hd) for h in heads]
        pw = [jnp.where(m['row'] > m['col'], m['kkt'] * m['decay'], 0.0) for m in tm]
        inv = [(m['row'] == m['col']).astype(F32) - p for m, p in zip(tm, pw)]
        for _ in range(int(math.log2(c)) - 1):
            pw = [_dot(p.astype(BF16), p.astype(BF16), 1, 0) for p in pw]
            inv = [i + _dot(i.astype(BF16), p.astype(BF16), 1, 0) for i, p in zip(inv, pw)]
        for h, m in zip(heads, tm):
            rhs = jnp.concatenate([v_ref[:, m['sl']] * m['b'], m['kb'] * m['e']], axis=1).astype(BF16)
            sol = _dot(inv[h].astype(BF16), rhs, 1, 0)
            u_ref[:, m['sl']] = sol[:, :hd]
            wq = jnp.concatenate([sol[:, hd:], m['q'] * m['e']], axis=0)
            kr = m['k'] * m['r']
            qk = m['qkt'] * m['decay']
            wq_ref[0, h], wqt_ref[0, h] = wq.astype(BF16), wq.T.astype(BF16)
            kr_ref[0, h], krt_ref[0, h] = kr.astype(BF16), kr.T.astype(BF16)
            qk_ref[0, h], qkt_ref[0, h] = qk.astype(BF16), qk.T.astype(BF16)
            invt_ref[0, h] = inv[h].T.astype(BF16)

    rows = pl.BlockSpec((c, w), lambda n: (n, 0))
    cols = pl.BlockSpec((c, DN_HEADS), lambda n: (n, 0))
    rowg = pl.BlockSpec((1, DN_HEADS, c), lambda n: (n, 0, 0))
    per = lambda a, b: (pl.BlockSpec((1, DN_HEADS, a, b), lambda n: (n, 0, 0, 0)),
                        jax.ShapeDtypeStruct((nch, DN_HEADS, a, b), BF16))
    outs = [(rows, jax.ShapeDtypeStruct((t, w), F32)), per(2 * c, hd), per(hd, 2 * c), per(c, hd), per(hd, c),
            per(c, c), per(c, c), per(c, c)]
    return pl.pallas_call(
        body, name=name, grid=(nch,), in_specs=[rows, rows, rows, cols, rowg, cols],
        out_specs=[o[0] for o in outs], out_shape=[o[1] for o in outs], compiler_params=_params(("parallel",)),
    )(q, k, v, gc, gcr, beta)


def _dn_state_fwd(u, wq, krt, qk, gc, *, name):
    t, w = u.shape
    hd = w // DN_HEADS
    c = DN_CHUNK
    nch = t // c

    def body(u_ref, wq_ref, krt_ref, qk_ref, gc_ref, o_ref, vn_ref, s_ref, state):
        @pl.when(pl.program_id(0) == 0)
        def _():
            state[...] = jnp.zeros_like(state)

        heads = range(DN_HEADS)
        sls = [slice(h * hd, (h + 1) * hd) for h in heads]
        shs = [state[h] for h in heads]
        wss = [_dot(wq_ref[0, h], shs[h].astype(BF16), 1, 0) for h in heads]
        vns = [(u_ref[:, sls[h]] - wss[h][:c]).astype(BF16) for h in heads]
        outs = [wss[h][c:] + _dot(qk_ref[0, h], vns[h], 1, 0) for h in heads]
        nxt = [shs[h] * jnp.exp(gc_ref[c - 1:c, h:h + 1]) + _dot(krt_ref[0, h], vns[h], 1, 0) for h in heads]
        for h in heads:
            s_ref[0, h] = shs[h]
            vn_ref[:, sls[h]] = vns[h]
            o_ref[:, sls[h]] = outs[h]
            state[h] = nxt[h]

    rows = pl.BlockSpec((c, w), lambda n: (n, 0))
    per = lambda a, b: pl.BlockSpec((1, DN_HEADS, a, b), lambda n: (n, 0, 0, 0))
    return pl.pallas_call(
        body, name=name, grid=(nch,),
        in_specs=[rows, per(2 * c, hd), per(hd, c), per(c, c), pl.BlockSpec((c, DN_HEADS), lambda n: (n, 0))],
        out_specs=[rows, rows, per(hd, hd)],
        out_shape=[jax.ShapeDtypeStruct((t, w), F32), jax.ShapeDtypeStruct((t, w), BF16),
                   jax.ShapeDtypeStruct((nch, DN_HEADS, hd, hd), F32)],
        scratch_shapes=[pltpu.VMEM((DN_HEADS, hd, hd), F32)], compiler_params=_params(("arbitrary",)),
    )(u, wq, krt, qk, gc)


def _dn_state_bwd(do, qkt, kr, vn, states, wqt, gc, *, name):
    t, w = do.shape
    hd = w // DN_HEADS
    c = DN_CHUNK
    nch = t // c

    def body(do_ref, qkt_ref, kr_ref, vn_ref, s_ref, wqt_ref, gc_ref, dvn_ref, dkr_ref, del_ref, dstate):
        @pl.when(pl.program_id(0) == 0)
        def _():
            dstate[...] = jnp.zeros_like(dstate)

        heads = range(DN_HEADS)
        sls = [slice(h * hd, (h + 1) * hd) for h in heads]
        dsns = [dstate[h] for h in heads]
        dsbs = [d.astype(BF16) for d in dsns]
        dobs = [do_ref[:, sl].astype(BF16) for sl in sls]
        dvns = [(_dot(qkt_ref[0, h], dobs[h], 1, 0) + _dot(kr_ref[0, h], dsbs[h], 1, 0)).astype(BF16) for h in heads]
        dkrs = [_dot(vn_ref[:, sls[h]], dsbs[h], 1, 1) for h in heads]
        nxt = [dsns[h] * jnp.exp(gc_ref[c - 1:c, h:h + 1])
               + _dot(wqt_ref[0, h], jnp.concatenate([-dvns[h], dobs[h]], axis=0), 1, 0) for h in heads]
        for h in heads:
            dvn_ref[:, sls[h]] = dvns[h]
            dkr_ref[:, sls[h]] = dkrs[h]
            d_el = jnp.sum(jnp.sum(dsns[h] * s_ref[0, h], axis=1, keepdims=True), axis=0, keepdims=True)
            del_ref[0, h:h + 1, :] = jnp.broadcast_to(d_el, (1, LANES))
            dstate[h] = nxt[h]

    rev = lambda n: nch - 1 - n
    rows = pl.BlockSpec((c, w), lambda n: (rev(n), 0))
    per = lambda a, b: pl.BlockSpec((1, DN_HEADS, a, b), lambda n: (rev(n), 0, 0, 0))
    return pl.pallas_call(
        body, name=name, grid=(nch,),
        in_specs=[rows, per(c, c), per(c, hd), rows, per(hd, hd), per(hd, 2 * c),
                  pl.BlockSpec((c, DN_HEADS), lambda n: (rev(n), 0))],
        out_specs=[rows, rows, pl.BlockSpec((1, DN_HEADS, LANES), lambda n: (rev(n), 0, 0))],
        out_shape=[jax.ShapeDtypeStruct((t, w), BF16), jax.ShapeDtypeStruct((t, w), F32),
                   jax.ShapeDtypeStruct((nch, DN_HEADS, LANES), F32)],
        scratch_shapes=[pltpu.VMEM((DN_HEADS, hd, hd), F32)], compiler_params=_params(("arbitrary",)),
    )(do, qkt, kr, vn, states, wqt, gc)


def _dn_local_bwd(q, k, v, gc, gcr, beta, invt, u, wq, vn, states, do, dvn, dkr, d_el, *, name):
    t, w = q.shape
    hd = w // DN_HEADS
    c = DN_CHUNK
    nch = t // c

    def body(q_ref, k_ref, v_ref, gc_ref, gcr_ref, b_ref, invt_ref, u_ref, wq_ref, vn_ref, s_ref, do_ref, dvn_ref,
             dkr_ref, del_ref, dq_ref, dk_ref, dv_ref, dgc_ref, dgr_ref, db_ref):
        heads = range(DN_HEADS)
        tms = [_dn_head_terms(q_ref, k_ref, gc_ref, gcr_ref, b_ref, h, hd) for h in heads]
        dobs = [do_ref[:, m['sl']].astype(BF16) for m in tms]
        tss = [_dot(jnp.concatenate([dob, -dvn_ref[:, m['sl']]], axis=0), s_ref[0, h].astype(BF16), 1, 1)
               for h, m, dob in zip(heads, tms, dobs)]
        d_qks = [_dot(dob, vn_ref[:, m['sl']], 1, 1) for m, dob in zip(tms, dobs)]
        d_rhss = [_dot(invt_ref[0, h], jnp.concatenate([dvn_ref[:, m['sl']], ts[c:].astype(BF16)], axis=1), 1, 0)
                  for h, m, ts in zip(heads, tms, tss)]
        d_as = [-_dot(d_rhs.astype(BF16),
                      jnp.concatenate([u_ref[:, m['sl']].astype(BF16), wq_ref[0, h, :c, :]], axis=1), 1, 1)
                for h, m, d_rhs in zip(heads, tms, d_rhss)]
        for h in heads:
            m, d_qe, d_qk, d_rhs, d_a = tms[h], tss[h][:c], d_qks[h], d_rhss[h], d_as[h]
            sl, qh, kh, bh, e, r, decay = m['sl'], m['q'], m['k'], m['b'], m['e'], m['r'], m['decay']
            vh = v_ref[:, sl]
            d_ru, d_rw = d_rhs[:, :hd], d_rhs[:, hd:]
            dv_ref[:, sl] = d_ru * bh
            d_e = jnp.sum(d_rw * m['kb'], axis=1, keepdims=True) + jnp.sum(d_qe * qh, axis=1, keepdims=True)
            d_n = jnp.where(m['row'] > m['col'], d_a, 0.0)
            d_m, d_p = (d_n * decay).astype(BF16), (d_qk * decay).astype(BF16)
            tk = _dot(jnp.concatenate([d_m, d_p], axis=0), kh.astype(BF16), 1, 0)
            d_kb = d_rw * e + tk[:c]
            dq_ref[:, sl] = tk[c:] + d_qe * e
            dk = _dot(d_m, m['kb'].astype(BF16), 0, 0) + _dot(d_p, qh.astype(BF16), 0, 0)
            dd = (d_n * m['kkt'] + d_qk * m['qkt']) * decay
            d_kr = dkr_ref[:, sl]
            d_r = jnp.sum(d_kr * kh, axis=1, keepdims=True)
            d_gl = del_ref[0, h:h + 1, 0:1] * m['el'] + jnp.sum(d_r * r, axis=0, keepdims=True)
            last = lax.broadcasted_iota(jnp.int32, (c, 1), 0) == c - 1
            dgc_ref[:, h:h + 1] = jnp.sum(dd, axis=1, keepdims=True) + d_e * e - d_r * r + jnp.where(last, d_gl, 0.0)
            dgr_ref[0, h:h + 1, :] = -jnp.sum(dd, axis=0, keepdims=True)
            dk_ref[:, sl] = dk + d_kr * r + d_kb * bh
            db_ref[:, h:h + 1] = jnp.sum(d_ru * vh, axis=1, keepdims=True) + jnp.sum(d_kb * kh, axis=1, keepdims=True)

    rows = pl.BlockSpec((c, w), lambda n: (n, 0))
    cols = pl.BlockSpec((c, DN_HEADS), lambda n: (n, 0))
    rowg = pl.BlockSpec((1, DN_HEADS, c), lambda n: (n, 0, 0))
    per = lambda a, b: pl.BlockSpec((1, DN_HEADS, a, b), lambda n: (n, 0, 0, 0))
    return pl.pallas_call(
        body, name=name, grid=(nch,),
        in_specs=[rows, rows, rows, cols, rowg, cols, per(c, c), rows, per(2 * c, hd), rows, per(hd, hd), rows, rows,
                  rows, pl.BlockSpec((1, DN_HEADS, LANES), lambda n: (n, 0, 0))],
        out_specs=[rows, rows, rows, cols, rowg, cols],
        out_shape=[jax.ShapeDtypeStruct((t, w), F32)] * 3
        + [jax.ShapeDtypeStruct((t, DN_HEADS), F32), jax.ShapeDtypeStruct((nch, DN_HEADS, c), F32),
           jax.ShapeDtypeStruct((t, DN_HEADS), F32)],
        compiler_params=_params(("parallel",)),
    )(q, k, v, gc, gcr, beta, invt, u, wq, vn, states, do, dvn, dkr, d_el)


def _other_chips():
    x, y = lax.axis_index("x"), lax.axis_index("y")
    return [(1 - x, y), (x, 1 - y), (1 - x, 1 - y)]


def _gather(src, *, name):
    def body(src_ref, out_ref, send_sems, recv_sems, local_sem):
        x, y, c = lax.axis_index("x"), lax.axis_index("y"), lax.axis_index("c")
        me, sibling, chips = (x, y, c), (x, y, 1 - c), _other_chips()
        slot = lambda px, py, pc: out_ref.at[4 * px + 2 * py + pc]

        def copy(k, block, to, own=False):
            return pltpu.make_async_remote_copy(
                src_ref=src_ref if own else slot(*block), dst_ref=slot(*block), send_sem=send_sems.at[k],
                recv_sem=recv_sems.at[k], device_id=to, device_id_type=MESH)

        local = pltpu.make_async_copy(src_ref, slot(*me), local_sem)
        local.start()
        first = [copy(0, me, sibling, own=True)] + [copy(1 + j, me, (*chip, c), own=True) for j, chip in enumerate(chips)]
        for cp in first:
            cp.start()
        passed = [copy(4 + j, (*chip, c), sibling) for j, chip in enumerate(chips)]
        for j, chip in enumerate(chips):
            copy(1 + j, (*chip, c), me).wait_recv()
            passed[j].start()
        copy(0, sibling, me).wait_recv()
        for j, chip in enumerate(chips):
            copy(4 + j, (*chip, 1 - c), me).wait_recv()
        for cp in first + passed:
            cp.wait_send()
        local.wait()

    return pl.pallas_call(
        body, name=name, out_shape=jax.ShapeDtypeStruct((N_DEV,) + src.shape, src.dtype),
        in_specs=[pl.BlockSpec(memory_space=pl.ANY)], out_specs=pl.BlockSpec(memory_space=pl.ANY),
        scratch_shapes=[pltpu.SemaphoreType.DMA((N_DEV - 1,)), pltpu.SemaphoreType.DMA((N_DEV - 1,)),
                        pltpu.SemaphoreType.DMA(())],
    )(src)


def _swap_with_sibling(src, *, name):
    def body(src_ref, out_ref, send_sem, recv_sem):
        x, y, c = lax.axis_index("x"), lax.axis_index("y"), lax.axis_index("c")
        cp = pltpu.make_async_remote_copy(src_ref=src_ref.at[1 - c], dst_ref=out_ref, send_sem=send_sem,
                                          recv_sem=recv_sem, device_id=(x, y, 1 - c), device_id_type=MESH)
        cp.start()
        cp.wait()

    return pl.pallas_call(
        body, name=name, out_shape=jax.ShapeDtypeStruct(src.shape[1:], src.dtype),
        in_specs=[pl.BlockSpec(memory_space=pl.ANY)], out_specs=pl.BlockSpec(memory_space=pl.ANY),
        scratch_shapes=[pltpu.SemaphoreType.DMA(()), pltpu.SemaphoreType.DMA(())],
    )(src)


def _add_own_half(src, got, *, name):
    _, n, r, c = src.shape
    tr = _tile(r, 512, 8)

    def body(s0_ref, s1_ref, got_ref, o_ref):
        own = jnp.where(lax.axis_index("c") == 0, s0_ref[0, 0], s1_ref[0, 0])
        o_ref[0] = own + got_ref[0]

    half = lambda h: pl.BlockSpec((1, 1, tr, c), lambda j, i: (h, j, i, 0))
    blk = pl.BlockSpec((1, tr, c), lambda j, i: (j, i, 0))
    return pl.pallas_call(
        body, name=name, grid=(n, r // tr), in_specs=[half(0), half(1), blk], out_specs=blk,
        out_shape=jax.ShapeDtypeStruct((n, r, c), src.dtype), compiler_params=_params(("parallel", "parallel")),
    )(src, src, got)


def _swap_between_chips(src, *, name):
    def body(src_ref, out_ref, send_sems, recv_sems, local_sem):
        x, y, c = lax.axis_index("x"), lax.axis_index("y"), lax.axis_index("c")
        mine = 2 * x + y
        local = pltpu.make_async_copy(src_ref.at[mine], out_ref.at[mine], local_sem)
        local.start()
        copies = [pltpu.make_async_remote_copy(
            src_ref=src_ref.at[2 * px + py], dst_ref=out_ref.at[mine], send_sem=send_sems.at[j],
            recv_sem=recv_sems.at[j], device_id=(px, py, c), device_id_type=MESH) for j, (px, py) in enumerate(_other_chips())]
        for cp in copies:
            cp.start()
        for j, (px, py) in enumerate(_other_chips()):
            pltpu.make_async_remote_copy(
                src_ref=src_ref.at[mine], dst_ref=out_ref.at[2 * px + py], send_sem=send_sems.at[j],
                recv_sem=recv_sems.at[j], device_id=(px, py, c), device_id_type=MESH).wait_recv()
        for cp in copies:
            cp.wait_send()
        local.wait()

    return pl.pallas_call(
        body, name=name, out_shape=jax.ShapeDtypeStruct(src.shape, src.dtype),
        in_specs=[pl.BlockSpec(memory_space=pl.ANY)], out_specs=pl.BlockSpec(memory_space=pl.ANY),
        scratch_shapes=[pltpu.SemaphoreType.DMA((3,)), pltpu.SemaphoreType.DMA((3,)), pltpu.SemaphoreType.DMA(())],
    )(src)


def _sum_adamw(parts, w, m, v, *, name):
    n_parts, r, c = parts.shape
    tr = _tile(r, 256, 8)
    c1, c2 = 1.0 - ADAM_B1 ** ADAM_STEP, 1.0 - ADAM_B2 ** ADAM_STEP

    def body(p_ref, w_ref, m_ref, v_ref, g_ref, d_ref, nm_ref, nv_ref):
        g = p_ref[0]
        for s in range(1, n_parts):
            g = g + p_ref[s]
        nm = ADAM_B1 * m_ref[...] + (1.0 - ADAM_B1) * g
        nv = ADAM_B2 * v_ref[...] + (1.0 - ADAM_B2) * (g * g)
        g_ref[...] = g
        nm_ref[...] = nm
        nv_ref[...] = nv
        d_ref[...] = -ADAM_LR * ((nm / c1) / (jnp.sqrt(nv / c2) + ADAM_EPS) + ADAM_WD * w_ref[...])

    blk = pl.BlockSpec((tr, c), lambda i: (i, 0))
    return pl.pallas_call(
        body, name=name, grid=(r // tr,), in_specs=[pl.BlockSpec((n_parts, tr, c), lambda i: (0, i, 0)), blk, blk, blk],
        out_specs=[blk] * 4, out_shape=[jax.ShapeDtypeStruct((r, c), F32)] * 4, compiler_params=_params(("parallel",)),
    )(parts, w, m, v)


def _pack_rows(n):
    return -(-n // (PACK_COLS * PACK_ROWS)) * PACK_ROWS


def _pack(blocks):
    parts, spans, at = [], [], 0
    for blk, n_lead in blocks:
        lead = blk.shape[:n_lead]
        n = math.prod(blk.shape[n_lead:])
        rows = _pack_rows(n)
        flat = blk.reshape(lead + (n,))
        flat = jnp.pad(flat, [(0, 0)] * n_lead + [(0, rows * PACK_COLS - n)])
        parts.append(flat.reshape(lead + (rows, PACK_COLS)))
        spans.append((at, rows))
        at += rows
    return jnp.concatenate(parts, axis=-2), spans


def _unpack(buf, span, shape):
    at, rows = span
    lead = buf.shape[:-2]
    flat = lax.slice_in_dim(buf, at, at + rows, axis=buf.ndim - 2).reshape(lead + (rows * PACK_COLS,))
    return lax.slice_in_dim(flat, 0, math.prod(shape), axis=len(lead)).reshape(lead + tuple(shape))


def _join_shards(g, axis):
    g = jnp.moveaxis(g, 0, axis)
    return g.reshape(g.shape[:axis] + (g.shape[axis] * g.shape[axis + 1],) + g.shape[axis + 2:])


def _split_shards(full, axis):
    s = full.shape
    g = full.reshape(s[:axis] + (N_DEV, s[axis] // N_DEV) + s[axis + 1:])
    return jnp.moveaxis(g, axis, 0)


def _ffn_fwd(x, g, w_in, w_out, tag):
    h = _rms_fwd(x, g, name=tag + "_norm", out_dtype=BF16)
    gate, up, a = _mm_swiglu(h, w_in, name=tag + "_in")
    return _mm(a, w_out, name=tag + "_out", res=x, scale=0.5, tk=2816), (x, h, gate, up, a)


def _ffn_bwd(saved, g, w_in, w_out, dxo, tag):
    x, h, gate, up, a = saved
    f = w_out.shape[0]
    d_w_out = _mm(a, dxo, name=tag + "_dwout", ta=True, scale=0.5, tm=1408)
    dgate, dup = _mm_dswiglu(dxo, w_out, gate, up, name=tag + "_da", scale=0.5)
    d_w_in = jnp.concatenate([_mm(dgate, h, name=tag + "_dwin_g", ta=True, tm=1408),
                              _mm(dup, h, name=tag + "_dwin_u", ta=True, tm=1408)], axis=0)
    dh = _mm(dgate, w_in[:f], name=tag + "_dh_g", tk=1408)
    dh = _mm(dup, w_in[f:], name=tag + "_dh_u", tk=1408, res=dh)
    dx, dg = _rms_bwd(x, g, dh, dxo, name=tag + "_dnorm")
    return dx, dg, d_w_in, d_w_out


def _block_diag(w):
    n, j, k = w.shape
    return (w[:, :, None, :] * jnp.eye(n, dtype=w.dtype)[:, None, :, None]).reshape(n * j, n * k)


def _diag_blocks(dense, n):
    j, k = dense.shape[0] // n, dense.shape[1] // n
    return jnp.stack([dense[i * j:(i + 1) * j, i * k:(i + 1) * k] for i in range(n)], axis=0)


def _attn_lru_fwd(x, p, tag):
    aw = ATTN_HEADS * 64
    h = _rms_fwd(x, p['mix_norm'], name=tag + "_norm", out_dtype=BF16)
    proj = _mm(h, p['ab_w_in'], name=tag + "_in", tb=True, tn=1280)
    q, k, v = (proj[:, i * aw:(i + 1) * aw] for i in range(3))
    outs, lses = [], []
    for window, dil in DILATED_PATTERNS:
        assert window // dil == ATTN_BLOCK
        o, l = _dattn_fwd(q, k, v, dil, name=f"{tag}_attn{dil}")
        outs.append(o)
        lses.append(l)
    wa, wx = _block_diag(p['lru_w_a']).astype(BF16), _block_diag(p['lru_w_x']).astype(BF16)
    sp, sp_vjp = jax.vjp(lambda lam: LRU_C * jax.nn.softplus(-lam), p['lru_lambda'])
    a, b, xc = _lru_gates_fwd(proj, 3, p['lru_conv_w'], p['lru_conv_b'], wa, p['lru_b_a'], wx, p['lru_b_x'], sp,
                              name=tag + "_gates")
    hs = _scan(a, b, name=tag + "_scan")
    cat, attn, lse_all = _mix_join_fwd(outs, lses, hs, proj, 4, name=tag + "_join")
    xo = _mm(cat, p['ab_w_out'], name=tag + "_out", res=x)
    return xo, (x, h, proj, q, k, v, attn, lse_all, a, hs, xc, wa, wx, sp, sp_vjp, cat)


def _attn_lru_bwd(saved, p, dxo, tag):
    x, h, proj, q, k, v, attn, lse_all, a, hs, xc, wa, wx, sp, sp_vjp, cat = saved
    aw = attn.shape[1]
    g = {'ab_w_out': _mm(cat, dxo, name=tag + "_dwout", ta=True)}
    dcat = _mm(dxo, p['ab_w_out'], name=tag + "_dcat", tb=True)
    delta, dhs, dgr = _mix_join_bwd(dcat, attn, hs, proj, 4, name=tag + "_djoin")
    a_next = jnp.concatenate([a[1:], jnp.zeros_like(a[:1])], axis=0)
    dtot = _scan(a_next, dhs, name=tag + "_dscan", reverse=True)
    h_prev = jnp.concatenate([jnp.zeros_like(hs[:1]), hs[:-1]], axis=0)
    dxc, dwa, dwx, vecs = _lru_gates_bwd(xc, dtot, h_prev, wa, p['lru_b_a'], wx, p['lru_b_x'], sp, name=tag + "_dgates")
    dxr, g['lru_conv_w'] = _conv_bwd(dxc, proj, 3, p['lru_conv_w'], name=tag + "_dconv")
    g['lru_b_a'], g['lru_b_x'], g['lru_conv_b'] = vecs[0], vecs[1], vecs[3]
    g['lru_lambda'], = sp_vjp(vecs[2])
    g['lru_w_a'], g['lru_w_x'] = _diag_blocks(dwa, LRU_BLOCKS), _diag_blocks(dwx, LRU_BLOCKS)
    dattn = dcat[:, :aw]
    dq = dk = dv = 0.0
    for _, dil in DILATED_PATTERNS:
        dq_, dk_, dv_ = _dattn_bwd(q, k, v, dattn, lse_all, delta, dil, name=f"{tag}_dattn{dil}")
        dq, dk, dv = dq + dq_, dk + dk_, dv + dv_
    dproj = jnp.concatenate([dq.astype(BF16), dk.astype(BF16), dv.astype(BF16), dxr, dgr], axis=-1)
    g['ab_w_in'] = _mm(dproj, h, name=tag + "_dwin", ta=True, tm=1280)
    dh = _mm(dproj, p['ab_w_in'], name=tag + "_dh", tk=1280)
    dx, g['mix_norm'] = _rms_bwd(x, p['mix_norm'], dh, dxo, name=tag + "_dnorm")
    return dx, g


def _dn_decay(a, b, a_log, dt_bias):
    t = a.shape[0]
    g = -jnp.exp(a_log) * jax.nn.softplus(a + dt_bias)
    gc = jnp.cumsum(g.reshape(t // DN_CHUNK, DN_CHUNK, DN_HEADS), axis=1)
    return gc.reshape(t, DN_HEADS), jnp.swapaxes(gc, 1, 2), jax.nn.sigmoid(b)


def _dn_in_width(w):
    return -(-(4 * w + 2 * DN_HEADS) // LANES) * LANES


def _deltanet_fwd(x, p, tag):
    w = p['dn_w_out'].shape[0]
    h = _rms_fwd(x, p['mix_norm'], name=tag + "_norm", out_dtype=BF16)
    proj = _mm(h, p['dn_w_in'], name=tag + "_in", tb=True, tn=1408)
    q, k, v, c = _dn_prep_fwd(proj, p['dn_conv_w'], name=tag + "_prep")
    a, b = proj[:, 4 * w:4 * w + DN_HEADS], proj[:, 4 * w + DN_HEADS:4 * w + 2 * DN_HEADS]
    (gc, gcr, beta), decay_vjp = jax.vjp(_dn_decay, a, b, p['dn_a_log'], p['dn_dt_bias'])
    prep = (q, k, v, gc, gcr, beta)
    u, wq, wqt, kr, krt, qk, qkt, invt = _dn_local_fwd(*prep, name=tag + "_local")
    o, vn, states = _dn_state_fwd(u, wq, krt, qk, gc, name=tag + "_state")
    og = _dn_gate_fwd(o, proj, 3, p['dn_o_norm'], name=tag + "_gate")
    xo = _mm(og, p['dn_w_out'], name=tag + "_out", res=x)
    return xo, (x, h, proj, c, o, prep, (u, wq, wqt, kr, qkt, invt, vn, states), decay_vjp, og)


def _deltanet_bwd(saved, p, dxo, tag):
    x, h, proj, c, o, prep, (u, wq, wqt, kr, qkt, invt, vn, states), decay_vjp, og = saved
    g = {'dn_w_out': _mm(og, dxo, name=tag + "_dwout", ta=True)}
    dog = _mm(dxo, p['dn_w_out'], name=tag + "_dog", tb=True)
    do, dz, g['dn_o_norm'] = _dn_gate_bwd(o, proj, 3, p['dn_o_norm'], dog, name=tag + "_dgate")
    dvn, dkr, d_el = _dn_state_bwd(do, qkt, kr, vn, states, wqt, prep[3], name=tag + "_dstate")
    dq, dk, dv, dgc, dgr, dbeta = _dn_local_bwd(*prep, invt, u, wq, vn, states, do, dvn, dkr, d_el, name=tag + "_dlocal")
    dc = _dn_prep_bwd(c, dq, dk, dv, name=tag + "_dprep")
    dqkv, g['dn_conv_w'] = _conv_bwd(dc, proj, 0, p['dn_conv_w'], name=tag + "_dconv")
    da, db, g['dn_a_log'], g['dn_dt_bias'] = decay_vjp((dgc, dgr, dbeta))
    t = x.shape[0]
    pad = jnp.zeros((t, p['dn_w_in'].shape[0] - dqkv.shape[1] - dz.shape[1] - 2 * DN_HEADS), BF16)
    dproj = jnp.concatenate([dqkv, dz, da.astype(BF16), db.astype(BF16), pad], axis=-1)
    g['dn_w_in'] = _mm(dproj, h, name=tag + "_dwin", ta=True, tm=1408)
    dh = _mm(dproj, p['dn_w_in'], name=tag + "_dh", tk=1408)
    dx, g['mix_norm'] = _rms_bwd(x, p['mix_norm'], dh, dxo, name=tag + "_dnorm")
    return dx, g


def _xattn_block_fwd(x, mem, p, tag):
    w = x.shape[1]
    h = _rms_fwd(x, p['xa_norm'], name=tag + "_norm", out_dtype=BF16)
    mh = _rms_fwd(mem, p['xa_mem_norm'], name=tag + "_mnorm", out_dtype=BF16)
    q = _mm(h, p['xa_wq'], name=tag + "_q")
    kv = _mm(mh, p['xa_wkv'], name=tag + "_kv", tb=True)
    k, v = kv[:, :w], kv[:, w:]
    o = _xattn_fwd(q, k, v, name=tag + "_attn").astype(BF16)
    xo = _mm(o, p['xa_wo'], name=tag + "_out", res=x)
    return xo, (x, h, mh, q, k, v, o)


def _xattn_block_bwd(saved, mem, p, dxo, tag):
    x, h, mh, q, k, v, o = saved
    g = {'xa_wo': _mm(o, dxo, name=tag + "_dwo", ta=True)}
    do = _mm(dxo, p['xa_wo'], name=tag + "_do", tb=True)
    dq, dk, dv = _xattn_bwd(q, k, v, do, name=tag + "_dattn")
    dq = dq.astype(BF16)
    dkv = jnp.concatenate([dk, dv], axis=-1).astype(BF16)
    g['xa_wq'] = _mm(h, dq, name=tag + "_dwq", ta=True)
    g['xa_wkv'] = _mm(dkv, mh, name=tag + "_dwkv", ta=True)
    dmh = _mm(dkv, p['xa_wkv'], name=tag + "_dmh")
    _, g['xa_mem_norm'] = _rms_bwd(mem, p['xa_mem_norm'], dmh, None, name=tag + "_dmnorm")
    dh = _mm(dq, p['xa_wq'], name=tag + "_dh", tb=True)
    dx, g['xa_norm'] = _rms_bwd(x, p['xa_norm'], dh, dxo, name=tag + "_dnorm")
    return dx, g


def _layer_params(full, layer):
    p = {n: full[n][layer] for n in ('ffn1_norm', 'ffn1_w_in', 'ffn1_w_out', 'mix_norm', 'xa_norm', 'xa_mem_norm',
                                     'xa_wq', 'xa_wkv', 'xa_wo', 'ffn2_norm', 'ffn2_w_in', 'ffn2_w_out')}
    mixer = ('ab_w_in', 'lru_conv_w', 'lru_conv_b', 'lru_w_a', 'lru_b_a', 'lru_w_x', 'lru_b_x', 'lru_lambda', 'ab_w_out') \
        if layer % 2 == 0 else ('dn_w_in', 'dn_conv_w', 'dn_a_log', 'dn_dt_bias', 'dn_o_norm', 'dn_w_out')
    p.update({n: full[n][layer // 2] for n in mixer})
    return p


def _step(x, mem, target, full, depth):
    saved = []
    for layer in range(depth):
        p = _layer_params(full, layer)
        tag = f"l{layer}"
        x, s1 = _ffn_fwd(x, p['ffn1_norm'], p['ffn1_w_in'], p['ffn1_w_out'], tag + "_ffn1")
        x, s2 = (_attn_lru_fwd if layer % 2 == 0 else _deltanet_fwd)(x, p, tag + "_mix")
        x, s3 = _xattn_block_fwd(x, mem, p, tag + "_xa")
        x, s4 = _ffn_fwd(x, p['ffn2_norm'], p['ffn2_w_in'], p['ffn2_w_out'], tag + "_ffn2")
        saved.append((p, s1, s2, s3, s4))
    sq, dx, d_final = _final_loss(x, full['final_norm'], target, name="final_loss")
    per_layer = []
    for layer in reversed(range(depth)):
        p, s1, s2, s3, s4 = saved[layer]
        tag = f"l{layer}"
        g = {}
        dx, g['ffn2_norm'], g['ffn2_w_in'], g['ffn2_w_out'] = _ffn_bwd(s4, p['ffn2_norm'], p['ffn2_w_in'], p['ffn2_w_out'], dx, tag + "_ffn2")
        dx, gx = _xattn_block_bwd(s3, mem, p, dx, tag + "_xa")
        dx, gm = (_attn_lru_bwd if layer % 2 == 0 else _deltanet_bwd)(s2, p, dx, tag + "_mix")
        dx, g['ffn1_norm'], g['ffn1_w_in'], g['ffn1_w_out'] = _ffn_bwd(s1, p['ffn1_norm'], p['ffn1_w_in'], p['ffn1_w_out'], dx, tag + "_ffn1")
        g.update(gx)
        g.update(gm)
        per_layer.insert(0, g)
    grads = {n: [g[n] for g in per_layer if n in g] for n in WEIGHTS[:-1]}
    grads['final_norm'] = d_final
    return sq, dx, grads


def kernel(x, mem, ffn1_norm, ffn1_w_in, ffn1_w_out, mix_norm, xa_norm, xa_mem_norm, xa_wq, xa_wkv, xa_wo, ffn2_norm, ffn2_w_in, ffn2_w_out, ab_w_in, lru_conv_w, lru_conv_b, lru_w_a, lru_b_a, lru_w_x, lru_b_x, lru_lambda, ab_w_out, dn_w_in, dn_conv_w, dn_a_log, dn_dt_bias, dn_o_norm, dn_w_out, final_norm, loss_target, m_ffn1_norm, m_ffn1_w_in, m_ffn1_w_out, m_mix_norm, m_xa_norm, m_xa_mem_norm, m_xa_wq, m_xa_wkv, m_xa_wo, m_ffn2_norm, m_ffn2_w_in, m_ffn2_w_out, m_ab_w_in, m_lru_conv_w, m_lru_conv_b, m_lru_w_a, m_lru_b_a, m_lru_w_x, m_lru_b_x, m_lru_lambda, m_ab_w_out, m_dn_w_in, m_dn_conv_w, m_dn_a_log, m_dn_dt_bias, m_dn_o_norm, m_dn_w_out, m_final_norm, v_ffn1_norm, v_ffn1_w_in, v_ffn1_w_out, v_mix_norm, v_xa_norm, v_xa_mem_norm, v_xa_wq, v_xa_wkv, v_xa_wo, v_ffn2_norm, v_ffn2_w_in, v_ffn2_w_out, v_ab_w_in, v_lru_conv_w, v_lru_conv_b, v_lru_w_a, v_lru_b_a, v_lru_w_x, v_lru_b_x, v_lru_lambda, v_ab_w_out, v_dn_w_in, v_dn_conv_w, v_dn_a_log, v_dn_dt_bias, v_dn_o_norm, v_dn_w_out, v_final_norm):
    args = dict(locals())
    flip = lambda n, a: jnp.swapaxes(a, 1, 2) if n in TRANSPOSED else a
    local = {n: flip(n, args[n]) for n in WEIGHTS}
    depth = ffn1_norm.shape[0]
    matrices = [n for n in WEIGHTS if n in SHARD_AXIS and n not in GATHER_F32]

    def entries(get):
        return [(a, n, l) for n in WEIGHTS for l, a in
                (enumerate(get(n)) if n in matrices else [(None, get(n))])]

    mats = [e for e in entries(lambda n: local[n]) if e[1] in matrices]
    send16, spans16 = _pack([(a.astype(BF16), 0) for a, _, _ in mats])
    send32, spans32 = _pack([(local[n], 0) for n in GATHER_F32])
    got16 = _gather(send16, name="gather_matrices")
    got32 = _gather(send32, name="gather_filters")
    full = {n: ([None] * local[n].shape[0] if n in matrices else local[n]) for n in WEIGHTS}
    for (a, n, l), span in zip(mats, spans16):
        full[n][l] = _unpack(got16, span, a.shape).reshape(N_DEV * a.shape[0], a.shape[1])
    for n, span in zip(GATHER_F32, spans32):
        full[n] = _join_shards(_unpack(got32, span, local[n].shape), SHARD_AXIS[n])
    dn_rows = full['dn_w_in'][0].shape[0]
    full['dn_w_in'] = [jnp.pad(w, ((0, _dn_in_width(w.shape[1]) - dn_rows), (0, 0))) for w in full['dn_w_in']]

    sq, dx, grads = _step(x[0], mem[0], loss_target[0], full, depth)
    grads['dn_w_in'] = [g[:dn_rows] for g in grads['dn_w_in']]
    loss = lax.psum(0.5 * jnp.sum(sq) / x.shape[2], ("x", "y", "c"))

    by_core = lambda z: jnp.swapaxes(z.reshape((N_DEV // 2, 2) + z.shape[1:]), 0, 1)

    def contribution(n):
        if n in matrices:
            return [by_core(g.reshape((N_DEV, g.shape[0] // N_DEV, g.shape[1]))) for g in grads[n]]
        g = grads[n] if n == 'final_norm' else jnp.stack(grads[n], axis=0)
        return by_core(_split_shards(g, SHARD_AXIS[n]) if n in SHARD_AXIS else jnp.broadcast_to(g, (N_DEV,) + g.shape))

    send, spans = _pack([(a, 2) for a, _, _ in entries(contribution)])
    got = _swap_with_sibling(send, name="grads_to_sibling")
    chip_sum = _add_own_half(send, got, name="grads_chip_sum")
    parts = _swap_between_chips(chip_sum, name="grads_between_chips")
    state = [_pack([(a, 0) for a, _, _ in entries(lambda n: flip(n, args[pre + n]))])[0] for pre in ("", "m_", "v_")]
    outs = _sum_adamw(parts, *state, name="sum_adamw")
    result = []
    for o in outs:
        got_rows = {}
        for (a, n, l), span in zip(entries(lambda n: local[n]), spans):
            got_rows.setdefault(n, []).append(_unpack(o, span, a.shape))
        result += [flip(n, jnp.stack(got_rows[n], axis=0) if n in matrices else got_rows[n][0]) for n in WEIGHTS]
    return (loss, dx[None], *result)
```

```python
import math

import jax
import jax.numpy as jnp
from jax import lax
from jax.experimental import pallas as pl
from jax.experimental.pallas import tpu as pltpu

F32, BF16 = jnp.float32, jnp.bfloat16
MESH = pl.DeviceIdType.MESH
N_DEV = 8
V7X_VMEM_LIMIT = 56 << 20
LANES = 128
PACK_COLS = 1024
PACK_ROWS = 16
NEG = -1e30

NORM_EPS = 1e-6
CONV_K = 4
ATTN_HEADS = 8
DILATED_PATTERNS = ((128, 1), (512, 4), (2048, 16))
ATTN_BLOCK = 128
LRU_BLOCKS = 8
LRU_C = 8.0
DN_HEADS = 8
DN_CHUNK = 64
XA_HEADS = 4
ADAM_LR, ADAM_B1, ADAM_B2, ADAM_EPS, ADAM_WD, ADAM_STEP = 0.001, 0.9, 0.999, 1e-08, 0.01, 10

WEIGHTS = ['ffn1_norm', 'ffn1_w_in', 'ffn1_w_out', 'mix_norm', 'xa_norm', 'xa_mem_norm', 'xa_wq', 'xa_wkv', 'xa_wo',
           'ffn2_norm', 'ffn2_w_in', 'ffn2_w_out', 'ab_w_in', 'lru_conv_w', 'lru_conv_b', 'lru_w_a', 'lru_b_a',
           'lru_w_x', 'lru_b_x', 'lru_lambda', 'ab_w_out', 'dn_w_in', 'dn_conv_w', 'dn_a_log', 'dn_dt_bias',
           'dn_o_norm', 'dn_w_out', 'final_norm']
SHARD_AXIS = {'ffn1_w_in': 2, 'ffn1_w_out': 1, 'xa_wq': 1, 'xa_wkv': 2, 'xa_wo': 1, 'ffn2_w_in': 2, 'ffn2_w_out': 1,
              'ab_w_in': 2, 'lru_conv_w': 2, 'ab_w_out': 1, 'dn_w_in': 2, 'dn_conv_w': 2, 'dn_w_out': 1}
GATHER_F32 = ('lru_conv_w', 'dn_conv_w')
TRANSPOSED = ('ffn1_w_in', 'xa_wkv', 'ffn2_w_in', 'ab_w_in', 'dn_w_in')


def _params(sem=None):
    return pltpu.CompilerParams(dimension_semantics=sem, vmem_limit_bytes=V7X_VMEM_LIMIT)


def _tile(n, pref, mult):
    best = None
    t = mult
    while t <= min(n, pref):
        if n % t == 0:
            best = t
        t += mult
    return n if best is None else best


def _dot(a, b, ca, cb, prec=None):
    return lax.dot_general(a, b, (((ca,), (cb,)), ((), ())), preferred_element_type=F32, precision=prec)


def _mm(a, b, *, name, ta=False, tb=False, out_dtype=F32, res=None, scale=1.0, tm=1024, tn=1024, tk=1024):
    m, kdim = (a.shape[1], a.shape[0]) if ta else a.shape
    n = b.shape[0] if tb else b.shape[1]
    assert (b.shape[1] if tb else b.shape[0]) == kdim
    tm = _tile(m, tm, LANES if ta else 16)
    tn = _tile(n, tn, LANES)
    tk = _tile(kdim, tk, LANES)
    nk = kdim // tk
    a_spec = pl.BlockSpec((tk, tm), lambda i, j, k: (k, i)) if ta else pl.BlockSpec((tm, tk), lambda i, j, k: (i, k))
    b_spec = pl.BlockSpec((tn, tk), lambda i, j, k: (j, k)) if tb else pl.BlockSpec((tk, tn), lambda i, j, k: (k, j))
    o_spec = pl.BlockSpec((tm, tn), lambda i, j, k: (i, j))
    ca, cb = (0 if ta else 1), (1 if tb else 0)
    has_res = res is not None

    def finish(acc, r_ref, o_ref):
        y = acc if scale == 1.0 else acc * scale
        if has_res:
            y = y + r_ref[...]
        o_ref[...] = y.astype(out_dtype)

    def body(*refs):
        a_ref, b_ref = refs[0], refs[1]
        r_ref = refs[2] if has_res else None
        o_ref = refs[2 + has_res]
        p = _dot(a_ref[...].astype(BF16), b_ref[...].astype(BF16), ca, cb)
        if nk == 1:
            finish(p, r_ref, o_ref)
            return
        acc = refs[3 + has_res]
        k = pl.program_id(2)

        @pl.when(k == 0)
        def _():
            acc[...] = p

        @pl.when(k > 0)
        def _():
            acc[...] += p

        @pl.when(k == nk - 1)
        def _():
            finish(acc[...], r_ref, o_ref)

    ins, specs = [a, b], [a_spec, b_spec]
    if has_res:
        ins.append(res)
        specs.append(o_spec)
    return pl.pallas_call(
        body, name=name, grid=(m // tm, n // tn, nk), in_specs=specs, out_specs=o_spec,
        out_shape=jax.ShapeDtypeStruct((m, n), out_dtype),
        scratch_shapes=[] if nk == 1 else [pltpu.VMEM((tm, tn), F32)],
        compiler_params=_params(("parallel", "parallel", "arbitrary")),
    )(*ins)


def _rms_fwd(x, g, *, name, out_dtype):
    r, d = x.shape
    tr = _tile(r, 512, 16)

    def body(x_ref, g_ref, o_ref):
        xv = x_ref[...]
        rs = lax.rsqrt(jnp.mean(xv * xv, axis=-1, keepdims=True) + NORM_EPS)
        o_ref[...] = (xv * rs * g_ref[...]).astype(out_dtype)

    return pl.pallas_call(
        body, name=name, grid=(r // tr,),
        in_specs=[pl.BlockSpec((tr, d), lambda i: (i, 0)), pl.BlockSpec((1, d), lambda i: (0, 0))],
        out_specs=pl.BlockSpec((tr, d), lambda i: (i, 0)),
        out_shape=jax.ShapeDtypeStruct((r, d), out_dtype), compiler_params=_params(("parallel",)),
    )(x, g.reshape(1, d))


def _rms_bwd(x, g, dh, dres, *, name):
    r, d = x.shape
    tr = _tile(r, 512, 8)
    has_res = dres is not None

    def body(*refs):
        x_ref, g_ref, dh_ref = refs[:3]
        r_ref = refs[3] if has_res else None
        dx_ref, dg_ref = refs[3 + has_res], refs[4 + has_res]
        xv = x_ref[...]
        rs = lax.rsqrt(jnp.mean(xv * xv, axis=-1, keepdims=True) + NORM_EPS)
        xh = xv * rs
        dhv = dh_ref[...]
        dgh = dhv * g_ref[...]
        dx = rs * (dgh - xh * jnp.mean(dgh * xh, axis=-1, keepdims=True))
        if has_res:
            dx = dx + r_ref[...]
        dx_ref[...] = dx
        part = jnp.sum(dhv * xh, axis=0, keepdims=True)

        @pl.when(pl.program_id(0) == 0)
        def _():
            dg_ref[...] = part

        @pl.when(pl.program_id(0) > 0)
        def _():
            dg_ref[...] += part

    row = pl.BlockSpec((tr, d), lambda i: (i, 0))
    vec = pl.BlockSpec((1, d), lambda i: (0, 0))
    ins, specs = [x, g.reshape(1, d), dh], [row, vec, row]
    if has_res:
        ins.append(dres)
        specs.append(row)
    dx, dg = pl.pallas_call(
        body, name=name, grid=(r // tr,), in_specs=specs, out_specs=[row, vec],
        out_shape=[jax.ShapeDtypeStruct((r, d), F32), jax.ShapeDtypeStruct((1, d), F32)],
        compiler_params=_params(("arbitrary",)),
    )(*ins)
    return dx, dg.reshape(d)


def _mm_swiglu(h, w_in, *, name, tm=1024, tn=256):
    t, d = h.shape
    f = w_in.shape[0] // 2
    tm, tn = _tile(t, tm, 16), _tile(f, tn, LANES)
    nj = f // tn

    def body(h_ref, wg_ref, wu_ref, g_ref, u_ref, a_ref):
        hv = h_ref[...]
        gate, up = _dot(hv, wg_ref[...], 1, 1), _dot(hv, wu_ref[...], 1, 1)
        g_ref[...] = gate.astype(BF16)
        u_ref[...] = up.astype(BF16)
        a_ref[...] = (gate * jax.nn.sigmoid(gate) * up).astype(BF16)

    out = pl.BlockSpec((tm, tn), lambda i, j: (i, j))
    return pl.pallas_call(
        body, name=name, grid=(t // tm, nj),
        in_specs=[pl.BlockSpec((tm, d), lambda i, j: (i, 0)), pl.BlockSpec((tn, d), lambda i, j: (j, 0)),
                  pl.BlockSpec((tn, d), lambda i, j: (j + nj, 0))],
        out_specs=[out, out, out], out_shape=[jax.ShapeDtypeStruct((t, f), BF16)] * 3,
        compiler_params=_params(("parallel", "parallel")),
    )(h, w_in, w_in)


def _mm_dswiglu(dy, w_out, gate, up, *, name, scale, tm=1024, tn=256):
    t, d = dy.shape
    f = w_out.shape[0]
    tm, tn = _tile(t, tm, 16), _tile(f, tn, LANES)

    def body(dy_ref, w_ref, g_ref, u_ref, dg_ref, du_ref):
        da = _dot(dy_ref[...].astype(BF16), w_ref[...], 1, 1) * scale
        gv = g_ref[...].astype(F32)
        s = jax.nn.sigmoid(gv)
        dg_ref[...] = (da * u_ref[...].astype(F32) * (s * (1.0 + gv * (1.0 - s)))).astype(BF16)
        du_ref[...] = (da * gv * s).astype(BF16)

    blk = pl.BlockSpec((tm, tn), lambda i, j: (i, j))
    return pl.pallas_call(
        body, name=name, grid=(t // tm, f // tn),
        in_specs=[pl.BlockSpec((tm, d), lambda i, j: (i, 0)), pl.BlockSpec((tn, d), lambda i, j: (j, 0)), blk, blk],
        out_specs=[blk, blk], out_shape=[jax.ShapeDtypeStruct((t, f), BF16)] * 2,
        compiler_params=_params(("parallel", "parallel")),
    )(dy, w_out, gate, up)


def _final_loss(x, g, target, *, name):
    r, d = x.shape
    tr = _tile(r, 512, 8)

    def body(x_ref, g_ref, t_ref, sq_ref, dx_ref, dg_ref):
        xv = x_ref[...]
        gv = g_ref[...]
        rs = lax.rsqrt(jnp.mean(xv * xv, axis=-1, keepdims=True) + NORM_EPS)
        xh = xv * rs
        err = xh * gv - t_ref[...]
        dy = err * (1.0 / d)
        dgh = dy * gv
        dx_ref[...] = rs * (dgh - xh * jnp.mean(dgh * xh, axis=-1, keepdims=True))
        sq = jnp.sum(err * err, axis=0, keepdims=True)
        part = jnp.sum(dy * xh, axis=0, keepdims=True)

        @pl.when(pl.program_id(0) == 0)
        def _():
            sq_ref[...] = sq
            dg_ref[...] = part

        @pl.when(pl.program_id(0) > 0)
        def _():
            sq_ref[...] += sq
            dg_ref[...] += part

    row = pl.BlockSpec((tr, d), lambda i: (i, 0))
    vec = pl.BlockSpec((1, d), lambda i: (0, 0))
    sq, dx, dg = pl.pallas_call(
        body, name=name, grid=(r // tr,), in_specs=[row, vec, row], out_specs=[vec, row, vec],
        out_shape=[jax.ShapeDtypeStruct((1, d), F32), jax.ShapeDtypeStruct((r, d), F32),
                   jax.ShapeDtypeStruct((1, d), F32)],
        compiler_params=_params(("arbitrary",)),
    )(x, g.reshape(1, d), target)
    return sq.reshape(d), dx, dg.reshape(d)


def _scan(a, b, *, name, reverse=False):
    t, w = a.shape
    tb = _tile(t, 1024, 8)
    nblk, ngrp = t // tb, tb // 8

    def body(a_ref, b_ref, h_ref, carry):
        @pl.when(pl.program_id(0) == 0)
        def _():
            carry[...] = jnp.zeros_like(carry)

        row = lax.broadcasted_iota(jnp.int32, (8, w), 0)

        def group(i, c):
            r0 = pl.multiple_of((ngrp - 1 - i if reverse else i) * 8, 8)
            av, bv = a_ref[pl.ds(r0, 8), :], b_ref[pl.ds(r0, 8), :]
            for s in (1, 2, 4):
                keep = row < 8 - s if reverse else row >= s
                shift = 8 - s if reverse else s
                bv = jnp.where(keep, bv + av * pltpu.roll(bv, shift, axis=0), bv)
                av = jnp.where(keep, av * pltpu.roll(av, shift, axis=0), av)
            hv = bv + av * c
            h_ref[pl.ds(r0, 8), :] = hv
            return hv[0:1, :] if reverse else hv[7:8, :]

        carry[0:1, :] = lax.fori_loop(0, ngrp, group, carry[0:1, :])

    blk = pl.BlockSpec((tb, w), lambda i: (nblk - 1 - i if reverse else i, 0))
    return pl.pallas_call(
        body, name=name, grid=(t // tb,), in_specs=[blk, blk], out_specs=blk,
        out_shape=jax.ShapeDtypeStruct((t, w), F32), scratch_shapes=[pltpu.VMEM((8, w), F32)],
        compiler_params=_params(("arbitrary",)),
    )(a, b)


def _row_specs(t, tb, width, col):
    per = tb // 8
    main = pl.BlockSpec((tb, width), lambda i: (i, col))
    before = pl.BlockSpec((8, width), lambda i: (jnp.maximum(i * per - 1, 0), col))
    after = pl.BlockSpec((8, width), lambda i: (jnp.minimum((i + 1) * per, t // 8 - 1), col))
    return main, before, after


def _with_rows_before(x_ref, before_ref):
    return jnp.concatenate([jnp.where(pl.program_id(0) > 0, before_ref[...], 0.0), x_ref[...]], axis=0)


def _tap(xe, s):
    return xe[8:] if s == 0 else pltpu.roll(xe, s, axis=0)[8:]


def _conv_rows(xe, w_ref):
    return sum(_tap(xe, CONV_K - 1 - k) * w_ref[k:k + 1, :] for k in range(CONV_K))


def _conv_bwd(dy, x, col, w, *, name):
    t, width = dy.shape
    tb = _tile(t, 256, 8)
    nblk = t // tb

    def body(dy_ref, dy_after_ref, x_ref, x_before_ref, w_ref, dx_ref, dw_ref):
        i = pl.program_id(0)
        dyv = dy_ref[...]
        dye = jnp.concatenate([dyv, jnp.where(i < nblk - 1, dy_after_ref[...], 0.0)], axis=0)
        dx = dyv * w_ref[CONV_K - 1:CONV_K, :]
        for s in range(1, CONV_K):
            dx = dx + pltpu.roll(dye, tb + 8 - s, axis=0)[:tb] * w_ref[CONV_K - 1 - s:CONV_K - s, :]
        dx_ref[...] = dx.astype(BF16)
        xe = _with_rows_before(x_ref, x_before_ref)

        @pl.when(i == 0)
        def _():
            dw_ref[...] = jnp.zeros_like(dw_ref)

        for k in range(CONV_K):
            dw_ref[k:k + 1, :] += jnp.sum(dyv * _tap(xe, CONV_K - 1 - k), axis=0, keepdims=True)

    main, _, after = _row_specs(t, tb, width, 0)
    xmain, xbefore, _ = _row_specs(t, tb, width, col)
    dx, dw = pl.pallas_call(
        body, name=name, grid=(nblk,),
        in_specs=[main, after, xmain, xbefore, pl.BlockSpec((CONV_K, width), lambda i: (0, 0))],
        out_specs=[main, pl.BlockSpec((8, width), lambda i: (0, 0))],
        out_shape=[jax.ShapeDtypeStruct((t, width), BF16), jax.ShapeDtypeStruct((8, width), F32)],
        compiler_params=_params(("arbitrary",)),
    )(dy, dy, x, x, w)
    return dx, dw[:CONV_K]


def _expm1(x):
    small = x * (1.0 + x * (0.5 + x * (1.0 / 6.0 + x * (1.0 / 24.0 + x * (1.0 / 120.0 + x * (1.0 / 720.0))))))
    return jnp.where(jnp.abs(x) < 0.1, small, jnp.exp(x) - 1.0)


def _lru_gate_terms(xc, wa_ref, ba_ref, wx_ref, bx_ref, sp_ref):
    xb = xc.astype(BF16)
    r = jax.nn.sigmoid(_dot(xb, wa_ref[...], 1, 0) + ba_ref[...])
    i = jax.nn.sigmoid(_dot(xb, wx_ref[...], 1, 0) + bx_ref[...])
    log_a = -r * sp_ref[...]
    return r, i, jnp.exp(log_a), jnp.sqrt(-_expm1(2.0 * log_a))


def _lru_gates_fwd(proj, col, conv_w, conv_b, wa, ba, wx, bx, sp, *, name):
    t = proj.shape[0]
    width = conv_w.shape[1]
    tb = _tile(t, 512, 8)

    def body(x_ref, x_before_ref, cw_ref, cb_ref, wa_ref, ba_ref, wx_ref, bx_ref, sp_ref, a_ref, b_ref, xc_ref):
        xc = _conv_rows(_with_rows_before(x_ref, x_before_ref), cw_ref) + cb_ref[...]
        r, i, a, mult = _lru_gate_terms(xc, wa_ref, ba_ref, wx_ref, bx_ref, sp_ref)
        a_ref[...] = a
        b_ref[...] = mult * i * xc
        xc_ref[...] = xc

    main, before, _ = _row_specs(t, tb, width, col)
    out = pl.BlockSpec((tb, width), lambda i: (i, 0))
    vec = pl.BlockSpec((1, width), lambda i: (0, 0))
    mat = pl.BlockSpec((width, width), lambda i: (0, 0))
    return pl.pallas_call(
        body, name=name, grid=(t // tb,),
        in_specs=[main, before, pl.BlockSpec((CONV_K, width), lambda i: (0, 0)), vec, mat, vec, mat, vec, vec],
        out_specs=[out] * 3, out_shape=[jax.ShapeDtypeStruct((t, width), F32)] * 3,
        compiler_params=_params(("parallel",)),
    )(proj, proj, conv_w, conv_b.reshape(1, -1), wa, ba.reshape(1, -1), wx, bx.reshape(1, -1), sp.reshape(1, -1))


def _lru_gates_bwd(xc, dtot, h_prev, wa, ba, wx, bx, sp, *, name):
    t, width = xc.shape
    tb = _tile(t, 512, 8)

    def body(xc_ref, dt_ref, hp_ref, wa_ref, ba_ref, wx_ref, bx_ref, sp_ref, dxc_ref, dwa_ref, dwx_ref, vec_ref):
        xc = xc_ref[...]
        r, i, a, mult = _lru_gate_terms(xc, wa_ref, ba_ref, wx_ref, bx_ref, sp_ref)
        db = dt_ref[...]
        d_la = db * hp_ref[...] * a - db * i * xc * (a * a / mult)
        d_pa = (-d_la * sp_ref[...]) * r * (1.0 - r)
        d_pi = db * mult * xc * i * (1.0 - i)
        dab, dib = d_pa.astype(BF16), d_pi.astype(BF16)
        dxc = db * mult * i + _dot(dab, wa_ref[...], 1, 1) + _dot(dib, wx_ref[...], 1, 1)
        dxc_ref[...] = dxc
        xb = xc.astype(BF16)
        rows = [jnp.sum(z, axis=0, keepdims=True) for z in (d_pa, d_pi, -d_la * r, dxc)]

        @pl.when(pl.program_id(0) == 0)
        def _():
            dwa_ref[...] = jnp.zeros_like(dwa_ref)
            dwx_ref[...] = jnp.zeros_like(dwx_ref)
            vec_ref[...] = jnp.zeros_like(vec_ref)

        dwa_ref[...] += _dot(xb, dab, 0, 0)
        dwx_ref[...] += _dot(xb, dib, 0, 0)
        for j, z in enumerate(rows):
            vec_ref[j:j + 1, :] += z

    blk = pl.BlockSpec((tb, width), lambda i: (i, 0))
    vec = pl.BlockSpec((1, width), lambda i: (0, 0))
    mat = pl.BlockSpec((width, width), lambda i: (0, 0))
    dxc, dwa, dwx, vecs = pl.pallas_call(
        body, name=name, grid=(t // tb,), in_specs=[blk, blk, blk, mat, vec, mat, vec, vec],
        out_specs=[blk, mat, mat, pl.BlockSpec((8, width), lambda i: (0, 0))],
        out_shape=[jax.ShapeDtypeStruct((t, width), F32), jax.ShapeDtypeStruct((width, width), F32),
                   jax.ShapeDtypeStruct((width, width), F32), jax.ShapeDtypeStruct((8, width), F32)],
        compiler_params=_params(("arbitrary",)),
    )(xc, dtot, h_prev, wa, ba.reshape(1, -1), wx, bx.reshape(1, -1), sp.reshape(1, -1))
    return dxc, dwa, dwx, vecs[:4]


GELU_C = math.sqrt(2.0 / math.pi)


def _gelu_terms(x):
    th = jnp.tanh(GELU_C * (x + 0.044715 * x * x * x))
    return 0.5 * x * (1.0 + th), 0.5 * (1.0 + th) + 0.5 * x * (1.0 - th * th) * GELU_C * (1.0 + 3 * 0.044715 * x * x)


def _mix_join_fwd(outs, lses, hs, proj, gr_col, *, name):
    t, w = hs.shape
    tb = _tile(t, 512, 16)
    n = len(outs)

    def body(*refs):
        o_refs, l_refs = refs[:n], refs[n:2 * n]
        hs_ref, gr_ref, cat_ref, attn_ref, lse_ref = refs[2 * n:]
        ls = [r[...] for r in l_refs]
        m = ls[0]
        for l in ls[1:]:
            m = jnp.maximum(m, l)
        ws = [jnp.exp(l - m) for l in ls]
        den = sum(ws)
        attn = sum(wt * r[...] for wt, r in zip(ws, o_refs)) / den
        attn_ref[...] = attn
        lse_ref[...] = m + jnp.log(den)
        cat_ref[:, :w] = attn.astype(BF16)
        cat_ref[:, w:] = (hs_ref[...] * _gelu_terms(gr_ref[...])[0]).astype(BF16)

    blk = pl.BlockSpec((tb, w), lambda i: (i, 0))
    return pl.pallas_call(
        body, name=name, grid=(t // tb,),
        in_specs=[blk] * (2 * n + 1) + [pl.BlockSpec((tb, w), lambda i: (i, gr_col))],
        out_specs=[pl.BlockSpec((tb, 2 * w), lambda i: (i, 0)), blk, blk],
        out_shape=[jax.ShapeDtypeStruct((t, 2 * w), BF16), jax.ShapeDtypeStruct((t, w), F32),
                   jax.ShapeDtypeStruct((t, w), F32)],
        compiler_params=_params(("parallel",)),
    )(*outs, *lses, hs, proj)


def _mix_join_bwd(dcat, attn, hs, proj, gr_col, *, name):
    t, w = hs.shape
    hd = w // ATTN_HEADS
    tb = _tile(t, 512, 16)

    def body(dcat_ref, attn_ref, hs_ref, gr_ref, delta_ref, dhs_ref, dgr_ref):
        for h in range(ATTN_HEADS):
            sl = slice(h * hd, (h + 1) * hd)
            d = jnp.sum(dcat_ref[:, sl] * attn_ref[:, sl], axis=-1, keepdims=True)
            delta_ref[:, sl] = jnp.broadcast_to(d, (tb, hd))
        dy = dcat_ref[:, w:]
        g, dg = _gelu_terms(gr_ref[...])
        dhs_ref[...] = dy * g
        dgr_ref[...] = (dy * hs_ref[...] * dg).astype(BF16)

    blk = pl.BlockSpec((tb, w), lambda i: (i, 0))
    return pl.pallas_call(
        body, name=name, grid=(t // tb,),
        in_specs=[pl.BlockSpec((tb, 2 * w), lambda i: (i, 0)), blk, blk, pl.BlockSpec((tb, w), lambda i: (i, gr_col))],
        out_specs=[blk, blk, blk],
        out_shape=[jax.ShapeDtypeStruct((t, w), F32), jax.ShapeDtypeStruct((t, w), F32),
                   jax.ShapeDtypeStruct((t, w), BF16)],
        compiler_params=_params(("parallel",)),
    )(dcat, attn, hs, proj)


def _silu_terms(x):
    s = jax.nn.sigmoid(x)
    return x * s, s * (1.0 + x * (1.0 - s))


def _dn_prep_fwd(proj, conv_w, *, name):
    t = proj.shape[0]
    w3 = conv_w.shape[1]
    w = w3 // 3
    hd = w // DN_HEADS
    tb = _tile(t, 256, 8)

    def body(x_ref, x_before_ref, cw_ref, q_ref, k_ref, v_ref, c_ref):
        c = _conv_rows(_with_rows_before(x_ref, x_before_ref), cw_ref)
        c_ref[...] = c
        s = _silu_terms(c)[0]
        v_ref[...] = s[:, 2 * w:]
        for part, ref, scale in ((0, q_ref, hd ** -0.5), (1, k_ref, 1.0)):
            for h in range(DN_HEADS):
                z = s[:, part * w + h * hd:part * w + (h + 1) * hd]
                ref[:, h * hd:(h + 1) * hd] = z * (lax.rsqrt(jnp.sum(z * z, axis=-1, keepdims=True) + 1e-6) * scale)

    main, before, _ = _row_specs(t, tb, w3, 0)
    out = pl.BlockSpec((tb, w), lambda i: (i, 0))
    return pl.pallas_call(
        body, name=name, grid=(t // tb,), in_specs=[main, before, pl.BlockSpec((CONV_K, w3), lambda i: (0, 0))],
        out_specs=[out, out, out, pl.BlockSpec((tb, w3), lambda i: (i, 0))],
        out_shape=[jax.ShapeDtypeStruct((t, w), F32)] * 3 + [jax.ShapeDtypeStruct((t, w3), F32)],
        compiler_params=_params(("parallel",)),
    )(proj, proj, conv_w)


def _dn_prep_bwd(c, dq, dk, dv, *, name):
    t, w3 = c.shape
    w = w3 // 3
    hd = w // DN_HEADS
    tb = _tile(t, 256, 8)

    def body(c_ref, dq_ref, dk_ref, dv_ref, dc_ref):
        cv = c_ref[...]
        s, ds = _silu_terms(cv)
        dc_ref[:, 2 * w:] = dv_ref[...] * ds[:, 2 * w:]
        for part, ref, scale in ((0, dq_ref, hd ** -0.5), (1, dk_ref, 1.0)):
            for h in range(DN_HEADS):
                cols = slice(part * w + h * hd, part * w + (h + 1) * hd)
                z = s[:, cols]
                rn = lax.rsqrt(jnp.sum(z * z, axis=-1, keepdims=True) + 1e-6)
                y = z * rn
                dy = ref[:, h * hd:(h + 1) * hd] * scale
                dc_ref[:, cols] = rn * (dy - y * jnp.sum(dy * y, axis=-1, keepdims=True)) * ds[:, cols]

    blk = pl.BlockSpec((tb, w), lambda i: (i, 0))
    wide = pl.BlockSpec((tb, w3), lambda i: (i, 0))
    return pl.pallas_call(
        body, name=name, grid=(t // tb,), in_specs=[wide, blk, blk, blk], out_specs=wide,
        out_shape=jax.ShapeDtypeStruct((t, w3), F32), compiler_params=_params(("parallel",)),
    )(c, dq, dk, dv)


def _dn_gate_fwd(o, proj, z_col, o_norm, *, name):
    t, w = o.shape
    hd = w // DN_HEADS
    tb = _tile(t, 512, 16)

    def body(o_ref, z_ref, g_ref, y_ref):
        for h in range(DN_HEADS):
            sl = slice(h * hd, (h + 1) * hd)
            ov = o_ref[:, sl]
            rn = lax.rsqrt(jnp.mean(ov * ov, axis=-1, keepdims=True) + NORM_EPS)
            y_ref[:, sl] = (ov * rn * g_ref[...] * _silu_terms(z_ref[:, sl])[0]).astype(BF16)

    blk = pl.BlockSpec((tb, w), lambda i: (i, 0))
    return pl.pallas_call(
        body, name=name, grid=(t // tb,),
        in_specs=[blk, pl.BlockSpec((tb, w), lambda i: (i, z_col)), pl.BlockSpec((1, hd), lambda i: (0, 0))],
        out_specs=blk, out_shape=jax.ShapeDtypeStruct((t, w), BF16), compiler_params=_params(("parallel",)),
    )(o, proj, o_norm.reshape(1, hd))


def _dn_gate_bwd(o, proj, z_col, o_norm, dy, *, name):
    t, w = o.shape
    hd = w // DN_HEADS
    tb = _tile(t, 512, 16)

    def body(o_ref, z_ref, g_ref, dy_ref, do_ref, dz_ref, dg_ref):
        gv = g_ref[...]
        dg = jnp.zeros((1, hd), F32)
        for h in range(DN_HEADS):
            sl = slice(h * hd, (h + 1) * hd)
            ov, dyv = o_ref[:, sl], dy_ref[:, sl]
            sz, dsz = _silu_terms(z_ref[:, sl])
            rn = lax.rsqrt(jnp.mean(ov * ov, axis=-1, keepdims=True) + NORM_EPS)
            nv = ov * rn
            dz_ref[:, sl] = (dyv * nv * gv * dsz).astype(BF16)
            dn = dyv * gv * sz
            do_ref[:, sl] = rn * (dn - nv * jnp.mean(dn * nv, axis=-1, keepdims=True))
            dg = dg + jnp.sum(dyv * nv * sz, axis=0, keepdims=True)

        @pl.when(pl.program_id(0) == 0)
        def _():
            dg_ref[...] = dg

        @pl.when(pl.program_id(0) > 0)
        def _():
            dg_ref[...] += dg

    blk = pl.BlockSpec((tb, w), lambda i: (i, 0))
    vec = pl.BlockSpec((1, hd), lambda i: (0, 0))
    do, dz, dg = pl.pallas_call(
        body, name=name, grid=(t // tb,), in_specs=[blk, pl.BlockSpec((tb, w), lambda i: (i, z_col)), vec, blk],
        out_specs=[blk, blk, vec],
        out_shape=[jax.ShapeDtypeStruct((t, w), F32), jax.ShapeDtypeStruct((t, w), BF16),
                   jax.ShapeDtypeStruct((1, hd), F32)],
        compiler_params=_params(("arbitrary",)),
    )(o, proj, o_norm.reshape(1, hd), dy)
    return do, dz, dg.reshape(hd)


def _band_masks(n):
    qi = lax.broadcasted_iota(jnp.int32, (ATTN_BLOCK, ATTN_BLOCK), 0)
    kj = lax.broadcasted_iota(jnp.int32, (ATTN_BLOCK, ATTN_BLOCK), 1)
    return kj <= qi, jnp.logical_and(kj >= qi, n > 0)


def _dattn_fwd(qkv, dil, *, name):
    seq, w = qkv.shape[0], qkv.shape[1] // (3 * dil)
    t = seq * dil
    hd = w // ATTN_HEADS
    assert seq % ATTN_BLOCK == 0
    nb = seq // ATTN_BLOCK
    scale = hd ** -0.5

    def body(q_ref, kp_ref, kc_ref, vp_ref, vc_ref, o_ref, lse_ref):
        mc, mp = _band_masks(pl.program_id(1))
        heads = range(ATTN_HEADS)
        sls = [slice(h * hd, (h + 1) * hd) for h in heads]
        qs = [(q_ref[:, sl] * scale).astype(BF16) for sl in sls]
        scs = [jnp.where(mc, _dot(qs[h], kc_ref[:, sls[h]].astype(BF16), 1, 1), NEG) for h in heads]
        sps = [jnp.where(mp, _dot(qs[h], kp_ref[:, sls[h]].astype(BF16), 1, 1), NEG) for h in heads]
        ms = [jnp.maximum(jnp.max(sc, axis=-1, keepdims=True), jnp.max(sp, axis=-1, keepdims=True))
              for sc, sp in zip(scs, sps)]
        pcs = [jnp.exp(sc - m) for sc, m in zip(scs, ms)]
        pps = [jnp.exp(sp - m) for sp, m in zip(sps, ms)]
        dens = [jnp.sum(pc, axis=-1, keepdims=True) + jnp.sum(pp, axis=-1, keepdims=True) for pc, pp in zip(pcs, pps)]
        outs = [_dot(pcs[h].astype(BF16), vc_ref[:, sls[h]].astype(BF16), 1, 0)
                + _dot(pps[h].astype(BF16), vp_ref[:, sls[h]].astype(BF16), 1, 0) for h in heads]
        for h in heads:
            o_ref[:, sls[h]] = outs[h] / dens[h]
            lse_ref[:, sls[h]] = jnp.broadcast_to(ms[h] + jnp.log(dens[h]), (ATTN_BLOCK, hd))

    cur = pl.BlockSpec((ATTN_BLOCK, w), lambda r, n: (n, r))
    part = lambda j, row: pl.BlockSpec((ATTN_BLOCK, w), lambda r, n: (row(n), 3 * r + j))
    here, before = (lambda n: n), (lambda n: jnp.maximum(n - 1, 0))
    o, lse = pl.pallas_call(
        body, name=name, grid=(dil, nb),
        in_specs=[part(0, here), part(1, before), part(1, here), part(2, before), part(2, here)], out_specs=[cur, cur],
        out_shape=[jax.ShapeDtypeStruct((seq, dil * w), F32)] * 2, compiler_params=_params(("parallel", "parallel")),
    )(qkv, qkv, qkv, qkv, qkv)
    return o.reshape(t, w), lse.reshape(t, w)


def _dattn_bwd(qkv, do, lse, delta, dil, *, name):
    t, w = do.shape
    hd = w // ATTN_HEADS
    seq = t // dil
    nb = seq // ATTN_BLOCK
    scale = hd ** -0.5

    def body(qc_ref, qn_ref, doc_ref, don_ref, lc_ref, ln_ref, dc_ref, dn_ref, kp_ref, kc_ref, vp_ref, vc_ref,
             dq_ref, dk_ref, dv_ref):
        n = pl.program_id(1)
        mc, mp = _band_masks(n)
        _, mx = _band_masks(jnp.where(n + 1 < nb, 1, 0))
        heads = range(ATTN_HEADS)
        sls = [slice(h * hd, (h + 1) * hd) for h in heads]
        ones = [slice(h * hd, h * hd + 1) for h in heads]
        bf = lambda ref, scl=None: [(ref[:, sl] if scl is None else ref[:, sl] * scl).astype(BF16) for sl in sls]
        qc, qn, kc, kp = bf(qc_ref, scale), bf(qn_ref, scale), bf(kc_ref), bf(kp_ref)
        vc, vp, doc, don = bf(vc_ref), bf(vp_ref), bf(doc_ref), bf(don_ref)
        p_c = [jnp.exp(jnp.where(mc, _dot(qc[h], kc[h], 1, 1), NEG) - lc_ref[:, ones[h]]) for h in heads]
        p_p = [jnp.exp(jnp.where(mp, _dot(qc[h], kp[h], 1, 1), NEG) - lc_ref[:, ones[h]]) for h in heads]
        p_x = [jnp.exp(jnp.where(mx, _dot(qn[h], kc[h], 1, 1), NEG) - ln_ref[:, ones[h]]) for h in heads]
        ds_c = [(p_c[h] * (_dot(doc[h], vc[h], 1, 1) - dc_ref[:, ones[h]])).astype(BF16) for h in heads]
        ds_p = [(p_p[h] * (_dot(doc[h], vp[h], 1, 1) - dc_ref[:, ones[h]])).astype(BF16) for h in heads]
        ds_x = [(p_x[h] * (_dot(don[h], vc[h], 1, 1) - dn_ref[:, ones[h]])).astype(BF16) for h in heads]
        for h in heads:
            dq_ref[:, sls[h]] = (_dot(ds_c[h], kc[h], 1, 0) + _dot(ds_p[h], kp[h], 1, 0)) * scale
            dk_ref[:, sls[h]] = _dot(ds_c[h], qc[h], 0, 0) + _dot(ds_x[h], qn[h], 0, 0)
            dv_ref[:, sls[h]] = _dot(p_c[h].astype(BF16), doc[h], 0, 0) + _dot(p_x[h].astype(BF16), don[h], 0, 0)

    cur = pl.BlockSpec((ATTN_BLOCK, w), lambda r, n: (n, r))
    nxt = pl.BlockSpec((ATTN_BLOCK, w), lambda r, n: (jnp.minimum(n + 1, nb - 1), r))
    part = lambda j, row: pl.BlockSpec((ATTN_BLOCK, w), lambda r, n: (row(n), 3 * r + j))
    here, before, after = (lambda n: n), (lambda n: jnp.maximum(n - 1, 0)), (lambda n: jnp.minimum(n + 1, nb - 1))
    do, lse, delta = (z.reshape(seq, dil * w) for z in (do, lse, delta))
    dq, dk, dv = pl.pallas_call(
        body, name=name, grid=(dil, nb),
        in_specs=[part(0, here), part(0, after), cur, nxt, cur, nxt, cur, nxt, part(1, before), part(1, here),
                  part(2, before), part(2, here)], out_specs=[cur, cur, cur],
        out_shape=[jax.ShapeDtypeStruct((seq, dil * w), F32)] * 3, compiler_params=_params(("parallel", "parallel")),
    )(qkv, qkv, do, do, lse, lse, delta, delta, qkv, qkv, qkv, qkv)
    return dq.reshape(t, w), dk.reshape(t, w), dv.reshape(t, w)


def _xattn_fwd(q, k, v, *, name):
    t, w = q.shape
    nm = k.shape[0]
    hd = w // XA_HEADS
    scale = hd ** -0.5
    tq = _tile(t, 512, 8)

    def body(q_ref, k_ref, v_ref, o_ref):
        for h in range(XA_HEADS):
            sl = slice(h * hd, (h + 1) * hd)
            s = _dot((q_ref[:, sl] * scale).astype(BF16), k_ref[:, sl].astype(BF16), 1, 1)
            p = jnp.exp(s - jnp.max(s, axis=-1, keepdims=True))
            p = p / jnp.sum(p, axis=-1, keepdims=True)
            o_ref[:, sl] = _dot(p.astype(BF16), v_ref[:, sl].astype(BF16), 1, 0)

    qs = pl.BlockSpec((tq, w), lambda i: (i, 0))
    ks = pl.BlockSpec((nm, w), lambda i: (0, 0))
    return pl.pallas_call(
        body, name=name, grid=(t // tq,), in_specs=[qs, ks, ks], out_specs=qs,
        out_shape=jax.ShapeDtypeStruct((t, w), F32), compiler_params=_params(("parallel",)),
    )(q, k, v)


def _xattn_bwd(q, k, v, do, *, name):
    t, w = q.shape
    nm = k.shape[0]
    hd = w // XA_HEADS
    scale = hd ** -0.5
    tq = _tile(t, 512, 8)

    def body(q_ref, k_ref, v_ref, do_ref, dq_ref, dk_ref, dv_ref):
        first = pl.program_id(0) == 0
        for h in range(XA_HEADS):
            sl = slice(h * hd, (h + 1) * hd)
            qh = (q_ref[:, sl] * scale).astype(BF16)
            kh, vh, doh = k_ref[:, sl].astype(BF16), v_ref[:, sl].astype(BF16), do_ref[:, sl].astype(BF16)
            s = _dot(qh, kh, 1, 1)
            p = jnp.exp(s - jnp.max(s, axis=-1, keepdims=True))
            p = p / jnp.sum(p, axis=-1, keepdims=True)
            dp = _dot(doh, vh, 1, 1)
            ds = (p * (dp - jnp.sum(p * dp, axis=-1, keepdims=True))).astype(BF16)
            dq_ref[:, sl] = _dot(ds, kh, 1, 0) * scale
            dk = _dot(ds, qh, 0, 0)
            dv = _dot(p.astype(BF16), doh, 0, 0)

            @pl.when(first)
            def _():
                dk_ref[:, sl] = dk
                dv_ref[:, sl] = dv

            @pl.when(jnp.logical_not(first))
            def _():
                dk_ref[:, sl] += dk
                dv_ref[:, sl] += dv

    qs = pl.BlockSpec((tq, w), lambda i: (i, 0))
    ks = pl.BlockSpec((nm, w), lambda i: (0, 0))
    return pl.pallas_call(
        body, name=name, grid=(t // tq,), in_specs=[qs, ks, ks, qs], out_specs=[qs, ks, ks],
        out_shape=[jax.ShapeDtypeStruct((t, w), F32), jax.ShapeDtypeStruct((nm, w), F32),
                   jax.ShapeDtypeStruct((nm, w), F32)],
        compiler_params=_params(("arbitrary",)),
    )(q, k, v, do)


def _dn_head_terms(q_ref, k_ref, gc_ref, gcr_ref, b_ref, h, hd):
    c = DN_CHUNK
    sl = slice(h * hd, (h + 1) * hd)
    qh, kh = q_ref[:, sl], k_ref[:, sl]
    gcc, gcr_h, bh = gc_ref[:, h:h + 1], gcr_ref[0, h:h + 1, :], b_ref[:, h:h + 1]
    row = lax.broadcasted_iota(jnp.int32, (c, c), 0)
    col = lax.broadcasted_iota(jnp.int32, (c, c), 1)
    decay = jnp.exp(jnp.where(row >= col, gcc - gcr_h, NEG))
    kb = kh * bh
    kkt = _dot(kb.astype(BF16), kh.astype(BF16), 1, 1)
    qkt = _dot(qh.astype(BF16), kh.astype(BF16), 1, 1)
    gl = gcc[c - 1:c, :]
    return dict(sl=sl, q=qh, k=kh, gcc=gcc, b=bh, row=row, col=col, decay=decay, kb=kb, kkt=kkt, qkt=qkt,
                e=jnp.exp(gcc), el=jnp.exp(gl), r=jnp.exp(gl - gcc))


def _dn_local_fwd(q, k, v, gc, gcr, beta, *, name):
    t, w = q.shape
    hd = w // DN_HEADS
    c = DN_CHUNK
    nch = t // c
    heads = range(DN_HEADS)

    def body(q_ref, k_ref, v_ref, gc_ref, gcr_ref, b_ref,
             u_ref, wq_ref, wqt_ref, kr_ref, krt_ref, qk_ref, qkt_ref, invt_ref):
        tm = [_dn_head_terms(q_ref, k_ref, gc_ref, gcr_ref, b_ref, h, hd) for h in heads]
        pw = [jnp.where(m['row'] > m['col'], m['kkt'] * m['decay'], 0.0) for m in tm]
        inv = [(m['row'] == m['col']).astype(F32) - p for m, p in zip(tm, pw)]
        for _ in range(int(math.log2(c)) - 1):
            pw = [_dot(p.astype(BF16), p.astype(BF16), 1, 0) for p in pw]
            inv = [i + _dot(i.astype(BF16), p.astype(BF16), 1, 0) for i, p in zip(inv, pw)]
        for h, m in zip(heads, tm):
            rhs = jnp.concatenate([v_ref[:, m['sl']] * m['b'], m['kb'] * m['e']], axis=1).astype(BF16)
            sol = _dot(inv[h].astype(BF16), rhs, 1, 0)
            u_ref[:, m['sl']] = sol[:, :hd]
            wq = jnp.concatenate([sol[:, hd:], m['q'] * m['e']], axis=0)
            kr = m['k'] * m['r']
            qk = m['qkt'] * m['decay']
            wq_ref[0, h], wqt_ref[0, h] = wq.astype(BF16), wq.T.astype(BF16)
            kr_ref[0, h], krt_ref[0, h] = kr.astype(BF16), kr.T.astype(BF16)
            qk_ref[0, h], qkt_ref[0, h] = qk.astype(BF16), qk.T.astype(BF16)
            invt_ref[0, h] = inv[h].T.astype(BF16)

    rows = pl.BlockSpec((c, w), lambda n: (n, 0))
    cols = pl.BlockSpec((c, DN_HEADS), lambda n: (n, 0))
    rowg = pl.BlockSpec((1, DN_HEADS, c), lambda n: (n, 0, 0))
    per = lambda a, b: (pl.BlockSpec((1, DN_HEADS, a, b), lambda n: (n, 0, 0, 0)),
                        jax.ShapeDtypeStruct((nch, DN_HEADS, a, b), BF16))
    outs = [(rows, jax.ShapeDtypeStruct((t, w), F32)), per(2 * c, hd), per(hd, 2 * c), per(c, hd), per(hd, c),
            per(c, c), per(c, c), per(c, c)]
    return pl.pallas_call(
        body, name=name, grid=(nch,), in_specs=[rows, rows, rows, cols, rowg, cols],
        out_specs=[o[0] for o in outs], out_shape=[o[1] for o in outs], compiler_params=_params(("parallel",)),
    )(q, k, v, gc, gcr, beta)


def _dn_state_fwd(u, wq, krt, qk, gc, *, name):
    t, w = u.shape
    hd = w // DN_HEADS
    c = DN_CHUNK
    nch = t // c

    def body(u_ref, wq_ref, krt_ref, qk_ref, gc_ref, o_ref, vn_ref, s_ref, state):
        @pl.when(pl.program_id(0) == 0)
        def _():
            state[...] = jnp.zeros_like(state)

        heads = range(DN_HEADS)
        sls = [slice(h * hd, (h + 1) * hd) for h in heads]
        shs = [state[h] for h in heads]
        wss = [_dot(wq_ref[0, h], shs[h].astype(BF16), 1, 0) for h in heads]
        vns = [(u_ref[:, sls[h]] - wss[h][:c]).astype(BF16) for h in heads]
        outs = [wss[h][c:] + _dot(qk_ref[0, h], vns[h], 1, 0) for h in heads]
        nxt = [shs[h] * jnp.exp(gc_ref[c - 1:c, h:h + 1]) + _dot(krt_ref[0, h], vns[h], 1, 0) for h in heads]
        for h in heads:
            s_ref[0, h] = shs[h]
            vn_ref[:, sls[h]] = vns[h]
            o_ref[:, sls[h]] = outs[h]
            state[h] = nxt[h]

    rows = pl.BlockSpec((c, w), lambda n: (n, 0))
    per = lambda a, b: pl.BlockSpec((1, DN_HEADS, a, b), lambda n: (n, 0, 0, 0))
    return pl.pallas_call(
        body, name=name, grid=(nch,),
        in_specs=[rows, per(2 * c, hd), per(hd, c), per(c, c), pl.BlockSpec((c, DN_HEADS), lambda n: (n, 0))],
        out_specs=[rows, rows, per(hd, hd)],
        out_shape=[jax.ShapeDtypeStruct((t, w), F32), jax.ShapeDtypeStruct((t, w), BF16),
                   jax.ShapeDtypeStruct((nch, DN_HEADS, hd, hd), F32)],
        scratch_shapes=[pltpu.VMEM((DN_HEADS, hd, hd), F32)], compiler_params=_params(("arbitrary",)),
    )(u, wq, krt, qk, gc)


def _dn_state_bwd(do, qkt, kr, vn, states, wqt, gc, *, name):
    t, w = do.shape
    hd = w // DN_HEADS
    c = DN_CHUNK
    nch = t // c

    def body(do_ref, qkt_ref, kr_ref, vn_ref, s_ref, wqt_ref, gc_ref, dvn_ref, dkr_ref, del_ref, dstate):
        @pl.when(pl.program_id(0) == 0)
        def _():
            dstate[...] = jnp.zeros_like(dstate)

        heads = range(DN_HEADS)
        sls = [slice(h * hd, (h + 1) * hd) for h in heads]
        dsns = [dstate[h] for h in heads]
        dsbs = [d.astype(BF16) for d in dsns]
        dobs = [do_ref[:, sl].astype(BF16) for sl in sls]
        dvns = [(_dot(qkt_ref[0, h], dobs[h], 1, 0) + _dot(kr_ref[0, h], dsbs[h], 1, 0)).astype(BF16) for h in heads]
        dkrs = [_dot(vn_ref[:, sls[h]], dsbs[h], 1, 1) for h in heads]
        nxt = [dsns[h] * jnp.exp(gc_ref[c - 1:c, h:h + 1])
               + _dot(wqt_ref[0, h], jnp.concatenate([-dvns[h], dobs[h]], axis=0), 1, 0) for h in heads]
        for h in heads:
            dvn_ref[:, sls[h]] = dvns[h]
            dkr_ref[:, sls[h]] = dkrs[h]
            d_el = jnp.sum(jnp.sum(dsns[h] * s_ref[0, h], axis=1, keepdims=True), axis=0, keepdims=True)
            del_ref[0, h:h + 1, :] = jnp.broadcast_to(d_el, (1, LANES))
            dstate[h] = nxt[h]

    rev = lambda n: nch - 1 - n
    rows = pl.BlockSpec((c, w), lambda n: (rev(n), 0))
    per = lambda a, b: pl.BlockSpec((1, DN_HEADS, a, b), lambda n: (rev(n), 0, 0, 0))
    return pl.pallas_call(
        body, name=name, grid=(nch,),
        in_specs=[rows, per(c, c), per(c, hd), rows, per(hd, hd), per(hd, 2 * c),
                  pl.BlockSpec((c, DN_HEADS), lambda n: (rev(n), 0))],
        out_specs=[rows, rows, pl.BlockSpec((1, DN_HEADS, LANES), lambda n: (rev(n), 0, 0))],
        out_shape=[jax.ShapeDtypeStruct((t, w), BF16), jax.ShapeDtypeStruct((t, w), F32),
                   jax.ShapeDtypeStruct((nch, DN_HEADS, LANES), F32)],
        scratch_shapes=[pltpu.VMEM((DN_HEADS, hd, hd), F32)], compiler_params=_params(("arbitrary",)),
    )(do, qkt, kr, vn, states, wqt, gc)


def _dn_local_bwd(q, k, v, gc, gcr, beta, invt, u, wq, vn, states, do, dvn, dkr, d_el, *, name):
    t, w = q.shape
    hd = w // DN_HEADS
    c = DN_CHUNK
    nch = t // c

    def body(q_ref, k_ref, v_ref, gc_ref, gcr_ref, b_ref, invt_ref, u_ref, wq_ref, vn_ref, s_ref, do_ref, dvn_ref,
             dkr_ref, del_ref, dq_ref, dk_ref, dv_ref, dgc_ref, dgr_ref, db_ref):
        heads = range(DN_HEADS)
        tms = [_dn_head_terms(q_ref, k_ref, gc_ref, gcr_ref, b_ref, h, hd) for h in heads]
        dobs = [do_ref[:, m['sl']].astype(BF16) for m in tms]
        tss = [_dot(jnp.concatenate([dob, -dvn_ref[:, m['sl']]], axis=0), s_ref[0, h].astype(BF16), 1, 1)
               for h, m, dob in zip(heads, tms, dobs)]
        d_qks = [_dot(dob, vn_ref[:, m['sl']], 1, 1) for m, dob in zip(tms, dobs)]
        d_rhss = [_dot(invt_ref[0, h], jnp.concatenate([dvn_ref[:, m['sl']], ts[c:].astype(BF16)], axis=1), 1, 0)
                  for h, m, ts in zip(heads, tms, tss)]
        d_as = [-_dot(d_rhs.astype(BF16),
                      jnp.concatenate([u_ref[:, m['sl']].astype(BF16), wq_ref[0, h, :c, :]], axis=1), 1, 1)
                for h, m, d_rhs in zip(heads, tms, d_rhss)]
        for h in heads:
            m, d_qe, d_qk, d_rhs, d_a = tms[h], tss[h][:c], d_qks[h], d_rhss[h], d_as[h]
            sl, qh, kh, bh, e, r, decay = m['sl'], m['q'], m['k'], m['b'], m['e'], m['r'], m['decay']
            vh = v_ref[:, sl]
            d_ru, d_rw = d_rhs[:, :hd], d_rhs[:, hd:]
            dv_ref[:, sl] = d_ru * bh
            d_e = jnp.sum(d_rw * m['kb'], axis=1, keepdims=True) + jnp.sum(d_qe * qh, axis=1, keepdims=True)
            d_n = jnp.where(m['row'] > m['col'], d_a, 0.0)
            d_m, d_p = (d_n * decay).astype(BF16), (d_qk * decay).astype(BF16)
            tk = _dot(jnp.concatenate([d_m, d_p], axis=0), kh.astype(BF16), 1, 0)
            d_kb = d_rw * e + tk[:c]
            dq_ref[:, sl] = tk[c:] + d_qe * e
            dk = _dot(d_m, m['kb'].astype(BF16), 0, 0) + _dot(d_p, qh.astype(BF16), 0, 0)
            dd = (d_n * m['kkt'] + d_qk * m['qkt']) * decay
            d_kr = dkr_ref[:, sl]
            d_r = jnp.sum(d_kr * kh, axis=1, keepdims=True)
            d_gl = del_ref[0, h:h + 1, 0:1] * m['el'] + jnp.sum(d_r * r, axis=0, keepdims=True)
            last = lax.broadcasted_iota(jnp.int32, (c, 1), 0) == c - 1
            dgc_ref[:, h:h + 1] = jnp.sum(dd, axis=1, keepdims=True) + d_e * e - d_r * r + jnp.where(last, d_gl, 0.0)
            dgr_ref[0, h:h + 1, :] = -jnp.sum(dd, axis=0, keepdims=True)
            dk_ref[:, sl] = dk + d_kr * r + d_kb * bh
            db_ref[:, h:h + 1] = jnp.sum(d_ru * vh, axis=1, keepdims=True) + jnp.sum(d_kb * kh, axis=1, keepdims=True)

    rows = pl.BlockSpec((c, w), lambda n: (n, 0))
    cols = pl.BlockSpec((c, DN_HEADS), lambda n: (n, 0))
    rowg = pl.BlockSpec((1, DN_HEADS, c), lambda n: (n, 0, 0))
    per = lambda a, b: pl.BlockSpec((1, DN_HEADS, a, b), lambda n: (n, 0, 0, 0))
    return pl.pallas_call(
        body, name=name, grid=(nch,),
        in_specs=[rows, rows, rows, cols, rowg, cols, per(c, c), rows, per(2 * c, hd), rows, per(hd, hd), rows, rows,
                  rows, pl.BlockSpec((1, DN_HEADS, LANES), lambda n: (n, 0, 0))],
        out_specs=[rows, rows, rows, cols, rowg, cols],
        out_shape=[jax.ShapeDtypeStruct((t, w), F32)] * 3
        + [jax.ShapeDtypeStruct((t, DN_HEADS), F32), jax.ShapeDtypeStruct((nch, DN_HEADS, c), F32),
           jax.ShapeDtypeStruct((t, DN_HEADS), F32)],
        compiler_params=_params(("parallel",)),
    )(q, k, v, gc, gcr, beta, invt, u, wq, vn, states, do, dvn, dkr, d_el)


def _other_chips():
    x, y = lax.axis_index("x"), lax.axis_index("y")
    return [(1 - x, y), (x, 1 - y), (1 - x, 1 - y)]


def _gather(src, *, name):
    def body(src_ref, out_ref, send_sems, recv_sems, local_sem):
        x, y, c = lax.axis_index("x"), lax.axis_index("y"), lax.axis_index("c")
        me, sibling, chips = (x, y, c), (x, y, 1 - c), _other_chips()
        slot = lambda px, py, pc: out_ref.at[4 * px + 2 * py + pc]

        def copy(k, block, to, own=False):
            return pltpu.make_async_remote_copy(
                src_ref=src_ref if own else slot(*block), dst_ref=slot(*block), send_sem=send_sems.at[k],
                recv_sem=recv_sems.at[k], device_id=to, device_id_type=MESH)

        local = pltpu.make_async_copy(src_ref, slot(*me), local_sem)
        local.start()
        first = [copy(0, me, sibling, own=True)] + [copy(1 + j, me, (*chip, c), own=True) for j, chip in enumerate(chips)]
        for cp in first:
            cp.start()
        passed = [copy(4 + j, (*chip, c), sibling) for j, chip in enumerate(chips)]
        for j, chip in enumerate(chips):
            copy(1 + j, (*chip, c), me).wait_recv()
            passed[j].start()
        copy(0, sibling, me).wait_recv()
        for j, chip in enumerate(chips):
            copy(4 + j, (*chip, 1 - c), me).wait_recv()
        for cp in first + passed:
            cp.wait_send()
        local.wait()

    return pl.pallas_call(
        body, name=name, out_shape=jax.ShapeDtypeStruct((N_DEV,) + src.shape, src.dtype),
        in_specs=[pl.BlockSpec(memory_space=pl.ANY)], out_specs=pl.BlockSpec(memory_space=pl.ANY),
        scratch_shapes=[pltpu.SemaphoreType.DMA((N_DEV - 1,)), pltpu.SemaphoreType.DMA((N_DEV - 1,)),
                        pltpu.SemaphoreType.DMA(())],
    )(src)


def _swap_with_sibling(src, *, name):
    def body(src_ref, out_ref, send_sem, recv_sem):
        x, y, c = lax.axis_index("x"), lax.axis_index("y"), lax.axis_index("c")
        cp = pltpu.make_async_remote_copy(src_ref=src_ref.at[1 - c], dst_ref=out_ref, send_sem=send_sem,
                                          recv_sem=recv_sem, device_id=(x, y, 1 - c), device_id_type=MESH)
        cp.start()
        cp.wait()

    return pl.pallas_call(
        body, name=name, out_shape=jax.ShapeDtypeStruct(src.shape[1:], src.dtype),
        in_specs=[pl.BlockSpec(memory_space=pl.ANY)], out_specs=pl.BlockSpec(memory_space=pl.ANY),
        scratch_shapes=[pltpu.SemaphoreType.DMA(()), pltpu.SemaphoreType.DMA(())],
    )(src)


def _add_own_half(src, got, *, name):
    _, n, r, c = src.shape
    tr = _tile(r, 512, 16)

    def body(s0_ref, s1_ref, got_ref, o_ref):
        own = jnp.where(lax.axis_index("c") == 0, s0_ref[0, 0], s1_ref[0, 0])
        o_ref[0] = (own + got_ref[0]).astype(BF16)

    half = lambda h: pl.BlockSpec((1, 1, tr, c), lambda j, i: (h, j, i, 0))
    blk = pl.BlockSpec((1, tr, c), lambda j, i: (j, i, 0))
    return pl.pallas_call(
        body, name=name, grid=(n, r // tr), in_specs=[half(0), half(1), blk], out_specs=blk,
        out_shape=jax.ShapeDtypeStruct((n, r, c), BF16), compiler_params=_params(("parallel", "parallel")),
    )(src, src, got)


def _swap_between_chips(src, *, name):
    def body(src_ref, out_ref, send_sems, recv_sems, local_sem):
        x, y, c = lax.axis_index("x"), lax.axis_index("y"), lax.axis_index("c")
        mine = 2 * x + y
        local = pltpu.make_async_copy(src_ref.at[mine], out_ref.at[mine], local_sem)
        local.start()
        copies = [pltpu.make_async_remote_copy(
            src_ref=src_ref.at[2 * px + py], dst_ref=out_ref.at[mine], send_sem=send_sems.at[j],
            recv_sem=recv_sems.at[j], device_id=(px, py, c), device_id_type=MESH) for j, (px, py) in enumerate(_other_chips())]
        for cp in copies:
            cp.start()
        for j, (px, py) in enumerate(_other_chips()):
            pltpu.make_async_remote_copy(
                src_ref=src_ref.at[mine], dst_ref=out_ref.at[2 * px + py], send_sem=send_sems.at[j],
                recv_sem=recv_sems.at[j], device_id=(px, py, c), device_id_type=MESH).wait_recv()
        for cp in copies:
            cp.wait_send()
        local.wait()

    return pl.pallas_call(
        body, name=name, out_shape=jax.ShapeDtypeStruct(src.shape, src.dtype),
        in_specs=[pl.BlockSpec(memory_space=pl.ANY)], out_specs=pl.BlockSpec(memory_space=pl.ANY),
        scratch_shapes=[pltpu.SemaphoreType.DMA((3,)), pltpu.SemaphoreType.DMA((3,)), pltpu.SemaphoreType.DMA(())],
    )(src)


def _sum_adamw(parts, w, m, v, *, name):
    n_parts, r, c = parts.shape
    tr = _tile(r, 256, 8)
    c1, c2 = 1.0 - ADAM_B1 ** ADAM_STEP, 1.0 - ADAM_B2 ** ADAM_STEP

    def body(p_ref, w_ref, m_ref, v_ref, g_ref, d_ref, nm_ref, nv_ref):
        g = p_ref[0].astype(F32)
        for s in range(1, n_parts):
            g = g + p_ref[s].astype(F32)
        nm = ADAM_B1 * m_ref[...] + (1.0 - ADAM_B1) * g
        nv = ADAM_B2 * v_ref[...] + (1.0 - ADAM_B2) * (g * g)
        g_ref[...] = g
        nm_ref[...] = nm
        nv_ref[...] = nv
        d_ref[...] = -ADAM_LR * ((nm / c1) / (jnp.sqrt(nv / c2) + ADAM_EPS) + ADAM_WD * w_ref[...])

    blk = pl.BlockSpec((tr, c), lambda i: (i, 0))
    return pl.pallas_call(
        body, name=name, grid=(r // tr,), in_specs=[pl.BlockSpec((n_parts, tr, c), lambda i: (0, i, 0)), blk, blk, blk],
        out_specs=[blk] * 4, out_shape=[jax.ShapeDtypeStruct((r, c), F32)] * 4, compiler_params=_params(("parallel",)),
    )(parts, w, m, v)


def _pack_rows(n):
    return -(-n // (PACK_COLS * PACK_ROWS)) * PACK_ROWS


def _pack(blocks):
    parts, spans, at = [], [], 0
    for blk, n_lead in blocks:
        lead = blk.shape[:n_lead]
        n = math.prod(blk.shape[n_lead:])
        rows = _pack_rows(n)
        flat = blk.reshape(lead + (n,))
        flat = jnp.pad(flat, [(0, 0)] * n_lead + [(0, rows * PACK_COLS - n)])
        parts.append(flat.reshape(lead + (rows, PACK_COLS)))
        spans.append((at, rows))
        at += rows
    return jnp.concatenate(parts, axis=-2), spans


def _unpack(buf, span, shape):
    at, rows = span
    lead = buf.shape[:-2]
    flat = lax.slice_in_dim(buf, at, at + rows, axis=buf.ndim - 2).reshape(lead + (rows * PACK_COLS,))
    return lax.slice_in_dim(flat, 0, math.prod(shape), axis=len(lead)).reshape(lead + tuple(shape))


def _join_shards(g, axis):
    g = jnp.moveaxis(g, 0, axis)
    return g.reshape(g.shape[:axis] + (g.shape[axis] * g.shape[axis + 1],) + g.shape[axis + 2:])


def _split_shards(full, axis):
    s = full.shape
    g = full.reshape(s[:axis] + (N_DEV, s[axis] // N_DEV) + s[axis + 1:])
    return jnp.moveaxis(g, axis, 0)


def _ffn_fwd(x, g, w_in, w_out, tag):
    h = _rms_fwd(x, g, name=tag + "_norm", out_dtype=BF16)
    gate, up, a = _mm_swiglu(h, w_in, name=tag + "_in", tm=512, tn=1408)
    return _mm(a, w_out, name=tag + "_out", res=x, scale=0.5, tk=2816), (x, h, gate, up, a)


def _ffn_bwd(saved, g, w_in, w_out, dxo, tag):
    x, h, gate, up, a = saved
    f = w_out.shape[0]
    d_w_out = _mm(a, dxo, name=tag + "_dwout", ta=True, scale=0.5, tm=1408, tk=2048)
    dgate, dup = _mm_dswiglu(dxo, w_out, gate, up, name=tag + "_da", scale=0.5, tm=512, tn=1408)
    d_w_in = jnp.concatenate([_mm(dgate, h, name=tag + "_dwin_g", ta=True, tm=1408, tk=2048),
                              _mm(dup, h, name=tag + "_dwin_u", ta=True, tm=1408, tk=2048)], axis=0)
    dh = _mm(dgate, w_in[:f], name=tag + "_dh_g", tk=2816)
    dh = _mm(dup, w_in[f:], name=tag + "_dh_u", tk=2816, res=dh)
    dx, dg = _rms_bwd(x, g, dh, dxo, name=tag + "_dnorm")
    return dx, dg, d_w_in, d_w_out


def _block_diag(w):
    n, j, k = w.shape
    return (w[:, :, None, :] * jnp.eye(n, dtype=w.dtype)[:, None, :, None]).reshape(n * j, n * k)


def _diag_blocks(dense, n):
    j, k = dense.shape[0] // n, dense.shape[1] // n
    return jnp.stack([dense[i * j:(i + 1) * j, i * k:(i + 1) * k] for i in range(n)], axis=0)


def _attn_lru_fwd(x, p, tag):
    aw = ATTN_HEADS * 64
    h = _rms_fwd(x, p['mix_norm'], name=tag + "_norm", out_dtype=BF16)
    proj = _mm(h, p['ab_w_in'], name=tag + "_in", tb=True, tn=1280)
    qkv = proj[:, :3 * aw].astype(BF16)
    views = {dil: qkv.reshape(qkv.shape[0] // dil, dil * 3 * aw) for _, dil in DILATED_PATTERNS}
    outs, lses = [], []
    for window, dil in DILATED_PATTERNS:
        assert window // dil == ATTN_BLOCK
        o, l = _dattn_fwd(views[dil], dil, name=f"{tag}_attn{dil}")
        outs.append(o)
        lses.append(l)
    wa, wx = _block_diag(p['lru_w_a']).astype(BF16), _block_diag(p['lru_w_x']).astype(BF16)
    sp, sp_vjp = jax.vjp(lambda lam: LRU_C * jax.nn.softplus(-lam), p['lru_lambda'])
    a, b, xc = _lru_gates_fwd(proj, 3, p['lru_conv_w'], p['lru_conv_b'], wa, p['lru_b_a'], wx, p['lru_b_x'], sp,
                              name=tag + "_gates")
    hs = _scan(a, b, name=tag + "_scan")
    cat, attn, lse_all = _mix_join_fwd(outs, lses, hs, proj, 4, name=tag + "_join")
    xo = _mm(cat, p['ab_w_out'], name=tag + "_out", res=x)
    return xo, (x, h, proj, views, attn, lse_all, a, hs, xc, wa, wx, sp, sp_vjp, cat)


def _attn_lru_bwd(saved, p, dxo, tag):
    x, h, proj, views, attn, lse_all, a, hs, xc, wa, wx, sp, sp_vjp, cat = saved
    aw = attn.shape[1]
    g = {'ab_w_out': _mm(cat, dxo, name=tag + "_dwout", ta=True)}
    dcat = _mm(dxo, p['ab_w_out'], name=tag + "_dcat", tb=True)
    delta, dhs, dgr = _mix_join_bwd(dcat, attn, hs, proj, 4, name=tag + "_djoin")
    a_next = jnp.concatenate([a[1:], jnp.zeros_like(a[:1])], axis=0)
    dtot = _scan(a_next, dhs, name=tag + "_dscan", reverse=True)
    h_prev = jnp.concatenate([jnp.zeros_like(hs[:1]), hs[:-1]], axis=0)
    dxc, dwa, dwx, vecs = _lru_gates_bwd(xc, dtot, h_prev, wa, p['lru_b_a'], wx, p['lru_b_x'], sp, name=tag + "_dgates")
    dxr, g['lru_conv_w'] = _conv_bwd(dxc, proj, 3, p['lru_conv_w'], name=tag + "_dconv")
    g['lru_b_a'], g['lru_b_x'], g['lru_conv_b'] = vecs[0], vecs[1], vecs[3]
    g['lru_lambda'], = sp_vjp(vecs[2])
    g['lru_w_a'], g['lru_w_x'] = _diag_blocks(dwa, LRU_BLOCKS), _diag_blocks(dwx, LRU_BLOCKS)
    dattn = dcat[:, :aw]
    dq = dk = dv = 0.0
    for _, dil in DILATED_PATTERNS:
        dq_, dk_, dv_ = _dattn_bwd(views[dil], dattn, lse_all, delta, dil, name=f"{tag}_dattn{dil}")
        dq, dk, dv = dq + dq_, dk + dk_, dv + dv_
    dproj = jnp.concatenate([dq.astype(BF16), dk.astype(BF16), dv.astype(BF16), dxr, dgr], axis=-1)
    g['ab_w_in'] = _mm(dproj, h, name=tag + "_dwin", ta=True, tm=1280)
    dh = _mm(dproj, p['ab_w_in'], name=tag + "_dh", tk=1280)
    dx, g['mix_norm'] = _rms_bwd(x, p['mix_norm'], dh, dxo, name=tag + "_dnorm")
    return dx, g


def _dn_decay(a, b, a_log, dt_bias):
    t = a.shape[0]
    g = -jnp.exp(a_log) * jax.nn.softplus(a + dt_bias)
    gc = jnp.cumsum(g.reshape(t // DN_CHUNK, DN_CHUNK, DN_HEADS), axis=1)
    return gc.reshape(t, DN_HEADS), jnp.swapaxes(gc, 1, 2), jax.nn.sigmoid(b)


def _dn_in_width(w):
    return -(-(4 * w + 2 * DN_HEADS) // LANES) * LANES


def _deltanet_fwd(x, p, tag):
    w = p['dn_w_out'].shape[0]
    h = _rms_fwd(x, p['mix_norm'], name=tag + "_norm", out_dtype=BF16)
    proj = _mm(h, p['dn_w_in'], name=tag + "_in", tb=True, tn=1408)
    q, k, v, c = _dn_prep_fwd(proj, p['dn_conv_w'], name=tag + "_prep")
    a, b = proj[:, 4 * w:4 * w + DN_HEADS], proj[:, 4 * w + DN_HEADS:4 * w + 2 * DN_HEADS]
    (gc, gcr, beta), decay_vjp = jax.vjp(_dn_decay, a, b, p['dn_a_log'], p['dn_dt_bias'])
    prep = (q, k, v, gc, gcr, beta)
    u, wq, wqt, kr, krt, qk, qkt, invt = _dn_local_fwd(*prep, name=tag + "_local")
    o, vn, states = _dn_state_fwd(u, wq, krt, qk, gc, name=tag + "_state")
    og = _dn_gate_fwd(o, proj, 3, p['dn_o_norm'], name=tag + "_gate")
    xo = _mm(og, p['dn_w_out'], name=tag + "_out", res=x)
    return xo, (x, h, proj, c, o, prep, (u, wq, wqt, kr, qkt, invt, vn, states), decay_vjp, og)


def _deltanet_bwd(saved, p, dxo, tag):
    x, h, proj, c, o, prep, (u, wq, wqt, kr, qkt, invt, vn, states), decay_vjp, og = saved
    g = {'dn_w_out': _mm(og, dxo, name=tag + "_dwout", ta=True)}
    dog = _mm(dxo, p['dn_w_out'], name=tag + "_dog", tb=True)
    do, dz, g['dn_o_norm'] = _dn_gate_bwd(o, proj, 3, p['dn_o_norm'], dog, name=tag + "_dgate")
    dvn, dkr, d_el = _dn_state_bwd(do, qkt, kr, vn, states, wqt, prep[3], name=tag + "_dstate")
    dq, dk, dv, dgc, dgr, dbeta = _dn_local_bwd(*prep, invt, u, wq, vn, states, do, dvn, dkr, d_el, name=tag + "_dlocal")
    dc = _dn_prep_bwd(c, dq, dk, dv, name=tag + "_dprep")
    dqkv, g['dn_conv_w'] = _conv_bwd(dc, proj, 0, p['dn_conv_w'], name=tag + "_dconv")
    da, db, g['dn_a_log'], g['dn_dt_bias'] = decay_vjp((dgc, dgr, dbeta))
    t = x.shape[0]
    pad = jnp.zeros((t, p['dn_w_in'].shape[0] - dqkv.shape[1] - dz.shape[1] - 2 * DN_HEADS), BF16)
    groups = [("qkv", dqkv), ("z", dz), ("ab", jnp.concatenate([da.astype(BF16), db.astype(BF16), pad], axis=-1))]
    parts, dh, lo = [], None, 0
    for label, piece in groups:
        hi = lo + piece.shape[1]
        parts.append(_mm(piece, h, name=f"{tag}_dwin_{label}", ta=True, tm=1536, tk=2048))
        dh = _mm(piece, p['dn_w_in'][lo:hi], name=f"{tag}_dh_{label}", tk=1536, res=dh)
        lo = hi
    g['dn_w_in'] = jnp.concatenate(parts, axis=0)
    dx, g['mix_norm'] = _rms_bwd(x, p['mix_norm'], dh, dxo, name=tag + "_dnorm")
    return dx, g


def _xattn_block_fwd(x, mem, p, tag):
    w = x.shape[1]
    h = _rms_fwd(x, p['xa_norm'], name=tag + "_norm", out_dtype=BF16)
    mh = _rms_fwd(mem, p['xa_mem_norm'], name=tag + "_mnorm", out_dtype=BF16)
    q = _mm(h, p['xa_wq'], name=tag + "_q")
    kv = _mm(mh, p['xa_wkv'], name=tag + "_kv", tb=True)
    k, v = kv[:, :w], kv[:, w:]
    o = _xattn_fwd(q, k, v, name=tag + "_attn").astype(BF16)
    xo = _mm(o, p['xa_wo'], name=tag + "_out", res=x)
    return xo, (x, h, mh, q, k, v, o)


def _xattn_block_bwd(saved, mem, p, dxo, tag):
    x, h, mh, q, k, v, o = saved
    g = {'xa_wo': _mm(o, dxo, name=tag + "_dwo", ta=True)}
    do = _mm(dxo, p['xa_wo'], name=tag + "_do", tb=True)
    dq, dk, dv = _xattn_bwd(q, k, v, do, name=tag + "_dattn")
    dq = dq.astype(BF16)
    dkv = jnp.concatenate([dk, dv], axis=-1).astype(BF16)
    g['xa_wq'] = _mm(h, dq, name=tag + "_dwq", ta=True)
    g['xa_wkv'] = _mm(dkv, mh, name=tag + "_dwkv", ta=True)
    dmh = _mm(dkv, p['xa_wkv'], name=tag + "_dmh")
    _, g['xa_mem_norm'] = _rms_bwd(mem, p['xa_mem_norm'], dmh, None, name=tag + "_dmnorm")
    dh = _mm(dq, p['xa_wq'], name=tag + "_dh", tb=True)
    dx, g['xa_norm'] = _rms_bwd(x, p['xa_norm'], dh, dxo, name=tag + "_dnorm")
    return dx, g


def _layer_params(full, layer):
    p = {n: full[n][layer] for n in ('ffn1_norm', 'ffn1_w_in', 'ffn1_w_out', 'mix_norm', 'xa_norm', 'xa_mem_norm',
                                     'xa_wq', 'xa_wkv', 'xa_wo', 'ffn2_norm', 'ffn2_w_in', 'ffn2_w_out')}
    mixer = ('ab_w_in', 'lru_conv_w', 'lru_conv_b', 'lru_w_a', 'lru_b_a', 'lru_w_x', 'lru_b_x', 'lru_lambda', 'ab_w_out') \
        if layer % 2 == 0 else ('dn_w_in', 'dn_conv_w', 'dn_a_log', 'dn_dt_bias', 'dn_o_norm', 'dn_w_out')
    p.update({n: full[n][layer // 2] for n in mixer})
    return p


def _step(x, mem, target, full, depth):
    saved = []
    for layer in range(depth):
        p = _layer_params(full, layer)
        tag = f"l{layer}"
        x, s1 = _ffn_fwd(x, p['ffn1_norm'], p['ffn1_w_in'], p['ffn1_w_out'], tag + "_ffn1")
        x, s2 = (_attn_lru_fwd if layer % 2 == 0 else _deltanet_fwd)(x, p, tag + "_mix")
        x, s3 = _xattn_block_fwd(x, mem, p, tag + "_xa")
        x, s4 = _ffn_fwd(x, p['ffn2_norm'], p['ffn2_w_in'], p['ffn2_w_out'], tag + "_ffn2")
        saved.append((p, s1, s2, s3, s4))
    sq, dx, d_final = _final_loss(x, full['final_norm'], target, name="final_loss")
    per_layer = []
    for layer in reversed(range(depth)):
        p, s1, s2, s3, s4 = saved[layer]
        tag = f"l{layer}"
        g = {}
        dx, g['ffn2_norm'], g['ffn2_w_in'], g['ffn2_w_out'] = _ffn_bwd(s4, p['ffn2_norm'], p['ffn2_w_in'], p['ffn2_w_out'], dx, tag + "_ffn2")
        dx, gx = _xattn_block_bwd(s3, mem, p, dx, tag + "_xa")
        dx, gm = (_attn_lru_bwd if layer % 2 == 0 else _deltanet_bwd)(s2, p, dx, tag + "_mix")
        dx, g['ffn1_norm'], g['ffn1_w_in'], g['ffn1_w_out'] = _ffn_bwd(s1, p['ffn1_norm'], p['ffn1_w_in'], p['ffn1_w_out'], dx, tag + "_ffn1")
        g.update(gx)
        g.update(gm)
        per_layer.insert(0, g)
    grads = {n: [g[n] for g in per_layer if n in g] for n in WEIGHTS[:-1]}
    grads['final_norm'] = d_final
    return sq, dx, grads


def kernel(x, mem, ffn1_norm, ffn1_w_in, ffn1_w_out, mix_norm, xa_norm, xa_mem_norm, xa_wq, xa_wkv, xa_wo, ffn2_norm, ffn2_w_in, ffn2_w_out, ab_w_in, lru_conv_w, lru_conv_b, lru_w_a, lru_b_a, lru_w_x, lru_b_x, lru_lambda, ab_w_out, dn_w_in, dn_conv_w, dn_a_log, dn_dt_bias, dn_o_norm, dn_w_out, final_norm, loss_target, m_ffn1_norm, m_ffn1_w_in, m_ffn1_w_out, m_mix_norm, m_xa_norm, m_xa_mem_norm, m_xa_wq, m_xa_wkv, m_xa_wo, m_ffn2_norm, m_ffn2_w_in, m_ffn2_w_out, m_ab_w_in, m_lru_conv_w, m_lru_conv_b, m_lru_w_a, m_lru_b_a, m_lru_w_x, m_lru_b_x, m_lru_lambda, m_ab_w_out, m_dn_w_in, m_dn_conv_w, m_dn_a_log, m_dn_dt_bias, m_dn_o_norm, m_dn_w_out, m_final_norm, v_ffn1_norm, v_ffn1_w_in, v_ffn1_w_out, v_mix_norm, v_xa_norm, v_xa_mem_norm, v_xa_wq, v_xa_wkv, v_xa_wo, v_ffn2_norm, v_ffn2_w_in, v_ffn2_w_out, v_ab_w_in, v_lru_conv_w, v_lru_conv_b, v_lru_w_a, v_lru_b_a, v_lru_w_x, v_lru_b_x, v_lru_lambda, v_ab_w_out, v_dn_w_in, v_dn_conv_w, v_dn_a_log, v_dn_dt_bias, v_dn_o_norm, v_dn_w_out, v_final_norm):
    args = dict(locals())
    flip = lambda n, a: jnp.swapaxes(a, 1, 2) if n in TRANSPOSED else a
    local = {n: flip(n, args[n]) for n in WEIGHTS}
    depth = ffn1_norm.shape[0]
    matrices = [n for n in WEIGHTS if n in SHARD_AXIS and n not in GATHER_F32]

    def entries(get):
        return [(a, n, l) for n in WEIGHTS for l, a in
                (enumerate(get(n)) if n in matrices else [(None, get(n))])]

    mats = [e for e in entries(lambda n: local[n]) if e[1] in matrices]
    send16, spans16 = _pack([(a.astype(BF16), 0) for a, _, _ in mats])
    send32, spans32 = _pack([(local[n], 0) for n in GATHER_F32])
    got16 = _gather(send16, name="gather_matrices")
    got32 = _gather(send32, name="gather_filters")
    full = {n: ([None] * local[n].shape[0] if n in matrices else local[n]) for n in WEIGHTS}
    for (a, n, l), span in zip(mats, spans16):
        full[n][l] = _unpack(got16, span, a.shape).reshape(N_DEV * a.shape[0], a.shape[1])
    for n, span in zip(GATHER_F32, spans32):
        full[n] = _join_shards(_unpack(got32, span, local[n].shape), SHARD_AXIS[n])
    dn_rows = full['dn_w_in'][0].shape[0]
    full['dn_w_in'] = [jnp.pad(w, ((0, _dn_in_width(w.shape[1]) - dn_rows), (0, 0))) for w in full['dn_w_in']]

    sq, dx, grads = _step(x[0], mem[0], loss_target[0], full, depth)
    grads['dn_w_in'] = [g[:dn_rows] for g in grads['dn_w_in']]
    loss = lax.psum(0.5 * jnp.sum(sq) / x.shape[2], ("x", "y", "c"))

    by_core = lambda z: jnp.swapaxes(z.reshape((N_DEV // 2, 2) + z.shape[1:]), 0, 1)

    def contribution(n):
        if n in matrices:
            return [by_core(g.reshape((N_DEV, g.shape[0] // N_DEV, g.shape[1]))) for g in grads[n]]
        g = grads[n] if n == 'final_norm' else jnp.stack(grads[n], axis=0)
        return by_core(_split_shards(g, SHARD_AXIS[n]) if n in SHARD_AXIS else jnp.broadcast_to(g, (N_DEV,) + g.shape))

    send, spans = _pack([(a, 2) for a, _, _ in entries(contribution)])
    got = _swap_with_sibling(send, name="grads_to_sibling")
    chip_sum = _add_own_half(send, got, name="grads_chip_sum")
    parts = _swap_between_chips(chip_sum, name="grads_between_chips")
    state = [_pack([(a, 0) for a, _, _ in entries(lambda n: flip(n, args[pre + n]))])[0] for pre in ("", "m_", "v_")]
    outs = _sum_adamw(parts, *state, name="sum_adamw")
    result = []
    for o in outs:
        got_rows = {}
        for (a, n, l), span in zip(entries(lambda n: local[n]), spans):
            got_rows.setdefault(n, []).append(_unpack(o, span, a.shape))
        result += [flip(n, jnp.stack(got_rows[n], axis=0) if n in matrices else got_rows[n][0]) for n in WEIGHTS]
    return (loss, dx[None], *result)
```

```python
import math

import jax
import jax.numpy as jnp
from jax import lax
from jax.experimental import pallas as pl
from jax.experimental.pallas import tpu as pltpu

F32, BF16 = jnp.float32, jnp.bfloat16
MESH = pl.DeviceIdType.MESH
N_DEV = 8
V7X_VMEM_LIMIT = 56 << 20
LANES = 128
PACK_COLS = 1024
PACK_ROWS = 16
NEG = -1e30

NORM_EPS = 1e-6
CONV_K = 4
ATTN_HEADS = 8
DILATED_PATTERNS = ((128, 1), (512, 4), (2048, 16))
ATTN_BLOCK = 128
LRU_BLOCKS = 8
LRU_C = 8.0
DN_HEADS = 8
DN_CHUNK = 64
DN_STEP_CHUNKS = 4
XA_HEADS = 4
ADAM_LR, ADAM_B1, ADAM_B2, ADAM_EPS, ADAM_WD, ADAM_STEP = 0.001, 0.9, 0.999, 1e-08, 0.01, 10

WEIGHTS = ['ffn1_norm', 'ffn1_w_in', 'ffn1_w_out', 'mix_norm', 'xa_norm', 'xa_mem_norm', 'xa_wq', 'xa_wkv', 'xa_wo',
           'ffn2_norm', 'ffn2_w_in', 'ffn2_w_out', 'ab_w_in', 'lru_conv_w', 'lru_conv_b', 'lru_w_a', 'lru_b_a',
           'lru_w_x', 'lru_b_x', 'lru_lambda', 'ab_w_out', 'dn_w_in', 'dn_conv_w', 'dn_a_log', 'dn_dt_bias',
           'dn_o_norm', 'dn_w_out', 'final_norm']
SHARD_AXIS = {'ffn1_w_in': 2, 'ffn1_w_out': 1, 'xa_wq': 1, 'xa_wkv': 2, 'xa_wo': 1, 'ffn2_w_in': 2, 'ffn2_w_out': 1,
              'ab_w_in': 2, 'lru_conv_w': 2, 'ab_w_out': 1, 'dn_w_in': 2, 'dn_conv_w': 2, 'dn_w_out': 1}
GATHER_F32 = ('lru_conv_w', 'dn_conv_w')
TRANSPOSED = ('ffn1_w_in', 'xa_wkv', 'ffn2_w_in', 'ab_w_in', 'dn_w_in')


def _params(sem=None):
    return pltpu.CompilerParams(dimension_semantics=sem, vmem_limit_bytes=V7X_VMEM_LIMIT)


def _tile(n, pref, mult):
    best = None
    t = mult
    while t <= min(n, pref):
        if n % t == 0:
            best = t
        t += mult
    return n if best is None else best


def _dot(a, b, ca, cb, prec=None):
    return lax.dot_general(a, b, (((ca,), (cb,)), ((), ())), preferred_element_type=F32, precision=prec)


def _mm(a, b, *, name, ta=False, tb=False, out_dtype=F32, res=None, scale=1.0, tm=1024, tn=1024, tk=1024,
        norm=None, rms_bwd=None):
    m, kdim = (a.shape[1], a.shape[0]) if ta else a.shape
    n = b.shape[0] if tb else b.shape[1]
    assert (b.shape[1] if tb else b.shape[0]) == kdim
    tm = _tile(m, tm, LANES if ta else 16)
    tn = _tile(n, tn, LANES)
    tk = _tile(kdim, tk, LANES)
    nk = kdim // tk
    a_spec = pl.BlockSpec((tk, tm), lambda i, j, k: (k, i)) if ta else pl.BlockSpec((tm, tk), lambda i, j, k: (i, k))
    b_spec = pl.BlockSpec((tn, tk), lambda i, j, k: (j, k)) if tb else pl.BlockSpec((tk, tn), lambda i, j, k: (k, j))
    o_spec = pl.BlockSpec((tm, tn), lambda i, j, k: (i, j))
    vec = pl.BlockSpec((1, tn), lambda i, j, k: (0, j))
    ca, cb = (0 if ta else 1), (1 if tb else 0)
    has_res = res is not None
    assert (norm is None and rms_bwd is None) or tn == n

    ins, specs = [a, b], [a_spec, b_spec]
    if has_res:
        ins.append(res)
        specs.append(o_spec)
    out_specs, out_shape = [o_spec], [jax.ShapeDtypeStruct((m, n), out_dtype)]
    if norm is not None:
        ins.append(norm.reshape(1, n))
        specs.append(vec)
        out_specs.append(o_spec)
        out_shape.append(jax.ShapeDtypeStruct((m, n), BF16))
    if rms_bwd is not None:
        ins += [rms_bwd[0], rms_bwd[1].reshape(1, n), rms_bwd[2]]
        specs += [o_spec, vec, o_spec]
        out_specs.append(vec)
        out_shape.append(jax.ShapeDtypeStruct((1, n), F32))
    n_in, n_out = len(ins), len(out_specs)

    def finish(acc, extra, outs, first_rows):
        y = acc if scale == 1.0 else acc * scale
        if has_res:
            y = y + extra[0][...]
        tail = extra[has_res:]
        if norm is not None:
            rs = lax.rsqrt(jnp.mean(y * y, axis=-1, keepdims=True) + NORM_EPS)
            outs[1][...] = (y * rs * tail[0][...]).astype(BF16)
        if rms_bwd is not None:
            x_ref, g_ref, dres_ref = tail
            xv = x_ref[...]
            rs = lax.rsqrt(jnp.mean(xv * xv, axis=-1, keepdims=True) + NORM_EPS)
            xh = xv * rs
            dgh = y * g_ref[...]
            part = jnp.sum(y * xh, axis=0, keepdims=True)
            y = dres_ref[...] + rs * (dgh - xh * jnp.mean(dgh * xh, axis=-1, keepdims=True))

            @pl.when(first_rows)
            def _():
                outs[1][...] = part

            @pl.when(jnp.logical_not(first_rows))
            def _():
                outs[1][...] += part

        outs[0][...] = y.astype(out_dtype)

    def body(*refs):
        a_ref, b_ref = refs[0], refs[1]
        extra, outs = refs[2:n_in], refs[n_in:n_in + n_out]
        p = _dot(a_ref[...].astype(BF16), b_ref[...].astype(BF16), ca, cb)
        first_rows = pl.program_id(0) == 0
        if nk == 1:
            finish(p, extra, outs, first_rows)
            return
        acc = refs[n_in + n_out]
        k = pl.program_id(2)

        @pl.when(k == 0)
        def _():
            acc[...] = p

        @pl.when(k > 0)
        def _():
            acc[...] += p

        @pl.when(k == nk - 1)
        def _():
            finish(acc[...], extra, outs, first_rows)

    out = pl.pallas_call(
        body, name=name, grid=(m // tm, n // tn, nk), in_specs=specs, out_specs=out_specs, out_shape=out_shape,
        scratch_shapes=[] if nk == 1 else [pltpu.VMEM((tm, tn), F32)],
        compiler_params=_params(("parallel" if rms_bwd is None else "arbitrary", "parallel", "arbitrary")),
    )(*ins)
    if rms_bwd is not None:
        return out[0], out[1].reshape(n)
    return out[0] if n_out == 1 else tuple(out)


def _rms_fwd(x, g, *, name, out_dtype):
    r, d = x.shape
    tr = _tile(r, 512, 16)

    def body(x_ref, g_ref, o_ref):
        xv = x_ref[...]
        rs = lax.rsqrt(jnp.mean(xv * xv, axis=-1, keepdims=True) + NORM_EPS)
        o_ref[...] = (xv * rs * g_ref[...]).astype(out_dtype)

    return pl.pallas_call(
        body, name=name, grid=(r // tr,),
        in_specs=[pl.BlockSpec((tr, d), lambda i: (i, 0)), pl.BlockSpec((1, d), lambda i: (0, 0))],
        out_specs=pl.BlockSpec((tr, d), lambda i: (i, 0)),
        out_shape=jax.ShapeDtypeStruct((r, d), out_dtype), compiler_params=_params(("parallel",)),
    )(x, g.reshape(1, d))


def _rms_bwd(x, g, dh, dres, *, name):
    r, d = x.shape
    tr = _tile(r, 512, 8)
    has_res = dres is not None

    def body(*refs):
        x_ref, g_ref, dh_ref = refs[:3]
        r_ref = refs[3] if has_res else None
        dx_ref, dg_ref = refs[3 + has_res], refs[4 + has_res]
        xv = x_ref[...]
        rs = lax.rsqrt(jnp.mean(xv * xv, axis=-1, keepdims=True) + NORM_EPS)
        xh = xv * rs
        dhv = dh_ref[...]
        dgh = dhv * g_ref[...]
        dx = rs * (dgh - xh * jnp.mean(dgh * xh, axis=-1, keepdims=True))
        if has_res:
            dx = dx + r_ref[...]
        dx_ref[...] = dx
        part = jnp.sum(dhv * xh, axis=0, keepdims=True)

        @pl.when(pl.program_id(0) == 0)
        def _():
            dg_ref[...] = part

        @pl.when(pl.program_id(0) > 0)
        def _():
            dg_ref[...] += part

    row = pl.BlockSpec((tr, d), lambda i: (i, 0))
    vec = pl.BlockSpec((1, d), lambda i: (0, 0))
    ins, specs = [x, g.reshape(1, d), dh], [row, vec, row]
    if has_res:
        ins.append(dres)
        specs.append(row)
    dx, dg = pl.pallas_call(
        body, name=name, grid=(r // tr,), in_specs=specs, out_specs=[row, vec],
        out_shape=[jax.ShapeDtypeStruct((r, d), F32), jax.ShapeDtypeStruct((1, d), F32)],
        compiler_params=_params(("arbitrary",)),
    )(*ins)
    return dx, dg.reshape(d)


def _mm_swiglu(h, w_in, *, name, tm=1024, tn=256):
    t, d = h.shape
    f = w_in.shape[0] // 2
    tm, tn = _tile(t, tm, 16), _tile(f, tn, LANES)
    nj = f // tn

    def body(h_ref, wg_ref, wu_ref, g_ref, u_ref, a_ref):
        hv = h_ref[...]
        gate, up = _dot(hv, wg_ref[...], 1, 1), _dot(hv, wu_ref[...], 1, 1)
        g_ref[...] = gate.astype(BF16)
        u_ref[...] = up.astype(BF16)
        a_ref[...] = (gate * jax.nn.sigmoid(gate) * up).astype(BF16)

    out = pl.BlockSpec((tm, tn), lambda i, j: (i, j))
    return pl.pallas_call(
        body, name=name, grid=(t // tm, nj),
        in_specs=[pl.BlockSpec((tm, d), lambda i, j: (i, 0)), pl.BlockSpec((tn, d), lambda i, j: (j, 0)),
                  pl.BlockSpec((tn, d), lambda i, j: (j + nj, 0))],
        out_specs=[out, out, out], out_shape=[jax.ShapeDtypeStruct((t, f), BF16)] * 3,
        compiler_params=_params(("parallel", "parallel")),
    )(h, w_in, w_in)


def _mm_dswiglu(dy, w_out, gate, up, *, name, scale, tm=1024, tn=256):
    t, d = dy.shape
    f = w_out.shape[0]
    tm, tn = _tile(t, tm, 16), _tile(f, tn, LANES)

    def body(dy_ref, w_ref, g_ref, u_ref, dg_ref, du_ref):
        da = _dot(dy_ref[...].astype(BF16), w_ref[...], 1, 1) * scale
        gv = g_ref[...].astype(F32)
        s = jax.nn.sigmoid(gv)
        dg_ref[...] = (da * u_ref[...].astype(F32) * (s * (1.0 + gv * (1.0 - s)))).astype(BF16)
        du_ref[...] = (da * gv * s).astype(BF16)

    blk = pl.BlockSpec((tm, tn), lambda i, j: (i, j))
    return pl.pallas_call(
        body, name=name, grid=(t // tm, f // tn),
        in_specs=[pl.BlockSpec((tm, d), lambda i, j: (i, 0)), pl.BlockSpec((tn, d), lambda i, j: (j, 0)), blk, blk],
        out_specs=[blk, blk], out_shape=[jax.ShapeDtypeStruct((t, f), BF16)] * 2,
        compiler_params=_params(("parallel", "parallel")),
    )(dy, w_out, gate, up)


def _final_loss(x, g, target, *, name):
    r, d = x.shape
    tr = _tile(r, 512, 8)

    def body(x_ref, g_ref, t_ref, sq_ref, dx_ref, dg_ref):
        xv = x_ref[...]
        gv = g_ref[...]
        rs = lax.rsqrt(jnp.mean(xv * xv, axis=-1, keepdims=True) + NORM_EPS)
        xh = xv * rs
        err = xh * gv - t_ref[...]
        dy = err * (1.0 / d)
        dgh = dy * gv
        dx_ref[...] = rs * (dgh - xh * jnp.mean(dgh * xh, axis=-1, keepdims=True))
        sq = jnp.sum(err * err, axis=0, keepdims=True)
        part = jnp.sum(dy * xh, axis=0, keepdims=True)

        @pl.when(pl.program_id(0) == 0)
        def _():
            sq_ref[...] = sq
            dg_ref[...] = part

        @pl.when(pl.program_id(0) > 0)
        def _():
            sq_ref[...] += sq
            dg_ref[...] += part

    row = pl.BlockSpec((tr, d), lambda i: (i, 0))
    vec = pl.BlockSpec((1, d), lambda i: (0, 0))
    sq, dx, dg = pl.pallas_call(
        body, name=name, grid=(r // tr,), in_specs=[row, vec, row], out_specs=[vec, row, vec],
        out_shape=[jax.ShapeDtypeStruct((1, d), F32), jax.ShapeDtypeStruct((r, d), F32),
                   jax.ShapeDtypeStruct((1, d), F32)],
        compiler_params=_params(("arbitrary",)),
    )(x, g.reshape(1, d), target)
    return sq.reshape(d), dx, dg.reshape(d)


def _scan(a, b, *, name, reverse=False):
    t, w = a.shape
    tb = _tile(t, 1024, 8)
    nblk, ngrp = t // tb, tb // 8

    def body(a_ref, b_ref, h_ref, carry):
        @pl.when(pl.program_id(0) == 0)
        def _():
            carry[...] = jnp.zeros_like(carry)

        row = lax.broadcasted_iota(jnp.int32, (8, w), 0)

        def group(i, c):
            r0 = pl.multiple_of((ngrp - 1 - i if reverse else i) * 8, 8)
            av, bv = a_ref[pl.ds(r0, 8), :], b_ref[pl.ds(r0, 8), :]
            for s in (1, 2, 4):
                keep = row < 8 - s if reverse else row >= s
                shift = 8 - s if reverse else s
                bv = jnp.where(keep, bv + av * pltpu.roll(bv, shift, axis=0), bv)
                av = jnp.where(keep, av * pltpu.roll(av, shift, axis=0), av)
            hv = bv + av * c
            h_ref[pl.ds(r0, 8), :] = hv
            return hv[0:1, :] if reverse else hv[7:8, :]

        carry[0:1, :] = lax.fori_loop(0, ngrp, group, carry[0:1, :])

    blk = pl.BlockSpec((tb, w), lambda i: (nblk - 1 - i if reverse else i, 0))
    return pl.pallas_call(
        body, name=name, grid=(t // tb,), in_specs=[blk, blk], out_specs=blk,
        out_shape=jax.ShapeDtypeStruct((t, w), F32), scratch_shapes=[pltpu.VMEM((8, w), F32)],
        compiler_params=_params(("arbitrary",)),
    )(a, b)


def _row_specs(t, tb, width, col):
    per = tb // 8
    main = pl.BlockSpec((tb, width), lambda i: (i, col))
    before = pl.BlockSpec((8, width), lambda i: (jnp.maximum(i * per - 1, 0), col))
    after = pl.BlockSpec((8, width), lambda i: (jnp.minimum((i + 1) * per, t // 8 - 1), col))
    return main, before, after


def _with_rows_before(x_ref, before_ref):
    return jnp.concatenate([jnp.where(pl.program_id(0) > 0, before_ref[...], 0.0), x_ref[...]], axis=0)


def _tap(xe, s):
    return xe[8:] if s == 0 else pltpu.roll(xe, s, axis=0)[8:]


def _conv_rows(xe, w_ref):
    return sum(_tap(xe, CONV_K - 1 - k) * w_ref[k:k + 1, :] for k in range(CONV_K))


def _conv_bwd(dy, x, col, w, *, name):
    t, width = dy.shape
    tb = _tile(t, 256, 8)
    nblk = t // tb

    def body(dy_ref, dy_after_ref, x_ref, x_before_ref, w_ref, dx_ref, dw_ref):
        i = pl.program_id(0)
        dyv = dy_ref[...]
        dye = jnp.concatenate([dyv, jnp.where(i < nblk - 1, dy_after_ref[...], 0.0)], axis=0)
        dx = dyv * w_ref[CONV_K - 1:CONV_K, :]
        for s in range(1, CONV_K):
            dx = dx + pltpu.roll(dye, tb + 8 - s, axis=0)[:tb] * w_ref[CONV_K - 1 - s:CONV_K - s, :]
        dx_ref[...] = dx.astype(BF16)
        xe = _with_rows_before(x_ref, x_before_ref)

        @pl.when(i == 0)
        def _():
            dw_ref[...] = jnp.zeros_like(dw_ref)

        for k in range(CONV_K):
            dw_ref[k:k + 1, :] += jnp.sum(dyv * _tap(xe, CONV_K - 1 - k), axis=0, keepdims=True)

    main, _, after = _row_specs(t, tb, width, 0)
    xmain, xbefore, _ = _row_specs(t, tb, width, col)
    dx, dw = pl.pallas_call(
        body, name=name, grid=(nblk,),
        in_specs=[main, after, xmain, xbefore, pl.BlockSpec((CONV_K, width), lambda i: (0, 0))],
        out_specs=[main, pl.BlockSpec((8, width), lambda i: (0, 0))],
        out_shape=[jax.ShapeDtypeStruct((t, width), BF16), jax.ShapeDtypeStruct((8, width), F32)],
        compiler_params=_params(("arbitrary",)),
    )(dy, dy, x, x, w)
    return dx, dw[:CONV_K]


def _expm1(x):
    small = x * (1.0 + x * (0.5 + x * (1.0 / 6.0 + x * (1.0 / 24.0 + x * (1.0 / 120.0 + x * (1.0 / 720.0))))))
    return jnp.where(jnp.abs(x) < 0.1, small, jnp.exp(x) - 1.0)


def _lru_gate_terms(xc, wa_ref, ba_ref, wx_ref, bx_ref, sp_ref):
    xb = xc.astype(BF16)
    r = jax.nn.sigmoid(_dot(xb, wa_ref[...], 1, 0) + ba_ref[...])
    i = jax.nn.sigmoid(_dot(xb, wx_ref[...], 1, 0) + bx_ref[...])
    log_a = -r * sp_ref[...]
    return r, i, jnp.exp(log_a), jnp.sqrt(-_expm1(2.0 * log_a))


def _lru_gates_fwd(proj, col, conv_w, conv_b, wa, ba, wx, bx, sp, *, name):
    t = proj.shape[0]
    width = conv_w.shape[1]
    tb = _tile(t, 512, 8)

    def body(x_ref, x_before_ref, cw_ref, cb_ref, wa_ref, ba_ref, wx_ref, bx_ref, sp_ref, a_ref, b_ref, xc_ref):
        xc = _conv_rows(_with_rows_before(x_ref, x_before_ref), cw_ref) + cb_ref[...]
        r, i, a, mult = _lru_gate_terms(xc, wa_ref, ba_ref, wx_ref, bx_ref, sp_ref)
        a_ref[...] = a
        b_ref[...] = mult * i * xc
        xc_ref[...] = xc

    main, before, _ = _row_specs(t, tb, width, col)
    out = pl.BlockSpec((tb, width), lambda i: (i, 0))
    vec = pl.BlockSpec((1, width), lambda i: (0, 0))
    mat = pl.BlockSpec((width, width), lambda i: (0, 0))
    return pl.pallas_call(
        body, name=name, grid=(t // tb,),
        in_specs=[main, before, pl.BlockSpec((CONV_K, width), lambda i: (0, 0)), vec, mat, vec, mat, vec, vec],
        out_specs=[out] * 3, out_shape=[jax.ShapeDtypeStruct((t, width), F32)] * 3,
        compiler_params=_params(("parallel",)),
    )(proj, proj, conv_w, conv_b.reshape(1, -1), wa, ba.reshape(1, -1), wx, bx.reshape(1, -1), sp.reshape(1, -1))


def _lru_gates_bwd(xc, dtot, h_prev, wa, ba, wx, bx, sp, *, name):
    t, width = xc.shape
    tb = _tile(t, 512, 8)

    def body(xc_ref, dt_ref, hp_ref, wa_ref, ba_ref, wx_ref, bx_ref, sp_ref, dxc_ref, dwa_ref, dwx_ref, vec_ref):
        xc = xc_ref[...]
        r, i, a, mult = _lru_gate_terms(xc, wa_ref, ba_ref, wx_ref, bx_ref, sp_ref)
        db = dt_ref[...]
        d_la = db * hp_ref[...] * a - db * i * xc * (a * a / mult)
        d_pa = (-d_la * sp_ref[...]) * r * (1.0 - r)
        d_pi = db * mult * xc * i * (1.0 - i)
        dab, dib = d_pa.astype(BF16), d_pi.astype(BF16)
        dxc = db * mult * i + _dot(dab, wa_ref[...], 1, 1) + _dot(dib, wx_ref[...], 1, 1)
        dxc_ref[...] = dxc
        xb = xc.astype(BF16)
        rows = [jnp.sum(z, axis=0, keepdims=True) for z in (d_pa, d_pi, -d_la * r, dxc)]

        @pl.when(pl.program_id(0) == 0)
        def _():
            dwa_ref[...] = jnp.zeros_like(dwa_ref)
            dwx_ref[...] = jnp.zeros_like(dwx_ref)
            vec_ref[...] = jnp.zeros_like(vec_ref)

        dwa_ref[...] += _dot(xb, dab, 0, 0)
        dwx_ref[...] += _dot(xb, dib, 0, 0)
        for j, z in enumerate(rows):
            vec_ref[j:j + 1, :] += z

    blk = pl.BlockSpec((tb, width), lambda i: (i, 0))
    vec = pl.BlockSpec((1, width), lambda i: (0, 0))
    mat = pl.BlockSpec((width, width), lambda i: (0, 0))
    dxc, dwa, dwx, vecs = pl.pallas_call(
        body, name=name, grid=(t // tb,), in_specs=[blk, blk, blk, mat, vec, mat, vec, vec],
        out_specs=[blk, mat, mat, pl.BlockSpec((8, width), lambda i: (0, 0))],
        out_shape=[jax.ShapeDtypeStruct((t, width), F32), jax.ShapeDtypeStruct((width, width), F32),
                   jax.ShapeDtypeStruct((width, width), F32), jax.ShapeDtypeStruct((8, width), F32)],
        compiler_params=_params(("arbitrary",)),
    )(xc, dtot, h_prev, wa, ba.reshape(1, -1), wx, bx.reshape(1, -1), sp.reshape(1, -1))
    return dxc, dwa, dwx, vecs[:4]


GELU_C = math.sqrt(2.0 / math.pi)


def _gelu_terms(x):
    th = jnp.tanh(GELU_C * (x + 0.044715 * x * x * x))
    return 0.5 * x * (1.0 + th), 0.5 * (1.0 + th) + 0.5 * x * (1.0 - th * th) * GELU_C * (1.0 + 3 * 0.044715 * x * x)


def _mix_join_fwd(outs, lses, hs, proj, gr_col, *, name):
    t, w = hs.shape
    tb = _tile(t, 512, 16)
    n = len(outs)

    def body(*refs):
        o_refs, l_refs = refs[:n], refs[n:2 * n]
        hs_ref, gr_ref, cat_ref, attn_ref, lse_ref = refs[2 * n:]
        ls = [r[...] for r in l_refs]
        m = ls[0]
        for l in ls[1:]:
            m = jnp.maximum(m, l)
        ws = [jnp.exp(l - m) for l in ls]
        den = sum(ws)
        attn = sum(wt * r[...] for wt, r in zip(ws, o_refs)) / den
        attn_ref[...] = attn
        lse_ref[...] = m + jnp.log(den)
        cat_ref[:, :w] = attn.astype(BF16)
        cat_ref[:, w:] = (hs_ref[...] * _gelu_terms(gr_ref[...])[0]).astype(BF16)

    blk = pl.BlockSpec((tb, w), lambda i: (i, 0))
    return pl.pallas_call(
        body, name=name, grid=(t // tb,),
        in_specs=[blk] * (2 * n + 1) + [pl.BlockSpec((tb, w), lambda i: (i, gr_col))],
        out_specs=[pl.BlockSpec((tb, 2 * w), lambda i: (i, 0)), blk, blk],
        out_shape=[jax.ShapeDtypeStruct((t, 2 * w), BF16), jax.ShapeDtypeStruct((t, w), F32),
                   jax.ShapeDtypeStruct((t, w), F32)],
        compiler_params=_params(("parallel",)),
    )(*outs, *lses, hs, proj)


def _mix_join_bwd(dcat, attn, hs, proj, gr_col, *, name):
    t, w = hs.shape
    hd = w // ATTN_HEADS
    tb = _tile(t, 512, 16)

    def body(dcat_ref, attn_ref, hs_ref, gr_ref, delta_ref, dhs_ref, dgr_ref):
        for h in range(ATTN_HEADS):
            sl = slice(h * hd, (h + 1) * hd)
            d = jnp.sum(dcat_ref[:, sl] * attn_ref[:, sl], axis=-1, keepdims=True)
            delta_ref[:, sl] = jnp.broadcast_to(d, (tb, hd))
        dy = dcat_ref[:, w:]
        g, dg = _gelu_terms(gr_ref[...])
        dhs_ref[...] = dy * g
        dgr_ref[...] = (dy * hs_ref[...] * dg).astype(BF16)

    blk = pl.BlockSpec((tb, w), lambda i: (i, 0))
    return pl.pallas_call(
        body, name=name, grid=(t // tb,),
        in_specs=[pl.BlockSpec((tb, 2 * w), lambda i: (i, 0)), blk, blk, pl.BlockSpec((tb, w), lambda i: (i, gr_col))],
        out_specs=[blk, blk, blk],
        out_shape=[jax.ShapeDtypeStruct((t, w), F32), jax.ShapeDtypeStruct((t, w), F32),
                   jax.ShapeDtypeStruct((t, w), BF16)],
        compiler_params=_params(("parallel",)),
    )(dcat, attn, hs, proj)


def _silu_terms(x):
    s = jax.nn.sigmoid(x)
    return x * s, s * (1.0 + x * (1.0 - s))


def _dn_prep_fwd(proj, conv_w, *, name):
    t = proj.shape[0]
    w3 = conv_w.shape[1]
    w = w3 // 3
    hd = w // DN_HEADS
    tb = _tile(t, 256, 8)

    def body(x_ref, x_before_ref, cw_ref, q_ref, k_ref, v_ref, c_ref):
        c = _conv_rows(_with_rows_before(x_ref, x_before_ref), cw_ref)
        c_ref[...] = c
        s = _silu_terms(c)[0]
        v_ref[...] = s[:, 2 * w:]
        for part, ref, scale in ((0, q_ref, hd ** -0.5), (1, k_ref, 1.0)):
            for h in range(DN_HEADS):
                z = s[:, part * w + h * hd:part * w + (h + 1) * hd]
                ref[:, h * hd:(h + 1) * hd] = z * (lax.rsqrt(jnp.sum(z * z, axis=-1, keepdims=True) + 1e-6) * scale)

    main, before, _ = _row_specs(t, tb, w3, 0)
    out = pl.BlockSpec((tb, w), lambda i: (i, 0))
    return pl.pallas_call(
        body, name=name, grid=(t // tb,), in_specs=[main, before, pl.BlockSpec((CONV_K, w3), lambda i: (0, 0))],
        out_specs=[out, out, out, pl.BlockSpec((tb, w3), lambda i: (i, 0))],
        out_shape=[jax.ShapeDtypeStruct((t, w), F32)] * 3 + [jax.ShapeDtypeStruct((t, w3), F32)],
        compiler_params=_params(("parallel",)),
    )(proj, proj, conv_w)


def _dn_prep_bwd(c, dq, dk, dv, *, name):
    t, w3 = c.shape
    w = w3 // 3
    hd = w // DN_HEADS
    tb = _tile(t, 256, 8)

    def body(c_ref, dq_ref, dk_ref, dv_ref, dc_ref):
        cv = c_ref[...]
        s, ds = _silu_terms(cv)
        dc_ref[:, 2 * w:] = dv_ref[...] * ds[:, 2 * w:]
        for part, ref, scale in ((0, dq_ref, hd ** -0.5), (1, dk_ref, 1.0)):
            for h in range(DN_HEADS):
                cols = slice(part * w + h * hd, part * w + (h + 1) * hd)
                z = s[:, cols]
                rn = lax.rsqrt(jnp.sum(z * z, axis=-1, keepdims=True) + 1e-6)
                y = z * rn
                dy = ref[:, h * hd:(h + 1) * hd] * scale
                dc_ref[:, cols] = rn * (dy - y * jnp.sum(dy * y, axis=-1, keepdims=True)) * ds[:, cols]

    blk = pl.BlockSpec((tb, w), lambda i: (i, 0))
    wide = pl.BlockSpec((tb, w3), lambda i: (i, 0))
    return pl.pallas_call(
        body, name=name, grid=(t // tb,), in_specs=[wide, blk, blk, blk], out_specs=wide,
        out_shape=jax.ShapeDtypeStruct((t, w3), F32), compiler_params=_params(("parallel",)),
    )(c, dq, dk, dv)


def _dn_gate_fwd(o, proj, z_col, o_norm, *, name):
    t, w = o.shape
    hd = w // DN_HEADS
    tb = _tile(t, 512, 16)

    def body(o_ref, z_ref, g_ref, y_ref):
        for h in range(DN_HEADS):
            sl = slice(h * hd, (h + 1) * hd)
            ov = o_ref[:, sl]
            rn = lax.rsqrt(jnp.mean(ov * ov, axis=-1, keepdims=True) + NORM_EPS)
            y_ref[:, sl] = (ov * rn * g_ref[...] * _silu_terms(z_ref[:, sl])[0]).astype(BF16)

    blk = pl.BlockSpec((tb, w), lambda i: (i, 0))
    return pl.pallas_call(
        body, name=name, grid=(t // tb,),
        in_specs=[blk, pl.BlockSpec((tb, w), lambda i: (i, z_col)), pl.BlockSpec((1, hd), lambda i: (0, 0))],
        out_specs=blk, out_shape=jax.ShapeDtypeStruct((t, w), BF16), compiler_params=_params(("parallel",)),
    )(o, proj, o_norm.reshape(1, hd))


def _dn_gate_bwd(o, proj, z_col, o_norm, dy, *, name):
    t, w = o.shape
    hd = w // DN_HEADS
    tb = _tile(t, 512, 16)

    def body(o_ref, z_ref, g_ref, dy_ref, do_ref, dz_ref, dg_ref):
        gv = g_ref[...]
        dg = jnp.zeros((1, hd), F32)
        for h in range(DN_HEADS):
            sl = slice(h * hd, (h + 1) * hd)
            ov, dyv = o_ref[:, sl], dy_ref[:, sl]
            sz, dsz = _silu_terms(z_ref[:, sl])
            rn = lax.rsqrt(jnp.mean(ov * ov, axis=-1, keepdims=True) + NORM_EPS)
            nv = ov * rn
            dz_ref[:, sl] = (dyv * nv * gv * dsz).astype(BF16)
            dn = dyv * gv * sz
            do_ref[:, sl] = rn * (dn - nv * jnp.mean(dn * nv, axis=-1, keepdims=True))
            dg = dg + jnp.sum(dyv * nv * sz, axis=0, keepdims=True)

        @pl.when(pl.program_id(0) == 0)
        def _():
            dg_ref[...] = dg

        @pl.when(pl.program_id(0) > 0)
        def _():
            dg_ref[...] += dg

    blk = pl.BlockSpec((tb, w), lambda i: (i, 0))
    vec = pl.BlockSpec((1, hd), lambda i: (0, 0))
    do, dz, dg = pl.pallas_call(
        body, name=name, grid=(t // tb,), in_specs=[blk, pl.BlockSpec((tb, w), lambda i: (i, z_col)), vec, blk],
        out_specs=[blk, blk, vec],
        out_shape=[jax.ShapeDtypeStruct((t, w), F32), jax.ShapeDtypeStruct((t, w), BF16),
                   jax.ShapeDtypeStruct((1, hd), F32)],
        compiler_params=_params(("arbitrary",)),
    )(o, proj, o_norm.reshape(1, hd), dy)
    return do, dz, dg.reshape(hd)


def _band_masks(n):
    qi = lax.broadcasted_iota(jnp.int32, (ATTN_BLOCK, ATTN_BLOCK), 0)
    kj = lax.broadcasted_iota(jnp.int32, (ATTN_BLOCK, ATTN_BLOCK), 1)
    return kj <= qi, jnp.logical_and(kj >= qi, n > 0)


def _dattn_fwd(qkv, dil, *, name):
    seq, w = qkv.shape[0], qkv.shape[1] // (3 * dil)
    t = seq * dil
    hd = w // ATTN_HEADS
    assert seq % ATTN_BLOCK == 0
    nb = seq // ATTN_BLOCK
    scale = hd ** -0.5

    def body(q_ref, kp_ref, kc_ref, vp_ref, vc_ref, o_ref, lse_ref):
        mc, mp = _band_masks(pl.program_id(1))
        heads = range(ATTN_HEADS)
        sls = [slice(h * hd, (h + 1) * hd) for h in heads]
        qs = [(q_ref[:, sl] * scale).astype(BF16) for sl in sls]
        scs = [jnp.where(mc, _dot(qs[h], kc_ref[:, sls[h]].astype(BF16), 1, 1), NEG) for h in heads]
        sps = [jnp.where(mp, _dot(qs[h], kp_ref[:, sls[h]].astype(BF16), 1, 1), NEG) for h in heads]
        ms = [jnp.maximum(jnp.max(sc, axis=-1, keepdims=True), jnp.max(sp, axis=-1, keepdims=True))
              for sc, sp in zip(scs, sps)]
        pcs = [jnp.exp(sc - m) for sc, m in zip(scs, ms)]
        pps = [jnp.exp(sp - m) for sp, m in zip(sps, ms)]
        dens = [jnp.sum(pc, axis=-1, keepdims=True) + jnp.sum(pp, axis=-1, keepdims=True) for pc, pp in zip(pcs, pps)]
        outs = [_dot(pcs[h].astype(BF16), vc_ref[:, sls[h]].astype(BF16), 1, 0)
                + _dot(pps[h].astype(BF16), vp_ref[:, sls[h]].astype(BF16), 1, 0) for h in heads]
        for h in heads:
            o_ref[:, sls[h]] = outs[h] / dens[h]
            lse_ref[:, sls[h]] = jnp.broadcast_to(ms[h] + jnp.log(dens[h]), (ATTN_BLOCK, hd))

    cur = pl.BlockSpec((ATTN_BLOCK, w), lambda r, n: (n, r))
    part = lambda j, row: pl.BlockSpec((ATTN_BLOCK, w), lambda r, n: (row(n), 3 * r + j))
    here, before = (lambda n: n), (lambda n: jnp.maximum(n - 1, 0))
    o, lse = pl.pallas_call(
        body, name=name, grid=(dil, nb),
        in_specs=[part(0, here), part(1, before), part(1, here), part(2, before), part(2, here)], out_specs=[cur, cur],
        out_shape=[jax.ShapeDtypeStruct((seq, dil * w), F32)] * 2, compiler_params=_params(("parallel", "parallel")),
    )(qkv, qkv, qkv, qkv, qkv)
    return o.reshape(t, w), lse.reshape(t, w)


def _dattn_bwd(qkv, do, lse, delta, dil, *, name):
    t, w = do.shape
    hd = w // ATTN_HEADS
    seq = t // dil
    nb = seq // ATTN_BLOCK
    scale = hd ** -0.5

    def body(qc_ref, qn_ref, doc_ref, don_ref, lc_ref, ln_ref, dc_ref, dn_ref, kp_ref, kc_ref, vp_ref, vc_ref,
             dq_ref, dk_ref, dv_ref):
        n = pl.program_id(1)
        mc, mp = _band_masks(n)
        _, mx = _band_masks(jnp.where(n + 1 < nb, 1, 0))
        heads = range(ATTN_HEADS)
        sls = [slice(h * hd, (h + 1) * hd) for h in heads]
        ones = [slice(h * hd, h * hd + 1) for h in heads]
        bf = lambda ref, scl=None: [(ref[:, sl] if scl is None else ref[:, sl] * scl).astype(BF16) for sl in sls]
        qc, qn, kc, kp = bf(qc_ref, scale), bf(qn_ref, scale), bf(kc_ref), bf(kp_ref)
        vc, vp, doc, don = bf(vc_ref), bf(vp_ref), bf(doc_ref), bf(don_ref)
        p_c = [jnp.exp(jnp.where(mc, _dot(qc[h], kc[h], 1, 1), NEG) - lc_ref[:, ones[h]]) for h in heads]
        p_p = [jnp.exp(jnp.where(mp, _dot(qc[h], kp[h], 1, 1), NEG) - lc_ref[:, ones[h]]) for h in heads]
        p_x = [jnp.exp(jnp.where(mx, _dot(qn[h], kc[h], 1, 1), NEG) - ln_ref[:, ones[h]]) for h in heads]
        ds_c = [(p_c[h] * (_dot(doc[h], vc[h], 1, 1) - dc_ref[:, ones[h]])).astype(BF16) for h in heads]
        ds_p = [(p_p[h] * (_dot(doc[h], vp[h], 1, 1) - dc_ref[:, ones[h]])).astype(BF16) for h in heads]
        ds_x = [(p_x[h] * (_dot(don[h], vc[h], 1, 1) - dn_ref[:, ones[h]])).astype(BF16) for h in heads]
        for h in heads:
            dq_ref[:, sls[h]] = (_dot(ds_c[h], kc[h], 1, 0) + _dot(ds_p[h], kp[h], 1, 0)) * scale
            dk_ref[:, sls[h]] = _dot(ds_c[h], qc[h], 0, 0) + _dot(ds_x[h], qn[h], 0, 0)
            dv_ref[:, sls[h]] = _dot(p_c[h].astype(BF16), doc[h], 0, 0) + _dot(p_x[h].astype(BF16), don[h], 0, 0)

    cur = pl.BlockSpec((ATTN_BLOCK, w), lambda r, n: (n, r))
    nxt = pl.BlockSpec((ATTN_BLOCK, w), lambda r, n: (jnp.minimum(n + 1, nb - 1), r))
    part = lambda j, row: pl.BlockSpec((ATTN_BLOCK, w), lambda r, n: (row(n), 3 * r + j))
    here, before, after = (lambda n: n), (lambda n: jnp.maximum(n - 1, 0)), (lambda n: jnp.minimum(n + 1, nb - 1))
    do, lse, delta = (z.reshape(seq, dil * w) for z in (do, lse, delta))
    dq, dk, dv = pl.pallas_call(
        body, name=name, grid=(dil, nb),
        in_specs=[part(0, here), part(0, after), cur, nxt, cur, nxt, cur, nxt, part(1, before), part(1, here),
                  part(2, before), part(2, here)], out_specs=[cur, cur, cur],
        out_shape=[jax.ShapeDtypeStruct((seq, dil * w), F32)] * 3, compiler_params=_params(("parallel", "parallel")),
    )(qkv, qkv, do, do, lse, lse, delta, delta, qkv, qkv, qkv, qkv)
    return dq.reshape(t, w), dk.reshape(t, w), dv.reshape(t, w)


def _xattn_fwd(q, k, v, *, name):
    t, w = q.shape
    nm = k.shape[0]
    hd = w // XA_HEADS
    scale = hd ** -0.5
    tq = _tile(t, 512, 8)

    def body(q_ref, k_ref, v_ref, o_ref):
        for h in range(XA_HEADS):
            sl = slice(h * hd, (h + 1) * hd)
            s = _dot((q_ref[:, sl] * scale).astype(BF16), k_ref[:, sl].astype(BF16), 1, 1)
            p = jnp.exp(s - jnp.max(s, axis=-1, keepdims=True))
            p = p / jnp.sum(p, axis=-1, keepdims=True)
            o_ref[:, sl] = _dot(p.astype(BF16), v_ref[:, sl].astype(BF16), 1, 0)

    qs = pl.BlockSpec((tq, w), lambda i: (i, 0))
    ks = pl.BlockSpec((nm, w), lambda i: (0, 0))
    return pl.pallas_call(
        body, name=name, grid=(t // tq,), in_specs=[qs, ks, ks], out_specs=qs,
        out_shape=jax.ShapeDtypeStruct((t, w), F32), compiler_params=_params(("parallel",)),
    )(q, k, v)


def _xattn_bwd(q, k, v, do, *, name):
    t, w = q.shape
    nm = k.shape[0]
    hd = w // XA_HEADS
    scale = hd ** -0.5
    tq = _tile(t, 512, 8)

    def body(q_ref, k_ref, v_ref, do_ref, dq_ref, dk_ref, dv_ref):
        first = pl.program_id(0) == 0
        for h in range(XA_HEADS):
            sl = slice(h * hd, (h + 1) * hd)
            qh = (q_ref[:, sl] * scale).astype(BF16)
            kh, vh, doh = k_ref[:, sl].astype(BF16), v_ref[:, sl].astype(BF16), do_ref[:, sl].astype(BF16)
            s = _dot(qh, kh, 1, 1)
            p = jnp.exp(s - jnp.max(s, axis=-1, keepdims=True))
            p = p / jnp.sum(p, axis=-1, keepdims=True)
            dp = _dot(doh, vh, 1, 1)
            ds = (p * (dp - jnp.sum(p * dp, axis=-1, keepdims=True))).astype(BF16)
            dq_ref[:, sl] = _dot(ds, kh, 1, 0) * scale
            dk = _dot(ds, qh, 0, 0)
            dv = _dot(p.astype(BF16), doh, 0, 0)

            @pl.when(first)
            def _():
                dk_ref[:, sl] = dk
                dv_ref[:, sl] = dv

            @pl.when(jnp.logical_not(first))
            def _():
                dk_ref[:, sl] += dk
                dv_ref[:, sl] += dv

    qs = pl.BlockSpec((tq, w), lambda i: (i, 0))
    ks = pl.BlockSpec((nm, w), lambda i: (0, 0))
    return pl.pallas_call(
        body, name=name, grid=(t // tq,), in_specs=[qs, ks, ks, qs], out_specs=[qs, ks, ks],
        out_shape=[jax.ShapeDtypeStruct((t, w), F32), jax.ShapeDtypeStruct((nm, w), F32),
                   jax.ShapeDtypeStruct((nm, w), F32)],
        compiler_params=_params(("arbitrary",)),
    )(q, k, v, do)


def _dn_head_terms(q_ref, k_ref, gc_ref, gcr_ref, b_ref, h, hd):
    c = DN_CHUNK
    sl = slice(h * hd, (h + 1) * hd)
    qh, kh = q_ref[:, sl], k_ref[:, sl]
    gcc, gcr_h, bh = gc_ref[:, h:h + 1], gcr_ref[0, h:h + 1, :], b_ref[:, h:h + 1]
    row = lax.broadcasted_iota(jnp.int32, (c, c), 0)
    col = lax.broadcasted_iota(jnp.int32, (c, c), 1)
    decay = jnp.exp(jnp.where(row >= col, gcc - gcr_h, NEG))
    kb = kh * bh
    kkt = _dot(kb.astype(BF16), kh.astype(BF16), 1, 1)
    qkt = _dot(qh.astype(BF16), kh.astype(BF16), 1, 1)
    gl = gcc[c - 1:c, :]
    return dict(sl=sl, q=qh, k=kh, gcc=gcc, b=bh, row=row, col=col, decay=decay, kb=kb, kkt=kkt, qkt=qkt,
                e=jnp.exp(gcc), el=jnp.exp(gl), r=jnp.exp(gl - gcc))


def _dn_local_fwd(q, k, v, gc, gcr, beta, *, name):
    t, w = q.shape
    hd = w // DN_HEADS
    c = DN_CHUNK
    nch = t // c
    heads = range(DN_HEADS)

    def body(q_ref, k_ref, v_ref, gc_ref, gcr_ref, b_ref,
             u_ref, wq_ref, wqt_ref, kr_ref, krt_ref, qk_ref, qkt_ref, invt_ref):
        tm = [_dn_head_terms(q_ref, k_ref, gc_ref, gcr_ref, b_ref, h, hd) for h in heads]
        pw = [jnp.where(m['row'] > m['col'], m['kkt'] * m['decay'], 0.0) for m in tm]
        inv = [(m['row'] == m['col']).astype(F32) - p for m, p in zip(tm, pw)]
        for _ in range(int(math.log2(c)) - 1):
            pw = [_dot(p.astype(BF16), p.astype(BF16), 1, 0) for p in pw]
            inv = [i + _dot(i.astype(BF16), p.astype(BF16), 1, 0) for i, p in zip(inv, pw)]
        for h, m in zip(heads, tm):
            rhs = jnp.concatenate([v_ref[:, m['sl']] * m['b'], m['kb'] * m['e']], axis=1).astype(BF16)
            sol = _dot(inv[h].astype(BF16), rhs, 1, 0)
            u_ref[:, m['sl']] = sol[:, :hd]
            wq = jnp.concatenate([sol[:, hd:], m['q'] * m['e']], axis=0)
            kr = m['k'] * m['r']
            qk = m['qkt'] * m['decay']
            wq_ref[0, h], wqt_ref[0, h] = wq.astype(BF16), wq.T.astype(BF16)
            kr_ref[0, h], krt_ref[0, h] = kr.astype(BF16), kr.T.astype(BF16)
            qk_ref[0, h], qkt_ref[0, h] = qk.astype(BF16), qk.T.astype(BF16)
            invt_ref[0, h] = inv[h].T.astype(BF16)

    rows = pl.BlockSpec((c, w), lambda n: (n, 0))
    cols = pl.BlockSpec((c, DN_HEADS), lambda n: (n, 0))
    rowg = pl.BlockSpec((1, DN_HEADS, c), lambda n: (n, 0, 0))
    per = lambda a, b: (pl.BlockSpec((1, DN_HEADS, a, b), lambda n: (n, 0, 0, 0)),
                        jax.ShapeDtypeStruct((nch, DN_HEADS, a, b), BF16))
    outs = [(rows, jax.ShapeDtypeStruct((t, w), F32)), per(2 * c, hd), per(hd, 2 * c), per(c, hd), per(hd, c),
            per(c, c), per(c, c), per(c, c)]
    return pl.pallas_call(
        body, name=name, grid=(nch,), in_specs=[rows, rows, rows, cols, rowg, cols],
        out_specs=[o[0] for o in outs], out_shape=[o[1] for o in outs], compiler_params=_params(("parallel",)),
    )(q, k, v, gc, gcr, beta)


def _dn_state_fwd(u, wq, krt, qk, gc, *, name):
    t, w = u.shape
    hd = w // DN_HEADS
    c = DN_CHUNK
    nch = t // c
    grp = _tile(nch, DN_STEP_CHUNKS, 1)

    def body(u_ref, wq_ref, krt_ref, qk_ref, gc_ref, o_ref, vn_ref, s_ref, state):
        @pl.when(pl.program_id(0) == 0)
        def _():
            state[...] = jnp.zeros_like(state)

        heads = range(DN_HEADS)
        sls = [slice(h * hd, (h + 1) * hd) for h in heads]
        shs = [state[h] for h in heads]
        for j in range(grp):
            rs = slice(j * c, (j + 1) * c)
            wss = [_dot(wq_ref[j, h], shs[h].astype(BF16), 1, 0) for h in heads]
            vns = [(u_ref[rs, sls[h]] - wss[h][:c]).astype(BF16) for h in heads]
            outs = [wss[h][c:] + _dot(qk_ref[j, h], vns[h], 1, 0) for h in heads]
            for h in heads:
                s_ref[j, h] = shs[h]
                vn_ref[rs, sls[h]] = vns[h]
                o_ref[rs, sls[h]] = outs[h]
            shs = [shs[h] * jnp.exp(gc_ref[(j + 1) * c - 1:(j + 1) * c, h:h + 1]) + _dot(krt_ref[j, h], vns[h], 1, 0)
                   for h in heads]
        for h in heads:
            state[h] = shs[h]

    rows = pl.BlockSpec((grp * c, w), lambda n: (n, 0))
    per = lambda a, b: pl.BlockSpec((grp, DN_HEADS, a, b), lambda n: (n, 0, 0, 0))
    return pl.pallas_call(
        body, name=name, grid=(nch // grp,),
        in_specs=[rows, per(2 * c, hd), per(hd, c), per(c, c), pl.BlockSpec((grp * c, DN_HEADS), lambda n: (n, 0))],
        out_specs=[rows, rows, per(hd, hd)],
        out_shape=[jax.ShapeDtypeStruct((t, w), F32), jax.ShapeDtypeStruct((t, w), BF16),
                   jax.ShapeDtypeStruct((nch, DN_HEADS, hd, hd), F32)],
        scratch_shapes=[pltpu.VMEM((DN_HEADS, hd, hd), F32)], compiler_params=_params(("arbitrary",)),
    )(u, wq, krt, qk, gc)


def _dn_state_bwd(do, qkt, kr, vn, states, wqt, gc, *, name):
    t, w = do.shape
    hd = w // DN_HEADS
    c = DN_CHUNK
    nch = t // c
    grp = _tile(nch, DN_STEP_CHUNKS, 1)

    def body(do_ref, qkt_ref, kr_ref, vn_ref, s_ref, wqt_ref, gc_ref, dvn_ref, dkr_ref, del_ref, dstate):
        @pl.when(pl.program_id(0) == 0)
        def _():
            dstate[...] = jnp.zeros_like(dstate)

        heads = range(DN_HEADS)
        sls = [slice(h * hd, (h + 1) * hd) for h in heads]
        dsns = [dstate[h] for h in heads]
        for j in reversed(range(grp)):
            rs = slice(j * c, (j + 1) * c)
            dsbs = [d.astype(BF16) for d in dsns]
            dobs = [do_ref[rs, sl].astype(BF16) for sl in sls]
            dvns = [(_dot(qkt_ref[j, h], dobs[h], 1, 0) + _dot(kr_ref[j, h], dsbs[h], 1, 0)).astype(BF16) for h in heads]
            dkrs = [_dot(vn_ref[rs, sls[h]], dsbs[h], 1, 1) for h in heads]
            for h in heads:
                dvn_ref[rs, sls[h]] = dvns[h]
                dkr_ref[rs, sls[h]] = dkrs[h]
                d_el = jnp.sum(jnp.sum(dsns[h] * s_ref[j, h], axis=1, keepdims=True), axis=0, keepdims=True)
                del_ref[j, h:h + 1, :] = jnp.broadcast_to(d_el, (1, LANES))
            dsns = [dsns[h] * jnp.exp(gc_ref[(j + 1) * c - 1:(j + 1) * c, h:h + 1])
                    + _dot(wqt_ref[j, h], jnp.concatenate([-dvns[h], dobs[h]], axis=0), 1, 0) for h in heads]
        for h in heads:
            dstate[h] = dsns[h]

    rev = lambda n: nch // grp - 1 - n
    rows = pl.BlockSpec((grp * c, w), lambda n: (rev(n), 0))
    per = lambda a, b: pl.BlockSpec((grp, DN_HEADS, a, b), lambda n: (rev(n), 0, 0, 0))
    return pl.pallas_call(
        body, name=name, grid=(nch // grp,),
        in_specs=[rows, per(c, c), per(c, hd), rows, per(hd, hd), per(hd, 2 * c),
                  pl.BlockSpec((grp * c, DN_HEADS), lambda n: (rev(n), 0))],
        out_specs=[rows, rows, pl.BlockSpec((grp, DN_HEADS, LANES), lambda n: (rev(n), 0, 0))],
        out_shape=[jax.ShapeDtypeStruct((t, w), BF16), jax.ShapeDtypeStruct((t, w), F32),
                   jax.ShapeDtypeStruct((nch, DN_HEADS, LANES), F32)],
        scratch_shapes=[pltpu.VMEM((DN_HEADS, hd, hd), F32)], compiler_params=_params(("arbitrary",)),
    )(do, qkt, kr, vn, states, wqt, gc)


def _dn_local_bwd(q, k, v, gc, gcr, beta, invt, u, wq, vn, states, do, dvn, dkr, d_el, *, name):
    t, w = q.shape
    hd = w // DN_HEADS
    c = DN_CHUNK
    nch = t // c

    def body(q_ref, k_ref, v_ref, gc_ref, gcr_ref, b_ref, invt_ref, u_ref, wq_ref, vn_ref, s_ref, do_ref, dvn_ref,
             dkr_ref, del_ref, dq_ref, dk_ref, dv_ref, dgc_ref, dgr_ref, db_ref):
        heads = range(DN_HEADS)
        tms = [_dn_head_terms(q_ref, k_ref, gc_ref, gcr_ref, b_ref, h, hd) for h in heads]
        dobs = [do_ref[:, m['sl']].astype(BF16) for m in tms]
        tss = [_dot(jnp.concatenate([dob, -dvn_ref[:, m['sl']]], axis=0), s_ref[0, h].astype(BF16), 1, 1)
               for h, m, dob in zip(heads, tms, dobs)]
        d_qks = [_dot(dob, vn_ref[:, m['sl']], 1, 1) for m, dob in zip(tms, dobs)]
        d_rhss = [_dot(invt_ref[0, h], jnp.concatenate([dvn_ref[:, m['sl']], ts[c:].astype(BF16)], axis=1), 1, 0)
                  for h, m, ts in zip(heads, tms, tss)]
        d_as = [-_dot(d_rhs.astype(BF16),
                      jnp.concatenate([u_ref[:, m['sl']].astype(BF16), wq_ref[0, h, :c, :]], axis=1), 1, 1)
                for h, m, d_rhs in zip(heads, tms, d_rhss)]
        for h in heads:
            m, d_qe, d_qk, d_rhs, d_a = tms[h], tss[h][:c], d_qks[h], d_rhss[h], d_as[h]
            sl, qh, kh, bh, e, r, decay = m['sl'], m['q'], m['k'], m['b'], m['e'], m['r'], m['decay']
            vh = v_ref[:, sl]
            d_ru, d_rw = d_rhs[:, :hd], d_rhs[:, hd:]
            dv_ref[:, sl] = d_ru * bh
            d_e = jnp.sum(d_rw * m['kb'], axis=1, keepdims=True) + jnp.sum(d_qe * qh, axis=1, keepdims=True)
            d_n = jnp.where(m['row'] > m['col'], d_a, 0.0)
            d_m, d_p = (d_n * decay).astype(BF16), (d_qk * decay).astype(BF16)
            tk = _dot(jnp.concatenate([d_m, d_p], axis=0), kh.astype(BF16), 1, 0)
            d_kb = d_rw * e + tk[:c]
            dq_ref[:, sl] = tk[c:] + d_qe * e
            dk = _dot(d_m, m['kb'].astype(BF16), 0, 0) + _dot(d_p, qh.astype(BF16), 0, 0)
            dd = (d_n * m['kkt'] + d_qk * m['qkt']) * decay
            d_kr = dkr_ref[:, sl]
            d_r = jnp.sum(d_kr * kh, axis=1, keepdims=True)
            d_gl = del_ref[0, h:h + 1, 0:1] * m['el'] + jnp.sum(d_r * r, axis=0, keepdims=True)
            last = lax.broadcasted_iota(jnp.int32, (c, 1), 0) == c - 1
            dgc_ref[:, h:h + 1] = jnp.sum(dd, axis=1, keepdims=True) + d_e * e - d_r * r + jnp.where(last, d_gl, 0.0)
            dgr_ref[0, h:h + 1, :] = -jnp.sum(dd, axis=0, keepdims=True)
            dk_ref[:, sl] = dk + d_kr * r + d_kb * bh
            db_ref[:, h:h + 1] = jnp.sum(d_ru * vh, axis=1, keepdims=True) + jnp.sum(d_kb * kh, axis=1, keepdims=True)

    rows = pl.BlockSpec((c, w), lambda n: (n, 0))
    cols = pl.BlockSpec((c, DN_HEADS), lambda n: (n, 0))
    rowg = pl.BlockSpec((1, DN_HEADS, c), lambda n: (n, 0, 0))
    per = lambda a, b: pl.BlockSpec((1, DN_HEADS, a, b), lambda n: (n, 0, 0, 0))
    return pl.pallas_call(
        body, name=name, grid=(nch,),
        in_specs=[rows, rows, rows, cols, rowg, cols, per(c, c), rows, per(2 * c, hd), rows, per(hd, hd), rows, rows,
                  rows, pl.BlockSpec((1, DN_HEADS, LANES), lambda n: (n, 0, 0))],
        out_specs=[rows, rows, rows, cols, rowg, cols],
        out_shape=[jax.ShapeDtypeStruct((t, w), F32)] * 3
        + [jax.ShapeDtypeStruct((t, DN_HEADS), F32), jax.ShapeDtypeStruct((nch, DN_HEADS, c), F32),
           jax.ShapeDtypeStruct((t, DN_HEADS), F32)],
        compiler_params=_params(("parallel",)),
    )(q, k, v, gc, gcr, beta, invt, u, wq, vn, states, do, dvn, dkr, d_el)


def _other_chips():
    x, y = lax.axis_index("x"), lax.axis_index("y")
    return [(1 - x, y), (x, 1 - y), (1 - x, 1 - y)]


def _gather(src, *, name):
    def body(src_ref, out_ref, send_sems, recv_sems, local_sem):
        x, y, c = lax.axis_index("x"), lax.axis_index("y"), lax.axis_index("c")
        me, sibling, chips = (x, y, c), (x, y, 1 - c), _other_chips()
        slot = lambda px, py, pc: out_ref.at[4 * px + 2 * py + pc]

        def copy(k, block, to, own=False):
            return pltpu.make_async_remote_copy(
                src_ref=src_ref if own else slot(*block), dst_ref=slot(*block), send_sem=send_sems.at[k],
                recv_sem=recv_sems.at[k], device_id=to, device_id_type=MESH)

        local = pltpu.make_async_copy(src_ref, slot(*me), local_sem)
        local.start()
        first = [copy(0, me, sibling, own=True)] + [copy(1 + j, me, (*chip, c), own=True) for j, chip in enumerate(chips)]
        for cp in first:
            cp.start()
        passed = [copy(4 + j, (*chip, c), sibling) for j, chip in enumerate(chips)]
        for j, chip in enumerate(chips):
            copy(1 + j, (*chip, c), me).wait_recv()
            passed[j].start()
        copy(0, sibling, me).wait_recv()
        for j, chip in enumerate(chips):
            copy(4 + j, (*chip, 1 - c), me).wait_recv()
        for cp in first + passed:
            cp.wait_send()
        local.wait()

    return pl.pallas_call(
        body, name=name, out_shape=jax.ShapeDtypeStruct((N_DEV,) + src.shape, src.dtype),
        in_specs=[pl.BlockSpec(memory_space=pl.ANY)], out_specs=pl.BlockSpec(memory_space=pl.ANY),
        scratch_shapes=[pltpu.SemaphoreType.DMA((N_DEV - 1,)), pltpu.SemaphoreType.DMA((N_DEV - 1,)),
                        pltpu.SemaphoreType.DMA(())],
    )(src)


def _swap_with_sibling(src, *, name):
    def body(src_ref, out_ref, send_sem, recv_sem):
        x, y, c = lax.axis_index("x"), lax.axis_index("y"), lax.axis_index("c")
        cp = pltpu.make_async_remote_copy(src_ref=src_ref.at[1 - c], dst_ref=out_ref, send_sem=send_sem,
                                          recv_sem=recv_sem, device_id=(x, y, 1 - c), device_id_type=MESH)
        cp.start()
        cp.wait()

    return pl.pallas_call(
        body, name=name, out_shape=jax.ShapeDtypeStruct(src.shape[1:], src.dtype),
        in_specs=[pl.BlockSpec(memory_space=pl.ANY)], out_specs=pl.BlockSpec(memory_space=pl.ANY),
        scratch_shapes=[pltpu.SemaphoreType.DMA(()), pltpu.SemaphoreType.DMA(())],
    )(src)


def _add_own_half(src, got, *, name):
    _, n, r, c = src.shape
    tr = _tile(r, 512, 16)

    def body(s0_ref, s1_ref, got_ref, o_ref):
        own = jnp.where(lax.axis_index("c") == 0, s0_ref[0, 0], s1_ref[0, 0])
        o_ref[0] = (own + got_ref[0]).astype(BF16)

    half = lambda h: pl.BlockSpec((1, 1, tr, c), lambda j, i: (h, j, i, 0))
    blk = pl.BlockSpec((1, tr, c), lambda j, i: (j, i, 0))
    return pl.pallas_call(
        body, name=name, grid=(n, r // tr), in_specs=[half(0), half(1), blk], out_specs=blk,
        out_shape=jax.ShapeDtypeStruct((n, r, c), BF16), compiler_params=_params(("parallel", "parallel")),
    )(src, src, got)


def _swap_between_chips(src, *, name):
    def body(src_ref, out_ref, send_sems, recv_sems, local_sem):
        x, y, c = lax.axis_index("x"), lax.axis_index("y"), lax.axis_index("c")
        mine = 2 * x + y
        local = pltpu.make_async_copy(src_ref.at[mine], out_ref.at[mine], local_sem)
        local.start()
        copies = [pltpu.make_async_remote_copy(
            src_ref=src_ref.at[2 * px + py], dst_ref=out_ref.at[mine], send_sem=send_sems.at[j],
            recv_sem=recv_sems.at[j], device_id=(px, py, c), device_id_type=MESH) for j, (px, py) in enumerate(_other_chips())]
        for cp in copies:
            cp.start()
        for j, (px, py) in enumerate(_other_chips()):
            pltpu.make_async_remote_copy(
                src_ref=src_ref.at[mine], dst_ref=out_ref.at[2 * px + py], send_sem=send_sems.at[j],
                recv_sem=recv_sems.at[j], device_id=(px, py, c), device_id_type=MESH).wait_recv()
        for cp in copies:
            cp.wait_send()
        local.wait()

    return pl.pallas_call(
        body, name=name, out_shape=jax.ShapeDtypeStruct(src.shape, src.dtype),
        in_specs=[pl.BlockSpec(memory_space=pl.ANY)], out_specs=pl.BlockSpec(memory_space=pl.ANY),
        scratch_shapes=[pltpu.SemaphoreType.DMA((3,)), pltpu.SemaphoreType.DMA((3,)), pltpu.SemaphoreType.DMA(())],
    )(src)


def _sum_adamw(parts, w, m, v, *, name):
    n_parts, r, c = parts.shape
    tr = _tile(r, 256, 8)
    c1, c2 = 1.0 - ADAM_B1 ** ADAM_STEP, 1.0 - ADAM_B2 ** ADAM_STEP

    def body(p_ref, w_ref, m_ref, v_ref, g_ref, d_ref, nm_ref, nv_ref):
        g = p_ref[0].astype(F32)
        for s in range(1, n_parts):
            g = g + p_ref[s].astype(F32)
        nm = ADAM_B1 * m_ref[...] + (1.0 - ADAM_B1) * g
        nv = ADAM_B2 * v_ref[...] + (1.0 - ADAM_B2) * (g * g)
        g_ref[...] = g
        nm_ref[...] = nm
        nv_ref[...] = nv
        d_ref[...] = -ADAM_LR * ((nm / c1) / (jnp.sqrt(nv / c2) + ADAM_EPS) + ADAM_WD * w_ref[...])

    blk = pl.BlockSpec((tr, c), lambda i: (i, 0))
    return pl.pallas_call(
        body, name=name, grid=(r // tr,), in_specs=[pl.BlockSpec((n_parts, tr, c), lambda i: (0, i, 0)), blk, blk, blk],
        out_specs=[blk] * 4, out_shape=[jax.ShapeDtypeStruct((r, c), F32)] * 4, compiler_params=_params(("parallel",)),
    )(parts, w, m, v)


def _pack_rows(n):
    return -(-n // (PACK_COLS * PACK_ROWS)) * PACK_ROWS


def _pack(blocks):
    parts, spans, at = [], [], 0
    for blk, n_lead in blocks:
        lead = blk.shape[:n_lead]
        n = math.prod(blk.shape[n_lead:])
        rows = _pack_rows(n)
        flat = blk.reshape(lead + (n,))
        flat = jnp.pad(flat, [(0, 0)] * n_lead + [(0, rows * PACK_COLS - n)])
        parts.append(flat.reshape(lead + (rows, PACK_COLS)))
        spans.append((at, rows))
        at += rows
    return jnp.concatenate(parts, axis=-2), spans


def _unpack(buf, span, shape):
    at, rows = span
    lead = buf.shape[:-2]
    flat = lax.slice_in_dim(buf, at, at + rows, axis=buf.ndim - 2).reshape(lead + (rows * PACK_COLS,))
    return lax.slice_in_dim(flat, 0, math.prod(shape), axis=len(lead)).reshape(lead + tuple(shape))


def _join_shards(g, axis):
    g = jnp.moveaxis(g, 0, axis)
    return g.reshape(g.shape[:axis] + (g.shape[axis] * g.shape[axis + 1],) + g.shape[axis + 2:])


def _split_shards(full, axis):
    s = full.shape
    g = full.reshape(s[:axis] + (N_DEV, s[axis] // N_DEV) + s[axis + 1:])
    return jnp.moveaxis(g, axis, 0)


def _residual_out(a, w, x, scale, next_gain, name, **tiles):
    out = _mm(a, w, name=name, res=x, scale=scale, norm=next_gain, **tiles)
    return out if next_gain is not None else (out, None)


def _ffn_fwd(x, h, w_in, w_out, tag, next_gain):
    gate, up, a = _mm_swiglu(h, w_in, name=tag + "_in", tm=512, tn=1408)
    xo, hn = _residual_out(a, w_out, x, 0.5, next_gain, tag + "_out", tk=2816)
    return xo, hn, (x, h, gate, up, a)


def _ffn_bwd(saved, g, w_in, w_out, dxo, tag):
    x, h, gate, up, a = saved
    f = w_out.shape[0]
    d_w_out = _mm(a, dxo, name=tag + "_dwout", ta=True, scale=0.5, tm=1408, tk=2048)
    dgate, dup = _mm_dswiglu(dxo, w_out, gate, up, name=tag + "_da", scale=0.5, tm=512, tn=1408)
    d_w_in = jnp.concatenate([_mm(dgate, h, name=tag + "_dwin_g", ta=True, tm=1408, tk=2048),
                              _mm(dup, h, name=tag + "_dwin_u", ta=True, tm=1408, tk=2048)], axis=0)
    dh = _mm(dgate, w_in[:f], name=tag + "_dh_g", tk=2816)
    dx, dg = _mm(dup, w_in[f:], name=tag + "_dh_u", tm=512, tk=2816, res=dh, rms_bwd=(x, g, dxo))
    return dx, dg, d_w_in, d_w_out


def _block_diag(w):
    n, j, k = w.shape
    return (w[:, :, None, :] * jnp.eye(n, dtype=w.dtype)[:, None, :, None]).reshape(n * j, n * k)


def _diag_blocks(dense, n):
    j, k = dense.shape[0] // n, dense.shape[1] // n
    return jnp.stack([dense[i * j:(i + 1) * j, i * k:(i + 1) * k] for i in range(n)], axis=0)


def _attn_lru_fwd(x, h, p, tag, next_gain):
    aw = ATTN_HEADS * 64
    proj = _mm(h, p['ab_w_in'], name=tag + "_in", tb=True, tn=1280)
    qkv = proj[:, :3 * aw].astype(BF16)
    views = {dil: qkv.reshape(qkv.shape[0] // dil, dil * 3 * aw) for _, dil in DILATED_PATTERNS}
    outs, lses = [], []
    for window, dil in DILATED_PATTERNS:
        assert window // dil == ATTN_BLOCK
        o, l = _dattn_fwd(views[dil], dil, name=f"{tag}_attn{dil}")
        outs.append(o)
        lses.append(l)
    wa, wx = _block_diag(p['lru_w_a']).astype(BF16), _block_diag(p['lru_w_x']).astype(BF16)
    sp, sp_vjp = jax.vjp(lambda lam: LRU_C * jax.nn.softplus(-lam), p['lru_lambda'])
    a, b, xc = _lru_gates_fwd(proj, 3, p['lru_conv_w'], p['lru_conv_b'], wa, p['lru_b_a'], wx, p['lru_b_x'], sp,
                              name=tag + "_gates")
    hs = _scan(a, b, name=tag + "_scan")
    cat, attn, lse_all = _mix_join_fwd(outs, lses, hs, proj, 4, name=tag + "_join")
    xo, hn = _residual_out(cat, p['ab_w_out'], x, 1.0, next_gain, tag + "_out")
    return xo, hn, (x, h, proj, views, attn, lse_all, a, hs, xc, wa, wx, sp, sp_vjp, cat)


def _attn_lru_bwd(saved, p, dxo, tag):
    x, h, proj, views, attn, lse_all, a, hs, xc, wa, wx, sp, sp_vjp, cat = saved
    aw = attn.shape[1]
    g = {'ab_w_out': _mm(cat, dxo, name=tag + "_dwout", ta=True)}
    dcat = _mm(dxo, p['ab_w_out'], name=tag + "_dcat", tb=True)
    delta, dhs, dgr = _mix_join_bwd(dcat, attn, hs, proj, 4, name=tag + "_djoin")
    a_next = jnp.concatenate([a[1:], jnp.zeros_like(a[:1])], axis=0)
    dtot = _scan(a_next, dhs, name=tag + "_dscan", reverse=True)
    h_prev = jnp.concatenate([jnp.zeros_like(hs[:1]), hs[:-1]], axis=0)
    dxc, dwa, dwx, vecs = _lru_gates_bwd(xc, dtot, h_prev, wa, p['lru_b_a'], wx, p['lru_b_x'], sp, name=tag + "_dgates")
    dxr, g['lru_conv_w'] = _conv_bwd(dxc, proj, 3, p['lru_conv_w'], name=tag + "_dconv")
    g['lru_b_a'], g['lru_b_x'], g['lru_conv_b'] = vecs[0], vecs[1], vecs[3]
    g['lru_lambda'], = sp_vjp(vecs[2])
    g['lru_w_a'], g['lru_w_x'] = _diag_blocks(dwa, LRU_BLOCKS), _diag_blocks(dwx, LRU_BLOCKS)
    dattn = dcat[:, :aw]
    dq = dk = dv = 0.0
    for _, dil in DILATED_PATTERNS:
        dq_, dk_, dv_ = _dattn_bwd(views[dil], dattn, lse_all, delta, dil, name=f"{tag}_dattn{dil}")
        dq, dk, dv = dq + dq_, dk + dk_, dv + dv_
    dproj = jnp.concatenate([dq.astype(BF16), dk.astype(BF16), dv.astype(BF16), dxr, dgr], axis=-1)
    g['ab_w_in'] = _mm(dproj, h, name=tag + "_dwin", ta=True, tm=1280)
    dx, g['mix_norm'] = _mm(dproj, p['ab_w_in'], name=tag + "_dh", tm=512, tk=1280, rms_bwd=(x, p['mix_norm'], dxo))
    return dx, g


def _dn_decay(a, b, a_log, dt_bias):
    t = a.shape[0]
    g = -jnp.exp(a_log) * jax.nn.softplus(a + dt_bias)
    gc = jnp.cumsum(g.reshape(t // DN_CHUNK, DN_CHUNK, DN_HEADS), axis=1)
    return gc.reshape(t, DN_HEADS), jnp.swapaxes(gc, 1, 2), jax.nn.sigmoid(b)


def _dn_in_width(w):
    return -(-(4 * w + 2 * DN_HEADS) // LANES) * LANES


def _deltanet_fwd(x, h, p, tag, next_gain):
    w = p['dn_w_out'].shape[0]
    proj = _mm(h, p['dn_w_in'], name=tag + "_in", tb=True, tn=1408)
    q, k, v, c = _dn_prep_fwd(proj, p['dn_conv_w'], name=tag + "_prep")
    a, b = proj[:, 4 * w:4 * w + DN_HEADS], proj[:, 4 * w + DN_HEADS:4 * w + 2 * DN_HEADS]
    (gc, gcr, beta), decay_vjp = jax.vjp(_dn_decay, a, b, p['dn_a_log'], p['dn_dt_bias'])
    prep = (q, k, v, gc, gcr, beta)
    u, wq, wqt, kr, krt, qk, qkt, invt = _dn_local_fwd(*prep, name=tag + "_local")
    o, vn, states = _dn_state_fwd(u, wq, krt, qk, gc, name=tag + "_state")
    og = _dn_gate_fwd(o, proj, 3, p['dn_o_norm'], name=tag + "_gate")
    xo, hn = _residual_out(og, p['dn_w_out'], x, 1.0, next_gain, tag + "_out")
    return xo, hn, (x, h, proj, c, o, prep, (u, wq, wqt, kr, qkt, invt, vn, states), decay_vjp, og)


def _deltanet_bwd(saved, p, dxo, tag):
    x, h, proj, c, o, prep, (u, wq, wqt, kr, qkt, invt, vn, states), decay_vjp, og = saved
    g = {'dn_w_out': _mm(og, dxo, name=tag + "_dwout", ta=True)}
    dog = _mm(dxo, p['dn_w_out'], name=tag + "_dog", tb=True)
    do, dz, g['dn_o_norm'] = _dn_gate_bwd(o, proj, 3, p['dn_o_norm'], dog, name=tag + "_dgate")
    dvn, dkr, d_el = _dn_state_bwd(do, qkt, kr, vn, states, wqt, prep[3], name=tag + "_dstate")
    dq, dk, dv, dgc, dgr, dbeta = _dn_local_bwd(*prep, invt, u, wq, vn, states, do, dvn, dkr, d_el, name=tag + "_dlocal")
    dc = _dn_prep_bwd(c, dq, dk, dv, name=tag + "_dprep")
    dqkv, g['dn_conv_w'] = _conv_bwd(dc, proj, 0, p['dn_conv_w'], name=tag + "_dconv")
    da, db, g['dn_a_log'], g['dn_dt_bias'] = decay_vjp((dgc, dgr, dbeta))
    t = x.shape[0]
    pad = jnp.zeros((t, p['dn_w_in'].shape[0] - dqkv.shape[1] - dz.shape[1] - 2 * DN_HEADS), BF16)
    groups = [("qkv", dqkv), ("z", dz), ("ab", jnp.concatenate([da.astype(BF16), db.astype(BF16), pad], axis=-1))]
    parts, dh, lo = [], None, 0
    for label, piece in groups:
        hi = lo + piece.shape[1]
        parts.append(_mm(piece, h, name=f"{tag}_dwin_{label}", ta=True, tm=1536, tk=2048))
        last = dict(tm=512, rms_bwd=(x, p['mix_norm'], dxo)) if label == groups[-1][0] else {}
        dh = _mm(piece, p['dn_w_in'][lo:hi], name=f"{tag}_dh_{label}", tk=1536, res=dh, **last)
        lo = hi
    g['dn_w_in'] = jnp.concatenate(parts, axis=0)
    dx, g['mix_norm'] = dh
    return dx, g


def _xattn_block_fwd(x, h, mem, p, tag, next_gain):
    w = x.shape[1]
    mh = _rms_fwd(mem, p['xa_mem_norm'], name=tag + "_mnorm", out_dtype=BF16)
    q = _mm(h, p['xa_wq'], name=tag + "_q")
    kv = _mm(mh, p['xa_wkv'], name=tag + "_kv", tb=True)
    k, v = kv[:, :w], kv[:, w:]
    o = _xattn_fwd(q, k, v, name=tag + "_attn").astype(BF16)
    xo, hn = _residual_out(o, p['xa_wo'], x, 1.0, next_gain, tag + "_out")
    return xo, hn, (x, h, mh, q, k, v, o)


def _xattn_block_bwd(saved, mem, p, dxo, tag):
    x, h, mh, q, k, v, o = saved
    g = {'xa_wo': _mm(o, dxo, name=tag + "_dwo", ta=True)}
    do = _mm(dxo, p['xa_wo'], name=tag + "_do", tb=True)
    dq, dk, dv = _xattn_bwd(q, k, v, do, name=tag + "_dattn")
    dq = dq.astype(BF16)
    dkv = jnp.concatenate([dk, dv], axis=-1).astype(BF16)
    g['xa_wq'] = _mm(h, dq, name=tag + "_dwq", ta=True)
    g['xa_wkv'] = _mm(dkv, mh, name=tag + "_dwkv", ta=True)
    dmh = _mm(dkv, p['xa_wkv'], name=tag + "_dmh")
    _, g['xa_mem_norm'] = _rms_bwd(mem, p['xa_mem_norm'], dmh, None, name=tag + "_dmnorm")
    dx, g['xa_norm'] = _mm(dq, p['xa_wq'], name=tag + "_dh", tb=True, tm=512, rms_bwd=(x, p['xa_norm'], dxo))
    return dx, g


def _layer_params(full, layer):
    p = {n: full[n][layer] for n in ('ffn1_norm', 'ffn1_w_in', 'ffn1_w_out', 'mix_norm', 'xa_norm', 'xa_mem_norm',
                                     'xa_wq', 'xa_wkv', 'xa_wo', 'ffn2_norm', 'ffn2_w_in', 'ffn2_w_out')}
    mixer = ('ab_w_in', 'lru_conv_w', 'lru_conv_b', 'lru_w_a', 'lru_b_a', 'lru_w_x', 'lru_b_x', 'lru_lambda', 'ab_w_out') \
        if layer % 2 == 0 else ('dn_w_in', 'dn_conv_w', 'dn_a_log', 'dn_dt_bias', 'dn_o_norm', 'dn_w_out')
    p.update({n: full[n][layer // 2] for n in mixer})
    return p


def _step(x, mem, target, full, depth):
    saved = []
    params = [_layer_params(full, layer) for layer in range(depth)]
    h = _rms_fwd(x, params[0]['ffn1_norm'], name="l0_ffn1_norm", out_dtype=BF16)
    for layer, p in enumerate(params):
        tag = f"l{layer}"
        after = params[layer + 1]['ffn1_norm'] if layer + 1 < depth else None
        x, h, s1 = _ffn_fwd(x, h, p['ffn1_w_in'], p['ffn1_w_out'], tag + "_ffn1", p['mix_norm'])
        x, h, s2 = (_attn_lru_fwd if layer % 2 == 0 else _deltanet_fwd)(x, h, p, tag + "_mix", p['xa_norm'])
        x, h, s3 = _xattn_block_fwd(x, h, mem, p, tag + "_xa", p['ffn2_norm'])
        x, h, s4 = _ffn_fwd(x, h, p['ffn2_w_in'], p['ffn2_w_out'], tag + "_ffn2", after)
        saved.append((p, s1, s2, s3, s4))
    sq, dx, d_final = _final_loss(x, full['final_norm'], target, name="final_loss")
    per_layer = []
    for layer in reversed(range(depth)):
        p, s1, s2, s3, s4 = saved[layer]
        tag = f"l{layer}"
        g = {}
        dx, g['ffn2_norm'], g['ffn2_w_in'], g['ffn2_w_out'] = _ffn_bwd(s4, p['ffn2_norm'], p['ffn2_w_in'], p['ffn2_w_out'], dx, tag + "_ffn2")
        dx, gx = _xattn_block_bwd(s3, mem, p, dx, tag + "_xa")
        dx, gm = (_attn_lru_bwd if layer % 2 == 0 else _deltanet_bwd)(s2, p, dx, tag + "_mix")
        dx, g['ffn1_norm'], g['ffn1_w_in'], g['ffn1_w_out'] = _ffn_bwd(s1, p['ffn1_norm'], p['ffn1_w_in'], p['ffn1_w_out'], dx, tag + "_ffn1")
        g.update(gx)
        g.update(gm)
        per_layer.insert(0, g)
    grads = {n: [g[n] for g in per_layer if n in g] for n in WEIGHTS[:-1]}
    grads['final_norm'] = d_final
    return sq, dx, grads


def kernel(x, mem, ffn1_norm, ffn1_w_in, ffn1_w_out, mix_norm, xa_norm, xa_mem_norm, xa_wq, xa_wkv, xa_wo, ffn2_norm, ffn2_w_in, ffn2_w_out, ab_w_in, lru_conv_w, lru_conv_b, lru_w_a, lru_b_a, lru_w_x, lru_b_x, lru_lambda, ab_w_out, dn_w_in, dn_conv_w, dn_a_log, dn_dt_bias, dn_o_norm, dn_w_out, final_norm, loss_target, m_ffn1_norm, m_ffn1_w_in, m_ffn1_w_out, m_mix_norm, m_xa_norm, m_xa_mem_norm, m_xa_wq, m_xa_wkv, m_xa_wo, m_ffn2_norm, m_ffn2_w_in, m_ffn2_w_out, m_ab_w_in, m_lru_conv_w, m_lru_conv_b, m_lru_w_a, m_lru_b_a, m_lru_w_x, m_lru_b_x, m_lru_lambda, m_ab_w_out, m_dn_w_in, m_dn_conv_w, m_dn_a_log, m_dn_dt_bias, m_dn_o_norm, m_dn_w_out, m_final_norm, v_ffn1_norm, v_ffn1_w_in, v_ffn1_w_out, v_mix_norm, v_xa_norm, v_xa_mem_norm, v_xa_wq, v_xa_wkv, v_xa_wo, v_ffn2_norm, v_ffn2_w_in, v_ffn2_w_out, v_ab_w_in, v_lru_conv_w, v_lru_conv_b, v_lru_w_a, v_lru_b_a, v_lru_w_x, v_lru_b_x, v_lru_lambda, v_ab_w_out, v_dn_w_in, v_dn_conv_w, v_dn_a_log, v_dn_dt_bias, v_dn_o_norm, v_dn_w_out, v_final_norm):
    args = dict(locals())
    flip = lambda n, a: jnp.swapaxes(a, 1, 2) if n in TRANSPOSED else a
    local = {n: flip(n, args[n]) for n in WEIGHTS}
    depth = ffn1_norm.shape[0]
    matrices = [n for n in WEIGHTS if n in SHARD_AXIS and n not in GATHER_F32]

    def entries(get):
        return [(a, n, l) for n in WEIGHTS for l, a in
                (enumerate(get(n)) if n in matrices else [(None, get(n))])]

    mats = [e for e in entries(lambda n: local[n]) if e[1] in matrices]
    send16, spans16 = _pack([(a.astype(BF16), 0) for a, _, _ in mats])
    send32, spans32 = _pack([(local[n], 0) for n in GATHER_F32])
    got16 = _gather(send16, name="gather_matrices")
    got32 = _gather(send32, name="gather_filters")
    full = {n: ([None] * local[n].shape[0] if n in matrices else local[n]) for n in WEIGHTS}
    for (a, n, l), span in zip(mats, spans16):
        full[n][l] = _unpack(got16, span, a.shape).reshape(N_DEV * a.shape[0], a.shape[1])
    for n, span in zip(GATHER_F32, spans32):
        full[n] = _join_shards(_unpack(got32, span, local[n].shape), SHARD_AXIS[n])
    dn_rows = full['dn_w_in'][0].shape[0]
    full['dn_w_in'] = [jnp.pad(w, ((0, _dn_in_width(w.shape[1]) - dn_rows), (0, 0))) for w in full['dn_w_in']]

    sq, dx, grads = _step(x[0], mem[0], loss_target[0], full, depth)
    grads['dn_w_in'] = [g[:dn_rows] for g in grads['dn_w_in']]
    loss = lax.psum(0.5 * jnp.sum(sq) / x.shape[2], ("x", "y", "c"))

    by_core = lambda z: jnp.swapaxes(z.reshape((N_DEV // 2, 2) + z.shape[1:]), 0, 1)

    def contribution(n):
        if n in matrices:
            return [by_core(g.reshape((N_DEV, g.shape[0] // N_DEV, g.shape[1]))) for g in grads[n]]
        g = grads[n] if n == 'final_norm' else jnp.stack(grads[n], axis=0)
        return by_core(_split_shards(g, SHARD_AXIS[n]) if n in SHARD_AXIS else jnp.broadcast_to(g, (N_DEV,) + g.shape))

    send, spans = _pack([(a, 2) for a, _, _ in entries(contribution)])
    got = _swap_with_sibling(send, name="grads_to_sibling")
    chip_sum = _add_own_half(send, got, name="grads_chip_sum")
    parts = _swap_between_chips(chip_sum, name="grads_between_chips")
    state = [_pack([(a, 0) for a, _, _ in entries(lambda n: flip(n, args[pre + n]))])[0] for pre in ("", "m_", "v_")]
    outs = _sum_adamw(parts, *state, name="sum_adamw")
    result = []
    for o in outs:
        got_rows = {}
        for (a, n, l), span in zip(entries(lambda n: local[n]), spans):
            got_rows.setdefault(n, []).append(_unpack(o, span, a.shape))
        result += [flip(n, jnp.stack(got_rows[n], axis=0) if n in matrices else got_rows[n][0]) for n in WEIGHTS]
    return (loss, dx[None], *result)
```

```python
import math

import jax
import jax.numpy as jnp
from jax import lax
from jax.experimental import pallas as pl
from jax.experimental.pallas import tpu as pltpu

F32, BF16 = jnp.float32, jnp.bfloat16
MESH = pl.DeviceIdType.MESH
N_DEV = 8
V7X_VMEM_LIMIT = 56 << 20
LANES = 128
PACK_COLS = 1024
PACK_ROWS = 16
NEG = -1e30

NORM_EPS = 1e-6
CONV_K = 4
ATTN_HEADS = 8
DILATED_PATTERNS = ((128, 1), (512, 4), (2048, 16))
ATTN_BLOCK = 128
LRU_BLOCKS = 8
LRU_C = 8.0
DN_HEADS = 8
DN_CHUNK = 64
DN_STEP_CHUNKS = 4
XA_HEADS = 4
ADAM_LR, ADAM_B1, ADAM_B2, ADAM_EPS, ADAM_WD, ADAM_STEP = 0.001, 0.9, 0.999, 1e-08, 0.01, 10

WEIGHTS = ['ffn1_norm', 'ffn1_w_in', 'ffn1_w_out', 'mix_norm', 'xa_norm', 'xa_mem_norm', 'xa_wq', 'xa_wkv', 'xa_wo',
           'ffn2_norm', 'ffn2_w_in', 'ffn2_w_out', 'ab_w_in', 'lru_conv_w', 'lru_conv_b', 'lru_w_a', 'lru_b_a',
           'lru_w_x', 'lru_b_x', 'lru_lambda', 'ab_w_out', 'dn_w_in', 'dn_conv_w', 'dn_a_log', 'dn_dt_bias',
           'dn_o_norm', 'dn_w_out', 'final_norm']
SHARD_AXIS = {'ffn1_w_in': 2, 'ffn1_w_out': 1, 'xa_wq': 1, 'xa_wkv': 2, 'xa_wo': 1, 'ffn2_w_in': 2, 'ffn2_w_out': 1,
              'ab_w_in': 2, 'lru_conv_w': 2, 'ab_w_out': 1, 'dn_w_in': 2, 'dn_conv_w': 2, 'dn_w_out': 1}
GATHER_F32 = ('lru_conv_w', 'dn_conv_w')
TRANSPOSED = ('ffn1_w_in', 'xa_wkv', 'ffn2_w_in', 'ab_w_in', 'dn_w_in')


def _params(sem=None):
    return pltpu.CompilerParams(dimension_semantics=sem, vmem_limit_bytes=V7X_VMEM_LIMIT)


def _tile(n, pref, mult):
    best = None
    t = mult
    while t <= min(n, pref):
        if n % t == 0:
            best = t
        t += mult
    return n if best is None else best


def _dot(a, b, ca, cb, prec=None):
    return lax.dot_general(a, b, (((ca,), (cb,)), ((), ())), preferred_element_type=F32, precision=prec)


def _mm(a, b, *, name, ta=False, tb=False, out_dtype=F32, res=None, scale=1.0, tm=1024, tn=1024, tk=1024,
        norm=None, rms_bwd=None):
    m, kdim = (a.shape[1], a.shape[0]) if ta else a.shape
    n = b.shape[0] if tb else b.shape[1]
    assert (b.shape[1] if tb else b.shape[0]) == kdim
    tm = _tile(m, tm, LANES if ta else 16)
    tn = _tile(n, tn, LANES)
    tk = _tile(kdim, tk, LANES)
    nk = kdim // tk
    a_spec = pl.BlockSpec((tk, tm), lambda i, j, k: (k, i)) if ta else pl.BlockSpec((tm, tk), lambda i, j, k: (i, k))
    b_spec = pl.BlockSpec((tn, tk), lambda i, j, k: (j, k)) if tb else pl.BlockSpec((tk, tn), lambda i, j, k: (k, j))
    o_spec = pl.BlockSpec((tm, tn), lambda i, j, k: (i, j))
    vec = pl.BlockSpec((1, tn), lambda i, j, k: (0, j))
    ca, cb = (0 if ta else 1), (1 if tb else 0)
    has_res = res is not None
    assert (norm is None and rms_bwd is None) or tn == n

    ins, specs = [a, b], [a_spec, b_spec]
    if has_res:
        ins.append(res)
        specs.append(o_spec)
    out_specs, out_shape = [o_spec], [jax.ShapeDtypeStruct((m, n), out_dtype)]
    if norm is not None:
        ins.append(norm.reshape(1, n))
        specs.append(vec)
        out_specs.append(o_spec)
        out_shape.append(jax.ShapeDtypeStruct((m, n), BF16))
    if rms_bwd is not None:
        ins += [rms_bwd[0], rms_bwd[1].reshape(1, n), rms_bwd[2]]
        specs += [o_spec, vec, o_spec]
        out_specs.append(vec)
        out_shape.append(jax.ShapeDtypeStruct((1, n), F32))
    n_in, n_out = len(ins), len(out_specs)

    def finish(acc, extra, outs, first_rows):
        y = acc if scale == 1.0 else acc * scale
        if has_res:
            y = y + extra[0][...]
        tail = extra[has_res:]
        if norm is not None:
            rs = lax.rsqrt(jnp.mean(y * y, axis=-1, keepdims=True) + NORM_EPS)
            outs[1][...] = (y * rs * tail[0][...]).astype(BF16)
        if rms_bwd is not None:
            x_ref, g_ref, dres_ref = tail
            xv = x_ref[...]
            rs = lax.rsqrt(jnp.mean(xv * xv, axis=-1, keepdims=True) + NORM_EPS)
            xh = xv * rs
            dgh = y * g_ref[...]
            part = jnp.sum(y * xh, axis=0, keepdims=True)
            y = dres_ref[...] + rs * (dgh - xh * jnp.mean(dgh * xh, axis=-1, keepdims=True))

            @pl.when(first_rows)
            def _():
                outs[1][...] = part

            @pl.when(jnp.logical_not(first_rows))
            def _():
                outs[1][...] += part

        outs[0][...] = y.astype(out_dtype)

    def body(*refs):
        a_ref, b_ref = refs[0], refs[1]
        extra, outs = refs[2:n_in], refs[n_in:n_in + n_out]
        p = _dot(a_ref[...].astype(BF16), b_ref[...].astype(BF16), ca, cb)
        first_rows = pl.program_id(0) == 0
        if nk == 1:
            finish(p, extra, outs, first_rows)
            return
        acc = refs[n_in + n_out]
        k = pl.program_id(2)

        @pl.when(k == 0)
        def _():
            acc[...] = p

        @pl.when(k > 0)
        def _():
            acc[...] += p

        @pl.when(k == nk - 1)
        def _():
            finish(acc[...], extra, outs, first_rows)

    out = pl.pallas_call(
        body, name=name, grid=(m // tm, n // tn, nk), in_specs=specs, out_specs=out_specs, out_shape=out_shape,
        scratch_shapes=[] if nk == 1 else [pltpu.VMEM((tm, tn), F32)],
        compiler_params=_params(("parallel" if rms_bwd is None else "arbitrary", "parallel", "arbitrary")),
    )(*ins)
    if rms_bwd is not None:
        return out[0], out[1].reshape(n)
    return out[0] if n_out == 1 else tuple(out)


def _rms_fwd(x, g, *, name, out_dtype):
    r, d = x.shape
    tr = _tile(r, 512, 16)

    def body(x_ref, g_ref, o_ref):
        xv = x_ref[...]
        rs = lax.rsqrt(jnp.mean(xv * xv, axis=-1, keepdims=True) + NORM_EPS)
        o_ref[...] = (xv * rs * g_ref[...]).astype(out_dtype)

    return pl.pallas_call(
        body, name=name, grid=(r // tr,),
        in_specs=[pl.BlockSpec((tr, d), lambda i: (i, 0)), pl.BlockSpec((1, d), lambda i: (0, 0))],
        out_specs=pl.BlockSpec((tr, d), lambda i: (i, 0)),
        out_shape=jax.ShapeDtypeStruct((r, d), out_dtype), compiler_params=_params(("parallel",)),
    )(x, g.reshape(1, d))


def _rms_bwd(x, g, dh, dres, *, name):
    r, d = x.shape
    tr = _tile(r, 512, 8)
    has_res = dres is not None

    def body(*refs):
        x_ref, g_ref, dh_ref = refs[:3]
        r_ref = refs[3] if has_res else None
        dx_ref, dg_ref = refs[3 + has_res], refs[4 + has_res]
        xv = x_ref[...]
        rs = lax.rsqrt(jnp.mean(xv * xv, axis=-1, keepdims=True) + NORM_EPS)
        xh = xv * rs
        dhv = dh_ref[...]
        dgh = dhv * g_ref[...]
        dx = rs * (dgh - xh * jnp.mean(dgh * xh, axis=-1, keepdims=True))
        if has_res:
            dx = dx + r_ref[...]
        dx_ref[...] = dx
        part = jnp.sum(dhv * xh, axis=0, keepdims=True)

        @pl.when(pl.program_id(0) == 0)
        def _():
            dg_ref[...] = part

        @pl.when(pl.program_id(0) > 0)
        def _():
            dg_ref[...] += part

    row = pl.BlockSpec((tr, d), lambda i: (i, 0))
    vec = pl.BlockSpec((1, d), lambda i: (0, 0))
    ins, specs = [x, g.reshape(1, d), dh], [row, vec, row]
    if has_res:
        ins.append(dres)
        specs.append(row)
    dx, dg = pl.pallas_call(
        body, name=name, grid=(r // tr,), in_specs=specs, out_specs=[row, vec],
        out_shape=[jax.ShapeDtypeStruct((r, d), F32), jax.ShapeDtypeStruct((1, d), F32)],
        compiler_params=_params(("arbitrary",)),
    )(*ins)
    return dx, dg.reshape(d)


def _mm_swiglu(h, w_in, *, name, tm=1024, tn=256):
    t, d = h.shape
    f = w_in.shape[0] // 2
    tm, tn = _tile(t, tm, 16), _tile(f, tn, LANES)
    nj = f // tn

    def body(h_ref, wg_ref, wu_ref, g_ref, u_ref, a_ref):
        hv = h_ref[...]
        gate, up = _dot(hv, wg_ref[...], 1, 1), _dot(hv, wu_ref[...], 1, 1)
        g_ref[...] = gate.astype(BF16)
        u_ref[...] = up.astype(BF16)
        a_ref[...] = (gate * jax.nn.sigmoid(gate) * up).astype(BF16)

    out = pl.BlockSpec((tm, tn), lambda j, i: (i, j))
    return pl.pallas_call(
        body, name=name, grid=(nj, t // tm),
        in_specs=[pl.BlockSpec((tm, d), lambda j, i: (i, 0)), pl.BlockSpec((tn, d), lambda j, i: (j, 0)),
                  pl.BlockSpec((tn, d), lambda j, i: (j + nj, 0))],
        out_specs=[out, out, out], out_shape=[jax.ShapeDtypeStruct((t, f), BF16)] * 3,
        compiler_params=_params(("parallel", "parallel")),
    )(h, w_in, w_in)


def _mm_dswiglu(dy, w_out, gate, up, *, name, scale, tm=1024, tn=256):
    t, d = dy.shape
    f = w_out.shape[0]
    tm, tn = _tile(t, tm, 16), _tile(f, tn, LANES)

    def body(dy_ref, w_ref, g_ref, u_ref, dg_ref, du_ref):
        da = _dot(dy_ref[...].astype(BF16), w_ref[...], 1, 1) * scale
        gv = g_ref[...].astype(F32)
        s = jax.nn.sigmoid(gv)
        dg_ref[...] = (da * u_ref[...].astype(F32) * (s * (1.0 + gv * (1.0 - s)))).astype(BF16)
        du_ref[...] = (da * gv * s).astype(BF16)

    blk = pl.BlockSpec((tm, tn), lambda j, i: (i, j))
    return pl.pallas_call(
        body, name=name, grid=(f // tn, t // tm),
        in_specs=[pl.BlockSpec((tm, d), lambda j, i: (i, 0)), pl.BlockSpec((tn, d), lambda j, i: (j, 0)), blk, blk],
        out_specs=[blk, blk], out_shape=[jax.ShapeDtypeStruct((t, f), BF16)] * 2,
        compiler_params=_params(("parallel", "parallel")),
    )(dy, w_out, gate, up)


def _final_loss(x, g, target, *, name):
    r, d = x.shape
    tr = _tile(r, 512, 8)

    def body(x_ref, g_ref, t_ref, sq_ref, dx_ref, dg_ref):
        xv = x_ref[...]
        gv = g_ref[...]
        rs = lax.rsqrt(jnp.mean(xv * xv, axis=-1, keepdims=True) + NORM_EPS)
        xh = xv * rs
        err = xh * gv - t_ref[...]
        dy = err * (1.0 / d)
        dgh = dy * gv
        dx_ref[...] = rs * (dgh - xh * jnp.mean(dgh * xh, axis=-1, keepdims=True))
        sq = jnp.sum(err * err, axis=0, keepdims=True)
        part = jnp.sum(dy * xh, axis=0, keepdims=True)

        @pl.when(pl.program_id(0) == 0)
        def _():
            sq_ref[...] = sq
            dg_ref[...] = part

        @pl.when(pl.program_id(0) > 0)
        def _():
            sq_ref[...] += sq
            dg_ref[...] += part

    row = pl.BlockSpec((tr, d), lambda i: (i, 0))
    vec = pl.BlockSpec((1, d), lambda i: (0, 0))
    sq, dx, dg = pl.pallas_call(
        body, name=name, grid=(r // tr,), in_specs=[row, vec, row], out_specs=[vec, row, vec],
        out_shape=[jax.ShapeDtypeStruct((1, d), F32), jax.ShapeDtypeStruct((r, d), F32),
                   jax.ShapeDtypeStruct((1, d), F32)],
        compiler_params=_params(("arbitrary",)),
    )(x, g.reshape(1, d), target)
    return sq.reshape(d), dx, dg.reshape(d)


def _scan(a, b, *, name, reverse=False):
    t, w = a.shape
    tb = _tile(t, 1024, 8)
    nblk, ngrp = t // tb, tb // 8

    def body(a_ref, b_ref, h_ref, carry):
        @pl.when(pl.program_id(0) == 0)
        def _():
            carry[...] = jnp.zeros_like(carry)

        row = lax.broadcasted_iota(jnp.int32, (8, w), 0)

        def group(i, c):
            r0 = pl.multiple_of((ngrp - 1 - i if reverse else i) * 8, 8)
            av, bv = a_ref[pl.ds(r0, 8), :], b_ref[pl.ds(r0, 8), :]
            for s in (1, 2, 4):
                keep = row < 8 - s if reverse else row >= s
                shift = 8 - s if reverse else s
                bv = jnp.where(keep, bv + av * pltpu.roll(bv, shift, axis=0), bv)
                av = jnp.where(keep, av * pltpu.roll(av, shift, axis=0), av)
            hv = bv + av * c
            h_ref[pl.ds(r0, 8), :] = hv
            return hv[0:1, :] if reverse else hv[7:8, :]

        carry[0:1, :] = lax.fori_loop(0, ngrp, group, carry[0:1, :])

    blk = pl.BlockSpec((tb, w), lambda i: (nblk - 1 - i if reverse else i, 0))
    return pl.pallas_call(
        body, name=name, grid=(t // tb,), in_specs=[blk, blk], out_specs=blk,
        out_shape=jax.ShapeDtypeStruct((t, w), F32), scratch_shapes=[pltpu.VMEM((8, w), F32)],
        compiler_params=_params(("arbitrary",)),
    )(a, b)


def _row_specs(t, tb, width, col):
    per = tb // 8
    main = pl.BlockSpec((tb, width), lambda i: (i, col))
    before = pl.BlockSpec((8, width), lambda i: (jnp.maximum(i * per - 1, 0), col))
    after = pl.BlockSpec((8, width), lambda i: (jnp.minimum((i + 1) * per, t // 8 - 1), col))
    return main, before, after


def _with_rows_before(x_ref, before_ref):
    return jnp.concatenate([jnp.where(pl.program_id(0) > 0, before_ref[...], 0.0), x_ref[...]], axis=0)


def _tap(xe, s):
    return xe[8:] if s == 0 else pltpu.roll(xe, s, axis=0)[8:]


def _conv_rows(xe, w_ref):
    return sum(_tap(xe, CONV_K - 1 - k) * w_ref[k:k + 1, :] for k in range(CONV_K))


def _conv_bwd(dy, x, col, w, *, name):
    t, width = dy.shape
    tb = _tile(t, 256, 8)
    nblk = t // tb

    def body(dy_ref, dy_after_ref, x_ref, x_before_ref, w_ref, dx_ref, dw_ref):
        i = pl.program_id(0)
        dyv = dy_ref[...]
        dye = jnp.concatenate([dyv, jnp.where(i < nblk - 1, dy_after_ref[...], 0.0)], axis=0)
        dx = dyv * w_ref[CONV_K - 1:CONV_K, :]
        for s in range(1, CONV_K):
            dx = dx + pltpu.roll(dye, tb + 8 - s, axis=0)[:tb] * w_ref[CONV_K - 1 - s:CONV_K - s, :]
        dx_ref[...] = dx.astype(BF16)
        xe = _with_rows_before(x_ref, x_before_ref)

        @pl.when(i == 0)
        def _():
            dw_ref[...] = jnp.zeros_like(dw_ref)

        for k in range(CONV_K):
            dw_ref[k:k + 1, :] += jnp.sum(dyv * _tap(xe, CONV_K - 1 - k), axis=0, keepdims=True)

    main, _, after = _row_specs(t, tb, width, 0)
    xmain, xbefore, _ = _row_specs(t, tb, width, col)
    dx, dw = pl.pallas_call(
        body, name=name, grid=(nblk,),
        in_specs=[main, after, xmain, xbefore, pl.BlockSpec((CONV_K, width), lambda i: (0, 0))],
        out_specs=[main, pl.BlockSpec((8, width), lambda i: (0, 0))],
        out_shape=[jax.ShapeDtypeStruct((t, width), BF16), jax.ShapeDtypeStruct((8, width), F32)],
        compiler_params=_params(("arbitrary",)),
    )(dy, dy, x, x, w)
    return dx, dw[:CONV_K]


def _expm1(x):
    small = x * (1.0 + x * (0.5 + x * (1.0 / 6.0 + x * (1.0 / 24.0 + x * (1.0 / 120.0 + x * (1.0 / 720.0))))))
    return jnp.where(jnp.abs(x) < 0.1, small, jnp.exp(x) - 1.0)


def _lru_gate_terms(xc, wa_ref, ba_ref, wx_ref, bx_ref, sp_ref):
    xb = xc.astype(BF16)
    r = jax.nn.sigmoid(_dot(xb, wa_ref[...], 1, 0) + ba_ref[...])
    i = jax.nn.sigmoid(_dot(xb, wx_ref[...], 1, 0) + bx_ref[...])
    log_a = -r * sp_ref[...]
    return r, i, jnp.exp(log_a), jnp.sqrt(-_expm1(2.0 * log_a))


def _lru_gates_fwd(proj, col, conv_w, conv_b, wa, ba, wx, bx, sp, *, name):
    t = proj.shape[0]
    width = conv_w.shape[1]
    tb = _tile(t, 512, 8)

    def body(x_ref, x_before_ref, cw_ref, cb_ref, wa_ref, ba_ref, wx_ref, bx_ref, sp_ref, a_ref, b_ref, xc_ref):
        xc = _conv_rows(_with_rows_before(x_ref, x_before_ref), cw_ref) + cb_ref[...]
        r, i, a, mult = _lru_gate_terms(xc, wa_ref, ba_ref, wx_ref, bx_ref, sp_ref)
        a_ref[...] = a
        b_ref[...] = mult * i * xc
        xc_ref[...] = xc

    main, before, _ = _row_specs(t, tb, width, col)
    out = pl.BlockSpec((tb, width), lambda i: (i, 0))
    vec = pl.BlockSpec((1, width), lambda i: (0, 0))
    mat = pl.BlockSpec((width, width), lambda i: (0, 0))
    return pl.pallas_call(
        body, name=name, grid=(t // tb,),
        in_specs=[main, before, pl.BlockSpec((CONV_K, width), lambda i: (0, 0)), vec, mat, vec, mat, vec, vec],
        out_specs=[out] * 3, out_shape=[jax.ShapeDtypeStruct((t, width), F32)] * 3,
        compiler_params=_params(("parallel",)),
    )(proj, proj, conv_w, conv_b.reshape(1, -1), wa, ba.reshape(1, -1), wx, bx.reshape(1, -1), sp.reshape(1, -1))


def _lru_gates_bwd(xc, dtot, h_prev, wa, ba, wx, bx, sp, *, name):
    t, width = xc.shape
    tb = _tile(t, 512, 8)

    def body(xc_ref, dt_ref, hp_ref, wa_ref, ba_ref, wx_ref, bx_ref, sp_ref, dxc_ref, dwa_ref, dwx_ref, vec_ref):
        xc = xc_ref[...]
        r, i, a, mult = _lru_gate_terms(xc, wa_ref, ba_ref, wx_ref, bx_ref, sp_ref)
        db = dt_ref[...]
        d_la = db * hp_ref[...] * a - db * i * xc * (a * a / mult)
        d_pa = (-d_la * sp_ref[...]) * r * (1.0 - r)
        d_pi = db * mult * xc * i * (1.0 - i)
        dab, dib = d_pa.astype(BF16), d_pi.astype(BF16)
        dxc = db * mult * i + _dot(dab, wa_ref[...], 1, 1) + _dot(dib, wx_ref[...], 1, 1)
        dxc_ref[...] = dxc
        xb = xc.astype(BF16)
        rows = [jnp.sum(z, axis=0, keepdims=True) for z in (d_pa, d_pi, -d_la * r, dxc)]

        @pl.when(pl.program_id(0) == 0)
        def _():
            dwa_ref[...] = jnp.zeros_like(dwa_ref)
            dwx_ref[...] = jnp.zeros_like(dwx_ref)
            vec_ref[...] = jnp.zeros_like(vec_ref)

        dwa_ref[...] += _dot(xb, dab, 0, 0)
        dwx_ref[...] += _dot(xb, dib, 0, 0)
        for j, z in enumerate(rows):
            vec_ref[j:j + 1, :] += z

    blk = pl.BlockSpec((tb, width), lambda i: (i, 0))
    vec = pl.BlockSpec((1, width), lambda i: (0, 0))
    mat = pl.BlockSpec((width, width), lambda i: (0, 0))
    dxc, dwa, dwx, vecs = pl.pallas_call(
        body, name=name, grid=(t // tb,), in_specs=[blk, blk, blk, mat, vec, mat, vec, vec],
        out_specs=[blk, mat, mat, pl.BlockSpec((8, width), lambda i: (0, 0))],
        out_shape=[jax.ShapeDtypeStruct((t, width), F32), jax.ShapeDtypeStruct((width, width), F32),
                   jax.ShapeDtypeStruct((width, width), F32), jax.ShapeDtypeStruct((8, width), F32)],
        compiler_params=_params(("arbitrary",)),
    )(xc, dtot, h_prev, wa, ba.reshape(1, -1), wx, bx.reshape(1, -1), sp.reshape(1, -1))
    return dxc, dwa, dwx, vecs[:4]


GELU_C = math.sqrt(2.0 / math.pi)


def _gelu_terms(x):
    th = jnp.tanh(GELU_C * (x + 0.044715 * x * x * x))
    return 0.5 * x * (1.0 + th), 0.5 * (1.0 + th) + 0.5 * x * (1.0 - th * th) * GELU_C * (1.0 + 3 * 0.044715 * x * x)


def _mix_join_fwd(outs, lses, hs, proj, gr_col, *, name):
    t, w = hs.shape
    tb = _tile(t, 512, 16)
    n = len(outs)

    def body(*refs):
        o_refs, l_refs = refs[:n], refs[n:2 * n]
        hs_ref, gr_ref, cat_ref, attn_ref, lse_ref = refs[2 * n:]
        ls = [r[...] for r in l_refs]
        m = ls[0]
        for l in ls[1:]:
            m = jnp.maximum(m, l)
        ws = [jnp.exp(l - m) for l in ls]
        den = sum(ws)
        attn = sum(wt * r[...] for wt, r in zip(ws, o_refs)) / den
        attn_ref[...] = attn
        lse_ref[...] = m + jnp.log(den)
        cat_ref[:, :w] = attn.astype(BF16)
        cat_ref[:, w:] = (hs_ref[...] * _gelu_terms(gr_ref[...])[0]).astype(BF16)

    blk = pl.BlockSpec((tb, w), lambda i: (i, 0))
    return pl.pallas_call(
        body, name=name, grid=(t // tb,),
        in_specs=[blk] * (2 * n + 1) + [pl.BlockSpec((tb, w), lambda i: (i, gr_col))],
        out_specs=[pl.BlockSpec((tb, 2 * w), lambda i: (i, 0)), blk, blk],
        out_shape=[jax.ShapeDtypeStruct((t, 2 * w), BF16), jax.ShapeDtypeStruct((t, w), F32),
                   jax.ShapeDtypeStruct((t, w), F32)],
        compiler_params=_params(("parallel",)),
    )(*outs, *lses, hs, proj)


def _mix_join_bwd(dcat, attn, lse, hs, proj, gr_col, *, name):
    t, w = hs.shape
    hd = w // ATTN_HEADS
    tb = _tile(t, 512, 16)

    def body(dcat_ref, attn_ref, lse_ref, hs_ref, gr_ref, stats_ref, dhs_ref, dgr_ref):
        stats_ref[...] = jnp.zeros_like(stats_ref)
        for h in range(ATTN_HEADS):
            sl = slice(h * hd, (h + 1) * hd)
            stats_ref[:, h:h + 1] = lse_ref[:, h * hd:h * hd + 1]
            stats_ref[:, ATTN_HEADS + h:ATTN_HEADS + h + 1] = jnp.sum(dcat_ref[:, sl] * attn_ref[:, sl], axis=-1, keepdims=True)
        dy = dcat_ref[:, w:]
        g, dg = _gelu_terms(gr_ref[...])
        dhs_ref[...] = dy * g
        dgr_ref[...] = (dy * hs_ref[...] * dg).astype(BF16)

    blk = pl.BlockSpec((tb, w), lambda i: (i, 0))
    return pl.pallas_call(
        body, name=name, grid=(t // tb,),
        in_specs=[pl.BlockSpec((tb, 2 * w), lambda i: (i, 0)), blk, blk, blk, pl.BlockSpec((tb, w), lambda i: (i, gr_col))],
        out_specs=[pl.BlockSpec((tb, LANES), lambda i: (i, 0)), blk, blk],
        out_shape=[jax.ShapeDtypeStruct((t, LANES), F32), jax.ShapeDtypeStruct((t, w), F32),
                   jax.ShapeDtypeStruct((t, w), BF16)],
        compiler_params=_params(("parallel",)),
    )(dcat, attn, lse, hs, proj)


def _silu_terms(x):
    s = jax.nn.sigmoid(x)
    return x * s, s * (1.0 + x * (1.0 - s))


def _dn_prep_fwd(proj, conv_w, *, name):
    t = proj.shape[0]
    w3 = conv_w.shape[1]
    w = w3 // 3
    hd = w // DN_HEADS
    tb = _tile(t, 256, 8)

    def body(x_ref, x_before_ref, cw_ref, q_ref, k_ref, v_ref, c_ref):
        c = _conv_rows(_with_rows_before(x_ref, x_before_ref), cw_ref)
        c_ref[...] = c
        s = _silu_terms(c)[0]
        v_ref[...] = s[:, 2 * w:]
        for part, ref, scale in ((0, q_ref, hd ** -0.5), (1, k_ref, 1.0)):
            for h in range(DN_HEADS):
                z = s[:, part * w + h * hd:part * w + (h + 1) * hd]
                ref[:, h * hd:(h + 1) * hd] = z * (lax.rsqrt(jnp.sum(z * z, axis=-1, keepdims=True) + 1e-6) * scale)

    main, before, _ = _row_specs(t, tb, w3, 0)
    out = pl.BlockSpec((tb, w), lambda i: (i, 0))
    return pl.pallas_call(
        body, name=name, grid=(t // tb,), in_specs=[main, before, pl.BlockSpec((CONV_K, w3), lambda i: (0, 0))],
        out_specs=[out, out, out, pl.BlockSpec((tb, w3), lambda i: (i, 0))],
        out_shape=[jax.ShapeDtypeStruct((t, w), F32)] * 3 + [jax.ShapeDtypeStruct((t, w3), F32)],
        compiler_params=_params(("parallel",)),
    )(proj, proj, conv_w)


def _dn_prep_bwd(c, dq, dk, dv, *, name):
    t, w3 = c.shape
    w = w3 // 3
    hd = w // DN_HEADS
    tb = _tile(t, 256, 8)

    def body(c_ref, dq_ref, dk_ref, dv_ref, dc_ref):
        cv = c_ref[...]
        s, ds = _silu_terms(cv)
        dc_ref[:, 2 * w:] = dv_ref[...] * ds[:, 2 * w:]
        for part, ref, scale in ((0, dq_ref, hd ** -0.5), (1, dk_ref, 1.0)):
            for h in range(DN_HEADS):
                cols = slice(part * w + h * hd, part * w + (h + 1) * hd)
                z = s[:, cols]
                rn = lax.rsqrt(jnp.sum(z * z, axis=-1, keepdims=True) + 1e-6)
                y = z * rn
                dy = ref[:, h * hd:(h + 1) * hd] * scale
                dc_ref[:, cols] = rn * (dy - y * jnp.sum(dy * y, axis=-1, keepdims=True)) * ds[:, cols]

    blk = pl.BlockSpec((tb, w), lambda i: (i, 0))
    wide = pl.BlockSpec((tb, w3), lambda i: (i, 0))
    return pl.pallas_call(
        body, name=name, grid=(t // tb,), in_specs=[wide, blk, blk, blk], out_specs=wide,
        out_shape=jax.ShapeDtypeStruct((t, w3), F32), compiler_params=_params(("parallel",)),
    )(c, dq, dk, dv)


def _dn_gate_fwd(o, proj, z_col, o_norm, *, name):
    t, w = o.shape
    hd = w // DN_HEADS
    tb = _tile(t, 512, 16)

    def body(o_ref, z_ref, g_ref, y_ref):
        for h in range(DN_HEADS):
            sl = slice(h * hd, (h + 1) * hd)
            ov = o_ref[:, sl]
            rn = lax.rsqrt(jnp.mean(ov * ov, axis=-1, keepdims=True) + NORM_EPS)
            y_ref[:, sl] = (ov * rn * g_ref[...] * _silu_terms(z_ref[:, sl])[0]).astype(BF16)

    blk = pl.BlockSpec((tb, w), lambda i: (i, 0))
    return pl.pallas_call(
        body, name=name, grid=(t // tb,),
        in_specs=[blk, pl.BlockSpec((tb, w), lambda i: (i, z_col)), pl.BlockSpec((1, hd), lambda i: (0, 0))],
        out_specs=blk, out_shape=jax.ShapeDtypeStruct((t, w), BF16), compiler_params=_params(("parallel",)),
    )(o, proj, o_norm.reshape(1, hd))


def _dn_gate_bwd(o, proj, z_col, o_norm, dy, *, name):
    t, w = o.shape
    hd = w // DN_HEADS
    tb = _tile(t, 512, 16)

    def body(o_ref, z_ref, g_ref, dy_ref, do_ref, dz_ref, dg_ref):
        gv = g_ref[...]
        dg = jnp.zeros((1, hd), F32)
        for h in range(DN_HEADS):
            sl = slice(h * hd, (h + 1) * hd)
            ov, dyv = o_ref[:, sl], dy_ref[:, sl]
            sz, dsz = _silu_terms(z_ref[:, sl])
            rn = lax.rsqrt(jnp.mean(ov * ov, axis=-1, keepdims=True) + NORM_EPS)
            nv = ov * rn
            dz_ref[:, sl] = (dyv * nv * gv * dsz).astype(BF16)
            dn = dyv * gv * sz
            do_ref[:, sl] = rn * (dn - nv * jnp.mean(dn * nv, axis=-1, keepdims=True))
            dg = dg + jnp.sum(dyv * nv * sz, axis=0, keepdims=True)

        @pl.when(pl.program_id(0) == 0)
        def _():
            dg_ref[...] = dg

        @pl.when(pl.program_id(0) > 0)
        def _():
            dg_ref[...] += dg

    blk = pl.BlockSpec((tb, w), lambda i: (i, 0))
    vec = pl.BlockSpec((1, hd), lambda i: (0, 0))
    do, dz, dg = pl.pallas_call(
        body, name=name, grid=(t // tb,), in_specs=[blk, pl.BlockSpec((tb, w), lambda i: (i, z_col)), vec, blk],
        out_specs=[blk, blk, vec],
        out_shape=[jax.ShapeDtypeStruct((t, w), F32), jax.ShapeDtypeStruct((t, w), BF16),
                   jax.ShapeDtypeStruct((1, hd), F32)],
        compiler_params=_params(("arbitrary",)),
    )(o, proj, o_norm.reshape(1, hd), dy)
    return do, dz, dg.reshape(hd)


def _band_masks(n):
    qi = lax.broadcasted_iota(jnp.int32, (ATTN_BLOCK, ATTN_BLOCK), 0)
    kj = lax.broadcasted_iota(jnp.int32, (ATTN_BLOCK, ATTN_BLOCK), 1)
    return kj <= qi, jnp.logical_and(kj >= qi, n > 0)


def _dattn_fwd(qkv, dil, *, name):
    seq, w = qkv.shape[0], qkv.shape[1] // (3 * dil)
    t = seq * dil
    hd = w // ATTN_HEADS
    assert seq % ATTN_BLOCK == 0
    nb = seq // ATTN_BLOCK
    scale = hd ** -0.5

    def body(q_ref, kp_ref, kc_ref, vp_ref, vc_ref, o_ref, lse_ref):
        mc, mp = _band_masks(pl.program_id(1))
        heads = range(ATTN_HEADS)
        sls = [slice(h * hd, (h + 1) * hd) for h in heads]
        qs = [(q_ref[:, sl] * scale).astype(BF16) for sl in sls]
        scs = [jnp.where(mc, _dot(qs[h], kc_ref[:, sls[h]].astype(BF16), 1, 1), NEG) for h in heads]
        sps = [jnp.where(mp, _dot(qs[h], kp_ref[:, sls[h]].astype(BF16), 1, 1), NEG) for h in heads]
        ms = [jnp.maximum(jnp.max(sc, axis=-1, keepdims=True), jnp.max(sp, axis=-1, keepdims=True))
              for sc, sp in zip(scs, sps)]
        pcs = [jnp.exp(sc - m) for sc, m in zip(scs, ms)]
        pps = [jnp.exp(sp - m) for sp, m in zip(sps, ms)]
        dens = [jnp.sum(pc, axis=-1, keepdims=True) + jnp.sum(pp, axis=-1, keepdims=True) for pc, pp in zip(pcs, pps)]
        outs = [_dot(pcs[h].astype(BF16), vc_ref[:, sls[h]].astype(BF16), 1, 0)
                + _dot(pps[h].astype(BF16), vp_ref[:, sls[h]].astype(BF16), 1, 0) for h in heads]
        for h in heads:
            o_ref[:, sls[h]] = outs[h] / dens[h]
            lse_ref[:, sls[h]] = jnp.broadcast_to(ms[h] + jnp.log(dens[h]), (ATTN_BLOCK, hd))

    cur = pl.BlockSpec((ATTN_BLOCK, w), lambda r, n: (n, r))
    part = lambda j, row: pl.BlockSpec((ATTN_BLOCK, w), lambda r, n: (row(n), 3 * r + j))
    here, before = (lambda n: n), (lambda n: jnp.maximum(n - 1, 0))
    o, lse = pl.pallas_call(
        body, name=name, grid=(dil, nb),
        in_specs=[part(0, here), part(1, before), part(1, here), part(2, before), part(2, here)], out_specs=[cur, cur],
        out_shape=[jax.ShapeDtypeStruct((seq, dil * w), F32)] * 2, compiler_params=_params(("parallel", "parallel")),
    )(qkv, qkv, qkv, qkv, qkv)
    return o.reshape(t, w), lse.reshape(t, w)


def _dattn_bwd(qkv, do, stats, dil, *, name):
    t, w = do.shape
    hd = w // ATTN_HEADS
    seq = t // dil
    nb = seq // ATTN_BLOCK
    scale = hd ** -0.5

    def body(qc_ref, qn_ref, doc_ref, don_ref, sc_ref, sn_ref, kp_ref, kc_ref, vp_ref, vc_ref, dqkv_ref):
        n = pl.program_id(1)
        mc, mp = _band_masks(n)
        _, mx = _band_masks(jnp.where(n + 1 < nb, 1, 0))
        heads = range(ATTN_HEADS)
        sls = [slice(h * hd, (h + 1) * hd) for h in heads]
        lse_of, delta_of = (lambda ref, h: ref[:, h:h + 1]), (lambda ref, h: ref[:, ATTN_HEADS + h:ATTN_HEADS + h + 1])
        bf = lambda ref, scl=None: [(ref[:, sl] if scl is None else ref[:, sl] * scl).astype(BF16) for sl in sls]
        qc, qn, kc, kp = bf(qc_ref, scale), bf(qn_ref, scale), bf(kc_ref), bf(kp_ref)
        vc, vp, doc, don = bf(vc_ref), bf(vp_ref), bf(doc_ref), bf(don_ref)
        p_c = [jnp.exp(jnp.where(mc, _dot(qc[h], kc[h], 1, 1), NEG) - lse_of(sc_ref, h)) for h in heads]
        p_p = [jnp.exp(jnp.where(mp, _dot(qc[h], kp[h], 1, 1), NEG) - lse_of(sc_ref, h)) for h in heads]
        p_x = [jnp.exp(jnp.where(mx, _dot(qn[h], kc[h], 1, 1), NEG) - lse_of(sn_ref, h)) for h in heads]
        ds_c = [(p_c[h] * (_dot(doc[h], vc[h], 1, 1) - delta_of(sc_ref, h))).astype(BF16) for h in heads]
        ds_p = [(p_p[h] * (_dot(doc[h], vp[h], 1, 1) - delta_of(sc_ref, h))).astype(BF16) for h in heads]
        ds_x = [(p_x[h] * (_dot(don[h], vc[h], 1, 1) - delta_of(sn_ref, h))).astype(BF16) for h in heads]
        for h in heads:
            at = lambda part: slice(part * w + h * hd, part * w + (h + 1) * hd)
            dqkv_ref[:, at(0)] = ((_dot(ds_c[h], kc[h], 1, 0) + _dot(ds_p[h], kp[h], 1, 0)) * scale).astype(BF16)
            dqkv_ref[:, at(1)] = (_dot(ds_c[h], qc[h], 0, 0) + _dot(ds_x[h], qn[h], 0, 0)).astype(BF16)
            dqkv_ref[:, at(2)] = (_dot(p_c[h].astype(BF16), doc[h], 0, 0) + _dot(p_x[h].astype(BF16), don[h], 0, 0)).astype(BF16)

    cur = pl.BlockSpec((ATTN_BLOCK, w), lambda r, n: (n, r))
    nxt = pl.BlockSpec((ATTN_BLOCK, w), lambda r, n: (jnp.minimum(n + 1, nb - 1), r))
    part = lambda j, row: pl.BlockSpec((ATTN_BLOCK, w), lambda r, n: (row(n), 3 * r + j))
    here, before, after = (lambda n: n), (lambda n: jnp.maximum(n - 1, 0)), (lambda n: jnp.minimum(n + 1, nb - 1))
    stat = lambda row: pl.BlockSpec((ATTN_BLOCK, LANES), lambda r, n: (row(n), r))
    do, stats = do.reshape(seq, dil * w), stats.reshape(seq, dil * LANES)
    dqkv = pl.pallas_call(
        body, name=name, grid=(dil, nb),
        in_specs=[part(0, here), part(0, after), cur, nxt, stat(here), stat(after), part(1, before), part(1, here),
                  part(2, before), part(2, here)], out_specs=pl.BlockSpec((ATTN_BLOCK, 3 * w), lambda r, n: (n, r)),
        out_shape=jax.ShapeDtypeStruct((seq, dil * 3 * w), BF16), compiler_params=_params(("parallel", "parallel")),
    )(qkv, qkv, do, do, stats, stats, qkv, qkv, qkv, qkv)
    return dqkv.reshape(t, 3 * w)


def _xattn_fwd(q, k, v, *, name):
    t, w = q.shape
    nm = k.shape[0]
    hd = w // XA_HEADS
    scale = hd ** -0.5
    tq = _tile(t, 512, 8)

    def body(q_ref, k_ref, v_ref, o_ref):
        for h in range(XA_HEADS):
            sl = slice(h * hd, (h + 1) * hd)
            s = _dot((q_ref[:, sl] * scale).astype(BF16), k_ref[:, sl].astype(BF16), 1, 1)
            p = jnp.exp(s - jnp.max(s, axis=-1, keepdims=True))
            p = p / jnp.sum(p, axis=-1, keepdims=True)
            o_ref[:, sl] = _dot(p.astype(BF16), v_ref[:, sl].astype(BF16), 1, 0)

    qs = pl.BlockSpec((tq, w), lambda i: (i, 0))
    ks = pl.BlockSpec((nm, w), lambda i: (0, 0))
    return pl.pallas_call(
        body, name=name, grid=(t // tq,), in_specs=[qs, ks, ks], out_specs=qs,
        out_shape=jax.ShapeDtypeStruct((t, w), F32), compiler_params=_params(("parallel",)),
    )(q, k, v)


def _xattn_bwd(q, k, v, do, *, name):
    t, w = q.shape
    nm = k.shape[0]
    hd = w // XA_HEADS
    scale = hd ** -0.5
    tq = _tile(t, 512, 8)

    def body(q_ref, k_ref, v_ref, do_ref, dq_ref, dk_ref, dv_ref):
        first = pl.program_id(0) == 0
        for h in range(XA_HEADS):
            sl = slice(h * hd, (h + 1) * hd)
            qh = (q_ref[:, sl] * scale).astype(BF16)
            kh, vh, doh = k_ref[:, sl].astype(BF16), v_ref[:, sl].astype(BF16), do_ref[:, sl].astype(BF16)
            s = _dot(qh, kh, 1, 1)
            p = jnp.exp(s - jnp.max(s, axis=-1, keepdims=True))
            p = p / jnp.sum(p, axis=-1, keepdims=True)
            dp = _dot(doh, vh, 1, 1)
            ds = (p * (dp - jnp.sum(p * dp, axis=-1, keepdims=True))).astype(BF16)
            dq_ref[:, sl] = _dot(ds, kh, 1, 0) * scale
            dk = _dot(ds, qh, 0, 0)
            dv = _dot(p.astype(BF16), doh, 0, 0)

            @pl.when(first)
            def _():
                dk_ref[:, sl] = dk
                dv_ref[:, sl] = dv

            @pl.when(jnp.logical_not(first))
            def _():
                dk_ref[:, sl] += dk
                dv_ref[:, sl] += dv

    qs = pl.BlockSpec((tq, w), lambda i: (i, 0))
    ks = pl.BlockSpec((nm, w), lambda i: (0, 0))
    return pl.pallas_call(
        body, name=name, grid=(t // tq,), in_specs=[qs, ks, ks, qs], out_specs=[qs, ks, ks],
        out_shape=[jax.ShapeDtypeStruct((t, w), F32), jax.ShapeDtypeStruct((nm, w), F32),
                   jax.ShapeDtypeStruct((nm, w), F32)],
        compiler_params=_params(("arbitrary",)),
    )(q, k, v, do)


def _dn_head_terms(q_ref, k_ref, gc_ref, gcr_ref, b_ref, h, hd):
    c = DN_CHUNK
    sl = slice(h * hd, (h + 1) * hd)
    qh, kh = q_ref[:, sl], k_ref[:, sl]
    gcc, gcr_h, bh = gc_ref[:, h:h + 1], gcr_ref[0, h:h + 1, :], b_ref[:, h:h + 1]
    row = lax.broadcasted_iota(jnp.int32, (c, c), 0)
    col = lax.broadcasted_iota(jnp.int32, (c, c), 1)
    decay = jnp.exp(jnp.where(row >= col, gcc - gcr_h, NEG))
    kb = kh * bh
    kkt = _dot(kb.astype(BF16), kh.astype(BF16), 1, 1)
    qkt = _dot(qh.astype(BF16), kh.astype(BF16), 1, 1)
    gl = gcc[c - 1:c, :]
    return dict(sl=sl, q=qh, k=kh, gcc=gcc, b=bh, row=row, col=col, decay=decay, kb=kb, kkt=kkt, qkt=qkt,
                e=jnp.exp(gcc), el=jnp.exp(gl), r=jnp.exp(gl - gcc))


def _dn_local_fwd(q, k, v, gc, gcr, beta, *, name):
    t, w = q.shape
    hd = w // DN_HEADS
    c = DN_CHUNK
    nch = t // c
    heads = range(DN_HEADS)

    def body(q_ref, k_ref, v_ref, gc_ref, gcr_ref, b_ref,
             u_ref, wq_ref, wqt_ref, kr_ref, krt_ref, qk_ref, qkt_ref, invt_ref):
        tm = [_dn_head_terms(q_ref, k_ref, gc_ref, gcr_ref, b_ref, h, hd) for h in heads]
        pw = [jnp.where(m['row'] > m['col'], m['kkt'] * m['decay'], 0.0) for m in tm]
        inv = [(m['row'] == m['col']).astype(F32) - p for m, p in zip(tm, pw)]
        for _ in range(int(math.log2(c)) - 1):
            pw = [_dot(p.astype(BF16), p.astype(BF16), 1, 0) for p in pw]
            inv = [i + _dot(i.astype(BF16), p.astype(BF16), 1, 0) for i, p in zip(inv, pw)]
        for h, m in zip(heads, tm):
            rhs = jnp.concatenate([v_ref[:, m['sl']] * m['b'], m['kb'] * m['e']], axis=1).astype(BF16)
            sol = _dot(inv[h].astype(BF16), rhs, 1, 0)
            u_ref[:, m['sl']] = sol[:, :hd]
            wq = jnp.concatenate([sol[:, hd:], m['q'] * m['e']], axis=0)
            kr = m['k'] * m['r']
            qk = m['qkt'] * m['decay']
            wq_ref[0, h], wqt_ref[0, h] = wq.astype(BF16), wq.T.astype(BF16)
            kr_ref[0, h], krt_ref[0, h] = kr.astype(BF16), kr.T.astype(BF16)
            qk_ref[0, h], qkt_ref[0, h] = qk.astype(BF16), qk.T.astype(BF16)
            invt_ref[0, h] = inv[h].T.astype(BF16)

    rows = pl.BlockSpec((c, w), lambda n: (n, 0))
    cols = pl.BlockSpec((c, DN_HEADS), lambda n: (n, 0))
    rowg = pl.BlockSpec((1, DN_HEADS, c), lambda n: (n, 0, 0))
    per = lambda a, b: (pl.BlockSpec((1, DN_HEADS, a, b), lambda n: (n, 0, 0, 0)),
                        jax.ShapeDtypeStruct((nch, DN_HEADS, a, b), BF16))
    outs = [(rows, jax.ShapeDtypeStruct((t, w), F32)), per(2 * c, hd), per(hd, 2 * c), per(c, hd), per(hd, c),
            per(c, c), per(c, c), per(c, c)]
    return pl.pallas_call(
        body, name=name, grid=(nch,), in_specs=[rows, rows, rows, cols, rowg, cols],
        out_specs=[o[0] for o in outs], out_shape=[o[1] for o in outs], compiler_params=_params(("parallel",)),
    )(q, k, v, gc, gcr, beta)


def _dn_state_fwd(u, wq, krt, qk, gc, *, name):
    t, w = u.shape
    hd = w // DN_HEADS
    c = DN_CHUNK
    nch = t // c
    grp = _tile(nch, DN_STEP_CHUNKS, 1)

    def body(u_ref, wq_ref, krt_ref, qk_ref, gc_ref, o_ref, vn_ref, s_ref, state):
        @pl.when(pl.program_id(0) == 0)
        def _():
            state[...] = jnp.zeros_like(state)

        heads = range(DN_HEADS)
        sls = [slice(h * hd, (h + 1) * hd) for h in heads]
        shs = [state[h] for h in heads]
        for j in range(grp):
            rs = slice(j * c, (j + 1) * c)
            wss = [_dot(wq_ref[j, h], shs[h].astype(BF16), 1, 0) for h in heads]
            vns = [(u_ref[rs, sls[h]] - wss[h][:c]).astype(BF16) for h in heads]
            outs = [wss[h][c:] + _dot(qk_ref[j, h], vns[h], 1, 0) for h in heads]
            for h in heads:
                s_ref[j, h] = shs[h]
                vn_ref[rs, sls[h]] = vns[h]
                o_ref[rs, sls[h]] = outs[h]
            shs = [shs[h] * jnp.exp(gc_ref[(j + 1) * c - 1:(j + 1) * c, h:h + 1]) + _dot(krt_ref[j, h], vns[h], 1, 0)
                   for h in heads]
        for h in heads:
            state[h] = shs[h]

    rows = pl.BlockSpec((grp * c, w), lambda n: (n, 0))
    per = lambda a, b: pl.BlockSpec((grp, DN_HEADS, a, b), lambda n: (n, 0, 0, 0))
    return pl.pallas_call(
        body, name=name, grid=(nch // grp,),
        in_specs=[rows, per(2 * c, hd), per(hd, c), per(c, c), pl.BlockSpec((grp * c, DN_HEADS), lambda n: (n, 0))],
        out_specs=[rows, rows, per(hd, hd)],
        out_shape=[jax.ShapeDtypeStruct((t, w), F32), jax.ShapeDtypeStruct((t, w), BF16),
                   jax.ShapeDtypeStruct((nch, DN_HEADS, hd, hd), F32)],
        scratch_shapes=[pltpu.VMEM((DN_HEADS, hd, hd), F32)], compiler_params=_params(("arbitrary",)),
    )(u, wq, krt, qk, gc)


def _dn_state_bwd(do, qkt, kr, vn, states, wqt, gc, *, name):
    t, w = do.shape
    hd = w // DN_HEADS
    c = DN_CHUNK
    nch = t // c
    grp = _tile(nch, DN_STEP_CHUNKS, 1)

    def body(do_ref, qkt_ref, kr_ref, vn_ref, s_ref, wqt_ref, gc_ref, dvn_ref, dkr_ref, del_ref, dstate):
        @pl.when(pl.program_id(0) == 0)
        def _():
            dstate[...] = jnp.zeros_like(dstate)

        heads = range(DN_HEADS)
        sls = [slice(h * hd, (h + 1) * hd) for h in heads]
        dsns = [dstate[h] for h in heads]
        for j in reversed(range(grp)):
            rs = slice(j * c, (j + 1) * c)
            dsbs = [d.astype(BF16) for d in dsns]
            dobs = [do_ref[rs, sl].astype(BF16) for sl in sls]
            dvns = [(_dot(qkt_ref[j, h], dobs[h], 1, 0) + _dot(kr_ref[j, h], dsbs[h], 1, 0)).astype(BF16) for h in heads]
            dkrs = [_dot(vn_ref[rs, sls[h]], dsbs[h], 1, 1) for h in heads]
            for h in heads:
                dvn_ref[rs, sls[h]] = dvns[h]
                dkr_ref[rs, sls[h]] = dkrs[h]
                d_el = jnp.sum(jnp.sum(dsns[h] * s_ref[j, h], axis=1, keepdims=True), axis=0, keepdims=True)
                del_ref[j, h:h + 1, :] = jnp.broadcast_to(d_el, (1, LANES))
            dsns = [dsns[h] * jnp.exp(gc_ref[(j + 1) * c - 1:(j + 1) * c, h:h + 1])
                    + _dot(wqt_ref[j, h], jnp.concatenate([-dvns[h], dobs[h]], axis=0), 1, 0) for h in heads]
        for h in heads:
            dstate[h] = dsns[h]

    rev = lambda n: nch // grp - 1 - n
    rows = pl.BlockSpec((grp * c, w), lambda n: (rev(n), 0))
    per = lambda a, b: pl.BlockSpec((grp, DN_HEADS, a, b), lambda n: (rev(n), 0, 0, 0))
    return pl.pallas_call(
        body, name=name, grid=(nch // grp,),
        in_specs=[rows, per(c, c), per(c, hd), rows, per(hd, hd), per(hd, 2 * c),
                  pl.BlockSpec((grp * c, DN_HEADS), lambda n: (rev(n), 0))],
        out_specs=[rows, rows, pl.BlockSpec((grp, DN_HEADS, LANES), lambda n: (rev(n), 0, 0))],
        out_shape=[jax.ShapeDtypeStruct((t, w), BF16), jax.ShapeDtypeStruct((t, w), F32),
                   jax.ShapeDtypeStruct((nch, DN_HEADS, LANES), F32)],
        scratch_shapes=[pltpu.VMEM((DN_HEADS, hd, hd), F32)], compiler_params=_params(("arbitrary",)),
    )(do, qkt, kr, vn, states, wqt, gc)


def _dn_local_bwd(q, k, v, gc, gcr, beta, invt, u, wq, vn, states, do, dvn, dkr, d_el, *, name):
    t, w = q.shape
    hd = w // DN_HEADS
    c = DN_CHUNK
    nch = t // c

    def body(q_ref, k_ref, v_ref, gc_ref, gcr_ref, b_ref, invt_ref, u_ref, wq_ref, vn_ref, s_ref, do_ref, dvn_ref,
             dkr_ref, del_ref, dq_ref, dk_ref, dv_ref, dgc_ref, dgr_ref, db_ref):
        heads = range(DN_HEADS)
        tms = [_dn_head_terms(q_ref, k_ref, gc_ref, gcr_ref, b_ref, h, hd) for h in heads]
        dobs = [do_ref[:, m['sl']].astype(BF16) for m in tms]
        tss = [_dot(jnp.concatenate([dob, -dvn_ref[:, m['sl']]], axis=0), s_ref[0, h].astype(BF16), 1, 1)
               for h, m, dob in zip(heads, tms, dobs)]
        d_qks = [_dot(dob, vn_ref[:, m['sl']], 1, 1) for m, dob in zip(tms, dobs)]
        d_rhss = [_dot(invt_ref[0, h], jnp.concatenate([dvn_ref[:, m['sl']], ts[c:].astype(BF16)], axis=1), 1, 0)
                  for h, m, ts in zip(heads, tms, tss)]
        d_as = [-_dot(d_rhs.astype(BF16),
                      jnp.concatenate([u_ref[:, m['sl']].astype(BF16), wq_ref[0, h, :c, :]], axis=1), 1, 1)
                for h, m, d_rhs in zip(heads, tms, d_rhss)]
        for h in heads:
            m, d_qe, d_qk, d_rhs, d_a = tms[h], tss[h][:c], d_qks[h], d_rhss[h], d_as[h]
            sl, qh, kh, bh, e, r, decay = m['sl'], m['q'], m['k'], m['b'], m['e'], m['r'], m['decay']
            vh = v_ref[:, sl]
            d_ru, d_rw = d_rhs[:, :hd], d_rhs[:, hd:]
            dv_ref[:, sl] = d_ru * bh
            d_e = jnp.sum(d_rw * m['kb'], axis=1, keepdims=True) + jnp.sum(d_qe * qh, axis=1, keepdims=True)
            d_n = jnp.where(m['row'] > m['col'], d_a, 0.0)
            d_m, d_p = (d_n * decay).astype(BF16), (d_qk * decay).astype(BF16)
            tk = _dot(jnp.concatenate([d_m, d_p], axis=0), kh.astype(BF16), 1, 0)
            d_kb = d_rw * e + tk[:c]
            dq_ref[:, sl] = tk[c:] + d_qe * e
            dk = _dot(d_m, m['kb'].astype(BF16), 0, 0) + _dot(d_p, qh.astype(BF16), 0, 0)
            dd = (d_n * m['kkt'] + d_qk * m['qkt']) * decay
            d_kr = dkr_ref[:, sl]
            d_r = jnp.sum(d_kr * kh, axis=1, keepdims=True)
            d_gl = del_ref[0, h:h + 1, 0:1] * m['el'] + jnp.sum(d_r * r, axis=0, keepdims=True)
            last = lax.broadcasted_iota(jnp.int32, (c, 1), 0) == c - 1
            dgc_ref[:, h:h + 1] = jnp.sum(dd, axis=1, keepdims=True) + d_e * e - d_r * r + jnp.where(last, d_gl, 0.0)
            dgr_ref[0, h:h + 1, :] = -jnp.sum(dd, axis=0, keepdims=True)
            dk_ref[:, sl] = dk + d_kr * r + d_kb * bh
            db_ref[:, h:h + 1] = jnp.sum(d_ru * vh, axis=1, keepdims=True) + jnp.sum(d_kb * kh, axis=1, keepdims=True)

    rows = pl.BlockSpec((c, w), lambda n: (n, 0))
    cols = pl.BlockSpec((c, DN_HEADS), lambda n: (n, 0))
    rowg = pl.BlockSpec((1, DN_HEADS, c), lambda n: (n, 0, 0))
    per = lambda a, b: pl.BlockSpec((1, DN_HEADS, a, b), lambda n: (n, 0, 0, 0))
    return pl.pallas_call(
        body, name=name, grid=(nch,),
        in_specs=[rows, rows, rows, cols, rowg, cols, per(c, c), rows, per(2 * c, hd), rows, per(hd, hd), rows, rows,
                  rows, pl.BlockSpec((1, DN_HEADS, LANES), lambda n: (n, 0, 0))],
        out_specs=[rows, rows, rows, cols, rowg, cols],
        out_shape=[jax.ShapeDtypeStruct((t, w), F32)] * 3
        + [jax.ShapeDtypeStruct((t, DN_HEADS), F32), jax.ShapeDtypeStruct((nch, DN_HEADS, c), F32),
           jax.ShapeDtypeStruct((t, DN_HEADS), F32)],
        compiler_params=_params(("parallel",)),
    )(q, k, v, gc, gcr, beta, invt, u, wq, vn, states, do, dvn, dkr, d_el)


def _other_chips():
    x, y = lax.axis_index("x"), lax.axis_index("y")
    return [(1 - x, y), (x, 1 - y), (1 - x, 1 - y)]


def _gather(src, *, name):
    def body(src_ref, out_ref, send_sems, recv_sems, local_sem):
        x, y, c = lax.axis_index("x"), lax.axis_index("y"), lax.axis_index("c")
        me, sibling, chips = (x, y, c), (x, y, 1 - c), _other_chips()
        slot = lambda px, py, pc: out_ref.at[4 * px + 2 * py + pc]

        def copy(k, block, to, own=False):
            return pltpu.make_async_remote_copy(
                src_ref=src_ref if own else slot(*block), dst_ref=slot(*block), send_sem=send_sems.at[k],
                recv_sem=recv_sems.at[k], device_id=to, device_id_type=MESH)

        local = pltpu.make_async_copy(src_ref, slot(*me), local_sem)
        local.start()
        first = [copy(0, me, sibling, own=True)] + [copy(1 + j, me, (*chip, c), own=True) for j, chip in enumerate(chips)]
        for cp in first:
            cp.start()
        passed = [copy(4 + j, (*chip, c), sibling) for j, chip in enumerate(chips)]
        for j, chip in enumerate(chips):
            copy(1 + j, (*chip, c), me).wait_recv()
            passed[j].start()
        copy(0, sibling, me).wait_recv()
        for j, chip in enumerate(chips):
            copy(4 + j, (*chip, 1 - c), me).wait_recv()
        for cp in first + passed:
            cp.wait_send()
        local.wait()

    return pl.pallas_call(
        body, name=name, out_shape=jax.ShapeDtypeStruct((N_DEV,) + src.shape, src.dtype),
        in_specs=[pl.BlockSpec(memory_space=pl.ANY)], out_specs=pl.BlockSpec(memory_space=pl.ANY),
        scratch_shapes=[pltpu.SemaphoreType.DMA((N_DEV - 1,)), pltpu.SemaphoreType.DMA((N_DEV - 1,)),
                        pltpu.SemaphoreType.DMA(())],
    )(src)


def _swap_with_sibling(src, *, name):
    def body(src_ref, out_ref, send_sem, recv_sem):
        x, y, c = lax.axis_index("x"), lax.axis_index("y"), lax.axis_index("c")
        cp = pltpu.make_async_remote_copy(src_ref=src_ref.at[1 - c], dst_ref=out_ref, send_sem=send_sem,
                                          recv_sem=recv_sem, device_id=(x, y, 1 - c), device_id_type=MESH)
        cp.start()
        cp.wait()

    return pl.pallas_call(
        body, name=name, out_shape=jax.ShapeDtypeStruct(src.shape[1:], src.dtype),
        in_specs=[pl.BlockSpec(memory_space=pl.ANY)], out_specs=pl.BlockSpec(memory_space=pl.ANY),
        scratch_shapes=[pltpu.SemaphoreType.DMA(()), pltpu.SemaphoreType.DMA(())],
    )(src)


def _add_own_half(src, got, *, name):
    _, n, r, c = src.shape
    tr = _tile(r, 512, 16)

    def body(s0_ref, s1_ref, got_ref, o_ref):
        own = jnp.where(lax.axis_index("c") == 0, s0_ref[0, 0], s1_ref[0, 0])
        o_ref[0] = (own + got_ref[0]).astype(BF16)

    half = lambda h: pl.BlockSpec((1, 1, tr, c), lambda j, i: (h, j, i, 0))
    blk = pl.BlockSpec((1, tr, c), lambda j, i: (j, i, 0))
    return pl.pallas_call(
        body, name=name, grid=(n, r // tr), in_specs=[half(0), half(1), blk], out_specs=blk,
        out_shape=jax.ShapeDtypeStruct((n, r, c), BF16), compiler_params=_params(("parallel", "parallel")),
    )(src, src, got)


def _swap_between_chips(src, *, name):
    def body(src_ref, out_ref, send_sems, recv_sems, local_sem):
        x, y, c = lax.axis_index("x"), lax.axis_index("y"), lax.axis_index("c")
        mine = 2 * x + y
        local = pltpu.make_async_copy(src_ref.at[mine], out_ref.at[mine], local_sem)
        local.start()
        copies = [pltpu.make_async_remote_copy(
            src_ref=src_ref.at[2 * px + py], dst_ref=out_ref.at[mine], send_sem=send_sems.at[j],
            recv_sem=recv_sems.at[j], device_id=(px, py, c), device_id_type=MESH) for j, (px, py) in enumerate(_other_chips())]
        for cp in copies:
            cp.start()
        for j, (px, py) in enumerate(_other_chips()):
            pltpu.make_async_remote_copy(
                src_ref=src_ref.at[mine], dst_ref=out_ref.at[2 * px + py], send_sem=send_sems.at[j],
                recv_sem=recv_sems.at[j], device_id=(px, py, c), device_id_type=MESH).wait_recv()
        for cp in copies:
            cp.wait_send()
        local.wait()

    return pl.pallas_call(
        body, name=name, out_shape=jax.ShapeDtypeStruct(src.shape, src.dtype),
        in_specs=[pl.BlockSpec(memory_space=pl.ANY)], out_specs=pl.BlockSpec(memory_space=pl.ANY),
        scratch_shapes=[pltpu.SemaphoreType.DMA((3,)), pltpu.SemaphoreType.DMA((3,)), pltpu.SemaphoreType.DMA(())],
    )(src)


def _sum_adamw(parts, w, m, v, *, name):
    n_parts, r, c = parts.shape
    tr = _tile(r, 256, 8)
    c1, c2 = 1.0 - ADAM_B1 ** ADAM_STEP, 1.0 - ADAM_B2 ** ADAM_STEP

    def body(p_ref, w_ref, m_ref, v_ref, g_ref, d_ref, nm_ref, nv_ref):
        g = p_ref[0].astype(F32)
        for s in range(1, n_parts):
            g = g + p_ref[s].astype(F32)
        nm = ADAM_B1 * m_ref[...] + (1.0 - ADAM_B1) * g
        nv = ADAM_B2 * v_ref[...] + (1.0 - ADAM_B2) * (g * g)
        g_ref[...] = g
        nm_ref[...] = nm
        nv_ref[...] = nv
        d_ref[...] = -ADAM_LR * ((nm / c1) / (jnp.sqrt(nv / c2) + ADAM_EPS) + ADAM_WD * w_ref[...])

    blk = pl.BlockSpec((tr, c), lambda i: (i, 0))
    return pl.pallas_call(
        body, name=name, grid=(r // tr,), in_specs=[pl.BlockSpec((n_parts, tr, c), lambda i: (0, i, 0)), blk, blk, blk],
        out_specs=[blk] * 4, out_shape=[jax.ShapeDtypeStruct((r, c), F32)] * 4, compiler_params=_params(("parallel",)),
    )(parts, w, m, v)


def _pack_rows(n):
    return -(-n // (PACK_COLS * PACK_ROWS)) * PACK_ROWS


def _pack(blocks):
    parts, spans, at = [], [], 0
    for blk, n_lead in blocks:
        lead = blk.shape[:n_lead]
        n = math.prod(blk.shape[n_lead:])
        rows = _pack_rows(n)
        flat = blk.reshape(lead + (n,))
        flat = jnp.pad(flat, [(0, 0)] * n_lead + [(0, rows * PACK_COLS - n)])
        parts.append(flat.reshape(lead + (rows, PACK_COLS)))
        spans.append((at, rows))
        at += rows
    return jnp.concatenate(parts, axis=-2), spans


def _unpack(buf, span, shape):
    at, rows = span
    lead = buf.shape[:-2]
    flat = lax.slice_in_dim(buf, at, at + rows, axis=buf.ndim - 2).reshape(lead + (rows * PACK_COLS,))
    return lax.slice_in_dim(flat, 0, math.prod(shape), axis=len(lead)).reshape(lead + tuple(shape))


def _join_shards(g, axis):
    g = jnp.moveaxis(g, 0, axis)
    return g.reshape(g.shape[:axis] + (g.shape[axis] * g.shape[axis + 1],) + g.shape[axis + 2:])


def _split_shards(full, axis):
    s = full.shape
    g = full.reshape(s[:axis] + (N_DEV, s[axis] // N_DEV) + s[axis + 1:])
    return jnp.moveaxis(g, axis, 0)


def _residual_out(a, w, x, scale, next_gain, name, **tiles):
    out = _mm(a, w, name=name, res=x, scale=scale, norm=next_gain, **tiles)
    return out if next_gain is not None else (out, None)


def _ffn_fwd(x, h, w_in, w_out, tag, next_gain):
    gate, up, a = _mm_swiglu(h, w_in, name=tag + "_in", tm=512, tn=1408)
    xo, hn = _residual_out(a, w_out, x, 0.5, next_gain, tag + "_out", tk=2816)
    return xo, hn, (x, h, gate, up, a)


def _ffn_bwd(saved, g, w_in, w_out, dxo, tag):
    x, h, gate, up, a = saved
    f = w_out.shape[0]
    d_w_out = _mm(a, dxo, name=tag + "_dwout", ta=True, scale=0.5, tm=1408, tk=2048)
    dgate, dup = _mm_dswiglu(dxo, w_out, gate, up, name=tag + "_da", scale=0.5, tm=512, tn=1408)
    d_w_in = jnp.concatenate([_mm(dgate, h, name=tag + "_dwin_g", ta=True, tm=1408, tk=2048),
                              _mm(dup, h, name=tag + "_dwin_u", ta=True, tm=1408, tk=2048)], axis=0)
    dh = _mm(dgate, w_in[:f], name=tag + "_dh_g", tk=2816)
    dx, dg = _mm(dup, w_in[f:], name=tag + "_dh_u", tm=512, tk=2816, res=dh, rms_bwd=(x, g, dxo))
    return dx, dg, d_w_in, d_w_out


def _block_diag(w):
    n, j, k = w.shape
    return (w[:, :, None, :] * jnp.eye(n, dtype=w.dtype)[:, None, :, None]).reshape(n * j, n * k)


def _diag_blocks(dense, n):
    j, k = dense.shape[0] // n, dense.shape[1] // n
    return jnp.stack([dense[i * j:(i + 1) * j, i * k:(i + 1) * k] for i in range(n)], axis=0)


def _attn_lru_fwd(x, h, p, tag, next_gain):
    aw = ATTN_HEADS * 64
    proj = _mm(h, p['ab_w_in'], name=tag + "_in", tb=True, tn=1280)
    qkv = proj[:, :3 * aw].astype(BF16)
    views = {dil: qkv.reshape(qkv.shape[0] // dil, dil * 3 * aw) for _, dil in DILATED_PATTERNS}
    outs, lses = [], []
    for window, dil in DILATED_PATTERNS:
        assert window // dil == ATTN_BLOCK
        o, l = _dattn_fwd(views[dil], dil, name=f"{tag}_attn{dil}")
        outs.append(o)
        lses.append(l)
    wa, wx = _block_diag(p['lru_w_a']).astype(BF16), _block_diag(p['lru_w_x']).astype(BF16)
    sp, sp_vjp = jax.vjp(lambda lam: LRU_C * jax.nn.softplus(-lam), p['lru_lambda'])
    a, b, xc = _lru_gates_fwd(proj, 3, p['lru_conv_w'], p['lru_conv_b'], wa, p['lru_b_a'], wx, p['lru_b_x'], sp,
                              name=tag + "_gates")
    hs = _scan(a, b, name=tag + "_scan")
    cat, attn, lse_all = _mix_join_fwd(outs, lses, hs, proj, 4, name=tag + "_join")
    xo, hn = _residual_out(cat, p['ab_w_out'], x, 1.0, next_gain, tag + "_out")
    return xo, hn, (x, h, proj, views, attn, lse_all, a, hs, xc, wa, wx, sp, sp_vjp, cat)


def _attn_lru_bwd(saved, p, dxo, tag):
    x, h, proj, views, attn, lse_all, a, hs, xc, wa, wx, sp, sp_vjp, cat = saved
    aw = attn.shape[1]
    g = {'ab_w_out': _mm(cat, dxo, name=tag + "_dwout", ta=True)}
    dcat = _mm(dxo, p['ab_w_out'], name=tag + "_dcat", tb=True)
    stats, dhs, dgr = _mix_join_bwd(dcat, attn, lse_all, hs, proj, 4, name=tag + "_djoin")
    a_next = jnp.concatenate([a[1:], jnp.zeros_like(a[:1])], axis=0)
    dtot = _scan(a_next, dhs, name=tag + "_dscan", reverse=True)
    h_prev = jnp.concatenate([jnp.zeros_like(hs[:1]), hs[:-1]], axis=0)
    dxc, dwa, dwx, vecs = _lru_gates_bwd(xc, dtot, h_prev, wa, p['lru_b_a'], wx, p['lru_b_x'], sp, name=tag + "_dgates")
    dxr, g['lru_conv_w'] = _conv_bwd(dxc, proj, 3, p['lru_conv_w'], name=tag + "_dconv")
    g['lru_b_a'], g['lru_b_x'], g['lru_conv_b'] = vecs[0], vecs[1], vecs[3]
    g['lru_lambda'], = sp_vjp(vecs[2])
    g['lru_w_a'], g['lru_w_x'] = _diag_blocks(dwa, LRU_BLOCKS), _diag_blocks(dwx, LRU_BLOCKS)
    dattn = dcat[:, :aw]
    dqkv = sum(_dattn_bwd(views[dil], dattn, stats, dil, name=f"{tag}_dattn{dil}").astype(F32)
               for _, dil in DILATED_PATTERNS)
    dproj = jnp.concatenate([dqkv.astype(BF16), dxr, dgr], axis=-1)
    g['ab_w_in'] = _mm(dproj, h, name=tag + "_dwin", ta=True, tm=1280)
    dx, g['mix_norm'] = _mm(dproj, p['ab_w_in'], name=tag + "_dh", tm=512, tk=2560, rms_bwd=(x, p['mix_norm'], dxo))
    return dx, g


def _dn_decay(a, b, a_log, dt_bias):
    t = a.shape[0]
    g = -jnp.exp(a_log) * jax.nn.softplus(a + dt_bias)
    gc = jnp.cumsum(g.reshape(t // DN_CHUNK, DN_CHUNK, DN_HEADS), axis=1)
    return gc.reshape(t, DN_HEADS), jnp.swapaxes(gc, 1, 2), jax.nn.sigmoid(b)


def _dn_in_width(w):
    return -(-(4 * w + 2 * DN_HEADS) // LANES) * LANES


def _deltanet_fwd(x, h, p, tag, next_gain):
    w = p['dn_w_out'].shape[0]
    proj = _mm(h, p['dn_w_in'], name=tag + "_in", tb=True, tn=1408)
    q, k, v, c = _dn_prep_fwd(proj, p['dn_conv_w'], name=tag + "_prep")
    a, b = proj[:, 4 * w:4 * w + DN_HEADS], proj[:, 4 * w + DN_HEADS:4 * w + 2 * DN_HEADS]
    (gc, gcr, beta), decay_vjp = jax.vjp(_dn_decay, a, b, p['dn_a_log'], p['dn_dt_bias'])
    prep = (q, k, v, gc, gcr, beta)
    u, wq, wqt, kr, krt, qk, qkt, invt = _dn_local_fwd(*prep, name=tag + "_local")
    o, vn, states = _dn_state_fwd(u, wq, krt, qk, gc, name=tag + "_state")
    og = _dn_gate_fwd(o, proj, 3, p['dn_o_norm'], name=tag + "_gate")
    xo, hn = _residual_out(og, p['dn_w_out'], x, 1.0, next_gain, tag + "_out")
    return xo, hn, (x, h, proj, c, o, prep, (u, wq, wqt, kr, qkt, invt, vn, states), decay_vjp, og)


def _deltanet_bwd(saved, p, dxo, tag):
    x, h, proj, c, o, prep, (u, wq, wqt, kr, qkt, invt, vn, states), decay_vjp, og = saved
    g = {'dn_w_out': _mm(og, dxo, name=tag + "_dwout", ta=True)}
    dog = _mm(dxo, p['dn_w_out'], name=tag + "_dog", tb=True)
    do, dz, g['dn_o_norm'] = _dn_gate_bwd(o, proj, 3, p['dn_o_norm'], dog, name=tag + "_dgate")
    dvn, dkr, d_el = _dn_state_bwd(do, qkt, kr, vn, states, wqt, prep[3], name=tag + "_dstate")
    dq, dk, dv, dgc, dgr, dbeta = _dn_local_bwd(*prep, invt, u, wq, vn, states, do, dvn, dkr, d_el, name=tag + "_dlocal")
    dc = _dn_prep_bwd(c, dq, dk, dv, name=tag + "_dprep")
    dqkv, g['dn_conv_w'] = _conv_bwd(dc, proj, 0, p['dn_conv_w'], name=tag + "_dconv")
    da, db, g['dn_a_log'], g['dn_dt_bias'] = decay_vjp((dgc, dgr, dbeta))
    t = x.shape[0]
    pad = jnp.zeros((t, p['dn_w_in'].shape[0] - dqkv.shape[1] - dz.shape[1] - 2 * DN_HEADS), BF16)
    groups = [("qkv", dqkv), ("z", dz), ("ab", jnp.concatenate([da.astype(BF16), db.astype(BF16), pad], axis=-1))]
    parts, dh, lo = [], None, 0
    for label, piece in groups:
        hi = lo + piece.shape[1]
        parts.append(_mm(piece, h, name=f"{tag}_dwin_{label}", ta=True, tm=1536, tk=2048))
        last = dict(tm=512, rms_bwd=(x, p['mix_norm'], dxo)) if label == groups[-1][0] else {}
        dh = _mm(piece, p['dn_w_in'][lo:hi], name=f"{tag}_dh_{label}", tk=1536, res=dh, **last)
        lo = hi
    g['dn_w_in'] = jnp.concatenate(parts, axis=0)
    dx, g['mix_norm'] = dh
    return dx, g


def _xattn_block_fwd(x, h, mem, p, tag, next_gain):
    w = x.shape[1]
    mh = _rms_fwd(mem, p['xa_mem_norm'], name=tag + "_mnorm", out_dtype=BF16)
    q = _mm(h, p['xa_wq'], name=tag + "_q")
    kv = _mm(mh, p['xa_wkv'], name=tag + "_kv", tb=True)
    k, v = kv[:, :w], kv[:, w:]
    o = _xattn_fwd(q, k, v, name=tag + "_attn").astype(BF16)
    xo, hn = _residual_out(o, p['xa_wo'], x, 1.0, next_gain, tag + "_out")
    return xo, hn, (x, h, mh, q, k, v, o)


def _xattn_block_bwd(saved, mem, p, dxo, tag):
    x, h, mh, q, k, v, o = saved
    g = {'xa_wo': _mm(o, dxo, name=tag + "_dwo", ta=True)}
    do = _mm(dxo, p['xa_wo'], name=tag + "_do", tb=True)
    dq, dk, dv = _xattn_bwd(q, k, v, do, name=tag + "_dattn")
    dq = dq.astype(BF16)
    dkv = jnp.concatenate([dk, dv], axis=-1).astype(BF16)
    g['xa_wq'] = _mm(h, dq, name=tag + "_dwq", ta=True)
    g['xa_wkv'] = _mm(dkv, mh, name=tag + "_dwkv", ta=True)
    dmh = _mm(dkv, p['xa_wkv'], name=tag + "_dmh")
    _, g['xa_mem_norm'] = _rms_bwd(mem, p['xa_mem_norm'], dmh, None, name=tag + "_dmnorm")
    dx, g['xa_norm'] = _mm(dq, p['xa_wq'], name=tag + "_dh", tb=True, tm=512, rms_bwd=(x, p['xa_norm'], dxo))
    return dx, g


def _layer_params(full, layer):
    p = {n: full[n][layer] for n in ('ffn1_norm', 'ffn1_w_in', 'ffn1_w_out', 'mix_norm', 'xa_norm', 'xa_mem_norm',
                                     'xa_wq', 'xa_wkv', 'xa_wo', 'ffn2_norm', 'ffn2_w_in', 'ffn2_w_out')}
    mixer = ('ab_w_in', 'lru_conv_w', 'lru_conv_b', 'lru_w_a', 'lru_b_a', 'lru_w_x', 'lru_b_x', 'lru_lambda', 'ab_w_out') \
        if layer % 2 == 0 else ('dn_w_in', 'dn_conv_w', 'dn_a_log', 'dn_dt_bias', 'dn_o_norm', 'dn_w_out')
    p.update({n: full[n][layer // 2] for n in mixer})
    return p


def _step(x, mem, target, full, depth):
    saved = []
    params = [_layer_params(full, layer) for layer in range(depth)]
    h = _rms_fwd(x, params[0]['ffn1_norm'], name="l0_ffn1_norm", out_dtype=BF16)
    for layer, p in enumerate(params):
        tag = f"l{layer}"
        after = params[layer + 1]['ffn1_norm'] if layer + 1 < depth else None
        x, h, s1 = _ffn_fwd(x, h, p['ffn1_w_in'], p['ffn1_w_out'], tag + "_ffn1", p['mix_norm'])
        x, h, s2 = (_attn_lru_fwd if layer % 2 == 0 else _deltanet_fwd)(x, h, p, tag + "_mix", p['xa_norm'])
        x, h, s3 = _xattn_block_fwd(x, h, mem, p, tag + "_xa", p['ffn2_norm'])
        x, h, s4 = _ffn_fwd(x, h, p['ffn2_w_in'], p['ffn2_w_out'], tag + "_ffn2", after)
        saved.append((p, s1, s2, s3, s4))
    sq, dx, d_final = _final_loss(x, full['final_norm'], target, name="final_loss")
    per_layer = []
    for layer in reversed(range(depth)):
        p, s1, s2, s3, s4 = saved[layer]
        tag = f"l{layer}"
        g = {}
        dx, g['ffn2_norm'], g['ffn2_w_in'], g['ffn2_w_out'] = _ffn_bwd(s4, p['ffn2_norm'], p['ffn2_w_in'], p['ffn2_w_out'], dx, tag + "_ffn2")
        dx, gx = _xattn_block_bwd(s3, mem, p, dx, tag + "_xa")
        dx, gm = (_attn_lru_bwd if layer % 2 == 0 else _deltanet_bwd)(s2, p, dx, tag + "_mix")
        dx, g['ffn1_norm'], g['ffn1_w_in'], g['ffn1_w_out'] = _ffn_bwd(s1, p['ffn1_norm'], p['ffn1_w_in'], p['ffn1_w_out'], dx, tag + "_ffn1")
        g.update(gx)
        g.update(gm)
        per_layer.insert(0, g)
    grads = {n: [g[n] for g in per_layer if n in g] for n in WEIGHTS[:-1]}
    grads['final_norm'] = d_final
    return sq, dx, grads


def kernel(x, mem, ffn1_norm, ffn1_w_in, ffn1_w_out, mix_norm, xa_norm, xa_mem_norm, xa_wq, xa_wkv, xa_wo, ffn2_norm, ffn2_w_in, ffn2_w_out, ab_w_in, lru_conv_w, lru_conv_b, lru_w_a, lru_b_a, lru_w_x, lru_b_x, lru_lambda, ab_w_out, dn_w_in, dn_conv_w, dn_a_log, dn_dt_bias, dn_o_norm, dn_w_out, final_norm, loss_target, m_ffn1_norm, m_ffn1_w_in, m_ffn1_w_out, m_mix_norm, m_xa_norm, m_xa_mem_norm, m_xa_wq, m_xa_wkv, m_xa_wo, m_ffn2_norm, m_ffn2_w_in, m_ffn2_w_out, m_ab_w_in, m_lru_conv_w, m_lru_conv_b, m_lru_w_a, m_lru_b_a, m_lru_w_x, m_lru_b_x, m_lru_lambda, m_ab_w_out, m_dn_w_in, m_dn_conv_w, m_dn_a_log, m_dn_dt_bias, m_dn_o_norm, m_dn_w_out, m_final_norm, v_ffn1_norm, v_ffn1_w_in, v_ffn1_w_out, v_mix_norm, v_xa_norm, v_xa_mem_norm, v_xa_wq, v_xa_wkv, v_xa_wo, v_ffn2_norm, v_ffn2_w_in, v_ffn2_w_out, v_ab_w_in, v_lru_conv_w, v_lru_conv_b, v_lru_w_a, v_lru_b_a, v_lru_w_x, v_lru_b_x, v_lru_lambda, v_ab_w_out, v_dn_w_in, v_dn_conv_w, v_dn_a_log, v_dn_dt_bias, v_dn_o_norm, v_dn_w_out, v_final_norm):
    args = dict(locals())
    flip = lambda n, a: jnp.swapaxes(a, 1, 2) if n in TRANSPOSED else a
    local = {n: flip(n, args[n]) for n in WEIGHTS}
    depth = ffn1_norm.shape[0]
    matrices = [n for n in WEIGHTS if n in SHARD_AXIS and n not in GATHER_F32]

    def entries(get):
        return [(a, n, l) for n in WEIGHTS for l, a in
                (enumerate(get(n)) if n in matrices else [(None, get(n))])]

    mats = [e for e in entries(lambda n: local[n]) if e[1] in matrices]
    send16, spans16 = _pack([(a.astype(BF16), 0) for a, _, _ in mats])
    send32, spans32 = _pack([(local[n], 0) for n in GATHER_F32])
    got16 = _gather(send16, name="gather_matrices")
    got32 = _gather(send32, name="gather_filters")
    full = {n: ([None] * local[n].shape[0] if n in matrices else local[n]) for n in WEIGHTS}
    for (a, n, l), span in zip(mats, spans16):
        full[n][l] = _unpack(got16, span, a.shape).reshape(N_DEV * a.shape[0], a.shape[1])
    for n, span in zip(GATHER_F32, spans32):
        full[n] = _join_shards(_unpack(got32, span, local[n].shape), SHARD_AXIS[n])
    dn_rows = full['dn_w_in'][0].shape[0]
    full['dn_w_in'] = [jnp.pad(w, ((0, _dn_in_width(w.shape[1]) - dn_rows), (0, 0))) for w in full['dn_w_in']]

    sq, dx, grads = _step(x[0], mem[0], loss_target[0], full, depth)
    grads['dn_w_in'] = [g[:dn_rows] for g in grads['dn_w_in']]
    loss = lax.psum(0.5 * jnp.sum(sq) / x.shape[2], ("x", "y", "c"))

    by_core = lambda z: jnp.swapaxes(z.reshape((N_DEV // 2, 2) + z.shape[1:]), 0, 1)

    def contribution(n):
        if n in matrices:
            return [by_core(g.reshape((N_DEV, g.shape[0] // N_DEV, g.shape[1]))) for g in grads[n]]
        g = grads[n] if n == 'final_norm' else jnp.stack(grads[n], axis=0)
        return by_core(_split_shards(g, SHARD_AXIS[n]) if n in SHARD_AXIS else jnp.broadcast_to(g, (N_DEV,) + g.shape))

    send, spans = _pack([(a, 2) for a, _, _ in entries(contribution)])
    got = _swap_with_sibling(send, name="grads_to_sibling")
    chip_sum = _add_own_half(send, got, name="grads_chip_sum")
    parts = _swap_between_chips(chip_sum, name="grads_between_chips")
    state = [_pack([(a, 0) for a, _, _ in entries(lambda n: flip(n, args[pre + n]))])[0] for pre in ("", "m_", "v_")]
    outs = _sum_adamw(parts, *state, name="sum_adamw")
    result = []
    for o in outs:
        got_rows = {}
        for (a, n, l), span in zip(entries(lambda n: local[n]), spans):
            got_rows.setdefault(n, []).append(_unpack(o, span, a.shape))
        result += [flip(n, jnp.stack(got_rows[n], axis=0) if n in matrices else got_rows[n][0]) for n in WEIGHTS]
    return (loss, dx[None], *result)
```

```python
import math

import jax
import jax.numpy as jnp
from jax import lax
from jax.experimental import pallas as pl
from jax.experimental.pallas import tpu as pltpu

F32, BF16 = jnp.float32, jnp.bfloat16
MESH = pl.DeviceIdType.MESH
N_DEV = 8
V7X_VMEM_LIMIT = 56 << 20
LANES = 128
PACK_COLS = 1024
PACK_ROWS = 16
NEG = -1e30

NORM_EPS = 1e-6
CONV_K = 4
ATTN_HEADS = 8
DILATED_PATTERNS = ((128, 1), (512, 4), (2048, 16))
ATTN_BLOCK = 128
LRU_BLOCKS = 8
LRU_C = 8.0
DN_HEADS = 8
DN_CHUNK = 64
DN_STEP_CHUNKS = 4
XA_HEADS = 4
ADAM_LR, ADAM_B1, ADAM_B2, ADAM_EPS, ADAM_WD, ADAM_STEP = 0.001, 0.9, 0.999, 1e-08, 0.01, 10

WEIGHTS = ['ffn1_norm', 'ffn1_w_in', 'ffn1_w_out', 'mix_norm', 'xa_norm', 'xa_mem_norm', 'xa_wq', 'xa_wkv', 'xa_wo',
           'ffn2_norm', 'ffn2_w_in', 'ffn2_w_out', 'ab_w_in', 'lru_conv_w', 'lru_conv_b', 'lru_w_a', 'lru_b_a',
           'lru_w_x', 'lru_b_x', 'lru_lambda', 'ab_w_out', 'dn_w_in', 'dn_conv_w', 'dn_a_log', 'dn_dt_bias',
           'dn_o_norm', 'dn_w_out', 'final_norm']
SHARD_AXIS = {'ffn1_w_in': 2, 'ffn1_w_out': 1, 'xa_wq': 1, 'xa_wkv': 2, 'xa_wo': 1, 'ffn2_w_in': 2, 'ffn2_w_out': 1,
              'ab_w_in': 2, 'lru_conv_w': 2, 'ab_w_out': 1, 'dn_w_in': 2, 'dn_conv_w': 2, 'dn_w_out': 1}
GATHER_F32 = ('lru_conv_w', 'dn_conv_w')
TRANSPOSED = ('ffn1_w_in', 'xa_wkv', 'ffn2_w_in', 'ab_w_in', 'dn_w_in')


def _params(sem=None):
    return pltpu.CompilerParams(dimension_semantics=sem, vmem_limit_bytes=V7X_VMEM_LIMIT)


def _tile(n, pref, mult):
    best = None
    t = mult
    while t <= min(n, pref):
        if n % t == 0:
            best = t
        t += mult
    return n if best is None else best


def _dot(a, b, ca, cb, prec=None):
    return lax.dot_general(a, b, (((ca,), (cb,)), ((), ())), preferred_element_type=F32, precision=prec)


def _mm(a, b, *, name, ta=False, tb=False, out_dtype=F32, res=None, scale=1.0, tm=1024, tn=1024, tk=1024,
        norm=None, rms_bwd=None):
    m, kdim = (a.shape[1], a.shape[0]) if ta else a.shape
    n = b.shape[0] if tb else b.shape[1]
    assert (b.shape[1] if tb else b.shape[0]) == kdim
    tm = _tile(m, tm, LANES if ta else 16)
    tn = _tile(n, tn, LANES)
    tk = _tile(kdim, tk, LANES)
    nk = kdim // tk
    a_spec = pl.BlockSpec((tk, tm), lambda i, j, k: (k, i)) if ta else pl.BlockSpec((tm, tk), lambda i, j, k: (i, k))
    b_spec = pl.BlockSpec((tn, tk), lambda i, j, k: (j, k)) if tb else pl.BlockSpec((tk, tn), lambda i, j, k: (k, j))
    o_spec = pl.BlockSpec((tm, tn), lambda i, j, k: (i, j))
    vec = pl.BlockSpec((1, tn), lambda i, j, k: (0, j))
    ca, cb = (0 if ta else 1), (1 if tb else 0)
    has_res = res is not None
    assert (norm is None and rms_bwd is None) or tn == n

    ins, specs = [a, b], [a_spec, b_spec]
    if has_res:
        ins.append(res)
        specs.append(o_spec)
    out_specs, out_shape = [o_spec], [jax.ShapeDtypeStruct((m, n), out_dtype)]
    if norm is not None:
        ins.append(norm.reshape(1, n))
        specs.append(vec)
        out_specs.append(o_spec)
        out_shape.append(jax.ShapeDtypeStruct((m, n), BF16))
    if rms_bwd is not None:
        ins += [rms_bwd[0], rms_bwd[1].reshape(1, n), rms_bwd[2]]
        specs += [o_spec, vec, o_spec]
        out_specs.append(vec)
        out_shape.append(jax.ShapeDtypeStruct((1, n), F32))
    n_in, n_out = len(ins), len(out_specs)

    def finish(acc, extra, outs, first_rows):
        y = acc if scale == 1.0 else acc * scale
        if has_res:
            y = y + extra[0][...]
        tail = extra[has_res:]
        if norm is not None:
            rs = lax.rsqrt(jnp.mean(y * y, axis=-1, keepdims=True) + NORM_EPS)
            outs[1][...] = (y * rs * tail[0][...]).astype(BF16)
        if rms_bwd is not None:
            x_ref, g_ref, dres_ref = tail
            xv = x_ref[...]
            rs = lax.rsqrt(jnp.mean(xv * xv, axis=-1, keepdims=True) + NORM_EPS)
            xh = xv * rs
            dgh = y * g_ref[...]
            part = jnp.sum(y * xh, axis=0, keepdims=True)
            y = dres_ref[...] + rs * (dgh - xh * jnp.mean(dgh * xh, axis=-1, keepdims=True))

            @pl.when(first_rows)
            def _():
                outs[1][...] = part

            @pl.when(jnp.logical_not(first_rows))
            def _():
                outs[1][...] += part

        outs[0][...] = y.astype(out_dtype)

    def body(*refs):
        a_ref, b_ref = refs[0], refs[1]
        extra, outs = refs[2:n_in], refs[n_in:n_in + n_out]
        p = _dot(a_ref[...].astype(BF16), b_ref[...].astype(BF16), ca, cb)
        first_rows = pl.program_id(0) == 0
        if nk == 1:
            finish(p, extra, outs, first_rows)
            return
        acc = refs[n_in + n_out]
        k = pl.program_id(2)

        @pl.when(k == 0)
        def _():
            acc[...] = p

        @pl.when(k > 0)
        def _():
            acc[...] += p

        @pl.when(k == nk - 1)
        def _():
            finish(acc[...], extra, outs, first_rows)

    out = pl.pallas_call(
        body, name=name, grid=(m // tm, n // tn, nk), in_specs=specs, out_specs=out_specs, out_shape=out_shape,
        scratch_shapes=[] if nk == 1 else [pltpu.VMEM((tm, tn), F32)],
        compiler_params=_params(("parallel" if rms_bwd is None else "arbitrary", "parallel", "arbitrary")),
    )(*ins)
    if rms_bwd is not None:
        return out[0], out[1].reshape(n)
    return out[0] if n_out == 1 else tuple(out)


def _rms_fwd(x, g, *, name, out_dtype):
    r, d = x.shape
    tr = _tile(r, 512, 16)

    def body(x_ref, g_ref, o_ref):
        xv = x_ref[...]
        rs = lax.rsqrt(jnp.mean(xv * xv, axis=-1, keepdims=True) + NORM_EPS)
        o_ref[...] = (xv * rs * g_ref[...]).astype(out_dtype)

    return pl.pallas_call(
        body, name=name, grid=(r // tr,),
        in_specs=[pl.BlockSpec((tr, d), lambda i: (i, 0)), pl.BlockSpec((1, d), lambda i: (0, 0))],
        out_specs=pl.BlockSpec((tr, d), lambda i: (i, 0)),
        out_shape=jax.ShapeDtypeStruct((r, d), out_dtype), compiler_params=_params(("parallel",)),
    )(x, g.reshape(1, d))


def _rms_bwd(x, g, dh, dres, *, name):
    r, d = x.shape
    tr = _tile(r, 512, 8)
    has_res = dres is not None

    def body(*refs):
        x_ref, g_ref, dh_ref = refs[:3]
        r_ref = refs[3] if has_res else None
        dx_ref, dg_ref = refs[3 + has_res], refs[4 + has_res]
        xv = x_ref[...]
        rs = lax.rsqrt(jnp.mean(xv * xv, axis=-1, keepdims=True) + NORM_EPS)
        xh = xv * rs
        dhv = dh_ref[...]
        dgh = dhv * g_ref[...]
        dx = rs * (dgh - xh * jnp.mean(dgh * xh, axis=-1, keepdims=True))
        if has_res:
            dx = dx + r_ref[...]
        dx_ref[...] = dx
        part = jnp.sum(dhv * xh, axis=0, keepdims=True)

        @pl.when(pl.program_id(0) == 0)
        def _():
            dg_ref[...] = part

        @pl.when(pl.program_id(0) > 0)
        def _():
            dg_ref[...] += part

    row = pl.BlockSpec((tr, d), lambda i: (i, 0))
    vec = pl.BlockSpec((1, d), lambda i: (0, 0))
    ins, specs = [x, g.reshape(1, d), dh], [row, vec, row]
    if has_res:
        ins.append(dres)
        specs.append(row)
    dx, dg = pl.pallas_call(
        body, name=name, grid=(r // tr,), in_specs=specs, out_specs=[row, vec],
        out_shape=[jax.ShapeDtypeStruct((r, d), F32), jax.ShapeDtypeStruct((1, d), F32)],
        compiler_params=_params(("arbitrary",)),
    )(*ins)
    return dx, dg.reshape(d)


def _mm_swiglu(h, w_in, *, name, tm=1024, tn=256):
    t, d = h.shape
    f = w_in.shape[0] // 2
    tm, tn = _tile(t, tm, 16), _tile(f, tn, LANES)
    nj = f // tn

    def body(h_ref, wg_ref, wu_ref, g_ref, u_ref, a_ref):
        hv = h_ref[...]
        gate, up = _dot(hv, wg_ref[...], 1, 1), _dot(hv, wu_ref[...], 1, 1)
        g_ref[...] = gate.astype(BF16)
        u_ref[...] = up.astype(BF16)
        a_ref[...] = (gate * jax.nn.sigmoid(gate) * up).astype(BF16)

    out = pl.BlockSpec((tm, tn), lambda j, i: (i, j))
    return pl.pallas_call(
        body, name=name, grid=(nj, t // tm),
        in_specs=[pl.BlockSpec((tm, d), lambda j, i: (i, 0)), pl.BlockSpec((tn, d), lambda j, i: (j, 0)),
                  pl.BlockSpec((tn, d), lambda j, i: (j + nj, 0))],
        out_specs=[out, out, out], out_shape=[jax.ShapeDtypeStruct((t, f), BF16)] * 3,
        compiler_params=_params(("parallel", "parallel")),
    )(h, w_in, w_in)


def _mm_dswiglu(dy, w_out, gate, up, *, name, scale, tm=1024, tn=256):
    t, d = dy.shape
    f = w_out.shape[0]
    tm, tn = _tile(t, tm, 16), _tile(f, tn, LANES)

    def body(dy_ref, w_ref, g_ref, u_ref, dg_ref, du_ref):
        da = _dot(dy_ref[...].astype(BF16), w_ref[...], 1, 1) * scale
        gv = g_ref[...].astype(F32)
        s = jax.nn.sigmoid(gv)
        dg_ref[...] = (da * u_ref[...].astype(F32) * (s * (1.0 + gv * (1.0 - s)))).astype(BF16)
        du_ref[...] = (da * gv * s).astype(BF16)

    blk = pl.BlockSpec((tm, tn), lambda j, i: (i, j))
    return pl.pallas_call(
        body, name=name, grid=(f // tn, t // tm),
        in_specs=[pl.BlockSpec((tm, d), lambda j, i: (i, 0)), pl.BlockSpec((tn, d), lambda j, i: (j, 0)), blk, blk],
        out_specs=[blk, blk], out_shape=[jax.ShapeDtypeStruct((t, f), BF16)] * 2,
        compiler_params=_params(("parallel", "parallel")),
    )(dy, w_out, gate, up)


def _final_loss(x, g, target, *, name):
    r, d = x.shape
    tr = _tile(r, 512, 8)

    def body(x_ref, g_ref, t_ref, sq_ref, dx_ref, dg_ref):
        xv = x_ref[...]
        gv = g_ref[...]
        rs = lax.rsqrt(jnp.mean(xv * xv, axis=-1, keepdims=True) + NORM_EPS)
        xh = xv * rs
        err = xh * gv - t_ref[...]
        dy = err * (1.0 / d)
        dgh = dy * gv
        dx_ref[...] = rs * (dgh - xh * jnp.mean(dgh * xh, axis=-1, keepdims=True))
        sq = jnp.sum(err * err, axis=0, keepdims=True)
        part = jnp.sum(dy * xh, axis=0, keepdims=True)

        @pl.when(pl.program_id(0) == 0)
        def _():
            sq_ref[...] = sq
            dg_ref[...] = part

        @pl.when(pl.program_id(0) > 0)
        def _():
            sq_ref[...] += sq
            dg_ref[...] += part

    row = pl.BlockSpec((tr, d), lambda i: (i, 0))
    vec = pl.BlockSpec((1, d), lambda i: (0, 0))
    sq, dx, dg = pl.pallas_call(
        body, name=name, grid=(r // tr,), in_specs=[row, vec, row], out_specs=[vec, row, vec],
        out_shape=[jax.ShapeDtypeStruct((1, d), F32), jax.ShapeDtypeStruct((r, d), F32),
                   jax.ShapeDtypeStruct((1, d), F32)],
        compiler_params=_params(("arbitrary",)),
    )(x, g.reshape(1, d), target)
    return sq.reshape(d), dx, dg.reshape(d)


def _scan(a, b, *, name, reverse=False):
    t, w = a.shape
    tb = _tile(t, 1024, 8)
    nblk, ngrp = t // tb, tb // 8

    def body(a_ref, b_ref, h_ref, carry):
        @pl.when(pl.program_id(0) == 0)
        def _():
            carry[...] = jnp.zeros_like(carry)

        row = lax.broadcasted_iota(jnp.int32, (8, w), 0)

        def group(i, c):
            r0 = pl.multiple_of((ngrp - 1 - i if reverse else i) * 8, 8)
            av, bv = a_ref[pl.ds(r0, 8), :], b_ref[pl.ds(r0, 8), :]
            for s in (1, 2, 4):
                keep = row < 8 - s if reverse else row >= s
                shift = 8 - s if reverse else s
                bv = jnp.where(keep, bv + av * pltpu.roll(bv, shift, axis=0), bv)
                av = jnp.where(keep, av * pltpu.roll(av, shift, axis=0), av)
            hv = bv + av * c
            h_ref[pl.ds(r0, 8), :] = hv
            return hv[0:1, :] if reverse else hv[7:8, :]

        carry[0:1, :] = lax.fori_loop(0, ngrp, group, carry[0:1, :])

    blk = pl.BlockSpec((tb, w), lambda i: (nblk - 1 - i if reverse else i, 0))
    return pl.pallas_call(
        body, name=name, grid=(t // tb,), in_specs=[blk, blk], out_specs=blk,
        out_shape=jax.ShapeDtypeStruct((t, w), F32), scratch_shapes=[pltpu.VMEM((8, w), F32)],
        compiler_params=_params(("arbitrary",)),
    )(a, b)


def _row_specs(t, tb, width, col):
    per = tb // 8
    main = pl.BlockSpec((tb, width), lambda i: (i, col))
    before = pl.BlockSpec((8, width), lambda i: (jnp.maximum(i * per - 1, 0), col))
    after = pl.BlockSpec((8, width), lambda i: (jnp.minimum((i + 1) * per, t // 8 - 1), col))
    return main, before, after


def _with_rows_before(x_ref, before_ref):
    return jnp.concatenate([jnp.where(pl.program_id(0) > 0, before_ref[...], 0.0), x_ref[...]], axis=0)


def _tap(xe, s):
    return xe[8:] if s == 0 else pltpu.roll(xe, s, axis=0)[8:]


def _conv_rows(xe, w_ref):
    return sum(_tap(xe, CONV_K - 1 - k) * w_ref[k:k + 1, :] for k in range(CONV_K))


def _conv_bwd(dy, x, col, w, *, name):
    t, width = dy.shape
    tb = _tile(t, 256, 8)
    nblk = t // tb

    def body(dy_ref, dy_after_ref, x_ref, x_before_ref, w_ref, dx_ref, dw_ref):
        i = pl.program_id(0)
        dyv = dy_ref[...]
        dye = jnp.concatenate([dyv, jnp.where(i < nblk - 1, dy_after_ref[...], 0.0)], axis=0)
        dx = dyv * w_ref[CONV_K - 1:CONV_K, :]
        for s in range(1, CONV_K):
            dx = dx + pltpu.roll(dye, tb + 8 - s, axis=0)[:tb] * w_ref[CONV_K - 1 - s:CONV_K - s, :]
        dx_ref[...] = dx.astype(BF16)
        xe = _with_rows_before(x_ref, x_before_ref)

        @pl.when(i == 0)
        def _():
            dw_ref[...] = jnp.zeros_like(dw_ref)

        for k in range(CONV_K):
            dw_ref[k:k + 1, :] += jnp.sum(dyv * _tap(xe, CONV_K - 1 - k), axis=0, keepdims=True)

    main, _, after = _row_specs(t, tb, width, 0)
    xmain, xbefore, _ = _row_specs(t, tb, width, col)
    dx, dw = pl.pallas_call(
        body, name=name, grid=(nblk,),
        in_specs=[main, after, xmain, xbefore, pl.BlockSpec((CONV_K, width), lambda i: (0, 0))],
        out_specs=[main, pl.BlockSpec((8, width), lambda i: (0, 0))],
        out_shape=[jax.ShapeDtypeStruct((t, width), BF16), jax.ShapeDtypeStruct((8, width), F32)],
        compiler_params=_params(("arbitrary",)),
    )(dy, dy, x, x, w)
    return dx, dw[:CONV_K]


def _expm1(x):
    small = x * (1.0 + x * (0.5 + x * (1.0 / 6.0 + x * (1.0 / 24.0 + x * (1.0 / 120.0 + x * (1.0 / 720.0))))))
    return jnp.where(jnp.abs(x) < 0.1, small, jnp.exp(x) - 1.0)


def _lru_gate_terms(xc, wa_ref, ba_ref, wx_ref, bx_ref, sp_ref):
    xb = xc.astype(BF16)
    r = jax.nn.sigmoid(_dot(xb, wa_ref[...], 1, 0) + ba_ref[...])
    i = jax.nn.sigmoid(_dot(xb, wx_ref[...], 1, 0) + bx_ref[...])
    log_a = -r * sp_ref[...]
    return r, i, jnp.exp(log_a), jnp.sqrt(-_expm1(2.0 * log_a))


def _lru_gates_fwd(proj, col, conv_w, conv_b, wa, ba, wx, bx, sp, *, name):
    t = proj.shape[0]
    width = conv_w.shape[1]
    tb = _tile(t, 512, 8)

    def body(x_ref, x_before_ref, cw_ref, cb_ref, wa_ref, ba_ref, wx_ref, bx_ref, sp_ref, a_ref, b_ref, xc_ref):
        xc = _conv_rows(_with_rows_before(x_ref, x_before_ref), cw_ref) + cb_ref[...]
        r, i, a, mult = _lru_gate_terms(xc, wa_ref, ba_ref, wx_ref, bx_ref, sp_ref)
        a_ref[...] = a
        b_ref[...] = mult * i * xc
        xc_ref[...] = xc

    main, before, _ = _row_specs(t, tb, width, col)
    out = pl.BlockSpec((tb, width), lambda i: (i, 0))
    vec = pl.BlockSpec((1, width), lambda i: (0, 0))
    mat = pl.BlockSpec((width, width), lambda i: (0, 0))
    return pl.pallas_call(
        body, name=name, grid=(t // tb,),
        in_specs=[main, before, pl.BlockSpec((CONV_K, width), lambda i: (0, 0)), vec, mat, vec, mat, vec, vec],
        out_specs=[out] * 3, out_shape=[jax.ShapeDtypeStruct((t, width), F32)] * 3,
        compiler_params=_params(("parallel",)),
    )(proj, proj, conv_w, conv_b.reshape(1, -1), wa, ba.reshape(1, -1), wx, bx.reshape(1, -1), sp.reshape(1, -1))


def _lru_gates_bwd(xc, dtot, h_prev, wa, ba, wx, bx, sp, *, name):
    t, width = xc.shape
    tb = _tile(t, 512, 8)

    def body(xc_ref, dt_ref, hp_ref, wa_ref, ba_ref, wx_ref, bx_ref, sp_ref, dxc_ref, dwa_ref, dwx_ref, vec_ref):
        xc = xc_ref[...]
        r, i, a, mult = _lru_gate_terms(xc, wa_ref, ba_ref, wx_ref, bx_ref, sp_ref)
        db = dt_ref[...]
        d_la = db * hp_ref[...] * a - db * i * xc * (a * a / mult)
        d_pa = (-d_la * sp_ref[...]) * r * (1.0 - r)
        d_pi = db * mult * xc * i * (1.0 - i)
        dab, dib = d_pa.astype(BF16), d_pi.astype(BF16)
        dxc = db * mult * i + _dot(dab, wa_ref[...], 1, 1) + _dot(dib, wx_ref[...], 1, 1)
        dxc_ref[...] = dxc
        xb = xc.astype(BF16)
        rows = [jnp.sum(z, axis=0, keepdims=True) for z in (d_pa, d_pi, -d_la * r, dxc)]

        @pl.when(pl.program_id(0) == 0)
        def _():
            dwa_ref[...] = jnp.zeros_like(dwa_ref)
            dwx_ref[...] = jnp.zeros_like(dwx_ref)
            vec_ref[...] = jnp.zeros_like(vec_ref)

        dwa_ref[...] += _dot(xb, dab, 0, 0)
        dwx_ref[...] += _dot(xb, dib, 0, 0)
        for j, z in enumerate(rows):
            vec_ref[j:j + 1, :] += z

    blk = pl.BlockSpec((tb, width), lambda i: (i, 0))
    vec = pl.BlockSpec((1, width), lambda i: (0, 0))
    mat = pl.BlockSpec((width, width), lambda i: (0, 0))
    dxc, dwa, dwx, vecs = pl.pallas_call(
        body, name=name, grid=(t // tb,), in_specs=[blk, blk, blk, mat, vec, mat, vec, vec],
        out_specs=[blk, mat, mat, pl.BlockSpec((8, width), lambda i: (0, 0))],
        out_shape=[jax.ShapeDtypeStruct((t, width), F32), jax.ShapeDtypeStruct((width, width), F32),
                   jax.ShapeDtypeStruct((width, width), F32), jax.ShapeDtypeStruct((8, width), F32)],
        compiler_params=_params(("arbitrary",)),
    )(xc, dtot, h_prev, wa, ba.reshape(1, -1), wx, bx.reshape(1, -1), sp.reshape(1, -1))
    return dxc, dwa, dwx, vecs[:4]


GELU_C = math.sqrt(2.0 / math.pi)


def _gelu_terms(x):
    th = jnp.tanh(GELU_C * (x + 0.044715 * x * x * x))
    return 0.5 * x * (1.0 + th), 0.5 * (1.0 + th) + 0.5 * x * (1.0 - th * th) * GELU_C * (1.0 + 3 * 0.044715 * x * x)


def _mix_join_fwd(outs, lses, hs, proj, gr_col, *, name):
    t, w = hs.shape
    tb = _tile(t, 512, 16)
    n = len(outs)

    def body(*refs):
        o_refs, l_refs = refs[:n], refs[n:2 * n]
        hs_ref, gr_ref, cat_ref, attn_ref, lse_ref = refs[2 * n:]
        ls = [r[...] for r in l_refs]
        m = ls[0]
        for l in ls[1:]:
            m = jnp.maximum(m, l)
        ws = [jnp.exp(l - m) for l in ls]
        den = sum(ws)
        attn = sum(wt * r[...] for wt, r in zip(ws, o_refs)) / den
        attn_ref[...] = attn
        lse_ref[...] = m + jnp.log(den)
        cat_ref[:, :w] = attn.astype(BF16)
        cat_ref[:, w:] = (hs_ref[...] * _gelu_terms(gr_ref[...])[0]).astype(BF16)

    blk = pl.BlockSpec((tb, w), lambda i: (i, 0))
    return pl.pallas_call(
        body, name=name, grid=(t // tb,),
        in_specs=[blk] * (2 * n + 1) + [pl.BlockSpec((tb, w), lambda i: (i, gr_col))],
        out_specs=[pl.BlockSpec((tb, 2 * w), lambda i: (i, 0)), blk, blk],
        out_shape=[jax.ShapeDtypeStruct((t, 2 * w), BF16), jax.ShapeDtypeStruct((t, w), F32),
                   jax.ShapeDtypeStruct((t, w), F32)],
        compiler_params=_params(("parallel",)),
    )(*outs, *lses, hs, proj)


def _mix_join_bwd(dcat, attn, lse, hs, proj, gr_col, *, name):
    t, w = hs.shape
    hd = w // ATTN_HEADS
    tb = _tile(t, 512, 16)

    def body(dcat_ref, attn_ref, lse_ref, hs_ref, gr_ref, stats_ref, dhs_ref, dgr_ref):
        stats_ref[...] = jnp.zeros_like(stats_ref)
        for h in range(ATTN_HEADS):
            sl = slice(h * hd, (h + 1) * hd)
            stats_ref[:, h:h + 1] = lse_ref[:, h * hd:h * hd + 1]
            stats_ref[:, ATTN_HEADS + h:ATTN_HEADS + h + 1] = jnp.sum(dcat_ref[:, sl] * attn_ref[:, sl], axis=-1, keepdims=True)
        dy = dcat_ref[:, w:]
        g, dg = _gelu_terms(gr_ref[...])
        dhs_ref[...] = dy * g
        dgr_ref[...] = (dy * hs_ref[...] * dg).astype(BF16)

    blk = pl.BlockSpec((tb, w), lambda i: (i, 0))
    return pl.pallas_call(
        body, name=name, grid=(t // tb,),
        in_specs=[pl.BlockSpec((tb, 2 * w), lambda i: (i, 0)), blk, blk, blk, pl.BlockSpec((tb, w), lambda i: (i, gr_col))],
        out_specs=[pl.BlockSpec((tb, LANES), lambda i: (i, 0)), blk, blk],
        out_shape=[jax.ShapeDtypeStruct((t, LANES), F32), jax.ShapeDtypeStruct((t, w), F32),
                   jax.ShapeDtypeStruct((t, w), BF16)],
        compiler_params=_params(("parallel",)),
    )(dcat, attn, lse, hs, proj)


def _silu_terms(x):
    s = jax.nn.sigmoid(x)
    return x * s, s * (1.0 + x * (1.0 - s))


def _dn_prep_fwd(proj, conv_w, *, name):
    t = proj.shape[0]
    w3 = conv_w.shape[1]
    w = w3 // 3
    hd = w // DN_HEADS
    tb = _tile(t, 256, 8)

    def body(x_ref, x_before_ref, cw_ref, q_ref, k_ref, v_ref, c_ref):
        c = _conv_rows(_with_rows_before(x_ref, x_before_ref), cw_ref)
        c_ref[...] = c
        s = _silu_terms(c)[0]
        v_ref[...] = s[:, 2 * w:]
        for part, ref, scale in ((0, q_ref, hd ** -0.5), (1, k_ref, 1.0)):
            for h in range(DN_HEADS):
                z = s[:, part * w + h * hd:part * w + (h + 1) * hd]
                ref[:, h * hd:(h + 1) * hd] = z * (lax.rsqrt(jnp.sum(z * z, axis=-1, keepdims=True) + 1e-6) * scale)

    main, before, _ = _row_specs(t, tb, w3, 0)
    out = pl.BlockSpec((tb, w), lambda i: (i, 0))
    return pl.pallas_call(
        body, name=name, grid=(t // tb,), in_specs=[main, before, pl.BlockSpec((CONV_K, w3), lambda i: (0, 0))],
        out_specs=[out, out, out, pl.BlockSpec((tb, w3), lambda i: (i, 0))],
        out_shape=[jax.ShapeDtypeStruct((t, w), F32)] * 3 + [jax.ShapeDtypeStruct((t, w3), F32)],
        compiler_params=_params(("parallel",)),
    )(proj, proj, conv_w)


def _dn_prep_bwd(c, dq, dk, dv, *, name):
    t, w3 = c.shape
    w = w3 // 3
    hd = w // DN_HEADS
    tb = _tile(t, 256, 8)

    def body(c_ref, dq_ref, dk_ref, dv_ref, dc_ref):
        cv = c_ref[...]
        s, ds = _silu_terms(cv)
        dc_ref[:, 2 * w:] = dv_ref[...] * ds[:, 2 * w:]
        for part, ref, scale in ((0, dq_ref, hd ** -0.5), (1, dk_ref, 1.0)):
            for h in range(DN_HEADS):
                cols = slice(part * w + h * hd, part * w + (h + 1) * hd)
                z = s[:, cols]
                rn = lax.rsqrt(jnp.sum(z * z, axis=-1, keepdims=True) + 1e-6)
                y = z * rn
                dy = ref[:, h * hd:(h + 1) * hd] * scale
                dc_ref[:, cols] = rn * (dy - y * jnp.sum(dy * y, axis=-1, keepdims=True)) * ds[:, cols]

    blk = pl.BlockSpec((tb, w), lambda i: (i, 0))
    wide = pl.BlockSpec((tb, w3), lambda i: (i, 0))
    return pl.pallas_call(
        body, name=name, grid=(t // tb,), in_specs=[wide, blk, blk, blk], out_specs=wide,
        out_shape=jax.ShapeDtypeStruct((t, w3), F32), compiler_params=_params(("parallel",)),
    )(c, dq, dk, dv)


def _dn_gate_fwd(o, proj, z_col, o_norm, *, name):
    t, w = o.shape
    hd = w // DN_HEADS
    tb = _tile(t, 512, 16)

    def body(o_ref, z_ref, g_ref, y_ref):
        for h in range(DN_HEADS):
            sl = slice(h * hd, (h + 1) * hd)
            ov = o_ref[:, sl]
            rn = lax.rsqrt(jnp.mean(ov * ov, axis=-1, keepdims=True) + NORM_EPS)
            y_ref[:, sl] = (ov * rn * g_ref[...] * _silu_terms(z_ref[:, sl])[0]).astype(BF16)

    blk = pl.BlockSpec((tb, w), lambda i: (i, 0))
    return pl.pallas_call(
        body, name=name, grid=(t // tb,),
        in_specs=[blk, pl.BlockSpec((tb, w), lambda i: (i, z_col)), pl.BlockSpec((1, hd), lambda i: (0, 0))],
        out_specs=blk, out_shape=jax.ShapeDtypeStruct((t, w), BF16), compiler_params=_params(("parallel",)),
    )(o, proj, o_norm.reshape(1, hd))


def _dn_gate_bwd(o, proj, z_col, o_norm, dy, *, name):
    t, w = o.shape
    hd = w // DN_HEADS
    tb = _tile(t, 512, 16)

    def body(o_ref, z_ref, g_ref, dy_ref, do_ref, dz_ref, dg_ref):
        gv = g_ref[...]
        dg = jnp.zeros((1, hd), F32)
        for h in range(DN_HEADS):
            sl = slice(h * hd, (h + 1) * hd)
            ov, dyv = o_ref[:, sl], dy_ref[:, sl]
            sz, dsz = _silu_terms(z_ref[:, sl])
            rn = lax.rsqrt(jnp.mean(ov * ov, axis=-1, keepdims=True) + NORM_EPS)
            nv = ov * rn
            dz_ref[:, sl] = (dyv * nv * gv * dsz).astype(BF16)
            dn = dyv * gv * sz
            do_ref[:, sl] = rn * (dn - nv * jnp.mean(dn * nv, axis=-1, keepdims=True))
            dg = dg + jnp.sum(dyv * nv * sz, axis=0, keepdims=True)

        @pl.when(pl.program_id(0) == 0)
        def _():
            dg_ref[...] = dg

        @pl.when(pl.program_id(0) > 0)
        def _():
            dg_ref[...] += dg

    blk = pl.BlockSpec((tb, w), lambda i: (i, 0))
    vec = pl.BlockSpec((1, hd), lambda i: (0, 0))
    do, dz, dg = pl.pallas_call(
        body, name=name, grid=(t // tb,), in_specs=[blk, pl.BlockSpec((tb, w), lambda i: (i, z_col)), vec, blk],
        out_specs=[blk, blk, vec],
        out_shape=[jax.ShapeDtypeStruct((t, w), F32), jax.ShapeDtypeStruct((t, w), BF16),
                   jax.ShapeDtypeStruct((1, hd), F32)],
        compiler_params=_params(("arbitrary",)),
    )(o, proj, o_norm.reshape(1, hd), dy)
    return do, dz, dg.reshape(hd)


def _band_masks(n):
    qi = lax.broadcasted_iota(jnp.int32, (ATTN_BLOCK, ATTN_BLOCK), 0)
    kj = lax.broadcasted_iota(jnp.int32, (ATTN_BLOCK, ATTN_BLOCK), 1)
    return kj <= qi, jnp.logical_and(kj >= qi, n > 0)


def _dattn_fwd(qkv, dil, *, name):
    seq, w = qkv.shape[0], qkv.shape[1] // (3 * dil)
    t = seq * dil
    hd = w // ATTN_HEADS
    assert seq % ATTN_BLOCK == 0
    nb = seq // ATTN_BLOCK
    scale = hd ** -0.5

    def body(q_ref, kp_ref, kc_ref, vp_ref, vc_ref, o_ref, lse_ref):
        mc, mp = _band_masks(pl.program_id(1))
        heads = range(ATTN_HEADS)
        sls = [slice(h * hd, (h + 1) * hd) for h in heads]
        qs = [(q_ref[:, sl] * scale).astype(BF16) for sl in sls]
        scs = [jnp.where(mc, _dot(qs[h], kc_ref[:, sls[h]].astype(BF16), 1, 1), NEG) for h in heads]
        sps = [jnp.where(mp, _dot(qs[h], kp_ref[:, sls[h]].astype(BF16), 1, 1), NEG) for h in heads]
        ms = [jnp.maximum(jnp.max(sc, axis=-1, keepdims=True), jnp.max(sp, axis=-1, keepdims=True))
              for sc, sp in zip(scs, sps)]
        pcs = [jnp.exp(sc - m) for sc, m in zip(scs, ms)]
        pps = [jnp.exp(sp - m) for sp, m in zip(sps, ms)]
        dens = [jnp.sum(pc, axis=-1, keepdims=True) + jnp.sum(pp, axis=-1, keepdims=True) for pc, pp in zip(pcs, pps)]
        outs = [_dot(pcs[h].astype(BF16), vc_ref[:, sls[h]].astype(BF16), 1, 0)
                + _dot(pps[h].astype(BF16), vp_ref[:, sls[h]].astype(BF16), 1, 0) for h in heads]
        for h in heads:
            o_ref[:, sls[h]] = outs[h] / dens[h]
            lse_ref[:, sls[h]] = jnp.broadcast_to(ms[h] + jnp.log(dens[h]), (ATTN_BLOCK, hd))

    cur = pl.BlockSpec((ATTN_BLOCK, w), lambda r, n: (n, r))
    part = lambda j, row: pl.BlockSpec((ATTN_BLOCK, w), lambda r, n: (row(n), 3 * r + j))
    here, before = (lambda n: n), (lambda n: jnp.maximum(n - 1, 0))
    o, lse = pl.pallas_call(
        body, name=name, grid=(dil, nb),
        in_specs=[part(0, here), part(1, before), part(1, here), part(2, before), part(2, here)], out_specs=[cur, cur],
        out_shape=[jax.ShapeDtypeStruct((seq, dil * w), F32)] * 2, compiler_params=_params(("parallel", "parallel")),
    )(qkv, qkv, qkv, qkv, qkv)
    return o.reshape(t, w), lse.reshape(t, w)


def _dattn_bwd(qkv, do, stats, dil, *, name):
    t, w = do.shape
    hd = w // ATTN_HEADS
    seq = t // dil
    nb = seq // ATTN_BLOCK
    scale = hd ** -0.5

    def body(qc_ref, qn_ref, doc_ref, don_ref, sc_ref, sn_ref, kp_ref, kc_ref, vp_ref, vc_ref, dqkv_ref):
        n = pl.program_id(1)
        mc, mp = _band_masks(n)
        _, mx = _band_masks(jnp.where(n + 1 < nb, 1, 0))
        heads = range(ATTN_HEADS)
        sls = [slice(h * hd, (h + 1) * hd) for h in heads]
        lse_of, delta_of = (lambda ref, h: ref[:, h:h + 1]), (lambda ref, h: ref[:, ATTN_HEADS + h:ATTN_HEADS + h + 1])
        bf = lambda ref, scl=None: [(ref[:, sl] if scl is None else ref[:, sl] * scl).astype(BF16) for sl in sls]
        qc, qn, kc, kp = bf(qc_ref, scale), bf(qn_ref, scale), bf(kc_ref), bf(kp_ref)
        vc, vp, doc, don = bf(vc_ref), bf(vp_ref), bf(doc_ref), bf(don_ref)
        p_c = [jnp.exp(jnp.where(mc, _dot(qc[h], kc[h], 1, 1), NEG) - lse_of(sc_ref, h)) for h in heads]
        p_p = [jnp.exp(jnp.where(mp, _dot(qc[h], kp[h], 1, 1), NEG) - lse_of(sc_ref, h)) for h in heads]
        p_x = [jnp.exp(jnp.where(mx, _dot(qn[h], kc[h], 1, 1), NEG) - lse_of(sn_ref, h)) for h in heads]
        ds_c = [(p_c[h] * (_dot(doc[h], vc[h], 1, 1) - delta_of(sc_ref, h))).astype(BF16) for h in heads]
        ds_p = [(p_p[h] * (_dot(doc[h], vp[h], 1, 1) - delta_of(sc_ref, h))).astype(BF16) for h in heads]
        ds_x = [(p_x[h] * (_dot(don[h], vc[h], 1, 1) - delta_of(sn_ref, h))).astype(BF16) for h in heads]
        for h in heads:
            at = lambda part: slice(part * w + h * hd, part * w + (h + 1) * hd)
            dqkv_ref[:, at(0)] = ((_dot(ds_c[h], kc[h], 1, 0) + _dot(ds_p[h], kp[h], 1, 0)) * scale).astype(BF16)
            dqkv_ref[:, at(1)] = (_dot(ds_c[h], qc[h], 0, 0) + _dot(ds_x[h], qn[h], 0, 0)).astype(BF16)
            dqkv_ref[:, at(2)] = (_dot(p_c[h].astype(BF16), doc[h], 0, 0) + _dot(p_x[h].astype(BF16), don[h], 0, 0)).astype(BF16)

    cur = pl.BlockSpec((ATTN_BLOCK, w), lambda r, n: (n, r))
    nxt = pl.BlockSpec((ATTN_BLOCK, w), lambda r, n: (jnp.minimum(n + 1, nb - 1), r))
    part = lambda j, row: pl.BlockSpec((ATTN_BLOCK, w), lambda r, n: (row(n), 3 * r + j))
    here, before, after = (lambda n: n), (lambda n: jnp.maximum(n - 1, 0)), (lambda n: jnp.minimum(n + 1, nb - 1))
    stat = lambda row: pl.BlockSpec((ATTN_BLOCK, LANES), lambda r, n: (row(n), r))
    do, stats = do.reshape(seq, dil * w), stats.reshape(seq, dil * LANES)
    dqkv = pl.pallas_call(
        body, name=name, grid=(dil, nb),
        in_specs=[part(0, here), part(0, after), cur, nxt, stat(here), stat(after), part(1, before), part(1, here),
                  part(2, before), part(2, here)], out_specs=pl.BlockSpec((ATTN_BLOCK, 3 * w), lambda r, n: (n, r)),
        out_shape=jax.ShapeDtypeStruct((seq, dil * 3 * w), BF16), compiler_params=_params(("parallel", "parallel")),
    )(qkv, qkv, do, do, stats, stats, qkv, qkv, qkv, qkv)
    return dqkv.reshape(t, 3 * w)


def _xattn_fwd(q, k, v, *, name):
    t, w = q.shape
    nm = k.shape[0]
    hd = w // XA_HEADS
    scale = hd ** -0.5
    tq = _tile(t, 512, 8)

    def body(q_ref, k_ref, v_ref, o_ref):
        heads = range(XA_HEADS)
        sls = [slice(h * hd, (h + 1) * hd) for h in heads]
        ss = [_dot((q_ref[:, sl] * scale).astype(BF16), k_ref[:, sl].astype(BF16), 1, 1) for sl in sls]
        ps = [jnp.exp(s - jnp.max(s, axis=-1, keepdims=True)) for s in ss]
        ps = [p / jnp.sum(p, axis=-1, keepdims=True) for p in ps]
        for h in heads:
            o_ref[:, sls[h]] = _dot(ps[h].astype(BF16), v_ref[:, sls[h]].astype(BF16), 1, 0).astype(BF16)

    qs = pl.BlockSpec((tq, w), lambda i: (i, 0))
    ks = pl.BlockSpec((nm, w), lambda i: (0, 0))
    return pl.pallas_call(
        body, name=name, grid=(t // tq,), in_specs=[qs, ks, ks], out_specs=qs,
        out_shape=jax.ShapeDtypeStruct((t, w), BF16), compiler_params=_params(("parallel",)),
    )(q, k, v)


def _xattn_bwd(q, k, v, do, *, name):
    t, w = q.shape
    nm = k.shape[0]
    hd = w // XA_HEADS
    scale = hd ** -0.5
    tq = _tile(t, 512, 8)

    def body(q_ref, k_ref, v_ref, do_ref, dq_ref, dk_ref, dv_ref):
        first = pl.program_id(0) == 0
        heads = range(XA_HEADS)
        sls = [slice(h * hd, (h + 1) * hd) for h in heads]
        qs_ = [(q_ref[:, sl] * scale).astype(BF16) for sl in sls]
        ks_, vs_, dos = ([ref[:, sl].astype(BF16) for sl in sls] for ref in (k_ref, v_ref, do_ref))
        ss = [_dot(qs_[h], ks_[h], 1, 1) for h in heads]
        ps = [jnp.exp(s - jnp.max(s, axis=-1, keepdims=True)) for s in ss]
        ps = [p / jnp.sum(p, axis=-1, keepdims=True) for p in ps]
        dps = [_dot(dos[h], vs_[h], 1, 1) for h in heads]
        dss = [(p * (dp - jnp.sum(p * dp, axis=-1, keepdims=True))).astype(BF16) for p, dp in zip(ps, dps)]
        for h in heads:
            sl = sls[h]
            dq_ref[:, sl] = (_dot(dss[h], ks_[h], 1, 0) * scale).astype(BF16)
            dk = _dot(dss[h], qs_[h], 0, 0)
            dv = _dot(ps[h].astype(BF16), dos[h], 0, 0)

            @pl.when(first)
            def _():
                dk_ref[:, sl] = dk
                dv_ref[:, sl] = dv

            @pl.when(jnp.logical_not(first))
            def _():
                dk_ref[:, sl] += dk
                dv_ref[:, sl] += dv

    qs = pl.BlockSpec((tq, w), lambda i: (i, 0))
    ks = pl.BlockSpec((nm, w), lambda i: (0, 0))
    return pl.pallas_call(
        body, name=name, grid=(t // tq,), in_specs=[qs, ks, ks, qs], out_specs=[qs, ks, ks],
        out_shape=[jax.ShapeDtypeStruct((t, w), BF16), jax.ShapeDtypeStruct((nm, w), F32),
                   jax.ShapeDtypeStruct((nm, w), F32)],
        compiler_params=_params(("arbitrary",)),
    )(q, k, v, do)


def _dn_head_terms(q_ref, k_ref, gc_ref, gcr_ref, b_ref, h, hd):
    c = DN_CHUNK
    sl = slice(h * hd, (h + 1) * hd)
    qh, kh = q_ref[:, sl], k_ref[:, sl]
    gcc, gcr_h, bh = gc_ref[:, h:h + 1], gcr_ref[0, h:h + 1, :], b_ref[:, h:h + 1]
    row = lax.broadcasted_iota(jnp.int32, (c, c), 0)
    col = lax.broadcasted_iota(jnp.int32, (c, c), 1)
    decay = jnp.exp(jnp.where(row >= col, gcc - gcr_h, NEG))
    kb = kh * bh
    kkt = _dot(kb.astype(BF16), kh.astype(BF16), 1, 1)
    qkt = _dot(qh.astype(BF16), kh.astype(BF16), 1, 1)
    gl = gcc[c - 1:c, :]
    return dict(sl=sl, q=qh, k=kh, gcc=gcc, b=bh, row=row, col=col, decay=decay, kb=kb, kkt=kkt, qkt=qkt,
                e=jnp.exp(gcc), el=jnp.exp(gl), r=jnp.exp(gl - gcc))


def _dn_local_fwd(q, k, v, gc, gcr, beta, *, name):
    t, w = q.shape
    hd = w // DN_HEADS
    c = DN_CHUNK
    nch = t // c
    heads = range(DN_HEADS)

    def body(q_ref, k_ref, v_ref, gc_ref, gcr_ref, b_ref,
             u_ref, wq_ref, wqt_ref, kr_ref, krt_ref, qk_ref, qkt_ref, invt_ref):
        tm = [_dn_head_terms(q_ref, k_ref, gc_ref, gcr_ref, b_ref, h, hd) for h in heads]
        pw = [jnp.where(m['row'] > m['col'], m['kkt'] * m['decay'], 0.0) for m in tm]
        inv = [(m['row'] == m['col']).astype(F32) - p for m, p in zip(tm, pw)]
        for _ in range(int(math.log2(c)) - 1):
            pw = [_dot(p.astype(BF16), p.astype(BF16), 1, 0) for p in pw]
            inv = [i + _dot(i.astype(BF16), p.astype(BF16), 1, 0) for i, p in zip(inv, pw)]
        for h, m in zip(heads, tm):
            rhs = jnp.concatenate([v_ref[:, m['sl']] * m['b'], m['kb'] * m['e']], axis=1).astype(BF16)
            sol = _dot(inv[h].astype(BF16), rhs, 1, 0)
            u_ref[:, m['sl']] = sol[:, :hd]
            wq = jnp.concatenate([sol[:, hd:], m['q'] * m['e']], axis=0)
            kr = m['k'] * m['r']
            qk = m['qkt'] * m['decay']
            wq_ref[0, h], wqt_ref[0, h] = wq.astype(BF16), wq.T.astype(BF16)
            kr_ref[0, h], krt_ref[0, h] = kr.astype(BF16), kr.T.astype(BF16)
            qk_ref[0, h], qkt_ref[0, h] = qk.astype(BF16), qk.T.astype(BF16)
            invt_ref[0, h] = inv[h].T.astype(BF16)

    rows = pl.BlockSpec((c, w), lambda n: (n, 0))
    cols = pl.BlockSpec((c, DN_HEADS), lambda n: (n, 0))
    rowg = pl.BlockSpec((1, DN_HEADS, c), lambda n: (n, 0, 0))
    per = lambda a, b: (pl.BlockSpec((1, DN_HEADS, a, b), lambda n: (n, 0, 0, 0)),
                        jax.ShapeDtypeStruct((nch, DN_HEADS, a, b), BF16))
    outs = [(rows, jax.ShapeDtypeStruct((t, w), F32)), per(2 * c, hd), per(hd, 2 * c), per(c, hd), per(hd, c),
            per(c, c), per(c, c), per(c, c)]
    return pl.pallas_call(
        body, name=name, grid=(nch,), in_specs=[rows, rows, rows, cols, rowg, cols],
        out_specs=[o[0] for o in outs], out_shape=[o[1] for o in outs], compiler_params=_params(("parallel",)),
    )(q, k, v, gc, gcr, beta)


def _dn_state_fwd(u, wq, krt, qk, gc, *, name):
    t, w = u.shape
    hd = w // DN_HEADS
    c = DN_CHUNK
    nch = t // c
    grp = _tile(nch, DN_STEP_CHUNKS, 1)

    def body(u_ref, wq_ref, krt_ref, qk_ref, gc_ref, o_ref, vn_ref, s_ref, state):
        @pl.when(pl.program_id(0) == 0)
        def _():
            state[...] = jnp.zeros_like(state)

        heads = range(DN_HEADS)
        sls = [slice(h * hd, (h + 1) * hd) for h in heads]
        shs = [state[h] for h in heads]
        for j in range(grp):
            rs = slice(j * c, (j + 1) * c)
            wss = [_dot(wq_ref[j, h], shs[h].astype(BF16), 1, 0) for h in heads]
            vns = [(u_ref[rs, sls[h]] - wss[h][:c]).astype(BF16) for h in heads]
            outs = [wss[h][c:] + _dot(qk_ref[j, h], vns[h], 1, 0) for h in heads]
            for h in heads:
                s_ref[j, h] = shs[h]
                vn_ref[rs, sls[h]] = vns[h]
                o_ref[rs, sls[h]] = outs[h]
            shs = [shs[h] * jnp.exp(gc_ref[(j + 1) * c - 1:(j + 1) * c, h:h + 1]) + _dot(krt_ref[j, h], vns[h], 1, 0)
                   for h in heads]
        for h in heads:
            state[h] = shs[h]

    rows = pl.BlockSpec((grp * c, w), lambda n: (n, 0))
    per = lambda a, b: pl.BlockSpec((grp, DN_HEADS, a, b), lambda n: (n, 0, 0, 0))
    return pl.pallas_call(
        body, name=name, grid=(nch // grp,),
        in_specs=[rows, per(2 * c, hd), per(hd, c), per(c, c), pl.BlockSpec((grp * c, DN_HEADS), lambda n: (n, 0))],
        out_specs=[rows, rows, per(hd, hd)],
        out_shape=[jax.ShapeDtypeStruct((t, w), F32), jax.ShapeDtypeStruct((t, w), BF16),
                   jax.ShapeDtypeStruct((nch, DN_HEADS, hd, hd), F32)],
        scratch_shapes=[pltpu.VMEM((DN_HEADS, hd, hd), F32)], compiler_params=_params(("arbitrary",)),
    )(u, wq, krt, qk, gc)


def _dn_state_bwd(do, qkt, kr, vn, states, wqt, gc, *, name):
    t, w = do.shape
    hd = w // DN_HEADS
    c = DN_CHUNK
    nch = t // c
    grp = _tile(nch, DN_STEP_CHUNKS, 1)

    def body(do_ref, qkt_ref, kr_ref, vn_ref, s_ref, wqt_ref, gc_ref, dvn_ref, dkr_ref, del_ref, dstate):
        @pl.when(pl.program_id(0) == 0)
        def _():
            dstate[...] = jnp.zeros_like(dstate)

        heads = range(DN_HEADS)
        sls = [slice(h * hd, (h + 1) * hd) for h in heads]
        dsns = [dstate[h] for h in heads]
        for j in reversed(range(grp)):
            rs = slice(j * c, (j + 1) * c)
            dsbs = [d.astype(BF16) for d in dsns]
            dobs = [do_ref[rs, sl].astype(BF16) for sl in sls]
            dvns = [(_dot(qkt_ref[j, h], dobs[h], 1, 0) + _dot(kr_ref[j, h], dsbs[h], 1, 0)).astype(BF16) for h in heads]
            dkrs = [_dot(vn_ref[rs, sls[h]], dsbs[h], 1, 1) for h in heads]
            for h in heads:
                dvn_ref[rs, sls[h]] = dvns[h]
                dkr_ref[rs, sls[h]] = dkrs[h]
                d_el = jnp.sum(jnp.sum(dsns[h] * s_ref[j, h], axis=1, keepdims=True), axis=0, keepdims=True)
                del_ref[j, h:h + 1, :] = jnp.broadcast_to(d_el, (1, LANES))
            dsns = [dsns[h] * jnp.exp(gc_ref[(j + 1) * c - 1:(j + 1) * c, h:h + 1])
                    + _dot(wqt_ref[j, h], jnp.concatenate([-dvns[h], dobs[h]], axis=0), 1, 0) for h in heads]
        for h in heads:
            dstate[h] = dsns[h]

    rev = lambda n: nch // grp - 1 - n
    rows = pl.BlockSpec((grp * c, w), lambda n: (rev(n), 0))
    per = lambda a, b: pl.BlockSpec((grp, DN_HEADS, a, b), lambda n: (rev(n), 0, 0, 0))
    return pl.pallas_call(
        body, name=name, grid=(nch // grp,),
        in_specs=[rows, per(c, c), per(c, hd), rows, per(hd, hd), per(hd, 2 * c),
                  pl.BlockSpec((grp * c, DN_HEADS), lambda n: (rev(n), 0))],
        out_specs=[rows, rows, pl.BlockSpec((grp, DN_HEADS, LANES), lambda n: (rev(n), 0, 0))],
        out_shape=[jax.ShapeDtypeStruct((t, w), BF16), jax.ShapeDtypeStruct((t, w), F32),
                   jax.ShapeDtypeStruct((nch, DN_HEADS, LANES), F32)],
        scratch_shapes=[pltpu.VMEM((DN_HEADS, hd, hd), F32)], compiler_params=_params(("arbitrary",)),
    )(do, qkt, kr, vn, states, wqt, gc)


def _dn_local_bwd(q, k, v, gc, gcr, beta, invt, u, wq, vn, states, do, dvn, dkr, d_el, *, name):
    t, w = q.shape
    hd = w // DN_HEADS
    c = DN_CHUNK
    nch = t // c

    def body(q_ref, k_ref, v_ref, gc_ref, gcr_ref, b_ref, invt_ref, u_ref, wq_ref, vn_ref, s_ref, do_ref, dvn_ref,
             dkr_ref, del_ref, dq_ref, dk_ref, dv_ref, dgc_ref, dgr_ref, db_ref):
        heads = range(DN_HEADS)
        tms = [_dn_head_terms(q_ref, k_ref, gc_ref, gcr_ref, b_ref, h, hd) for h in heads]
        dobs = [do_ref[:, m['sl']].astype(BF16) for m in tms]
        tss = [_dot(jnp.concatenate([dob, -dvn_ref[:, m['sl']]], axis=0), s_ref[0, h].astype(BF16), 1, 1)
               for h, m, dob in zip(heads, tms, dobs)]
        d_qks = [_dot(dob, vn_ref[:, m['sl']], 1, 1) for m, dob in zip(tms, dobs)]
        d_rhss = [_dot(invt_ref[0, h], jnp.concatenate([dvn_ref[:, m['sl']], ts[c:].astype(BF16)], axis=1), 1, 0)
                  for h, m, ts in zip(heads, tms, tss)]
        d_as = [-_dot(d_rhs.astype(BF16),
                      jnp.concatenate([u_ref[:, m['sl']].astype(BF16), wq_ref[0, h, :c, :]], axis=1), 1, 1)
                for h, m, d_rhs in zip(heads, tms, d_rhss)]
        for h in heads:
            m, d_qe, d_qk, d_rhs, d_a = tms[h], tss[h][:c], d_qks[h], d_rhss[h], d_as[h]
            sl, qh, kh, bh, e, r, decay = m['sl'], m['q'], m['k'], m['b'], m['e'], m['r'], m['decay']
            vh = v_ref[:, sl]
            d_ru, d_rw = d_rhs[:, :hd], d_rhs[:, hd:]
            dv_ref[:, sl] = d_ru * bh
            d_e = jnp.sum(d_rw * m['kb'], axis=1, keepdims=True) + jnp.sum(d_qe * qh, axis=1, keepdims=True)
            d_n = jnp.where(m['row'] > m['col'], d_a, 0.0)
            d_m, d_p = (d_n * decay).astype(BF16), (d_qk * decay).astype(BF16)
            tk = _dot(jnp.concatenate([d_m, d_p], axis=0), kh.astype(BF16), 1, 0)
            d_kb = d_rw * e + tk[:c]
            dq_ref[:, sl] = tk[c:] + d_qe * e
            dk = _dot(d_m, m['kb'].astype(BF16), 0, 0) + _dot(d_p, qh.astype(BF16), 0, 0)
            dd = (d_n * m['kkt'] + d_qk * m['qkt']) * decay
            d_kr = dkr_ref[:, sl]
            d_r = jnp.sum(d_kr * kh, axis=1, keepdims=True)
            d_gl = del_ref[0, h:h + 1, 0:1] * m['el'] + jnp.sum(d_r * r, axis=0, keepdims=True)
            last = lax.broadcasted_iota(jnp.int32, (c, 1), 0) == c - 1
            dgc_ref[:, h:h + 1] = jnp.sum(dd, axis=1, keepdims=True) + d_e * e - d_r * r + jnp.where(last, d_gl, 0.0)
            dgr_ref[0, h:h + 1, :] = -jnp.sum(dd, axis=0, keepdims=True)
            dk_ref[:, sl] = dk + d_kr * r + d_kb * bh
            db_ref[:, h:h + 1] = jnp.sum(d_ru * vh, axis=1, keepdims=True) + jnp.sum(d_kb * kh, axis=1, keepdims=True)

    rows = pl.BlockSpec((c, w), lambda n: (n, 0))
    cols = pl.BlockSpec((c, DN_HEADS), lambda n: (n, 0))
    rowg = pl.BlockSpec((1, DN_HEADS, c), lambda n: (n, 0, 0))
    per = lambda a, b: pl.BlockSpec((1, DN_HEADS, a, b), lambda n: (n, 0, 0, 0))
    return pl.pallas_call(
        body, name=name, grid=(nch,),
        in_specs=[rows, rows, rows, cols, rowg, cols, per(c, c), rows, per(2 * c, hd), rows, per(hd, hd), rows, rows,
                  rows, pl.BlockSpec((1, DN_HEADS, LANES), lambda n: (n, 0, 0))],
        out_specs=[rows, rows, rows, cols, rowg, cols],
        out_shape=[jax.ShapeDtypeStruct((t, w), F32)] * 3
        + [jax.ShapeDtypeStruct((t, DN_HEADS), F32), jax.ShapeDtypeStruct((nch, DN_HEADS, c), F32),
           jax.ShapeDtypeStruct((t, DN_HEADS), F32)],
        compiler_params=_params(("parallel",)),
    )(q, k, v, gc, gcr, beta, invt, u, wq, vn, states, do, dvn, dkr, d_el)


def _other_chips():
    x, y = lax.axis_index("x"), lax.axis_index("y")
    return [(1 - x, y), (x, 1 - y), (1 - x, 1 - y)]


def _gather(src, *, name):
    def body(src_ref, out_ref, send_sems, recv_sems, local_sem):
        x, y, c = lax.axis_index("x"), lax.axis_index("y"), lax.axis_index("c")
        me, sibling, chips = (x, y, c), (x, y, 1 - c), _other_chips()
        slot = lambda px, py, pc: out_ref.at[4 * px + 2 * py + pc]

        def copy(k, block, to, own=False):
            return pltpu.make_async_remote_copy(
                src_ref=src_ref if own else slot(*block), dst_ref=slot(*block), send_sem=send_sems.at[k],
                recv_sem=recv_sems.at[k], device_id=to, device_id_type=MESH)

        local = pltpu.make_async_copy(src_ref, slot(*me), local_sem)
        local.start()
        first = [copy(0, me, sibling, own=True)] + [copy(1 + j, me, (*chip, c), own=True) for j, chip in enumerate(chips)]
        for cp in first:
            cp.start()
        passed = [copy(4 + j, (*chip, c), sibling) for j, chip in enumerate(chips)]
        for j, chip in enumerate(chips):
            copy(1 + j, (*chip, c), me).wait_recv()
            passed[j].start()
        copy(0, sibling, me).wait_recv()
        for j, chip in enumerate(chips):
            copy(4 + j, (*chip, 1 - c), me).wait_recv()
        for cp in first + passed:
            cp.wait_send()
        local.wait()

    return pl.pallas_call(
        body, name=name, out_shape=jax.ShapeDtypeStruct((N_DEV,) + src.shape, src.dtype),
        in_specs=[pl.BlockSpec(memory_space=pl.ANY)], out_specs=pl.BlockSpec(memory_space=pl.ANY),
        scratch_shapes=[pltpu.SemaphoreType.DMA((N_DEV - 1,)), pltpu.SemaphoreType.DMA((N_DEV - 1,)),
                        pltpu.SemaphoreType.DMA(())],
    )(src)


def _swap_with_sibling(src, *, name):
    def body(src_ref, out_ref, send_sem, recv_sem):
        x, y, c = lax.axis_index("x"), lax.axis_index("y"), lax.axis_index("c")
        cp = pltpu.make_async_remote_copy(src_ref=src_ref.at[1 - c], dst_ref=out_ref, send_sem=send_sem,
                                          recv_sem=recv_sem, device_id=(x, y, 1 - c), device_id_type=MESH)
        cp.start()
        cp.wait()

    return pl.pallas_call(
        body, name=name, out_shape=jax.ShapeDtypeStruct(src.shape[1:], src.dtype),
        in_specs=[pl.BlockSpec(memory_space=pl.ANY)], out_specs=pl.BlockSpec(memory_space=pl.ANY),
        scratch_shapes=[pltpu.SemaphoreType.DMA(()), pltpu.SemaphoreType.DMA(())],
    )(src)


def _add_own_half(src, got, *, name):
    _, n, r, c = src.shape
    tr = _tile(r, 512, 16)

    def body(s0_ref, s1_ref, got_ref, o_ref):
        own = jnp.where(lax.axis_index("c") == 0, s0_ref[0, 0], s1_ref[0, 0])
        o_ref[0] = (own.astype(F32) + got_ref[0].astype(F32)).astype(BF16)

    half = lambda h: pl.BlockSpec((1, 1, tr, c), lambda j, i: (h, j, i, 0))
    blk = pl.BlockSpec((1, tr, c), lambda j, i: (j, i, 0))
    return pl.pallas_call(
        body, name=name, grid=(n, r // tr), in_specs=[half(0), half(1), blk], out_specs=blk,
        out_shape=jax.ShapeDtypeStruct((n, r, c), BF16), compiler_params=_params(("parallel", "parallel")),
    )(src, src, got)


def _swap_between_chips(src, *, name):
    def body(src_ref, out_ref, send_sems, recv_sems, local_sem):
        x, y, c = lax.axis_index("x"), lax.axis_index("y"), lax.axis_index("c")
        mine = 2 * x + y
        local = pltpu.make_async_copy(src_ref.at[mine], out_ref.at[mine], local_sem)
        local.start()
        copies = [pltpu.make_async_remote_copy(
            src_ref=src_ref.at[2 * px + py], dst_ref=out_ref.at[mine], send_sem=send_sems.at[j],
            recv_sem=recv_sems.at[j], device_id=(px, py, c), device_id_type=MESH) for j, (px, py) in enumerate(_other_chips())]
        for cp in copies:
            cp.start()
        for j, (px, py) in enumerate(_other_chips()):
            pltpu.make_async_remote_copy(
                src_ref=src_ref.at[mine], dst_ref=out_ref.at[2 * px + py], send_sem=send_sems.at[j],
                recv_sem=recv_sems.at[j], device_id=(px, py, c), device_id_type=MESH).wait_recv()
        for cp in copies:
            cp.wait_send()
        local.wait()

    return pl.pallas_call(
        body, name=name, out_shape=jax.ShapeDtypeStruct(src.shape, src.dtype),
        in_specs=[pl.BlockSpec(memory_space=pl.ANY)], out_specs=pl.BlockSpec(memory_space=pl.ANY),
        scratch_shapes=[pltpu.SemaphoreType.DMA((3,)), pltpu.SemaphoreType.DMA((3,)), pltpu.SemaphoreType.DMA(())],
    )(src)


def _sum_adamw(parts, w, m, v, *, name):
    n_parts, r, c = parts.shape
    tr = _tile(r, 256, 8)
    c1, c2 = 1.0 - ADAM_B1 ** ADAM_STEP, 1.0 - ADAM_B2 ** ADAM_STEP

    def body(p_ref, w_ref, m_ref, v_ref, g_ref, d_ref, nm_ref, nv_ref):
        g = p_ref[0].astype(F32)
        for s in range(1, n_parts):
            g = g + p_ref[s].astype(F32)
        nm = ADAM_B1 * m_ref[...] + (1.0 - ADAM_B1) * g
        nv = ADAM_B2 * v_ref[...] + (1.0 - ADAM_B2) * (g * g)
        g_ref[...] = g
        nm_ref[...] = nm
        nv_ref[...] = nv
        d_ref[...] = -ADAM_LR * ((nm / c1) / (jnp.sqrt(nv / c2) + ADAM_EPS) + ADAM_WD * w_ref[...])

    blk = pl.BlockSpec((tr, c), lambda i: (i, 0))
    return pl.pallas_call(
        body, name=name, grid=(r // tr,), in_specs=[pl.BlockSpec((n_parts, tr, c), lambda i: (0, i, 0)), blk, blk, blk],
        out_specs=[blk] * 4, out_shape=[jax.ShapeDtypeStruct((r, c), F32)] * 4, compiler_params=_params(("parallel",)),
    )(parts, w, m, v)


def _pack_rows(n):
    return -(-n // (PACK_COLS * PACK_ROWS)) * PACK_ROWS


def _pack(blocks):
    parts, spans, at = [], [], 0
    for blk, n_lead in blocks:
        lead = blk.shape[:n_lead]
        n = math.prod(blk.shape[n_lead:])
        rows = _pack_rows(n)
        flat = blk.reshape(lead + (n,))
        flat = jnp.pad(flat, [(0, 0)] * n_lead + [(0, rows * PACK_COLS - n)])
        parts.append(flat.reshape(lead + (rows, PACK_COLS)))
        spans.append((at, rows))
        at += rows
    return jnp.concatenate(parts, axis=-2), spans


def _unpack(buf, span, shape):
    at, rows = span
    lead = buf.shape[:-2]
    flat = lax.slice_in_dim(buf, at, at + rows, axis=buf.ndim - 2).reshape(lead + (rows * PACK_COLS,))
    return lax.slice_in_dim(flat, 0, math.prod(shape), axis=len(lead)).reshape(lead + tuple(shape))


def _join_shards(g, axis):
    g = jnp.moveaxis(g, 0, axis)
    return g.reshape(g.shape[:axis] + (g.shape[axis] * g.shape[axis + 1],) + g.shape[axis + 2:])


def _split_shards(full, axis):
    s = full.shape
    g = full.reshape(s[:axis] + (N_DEV, s[axis] // N_DEV) + s[axis + 1:])
    return jnp.moveaxis(g, axis, 0)


def _residual_out(a, w, x, scale, next_gain, name, **tiles):
    out = _mm(a, w, name=name, res=x, scale=scale, norm=next_gain, **tiles)
    return out if next_gain is not None else (out, None)


def _ffn_fwd(x, h, w_in, w_out, tag, next_gain):
    gate, up, a = _mm_swiglu(h, w_in, name=tag + "_in", tm=512, tn=1408)
    xo, hn = _residual_out(a, w_out, x, 0.5, next_gain, tag + "_out", tk=2816)
    return xo, hn, (x, h, gate, up, a)


def _ffn_bwd(saved, g, w_in, w_out, dxo, tag):
    x, h, gate, up, a = saved
    f = w_out.shape[0]
    d_w_out = _mm(a, dxo, name=tag + "_dwout", ta=True, scale=0.5, tm=1408, tk=2048)
    dgate, dup = _mm_dswiglu(dxo, w_out, gate, up, name=tag + "_da", scale=0.5, tm=512, tn=1408)
    d_w_in = jnp.concatenate([_mm(dgate, h, name=tag + "_dwin_g", ta=True, tm=1408, tk=2048),
                              _mm(dup, h, name=tag + "_dwin_u", ta=True, tm=1408, tk=2048)], axis=0)
    dh = _mm(dgate, w_in[:f], name=tag + "_dh_g", tk=2816)
    dx, dg = _mm(dup, w_in[f:], name=tag + "_dh_u", tm=512, tk=2816, res=dh, rms_bwd=(x, g, dxo))
    return dx, dg, d_w_in, d_w_out


def _block_diag(w):
    n, j, k = w.shape
    return (w[:, :, None, :] * jnp.eye(n, dtype=w.dtype)[:, None, :, None]).reshape(n * j, n * k)


def _diag_blocks(dense, n):
    j, k = dense.shape[0] // n, dense.shape[1] // n
    return jnp.stack([dense[i * j:(i + 1) * j, i * k:(i + 1) * k] for i in range(n)], axis=0)


def _attn_lru_fwd(x, h, p, tag, next_gain):
    aw = ATTN_HEADS * 64
    proj = _mm(h, p['ab_w_in'], name=tag + "_in", tb=True, tn=1280)
    qkv = proj[:, :3 * aw].astype(BF16)
    views = {dil: qkv.reshape(qkv.shape[0] // dil, dil * 3 * aw) for _, dil in DILATED_PATTERNS}
    outs, lses = [], []
    for window, dil in DILATED_PATTERNS:
        assert window // dil == ATTN_BLOCK
        o, l = _dattn_fwd(views[dil], dil, name=f"{tag}_attn{dil}")
        outs.append(o)
        lses.append(l)
    wa, wx = _block_diag(p['lru_w_a']).astype(BF16), _block_diag(p['lru_w_x']).astype(BF16)
    sp, sp_vjp = jax.vjp(lambda lam: LRU_C * jax.nn.softplus(-lam), p['lru_lambda'])
    a, b, xc = _lru_gates_fwd(proj, 3, p['lru_conv_w'], p['lru_conv_b'], wa, p['lru_b_a'], wx, p['lru_b_x'], sp,
                              name=tag + "_gates")
    hs = _scan(a, b, name=tag + "_scan")
    cat, attn, lse_all = _mix_join_fwd(outs, lses, hs, proj, 4, name=tag + "_join")
    xo, hn = _residual_out(cat, p['ab_w_out'], x, 1.0, next_gain, tag + "_out")
    return xo, hn, (x, h, proj, views, attn, lse_all, a, hs, xc, wa, wx, sp, sp_vjp, cat)


def _attn_lru_bwd(saved, p, dxo, tag):
    x, h, proj, views, attn, lse_all, a, hs, xc, wa, wx, sp, sp_vjp, cat = saved
    aw = attn.shape[1]
    g = {'ab_w_out': _mm(cat, dxo, name=tag + "_dwout", ta=True)}
    dcat = _mm(dxo, p['ab_w_out'], name=tag + "_dcat", tb=True)
    stats, dhs, dgr = _mix_join_bwd(dcat, attn, lse_all, hs, proj, 4, name=tag + "_djoin")
    a_next = jnp.concatenate([a[1:], jnp.zeros_like(a[:1])], axis=0)
    dtot = _scan(a_next, dhs, name=tag + "_dscan", reverse=True)
    h_prev = jnp.concatenate([jnp.zeros_like(hs[:1]), hs[:-1]], axis=0)
    dxc, dwa, dwx, vecs = _lru_gates_bwd(xc, dtot, h_prev, wa, p['lru_b_a'], wx, p['lru_b_x'], sp, name=tag + "_dgates")
    dxr, g['lru_conv_w'] = _conv_bwd(dxc, proj, 3, p['lru_conv_w'], name=tag + "_dconv")
    g['lru_b_a'], g['lru_b_x'], g['lru_conv_b'] = vecs[0], vecs[1], vecs[3]
    g['lru_lambda'], = sp_vjp(vecs[2])
    g['lru_w_a'], g['lru_w_x'] = _diag_blocks(dwa, LRU_BLOCKS), _diag_blocks(dwx, LRU_BLOCKS)
    dattn = dcat[:, :aw]
    dqkv = sum(_dattn_bwd(views[dil], dattn, stats, dil, name=f"{tag}_dattn{dil}").astype(F32)
               for _, dil in DILATED_PATTERNS)
    dproj = jnp.concatenate([dqkv.astype(BF16), dxr, dgr], axis=-1)
    g['ab_w_in'] = _mm(dproj, h, name=tag + "_dwin", ta=True, tm=1280)
    dx, g['mix_norm'] = _mm(dproj, p['ab_w_in'], name=tag + "_dh", tm=512, tk=2560, rms_bwd=(x, p['mix_norm'], dxo))
    return dx, g


def _dn_decay(a, b, a_log, dt_bias):
    t = a.shape[0]
    g = -jnp.exp(a_log) * jax.nn.softplus(a + dt_bias)
    gc = jnp.cumsum(g.reshape(t // DN_CHUNK, DN_CHUNK, DN_HEADS), axis=1)
    return gc.reshape(t, DN_HEADS), jnp.swapaxes(gc, 1, 2), jax.nn.sigmoid(b)


def _dn_in_width(w):
    return -(-(4 * w + 2 * DN_HEADS) // LANES) * LANES


def _deltanet_fwd(x, h, p, tag, next_gain):
    w = p['dn_w_out'].shape[0]
    proj = _mm(h, p['dn_w_in'], name=tag + "_in", tb=True, tn=1408)
    q, k, v, c = _dn_prep_fwd(proj, p['dn_conv_w'], name=tag + "_prep")
    a, b = proj[:, 4 * w:4 * w + DN_HEADS], proj[:, 4 * w + DN_HEADS:4 * w + 2 * DN_HEADS]
    (gc, gcr, beta), decay_vjp = jax.vjp(_dn_decay, a, b, p['dn_a_log'], p['dn_dt_bias'])
    prep = (q, k, v, gc, gcr, beta)
    u, wq, wqt, kr, krt, qk, qkt, invt = _dn_local_fwd(*prep, name=tag + "_local")
    o, vn, states = _dn_state_fwd(u, wq, krt, qk, gc, name=tag + "_state")
    og = _dn_gate_fwd(o, proj, 3, p['dn_o_norm'], name=tag + "_gate")
    xo, hn = _residual_out(og, p['dn_w_out'], x, 1.0, next_gain, tag + "_out")
    return xo, hn, (x, h, proj, c, o, prep, (u, wq, wqt, kr, qkt, invt, vn, states), decay_vjp, og)


def _deltanet_bwd(saved, p, dxo, tag):
    x, h, proj, c, o, prep, (u, wq, wqt, kr, qkt, invt, vn, states), decay_vjp, og = saved
    g = {'dn_w_out': _mm(og, dxo, name=tag + "_dwout", ta=True)}
    dog = _mm(dxo, p['dn_w_out'], name=tag + "_dog", tb=True)
    do, dz, g['dn_o_norm'] = _dn_gate_bwd(o, proj, 3, p['dn_o_norm'], dog, name=tag + "_dgate")
    dvn, dkr, d_el = _dn_state_bwd(do, qkt, kr, vn, states, wqt, prep[3], name=tag + "_dstate")
    dq, dk, dv, dgc, dgr, dbeta = _dn_local_bwd(*prep, invt, u, wq, vn, states, do, dvn, dkr, d_el, name=tag + "_dlocal")
    dc = _dn_prep_bwd(c, dq, dk, dv, name=tag + "_dprep")
    dqkv, g['dn_conv_w'] = _conv_bwd(dc, proj, 0, p['dn_conv_w'], name=tag + "_dconv")
    da, db, g['dn_a_log'], g['dn_dt_bias'] = decay_vjp((dgc, dgr, dbeta))
    t = x.shape[0]
    pad = jnp.zeros((t, p['dn_w_in'].shape[0] - dqkv.shape[1] - dz.shape[1] - 2 * DN_HEADS), BF16)
    groups = [("qkv", dqkv), ("z", dz), ("ab", jnp.concatenate([da.astype(BF16), db.astype(BF16), pad], axis=-1))]
    parts, dh, lo = [], None, 0
    for label, piece in groups:
        hi = lo + piece.shape[1]
        parts.append(_mm(piece, h, name=f"{tag}_dwin_{label}", ta=True, tm=1536, tk=2048))
        last = dict(tm=512, rms_bwd=(x, p['mix_norm'], dxo)) if label == groups[-1][0] else {}
        dh = _mm(piece, p['dn_w_in'][lo:hi], name=f"{tag}_dh_{label}", tk=1536, res=dh, **last)
        lo = hi
    g['dn_w_in'] = jnp.concatenate(parts, axis=0)
    dx, g['mix_norm'] = dh
    return dx, g


def _xattn_block_fwd(x, h, mem, p, tag, next_gain):
    w = x.shape[1]
    mh = _rms_fwd(mem, p['xa_mem_norm'], name=tag + "_mnorm", out_dtype=BF16)
    q = _mm(h, p['xa_wq'], name=tag + "_q")
    kv = _mm(mh, p['xa_wkv'], name=tag + "_kv", tb=True)
    k, v = kv[:, :w], kv[:, w:]
    o = _xattn_fwd(q, k, v, name=tag + "_attn").astype(BF16)
    xo, hn = _residual_out(o, p['xa_wo'], x, 1.0, next_gain, tag + "_out")
    return xo, hn, (x, h, mh, q, k, v, o)


def _xattn_block_bwd(saved, mem, p, dxo, tag):
    x, h, mh, q, k, v, o = saved
    g = {'xa_wo': _mm(o, dxo, name=tag + "_dwo", ta=True)}
    do = _mm(dxo, p['xa_wo'], name=tag + "_do", tb=True)
    dq, dk, dv = _xattn_bwd(q, k, v, do, name=tag + "_dattn")
    dq = dq.astype(BF16)
    dkv = jnp.concatenate([dk, dv], axis=-1).astype(BF16)
    g['xa_wq'] = _mm(h, dq, name=tag + "_dwq", ta=True)
    g['xa_wkv'] = _mm(dkv, mh, name=tag + "_dwkv", ta=True)
    dmh = _mm(dkv, p['xa_wkv'], name=tag + "_dmh")
    _, g['xa_mem_norm'] = _rms_bwd(mem, p['xa_mem_norm'], dmh, None, name=tag + "_dmnorm")
    dx, g['xa_norm'] = _mm(dq, p['xa_wq'], name=tag + "_dh", tb=True, tm=512, rms_bwd=(x, p['xa_norm'], dxo))
    return dx, g


def _layer_params(full, layer):
    p = {n: full[n][layer] for n in ('ffn1_norm', 'ffn1_w_in', 'ffn1_w_out', 'mix_norm', 'xa_norm', 'xa_mem_norm',
                                     'xa_wq', 'xa_wkv', 'xa_wo', 'ffn2_norm', 'ffn2_w_in', 'ffn2_w_out')}
    mixer = ('ab_w_in', 'lru_conv_w', 'lru_conv_b', 'lru_w_a', 'lru_b_a', 'lru_w_x', 'lru_b_x', 'lru_lambda', 'ab_w_out') \
        if layer % 2 == 0 else ('dn_w_in', 'dn_conv_w', 'dn_a_log', 'dn_dt_bias', 'dn_o_norm', 'dn_w_out')
    p.update({n: full[n][layer // 2] for n in mixer})
    return p


def _step(x, mem, target, full, depth):
    saved = []
    params = [_layer_params(full, layer) for layer in range(depth)]
    h = _rms_fwd(x, params[0]['ffn1_norm'], name="l0_ffn1_norm", out_dtype=BF16)
    for layer, p in enumerate(params):
        tag = f"l{layer}"
        after = params[layer + 1]['ffn1_norm'] if layer + 1 < depth else None
        x, h, s1 = _ffn_fwd(x, h, p['ffn1_w_in'], p['ffn1_w_out'], tag + "_ffn1", p['mix_norm'])
        x, h, s2 = (_attn_lru_fwd if layer % 2 == 0 else _deltanet_fwd)(x, h, p, tag + "_mix", p['xa_norm'])
        x, h, s3 = _xattn_block_fwd(x, h, mem, p, tag + "_xa", p['ffn2_norm'])
        x, h, s4 = _ffn_fwd(x, h, p['ffn2_w_in'], p['ffn2_w_out'], tag + "_ffn2", after)
        saved.append((p, s1, s2, s3, s4))
    sq, dx, d_final = _final_loss(x, full['final_norm'], target, name="final_loss")
    per_layer = []
    for layer in reversed(range(depth)):
        p, s1, s2, s3, s4 = saved[layer]
        tag = f"l{layer}"
        g = {}
        dx, g['ffn2_norm'], g['ffn2_w_in'], g['ffn2_w_out'] = _ffn_bwd(s4, p['ffn2_norm'], p['ffn2_w_in'], p['ffn2_w_out'], dx, tag + "_ffn2")
        dx, gx = _xattn_block_bwd(s3, mem, p, dx, tag + "_xa")
        dx, gm = (_attn_lru_bwd if layer % 2 == 0 else _deltanet_bwd)(s2, p, dx, tag + "_mix")
        dx, g['ffn1_norm'], g['ffn1_w_in'], g['ffn1_w_out'] = _ffn_bwd(s1, p['ffn1_norm'], p['ffn1_w_in'], p['ffn1_w_out'], dx, tag + "_ffn1")
        g.update(gx)
        g.update(gm)
        per_layer.insert(0, g)
    grads = {n: [g[n] for g in per_layer if n in g] for n in WEIGHTS[:-1]}
    grads['final_norm'] = d_final
    return sq, dx, grads


def kernel(x, mem, ffn1_norm, ffn1_w_in, ffn1_w_out, mix_norm, xa_norm, xa_mem_norm, xa_wq, xa_wkv, xa_wo, ffn2_norm, ffn2_w_in, ffn2_w_out, ab_w_in, lru_conv_w, lru_conv_b, lru_w_a, lru_b_a, lru_w_x, lru_b_x, lru_lambda, ab_w_out, dn_w_in, dn_conv_w, dn_a_log, dn_dt_bias, dn_o_norm, dn_w_out, final_norm, loss_target, m_ffn1_norm, m_ffn1_w_in, m_ffn1_w_out, m_mix_norm, m_xa_norm, m_xa_mem_norm, m_xa_wq, m_xa_wkv, m_xa_wo, m_ffn2_norm, m_ffn2_w_in, m_ffn2_w_out, m_ab_w_in, m_lru_conv_w, m_lru_conv_b, m_lru_w_a, m_lru_b_a, m_lru_w_x, m_lru_b_x, m_lru_lambda, m_ab_w_out, m_dn_w_in, m_dn_conv_w, m_dn_a_log, m_dn_dt_bias, m_dn_o_norm, m_dn_w_out, m_final_norm, v_ffn1_norm, v_ffn1_w_in, v_ffn1_w_out, v_mix_norm, v_xa_norm, v_xa_mem_norm, v_xa_wq, v_xa_wkv, v_xa_wo, v_ffn2_norm, v_ffn2_w_in, v_ffn2_w_out, v_ab_w_in, v_lru_conv_w, v_lru_conv_b, v_lru_w_a, v_lru_b_a, v_lru_w_x, v_lru_b_x, v_lru_lambda, v_ab_w_out, v_dn_w_in, v_dn_conv_w, v_dn_a_log, v_dn_dt_bias, v_dn_o_norm, v_dn_w_out, v_final_norm):
    args = dict(locals())
    flip = lambda n, a: jnp.swapaxes(a, 1, 2) if n in TRANSPOSED else a
    local = {n: flip(n, args[n]) for n in WEIGHTS}
    depth = ffn1_norm.shape[0]
    matrices = [n for n in WEIGHTS if n in SHARD_AXIS and n not in GATHER_F32]

    def entries(get):
        return [(a, n, l) for n in WEIGHTS for l, a in
                (enumerate(get(n)) if n in matrices else [(None, get(n))])]

    mats = [e for e in entries(lambda n: local[n]) if e[1] in matrices]
    send16, spans16 = _pack([(a.astype(BF16), 0) for a, _, _ in mats])
    send32, spans32 = _pack([(local[n], 0) for n in GATHER_F32])
    got16 = _gather(send16, name="gather_matrices")
    got32 = _gather(send32, name="gather_filters")
    full = {n: ([None] * local[n].shape[0] if n in matrices else local[n]) for n in WEIGHTS}
    for (a, n, l), span in zip(mats, spans16):
        full[n][l] = _unpack(got16, span, a.shape).reshape(N_DEV * a.shape[0], a.shape[1])
    for n, span in zip(GATHER_F32, spans32):
        full[n] = _join_shards(_unpack(got32, span, local[n].shape), SHARD_AXIS[n])
    dn_rows = full['dn_w_in'][0].shape[0]
    full['dn_w_in'] = [jnp.pad(w, ((0, _dn_in_width(w.shape[1]) - dn_rows), (0, 0))) for w in full['dn_w_in']]

    sq, dx, grads = _step(x[0], mem[0], loss_target[0], full, depth)
    grads['dn_w_in'] = [g[:dn_rows] for g in grads['dn_w_in']]
    loss = lax.psum(0.5 * jnp.sum(sq) / x.shape[2], ("x", "y", "c"))

    by_core = lambda z: jnp.swapaxes(z.reshape((N_DEV // 2, 2) + z.shape[1:]), 0, 1)

    def contribution(n):
        if n in matrices:
            return [by_core(g.reshape((N_DEV, g.shape[0] // N_DEV, g.shape[1]))) for g in grads[n]]
        g = grads[n] if n == 'final_norm' else jnp.stack(grads[n], axis=0)
        return by_core(_split_shards(g, SHARD_AXIS[n]) if n in SHARD_AXIS else jnp.broadcast_to(g, (N_DEV,) + g.shape))

    send, spans = _pack([(a.astype(BF16), 2) for a, _, _ in entries(contribution)])
    got = _swap_with_sibling(send, name="grads_to_sibling")
    chip_sum = _add_own_half(send, got, name="grads_chip_sum")
    parts = _swap_between_chips(chip_sum, name="grads_between_chips")
    state = [_pack([(a, 0) for a, _, _ in entries(lambda n: flip(n, args[pre + n]))])[0] for pre in ("", "m_", "v_")]
    outs = _sum_adamw(parts, *state, name="sum_adamw")
    result = []
    for o in outs:
        got_rows = {}
        for (a, n, l), span in zip(entries(lambda n: local[n]), spans):
            got_rows.setdefault(n, []).append(_unpack(o, span, a.shape))
        result += [flip(n, jnp.stack(got_rows[n], axis=0) if n in matrices else got_rows[n][0]) for n in WEIGHTS]
    return (loss, dx[None], *result)
```

```python
import math

import jax
import jax.numpy as jnp
from jax import lax
from jax.experimental import pallas as pl
from jax.experimental.pallas import tpu as pltpu

F32, BF16 = jnp.float32, jnp.bfloat16
MESH = pl.DeviceIdType.MESH
N_DEV = 8
V7X_VMEM_LIMIT = 56 << 20
LANES = 128
PACK_COLS = 1024
PACK_ROWS = 16
NEG = -1e30

NORM_EPS = 1e-6
CONV_K = 4
ATTN_HEADS = 8
DILATED_PATTERNS = ((128, 1), (512, 4), (2048, 16))
ATTN_BLOCK = 128
LRU_BLOCKS = 8
LRU_C = 8.0
DN_HEADS = 8
DN_CHUNK = 64
DN_STEP_CHUNKS = 4
DN_LOCAL_CHUNKS = 4
XA_HEADS = 4
ADAM_LR, ADAM_B1, ADAM_B2, ADAM_EPS, ADAM_WD, ADAM_STEP = 0.001, 0.9, 0.999, 1e-08, 0.01, 10

WEIGHTS = ['ffn1_norm', 'ffn1_w_in', 'ffn1_w_out', 'mix_norm', 'xa_norm', 'xa_mem_norm', 'xa_wq', 'xa_wkv', 'xa_wo',
           'ffn2_norm', 'ffn2_w_in', 'ffn2_w_out', 'ab_w_in', 'lru_conv_w', 'lru_conv_b', 'lru_w_a', 'lru_b_a',
           'lru_w_x', 'lru_b_x', 'lru_lambda', 'ab_w_out', 'dn_w_in', 'dn_conv_w', 'dn_a_log', 'dn_dt_bias',
           'dn_o_norm', 'dn_w_out', 'final_norm']
SHARD_AXIS = {'ffn1_w_in': 2, 'ffn1_w_out': 1, 'xa_wq': 1, 'xa_wkv': 2, 'xa_wo': 1, 'ffn2_w_in': 2, 'ffn2_w_out': 1,
              'ab_w_in': 2, 'lru_conv_w': 2, 'ab_w_out': 1, 'dn_w_in': 2, 'dn_conv_w': 2, 'dn_w_out': 1}
GATHER_F32 = ('lru_conv_w', 'dn_conv_w')
TRANSPOSED = ('ffn1_w_in', 'xa_wkv', 'ffn2_w_in', 'ab_w_in', 'dn_w_in')


def _params(sem=None):
    return pltpu.CompilerParams(dimension_semantics=sem, vmem_limit_bytes=V7X_VMEM_LIMIT)


def _tile(n, pref, mult):
    best = None
    t = mult
    while t <= min(n, pref):
        if n % t == 0:
            best = t
        t += mult
    return n if best is None else best


def _dot(a, b, ca, cb, prec=None):
    return lax.dot_general(a, b, (((ca,), (cb,)), ((), ())), preferred_element_type=F32, precision=prec)


def _mm(a, b, *, name, ta=False, tb=False, out_dtype=F32, res=None, scale=1.0, tm=1024, tn=1024, tk=1024,
        norm=None, rms_bwd=None):
    m, kdim = (a.shape[1], a.shape[0]) if ta else a.shape
    n = b.shape[0] if tb else b.shape[1]
    assert (b.shape[1] if tb else b.shape[0]) == kdim
    tm = _tile(m, tm, LANES if ta else 16)
    tn = _tile(n, tn, LANES)
    tk = _tile(kdim, tk, LANES)
    nk = kdim // tk
    a_spec = pl.BlockSpec((tk, tm), lambda i, j, k: (k, i)) if ta else pl.BlockSpec((tm, tk), lambda i, j, k: (i, k))
    b_spec = pl.BlockSpec((tn, tk), lambda i, j, k: (j, k)) if tb else pl.BlockSpec((tk, tn), lambda i, j, k: (k, j))
    o_spec = pl.BlockSpec((tm, tn), lambda i, j, k: (i, j))
    vec = pl.BlockSpec((1, tn), lambda i, j, k: (0, j))
    ca, cb = (0 if ta else 1), (1 if tb else 0)
    has_res = res is not None
    assert (norm is None and rms_bwd is None) or tn == n

    ins, specs = [a, b], [a_spec, b_spec]
    if has_res:
        ins.append(res)
        specs.append(o_spec)
    out_specs, out_shape = [o_spec], [jax.ShapeDtypeStruct((m, n), out_dtype)]
    if norm is not None:
        ins.append(norm.reshape(1, n))
        specs.append(vec)
        out_specs.append(o_spec)
        out_shape.append(jax.ShapeDtypeStruct((m, n), BF16))
    if rms_bwd is not None:
        ins += [rms_bwd[0], rms_bwd[1].reshape(1, n), rms_bwd[2]]
        specs += [o_spec, vec, o_spec]
        out_specs.append(vec)
        out_shape.append(jax.ShapeDtypeStruct((1, n), F32))
    n_in, n_out = len(ins), len(out_specs)

    def finish(acc, extra, outs, first_rows):
        y = acc if scale == 1.0 else acc * scale
        if has_res:
            y = y + extra[0][...]
        tail = extra[has_res:]
        if norm is not None:
            rs = lax.rsqrt(jnp.mean(y * y, axis=-1, keepdims=True) + NORM_EPS)
            outs[1][...] = (y * rs * tail[0][...]).astype(BF16)
        if rms_bwd is not None:
            x_ref, g_ref, dres_ref = tail
            xv = x_ref[...]
            rs = lax.rsqrt(jnp.mean(xv * xv, axis=-1, keepdims=True) + NORM_EPS)
            xh = xv * rs
            dgh = y * g_ref[...]
            part = jnp.sum(y * xh, axis=0, keepdims=True)
            y = dres_ref[...] + rs * (dgh - xh * jnp.mean(dgh * xh, axis=-1, keepdims=True))

            @pl.when(first_rows)
            def _():
                outs[1][...] = part

            @pl.when(jnp.logical_not(first_rows))
            def _():
                outs[1][...] += part

        outs[0][...] = y.astype(out_dtype)

    def body(*refs):
        a_ref, b_ref = refs[0], refs[1]
        extra, outs = refs[2:n_in], refs[n_in:n_in + n_out]
        p = _dot(a_ref[...].astype(BF16), b_ref[...].astype(BF16), ca, cb)
        first_rows = pl.program_id(0) == 0
        if nk == 1:
            finish(p, extra, outs, first_rows)
            return
        acc = refs[n_in + n_out]
        k = pl.program_id(2)

        @pl.when(k == 0)
        def _():
            acc[...] = p

        @pl.when(k > 0)
        def _():
            acc[...] += p

        @pl.when(k == nk - 1)
        def _():
            finish(acc[...], extra, outs, first_rows)

    out = pl.pallas_call(
        body, name=name, grid=(m // tm, n // tn, nk), in_specs=specs, out_specs=out_specs, out_shape=out_shape,
        scratch_shapes=[] if nk == 1 else [pltpu.VMEM((tm, tn), F32)],
        compiler_params=_params(("parallel" if rms_bwd is None else "arbitrary", "parallel", "arbitrary")),
    )(*ins)
    if rms_bwd is not None:
        return out[0], out[1].reshape(n)
    return out[0] if n_out == 1 else tuple(out)


def _rms_fwd(x, g, *, name, out_dtype):
    r, d = x.shape
    tr = _tile(r, 512, 16)

    def body(x_ref, g_ref, o_ref):
        xv = x_ref[...]
        rs = lax.rsqrt(jnp.mean(xv * xv, axis=-1, keepdims=True) + NORM_EPS)
        o_ref[...] = (xv * rs * g_ref[...]).astype(out_dtype)

    return pl.pallas_call(
        body, name=name, grid=(r // tr,),
        in_specs=[pl.BlockSpec((tr, d), lambda i: (i, 0)), pl.BlockSpec((1, d), lambda i: (0, 0))],
        out_specs=pl.BlockSpec((tr, d), lambda i: (i, 0)),
        out_shape=jax.ShapeDtypeStruct((r, d), out_dtype), compiler_params=_params(("parallel",)),
    )(x, g.reshape(1, d))


def _rms_bwd(x, g, dh, dres, *, name):
    r, d = x.shape
    tr = _tile(r, 512, 8)
    has_res = dres is not None

    def body(*refs):
        x_ref, g_ref, dh_ref = refs[:3]
        r_ref = refs[3] if has_res else None
        dx_ref, dg_ref = refs[3 + has_res], refs[4 + has_res]
        xv = x_ref[...]
        rs = lax.rsqrt(jnp.mean(xv * xv, axis=-1, keepdims=True) + NORM_EPS)
        xh = xv * rs
        dhv = dh_ref[...]
        dgh = dhv * g_ref[...]
        dx = rs * (dgh - xh * jnp.mean(dgh * xh, axis=-1, keepdims=True))
        if has_res:
            dx = dx + r_ref[...]
        dx_ref[...] = dx
        part = jnp.sum(dhv * xh, axis=0, keepdims=True)

        @pl.when(pl.program_id(0) == 0)
        def _():
            dg_ref[...] = part

        @pl.when(pl.program_id(0) > 0)
        def _():
            dg_ref[...] += part

    row = pl.BlockSpec((tr, d), lambda i: (i, 0))
    vec = pl.BlockSpec((1, d), lambda i: (0, 0))
    ins, specs = [x, g.reshape(1, d), dh], [row, vec, row]
    if has_res:
        ins.append(dres)
        specs.append(row)
    dx, dg = pl.pallas_call(
        body, name=name, grid=(r // tr,), in_specs=specs, out_specs=[row, vec],
        out_shape=[jax.ShapeDtypeStruct((r, d), F32), jax.ShapeDtypeStruct((1, d), F32)],
        compiler_params=_params(("arbitrary",)),
    )(*ins)
    return dx, dg.reshape(d)


def _mm_swiglu(h, w_in, *, name, tm=1024, tn=256):
    t, d = h.shape
    f = w_in.shape[0] // 2
    tm, tn = _tile(t, tm, 16), _tile(f, tn, LANES)
    nj = f // tn

    def body(h_ref, wg_ref, wu_ref, g_ref, u_ref, a_ref):
        hv = h_ref[...]
        gate, up = _dot(hv, wg_ref[...], 1, 1), _dot(hv, wu_ref[...], 1, 1)
        g_ref[...] = gate.astype(BF16)
        u_ref[...] = up.astype(BF16)
        a_ref[...] = (gate * jax.nn.sigmoid(gate) * up).astype(BF16)

    out = pl.BlockSpec((tm, tn), lambda j, i: (i, j))
    return pl.pallas_call(
        body, name=name, grid=(nj, t // tm),
        in_specs=[pl.BlockSpec((tm, d), lambda j, i: (i, 0)), pl.BlockSpec((tn, d), lambda j, i: (j, 0)),
                  pl.BlockSpec((tn, d), lambda j, i: (j + nj, 0))],
        out_specs=[out, out, out], out_shape=[jax.ShapeDtypeStruct((t, f), BF16)] * 3,
        compiler_params=_params(("parallel", "parallel")),
    )(h, w_in, w_in)


def _mm_dswiglu(dy, w_out, gate, up, *, name, scale, tm=1024, tn=256):
    t, d = dy.shape
    f = w_out.shape[0]
    tm, tn = _tile(t, tm, 16), _tile(f, tn, LANES)

    def body(dy_ref, w_ref, g_ref, u_ref, dg_ref, du_ref):
        da = _dot(dy_ref[...].astype(BF16), w_ref[...], 1, 1) * scale
        gv = g_ref[...].astype(F32)
        s = jax.nn.sigmoid(gv)
        dg_ref[...] = (da * u_ref[...].astype(F32) * (s * (1.0 + gv * (1.0 - s)))).astype(BF16)
        du_ref[...] = (da * gv * s).astype(BF16)

    blk = pl.BlockSpec((tm, tn), lambda j, i: (i, j))
    return pl.pallas_call(
        body, name=name, grid=(f // tn, t // tm),
        in_specs=[pl.BlockSpec((tm, d), lambda j, i: (i, 0)), pl.BlockSpec((tn, d), lambda j, i: (j, 0)), blk, blk],
        out_specs=[blk, blk], out_shape=[jax.ShapeDtypeStruct((t, f), BF16)] * 2,
        compiler_params=_params(("parallel", "parallel")),
    )(dy, w_out, gate, up)


def _final_loss(x, g, target, *, name):
    r, d = x.shape
    tr = _tile(r, 512, 8)

    def body(x_ref, g_ref, t_ref, sq_ref, dx_ref, dg_ref):
        xv = x_ref[...]
        gv = g_ref[...]
        rs = lax.rsqrt(jnp.mean(xv * xv, axis=-1, keepdims=True) + NORM_EPS)
        xh = xv * rs
        err = xh * gv - t_ref[...]
        dy = err * (1.0 / d)
        dgh = dy * gv
        dx_ref[...] = rs * (dgh - xh * jnp.mean(dgh * xh, axis=-1, keepdims=True))
        sq = jnp.sum(err * err, axis=0, keepdims=True)
        part = jnp.sum(dy * xh, axis=0, keepdims=True)

        @pl.when(pl.program_id(0) == 0)
        def _():
            sq_ref[...] = sq
            dg_ref[...] = part

        @pl.when(pl.program_id(0) > 0)
        def _():
            sq_ref[...] += sq
            dg_ref[...] += part

    row = pl.BlockSpec((tr, d), lambda i: (i, 0))
    vec = pl.BlockSpec((1, d), lambda i: (0, 0))
    sq, dx, dg = pl.pallas_call(
        body, name=name, grid=(r // tr,), in_specs=[row, vec, row], out_specs=[vec, row, vec],
        out_shape=[jax.ShapeDtypeStruct((1, d), F32), jax.ShapeDtypeStruct((r, d), F32),
                   jax.ShapeDtypeStruct((1, d), F32)],
        compiler_params=_params(("arbitrary",)),
    )(x, g.reshape(1, d), target)
    return sq.reshape(d), dx, dg.reshape(d)


def _scan(a, b, *, name, reverse=False):
    t, w = a.shape
    tb = _tile(t, 1024, 8)
    nblk, ngrp = t // tb, tb // 8

    def body(a_ref, b_ref, h_ref, carry):
        @pl.when(pl.program_id(0) == 0)
        def _():
            carry[...] = jnp.zeros_like(carry)

        row = lax.broadcasted_iota(jnp.int32, (8, w), 0)

        def group(i, c):
            r0 = pl.multiple_of((ngrp - 1 - i if reverse else i) * 8, 8)
            av, bv = a_ref[pl.ds(r0, 8), :], b_ref[pl.ds(r0, 8), :]
            for s in (1, 2, 4):
                keep = row < 8 - s if reverse else row >= s
                shift = 8 - s if reverse else s
                bv = jnp.where(keep, bv + av * pltpu.roll(bv, shift, axis=0), bv)
                av = jnp.where(keep, av * pltpu.roll(av, shift, axis=0), av)
            hv = bv + av * c
            h_ref[pl.ds(r0, 8), :] = hv
            return hv[0:1, :] if reverse else hv[7:8, :]

        carry[0:1, :] = lax.fori_loop(0, ngrp, group, carry[0:1, :])

    blk = pl.BlockSpec((tb, w), lambda i: (nblk - 1 - i if reverse else i, 0))
    return pl.pallas_call(
        body, name=name, grid=(t // tb,), in_specs=[blk, blk], out_specs=blk,
        out_shape=jax.ShapeDtypeStruct((t, w), F32), scratch_shapes=[pltpu.VMEM((8, w), F32)],
        compiler_params=_params(("arbitrary",)),
    )(a, b)


def _row_specs(t, tb, width, col):
    per = tb // 8
    main = pl.BlockSpec((tb, width), lambda i: (i, col))
    before = pl.BlockSpec((8, width), lambda i: (jnp.maximum(i * per - 1, 0), col))
    after = pl.BlockSpec((8, width), lambda i: (jnp.minimum((i + 1) * per, t // 8 - 1), col))
    return main, before, after


def _with_rows_before(x_ref, before_ref):
    return jnp.concatenate([jnp.where(pl.program_id(0) > 0, before_ref[...], 0.0), x_ref[...]], axis=0)


def _tap(xe, s):
    return xe[8:] if s == 0 else pltpu.roll(xe, s, axis=0)[8:]


def _conv_rows(xe, w_ref):
    return sum(_tap(xe, CONV_K - 1 - k) * w_ref[k:k + 1, :] for k in range(CONV_K))


def _conv_bwd(dy, x, col, w, *, name):
    t, width = dy.shape
    tb = _tile(t, 256, 8)
    nblk = t // tb

    def body(dy_ref, dy_after_ref, x_ref, x_before_ref, w_ref, dx_ref, dw_ref):
        i = pl.program_id(0)
        dyv = dy_ref[...]
        dye = jnp.concatenate([dyv, jnp.where(i < nblk - 1, dy_after_ref[...], 0.0)], axis=0)
        dx = dyv * w_ref[CONV_K - 1:CONV_K, :]
        for s in range(1, CONV_K):
            dx = dx + pltpu.roll(dye, tb + 8 - s, axis=0)[:tb] * w_ref[CONV_K - 1 - s:CONV_K - s, :]
        dx_ref[...] = dx.astype(BF16)
        xe = _with_rows_before(x_ref, x_before_ref)

        @pl.when(i == 0)
        def _():
            dw_ref[...] = jnp.zeros_like(dw_ref)

        for k in range(CONV_K):
            dw_ref[k:k + 1, :] += jnp.sum(dyv * _tap(xe, CONV_K - 1 - k), axis=0, keepdims=True)

    main, _, after = _row_specs(t, tb, width, 0)
    xmain, xbefore, _ = _row_specs(t, tb, width, col)
    dx, dw = pl.pallas_call(
        body, name=name, grid=(nblk,),
        in_specs=[main, after, xmain, xbefore, pl.BlockSpec((CONV_K, width), lambda i: (0, 0))],
        out_specs=[main, pl.BlockSpec((8, width), lambda i: (0, 0))],
        out_shape=[jax.ShapeDtypeStruct((t, width), BF16), jax.ShapeDtypeStruct((8, width), F32)],
        compiler_params=_params(("arbitrary",)),
    )(dy, dy, x, x, w)
    return dx, dw[:CONV_K]


def _expm1(x):
    small = x * (1.0 + x * (0.5 + x * (1.0 / 6.0 + x * (1.0 / 24.0 + x * (1.0 / 120.0 + x * (1.0 / 720.0))))))
    return jnp.where(jnp.abs(x) < 0.1, small, jnp.exp(x) - 1.0)


def _lru_gate_terms(xc, wa_ref, ba_ref, wx_ref, bx_ref, sp_ref):
    xb = xc.astype(BF16)
    r = jax.nn.sigmoid(_dot(xb, wa_ref[...], 1, 0) + ba_ref[...])
    i = jax.nn.sigmoid(_dot(xb, wx_ref[...], 1, 0) + bx_ref[...])
    log_a = -r * sp_ref[...]
    return r, i, jnp.exp(log_a), jnp.sqrt(-_expm1(2.0 * log_a))


def _lru_gates_fwd(proj, col, conv_w, conv_b, wa, ba, wx, bx, sp, *, name):
    t = proj.shape[0]
    width = conv_w.shape[1]
    tb = _tile(t, 512, 8)

    def body(x_ref, x_before_ref, cw_ref, cb_ref, wa_ref, ba_ref, wx_ref, bx_ref, sp_ref, a_ref, b_ref, xc_ref):
        xc = _conv_rows(_with_rows_before(x_ref, x_before_ref), cw_ref) + cb_ref[...]
        r, i, a, mult = _lru_gate_terms(xc, wa_ref, ba_ref, wx_ref, bx_ref, sp_ref)
        a_ref[...] = a
        b_ref[...] = mult * i * xc
        xc_ref[...] = xc

    main, before, _ = _row_specs(t, tb, width, col)
    out = pl.BlockSpec((tb, width), lambda i: (i, 0))
    vec = pl.BlockSpec((1, width), lambda i: (0, 0))
    mat = pl.BlockSpec((width, width), lambda i: (0, 0))
    return pl.pallas_call(
        body, name=name, grid=(t // tb,),
        in_specs=[main, before, pl.BlockSpec((CONV_K, width), lambda i: (0, 0)), vec, mat, vec, mat, vec, vec],
        out_specs=[out] * 3, out_shape=[jax.ShapeDtypeStruct((t, width), F32)] * 3,
        compiler_params=_params(("parallel",)),
    )(proj, proj, conv_w, conv_b.reshape(1, -1), wa, ba.reshape(1, -1), wx, bx.reshape(1, -1), sp.reshape(1, -1))


def _lru_gates_bwd(xc, dtot, h_prev, wa, ba, wx, bx, sp, *, name):
    t, width = xc.shape
    tb = _tile(t, 512, 8)

    def body(xc_ref, dt_ref, hp_ref, wa_ref, ba_ref, wx_ref, bx_ref, sp_ref, dxc_ref, dwa_ref, dwx_ref, vec_ref):
        xc = xc_ref[...]
        r, i, a, mult = _lru_gate_terms(xc, wa_ref, ba_ref, wx_ref, bx_ref, sp_ref)
        db = dt_ref[...]
        d_la = db * hp_ref[...] * a - db * i * xc * (a * a / mult)
        d_pa = (-d_la * sp_ref[...]) * r * (1.0 - r)
        d_pi = db * mult * xc * i * (1.0 - i)
        dab, dib = d_pa.astype(BF16), d_pi.astype(BF16)
        dxc = db * mult * i + _dot(dab, wa_ref[...], 1, 1) + _dot(dib, wx_ref[...], 1, 1)
        dxc_ref[...] = dxc
        xb = xc.astype(BF16)
        rows = [jnp.sum(z, axis=0, keepdims=True) for z in (d_pa, d_pi, -d_la * r, dxc)]

        @pl.when(pl.program_id(0) == 0)
        def _():
            dwa_ref[...] = jnp.zeros_like(dwa_ref)
            dwx_ref[...] = jnp.zeros_like(dwx_ref)
            vec_ref[...] = jnp.zeros_like(vec_ref)

        dwa_ref[...] += _dot(xb, dab, 0, 0)
        dwx_ref[...] += _dot(xb, dib, 0, 0)
        for j, z in enumerate(rows):
            vec_ref[j:j + 1, :] += z

    blk = pl.BlockSpec((tb, width), lambda i: (i, 0))
    vec = pl.BlockSpec((1, width), lambda i: (0, 0))
    mat = pl.BlockSpec((width, width), lambda i: (0, 0))
    dxc, dwa, dwx, vecs = pl.pallas_call(
        body, name=name, grid=(t // tb,), in_specs=[blk, blk, blk, mat, vec, mat, vec, vec],
        out_specs=[blk, mat, mat, pl.BlockSpec((8, width), lambda i: (0, 0))],
        out_shape=[jax.ShapeDtypeStruct((t, width), F32), jax.ShapeDtypeStruct((width, width), F32),
                   jax.ShapeDtypeStruct((width, width), F32), jax.ShapeDtypeStruct((8, width), F32)],
        compiler_params=_params(("arbitrary",)),
    )(xc, dtot, h_prev, wa, ba.reshape(1, -1), wx, bx.reshape(1, -1), sp.reshape(1, -1))
    return dxc, dwa, dwx, vecs[:4]


GELU_C = math.sqrt(2.0 / math.pi)


def _gelu_terms(x):
    th = jnp.tanh(GELU_C * (x + 0.044715 * x * x * x))
    return 0.5 * x * (1.0 + th), 0.5 * (1.0 + th) + 0.5 * x * (1.0 - th * th) * GELU_C * (1.0 + 3 * 0.044715 * x * x)


def _mix_join_fwd(outs, lses, hs, proj, gr_col, *, name):
    t, w = hs.shape
    tb = _tile(t, 512, 16)
    n = len(outs)

    def body(*refs):
        o_refs, l_refs = refs[:n], refs[n:2 * n]
        hs_ref, gr_ref, cat_ref, attn_ref, lse_ref = refs[2 * n:]
        ls = [r[...] for r in l_refs]
        m = ls[0]
        for l in ls[1:]:
            m = jnp.maximum(m, l)
        ws = [jnp.exp(l - m) for l in ls]
        den = sum(ws)
        attn = sum(wt * r[...] for wt, r in zip(ws, o_refs)) / den
        attn_ref[...] = attn
        lse_ref[...] = m + jnp.log(den)
        cat_ref[:, :w] = attn.astype(BF16)
        cat_ref[:, w:] = (hs_ref[...] * _gelu_terms(gr_ref[...])[0]).astype(BF16)

    blk = pl.BlockSpec((tb, w), lambda i: (i, 0))
    return pl.pallas_call(
        body, name=name, grid=(t // tb,),
        in_specs=[blk] * (2 * n + 1) + [pl.BlockSpec((tb, w), lambda i: (i, gr_col))],
        out_specs=[pl.BlockSpec((tb, 2 * w), lambda i: (i, 0)), blk, blk],
        out_shape=[jax.ShapeDtypeStruct((t, 2 * w), BF16), jax.ShapeDtypeStruct((t, w), F32),
                   jax.ShapeDtypeStruct((t, w), F32)],
        compiler_params=_params(("parallel",)),
    )(*outs, *lses, hs, proj)


def _mix_join_bwd(dcat, attn, lse, hs, proj, gr_col, *, name):
    t, w = hs.shape
    hd = w // ATTN_HEADS
    tb = _tile(t, 512, 16)

    def body(dcat_ref, attn_ref, lse_ref, hs_ref, gr_ref, stats_ref, dhs_ref, dgr_ref):
        stats_ref[...] = jnp.zeros_like(stats_ref)
        for h in range(ATTN_HEADS):
            sl = slice(h * hd, (h + 1) * hd)
            stats_ref[:, h:h + 1] = lse_ref[:, h * hd:h * hd + 1]
            stats_ref[:, ATTN_HEADS + h:ATTN_HEADS + h + 1] = jnp.sum(dcat_ref[:, sl] * attn_ref[:, sl], axis=-1, keepdims=True)
        dy = dcat_ref[:, w:]
        g, dg = _gelu_terms(gr_ref[...])
        dhs_ref[...] = dy * g
        dgr_ref[...] = (dy * hs_ref[...] * dg).astype(BF16)

    blk = pl.BlockSpec((tb, w), lambda i: (i, 0))
    return pl.pallas_call(
        body, name=name, grid=(t // tb,),
        in_specs=[pl.BlockSpec((tb, 2 * w), lambda i: (i, 0)), blk, blk, blk, pl.BlockSpec((tb, w), lambda i: (i, gr_col))],
        out_specs=[pl.BlockSpec((tb, LANES), lambda i: (i, 0)), blk, blk],
        out_shape=[jax.ShapeDtypeStruct((t, LANES), F32), jax.ShapeDtypeStruct((t, w), F32),
                   jax.ShapeDtypeStruct((t, w), BF16)],
        compiler_params=_params(("parallel",)),
    )(dcat, attn, lse, hs, proj)


def _silu_terms(x):
    s = jax.nn.sigmoid(x)
    return x * s, s * (1.0 + x * (1.0 - s))


def _dn_prep_fwd(proj, conv_w, *, name):
    t = proj.shape[0]
    w3 = conv_w.shape[1]
    w = w3 // 3
    hd = w // DN_HEADS
    tb = _tile(t, 256, 8)

    def body(x_ref, x_before_ref, cw_ref, q_ref, k_ref, v_ref, c_ref):
        c = _conv_rows(_with_rows_before(x_ref, x_before_ref), cw_ref)
        c_ref[...] = c
        s = _silu_terms(c)[0]
        v_ref[...] = s[:, 2 * w:]
        for part, ref, scale in ((0, q_ref, hd ** -0.5), (1, k_ref, 1.0)):
            for h in range(DN_HEADS):
                z = s[:, part * w + h * hd:part * w + (h + 1) * hd]
                ref[:, h * hd:(h + 1) * hd] = z * (lax.rsqrt(jnp.sum(z * z, axis=-1, keepdims=True) + 1e-6) * scale)

    main, before, _ = _row_specs(t, tb, w3, 0)
    out = pl.BlockSpec((tb, w), lambda i: (i, 0))
    return pl.pallas_call(
        body, name=name, grid=(t // tb,), in_specs=[main, before, pl.BlockSpec((CONV_K, w3), lambda i: (0, 0))],
        out_specs=[out, out, out, pl.BlockSpec((tb, w3), lambda i: (i, 0))],
        out_shape=[jax.ShapeDtypeStruct((t, w), F32)] * 3 + [jax.ShapeDtypeStruct((t, w3), F32)],
        compiler_params=_params(("parallel",)),
    )(proj, proj, conv_w)


def _dn_prep_bwd(c, dq, dk, dv, *, name):
    t, w3 = c.shape
    w = w3 // 3
    hd = w // DN_HEADS
    tb = _tile(t, 256, 8)

    def body(c_ref, dq_ref, dk_ref, dv_ref, dc_ref):
        cv = c_ref[...]
        s, ds = _silu_terms(cv)
        dc_ref[:, 2 * w:] = dv_ref[...] * ds[:, 2 * w:]
        for part, ref, scale in ((0, dq_ref, hd ** -0.5), (1, dk_ref, 1.0)):
            for h in range(DN_HEADS):
                cols = slice(part * w + h * hd, part * w + (h + 1) * hd)
                z = s[:, cols]
                rn = lax.rsqrt(jnp.sum(z * z, axis=-1, keepdims=True) + 1e-6)
                y = z * rn
                dy = ref[:, h * hd:(h + 1) * hd] * scale
                dc_ref[:, cols] = rn * (dy - y * jnp.sum(dy * y, axis=-1, keepdims=True)) * ds[:, cols]

    blk = pl.BlockSpec((tb, w), lambda i: (i, 0))
    wide = pl.BlockSpec((tb, w3), lambda i: (i, 0))
    return pl.pallas_call(
        body, name=name, grid=(t // tb,), in_specs=[wide, blk, blk, blk], out_specs=wide,
        out_shape=jax.ShapeDtypeStruct((t, w3), F32), compiler_params=_params(("parallel",)),
    )(c, dq, dk, dv)


def _dn_gate_fwd(o, proj, z_col, o_norm, *, name):
    t, w = o.shape
    hd = w // DN_HEADS
    tb = _tile(t, 512, 16)

    def body(o_ref, z_ref, g_ref, y_ref):
        for h in range(DN_HEADS):
            sl = slice(h * hd, (h + 1) * hd)
            ov = o_ref[:, sl]
            rn = lax.rsqrt(jnp.mean(ov * ov, axis=-1, keepdims=True) + NORM_EPS)
            y_ref[:, sl] = (ov * rn * g_ref[...] * _silu_terms(z_ref[:, sl])[0]).astype(BF16)

    blk = pl.BlockSpec((tb, w), lambda i: (i, 0))
    return pl.pallas_call(
        body, name=name, grid=(t // tb,),
        in_specs=[blk, pl.BlockSpec((tb, w), lambda i: (i, z_col)), pl.BlockSpec((1, hd), lambda i: (0, 0))],
        out_specs=blk, out_shape=jax.ShapeDtypeStruct((t, w), BF16), compiler_params=_params(("parallel",)),
    )(o, proj, o_norm.reshape(1, hd))


def _dn_gate_bwd(o, proj, z_col, o_norm, dy, *, name):
    t, w = o.shape
    hd = w // DN_HEADS
    tb = _tile(t, 512, 16)

    def body(o_ref, z_ref, g_ref, dy_ref, do_ref, dz_ref, dg_ref):
        gv = g_ref[...]
        dg = jnp.zeros((1, hd), F32)
        for h in range(DN_HEADS):
            sl = slice(h * hd, (h + 1) * hd)
            ov, dyv = o_ref[:, sl], dy_ref[:, sl]
            sz, dsz = _silu_terms(z_ref[:, sl])
            rn = lax.rsqrt(jnp.mean(ov * ov, axis=-1, keepdims=True) + NORM_EPS)
            nv = ov * rn
            dz_ref[:, sl] = (dyv * nv * gv * dsz).astype(BF16)
            dn = dyv * gv * sz
            do_ref[:, sl] = rn * (dn - nv * jnp.mean(dn * nv, axis=-1, keepdims=True))
            dg = dg + jnp.sum(dyv * nv * sz, axis=0, keepdims=True)

        @pl.when(pl.program_id(0) == 0)
        def _():
            dg_ref[...] = dg

        @pl.when(pl.program_id(0) > 0)
        def _():
            dg_ref[...] += dg

    blk = pl.BlockSpec((tb, w), lambda i: (i, 0))
    vec = pl.BlockSpec((1, hd), lambda i: (0, 0))
    do, dz, dg = pl.pallas_call(
        body, name=name, grid=(t // tb,), in_specs=[blk, pl.BlockSpec((tb, w), lambda i: (i, z_col)), vec, blk],
        out_specs=[blk, blk, vec],
        out_shape=[jax.ShapeDtypeStruct((t, w), F32), jax.ShapeDtypeStruct((t, w), BF16),
                   jax.ShapeDtypeStruct((1, hd), F32)],
        compiler_params=_params(("arbitrary",)),
    )(o, proj, o_norm.reshape(1, hd), dy)
    return do, dz, dg.reshape(hd)


def _band_masks(n):
    qi = lax.broadcasted_iota(jnp.int32, (ATTN_BLOCK, ATTN_BLOCK), 0)
    kj = lax.broadcasted_iota(jnp.int32, (ATTN_BLOCK, ATTN_BLOCK), 1)
    return kj <= qi, jnp.logical_and(kj >= qi, n > 0)


def _dattn_fwd(qkv, dil, *, name):
    seq, w = qkv.shape[0], qkv.shape[1] // (3 * dil)
    t = seq * dil
    hd = w // ATTN_HEADS
    assert seq % ATTN_BLOCK == 0
    nb = seq // ATTN_BLOCK
    scale = hd ** -0.5

    def body(q_ref, kp_ref, kc_ref, vp_ref, vc_ref, o_ref, lse_ref):
        mc, mp = _band_masks(pl.program_id(1))
        heads = range(ATTN_HEADS)
        sls = [slice(h * hd, (h + 1) * hd) for h in heads]
        qs = [(q_ref[:, sl] * scale).astype(BF16) for sl in sls]
        scs = [jnp.where(mc, _dot(qs[h], kc_ref[:, sls[h]].astype(BF16), 1, 1), NEG) for h in heads]
        sps = [jnp.where(mp, _dot(qs[h], kp_ref[:, sls[h]].astype(BF16), 1, 1), NEG) for h in heads]
        ms = [jnp.maximum(jnp.max(sc, axis=-1, keepdims=True), jnp.max(sp, axis=-1, keepdims=True))
              for sc, sp in zip(scs, sps)]
        pcs = [jnp.exp(sc - m) for sc, m in zip(scs, ms)]
        pps = [jnp.exp(sp - m) for sp, m in zip(sps, ms)]
        dens = [jnp.sum(pc, axis=-1, keepdims=True) + jnp.sum(pp, axis=-1, keepdims=True) for pc, pp in zip(pcs, pps)]
        outs = [_dot(pcs[h].astype(BF16), vc_ref[:, sls[h]].astype(BF16), 1, 0)
                + _dot(pps[h].astype(BF16), vp_ref[:, sls[h]].astype(BF16), 1, 0) for h in heads]
        for h in heads:
            o_ref[:, sls[h]] = outs[h] / dens[h]
            lse_ref[:, sls[h]] = jnp.broadcast_to(ms[h] + jnp.log(dens[h]), (ATTN_BLOCK, hd))

    cur = pl.BlockSpec((ATTN_BLOCK, w), lambda r, n: (n, r))
    part = lambda j, row: pl.BlockSpec((ATTN_BLOCK, w), lambda r, n: (row(n), 3 * r + j))
    here, before = (lambda n: n), (lambda n: jnp.maximum(n - 1, 0))
    o, lse = pl.pallas_call(
        body, name=name, grid=(dil, nb),
        in_specs=[part(0, here), part(1, before), part(1, here), part(2, before), part(2, here)], out_specs=[cur, cur],
        out_shape=[jax.ShapeDtypeStruct((seq, dil * w), F32)] * 2, compiler_params=_params(("parallel", "parallel")),
    )(qkv, qkv, qkv, qkv, qkv)
    return o.reshape(t, w), lse.reshape(t, w)


def _dattn_bwd(qkv, do, stats, dil, *, name):
    t, w = do.shape
    hd = w // ATTN_HEADS
    seq = t // dil
    nb = seq // ATTN_BLOCK
    scale = hd ** -0.5

    def body(qc_ref, qn_ref, doc_ref, don_ref, sc_ref, sn_ref, kp_ref, kc_ref, vp_ref, vc_ref, dqkv_ref):
        n = pl.program_id(1)
        mc, mp = _band_masks(n)
        _, mx = _band_masks(jnp.where(n + 1 < nb, 1, 0))
        heads = range(ATTN_HEADS)
        sls = [slice(h * hd, (h + 1) * hd) for h in heads]
        lse_of, delta_of = (lambda ref, h: ref[:, h:h + 1]), (lambda ref, h: ref[:, ATTN_HEADS + h:ATTN_HEADS + h + 1])
        bf = lambda ref, scl=None: [(ref[:, sl] if scl is None else ref[:, sl] * scl).astype(BF16) for sl in sls]
        qc, qn, kc, kp = bf(qc_ref, scale), bf(qn_ref, scale), bf(kc_ref), bf(kp_ref)
        vc, vp, doc, don = bf(vc_ref), bf(vp_ref), bf(doc_ref), bf(don_ref)
        p_c = [jnp.exp(jnp.where(mc, _dot(qc[h], kc[h], 1, 1), NEG) - lse_of(sc_ref, h)) for h in heads]
        p_p = [jnp.exp(jnp.where(mp, _dot(qc[h], kp[h], 1, 1), NEG) - lse_of(sc_ref, h)) for h in heads]
        p_x = [jnp.exp(jnp.where(mx, _dot(qn[h], kc[h], 1, 1), NEG) - lse_of(sn_ref, h)) for h in heads]
        ds_c = [(p_c[h] * (_dot(doc[h], vc[h], 1, 1) - delta_of(sc_ref, h))).astype(BF16) for h in heads]
        ds_p = [(p_p[h] * (_dot(doc[h], vp[h], 1, 1) - delta_of(sc_ref, h))).astype(BF16) for h in heads]
        ds_x = [(p_x[h] * (_dot(don[h], vc[h], 1, 1) - delta_of(sn_ref, h))).astype(BF16) for h in heads]
        for h in heads:
            at = lambda part: slice(part * w + h * hd, part * w + (h + 1) * hd)
            dqkv_ref[:, at(0)] = ((_dot(ds_c[h], kc[h], 1, 0) + _dot(ds_p[h], kp[h], 1, 0)) * scale).astype(BF16)
            dqkv_ref[:, at(1)] = (_dot(ds_c[h], qc[h], 0, 0) + _dot(ds_x[h], qn[h], 0, 0)).astype(BF16)
            dqkv_ref[:, at(2)] = (_dot(p_c[h].astype(BF16), doc[h], 0, 0) + _dot(p_x[h].astype(BF16), don[h], 0, 0)).astype(BF16)

    cur = pl.BlockSpec((ATTN_BLOCK, w), lambda r, n: (n, r))
    nxt = pl.BlockSpec((ATTN_BLOCK, w), lambda r, n: (jnp.minimum(n + 1, nb - 1), r))
    part = lambda j, row: pl.BlockSpec((ATTN_BLOCK, w), lambda r, n: (row(n), 3 * r + j))
    here, before, after = (lambda n: n), (lambda n: jnp.maximum(n - 1, 0)), (lambda n: jnp.minimum(n + 1, nb - 1))
    stat = lambda row: pl.BlockSpec((ATTN_BLOCK, LANES), lambda r, n: (row(n), r))
    do, stats = do.reshape(seq, dil * w), stats.reshape(seq, dil * LANES)
    dqkv = pl.pallas_call(
        body, name=name, grid=(dil, nb),
        in_specs=[part(0, here), part(0, after), cur, nxt, stat(here), stat(after), part(1, before), part(1, here),
                  part(2, before), part(2, here)], out_specs=pl.BlockSpec((ATTN_BLOCK, 3 * w), lambda r, n: (n, r)),
        out_shape=jax.ShapeDtypeStruct((seq, dil * 3 * w), BF16), compiler_params=_params(("parallel", "parallel")),
    )(qkv, qkv, do, do, stats, stats, qkv, qkv, qkv, qkv)
    return dqkv.reshape(t, 3 * w)


def _xattn_fwd(q, k, v, *, name):
    t, w = q.shape
    nm = k.shape[0]
    hd = w // XA_HEADS
    scale = hd ** -0.5
    tq = _tile(t, 512, 8)

    def body(q_ref, k_ref, v_ref, o_ref):
        heads = range(XA_HEADS)
        sls = [slice(h * hd, (h + 1) * hd) for h in heads]
        ss = [_dot((q_ref[:, sl] * scale).astype(BF16), k_ref[:, sl].astype(BF16), 1, 1) for sl in sls]
        ps = [jnp.exp(s - jnp.max(s, axis=-1, keepdims=True)) for s in ss]
        ps = [p / jnp.sum(p, axis=-1, keepdims=True) for p in ps]
        for h in heads:
            o_ref[:, sls[h]] = _dot(ps[h].astype(BF16), v_ref[:, sls[h]].astype(BF16), 1, 0).astype(BF16)

    qs = pl.BlockSpec((tq, w), lambda i: (i, 0))
    ks = pl.BlockSpec((nm, w), lambda i: (0, 0))
    return pl.pallas_call(
        body, name=name, grid=(t // tq,), in_specs=[qs, ks, ks], out_specs=qs,
        out_shape=jax.ShapeDtypeStruct((t, w), BF16), compiler_params=_params(("parallel",)),
    )(q, k, v)


def _xattn_bwd(q, k, v, do, *, name):
    t, w = q.shape
    nm = k.shape[0]
    hd = w // XA_HEADS
    scale = hd ** -0.5
    tq = _tile(t, 512, 8)

    def body(q_ref, k_ref, v_ref, do_ref, dq_ref, dk_ref, dv_ref):
        first = pl.program_id(0) == 0
        heads = range(XA_HEADS)
        sls = [slice(h * hd, (h + 1) * hd) for h in heads]
        qs_ = [(q_ref[:, sl] * scale).astype(BF16) for sl in sls]
        ks_, vs_, dos = ([ref[:, sl].astype(BF16) for sl in sls] for ref in (k_ref, v_ref, do_ref))
        ss = [_dot(qs_[h], ks_[h], 1, 1) for h in heads]
        ps = [jnp.exp(s - jnp.max(s, axis=-1, keepdims=True)) for s in ss]
        ps = [p / jnp.sum(p, axis=-1, keepdims=True) for p in ps]
        dps = [_dot(dos[h], vs_[h], 1, 1) for h in heads]
        dss = [(p * (dp - jnp.sum(p * dp, axis=-1, keepdims=True))).astype(BF16) for p, dp in zip(ps, dps)]
        for h in heads:
            sl = sls[h]
            dq_ref[:, sl] = (_dot(dss[h], ks_[h], 1, 0) * scale).astype(BF16)
            dk = _dot(dss[h], qs_[h], 0, 0)
            dv = _dot(ps[h].astype(BF16), dos[h], 0, 0)

            @pl.when(first)
            def _():
                dk_ref[:, sl] = dk
                dv_ref[:, sl] = dv

            @pl.when(jnp.logical_not(first))
            def _():
                dk_ref[:, sl] += dk
                dv_ref[:, sl] += dv

    qs = pl.BlockSpec((tq, w), lambda i: (i, 0))
    ks = pl.BlockSpec((nm, w), lambda i: (0, 0))
    return pl.pallas_call(
        body, name=name, grid=(t // tq,), in_specs=[qs, ks, ks, qs], out_specs=[qs, ks, ks],
        out_shape=[jax.ShapeDtypeStruct((t, w), BF16), jax.ShapeDtypeStruct((nm, w), F32),
                   jax.ShapeDtypeStruct((nm, w), F32)],
        compiler_params=_params(("arbitrary",)),
    )(q, k, v, do)


def _dn_head_terms(q_ref, k_ref, gc_ref, gcr_ref, b_ref, h, hd, j=0):
    c = DN_CHUNK
    sl, rs = slice(h * hd, (h + 1) * hd), slice(j * c, (j + 1) * c)
    qh, kh = q_ref[rs, sl], k_ref[rs, sl]
    gcc, gcr_h, bh = gc_ref[rs, h:h + 1], gcr_ref[j, h:h + 1, :], b_ref[rs, h:h + 1]
    row = lax.broadcasted_iota(jnp.int32, (c, c), 0)
    col = lax.broadcasted_iota(jnp.int32, (c, c), 1)
    decay = jnp.exp(jnp.where(row >= col, gcc - gcr_h, NEG))
    kb = kh * bh
    kkt = _dot(kb.astype(BF16), kh.astype(BF16), 1, 1)
    qkt = _dot(qh.astype(BF16), kh.astype(BF16), 1, 1)
    gl = gcc[c - 1:c, :]
    return dict(sl=sl, rs=rs, j=j, h=h, q=qh, k=kh, gcc=gcc, b=bh, row=row, col=col, decay=decay, kb=kb, kkt=kkt,
                qkt=qkt, e=jnp.exp(gcc), el=jnp.exp(gl), r=jnp.exp(gl - gcc))


def _dn_local_fwd(q, k, v, gc, gcr, beta, *, name):
    t, w = q.shape
    hd = w // DN_HEADS
    c = DN_CHUNK
    nch = t // c
    grp = _tile(nch, DN_LOCAL_CHUNKS, 1)

    def body(q_ref, k_ref, v_ref, gc_ref, gcr_ref, b_ref,
             u_ref, wq_ref, wqt_ref, kr_ref, krt_ref, qk_ref, qkt_ref, invt_ref):
        tm = [_dn_head_terms(q_ref, k_ref, gc_ref, gcr_ref, b_ref, h, hd, j) for j in range(grp) for h in range(DN_HEADS)]
        pw = [jnp.where(m['row'] > m['col'], m['kkt'] * m['decay'], 0.0) for m in tm]
        inv = [(m['row'] == m['col']).astype(F32) - p for m, p in zip(tm, pw)]
        for _ in range(int(math.log2(c)) - 1):
            pw = [_dot(p.astype(BF16), p.astype(BF16), 1, 0) for p in pw]
            inv = [i + _dot(i.astype(BF16), p.astype(BF16), 1, 0) for i, p in zip(inv, pw)]
        for m, iv in zip(tm, inv):
            j, h = m['j'], m['h']
            rhs = jnp.concatenate([v_ref[m['rs'], m['sl']] * m['b'], m['kb'] * m['e']], axis=1).astype(BF16)
            sol = _dot(iv.astype(BF16), rhs, 1, 0)
            u_ref[m['rs'], m['sl']] = sol[:, :hd]
            wq = jnp.concatenate([sol[:, hd:], m['q'] * m['e']], axis=0)
            kr = m['k'] * m['r']
            qk = m['qkt'] * m['decay']
            wq_ref[j, h], wqt_ref[j, h] = wq.astype(BF16), wq.T.astype(BF16)
            kr_ref[j, h], krt_ref[j, h] = kr.astype(BF16), kr.T.astype(BF16)
            qk_ref[j, h], qkt_ref[j, h] = qk.astype(BF16), qk.T.astype(BF16)
            invt_ref[j, h] = iv.T.astype(BF16)

    rows = pl.BlockSpec((grp * c, w), lambda n: (n, 0))
    cols = pl.BlockSpec((grp * c, DN_HEADS), lambda n: (n, 0))
    rowg = pl.BlockSpec((grp, DN_HEADS, c), lambda n: (n, 0, 0))
    per = lambda a, b: (pl.BlockSpec((grp, DN_HEADS, a, b), lambda n: (n, 0, 0, 0)),
                        jax.ShapeDtypeStruct((nch, DN_HEADS, a, b), BF16))
    outs = [(rows, jax.ShapeDtypeStruct((t, w), F32)), per(2 * c, hd), per(hd, 2 * c), per(c, hd), per(hd, c),
            per(c, c), per(c, c), per(c, c)]
    return pl.pallas_call(
        body, name=name, grid=(nch // grp,), in_specs=[rows, rows, rows, cols, rowg, cols],
        out_specs=[o[0] for o in outs], out_shape=[o[1] for o in outs], compiler_params=_params(("parallel",)),
    )(q, k, v, gc, gcr, beta)


def _dn_state_fwd(u, wq, krt, qk, gc, *, name):
    t, w = u.shape
    hd = w // DN_HEADS
    c = DN_CHUNK
    nch = t // c
    grp = _tile(nch, DN_STEP_CHUNKS, 1)

    def body(u_ref, wq_ref, krt_ref, qk_ref, gc_ref, o_ref, vn_ref, s_ref, state):
        @pl.when(pl.program_id(0) == 0)
        def _():
            state[...] = jnp.zeros_like(state)

        heads = range(DN_HEADS)
        sls = [slice(h * hd, (h + 1) * hd) for h in heads]
        shs = [state[h] for h in heads]
        for j in range(grp):
            rs = slice(j * c, (j + 1) * c)
            wss = [_dot(wq_ref[j, h], shs[h].astype(BF16), 1, 0) for h in heads]
            vns = [(u_ref[rs, sls[h]] - wss[h][:c]).astype(BF16) for h in heads]
            outs = [wss[h][c:] + _dot(qk_ref[j, h], vns[h], 1, 0) for h in heads]
            for h in heads:
                s_ref[j, h] = shs[h]
                vn_ref[rs, sls[h]] = vns[h]
                o_ref[rs, sls[h]] = outs[h]
            shs = [shs[h] * jnp.exp(gc_ref[(j + 1) * c - 1:(j + 1) * c, h:h + 1]) + _dot(krt_ref[j, h], vns[h], 1, 0)
                   for h in heads]
        for h in heads:
            state[h] = shs[h]

    rows = pl.BlockSpec((grp * c, w), lambda n: (n, 0))
    per = lambda a, b: pl.BlockSpec((grp, DN_HEADS, a, b), lambda n: (n, 0, 0, 0))
    return pl.pallas_call(
        body, name=name, grid=(nch // grp,),
        in_specs=[rows, per(2 * c, hd), per(hd, c), per(c, c), pl.BlockSpec((grp * c, DN_HEADS), lambda n: (n, 0))],
        out_specs=[rows, rows, per(hd, hd)],
        out_shape=[jax.ShapeDtypeStruct((t, w), F32), jax.ShapeDtypeStruct((t, w), BF16),
                   jax.ShapeDtypeStruct((nch, DN_HEADS, hd, hd), F32)],
        scratch_shapes=[pltpu.VMEM((DN_HEADS, hd, hd), F32)], compiler_params=_params(("arbitrary",)),
    )(u, wq, krt, qk, gc)


def _dn_state_bwd(do, qkt, kr, vn, states, wqt, gc, *, name):
    t, w = do.shape
    hd = w // DN_HEADS
    c = DN_CHUNK
    nch = t // c
    grp = _tile(nch, DN_STEP_CHUNKS, 1)

    def body(do_ref, qkt_ref, kr_ref, vn_ref, s_ref, wqt_ref, gc_ref, dvn_ref, dkr_ref, del_ref, dstate):
        @pl.when(pl.program_id(0) == 0)
        def _():
            dstate[...] = jnp.zeros_like(dstate)

        heads = range(DN_HEADS)
        sls = [slice(h * hd, (h + 1) * hd) for h in heads]
        dsns = [dstate[h] for h in heads]
        for j in reversed(range(grp)):
            rs = slice(j * c, (j + 1) * c)
            dsbs = [d.astype(BF16) for d in dsns]
            dobs = [do_ref[rs, sl].astype(BF16) for sl in sls]
            dvns = [(_dot(qkt_ref[j, h], dobs[h], 1, 0) + _dot(kr_ref[j, h], dsbs[h], 1, 0)).astype(BF16) for h in heads]
            dkrs = [_dot(vn_ref[rs, sls[h]], dsbs[h], 1, 1) for h in heads]
            for h in heads:
                dvn_ref[rs, sls[h]] = dvns[h]
                dkr_ref[rs, sls[h]] = dkrs[h]
                d_el = jnp.sum(jnp.sum(dsns[h] * s_ref[j, h], axis=1, keepdims=True), axis=0, keepdims=True)
                del_ref[j, h:h + 1, :] = jnp.broadcast_to(d_el, (1, LANES))
            dsns = [dsns[h] * jnp.exp(gc_ref[(j + 1) * c - 1:(j + 1) * c, h:h + 1])
                    + _dot(wqt_ref[j, h], jnp.concatenate([-dvns[h], dobs[h]], axis=0), 1, 0) for h in heads]
        for h in heads:
            dstate[h] = dsns[h]

    rev = lambda n: nch // grp - 1 - n
    rows = pl.BlockSpec((grp * c, w), lambda n: (rev(n), 0))
    per = lambda a, b: pl.BlockSpec((grp, DN_HEADS, a, b), lambda n: (rev(n), 0, 0, 0))
    return pl.pallas_call(
        body, name=name, grid=(nch // grp,),
        in_specs=[rows, per(c, c), per(c, hd), rows, per(hd, hd), per(hd, 2 * c),
                  pl.BlockSpec((grp * c, DN_HEADS), lambda n: (rev(n), 0))],
        out_specs=[rows, rows, pl.BlockSpec((grp, DN_HEADS, LANES), lambda n: (rev(n), 0, 0))],
        out_shape=[jax.ShapeDtypeStruct((t, w), BF16), jax.ShapeDtypeStruct((t, w), F32),
                   jax.ShapeDtypeStruct((nch, DN_HEADS, LANES), F32)],
        scratch_shapes=[pltpu.VMEM((DN_HEADS, hd, hd), F32)], compiler_params=_params(("arbitrary",)),
    )(do, qkt, kr, vn, states, wqt, gc)


def _dn_local_bwd(q, k, v, gc, gcr, beta, invt, u, wq, vn, states, do, dvn, dkr, d_el, *, name):
    t, w = q.shape
    hd = w // DN_HEADS
    c = DN_CHUNK
    nch = t // c
    grp = _tile(nch, DN_LOCAL_CHUNKS, 1)

    def body(q_ref, k_ref, v_ref, gc_ref, gcr_ref, b_ref, invt_ref, u_ref, wq_ref, vn_ref, s_ref, do_ref, dvn_ref,
             dkr_ref, del_ref, dq_ref, dk_ref, dv_ref, dgc_ref, dgr_ref, db_ref):
        tms = [_dn_head_terms(q_ref, k_ref, gc_ref, gcr_ref, b_ref, h, hd, j) for j in range(grp) for h in range(DN_HEADS)]
        dobs = [do_ref[m['rs'], m['sl']].astype(BF16) for m in tms]
        tss = [_dot(jnp.concatenate([dob, -dvn_ref[m['rs'], m['sl']]], axis=0), s_ref[m['j'], m['h']].astype(BF16), 1, 1)
               for m, dob in zip(tms, dobs)]
        d_qks = [_dot(dob, vn_ref[m['rs'], m['sl']], 1, 1) for m, dob in zip(tms, dobs)]
        d_rhss = [_dot(invt_ref[m['j'], m['h']],
                       jnp.concatenate([dvn_ref[m['rs'], m['sl']], ts[c:].astype(BF16)], axis=1), 1, 0)
                  for m, ts in zip(tms, tss)]
        d_as = [-_dot(d_rhs.astype(BF16), jnp.concatenate(
            [u_ref[m['rs'], m['sl']].astype(BF16), wq_ref[m['j'], m['h'], :c, :]], axis=1), 1, 1)
                for m, d_rhs in zip(tms, d_rhss)]
        for m, ts, d_qk, d_rhs, d_a in zip(tms, tss, d_qks, d_rhss, d_as):
            d_qe, j, h, rs = ts[:c], m['j'], m['h'], m['rs']
            sl, qh, kh, bh, e, r, decay = m['sl'], m['q'], m['k'], m['b'], m['e'], m['r'], m['decay']
            vh = v_ref[rs, sl]
            d_ru, d_rw = d_rhs[:, :hd], d_rhs[:, hd:]
            dv_ref[rs, sl] = d_ru * bh
            d_e = jnp.sum(d_rw * m['kb'], axis=1, keepdims=True) + jnp.sum(d_qe * qh, axis=1, keepdims=True)
            d_n = jnp.where(m['row'] > m['col'], d_a, 0.0)
            d_m, d_p = (d_n * decay).astype(BF16), (d_qk * decay).astype(BF16)
            tk = _dot(jnp.concatenate([d_m, d_p], axis=0), kh.astype(BF16), 1, 0)
            d_kb = d_rw * e + tk[:c]
            dq_ref[rs, sl] = tk[c:] + d_qe * e
            dk = _dot(d_m, m['kb'].astype(BF16), 0, 0) + _dot(d_p, qh.astype(BF16), 0, 0)
            dd = (d_n * m['kkt'] + d_qk * m['qkt']) * decay
            d_kr = dkr_ref[rs, sl]
            d_r = jnp.sum(d_kr * kh, axis=1, keepdims=True)
            d_gl = del_ref[j, h:h + 1, 0:1] * m['el'] + jnp.sum(d_r * r, axis=0, keepdims=True)
            last = lax.broadcasted_iota(jnp.int32, (c, 1), 0) == c - 1
            dgc_ref[rs, h:h + 1] = jnp.sum(dd, axis=1, keepdims=True) + d_e * e - d_r * r + jnp.where(last, d_gl, 0.0)
            dgr_ref[j, h:h + 1, :] = -jnp.sum(dd, axis=0, keepdims=True)
            dk_ref[rs, sl] = dk + d_kr * r + d_kb * bh
            db_ref[rs, h:h + 1] = jnp.sum(d_ru * vh, axis=1, keepdims=True) + jnp.sum(d_kb * kh, axis=1, keepdims=True)

    rows = pl.BlockSpec((grp * c, w), lambda n: (n, 0))
    cols = pl.BlockSpec((grp * c, DN_HEADS), lambda n: (n, 0))
    rowg = pl.BlockSpec((grp, DN_HEADS, c), lambda n: (n, 0, 0))
    per = lambda a, b: pl.BlockSpec((grp, DN_HEADS, a, b), lambda n: (n, 0, 0, 0))
    return pl.pallas_call(
        body, name=name, grid=(nch // grp,),
        in_specs=[rows, rows, rows, cols, rowg, cols, per(c, c), rows, per(2 * c, hd), rows, per(hd, hd), rows, rows,
                  rows, pl.BlockSpec((grp, DN_HEADS, LANES), lambda n: (n, 0, 0))],
        out_specs=[rows, rows, rows, cols, rowg, cols],
        out_shape=[jax.ShapeDtypeStruct((t, w), F32)] * 3
        + [jax.ShapeDtypeStruct((t, DN_HEADS), F32), jax.ShapeDtypeStruct((nch, DN_HEADS, c), F32),
           jax.ShapeDtypeStruct((t, DN_HEADS), F32)],
        compiler_params=_params(("parallel",)),
    )(q, k, v, gc, gcr, beta, invt, u, wq, vn, states, do, dvn, dkr, d_el)


def _other_chips():
    x, y = lax.axis_index("x"), lax.axis_index("y")
    return [(1 - x, y), (x, 1 - y), (1 - x, 1 - y)]


def _gather(src, *, name):
    def body(src_ref, out_ref, send_sems, recv_sems, local_sem):
        x, y, c = lax.axis_index("x"), lax.axis_index("y"), lax.axis_index("c")
        me, sibling, chips = (x, y, c), (x, y, 1 - c), _other_chips()
        slot = lambda px, py, pc: out_ref.at[4 * px + 2 * py + pc]

        def copy(k, block, to, own=False):
            return pltpu.make_async_remote_copy(
                src_ref=src_ref if own else slot(*block), dst_ref=slot(*block), send_sem=send_sems.at[k],
                recv_sem=recv_sems.at[k], device_id=to, device_id_type=MESH)

        local = pltpu.make_async_copy(src_ref, slot(*me), local_sem)
        local.start()
        first = [copy(0, me, sibling, own=True)] + [copy(1 + j, me, (*chip, c), own=True) for j, chip in enumerate(chips)]
        for cp in first:
            cp.start()
        passed = [copy(4 + j, (*chip, c), sibling) for j, chip in enumerate(chips)]
        for j, chip in enumerate(chips):
            copy(1 + j, (*chip, c), me).wait_recv()
            passed[j].start()
        copy(0, sibling, me).wait_recv()
        for j, chip in enumerate(chips):
            copy(4 + j, (*chip, 1 - c), me).wait_recv()
        for cp in first + passed:
            cp.wait_send()
        local.wait()

    return pl.pallas_call(
        body, name=name, out_shape=jax.ShapeDtypeStruct((N_DEV,) + src.shape, src.dtype),
        in_specs=[pl.BlockSpec(memory_space=pl.ANY)], out_specs=pl.BlockSpec(memory_space=pl.ANY),
        scratch_shapes=[pltpu.SemaphoreType.DMA((N_DEV - 1,)), pltpu.SemaphoreType.DMA((N_DEV - 1,)),
                        pltpu.SemaphoreType.DMA(())],
    )(src)


def _swap_with_sibling(src, *, name):
    def body(src_ref, out_ref, send_sem, recv_sem):
        x, y, c = lax.axis_index("x"), lax.axis_index("y"), lax.axis_index("c")
        cp = pltpu.make_async_remote_copy(src_ref=src_ref.at[1 - c], dst_ref=out_ref, send_sem=send_sem,
                                          recv_sem=recv_sem, device_id=(x, y, 1 - c), device_id_type=MESH)
        cp.start()
        cp.wait()

    return pl.pallas_call(
        body, name=name, out_shape=jax.ShapeDtypeStruct(src.shape[1:], src.dtype),
        in_specs=[pl.BlockSpec(memory_space=pl.ANY)], out_specs=pl.BlockSpec(memory_space=pl.ANY),
        scratch_shapes=[pltpu.SemaphoreType.DMA(()), pltpu.SemaphoreType.DMA(())],
    )(src)


def _add_own_half(src, got, *, name):
    _, n, r, c = src.shape
    tr = _tile(r, 512, 16)

    def body(s0_ref, s1_ref, got_ref, o_ref):
        own = jnp.where(lax.axis_index("c") == 0, s0_ref[0, 0], s1_ref[0, 0])
        o_ref[0] = (own.astype(F32) + got_ref[0].astype(F32)).astype(BF16)

    half = lambda h: pl.BlockSpec((1, 1, tr, c), lambda j, i: (h, j, i, 0))
    blk = pl.BlockSpec((1, tr, c), lambda j, i: (j, i, 0))
    return pl.pallas_call(
        body, name=name, grid=(n, r // tr), in_specs=[half(0), half(1), blk], out_specs=blk,
        out_shape=jax.ShapeDtypeStruct((n, r, c), BF16), compiler_params=_params(("parallel", "parallel")),
    )(src, src, got)


def _swap_between_chips(src, *, name):
    def body(src_ref, out_ref, send_sems, recv_sems, local_sem):
        x, y, c = lax.axis_index("x"), lax.axis_index("y"), lax.axis_index("c")
        mine = 2 * x + y
        local = pltpu.make_async_copy(src_ref.at[mine], out_ref.at[mine], local_sem)
        local.start()
        copies = [pltpu.make_async_remote_copy(
            src_ref=src_ref.at[2 * px + py], dst_ref=out_ref.at[mine], send_sem=send_sems.at[j],
            recv_sem=recv_sems.at[j], device_id=(px, py, c), device_id_type=MESH) for j, (px, py) in enumerate(_other_chips())]
        for cp in copies:
            cp.start()
        for j, (px, py) in enumerate(_other_chips()):
            pltpu.make_async_remote_copy(
                src_ref=src_ref.at[mine], dst_ref=out_ref.at[2 * px + py], send_sem=send_sems.at[j],
                recv_sem=recv_sems.at[j], device_id=(px, py, c), device_id_type=MESH).wait_recv()
        for cp in copies:
            cp.wait_send()
        local.wait()

    return pl.pallas_call(
        body, name=name, out_shape=jax.ShapeDtypeStruct(src.shape, src.dtype),
        in_specs=[pl.BlockSpec(memory_space=pl.ANY)], out_specs=pl.BlockSpec(memory_space=pl.ANY),
        scratch_shapes=[pltpu.SemaphoreType.DMA((3,)), pltpu.SemaphoreType.DMA((3,)), pltpu.SemaphoreType.DMA(())],
    )(src)


def _sum_adamw(parts, w, m, v, *, name):
    n_parts, r, c = parts.shape
    tr = _tile(r, 256, 8)
    c1, c2 = 1.0 - ADAM_B1 ** ADAM_STEP, 1.0 - ADAM_B2 ** ADAM_STEP

    def body(p_ref, w_ref, m_ref, v_ref, g_ref, d_ref, nm_ref, nv_ref):
        g = p_ref[0].astype(F32)
        for s in range(1, n_parts):
            g = g + p_ref[s].astype(F32)
        nm = ADAM_B1 * m_ref[...] + (1.0 - ADAM_B1) * g
        nv = ADAM_B2 * v_ref[...] + (1.0 - ADAM_B2) * (g * g)
        g_ref[...] = g
        nm_ref[...] = nm
        nv_ref[...] = nv
        d_ref[...] = -ADAM_LR * ((nm / c1) / (jnp.sqrt(nv / c2) + ADAM_EPS) + ADAM_WD * w_ref[...])

    blk = pl.BlockSpec((tr, c), lambda i: (i, 0))
    return pl.pallas_call(
        body, name=name, grid=(r // tr,), in_specs=[pl.BlockSpec((n_parts, tr, c), lambda i: (0, i, 0)), blk, blk, blk],
        out_specs=[blk] * 4, out_shape=[jax.ShapeDtypeStruct((r, c), F32)] * 4, compiler_params=_params(("parallel",)),
    )(parts, w, m, v)


def _pack_rows(n):
    return -(-n // (PACK_COLS * PACK_ROWS)) * PACK_ROWS


def _pack(blocks):
    parts, spans, at = [], [], 0
    for blk, n_lead in blocks:
        lead = blk.shape[:n_lead]
        n = math.prod(blk.shape[n_lead:])
        rows = _pack_rows(n)
        flat = blk.reshape(lead + (n,))
        flat = jnp.pad(flat, [(0, 0)] * n_lead + [(0, rows * PACK_COLS - n)])
        parts.append(flat.reshape(lead + (rows, PACK_COLS)))
        spans.append((at, rows))
        at += rows
    return jnp.concatenate(parts, axis=-2), spans


def _unpack(buf, span, shape):
    at, rows = span
    lead = buf.shape[:-2]
    flat = lax.slice_in_dim(buf, at, at + rows, axis=buf.ndim - 2).reshape(lead + (rows * PACK_COLS,))
    return lax.slice_in_dim(flat, 0, math.prod(shape), axis=len(lead)).reshape(lead + tuple(shape))


def _join_shards(g, axis):
    g = jnp.moveaxis(g, 0, axis)
    return g.reshape(g.shape[:axis] + (g.shape[axis] * g.shape[axis + 1],) + g.shape[axis + 2:])


def _split_shards(full, axis):
    s = full.shape
    g = full.reshape(s[:axis] + (N_DEV, s[axis] // N_DEV) + s[axis + 1:])
    return jnp.moveaxis(g, axis, 0)


def _residual_out(a, w, x, scale, next_gain, name, **tiles):
    out = _mm(a, w, name=name, res=x, scale=scale, norm=next_gain, **tiles)
    return out if next_gain is not None else (out, None)


def _ffn_fwd(x, h, w_in, w_out, tag, next_gain):
    gate, up, a = _mm_swiglu(h, w_in, name=tag + "_in", tm=512, tn=1408)
    xo, hn = _residual_out(a, w_out, x, 0.5, next_gain, tag + "_out", tk=2816)
    return xo, hn, (x, h, gate, up, a)


def _ffn_bwd(saved, g, w_in, w_out, dxo, tag):
    x, h, gate, up, a = saved
    f = w_out.shape[0]
    d_w_out = _mm(a, dxo, name=tag + "_dwout", ta=True, scale=0.5, tm=1408, tk=2048)
    dgate, dup = _mm_dswiglu(dxo, w_out, gate, up, name=tag + "_da", scale=0.5, tm=512, tn=1408)
    d_w_in = jnp.concatenate([_mm(dgate, h, name=tag + "_dwin_g", ta=True, tm=1408, tk=2048),
                              _mm(dup, h, name=tag + "_dwin_u", ta=True, tm=1408, tk=2048)], axis=0)
    dh = _mm(dgate, w_in[:f], name=tag + "_dh_g", tk=2816)
    dx, dg = _mm(dup, w_in[f:], name=tag + "_dh_u", tm=512, tk=2816, res=dh, rms_bwd=(x, g, dxo))
    return dx, dg, d_w_in, d_w_out


def _block_diag(w):
    n, j, k = w.shape
    return (w[:, :, None, :] * jnp.eye(n, dtype=w.dtype)[:, None, :, None]).reshape(n * j, n * k)


def _diag_blocks(dense, n):
    j, k = dense.shape[0] // n, dense.shape[1] // n
    return jnp.stack([dense[i * j:(i + 1) * j, i * k:(i + 1) * k] for i in range(n)], axis=0)


def _attn_lru_fwd(x, h, p, tag, next_gain):
    aw = ATTN_HEADS * 64
    proj = _mm(h, p['ab_w_in'], name=tag + "_in", tb=True, tn=1280)
    qkv = proj[:, :3 * aw].astype(BF16)
    views = {dil: qkv.reshape(qkv.shape[0] // dil, dil * 3 * aw) for _, dil in DILATED_PATTERNS}
    outs, lses = [], []
    for window, dil in DILATED_PATTERNS:
        assert window // dil == ATTN_BLOCK
        o, l = _dattn_fwd(views[dil], dil, name=f"{tag}_attn{dil}")
        outs.append(o)
        lses.append(l)
    wa, wx = _block_diag(p['lru_w_a']).astype(BF16), _block_diag(p['lru_w_x']).astype(BF16)
    sp, sp_vjp = jax.vjp(lambda lam: LRU_C * jax.nn.softplus(-lam), p['lru_lambda'])
    a, b, xc = _lru_gates_fwd(proj, 3, p['lru_conv_w'], p['lru_conv_b'], wa, p['lru_b_a'], wx, p['lru_b_x'], sp,
                              name=tag + "_gates")
    hs = _scan(a, b, name=tag + "_scan")
    cat, attn, lse_all = _mix_join_fwd(outs, lses, hs, proj, 4, name=tag + "_join")
    xo, hn = _residual_out(cat, p['ab_w_out'], x, 1.0, next_gain, tag + "_out")
    return xo, hn, (x, h, proj, views, attn, lse_all, a, hs, xc, wa, wx, sp, sp_vjp, cat)


def _attn_lru_bwd(saved, p, dxo, tag):
    x, h, proj, views, attn, lse_all, a, hs, xc, wa, wx, sp, sp_vjp, cat = saved
    aw = attn.shape[1]
    g = {'ab_w_out': _mm(cat, dxo, name=tag + "_dwout", ta=True)}
    dcat = _mm(dxo, p['ab_w_out'], name=tag + "_dcat", tb=True)
    stats, dhs, dgr = _mix_join_bwd(dcat, attn, lse_all, hs, proj, 4, name=tag + "_djoin")
    a_next = jnp.concatenate([a[1:], jnp.zeros_like(a[:1])], axis=0)
    dtot = _scan(a_next, dhs, name=tag + "_dscan", reverse=True)
    h_prev = jnp.concatenate([jnp.zeros_like(hs[:1]), hs[:-1]], axis=0)
    dxc, dwa, dwx, vecs = _lru_gates_bwd(xc, dtot, h_prev, wa, p['lru_b_a'], wx, p['lru_b_x'], sp, name=tag + "_dgates")
    dxr, g['lru_conv_w'] = _conv_bwd(dxc, proj, 3, p['lru_conv_w'], name=tag + "_dconv")
    g['lru_b_a'], g['lru_b_x'], g['lru_conv_b'] = vecs[0], vecs[1], vecs[3]
    g['lru_lambda'], = sp_vjp(vecs[2])
    g['lru_w_a'], g['lru_w_x'] = _diag_blocks(dwa, LRU_BLOCKS), _diag_blocks(dwx, LRU_BLOCKS)
    dattn = dcat[:, :aw]
    dqkv = sum(_dattn_bwd(views[dil], dattn, stats, dil, name=f"{tag}_dattn{dil}").astype(F32)
               for _, dil in DILATED_PATTERNS)
    dproj = jnp.concatenate([dqkv.astype(BF16), dxr, dgr], axis=-1)
    g['ab_w_in'] = _mm(dproj, h, name=tag + "_dwin", ta=True, tm=1280)
    dx, g['mix_norm'] = _mm(dproj, p['ab_w_in'], name=tag + "_dh", tm=512, tk=2560, rms_bwd=(x, p['mix_norm'], dxo))
    return dx, g


def _dn_decay(a, b, a_log, dt_bias):
    t = a.shape[0]
    g = -jnp.exp(a_log) * jax.nn.softplus(a + dt_bias)
    gc = jnp.cumsum(g.reshape(t // DN_CHUNK, DN_CHUNK, DN_HEADS), axis=1)
    return gc.reshape(t, DN_HEADS), jnp.swapaxes(gc, 1, 2), jax.nn.sigmoid(b)


def _dn_in_width(w):
    return -(-(4 * w + 2 * DN_HEADS) // LANES) * LANES


def _deltanet_fwd(x, h, p, tag, next_gain):
    w = p['dn_w_out'].shape[0]
    proj = _mm(h, p['dn_w_in'], name=tag + "_in", tb=True, tn=1408)
    q, k, v, c = _dn_prep_fwd(proj, p['dn_conv_w'], name=tag + "_prep")
    a, b = proj[:, 4 * w:4 * w + DN_HEADS], proj[:, 4 * w + DN_HEADS:4 * w + 2 * DN_HEADS]
    (gc, gcr, beta), decay_vjp = jax.vjp(_dn_decay, a, b, p['dn_a_log'], p['dn_dt_bias'])
    prep = (q, k, v, gc, gcr, beta)
    u, wq, wqt, kr, krt, qk, qkt, invt = _dn_local_fwd(*prep, name=tag + "_local")
    o, vn, states = _dn_state_fwd(u, wq, krt, qk, gc, name=tag + "_state")
    og = _dn_gate_fwd(o, proj, 3, p['dn_o_norm'], name=tag + "_gate")
    xo, hn = _residual_out(og, p['dn_w_out'], x, 1.0, next_gain, tag + "_out")
    return xo, hn, (x, h, proj, c, o, prep, (u, wq, wqt, kr, qkt, invt, vn, states), decay_vjp, og)


def _deltanet_bwd(saved, p, dxo, tag):
    x, h, proj, c, o, prep, (u, wq, wqt, kr, qkt, invt, vn, states), decay_vjp, og = saved
    g = {'dn_w_out': _mm(og, dxo, name=tag + "_dwout", ta=True)}
    dog = _mm(dxo, p['dn_w_out'], name=tag + "_dog", tb=True)
    do, dz, g['dn_o_norm'] = _dn_gate_bwd(o, proj, 3, p['dn_o_norm'], dog, name=tag + "_dgate")
    dvn, dkr, d_el = _dn_state_bwd(do, qkt, kr, vn, states, wqt, prep[3], name=tag + "_dstate")
    dq, dk, dv, dgc, dgr, dbeta = _dn_local_bwd(*prep, invt, u, wq, vn, states, do, dvn, dkr, d_el, name=tag + "_dlocal")
    dc = _dn_prep_bwd(c, dq, dk, dv, name=tag + "_dprep")
    dqkv, g['dn_conv_w'] = _conv_bwd(dc, proj, 0, p['dn_conv_w'], name=tag + "_dconv")
    da, db, g['dn_a_log'], g['dn_dt_bias'] = decay_vjp((dgc, dgr, dbeta))
    t = x.shape[0]
    pad = jnp.zeros((t, p['dn_w_in'].shape[0] - dqkv.shape[1] - dz.shape[1] - 2 * DN_HEADS), BF16)
    groups = [("qkv", dqkv), ("z", dz), ("ab", jnp.concatenate([da.astype(BF16), db.astype(BF16), pad], axis=-1))]
    parts, dh, lo = [], None, 0
    for label, piece in groups:
        hi = lo + piece.shape[1]
        parts.append(_mm(piece, h, name=f"{tag}_dwin_{label}", ta=True, tm=1536, tk=2048))
        last = dict(tm=512, rms_bwd=(x, p['mix_norm'], dxo)) if label == groups[-1][0] else {}
        dh = _mm(piece, p['dn_w_in'][lo:hi], name=f"{tag}_dh_{label}", tk=1536, res=dh, **last)
        lo = hi
    g['dn_w_in'] = jnp.concatenate(parts, axis=0)
    dx, g['mix_norm'] = dh
    return dx, g


def _xattn_block_fwd(x, h, mem, p, tag, next_gain):
    w = x.shape[1]
    mh = _rms_fwd(mem, p['xa_mem_norm'], name=tag + "_mnorm", out_dtype=BF16)
    q = _mm(h, p['xa_wq'], name=tag + "_q")
    kv = _mm(mh, p['xa_wkv'], name=tag + "_kv", tb=True)
    k, v = kv[:, :w], kv[:, w:]
    o = _xattn_fwd(q, k, v, name=tag + "_attn").astype(BF16)
    xo, hn = _residual_out(o, p['xa_wo'], x, 1.0, next_gain, tag + "_out")
    return xo, hn, (x, h, mh, q, k, v, o)


def _xattn_block_bwd(saved, mem, p, dxo, tag):
    x, h, mh, q, k, v, o = saved
    g = {'xa_wo': _mm(o, dxo, name=tag + "_dwo", ta=True)}
    do = _mm(dxo, p['xa_wo'], name=tag + "_do", tb=True)
    dq, dk, dv = _xattn_bwd(q, k, v, do, name=tag + "_dattn")
    dq = dq.astype(BF16)
    dkv = jnp.concatenate([dk, dv], axis=-1).astype(BF16)
    g['xa_wq'] = _mm(h, dq, name=tag + "_dwq", ta=True)
    g['xa_wkv'] = _mm(dkv, mh, name=tag + "_dwkv", ta=True)
    dmh = _mm(dkv, p['xa_wkv'], name=tag + "_dmh")
    _, g['xa_mem_norm'] = _rms_bwd(mem, p['xa_mem_norm'], dmh, None, name=tag + "_dmnorm")
    dx, g['xa_norm'] = _mm(dq, p['xa_wq'], name=tag + "_dh", tb=True, tm=512, rms_bwd=(x, p['xa_norm'], dxo))
    return dx, g


def _layer_params(full, layer):
    p = {n: full[n][layer] for n in ('ffn1_norm', 'ffn1_w_in', 'ffn1_w_out', 'mix_norm', 'xa_norm', 'xa_mem_norm',
                                     'xa_wq', 'xa_wkv', 'xa_wo', 'ffn2_norm', 'ffn2_w_in', 'ffn2_w_out')}
    mixer = ('ab_w_in', 'lru_conv_w', 'lru_conv_b', 'lru_w_a', 'lru_b_a', 'lru_w_x', 'lru_b_x', 'lru_lambda', 'ab_w_out') \
        if layer % 2 == 0 else ('dn_w_in', 'dn_conv_w', 'dn_a_log', 'dn_dt_bias', 'dn_o_norm', 'dn_w_out')
    p.update({n: full[n][layer // 2] for n in mixer})
    return p


def _step(x, mem, target, full, depth):
    saved = []
    params = [_layer_params(full, layer) for layer in range(depth)]
    h = _rms_fwd(x, params[0]['ffn1_norm'], name="l0_ffn1_norm", out_dtype=BF16)
    for layer, p in enumerate(params):
        tag = f"l{layer}"
        after = params[layer + 1]['ffn1_norm'] if layer + 1 < depth else None
        x, h, s1 = _ffn_fwd(x, h, p['ffn1_w_in'], p['ffn1_w_out'], tag + "_ffn1", p['mix_norm'])
        x, h, s2 = (_attn_lru_fwd if layer % 2 == 0 else _deltanet_fwd)(x, h, p, tag + "_mix", p['xa_norm'])
        x, h, s3 = _xattn_block_fwd(x, h, mem, p, tag + "_xa", p['ffn2_norm'])
        x, h, s4 = _ffn_fwd(x, h, p['ffn2_w_in'], p['ffn2_w_out'], tag + "_ffn2", after)
        saved.append((p, s1, s2, s3, s4))
    sq, dx, d_final = _final_loss(x, full['final_norm'], target, name="final_loss")
    per_layer = []
    for layer in reversed(range(depth)):
        p, s1, s2, s3, s4 = saved[layer]
        tag = f"l{layer}"
        g = {}
        dx, g['ffn2_norm'], g['ffn2_w_in'], g['ffn2_w_out'] = _ffn_bwd(s4, p['ffn2_norm'], p['ffn2_w_in'], p['ffn2_w_out'], dx, tag + "_ffn2")
        dx, gx = _xattn_block_bwd(s3, mem, p, dx, tag + "_xa")
        dx, gm = (_attn_lru_bwd if layer % 2 == 0 else _deltanet_bwd)(s2, p, dx, tag + "_mix")
        dx, g['ffn1_norm'], g['ffn1_w_in'], g['ffn1_w_out'] = _ffn_bwd(s1, p['ffn1_norm'], p['ffn1_w_in'], p['ffn1_w_out'], dx, tag + "_ffn1")
        g.update(gx)
        g.update(gm)
        per_layer.insert(0, g)
    grads = {n: [g[n] for g in per_layer if n in g] for n in WEIGHTS[:-1]}
    grads['final_norm'] = d_final
    return sq, dx, grads


def kernel(x, mem, ffn1_norm, ffn1_w_in, ffn1_w_out, mix_norm, xa_norm, xa_mem_norm, xa_wq, xa_wkv, xa_wo, ffn2_norm, ffn2_w_in, ffn2_w_out, ab_w_in, lru_conv_w, lru_conv_b, lru_w_a, lru_b_a, lru_w_x, lru_b_x, lru_lambda, ab_w_out, dn_w_in, dn_conv_w, dn_a_log, dn_dt_bias, dn_o_norm, dn_w_out, final_norm, loss_target, m_ffn1_norm, m_ffn1_w_in, m_ffn1_w_out, m_mix_norm, m_xa_norm, m_xa_mem_norm, m_xa_wq, m_xa_wkv, m_xa_wo, m_ffn2_norm, m_ffn2_w_in, m_ffn2_w_out, m_ab_w_in, m_lru_conv_w, m_lru_conv_b, m_lru_w_a, m_lru_b_a, m_lru_w_x, m_lru_b_x, m_lru_lambda, m_ab_w_out, m_dn_w_in, m_dn_conv_w, m_dn_a_log, m_dn_dt_bias, m_dn_o_norm, m_dn_w_out, m_final_norm, v_ffn1_norm, v_ffn1_w_in, v_ffn1_w_out, v_mix_norm, v_xa_norm, v_xa_mem_norm, v_xa_wq, v_xa_wkv, v_xa_wo, v_ffn2_norm, v_ffn2_w_in, v_ffn2_w_out, v_ab_w_in, v_lru_conv_w, v_lru_conv_b, v_lru_w_a, v_lru_b_a, v_lru_w_x, v_lru_b_x, v_lru_lambda, v_ab_w_out, v_dn_w_in, v_dn_conv_w, v_dn_a_log, v_dn_dt_bias, v_dn_o_norm, v_dn_w_out, v_final_norm):
    args = dict(locals())
    flip = lambda n, a: jnp.swapaxes(a, 1, 2) if n in TRANSPOSED else a
    local = {n: flip(n, args[n]) for n in WEIGHTS}
    depth = ffn1_norm.shape[0]
    matrices = [n for n in WEIGHTS if n in SHARD_AXIS and n not in GATHER_F32]

    def entries(get):
        return [(a, n, l) for n in WEIGHTS for l, a in
                (enumerate(get(n)) if n in matrices else [(None, get(n))])]

    mats = [e for e in entries(lambda n: local[n]) if e[1] in matrices]
    send16, spans16 = _pack([(a.astype(BF16), 0) for a, _, _ in mats])
    send32, spans32 = _pack([(local[n], 0) for n in GATHER_F32])
    got16 = _gather(send16, name="gather_matrices")
    got32 = _gather(send32, name="gather_filters")
    full = {n: ([None] * local[n].shape[0] if n in matrices else local[n]) for n in WEIGHTS}
    for (a, n, l), span in zip(mats, spans16):
        full[n][l] = _unpack(got16, span, a.shape).reshape(N_DEV * a.shape[0], a.shape[1])
    for n, span in zip(GATHER_F32, spans32):
        full[n] = _join_shards(_unpack(got32, span, local[n].shape), SHARD_AXIS[n])
    dn_rows = full['dn_w_in'][0].shape[0]
    full['dn_w_in'] = [jnp.pad(w, ((0, _dn_in_width(w.shape[1]) - dn_rows), (0, 0))) for w in full['dn_w_in']]

    sq, dx, grads = _step(x[0], mem[0], loss_target[0], full, depth)
    grads['dn_w_in'] = [g[:dn_rows] for g in grads['dn_w_in']]
    loss = lax.psum(0.5 * jnp.sum(sq) / x.shape[2], ("x", "y", "c"))

    by_core = lambda z: jnp.swapaxes(z.reshape((N_DEV // 2, 2) + z.shape[1:]), 0, 1)

    def contribution(n):
        if n in matrices:
            return [by_core(g.reshape((N_DEV, g.shape[0] // N_DEV, g.shape[1]))) for g in grads[n]]
        g = grads[n] if n == 'final_norm' else jnp.stack(grads[n], axis=0)
        return by_core(_split_shards(g, SHARD_AXIS[n]) if n in SHARD_AXIS else jnp.broadcast_to(g, (N_DEV,) + g.shape))

    send, spans = _pack([(a.astype(BF16), 2) for a, _, _ in entries(contribution)])
    got = _swap_with_sibling(send, name="grads_to_sibling")
    chip_sum = _add_own_half(send, got, name="grads_chip_sum")
    parts = _swap_between_chips(chip_sum, name="grads_between_chips")
    state = [_pack([(a, 0) for a, _, _ in entries(lambda n: flip(n, args[pre + n]))])[0] for pre in ("", "m_", "v_")]
    outs = _sum_adamw(parts, *state, name="sum_adamw")
    result = []
    for o in outs:
        got_rows = {}
        for (a, n, l), span in zip(entries(lambda n: local[n]), spans):
            got_rows.setdefault(n, []).append(_unpack(o, span, a.shape))
        result += [flip(n, jnp.stack(got_rows[n], axis=0) if n in matrices else got_rows[n][0]) for n in WEIGHTS]
    return (loss, dx[None], *result)
```

```python
import math

import jax
import jax.numpy as jnp
from jax import lax
from jax.experimental import pallas as pl
from jax.experimental.pallas import tpu as pltpu

F32, BF16 = jnp.float32, jnp.bfloat16
MESH = pl.DeviceIdType.MESH
N_DEV = 8
V7X_VMEM_LIMIT = 56 << 20
LANES = 128
PACK_COLS = 1024
PACK_ROWS = 16
NEG = -1e30

NORM_EPS = 1e-6
CONV_K = 4
ATTN_HEADS = 8
DILATED_PATTERNS = ((128, 1), (512, 4), (2048, 16))
ATTN_BLOCK = 128
LRU_BLOCKS = 8
LRU_C = 8.0
DN_HEADS = 8
DN_CHUNK = 64
DN_STEP_CHUNKS = 4
DN_LOCAL_CHUNKS = 4
XA_HEADS = 4
ADAM_LR, ADAM_B1, ADAM_B2, ADAM_EPS, ADAM_WD, ADAM_STEP = 0.001, 0.9, 0.999, 1e-08, 0.01, 10

WEIGHTS = ['ffn1_norm', 'ffn1_w_in', 'ffn1_w_out', 'mix_norm', 'xa_norm', 'xa_mem_norm', 'xa_wq', 'xa_wkv', 'xa_wo',
           'ffn2_norm', 'ffn2_w_in', 'ffn2_w_out', 'ab_w_in', 'lru_conv_w', 'lru_conv_b', 'lru_w_a', 'lru_b_a',
           'lru_w_x', 'lru_b_x', 'lru_lambda', 'ab_w_out', 'dn_w_in', 'dn_conv_w', 'dn_a_log', 'dn_dt_bias',
           'dn_o_norm', 'dn_w_out', 'final_norm']
SHARD_AXIS = {'ffn1_w_in': 2, 'ffn1_w_out': 1, 'xa_wq': 1, 'xa_wkv': 2, 'xa_wo': 1, 'ffn2_w_in': 2, 'ffn2_w_out': 1,
              'ab_w_in': 2, 'lru_conv_w': 2, 'ab_w_out': 1, 'dn_w_in': 2, 'dn_conv_w': 2, 'dn_w_out': 1}
GATHER_F32 = ('lru_conv_w', 'dn_conv_w')
TRANSPOSED = ('ffn1_w_in', 'xa_wkv', 'ffn2_w_in', 'ab_w_in', 'dn_w_in')


def _params(sem=None):
    return pltpu.CompilerParams(dimension_semantics=sem, vmem_limit_bytes=V7X_VMEM_LIMIT)


def _tile(n, pref, mult):
    best = None
    t = mult
    while t <= min(n, pref):
        if n % t == 0:
            best = t
        t += mult
    return n if best is None else best


def _dot(a, b, ca, cb, prec=None):
    return lax.dot_general(a, b, (((ca,), (cb,)), ((), ())), preferred_element_type=F32, precision=prec)


def _mm(a, b, *, name, ta=False, tb=False, out_dtype=F32, res=None, scale=1.0, tm=1024, tn=1024, tk=1024,
        norm=None, rms_bwd=None):
    m, kdim = (a.shape[1], a.shape[0]) if ta else a.shape
    n = b.shape[0] if tb else b.shape[1]
    assert (b.shape[1] if tb else b.shape[0]) == kdim
    tm = _tile(m, tm, LANES if ta else 16)
    tn = _tile(n, tn, LANES)
    tk = _tile(kdim, tk, LANES)
    nk = kdim // tk
    a_spec = pl.BlockSpec((tk, tm), lambda i, j, k: (k, i)) if ta else pl.BlockSpec((tm, tk), lambda i, j, k: (i, k))
    b_spec = pl.BlockSpec((tn, tk), lambda i, j, k: (j, k)) if tb else pl.BlockSpec((tk, tn), lambda i, j, k: (k, j))
    o_spec = pl.BlockSpec((tm, tn), lambda i, j, k: (i, j))
    vec = pl.BlockSpec((1, tn), lambda i, j, k: (0, j))
    ca, cb = (0 if ta else 1), (1 if tb else 0)
    has_res = res is not None
    assert (norm is None and rms_bwd is None) or tn == n

    ins, specs = [a, b], [a_spec, b_spec]
    if has_res:
        ins.append(res)
        specs.append(o_spec)
    out_specs, out_shape = [o_spec], [jax.ShapeDtypeStruct((m, n), out_dtype)]
    if norm is not None:
        ins.append(norm.reshape(1, n))
        specs.append(vec)
        out_specs.append(o_spec)
        out_shape.append(jax.ShapeDtypeStruct((m, n), BF16))
    if rms_bwd is not None:
        ins += [rms_bwd[0], rms_bwd[1].reshape(1, n), rms_bwd[2]]
        specs += [o_spec, vec, o_spec]
        out_specs.append(vec)
        out_shape.append(jax.ShapeDtypeStruct((1, n), F32))
    n_in, n_out = len(ins), len(out_specs)

    def finish(acc, extra, outs, first_rows):
        y = acc if scale == 1.0 else acc * scale
        if has_res:
            y = y + extra[0][...]
        tail = extra[has_res:]
        if norm is not None:
            rs = lax.rsqrt(jnp.mean(y * y, axis=-1, keepdims=True) + NORM_EPS)
            outs[1][...] = (y * rs * tail[0][...]).astype(BF16)
        if rms_bwd is not None:
            x_ref, g_ref, dres_ref = tail
            xv = x_ref[...]
            rs = lax.rsqrt(jnp.mean(xv * xv, axis=-1, keepdims=True) + NORM_EPS)
            xh = xv * rs
            dgh = y * g_ref[...]
            part = jnp.sum(y * xh, axis=0, keepdims=True)
            y = dres_ref[...] + rs * (dgh - xh * jnp.mean(dgh * xh, axis=-1, keepdims=True))

            @pl.when(first_rows)
            def _():
                outs[1][...] = part

            @pl.when(jnp.logical_not(first_rows))
            def _():
                outs[1][...] += part

        outs[0][...] = y.astype(out_dtype)

    def body(*refs):
        a_ref, b_ref = refs[0], refs[1]
        extra, outs = refs[2:n_in], refs[n_in:n_in + n_out]
        p = _dot(a_ref[...].astype(BF16), b_ref[...].astype(BF16), ca, cb)
        first_rows = pl.program_id(0) == 0
        if nk == 1:
            finish(p, extra, outs, first_rows)
            return
        acc = refs[n_in + n_out]
        k = pl.program_id(2)

        @pl.when(k == 0)
        def _():
            acc[...] = p

        @pl.when(k > 0)
        def _():
            acc[...] += p

        @pl.when(k == nk - 1)
        def _():
            finish(acc[...], extra, outs, first_rows)

    out = pl.pallas_call(
        body, name=name, grid=(m // tm, n // tn, nk), in_specs=specs, out_specs=out_specs, out_shape=out_shape,
        scratch_shapes=[] if nk == 1 else [pltpu.VMEM((tm, tn), F32)],
        compiler_params=_params(("parallel" if rms_bwd is None else "arbitrary", "parallel", "arbitrary")),
    )(*ins)
    if rms_bwd is not None:
        return out[0], out[1].reshape(n)
    return out[0] if n_out == 1 else tuple(out)


def _rms_fwd(x, g, *, name, out_dtype):
    r, d = x.shape
    tr = _tile(r, 512, 16)

    def body(x_ref, g_ref, o_ref):
        xv = x_ref[...]
        rs = lax.rsqrt(jnp.mean(xv * xv, axis=-1, keepdims=True) + NORM_EPS)
        o_ref[...] = (xv * rs * g_ref[...]).astype(out_dtype)

    return pl.pallas_call(
        body, name=name, grid=(r // tr,),
        in_specs=[pl.BlockSpec((tr, d), lambda i: (i, 0)), pl.BlockSpec((1, d), lambda i: (0, 0))],
        out_specs=pl.BlockSpec((tr, d), lambda i: (i, 0)),
        out_shape=jax.ShapeDtypeStruct((r, d), out_dtype), compiler_params=_params(("parallel",)),
    )(x, g.reshape(1, d))


def _rms_bwd(x, g, dh, dres, *, name):
    r, d = x.shape
    tr = _tile(r, 512, 8)
    has_res = dres is not None

    def body(*refs):
        x_ref, g_ref, dh_ref = refs[:3]
        r_ref = refs[3] if has_res else None
        dx_ref, dg_ref = refs[3 + has_res], refs[4 + has_res]
        xv = x_ref[...]
        rs = lax.rsqrt(jnp.mean(xv * xv, axis=-1, keepdims=True) + NORM_EPS)
        xh = xv * rs
        dhv = dh_ref[...]
        dgh = dhv * g_ref[...]
        dx = rs * (dgh - xh * jnp.mean(dgh * xh, axis=-1, keepdims=True))
        if has_res:
            dx = dx + r_ref[...]
        dx_ref[...] = dx
        part = jnp.sum(dhv * xh, axis=0, keepdims=True)

        @pl.when(pl.program_id(0) == 0)
        def _():
            dg_ref[...] = part

        @pl.when(pl.program_id(0) > 0)
        def _():
            dg_ref[...] += part

    row = pl.BlockSpec((tr, d), lambda i: (i, 0))
    vec = pl.BlockSpec((1, d), lambda i: (0, 0))
    ins, specs = [x, g.reshape(1, d), dh], [row, vec, row]
    if has_res:
        ins.append(dres)
        specs.append(row)
    dx, dg = pl.pallas_call(
        body, name=name, grid=(r // tr,), in_specs=specs, out_specs=[row, vec],
        out_shape=[jax.ShapeDtypeStruct((r, d), F32), jax.ShapeDtypeStruct((1, d), F32)],
        compiler_params=_params(("arbitrary",)),
    )(*ins)
    return dx, dg.reshape(d)


def _mm_swiglu(h, w_in, *, name, tm=1024, tn=256):
    t, d = h.shape
    f = w_in.shape[0] // 2
    tm, tn = _tile(t, tm, 16), _tile(f, tn, LANES)
    nj = f // tn

    def body(h_ref, wg_ref, wu_ref, g_ref, u_ref, a_ref):
        hv = h_ref[...]
        gate, up = _dot(hv, wg_ref[...], 1, 1), _dot(hv, wu_ref[...], 1, 1)
        g_ref[...] = gate.astype(BF16)
        u_ref[...] = up.astype(BF16)
        a_ref[...] = (gate * jax.nn.sigmoid(gate) * up).astype(BF16)

    out = pl.BlockSpec((tm, tn), lambda j, i: (i, j))
    return pl.pallas_call(
        body, name=name, grid=(nj, t // tm),
        in_specs=[pl.BlockSpec((tm, d), lambda j, i: (i, 0)), pl.BlockSpec((tn, d), lambda j, i: (j, 0)),
                  pl.BlockSpec((tn, d), lambda j, i: (j + nj, 0))],
        out_specs=[out, out, out], out_shape=[jax.ShapeDtypeStruct((t, f), BF16)] * 3,
        compiler_params=_params(("parallel", "parallel")),
    )(h, w_in, w_in)


def _mm_dswiglu(dy, w_out, gate, up, *, name, scale, tm=1024, tn=256):
    t, d = dy.shape
    f = w_out.shape[0]
    tm, tn = _tile(t, tm, 16), _tile(f, tn, LANES)

    def body(dy_ref, w_ref, g_ref, u_ref, dg_ref, du_ref):
        da = _dot(dy_ref[...].astype(BF16), w_ref[...], 1, 1) * scale
        gv = g_ref[...].astype(F32)
        s = jax.nn.sigmoid(gv)
        dg_ref[...] = (da * u_ref[...].astype(F32) * (s * (1.0 + gv * (1.0 - s)))).astype(BF16)
        du_ref[...] = (da * gv * s).astype(BF16)

    blk = pl.BlockSpec((tm, tn), lambda j, i: (i, j))
    return pl.pallas_call(
        body, name=name, grid=(f // tn, t // tm),
        in_specs=[pl.BlockSpec((tm, d), lambda j, i: (i, 0)), pl.BlockSpec((tn, d), lambda j, i: (j, 0)), blk, blk],
        out_specs=[blk, blk], out_shape=[jax.ShapeDtypeStruct((t, f), BF16)] * 2,
        compiler_params=_params(("parallel", "parallel")),
    )(dy, w_out, gate, up)


def _final_loss(x, g, target, *, name):
    r, d = x.shape
    tr = _tile(r, 512, 8)

    def body(x_ref, g_ref, t_ref, sq_ref, dx_ref, dg_ref):
        xv = x_ref[...]
        gv = g_ref[...]
        rs = lax.rsqrt(jnp.mean(xv * xv, axis=-1, keepdims=True) + NORM_EPS)
        xh = xv * rs
        err = xh * gv - t_ref[...]
        dy = err * (1.0 / d)
        dgh = dy * gv
        dx_ref[...] = rs * (dgh - xh * jnp.mean(dgh * xh, axis=-1, keepdims=True))
        sq = jnp.sum(err * err, axis=0, keepdims=True)
        part = jnp.sum(dy * xh, axis=0, keepdims=True)

        @pl.when(pl.program_id(0) == 0)
        def _():
            sq_ref[...] = sq
            dg_ref[...] = part

        @pl.when(pl.program_id(0) > 0)
        def _():
            sq_ref[...] += sq
            dg_ref[...] += part

    row = pl.BlockSpec((tr, d), lambda i: (i, 0))
    vec = pl.BlockSpec((1, d), lambda i: (0, 0))
    sq, dx, dg = pl.pallas_call(
        body, name=name, grid=(r // tr,), in_specs=[row, vec, row], out_specs=[vec, row, vec],
        out_shape=[jax.ShapeDtypeStruct((1, d), F32), jax.ShapeDtypeStruct((r, d), F32),
                   jax.ShapeDtypeStruct((1, d), F32)],
        compiler_params=_params(("arbitrary",)),
    )(x, g.reshape(1, d), target)
    return sq.reshape(d), dx, dg.reshape(d)


def _scan(a, b, *, name, reverse=False):
    t, w = a.shape
    tb = _tile(t, 1024, 8)
    nblk, ngrp = t // tb, tb // 8

    def body(a_ref, b_ref, h_ref, carry):
        @pl.when(pl.program_id(0) == 0)
        def _():
            carry[...] = jnp.zeros_like(carry)

        row = lax.broadcasted_iota(jnp.int32, (8, w), 0)

        def group(i, c):
            r0 = pl.multiple_of((ngrp - 1 - i if reverse else i) * 8, 8)
            av, bv = a_ref[pl.ds(r0, 8), :], b_ref[pl.ds(r0, 8), :]
            for s in (1, 2, 4):
                keep = row < 8 - s if reverse else row >= s
                shift = 8 - s if reverse else s
                bv = jnp.where(keep, bv + av * pltpu.roll(bv, shift, axis=0), bv)
                av = jnp.where(keep, av * pltpu.roll(av, shift, axis=0), av)
            hv = bv + av * c
            h_ref[pl.ds(r0, 8), :] = hv
            return hv[0:1, :] if reverse else hv[7:8, :]

        carry[0:1, :] = lax.fori_loop(0, ngrp, group, carry[0:1, :])

    blk = pl.BlockSpec((tb, w), lambda i: (nblk - 1 - i if reverse else i, 0))
    return pl.pallas_call(
        body, name=name, grid=(t // tb,), in_specs=[blk, blk], out_specs=blk,
        out_shape=jax.ShapeDtypeStruct((t, w), F32), scratch_shapes=[pltpu.VMEM((8, w), F32)],
        compiler_params=_params(("arbitrary",)),
    )(a, b)


def _row_specs(t, tb, width, col):
    per = tb // 8
    main = pl.BlockSpec((tb, width), lambda i: (i, col))
    before = pl.BlockSpec((8, width), lambda i: (jnp.maximum(i * per - 1, 0), col))
    after = pl.BlockSpec((8, width), lambda i: (jnp.minimum((i + 1) * per, t // 8 - 1), col))
    return main, before, after


def _with_rows_before(x_ref, before_ref):
    return jnp.concatenate([jnp.where(pl.program_id(0) > 0, before_ref[...], 0.0), x_ref[...]], axis=0)


def _tap(xe, s):
    return xe[8:] if s == 0 else pltpu.roll(xe, s, axis=0)[8:]


def _conv_rows(xe, w_ref):
    return sum(_tap(xe, CONV_K - 1 - k) * w_ref[k:k + 1, :] for k in range(CONV_K))


def _conv_bwd(dy, x, col, w, *, name):
    t, width = dy.shape
    tb = _tile(t, 256, 8)
    nblk = t // tb

    def body(dy_ref, dy_after_ref, x_ref, x_before_ref, w_ref, dx_ref, dw_ref):
        i = pl.program_id(0)
        dyv = dy_ref[...]
        dye = jnp.concatenate([dyv, jnp.where(i < nblk - 1, dy_after_ref[...], 0.0)], axis=0)
        dx = dyv * w_ref[CONV_K - 1:CONV_K, :]
        for s in range(1, CONV_K):
            dx = dx + pltpu.roll(dye, tb + 8 - s, axis=0)[:tb] * w_ref[CONV_K - 1 - s:CONV_K - s, :]
        dx_ref[...] = dx.astype(BF16)
        xe = _with_rows_before(x_ref, x_before_ref)

        @pl.when(i == 0)
        def _():
            dw_ref[...] = jnp.zeros_like(dw_ref)

        for k in range(CONV_K):
            dw_ref[k:k + 1, :] += jnp.sum(dyv * _tap(xe, CONV_K - 1 - k), axis=0, keepdims=True)

    main, _, after = _row_specs(t, tb, width, 0)
    xmain, xbefore, _ = _row_specs(t, tb, width, col)
    dx, dw = pl.pallas_call(
        body, name=name, grid=(nblk,),
        in_specs=[main, after, xmain, xbefore, pl.BlockSpec((CONV_K, width), lambda i: (0, 0))],
        out_specs=[main, pl.BlockSpec((8, width), lambda i: (0, 0))],
        out_shape=[jax.ShapeDtypeStruct((t, width), BF16), jax.ShapeDtypeStruct((8, width), F32)],
        compiler_params=_params(("arbitrary",)),
    )(dy, dy, x, x, w)
    return dx, dw[:CONV_K]


def _expm1(x):
    small = x * (1.0 + x * (0.5 + x * (1.0 / 6.0 + x * (1.0 / 24.0 + x * (1.0 / 120.0 + x * (1.0 / 720.0))))))
    return jnp.where(jnp.abs(x) < 0.1, small, jnp.exp(x) - 1.0)


def _lru_gate_terms(xc, wa_ref, ba_ref, wx_ref, bx_ref, sp_ref):
    xb = xc.astype(BF16)
    r = jax.nn.sigmoid(_dot(xb, wa_ref[...], 1, 0) + ba_ref[...])
    i = jax.nn.sigmoid(_dot(xb, wx_ref[...], 1, 0) + bx_ref[...])
    log_a = -r * sp_ref[...]
    return r, i, jnp.exp(log_a), jnp.sqrt(-_expm1(2.0 * log_a))


def _lru_gates_fwd(proj, col, conv_w, conv_b, wa, ba, wx, bx, sp, *, name):
    t = proj.shape[0]
    width = conv_w.shape[1]
    tb = _tile(t, 512, 8)

    def body(x_ref, x_before_ref, cw_ref, cb_ref, wa_ref, ba_ref, wx_ref, bx_ref, sp_ref, a_ref, b_ref, xc_ref):
        xc = _conv_rows(_with_rows_before(x_ref, x_before_ref), cw_ref) + cb_ref[...]
        r, i, a, mult = _lru_gate_terms(xc, wa_ref, ba_ref, wx_ref, bx_ref, sp_ref)
        a_ref[...] = a
        b_ref[...] = mult * i * xc
        xc_ref[...] = xc

    main, before, _ = _row_specs(t, tb, width, col)
    out = pl.BlockSpec((tb, width), lambda i: (i, 0))
    vec = pl.BlockSpec((1, width), lambda i: (0, 0))
    mat = pl.BlockSpec((width, width), lambda i: (0, 0))
    return pl.pallas_call(
        body, name=name, grid=(t // tb,),
        in_specs=[main, before, pl.BlockSpec((CONV_K, width), lambda i: (0, 0)), vec, mat, vec, mat, vec, vec],
        out_specs=[out] * 3, out_shape=[jax.ShapeDtypeStruct((t, width), F32)] * 3,
        compiler_params=_params(("parallel",)),
    )(proj, proj, conv_w, conv_b.reshape(1, -1), wa, ba.reshape(1, -1), wx, bx.reshape(1, -1), sp.reshape(1, -1))


def _lru_gates_bwd(xc, dtot, h_prev, wa, ba, wx, bx, sp, *, name):
    t, width = xc.shape
    tb = _tile(t, 512, 8)

    def body(xc_ref, dt_ref, hp_ref, wa_ref, ba_ref, wx_ref, bx_ref, sp_ref, dxc_ref, dwa_ref, dwx_ref, vec_ref):
        xc = xc_ref[...]
        r, i, a, mult = _lru_gate_terms(xc, wa_ref, ba_ref, wx_ref, bx_ref, sp_ref)
        db = dt_ref[...]
        d_la = db * hp_ref[...] * a - db * i * xc * (a * a / mult)
        d_pa = (-d_la * sp_ref[...]) * r * (1.0 - r)
        d_pi = db * mult * xc * i * (1.0 - i)
        dab, dib = d_pa.astype(BF16), d_pi.astype(BF16)
        dxc = db * mult * i + _dot(dab, wa_ref[...], 1, 1) + _dot(dib, wx_ref[...], 1, 1)
        dxc_ref[...] = dxc
        xb = xc.astype(BF16)
        rows = [jnp.sum(z, axis=0, keepdims=True) for z in (d_pa, d_pi, -d_la * r, dxc)]

        @pl.when(pl.program_id(0) == 0)
        def _():
            dwa_ref[...] = jnp.zeros_like(dwa_ref)
            dwx_ref[...] = jnp.zeros_like(dwx_ref)
            vec_ref[...] = jnp.zeros_like(vec_ref)

        dwa_ref[...] += _dot(xb, dab, 0, 0)
        dwx_ref[...] += _dot(xb, dib, 0, 0)
        for j, z in enumerate(rows):
            vec_ref[j:j + 1, :] += z

    blk = pl.BlockSpec((tb, width), lambda i: (i, 0))
    vec = pl.BlockSpec((1, width), lambda i: (0, 0))
    mat = pl.BlockSpec((width, width), lambda i: (0, 0))
    dxc, dwa, dwx, vecs = pl.pallas_call(
        body, name=name, grid=(t // tb,), in_specs=[blk, blk, blk, mat, vec, mat, vec, vec],
        out_specs=[blk, mat, mat, pl.BlockSpec((8, width), lambda i: (0, 0))],
        out_shape=[jax.ShapeDtypeStruct((t, width), F32), jax.ShapeDtypeStruct((width, width), F32),
                   jax.ShapeDtypeStruct((width, width), F32), jax.ShapeDtypeStruct((8, width), F32)],
        compiler_params=_params(("arbitrary",)),
    )(xc, dtot, h_prev, wa, ba.reshape(1, -1), wx, bx.reshape(1, -1), sp.reshape(1, -1))
    return dxc, dwa, dwx, vecs[:4]


GELU_C = math.sqrt(2.0 / math.pi)


def _gelu_terms(x):
    th = jnp.tanh(GELU_C * (x + 0.044715 * x * x * x))
    return 0.5 * x * (1.0 + th), 0.5 * (1.0 + th) + 0.5 * x * (1.0 - th * th) * GELU_C * (1.0 + 3 * 0.044715 * x * x)


def _mix_join_fwd(outs, lses, hs, proj, gr_col, *, name):
    t, w = hs.shape
    tb = _tile(t, 512, 16)
    n = len(outs)

    def body(*refs):
        o_refs, l_refs = refs[:n], refs[n:2 * n]
        hs_ref, gr_ref, cat_ref, attn_ref, lse_ref = refs[2 * n:]
        ls = [r[...] for r in l_refs]
        m = ls[0]
        for l in ls[1:]:
            m = jnp.maximum(m, l)
        ws = [jnp.exp(l - m) for l in ls]
        den = sum(ws)
        attn = sum(wt * r[...] for wt, r in zip(ws, o_refs)) / den
        attn_ref[...] = attn
        lse_ref[...] = m + jnp.log(den)
        cat_ref[:, :w] = attn.astype(BF16)
        cat_ref[:, w:] = (hs_ref[...] * _gelu_terms(gr_ref[...])[0]).astype(BF16)

    blk = pl.BlockSpec((tb, w), lambda i: (i, 0))
    return pl.pallas_call(
        body, name=name, grid=(t // tb,),
        in_specs=[blk] * (2 * n + 1) + [pl.BlockSpec((tb, w), lambda i: (i, gr_col))],
        out_specs=[pl.BlockSpec((tb, 2 * w), lambda i: (i, 0)), blk, blk],
        out_shape=[jax.ShapeDtypeStruct((t, 2 * w), BF16), jax.ShapeDtypeStruct((t, w), F32),
                   jax.ShapeDtypeStruct((t, w), F32)],
        compiler_params=_params(("parallel",)),
    )(*outs, *lses, hs, proj)


def _mix_join_bwd(dcat, attn, lse, hs, proj, gr_col, *, name):
    t, w = hs.shape
    hd = w // ATTN_HEADS
    tb = _tile(t, 512, 16)

    def body(dcat_ref, attn_ref, lse_ref, hs_ref, gr_ref, stats_ref, dhs_ref, dgr_ref):
        stats_ref[...] = jnp.zeros_like(stats_ref)
        for h in range(ATTN_HEADS):
            sl = slice(h * hd, (h + 1) * hd)
            stats_ref[:, h:h + 1] = lse_ref[:, h * hd:h * hd + 1]
            stats_ref[:, ATTN_HEADS + h:ATTN_HEADS + h + 1] = jnp.sum(dcat_ref[:, sl] * attn_ref[:, sl], axis=-1, keepdims=True)
        dy = dcat_ref[:, w:]
        g, dg = _gelu_terms(gr_ref[...])
        dhs_ref[...] = dy * g
        dgr_ref[...] = (dy * hs_ref[...] * dg).astype(BF16)

    blk = pl.BlockSpec((tb, w), lambda i: (i, 0))
    return pl.pallas_call(
        body, name=name, grid=(t // tb,),
        in_specs=[pl.BlockSpec((tb, 2 * w), lambda i: (i, 0)), blk, blk, blk, pl.BlockSpec((tb, w), lambda i: (i, gr_col))],
        out_specs=[pl.BlockSpec((tb, LANES), lambda i: (i, 0)), blk, blk],
        out_shape=[jax.ShapeDtypeStruct((t, LANES), F32), jax.ShapeDtypeStruct((t, w), F32),
                   jax.ShapeDtypeStruct((t, w), BF16)],
        compiler_params=_params(("parallel",)),
    )(dcat, attn, lse, hs, proj)


def _silu_terms(x):
    s = jax.nn.sigmoid(x)
    return x * s, s * (1.0 + x * (1.0 - s))


def _dn_prep_fwd(proj, conv_w, *, name):
    t = proj.shape[0]
    w3 = conv_w.shape[1]
    w = w3 // 3
    hd = w // DN_HEADS
    tb = _tile(t, 256, 8)

    def body(x_ref, x_before_ref, cw_ref, q_ref, k_ref, v_ref, c_ref):
        c = _conv_rows(_with_rows_before(x_ref, x_before_ref), cw_ref)
        c_ref[...] = c
        s = _silu_terms(c)[0]
        v_ref[...] = s[:, 2 * w:]
        for part, ref, scale in ((0, q_ref, hd ** -0.5), (1, k_ref, 1.0)):
            for h in range(DN_HEADS):
                z = s[:, part * w + h * hd:part * w + (h + 1) * hd]
                ref[:, h * hd:(h + 1) * hd] = z * (lax.rsqrt(jnp.sum(z * z, axis=-1, keepdims=True) + 1e-6) * scale)

    main, before, _ = _row_specs(t, tb, w3, 0)
    out = pl.BlockSpec((tb, w), lambda i: (i, 0))
    return pl.pallas_call(
        body, name=name, grid=(t // tb,), in_specs=[main, before, pl.BlockSpec((CONV_K, w3), lambda i: (0, 0))],
        out_specs=[out, out, out, pl.BlockSpec((tb, w3), lambda i: (i, 0))],
        out_shape=[jax.ShapeDtypeStruct((t, w), F32)] * 3 + [jax.ShapeDtypeStruct((t, w3), F32)],
        compiler_params=_params(("parallel",)),
    )(proj, proj, conv_w)


def _dn_prep_bwd(c, dq, dk, dv, *, name):
    t, w3 = c.shape
    w = w3 // 3
    hd = w // DN_HEADS
    tb = _tile(t, 256, 8)

    def body(c_ref, dq_ref, dk_ref, dv_ref, dc_ref):
        cv = c_ref[...]
        s, ds = _silu_terms(cv)
        dc_ref[:, 2 * w:] = dv_ref[...] * ds[:, 2 * w:]
        for part, ref, scale in ((0, dq_ref, hd ** -0.5), (1, dk_ref, 1.0)):
            for h in range(DN_HEADS):
                cols = slice(part * w + h * hd, part * w + (h + 1) * hd)
                z = s[:, cols]
                rn = lax.rsqrt(jnp.sum(z * z, axis=-1, keepdims=True) + 1e-6)
                y = z * rn
                dy = ref[:, h * hd:(h + 1) * hd] * scale
                dc_ref[:, cols] = rn * (dy - y * jnp.sum(dy * y, axis=-1, keepdims=True)) * ds[:, cols]

    blk = pl.BlockSpec((tb, w), lambda i: (i, 0))
    wide = pl.BlockSpec((tb, w3), lambda i: (i, 0))
    return pl.pallas_call(
        body, name=name, grid=(t // tb,), in_specs=[wide, blk, blk, blk], out_specs=wide,
        out_shape=jax.ShapeDtypeStruct((t, w3), F32), compiler_params=_params(("parallel",)),
    )(c, dq, dk, dv)


def _dn_gate_fwd(o, proj, z_col, o_norm, *, name):
    t, w = o.shape
    hd = w // DN_HEADS
    tb = _tile(t, 512, 16)

    def body(o_ref, z_ref, g_ref, y_ref):
        for h in range(DN_HEADS):
            sl = slice(h * hd, (h + 1) * hd)
            ov = o_ref[:, sl]
            rn = lax.rsqrt(jnp.mean(ov * ov, axis=-1, keepdims=True) + NORM_EPS)
            y_ref[:, sl] = (ov * rn * g_ref[...] * _silu_terms(z_ref[:, sl])[0]).astype(BF16)

    blk = pl.BlockSpec((tb, w), lambda i: (i, 0))
    return pl.pallas_call(
        body, name=name, grid=(t // tb,),
        in_specs=[blk, pl.BlockSpec((tb, w), lambda i: (i, z_col)), pl.BlockSpec((1, hd), lambda i: (0, 0))],
        out_specs=blk, out_shape=jax.ShapeDtypeStruct((t, w), BF16), compiler_params=_params(("parallel",)),
    )(o, proj, o_norm.reshape(1, hd))


def _dn_gate_bwd(o, proj, z_col, o_norm, dy, *, name):
    t, w = o.shape
    hd = w // DN_HEADS
    tb = _tile(t, 512, 16)

    def body(o_ref, z_ref, g_ref, dy_ref, do_ref, dz_ref, dg_ref):
        gv = g_ref[...]
        dg = jnp.zeros((1, hd), F32)
        for h in range(DN_HEADS):
            sl = slice(h * hd, (h + 1) * hd)
            ov, dyv = o_ref[:, sl], dy_ref[:, sl]
            sz, dsz = _silu_terms(z_ref[:, sl])
            rn = lax.rsqrt(jnp.mean(ov * ov, axis=-1, keepdims=True) + NORM_EPS)
            nv = ov * rn
            dz_ref[:, sl] = (dyv * nv * gv * dsz).astype(BF16)
            dn = dyv * gv * sz
            do_ref[:, sl] = rn * (dn - nv * jnp.mean(dn * nv, axis=-1, keepdims=True))
            dg = dg + jnp.sum(dyv * nv * sz, axis=0, keepdims=True)

        @pl.when(pl.program_id(0) == 0)
        def _():
            dg_ref[...] = dg

        @pl.when(pl.program_id(0) > 0)
        def _():
            dg_ref[...] += dg

    blk = pl.BlockSpec((tb, w), lambda i: (i, 0))
    vec = pl.BlockSpec((1, hd), lambda i: (0, 0))
    do, dz, dg = pl.pallas_call(
        body, name=name, grid=(t // tb,), in_specs=[blk, pl.BlockSpec((tb, w), lambda i: (i, z_col)), vec, blk],
        out_specs=[blk, blk, vec],
        out_shape=[jax.ShapeDtypeStruct((t, w), F32), jax.ShapeDtypeStruct((t, w), BF16),
                   jax.ShapeDtypeStruct((1, hd), F32)],
        compiler_params=_params(("arbitrary",)),
    )(o, proj, o_norm.reshape(1, hd), dy)
    return do, dz, dg.reshape(hd)


def _band_masks(n):
    qi = lax.broadcasted_iota(jnp.int32, (ATTN_BLOCK, ATTN_BLOCK), 0)
    kj = lax.broadcasted_iota(jnp.int32, (ATTN_BLOCK, ATTN_BLOCK), 1)
    return kj <= qi, jnp.logical_and(kj >= qi, n > 0)


def _dattn_fwd(qkv, dil, *, name):
    seq, w = qkv.shape[0], qkv.shape[1] // (3 * dil)
    t = seq * dil
    hd = w // ATTN_HEADS
    assert seq % ATTN_BLOCK == 0
    nb = seq // ATTN_BLOCK
    scale = hd ** -0.5

    def body(q_ref, kp_ref, kc_ref, vp_ref, vc_ref, o_ref, lse_ref):
        mc, mp = _band_masks(pl.program_id(1))
        heads = range(ATTN_HEADS)
        sls = [slice(h * hd, (h + 1) * hd) for h in heads]
        qs = [(q_ref[:, sl] * scale).astype(BF16) for sl in sls]
        scs = [jnp.where(mc, _dot(qs[h], kc_ref[:, sls[h]].astype(BF16), 1, 1), NEG) for h in heads]
        sps = [jnp.where(mp, _dot(qs[h], kp_ref[:, sls[h]].astype(BF16), 1, 1), NEG) for h in heads]
        ms = [jnp.maximum(jnp.max(sc, axis=-1, keepdims=True), jnp.max(sp, axis=-1, keepdims=True))
              for sc, sp in zip(scs, sps)]
        pcs = [jnp.exp(sc - m) for sc, m in zip(scs, ms)]
        pps = [jnp.exp(sp - m) for sp, m in zip(sps, ms)]
        dens = [jnp.sum(pc, axis=-1, keepdims=True) + jnp.sum(pp, axis=-1, keepdims=True) for pc, pp in zip(pcs, pps)]
        outs = [_dot(pcs[h].astype(BF16), vc_ref[:, sls[h]].astype(BF16), 1, 0)
                + _dot(pps[h].astype(BF16), vp_ref[:, sls[h]].astype(BF16), 1, 0) for h in heads]
        for h in heads:
            o_ref[:, sls[h]] = outs[h] / dens[h]
            lse_ref[:, sls[h]] = jnp.broadcast_to(ms[h] + jnp.log(dens[h]), (ATTN_BLOCK, hd))

    cur = pl.BlockSpec((ATTN_BLOCK, w), lambda r, n: (n, r))
    part = lambda j, row: pl.BlockSpec((ATTN_BLOCK, w), lambda r, n: (row(n), 3 * r + j))
    here, before = (lambda n: n), (lambda n: jnp.maximum(n - 1, 0))
    o, lse = pl.pallas_call(
        body, name=name, grid=(dil, nb),
        in_specs=[part(0, here), part(1, before), part(1, here), part(2, before), part(2, here)], out_specs=[cur, cur],
        out_shape=[jax.ShapeDtypeStruct((seq, dil * w), F32)] * 2, compiler_params=_params(("parallel", "parallel")),
    )(qkv, qkv, qkv, qkv, qkv)
    return o.reshape(t, w), lse.reshape(t, w)


def _dattn_bwd(qkv, do, stats, dil, *, name):
    t, w = do.shape
    hd = w // ATTN_HEADS
    seq = t // dil
    nb = seq // ATTN_BLOCK
    scale = hd ** -0.5

    def body(qc_ref, qn_ref, doc_ref, don_ref, sc_ref, sn_ref, kp_ref, kc_ref, vp_ref, vc_ref, dqkv_ref):
        n = pl.program_id(1)
        mc, mp = _band_masks(n)
        _, mx = _band_masks(jnp.where(n + 1 < nb, 1, 0))
        heads = range(ATTN_HEADS)
        sls = [slice(h * hd, (h + 1) * hd) for h in heads]
        lse_of, delta_of = (lambda ref, h: ref[:, h:h + 1]), (lambda ref, h: ref[:, ATTN_HEADS + h:ATTN_HEADS + h + 1])
        bf = lambda ref, scl=None: [(ref[:, sl] if scl is None else ref[:, sl] * scl).astype(BF16) for sl in sls]
        qc, qn, kc, kp = bf(qc_ref, scale), bf(qn_ref, scale), bf(kc_ref), bf(kp_ref)
        vc, vp, doc, don = bf(vc_ref), bf(vp_ref), bf(doc_ref), bf(don_ref)
        p_c = [jnp.exp(jnp.where(mc, _dot(qc[h], kc[h], 1, 1), NEG) - lse_of(sc_ref, h)) for h in heads]
        p_p = [jnp.exp(jnp.where(mp, _dot(qc[h], kp[h], 1, 1), NEG) - lse_of(sc_ref, h)) for h in heads]
        p_x = [jnp.exp(jnp.where(mx, _dot(qn[h], kc[h], 1, 1), NEG) - lse_of(sn_ref, h)) for h in heads]
        ds_c = [(p_c[h] * (_dot(doc[h], vc[h], 1, 1) - delta_of(sc_ref, h))).astype(BF16) for h in heads]
        ds_p = [(p_p[h] * (_dot(doc[h], vp[h], 1, 1) - delta_of(sc_ref, h))).astype(BF16) for h in heads]
        ds_x = [(p_x[h] * (_dot(don[h], vc[h], 1, 1) - delta_of(sn_ref, h))).astype(BF16) for h in heads]
        for h in heads:
            at = lambda part: slice(part * w + h * hd, part * w + (h + 1) * hd)
            dqkv_ref[:, at(0)] = ((_dot(ds_c[h], kc[h], 1, 0) + _dot(ds_p[h], kp[h], 1, 0)) * scale).astype(BF16)
            dqkv_ref[:, at(1)] = (_dot(ds_c[h], qc[h], 0, 0) + _dot(ds_x[h], qn[h], 0, 0)).astype(BF16)
            dqkv_ref[:, at(2)] = (_dot(p_c[h].astype(BF16), doc[h], 0, 0) + _dot(p_x[h].astype(BF16), don[h], 0, 0)).astype(BF16)

    cur = pl.BlockSpec((ATTN_BLOCK, w), lambda r, n: (n, r))
    nxt = pl.BlockSpec((ATTN_BLOCK, w), lambda r, n: (jnp.minimum(n + 1, nb - 1), r))
    part = lambda j, row: pl.BlockSpec((ATTN_BLOCK, w), lambda r, n: (row(n), 3 * r + j))
    here, before, after = (lambda n: n), (lambda n: jnp.maximum(n - 1, 0)), (lambda n: jnp.minimum(n + 1, nb - 1))
    stat = lambda row: pl.BlockSpec((ATTN_BLOCK, LANES), lambda r, n: (row(n), r))
    do, stats = do.reshape(seq, dil * w), stats.reshape(seq, dil * LANES)
    dqkv = pl.pallas_call(
        body, name=name, grid=(dil, nb),
        in_specs=[part(0, here), part(0, after), cur, nxt, stat(here), stat(after), part(1, before), part(1, here),
                  part(2, before), part(2, here)], out_specs=pl.BlockSpec((ATTN_BLOCK, 3 * w), lambda r, n: (n, r)),
        out_shape=jax.ShapeDtypeStruct((seq, dil * 3 * w), BF16), compiler_params=_params(("parallel", "parallel")),
    )(qkv, qkv, do, do, stats, stats, qkv, qkv, qkv, qkv)
    return dqkv.reshape(t, 3 * w)


def _xattn_fwd(q, k, v, *, name):
    t, w = q.shape
    nm = k.shape[0]
    hd = w // XA_HEADS
    scale = hd ** -0.5
    tq = _tile(t, 512, 8)

    def body(q_ref, k_ref, v_ref, o_ref):
        heads = range(XA_HEADS)
        sls = [slice(h * hd, (h + 1) * hd) for h in heads]
        ss = [_dot((q_ref[:, sl] * scale).astype(BF16), k_ref[:, sl].astype(BF16), 1, 1) for sl in sls]
        ps = [jnp.exp(s - jnp.max(s, axis=-1, keepdims=True)) for s in ss]
        ps = [p / jnp.sum(p, axis=-1, keepdims=True) for p in ps]
        for h in heads:
            o_ref[:, sls[h]] = _dot(ps[h].astype(BF16), v_ref[:, sls[h]].astype(BF16), 1, 0).astype(BF16)

    qs = pl.BlockSpec((tq, w), lambda i: (i, 0))
    ks = pl.BlockSpec((nm, w), lambda i: (0, 0))
    return pl.pallas_call(
        body, name=name, grid=(t // tq,), in_specs=[qs, ks, ks], out_specs=qs,
        out_shape=jax.ShapeDtypeStruct((t, w), BF16), compiler_params=_params(("parallel",)),
    )(q, k, v)


def _xattn_bwd(q, k, v, do, *, name):
    t, w = q.shape
    nm = k.shape[0]
    hd = w // XA_HEADS
    scale = hd ** -0.5
    tq = _tile(t, 512, 8)

    def body(q_ref, k_ref, v_ref, do_ref, dq_ref, dk_ref, dv_ref):
        first = pl.program_id(0) == 0
        heads = range(XA_HEADS)
        sls = [slice(h * hd, (h + 1) * hd) for h in heads]
        qs_ = [(q_ref[:, sl] * scale).astype(BF16) for sl in sls]
        ks_, vs_, dos = ([ref[:, sl].astype(BF16) for sl in sls] for ref in (k_ref, v_ref, do_ref))
        ss = [_dot(qs_[h], ks_[h], 1, 1) for h in heads]
        ps = [jnp.exp(s - jnp.max(s, axis=-1, keepdims=True)) for s in ss]
        ps = [p / jnp.sum(p, axis=-1, keepdims=True) for p in ps]
        dps = [_dot(dos[h], vs_[h], 1, 1) for h in heads]
        dss = [(p * (dp - jnp.sum(p * dp, axis=-1, keepdims=True))).astype(BF16) for p, dp in zip(ps, dps)]
        for h in heads:
            sl = sls[h]
            dq_ref[:, sl] = (_dot(dss[h], ks_[h], 1, 0) * scale).astype(BF16)
            dk = _dot(dss[h], qs_[h], 0, 0)
            dv = _dot(ps[h].astype(BF16), dos[h], 0, 0)

            @pl.when(first)
            def _():
                dk_ref[:, sl] = dk
                dv_ref[:, sl] = dv

            @pl.when(jnp.logical_not(first))
            def _():
                dk_ref[:, sl] += dk
                dv_ref[:, sl] += dv

    qs = pl.BlockSpec((tq, w), lambda i: (i, 0))
    ks = pl.BlockSpec((nm, w), lambda i: (0, 0))
    return pl.pallas_call(
        body, name=name, grid=(t // tq,), in_specs=[qs, ks, ks, qs], out_specs=[qs, ks, ks],
        out_shape=[jax.ShapeDtypeStruct((t, w), BF16), jax.ShapeDtypeStruct((nm, w), F32),
                   jax.ShapeDtypeStruct((nm, w), F32)],
        compiler_params=_params(("arbitrary",)),
    )(q, k, v, do)


def _dn_head_terms(q_ref, k_ref, gc_ref, gcr_ref, b_ref, h, hd, j=0):
    c = DN_CHUNK
    sl, rs = slice(h * hd, (h + 1) * hd), slice(j * c, (j + 1) * c)
    qh, kh = q_ref[rs, sl], k_ref[rs, sl]
    gcc, gcr_h, bh = gc_ref[rs, h:h + 1], gcr_ref[j, h:h + 1, :], b_ref[rs, h:h + 1]
    row = lax.broadcasted_iota(jnp.int32, (c, c), 0)
    col = lax.broadcasted_iota(jnp.int32, (c, c), 1)
    decay = jnp.exp(jnp.where(row >= col, gcc - gcr_h, NEG))
    kb = kh * bh
    kkt = _dot(kb.astype(BF16), kh.astype(BF16), 1, 1)
    qkt = _dot(qh.astype(BF16), kh.astype(BF16), 1, 1)
    gl = gcc[c - 1:c, :]
    return dict(sl=sl, rs=rs, j=j, h=h, q=qh, k=kh, gcc=gcc, b=bh, row=row, col=col, decay=decay, kb=kb, kkt=kkt,
                qkt=qkt, e=jnp.exp(gcc), el=jnp.exp(gl), r=jnp.exp(gl - gcc))


def _dn_local_fwd(q, k, v, gc, gcr, beta, *, name):
    t, w = q.shape
    hd = w // DN_HEADS
    c = DN_CHUNK
    nch = t // c
    grp = _tile(nch, DN_LOCAL_CHUNKS, 1)

    def body(q_ref, k_ref, v_ref, gc_ref, gcr_ref, b_ref,
             u_ref, wq_ref, wqt_ref, kr_ref, krt_ref, qk_ref, qkt_ref, invt_ref):
        tm = [_dn_head_terms(q_ref, k_ref, gc_ref, gcr_ref, b_ref, h, hd, j) for j in range(grp) for h in range(DN_HEADS)]
        pw = [jnp.where(m['row'] > m['col'], m['kkt'] * m['decay'], 0.0) for m in tm]
        inv = [(m['row'] == m['col']).astype(F32) - p for m, p in zip(tm, pw)]
        for _ in range(int(math.log2(c)) - 1):
            pw = [_dot(p.astype(BF16), p.astype(BF16), 1, 0) for p in pw]
            inv = [i + _dot(i.astype(BF16), p.astype(BF16), 1, 0) for i, p in zip(inv, pw)]
        for m, iv in zip(tm, inv):
            j, h = m['j'], m['h']
            rhs = jnp.concatenate([v_ref[m['rs'], m['sl']] * m['b'], m['kb'] * m['e']], axis=1).astype(BF16)
            sol = _dot(iv.astype(BF16), rhs, 1, 0)
            u_ref[m['rs'], m['sl']] = sol[:, :hd]
            wq = jnp.concatenate([sol[:, hd:], m['q'] * m['e']], axis=0)
            kr = m['k'] * m['r']
            qk = m['qkt'] * m['decay']
            wq_ref[j, h], wqt_ref[j, h] = wq.astype(BF16), wq.T.astype(BF16)
            kr_ref[j, h], krt_ref[j, h] = kr.astype(BF16), kr.T.astype(BF16)
            qk_ref[j, h], qkt_ref[j, h] = qk.astype(BF16), qk.T.astype(BF16)
            invt_ref[j, h] = iv.T.astype(BF16)

    rows = pl.BlockSpec((grp * c, w), lambda n: (n, 0))
    cols = pl.BlockSpec((grp * c, DN_HEADS), lambda n: (n, 0))
    rowg = pl.BlockSpec((grp, DN_HEADS, c), lambda n: (n, 0, 0))
    per = lambda a, b: (pl.BlockSpec((grp, DN_HEADS, a, b), lambda n: (n, 0, 0, 0)),
                        jax.ShapeDtypeStruct((nch, DN_HEADS, a, b), BF16))
    outs = [(rows, jax.ShapeDtypeStruct((t, w), F32)), per(2 * c, hd), per(hd, 2 * c), per(c, hd), per(hd, c),
            per(c, c), per(c, c), per(c, c)]
    return pl.pallas_call(
        body, name=name, grid=(nch // grp,), in_specs=[rows, rows, rows, cols, rowg, cols],
        out_specs=[o[0] for o in outs], out_shape=[o[1] for o in outs], compiler_params=_params(("parallel",)),
    )(q, k, v, gc, gcr, beta)


def _dn_state_fwd(u, wq, krt, qk, gc, *, name):
    t, w = u.shape
    hd = w // DN_HEADS
    c = DN_CHUNK
    nch = t // c
    grp = _tile(nch, DN_STEP_CHUNKS, 1)

    def body(u_ref, wq_ref, krt_ref, qk_ref, gc_ref, o_ref, vn_ref, s_ref, state):
        @pl.when(pl.program_id(0) == 0)
        def _():
            state[...] = jnp.zeros_like(state)

        heads = range(DN_HEADS)
        sls = [slice(h * hd, (h + 1) * hd) for h in heads]
        shs = [state[h] for h in heads]
        for j in range(grp):
            rs = slice(j * c, (j + 1) * c)
            wss = [_dot(wq_ref[j, h], shs[h].astype(BF16), 1, 0) for h in heads]
            vns = [(u_ref[rs, sls[h]] - wss[h][:c]).astype(BF16) for h in heads]
            outs = [wss[h][c:] + _dot(qk_ref[j, h], vns[h], 1, 0) for h in heads]
            for h in heads:
                s_ref[j, h] = shs[h]
                vn_ref[rs, sls[h]] = vns[h]
                o_ref[rs, sls[h]] = outs[h]
            shs = [shs[h] * jnp.exp(gc_ref[(j + 1) * c - 1:(j + 1) * c, h:h + 1]) + _dot(krt_ref[j, h], vns[h], 1, 0)
                   for h in heads]
        for h in heads:
            state[h] = shs[h]

    rows = pl.BlockSpec((grp * c, w), lambda n: (n, 0))
    per = lambda a, b: pl.BlockSpec((grp, DN_HEADS, a, b), lambda n: (n, 0, 0, 0))
    return pl.pallas_call(
        body, name=name, grid=(nch // grp,),
        in_specs=[rows, per(2 * c, hd), per(hd, c), per(c, c), pl.BlockSpec((grp * c, DN_HEADS), lambda n: (n, 0))],
        out_specs=[rows, rows, per(hd, hd)],
        out_shape=[jax.ShapeDtypeStruct((t, w), F32), jax.ShapeDtypeStruct((t, w), BF16),
                   jax.ShapeDtypeStruct((nch, DN_HEADS, hd, hd), F32)],
        scratch_shapes=[pltpu.VMEM((DN_HEADS, hd, hd), F32)], compiler_params=_params(("arbitrary",)),
    )(u, wq, krt, qk, gc)


def _dn_state_bwd(do, qkt, kr, vn, states, wqt, gc, *, name):
    t, w = do.shape
    hd = w // DN_HEADS
    c = DN_CHUNK
    nch = t // c
    grp = _tile(nch, DN_STEP_CHUNKS, 1)

    def body(do_ref, qkt_ref, kr_ref, vn_ref, s_ref, wqt_ref, gc_ref, dvn_ref, dkr_ref, del_ref, dstate):
        @pl.when(pl.program_id(0) == 0)
        def _():
            dstate[...] = jnp.zeros_like(dstate)

        heads = range(DN_HEADS)
        sls = [slice(h * hd, (h + 1) * hd) for h in heads]
        dsns = [dstate[h] for h in heads]
        for j in reversed(range(grp)):
            rs = slice(j * c, (j + 1) * c)
            dsbs = [d.astype(BF16) for d in dsns]
            dobs = [do_ref[rs, sl].astype(BF16) for sl in sls]
            dvns = [(_dot(qkt_ref[j, h], dobs[h], 1, 0) + _dot(kr_ref[j, h], dsbs[h], 1, 0)).astype(BF16) for h in heads]
            dkrs = [_dot(vn_ref[rs, sls[h]], dsbs[h], 1, 1) for h in heads]
            for h in heads:
                dvn_ref[rs, sls[h]] = dvns[h]
                dkr_ref[rs, sls[h]] = dkrs[h]
                d_el = jnp.sum(jnp.sum(dsns[h] * s_ref[j, h], axis=1, keepdims=True), axis=0, keepdims=True)
                del_ref[j, h:h + 1, :] = jnp.broadcast_to(d_el, (1, LANES))
            dsns = [dsns[h] * jnp.exp(gc_ref[(j + 1) * c - 1:(j + 1) * c, h:h + 1])
                    + _dot(wqt_ref[j, h], jnp.concatenate([-dvns[h], dobs[h]], axis=0), 1, 0) for h in heads]
        for h in heads:
            dstate[h] = dsns[h]

    rev = lambda n: nch // grp - 1 - n
    rows = pl.BlockSpec((grp * c, w), lambda n: (rev(n), 0))
    per = lambda a, b: pl.BlockSpec((grp, DN_HEADS, a, b), lambda n: (rev(n), 0, 0, 0))
    return pl.pallas_call(
        body, name=name, grid=(nch // grp,),
        in_specs=[rows, per(c, c), per(c, hd), rows, per(hd, hd), per(hd, 2 * c),
                  pl.BlockSpec((grp * c, DN_HEADS), lambda n: (rev(n), 0))],
        out_specs=[rows, rows, pl.BlockSpec((grp, DN_HEADS, LANES), lambda n: (rev(n), 0, 0))],
        out_shape=[jax.ShapeDtypeStruct((t, w), BF16), jax.ShapeDtypeStruct((t, w), F32),
                   jax.ShapeDtypeStruct((nch, DN_HEADS, LANES), F32)],
        scratch_shapes=[pltpu.VMEM((DN_HEADS, hd, hd), F32)], compiler_params=_params(("arbitrary",)),
    )(do, qkt, kr, vn, states, wqt, gc)


def _dn_local_bwd(q, k, v, gc, gcr, beta, invt, u, wq, vn, states, do, dvn, dkr, d_el, *, name):
    t, w = q.shape
    hd = w // DN_HEADS
    c = DN_CHUNK
    nch = t // c
    grp = _tile(nch, DN_LOCAL_CHUNKS, 1)

    def body(q_ref, k_ref, v_ref, gc_ref, gcr_ref, b_ref, invt_ref, u_ref, wq_ref, vn_ref, s_ref, do_ref, dvn_ref,
             dkr_ref, del_ref, dq_ref, dk_ref, dv_ref, dgc_ref, dgr_ref, db_ref):
        tms = [_dn_head_terms(q_ref, k_ref, gc_ref, gcr_ref, b_ref, h, hd, j) for j in range(grp) for h in range(DN_HEADS)]
        dobs = [do_ref[m['rs'], m['sl']].astype(BF16) for m in tms]
        tss = [_dot(jnp.concatenate([dob, -dvn_ref[m['rs'], m['sl']]], axis=0), s_ref[m['j'], m['h']].astype(BF16), 1, 1)
               for m, dob in zip(tms, dobs)]
        d_qks = [_dot(dob, vn_ref[m['rs'], m['sl']], 1, 1) for m, dob in zip(tms, dobs)]
        d_rhss = [_dot(invt_ref[m['j'], m['h']],
                       jnp.concatenate([dvn_ref[m['rs'], m['sl']], ts[c:].astype(BF16)], axis=1), 1, 0)
                  for m, ts in zip(tms, tss)]
        d_as = [-_dot(d_rhs.astype(BF16), jnp.concatenate(
            [u_ref[m['rs'], m['sl']].astype(BF16), wq_ref[m['j'], m['h'], :c, :]], axis=1), 1, 1)
                for m, d_rhs in zip(tms, d_rhss)]
        for m, ts, d_qk, d_rhs, d_a in zip(tms, tss, d_qks, d_rhss, d_as):
            d_qe, j, h, rs = ts[:c], m['j'], m['h'], m['rs']
            sl, qh, kh, bh, e, r, decay = m['sl'], m['q'], m['k'], m['b'], m['e'], m['r'], m['decay']
            vh = v_ref[rs, sl]
            d_ru, d_rw = d_rhs[:, :hd], d_rhs[:, hd:]
            dv_ref[rs, sl] = d_ru * bh
            d_e = jnp.sum(d_rw * m['kb'], axis=1, keepdims=True) + jnp.sum(d_qe * qh, axis=1, keepdims=True)
            d_n = jnp.where(m['row'] > m['col'], d_a, 0.0)
            d_m, d_p = (d_n * decay).astype(BF16), (d_qk * decay).astype(BF16)
            tk = _dot(jnp.concatenate([d_m, d_p], axis=0), kh.astype(BF16), 1, 0)
            d_kb = d_rw * e + tk[:c]
            dq_ref[rs, sl] = tk[c:] + d_qe * e
            dk = _dot(d_m, m['kb'].astype(BF16), 0, 0) + _dot(d_p, qh.astype(BF16), 0, 0)
            dd = (d_n * m['kkt'] + d_qk * m['qkt']) * decay
            d_kr = dkr_ref[rs, sl]
            d_r = jnp.sum(d_kr * kh, axis=1, keepdims=True)
            d_gl = del_ref[j, h:h + 1, 0:1] * m['el'] + jnp.sum(d_r * r, axis=0, keepdims=True)
            last = lax.broadcasted_iota(jnp.int32, (c, 1), 0) == c - 1
            dgc_ref[rs, h:h + 1] = jnp.sum(dd, axis=1, keepdims=True) + d_e * e - d_r * r + jnp.where(last, d_gl, 0.0)
            dgr_ref[j, h:h + 1, :] = -jnp.sum(dd, axis=0, keepdims=True)
            dk_ref[rs, sl] = dk + d_kr * r + d_kb * bh
            db_ref[rs, h:h + 1] = jnp.sum(d_ru * vh, axis=1, keepdims=True) + jnp.sum(d_kb * kh, axis=1, keepdims=True)

    rows = pl.BlockSpec((grp * c, w), lambda n: (n, 0))
    cols = pl.BlockSpec((grp * c, DN_HEADS), lambda n: (n, 0))
    rowg = pl.BlockSpec((grp, DN_HEADS, c), lambda n: (n, 0, 0))
    per = lambda a, b: pl.BlockSpec((grp, DN_HEADS, a, b), lambda n: (n, 0, 0, 0))
    return pl.pallas_call(
        body, name=name, grid=(nch // grp,),
        in_specs=[rows, rows, rows, cols, rowg, cols, per(c, c), rows, per(2 * c, hd), rows, per(hd, hd), rows, rows,
                  rows, pl.BlockSpec((grp, DN_HEADS, LANES), lambda n: (n, 0, 0))],
        out_specs=[rows, rows, rows, cols, rowg, cols],
        out_shape=[jax.ShapeDtypeStruct((t, w), F32)] * 3
        + [jax.ShapeDtypeStruct((t, DN_HEADS), F32), jax.ShapeDtypeStruct((nch, DN_HEADS, c), F32),
           jax.ShapeDtypeStruct((t, DN_HEADS), F32)],
        compiler_params=_params(("parallel",)),
    )(q, k, v, gc, gcr, beta, invt, u, wq, vn, states, do, dvn, dkr, d_el)


def _other_chips():
    x, y = lax.axis_index("x"), lax.axis_index("y")
    return [(1 - x, y), (x, 1 - y), (1 - x, 1 - y)]


def _gather(src, *, name):
    def body(src_ref, out_ref, send_sems, recv_sems, local_sem):
        x, y, c = lax.axis_index("x"), lax.axis_index("y"), lax.axis_index("c")
        me, sibling, chips = (x, y, c), (x, y, 1 - c), _other_chips()
        slot = lambda px, py, pc: out_ref.at[4 * px + 2 * py + pc]

        def copy(k, block, to, own=False):
            return pltpu.make_async_remote_copy(
                src_ref=src_ref if own else slot(*block), dst_ref=slot(*block), send_sem=send_sems.at[k],
                recv_sem=recv_sems.at[k], device_id=to, device_id_type=MESH)

        local = pltpu.make_async_copy(src_ref, slot(*me), local_sem)
        local.start()
        first = [copy(0, me, sibling, own=True)] + [copy(1 + j, me, (*chip, c), own=True) for j, chip in enumerate(chips)]
        for cp in first:
            cp.start()
        passed = [copy(4 + j, (*chip, c), sibling) for j, chip in enumerate(chips)]
        for j, chip in enumerate(chips):
            copy(1 + j, (*chip, c), me).wait_recv()
            passed[j].start()
        copy(0, sibling, me).wait_recv()
        for j, chip in enumerate(chips):
            copy(4 + j, (*chip, 1 - c), me).wait_recv()
        for cp in first + passed:
            cp.wait_send()
        local.wait()

    return pl.pallas_call(
        body, name=name, out_shape=jax.ShapeDtypeStruct((N_DEV,) + src.shape, src.dtype),
        in_specs=[pl.BlockSpec(memory_space=pl.ANY)], out_specs=pl.BlockSpec(memory_space=pl.ANY),
        scratch_shapes=[pltpu.SemaphoreType.DMA((N_DEV - 1,)), pltpu.SemaphoreType.DMA((N_DEV - 1,)),
                        pltpu.SemaphoreType.DMA(())],
    )(src)


def _swap_with_sibling(src, *, name):
    def body(src_ref, out_ref, send_sem, recv_sem):
        x, y, c = lax.axis_index("x"), lax.axis_index("y"), lax.axis_index("c")
        cp = pltpu.make_async_remote_copy(src_ref=src_ref.at[1 - c], dst_ref=out_ref, send_sem=send_sem,
                                          recv_sem=recv_sem, device_id=(x, y, 1 - c), device_id_type=MESH)
        cp.start()
        cp.wait()

    return pl.pallas_call(
        body, name=name, out_shape=jax.ShapeDtypeStruct(src.shape[1:], src.dtype),
        in_specs=[pl.BlockSpec(memory_space=pl.ANY)], out_specs=pl.BlockSpec(memory_space=pl.ANY),
        scratch_shapes=[pltpu.SemaphoreType.DMA(()), pltpu.SemaphoreType.DMA(())],
    )(src)


def _add_own_half(src, got, *, name):
    _, n, r, c = src.shape
    tr = _tile(r, 512, 16)

    def body(s0_ref, s1_ref, got_ref, o_ref):
        own = jnp.where(lax.axis_index("c") == 0, s0_ref[0, 0], s1_ref[0, 0])
        o_ref[0] = (own.astype(F32) + got_ref[0].astype(F32)).astype(BF16)

    half = lambda h: pl.BlockSpec((1, 1, tr, c), lambda j, i: (h, j, i, 0))
    blk = pl.BlockSpec((1, tr, c), lambda j, i: (j, i, 0))
    return pl.pallas_call(
        body, name=name, grid=(n, r // tr), in_specs=[half(0), half(1), blk], out_specs=blk,
        out_shape=jax.ShapeDtypeStruct((n, r, c), BF16), compiler_params=_params(("parallel", "parallel")),
    )(src, src, got)


def _swap_between_chips(src, *, name):
    def body(src_ref, out_ref, send_sems, recv_sems, local_sem):
        x, y, c = lax.axis_index("x"), lax.axis_index("y"), lax.axis_index("c")
        mine = 2 * x + y
        local = pltpu.make_async_copy(src_ref.at[mine], out_ref.at[mine], local_sem)
        local.start()
        copies = [pltpu.make_async_remote_copy(
            src_ref=src_ref.at[2 * px + py], dst_ref=out_ref.at[mine], send_sem=send_sems.at[j],
            recv_sem=recv_sems.at[j], device_id=(px, py, c), device_id_type=MESH) for j, (px, py) in enumerate(_other_chips())]
        for cp in copies:
            cp.start()
        for j, (px, py) in enumerate(_other_chips()):
            pltpu.make_async_remote_copy(
                src_ref=src_ref.at[mine], dst_ref=out_ref.at[2 * px + py], send_sem=send_sems.at[j],
                recv_sem=recv_sems.at[j], device_id=(px, py, c), device_id_type=MESH).wait_recv()
        for cp in copies:
            cp.wait_send()
        local.wait()

    return pl.pallas_call(
        body, name=name, out_shape=jax.ShapeDtypeStruct(src.shape, src.dtype),
        in_specs=[pl.BlockSpec(memory_space=pl.ANY)], out_specs=pl.BlockSpec(memory_space=pl.ANY),
        scratch_shapes=[pltpu.SemaphoreType.DMA((3,)), pltpu.SemaphoreType.DMA((3,)), pltpu.SemaphoreType.DMA(())],
    )(src)


def _sum_adamw(parts, w, m, v, *, name):
    n_parts, r, c = parts.shape
    tr = _tile(r, 256, 8)
    c1, c2 = 1.0 - ADAM_B1 ** ADAM_STEP, 1.0 - ADAM_B2 ** ADAM_STEP

    def body(p_ref, w_ref, m_ref, v_ref, g_ref, d_ref, nm_ref, nv_ref):
        g = p_ref[0].astype(F32)
        for s in range(1, n_parts):
            g = g + p_ref[s].astype(F32)
        nm = ADAM_B1 * m_ref[...] + (1.0 - ADAM_B1) * g
        nv = ADAM_B2 * v_ref[...] + (1.0 - ADAM_B2) * (g * g)
        g_ref[...] = g
        nm_ref[...] = nm
        nv_ref[...] = nv
        d_ref[...] = -ADAM_LR * ((nm / c1) / (jnp.sqrt(nv / c2) + ADAM_EPS) + ADAM_WD * w_ref[...])

    blk = pl.BlockSpec((tr, c), lambda i: (i, 0))
    return pl.pallas_call(
        body, name=name, grid=(r // tr,), in_specs=[pl.BlockSpec((n_parts, tr, c), lambda i: (0, i, 0)), blk, blk, blk],
        out_specs=[blk] * 4, out_shape=[jax.ShapeDtypeStruct((r, c), F32)] * 4, compiler_params=_params(("parallel",)),
    )(parts, w, m, v)


def _pack_rows(n):
    return -(-n // (PACK_COLS * PACK_ROWS)) * PACK_ROWS


def _pack(blocks):
    parts, spans, at = [], [], 0
    for blk, n_lead in blocks:
        lead = blk.shape[:n_lead]
        n = math.prod(blk.shape[n_lead:])
        rows = _pack_rows(n)
        flat = blk.reshape(lead + (n,))
        flat = jnp.pad(flat, [(0, 0)] * n_lead + [(0, rows * PACK_COLS - n)])
        parts.append(flat.reshape(lead + (rows, PACK_COLS)))
        spans.append((at, rows))
        at += rows
    return jnp.concatenate(parts, axis=-2), spans


def _unpack(buf, span, shape):
    at, rows = span
    lead = buf.shape[:-2]
    flat = lax.slice_in_dim(buf, at, at + rows, axis=buf.ndim - 2).reshape(lead + (rows * PACK_COLS,))
    return lax.slice_in_dim(flat, 0, math.prod(shape), axis=len(lead)).reshape(lead + tuple(shape))


def _join_shards(g, axis):
    g = jnp.moveaxis(g, 0, axis)
    return g.reshape(g.shape[:axis] + (g.shape[axis] * g.shape[axis + 1],) + g.shape[axis + 2:])


def _split_shards(full, axis):
    s = full.shape
    g = full.reshape(s[:axis] + (N_DEV, s[axis] // N_DEV) + s[axis + 1:])
    return jnp.moveaxis(g, axis, 0)


def _residual_out(a, w, x, scale, next_gain, name, **tiles):
    out = _mm(a, w, name=name, res=x, scale=scale, norm=next_gain, **tiles)
    return out if next_gain is not None else (out, None)


def _ffn_fwd(x, h, w_in, w_out, tag, next_gain):
    gate, up, a = _mm_swiglu(h, w_in, name=tag + "_in", tm=512, tn=2816)
    xo, hn = _residual_out(a, w_out, x, 0.5, next_gain, tag + "_out", tk=2816)
    return xo, hn, (x, h, gate, up, a)


def _ffn_bwd(saved, g, w_in, w_out, dxo, tag):
    x, h, gate, up, a = saved
    f = w_out.shape[0]
    d_w_out = _mm(a, dxo, name=tag + "_dwout", ta=True, scale=0.5, tm=1408, tk=2048)
    dgate, dup = _mm_dswiglu(dxo, w_out, gate, up, name=tag + "_da", scale=0.5, tm=512, tn=2816)
    d_w_in = jnp.concatenate([_mm(dgate, h, name=tag + "_dwin_g", ta=True, tm=1408, tk=2048),
                              _mm(dup, h, name=tag + "_dwin_u", ta=True, tm=1408, tk=2048)], axis=0)
    dh = _mm(dgate, w_in[:f], name=tag + "_dh_g", tk=2816)
    dx, dg = _mm(dup, w_in[f:], name=tag + "_dh_u", tm=512, tk=2816, res=dh, rms_bwd=(x, g, dxo))
    return dx, dg, d_w_in, d_w_out


def _block_diag(w):
    n, j, k = w.shape
    return (w[:, :, None, :] * jnp.eye(n, dtype=w.dtype)[:, None, :, None]).reshape(n * j, n * k)


def _diag_blocks(dense, n):
    j, k = dense.shape[0] // n, dense.shape[1] // n
    return jnp.stack([dense[i * j:(i + 1) * j, i * k:(i + 1) * k] for i in range(n)], axis=0)


def _attn_lru_fwd(x, h, p, tag, next_gain):
    aw = ATTN_HEADS * 64
    proj = _mm(h, p['ab_w_in'], name=tag + "_in", tb=True, tn=1280)
    qkv = proj[:, :3 * aw].astype(BF16)
    views = {dil: qkv.reshape(qkv.shape[0] // dil, dil * 3 * aw) for _, dil in DILATED_PATTERNS}
    outs, lses = [], []
    for window, dil in DILATED_PATTERNS:
        assert window // dil == ATTN_BLOCK
        o, l = _dattn_fwd(views[dil], dil, name=f"{tag}_attn{dil}")
        outs.append(o)
        lses.append(l)
    wa, wx = _block_diag(p['lru_w_a']).astype(BF16), _block_diag(p['lru_w_x']).astype(BF16)
    sp, sp_vjp = jax.vjp(lambda lam: LRU_C * jax.nn.softplus(-lam), p['lru_lambda'])
    a, b, xc = _lru_gates_fwd(proj, 3, p['lru_conv_w'], p['lru_conv_b'], wa, p['lru_b_a'], wx, p['lru_b_x'], sp,
                              name=tag + "_gates")
    hs = _scan(a, b, name=tag + "_scan")
    cat, attn, lse_all = _mix_join_fwd(outs, lses, hs, proj, 4, name=tag + "_join")
    xo, hn = _residual_out(cat, p['ab_w_out'], x, 1.0, next_gain, tag + "_out")
    return xo, hn, (x, h, proj, views, attn, lse_all, a, hs, xc, wa, wx, sp, sp_vjp, cat)


def _attn_lru_bwd(saved, p, dxo, tag):
    x, h, proj, views, attn, lse_all, a, hs, xc, wa, wx, sp, sp_vjp, cat = saved
    aw = attn.shape[1]
    g = {'ab_w_out': _mm(cat, dxo, name=tag + "_dwout", ta=True)}
    dcat = _mm(dxo, p['ab_w_out'], name=tag + "_dcat", tb=True)
    stats, dhs, dgr = _mix_join_bwd(dcat, attn, lse_all, hs, proj, 4, name=tag + "_djoin")
    a_next = jnp.concatenate([a[1:], jnp.zeros_like(a[:1])], axis=0)
    dtot = _scan(a_next, dhs, name=tag + "_dscan", reverse=True)
    h_prev = jnp.concatenate([jnp.zeros_like(hs[:1]), hs[:-1]], axis=0)
    dxc, dwa, dwx, vecs = _lru_gates_bwd(xc, dtot, h_prev, wa, p['lru_b_a'], wx, p['lru_b_x'], sp, name=tag + "_dgates")
    dxr, g['lru_conv_w'] = _conv_bwd(dxc, proj, 3, p['lru_conv_w'], name=tag + "_dconv")
    g['lru_b_a'], g['lru_b_x'], g['lru_conv_b'] = vecs[0], vecs[1], vecs[3]
    g['lru_lambda'], = sp_vjp(vecs[2])
    g['lru_w_a'], g['lru_w_x'] = _diag_blocks(dwa, LRU_BLOCKS), _diag_blocks(dwx, LRU_BLOCKS)
    dattn = dcat[:, :aw]
    dqkv = sum(_dattn_bwd(views[dil], dattn, stats, dil, name=f"{tag}_dattn{dil}").astype(F32)
               for _, dil in DILATED_PATTERNS)
    dproj = jnp.concatenate([dqkv.astype(BF16), dxr, dgr], axis=-1)
    g['ab_w_in'] = _mm(dproj, h, name=tag + "_dwin", ta=True, tm=1280)
    dx, g['mix_norm'] = _mm(dproj, p['ab_w_in'], name=tag + "_dh", tm=512, tk=2560, rms_bwd=(x, p['mix_norm'], dxo))
    return dx, g


def _dn_decay(a, b, a_log, dt_bias):
    t = a.shape[0]
    g = -jnp.exp(a_log) * jax.nn.softplus(a + dt_bias)
    gc = jnp.cumsum(g.reshape(t // DN_CHUNK, DN_CHUNK, DN_HEADS), axis=1)
    return gc.reshape(t, DN_HEADS), jnp.swapaxes(gc, 1, 2), jax.nn.sigmoid(b)


def _dn_in_width(w):
    return -(-(4 * w + 2 * DN_HEADS) // LANES) * LANES


def _deltanet_fwd(x, h, p, tag, next_gain):
    w = p['dn_w_out'].shape[0]
    proj = _mm(h, p['dn_w_in'], name=tag + "_in", tb=True, tn=1408)
    q, k, v, c = _dn_prep_fwd(proj, p['dn_conv_w'], name=tag + "_prep")
    a, b = proj[:, 4 * w:4 * w + DN_HEADS], proj[:, 4 * w + DN_HEADS:4 * w + 2 * DN_HEADS]
    (gc, gcr, beta), decay_vjp = jax.vjp(_dn_decay, a, b, p['dn_a_log'], p['dn_dt_bias'])
    prep = (q, k, v, gc, gcr, beta)
    u, wq, wqt, kr, krt, qk, qkt, invt = _dn_local_fwd(*prep, name=tag + "_local")
    o, vn, states = _dn_state_fwd(u, wq, krt, qk, gc, name=tag + "_state")
    og = _dn_gate_fwd(o, proj, 3, p['dn_o_norm'], name=tag + "_gate")
    xo, hn = _residual_out(og, p['dn_w_out'], x, 1.0, next_gain, tag + "_out")
    return xo, hn, (x, h, proj, c, o, prep, (u, wq, wqt, kr, qkt, invt, vn, states), decay_vjp, og)


def _deltanet_bwd(saved, p, dxo, tag):
    x, h, proj, c, o, prep, (u, wq, wqt, kr, qkt, invt, vn, states), decay_vjp, og = saved
    g = {'dn_w_out': _mm(og, dxo, name=tag + "_dwout", ta=True)}
    dog = _mm(dxo, p['dn_w_out'], name=tag + "_dog", tb=True)
    do, dz, g['dn_o_norm'] = _dn_gate_bwd(o, proj, 3, p['dn_o_norm'], dog, name=tag + "_dgate")
    dvn, dkr, d_el = _dn_state_bwd(do, qkt, kr, vn, states, wqt, prep[3], name=tag + "_dstate")
    dq, dk, dv, dgc, dgr, dbeta = _dn_local_bwd(*prep, invt, u, wq, vn, states, do, dvn, dkr, d_el, name=tag + "_dlocal")
    dc = _dn_prep_bwd(c, dq, dk, dv, name=tag + "_dprep")
    dqkv, g['dn_conv_w'] = _conv_bwd(dc, proj, 0, p['dn_conv_w'], name=tag + "_dconv")
    da, db, g['dn_a_log'], g['dn_dt_bias'] = decay_vjp((dgc, dgr, dbeta))
    t = x.shape[0]
    pad = jnp.zeros((t, p['dn_w_in'].shape[0] - dqkv.shape[1] - dz.shape[1] - 2 * DN_HEADS), BF16)
    groups = [("qkv", dqkv), ("z", dz), ("ab", jnp.concatenate([da.astype(BF16), db.astype(BF16), pad], axis=-1))]
    parts, dh, lo = [], None, 0
    for label, piece in groups:
        hi = lo + piece.shape[1]
        parts.append(_mm(piece, h, name=f"{tag}_dwin_{label}", ta=True, tm=1536, tk=2048))
        last = dict(tm=512, rms_bwd=(x, p['mix_norm'], dxo)) if label == groups[-1][0] else {}
        dh = _mm(piece, p['dn_w_in'][lo:hi], name=f"{tag}_dh_{label}", tk=1536, res=dh, **last)
        lo = hi
    g['dn_w_in'] = jnp.concatenate(parts, axis=0)
    dx, g['mix_norm'] = dh
    return dx, g


def _xattn_block_fwd(x, h, mem, p, tag, next_gain):
    w = x.shape[1]
    mh = _rms_fwd(mem, p['xa_mem_norm'], name=tag + "_mnorm", out_dtype=BF16)
    q = _mm(h, p['xa_wq'], name=tag + "_q")
    kv = _mm(mh, p['xa_wkv'], name=tag + "_kv", tb=True)
    k, v = kv[:, :w], kv[:, w:]
    o = _xattn_fwd(q, k, v, name=tag + "_attn").astype(BF16)
    xo, hn = _residual_out(o, p['xa_wo'], x, 1.0, next_gain, tag + "_out")
    return xo, hn, (x, h, mh, q, k, v, o)


def _xattn_block_bwd(saved, mem, p, dxo, tag):
    x, h, mh, q, k, v, o = saved
    g = {'xa_wo': _mm(o, dxo, name=tag + "_dwo", ta=True)}
    do = _mm(dxo, p['xa_wo'], name=tag + "_do", tb=True)
    dq, dk, dv = _xattn_bwd(q, k, v, do, name=tag + "_dattn")
    dq = dq.astype(BF16)
    dkv = jnp.concatenate([dk, dv], axis=-1).astype(BF16)
    g['xa_wq'] = _mm(h, dq, name=tag + "_dwq", ta=True)
    g['xa_wkv'] = _mm(dkv, mh, name=tag + "_dwkv", ta=True)
    dmh = _mm(dkv, p['xa_wkv'], name=tag + "_dmh")
    _, g['xa_mem_norm'] = _rms_bwd(mem, p['xa_mem_norm'], dmh, None, name=tag + "_dmnorm")
    dx, g['xa_norm'] = _mm(dq, p['xa_wq'], name=tag + "_dh", tb=True, tm=512, rms_bwd=(x, p['xa_norm'], dxo))
    return dx, g


def _layer_params(full, layer):
    p = {n: full[n][layer] for n in ('ffn1_norm', 'ffn1_w_in', 'ffn1_w_out', 'mix_norm', 'xa_norm', 'xa_mem_norm',
                                     'xa_wq', 'xa_wkv', 'xa_wo', 'ffn2_norm', 'ffn2_w_in', 'ffn2_w_out')}
    mixer = ('ab_w_in', 'lru_conv_w', 'lru_conv_b', 'lru_w_a', 'lru_b_a', 'lru_w_x', 'lru_b_x', 'lru_lambda', 'ab_w_out') \
        if layer % 2 == 0 else ('dn_w_in', 'dn_conv_w', 'dn_a_log', 'dn_dt_bias', 'dn_o_norm', 'dn_w_out')
    p.update({n: full[n][layer // 2] for n in mixer})
    return p


def _step(x, mem, target, full, depth):
    saved = []
    params = [_layer_params(full, layer) for layer in range(depth)]
    h = _rms_fwd(x, params[0]['ffn1_norm'], name="l0_ffn1_norm", out_dtype=BF16)
    for layer, p in enumerate(params):
        tag = f"l{layer}"
        after = params[layer + 1]['ffn1_norm'] if layer + 1 < depth else None
        x, h, s1 = _ffn_fwd(x, h, p['ffn1_w_in'], p['ffn1_w_out'], tag + "_ffn1", p['mix_norm'])
        x, h, s2 = (_attn_lru_fwd if layer % 2 == 0 else _deltanet_fwd)(x, h, p, tag + "_mix", p['xa_norm'])
        x, h, s3 = _xattn_block_fwd(x, h, mem, p, tag + "_xa", p['ffn2_norm'])
        x, h, s4 = _ffn_fwd(x, h, p['ffn2_w_in'], p['ffn2_w_out'], tag + "_ffn2", after)
        saved.append((p, s1, s2, s3, s4))
    sq, dx, d_final = _final_loss(x, full['final_norm'], target, name="final_loss")
    per_layer = []
    for layer in reversed(range(depth)):
        p, s1, s2, s3, s4 = saved[layer]
        tag = f"l{layer}"
        g = {}
        dx, g['ffn2_norm'], g['ffn2_w_in'], g['ffn2_w_out'] = _ffn_bwd(s4, p['ffn2_norm'], p['ffn2_w_in'], p['ffn2_w_out'], dx, tag + "_ffn2")
        dx, gx = _xattn_block_bwd(s3, mem, p, dx, tag + "_xa")
        dx, gm = (_attn_lru_bwd if layer % 2 == 0 else _deltanet_bwd)(s2, p, dx, tag + "_mix")
        dx, g['ffn1_norm'], g['ffn1_w_in'], g['ffn1_w_out'] = _ffn_bwd(s1, p['ffn1_norm'], p['ffn1_w_in'], p['ffn1_w_out'], dx, tag + "_ffn1")
        g.update(gx)
        g.update(gm)
        per_layer.insert(0, g)
    grads = {n: [g[n] for g in per_layer if n in g] for n in WEIGHTS[:-1]}
    grads['final_norm'] = d_final
    return sq, dx, grads


def kernel(x, mem, ffn1_norm, ffn1_w_in, ffn1_w_out, mix_norm, xa_norm, xa_mem_norm, xa_wq, xa_wkv, xa_wo, ffn2_norm, ffn2_w_in, ffn2_w_out, ab_w_in, lru_conv_w, lru_conv_b, lru_w_a, lru_b_a, lru_w_x, lru_b_x, lru_lambda, ab_w_out, dn_w_in, dn_conv_w, dn_a_log, dn_dt_bias, dn_o_norm, dn_w_out, final_norm, loss_target, m_ffn1_norm, m_ffn1_w_in, m_ffn1_w_out, m_mix_norm, m_xa_norm, m_xa_mem_norm, m_xa_wq, m_xa_wkv, m_xa_wo, m_ffn2_norm, m_ffn2_w_in, m_ffn2_w_out, m_ab_w_in, m_lru_conv_w, m_lru_conv_b, m_lru_w_a, m_lru_b_a, m_lru_w_x, m_lru_b_x, m_lru_lambda, m_ab_w_out, m_dn_w_in, m_dn_conv_w, m_dn_a_log, m_dn_dt_bias, m_dn_o_norm, m_dn_w_out, m_final_norm, v_ffn1_norm, v_ffn1_w_in, v_ffn1_w_out, v_mix_norm, v_xa_norm, v_xa_mem_norm, v_xa_wq, v_xa_wkv, v_xa_wo, v_ffn2_norm, v_ffn2_w_in, v_ffn2_w_out, v_ab_w_in, v_lru_conv_w, v_lru_conv_b, v_lru_w_a, v_lru_b_a, v_lru_w_x, v_lru_b_x, v_lru_lambda, v_ab_w_out, v_dn_w_in, v_dn_conv_w, v_dn_a_log, v_dn_dt_bias, v_dn_o_norm, v_dn_w_out, v_final_norm):
    args = dict(locals())
    flip = lambda n, a: jnp.swapaxes(a, 1, 2) if n in TRANSPOSED else a
    local = {n: flip(n, args[n]) for n in WEIGHTS}
    depth = ffn1_norm.shape[0]
    matrices = [n for n in WEIGHTS if n in SHARD_AXIS and n not in GATHER_F32]

    def entries(get):
        return [(a, n, l) for n in WEIGHTS for l, a in
                (enumerate(get(n)) if n in matrices else [(None, get(n))])]

    mats = [e for e in entries(lambda n: local[n]) if e[1] in matrices]
    send16, spans16 = _pack([(a.astype(BF16), 0) for a, _, _ in mats])
    send32, spans32 = _pack([(local[n], 0) for n in GATHER_F32])
    got16 = _gather(send16, name="gather_matrices")
    got32 = _gather(send32, name="gather_filters")
    full = {n: ([None] * local[n].shape[0] if n in matrices else local[n]) for n in WEIGHTS}
    for (a, n, l), span in zip(mats, spans16):
        full[n][l] = _unpack(got16, span, a.shape).reshape(N_DEV * a.shape[0], a.shape[1])
    for n, span in zip(GATHER_F32, spans32):
        full[n] = _join_shards(_unpack(got32, span, local[n].shape), SHARD_AXIS[n])
    dn_rows = full['dn_w_in'][0].shape[0]
    full['dn_w_in'] = [jnp.pad(w, ((0, _dn_in_width(w.shape[1]) - dn_rows), (0, 0))) for w in full['dn_w_in']]

    sq, dx, grads = _step(x[0], mem[0], loss_target[0], full, depth)
    grads['dn_w_in'] = [g[:dn_rows] for g in grads['dn_w_in']]
    loss = lax.psum(0.5 * jnp.sum(sq) / x.shape[2], ("x", "y", "c"))

    by_core = lambda z: jnp.swapaxes(z.reshape((N_DEV // 2, 2) + z.shape[1:]), 0, 1)

    def contribution(n):
        if n in matrices:
            return [by_core(g.reshape((N_DEV, g.shape[0] // N_DEV, g.shape[1]))) for g in grads[n]]
        g = grads[n] if n == 'final_norm' else jnp.stack(grads[n], axis=0)
        return by_core(_split_shards(g, SHARD_AXIS[n]) if n in SHARD_AXIS else jnp.broadcast_to(g, (N_DEV,) + g.shape))

    send, spans = _pack([(a.astype(BF16), 2) for a, _, _ in entries(contribution)])
    got = _swap_with_sibling(send, name="grads_to_sibling")
    chip_sum = _add_own_half(send, got, name="grads_chip_sum")
    parts = _swap_between_chips(chip_sum, name="grads_between_chips")
    state = [_pack([(a, 0) for a, _, _ in entries(lambda n: flip(n, args[pre + n]))])[0] for pre in ("", "m_", "v_")]
    outs = _sum_adamw(parts, *state, name="sum_adamw")
    result = []
    for o in outs:
        got_rows = {}
        for (a, n, l), span in zip(entries(lambda n: local[n]), spans):
            got_rows.setdefault(n, []).append(_unpack(o, span, a.shape))
        result += [flip(n, jnp.stack(got_rows[n], axis=0) if n in matrices else got_rows[n][0]) for n in WEIGHTS]
    return (loss, dx[None], *result)
```

```python
import math

import jax
import jax.numpy as jnp
from jax import lax
from jax.experimental import pallas as pl
from jax.experimental.pallas import tpu as pltpu

F32, BF16 = jnp.float32, jnp.bfloat16
MESH = pl.DeviceIdType.MESH
N_DEV = 8
V7X_VMEM_LIMIT = 56 << 20
LANES = 128
PACK_COLS = 1024
PACK_ROWS = 16
NEG = -1e30

NORM_EPS = 1e-6
CONV_K = 4
ATTN_HEADS = 8
DILATED_PATTERNS = ((128, 1), (512, 4), (2048, 16))
ATTN_BLOCK = 128
LRU_BLOCKS = 8
LRU_C = 8.0
DN_HEADS = 8
DN_CHUNK = 64
DN_STEP_CHUNKS = 4
DN_LOCAL_CHUNKS = 4
XA_HEADS = 4
ADAM_LR, ADAM_B1, ADAM_B2, ADAM_EPS, ADAM_WD, ADAM_STEP = 0.001, 0.9, 0.999, 1e-08, 0.01, 10

WEIGHTS = ['ffn1_norm', 'ffn1_w_in', 'ffn1_w_out', 'mix_norm', 'xa_norm', 'xa_mem_norm', 'xa_wq', 'xa_wkv', 'xa_wo',
           'ffn2_norm', 'ffn2_w_in', 'ffn2_w_out', 'ab_w_in', 'lru_conv_w', 'lru_conv_b', 'lru_w_a', 'lru_b_a',
           'lru_w_x', 'lru_b_x', 'lru_lambda', 'ab_w_out', 'dn_w_in', 'dn_conv_w', 'dn_a_log', 'dn_dt_bias',
           'dn_o_norm', 'dn_w_out', 'final_norm']
SHARD_AXIS = {'ffn1_w_in': 2, 'ffn1_w_out': 1, 'xa_wq': 1, 'xa_wkv': 2, 'xa_wo': 1, 'ffn2_w_in': 2, 'ffn2_w_out': 1,
              'ab_w_in': 2, 'lru_conv_w': 2, 'ab_w_out': 1, 'dn_w_in': 2, 'dn_conv_w': 2, 'dn_w_out': 1}
GATHER_F32 = ('lru_conv_w', 'dn_conv_w')
TRANSPOSED = ('ffn1_w_in', 'xa_wkv', 'ffn2_w_in', 'ab_w_in', 'dn_w_in')


def _params(sem=None):
    return pltpu.CompilerParams(dimension_semantics=sem, vmem_limit_bytes=V7X_VMEM_LIMIT)


def _tile(n, pref, mult):
    best = None
    t = mult
    while t <= min(n, pref):
        if n % t == 0:
            best = t
        t += mult
    return n if best is None else best


def _dot(a, b, ca, cb, prec=None):
    return lax.dot_general(a, b, (((ca,), (cb,)), ((), ())), preferred_element_type=F32, precision=prec)


def _mm(a, b, *, name, ta=False, tb=False, out_dtype=F32, res=None, scale=1.0, tm=1024, tn=1024, tk=1024,
        norm=None, rms_bwd=None):
    m, kdim = (a.shape[1], a.shape[0]) if ta else a.shape
    n = b.shape[0] if tb else b.shape[1]
    assert (b.shape[1] if tb else b.shape[0]) == kdim
    tm = _tile(m, tm, LANES if ta else 16)
    tn = _tile(n, tn, LANES)
    tk = _tile(kdim, tk, LANES)
    nk = kdim // tk
    a_spec = pl.BlockSpec((tk, tm), lambda i, j, k: (k, i)) if ta else pl.BlockSpec((tm, tk), lambda i, j, k: (i, k))
    b_spec = pl.BlockSpec((tn, tk), lambda i, j, k: (j, k)) if tb else pl.BlockSpec((tk, tn), lambda i, j, k: (k, j))
    o_spec = pl.BlockSpec((tm, tn), lambda i, j, k: (i, j))
    vec = pl.BlockSpec((1, tn), lambda i, j, k: (0, j))
    ca, cb = (0 if ta else 1), (1 if tb else 0)
    has_res = res is not None
    assert (norm is None and rms_bwd is None) or tn == n

    ins, specs = [a, b], [a_spec, b_spec]
    if has_res:
        ins.append(res)
        specs.append(o_spec)
    out_specs, out_shape = [o_spec], [jax.ShapeDtypeStruct((m, n), out_dtype)]
    if norm is not None:
        ins.append(norm.reshape(1, n))
        specs.append(vec)
        out_specs.append(o_spec)
        out_shape.append(jax.ShapeDtypeStruct((m, n), BF16))
    if rms_bwd is not None:
        ins += [rms_bwd[0], rms_bwd[1].reshape(1, n), rms_bwd[2]]
        specs += [o_spec, vec, o_spec]
        out_specs.append(vec)
        out_shape.append(jax.ShapeDtypeStruct((1, n), F32))
    n_in, n_out = len(ins), len(out_specs)

    def finish(acc, extra, outs, first_rows):
        y = acc if scale == 1.0 else acc * scale
        if has_res:
            y = y + extra[0][...]
        tail = extra[has_res:]
        if norm is not None:
            rs = lax.rsqrt(jnp.mean(y * y, axis=-1, keepdims=True) + NORM_EPS)
            outs[1][...] = (y * rs * tail[0][...]).astype(BF16)
        if rms_bwd is not None:
            x_ref, g_ref, dres_ref = tail
            xv = x_ref[...]
            rs = lax.rsqrt(jnp.mean(xv * xv, axis=-1, keepdims=True) + NORM_EPS)
            xh = xv * rs
            dgh = y * g_ref[...]
            part = jnp.sum(y * xh, axis=0, keepdims=True)
            y = dres_ref[...] + rs * (dgh - xh * jnp.mean(dgh * xh, axis=-1, keepdims=True))

            @pl.when(first_rows)
            def _():
                outs[1][...] = part

            @pl.when(jnp.logical_not(first_rows))
            def _():
                outs[1][...] += part

        outs[0][...] = y.astype(out_dtype)

    def body(*refs):
        a_ref, b_ref = refs[0], refs[1]
        extra, outs = refs[2:n_in], refs[n_in:n_in + n_out]
        p = _dot(a_ref[...].astype(BF16), b_ref[...].astype(BF16), ca, cb)
        first_rows = pl.program_id(0) == 0
        if nk == 1:
            finish(p, extra, outs, first_rows)
            return
        acc = refs[n_in + n_out]
        k = pl.program_id(2)

        @pl.when(k == 0)
        def _():
            acc[...] = p

        @pl.when(k > 0)
        def _():
            acc[...] += p

        @pl.when(k == nk - 1)
        def _():
            finish(acc[...], extra, outs, first_rows)

    out = pl.pallas_call(
        body, name=name, grid=(m // tm, n // tn, nk), in_specs=specs, out_specs=out_specs, out_shape=out_shape,
        scratch_shapes=[] if nk == 1 else [pltpu.VMEM((tm, tn), F32)],
        compiler_params=_params(("parallel" if rms_bwd is None else "arbitrary", "parallel", "arbitrary")),
    )(*ins)
    if rms_bwd is not None:
        return out[0], out[1].reshape(n)
    return out[0] if n_out == 1 else tuple(out)


def _rms_fwd(x, g, *, name, out_dtype):
    r, d = x.shape
    tr = _tile(r, 512, 16)

    def body(x_ref, g_ref, o_ref):
        xv = x_ref[...]
        rs = lax.rsqrt(jnp.mean(xv * xv, axis=-1, keepdims=True) + NORM_EPS)
        o_ref[...] = (xv * rs * g_ref[...]).astype(out_dtype)

    return pl.pallas_call(
        body, name=name, grid=(r // tr,),
        in_specs=[pl.BlockSpec((tr, d), lambda i: (i, 0)), pl.BlockSpec((1, d), lambda i: (0, 0))],
        out_specs=pl.BlockSpec((tr, d), lambda i: (i, 0)),
        out_shape=jax.ShapeDtypeStruct((r, d), out_dtype), compiler_params=_params(("parallel",)),
    )(x, g.reshape(1, d))


def _rms_bwd(x, g, dh, dres, *, name):
    r, d = x.shape
    tr = _tile(r, 512, 8)
    has_res = dres is not None

    def body(*refs):
        x_ref, g_ref, dh_ref = refs[:3]
        r_ref = refs[3] if has_res else None
        dx_ref, dg_ref = refs[3 + has_res], refs[4 + has_res]
        xv = x_ref[...]
        rs = lax.rsqrt(jnp.mean(xv * xv, axis=-1, keepdims=True) + NORM_EPS)
        xh = xv * rs
        dhv = dh_ref[...]
        dgh = dhv * g_ref[...]
        dx = rs * (dgh - xh * jnp.mean(dgh * xh, axis=-1, keepdims=True))
        if has_res:
            dx = dx + r_ref[...]
        dx_ref[...] = dx
        part = jnp.sum(dhv * xh, axis=0, keepdims=True)

        @pl.when(pl.program_id(0) == 0)
        def _():
            dg_ref[...] = part

        @pl.when(pl.program_id(0) > 0)
        def _():
            dg_ref[...] += part

    row = pl.BlockSpec((tr, d), lambda i: (i, 0))
    vec = pl.BlockSpec((1, d), lambda i: (0, 0))
    ins, specs = [x, g.reshape(1, d), dh], [row, vec, row]
    if has_res:
        ins.append(dres)
        specs.append(row)
    dx, dg = pl.pallas_call(
        body, name=name, grid=(r // tr,), in_specs=specs, out_specs=[row, vec],
        out_shape=[jax.ShapeDtypeStruct((r, d), F32), jax.ShapeDtypeStruct((1, d), F32)],
        compiler_params=_params(("arbitrary",)),
    )(*ins)
    return dx, dg.reshape(d)


def _mm_swiglu(h, w_in, *, name, tm=1024, tn=256):
    t, d = h.shape
    f = w_in.shape[0] // 2
    tm, tn = _tile(t, tm, 16), _tile(f, tn, LANES)
    nj = f // tn

    def body(h_ref, wg_ref, wu_ref, g_ref, u_ref, a_ref):
        hv = h_ref[...]
        gate, up = _dot(hv, wg_ref[...], 1, 1), _dot(hv, wu_ref[...], 1, 1)
        g_ref[...] = gate.astype(BF16)
        u_ref[...] = up.astype(BF16)
        a_ref[...] = (gate * jax.nn.sigmoid(gate) * up).astype(BF16)

    out = pl.BlockSpec((tm, tn), lambda j, i: (i, j))
    return pl.pallas_call(
        body, name=name, grid=(nj, t // tm),
        in_specs=[pl.BlockSpec((tm, d), lambda j, i: (i, 0)), pl.BlockSpec((tn, d), lambda j, i: (j, 0)),
                  pl.BlockSpec((tn, d), lambda j, i: (j + nj, 0))],
        out_specs=[out, out, out], out_shape=[jax.ShapeDtypeStruct((t, f), BF16)] * 3,
        compiler_params=_params(("parallel", "parallel")),
    )(h, w_in, w_in)


def _mm_dswiglu(dy, w_out, gate, up, *, name, scale, tm=512):
    t, d = dy.shape
    f = w_out.shape[0]
    tm = _tile(t, tm, 16)

    def body(dy_ref, w_ref, g_ref, u_ref, o_ref):
        da = _dot(dy_ref[...].astype(BF16), w_ref[...], 1, 1) * scale
        gv = g_ref[...].astype(F32)
        s = jax.nn.sigmoid(gv)
        o_ref[:, :f] = (da * u_ref[...].astype(F32) * (s * (1.0 + gv * (1.0 - s)))).astype(BF16)
        o_ref[:, f:] = (da * gv * s).astype(BF16)

    blk = pl.BlockSpec((tm, f), lambda i: (i, 0))
    return pl.pallas_call(
        body, name=name, grid=(t // tm,),
        in_specs=[pl.BlockSpec((tm, d), lambda i: (i, 0)), pl.BlockSpec((f, d), lambda i: (0, 0)), blk, blk],
        out_specs=pl.BlockSpec((tm, 2 * f), lambda i: (i, 0)), out_shape=jax.ShapeDtypeStruct((t, 2 * f), BF16),
        compiler_params=_params(("parallel",)),
    )(dy, w_out, gate, up)


def _final_loss(x, g, target, *, name):
    r, d = x.shape
    tr = _tile(r, 512, 8)

    def body(x_ref, g_ref, t_ref, sq_ref, dx_ref, dg_ref):
        xv = x_ref[...]
        gv = g_ref[...]
        rs = lax.rsqrt(jnp.mean(xv * xv, axis=-1, keepdims=True) + NORM_EPS)
        xh = xv * rs
        err = xh * gv - t_ref[...]
        dy = err * (1.0 / d)
        dgh = dy * gv
        dx_ref[...] = rs * (dgh - xh * jnp.mean(dgh * xh, axis=-1, keepdims=True))
        sq = jnp.sum(err * err, axis=0, keepdims=True)
        part = jnp.sum(dy * xh, axis=0, keepdims=True)

        @pl.when(pl.program_id(0) == 0)
        def _():
            sq_ref[...] = sq
            dg_ref[...] = part

        @pl.when(pl.program_id(0) > 0)
        def _():
            sq_ref[...] += sq
            dg_ref[...] += part

    row = pl.BlockSpec((tr, d), lambda i: (i, 0))
    vec = pl.BlockSpec((1, d), lambda i: (0, 0))
    sq, dx, dg = pl.pallas_call(
        body, name=name, grid=(r // tr,), in_specs=[row, vec, row], out_specs=[vec, row, vec],
        out_shape=[jax.ShapeDtypeStruct((1, d), F32), jax.ShapeDtypeStruct((r, d), F32),
                   jax.ShapeDtypeStruct((1, d), F32)],
        compiler_params=_params(("arbitrary",)),
    )(x, g.reshape(1, d), target)
    return sq.reshape(d), dx, dg.reshape(d)


def _scan(a, b, *, name, reverse=False):
    t, w = a.shape
    tb = _tile(t, 1024, 8)
    nblk, ngrp = t // tb, tb // 8

    def body(a_ref, b_ref, h_ref, carry):
        @pl.when(pl.program_id(0) == 0)
        def _():
            carry[...] = jnp.zeros_like(carry)

        row = lax.broadcasted_iota(jnp.int32, (8, w), 0)

        def group(i, c):
            r0 = pl.multiple_of((ngrp - 1 - i if reverse else i) * 8, 8)
            av, bv = a_ref[pl.ds(r0, 8), :], b_ref[pl.ds(r0, 8), :]
            for s in (1, 2, 4):
                keep = row < 8 - s if reverse else row >= s
                shift = 8 - s if reverse else s
                bv = jnp.where(keep, bv + av * pltpu.roll(bv, shift, axis=0), bv)
                av = jnp.where(keep, av * pltpu.roll(av, shift, axis=0), av)
            hv = bv + av * c
            h_ref[pl.ds(r0, 8), :] = hv
            return hv[0:1, :] if reverse else hv[7:8, :]

        carry[0:1, :] = lax.fori_loop(0, ngrp, group, carry[0:1, :])

    blk = pl.BlockSpec((tb, w), lambda i: (nblk - 1 - i if reverse else i, 0))
    return pl.pallas_call(
        body, name=name, grid=(t // tb,), in_specs=[blk, blk], out_specs=blk,
        out_shape=jax.ShapeDtypeStruct((t, w), F32), scratch_shapes=[pltpu.VMEM((8, w), F32)],
        compiler_params=_params(("arbitrary",)),
    )(a, b)


def _row_specs(t, tb, width, col):
    per = tb // 8
    main = pl.BlockSpec((tb, width), lambda i: (i, col))
    before = pl.BlockSpec((8, width), lambda i: (jnp.maximum(i * per - 1, 0), col))
    after = pl.BlockSpec((8, width), lambda i: (jnp.minimum((i + 1) * per, t // 8 - 1), col))
    return main, before, after


def _with_rows_before(x_ref, before_ref):
    return jnp.concatenate([jnp.where(pl.program_id(0) > 0, before_ref[...], 0.0), x_ref[...]], axis=0)


def _tap(xe, s):
    return xe[8:] if s == 0 else pltpu.roll(xe, s, axis=0)[8:]


def _conv_rows(xe, w_ref):
    return sum(_tap(xe, CONV_K - 1 - k) * w_ref[k:k + 1, :] for k in range(CONV_K))


def _conv_bwd(dy, x, col, w, *, name):
    t, width = dy.shape
    tb = _tile(t, 256, 8)
    nblk = t // tb

    def body(dy_ref, dy_after_ref, x_ref, x_before_ref, w_ref, dx_ref, dw_ref):
        i = pl.program_id(0)
        dyv = dy_ref[...]
        dye = jnp.concatenate([dyv, jnp.where(i < nblk - 1, dy_after_ref[...], 0.0)], axis=0)
        dx = dyv * w_ref[CONV_K - 1:CONV_K, :]
        for s in range(1, CONV_K):
            dx = dx + pltpu.roll(dye, tb + 8 - s, axis=0)[:tb] * w_ref[CONV_K - 1 - s:CONV_K - s, :]
        dx_ref[...] = dx.astype(BF16)
        xe = _with_rows_before(x_ref, x_before_ref)

        @pl.when(i == 0)
        def _():
            dw_ref[...] = jnp.zeros_like(dw_ref)

        for k in range(CONV_K):
            dw_ref[k:k + 1, :] += jnp.sum(dyv * _tap(xe, CONV_K - 1 - k), axis=0, keepdims=True)

    main, _, after = _row_specs(t, tb, width, 0)
    xmain, xbefore, _ = _row_specs(t, tb, width, col)
    dx, dw = pl.pallas_call(
        body, name=name, grid=(nblk,),
        in_specs=[main, after, xmain, xbefore, pl.BlockSpec((CONV_K, width), lambda i: (0, 0))],
        out_specs=[main, pl.BlockSpec((8, width), lambda i: (0, 0))],
        out_shape=[jax.ShapeDtypeStruct((t, width), BF16), jax.ShapeDtypeStruct((8, width), F32)],
        compiler_params=_params(("arbitrary",)),
    )(dy, dy, x, x, w)
    return dx, dw[:CONV_K]


def _expm1(x):
    small = x * (1.0 + x * (0.5 + x * (1.0 / 6.0 + x * (1.0 / 24.0 + x * (1.0 / 120.0 + x * (1.0 / 720.0))))))
    return jnp.where(jnp.abs(x) < 0.1, small, jnp.exp(x) - 1.0)


def _lru_gate_terms(xc, wa_ref, ba_ref, wx_ref, bx_ref, sp_ref):
    xb = xc.astype(BF16)
    r = jax.nn.sigmoid(_dot(xb, wa_ref[...], 1, 0) + ba_ref[...])
    i = jax.nn.sigmoid(_dot(xb, wx_ref[...], 1, 0) + bx_ref[...])
    log_a = -r * sp_ref[...]
    return r, i, jnp.exp(log_a), jnp.sqrt(-_expm1(2.0 * log_a))


def _lru_gates_fwd(proj, col, conv_w, conv_b, wa, ba, wx, bx, sp, *, name):
    t = proj.shape[0]
    width = conv_w.shape[1]
    tb = _tile(t, 512, 8)

    def body(x_ref, x_before_ref, cw_ref, cb_ref, wa_ref, ba_ref, wx_ref, bx_ref, sp_ref, a_ref, b_ref, xc_ref):
        xc = _conv_rows(_with_rows_before(x_ref, x_before_ref), cw_ref) + cb_ref[...]
        r, i, a, mult = _lru_gate_terms(xc, wa_ref, ba_ref, wx_ref, bx_ref, sp_ref)
        a_ref[...] = a
        b_ref[...] = mult * i * xc
        xc_ref[...] = xc

    main, before, _ = _row_specs(t, tb, width, col)
    out = pl.BlockSpec((tb, width), lambda i: (i, 0))
    vec = pl.BlockSpec((1, width), lambda i: (0, 0))
    mat = pl.BlockSpec((width, width), lambda i: (0, 0))
    return pl.pallas_call(
        body, name=name, grid=(t // tb,),
        in_specs=[main, before, pl.BlockSpec((CONV_K, width), lambda i: (0, 0)), vec, mat, vec, mat, vec, vec],
        out_specs=[out] * 3, out_shape=[jax.ShapeDtypeStruct((t, width), F32)] * 3,
        compiler_params=_params(("parallel",)),
    )(proj, proj, conv_w, conv_b.reshape(1, -1), wa, ba.reshape(1, -1), wx, bx.reshape(1, -1), sp.reshape(1, -1))


def _lru_gates_bwd(xc, dtot, h_prev, wa, ba, wx, bx, sp, *, name):
    t, width = xc.shape
    tb = _tile(t, 512, 8)

    def body(xc_ref, dt_ref, hp_ref, wa_ref, ba_ref, wx_ref, bx_ref, sp_ref, dxc_ref, dwa_ref, dwx_ref, vec_ref):
        xc = xc_ref[...]
        r, i, a, mult = _lru_gate_terms(xc, wa_ref, ba_ref, wx_ref, bx_ref, sp_ref)
        db = dt_ref[...]
        d_la = db * hp_ref[...] * a - db * i * xc * (a * a / mult)
        d_pa = (-d_la * sp_ref[...]) * r * (1.0 - r)
        d_pi = db * mult * xc * i * (1.0 - i)
        dab, dib = d_pa.astype(BF16), d_pi.astype(BF16)
        dxc = db * mult * i + _dot(dab, wa_ref[...], 1, 1) + _dot(dib, wx_ref[...], 1, 1)
        dxc_ref[...] = dxc
        xb = xc.astype(BF16)
        rows = [jnp.sum(z, axis=0, keepdims=True) for z in (d_pa, d_pi, -d_la * r, dxc)]

        @pl.when(pl.program_id(0) == 0)
        def _():
            dwa_ref[...] = jnp.zeros_like(dwa_ref)
            dwx_ref[...] = jnp.zeros_like(dwx_ref)
            vec_ref[...] = jnp.zeros_like(vec_ref)

        dwa_ref[...] += _dot(xb, dab, 0, 0)
        dwx_ref[...] += _dot(xb, dib, 0, 0)
        for j, z in enumerate(rows):
            vec_ref[j:j + 1, :] += z

    blk = pl.BlockSpec((tb, width), lambda i: (i, 0))
    vec = pl.BlockSpec((1, width), lambda i: (0, 0))
    mat = pl.BlockSpec((width, width), lambda i: (0, 0))
    dxc, dwa, dwx, vecs = pl.pallas_call(
        body, name=name, grid=(t // tb,), in_specs=[blk, blk, blk, mat, vec, mat, vec, vec],
        out_specs=[blk, mat, mat, pl.BlockSpec((8, width), lambda i: (0, 0))],
        out_shape=[jax.ShapeDtypeStruct((t, width), F32), jax.ShapeDtypeStruct((width, width), F32),
                   jax.ShapeDtypeStruct((width, width), F32), jax.ShapeDtypeStruct((8, width), F32)],
        compiler_params=_params(("arbitrary",)),
    )(xc, dtot, h_prev, wa, ba.reshape(1, -1), wx, bx.reshape(1, -1), sp.reshape(1, -1))
    return dxc, dwa, dwx, vecs[:4]


GELU_C = math.sqrt(2.0 / math.pi)


def _gelu_terms(x):
    th = jnp.tanh(GELU_C * (x + 0.044715 * x * x * x))
    return 0.5 * x * (1.0 + th), 0.5 * (1.0 + th) + 0.5 * x * (1.0 - th * th) * GELU_C * (1.0 + 3 * 0.044715 * x * x)


def _mix_join_fwd(outs, lses, hs, proj, gr_col, *, name):
    t, w = hs.shape
    tb = _tile(t, 512, 16)
    n = len(outs)

    def body(*refs):
        o_refs, l_refs = refs[:n], refs[n:2 * n]
        hs_ref, gr_ref, cat_ref, attn_ref, lse_ref = refs[2 * n:]
        ls = [r[...] for r in l_refs]
        m = ls[0]
        for l in ls[1:]:
            m = jnp.maximum(m, l)
        ws = [jnp.exp(l - m) for l in ls]
        den = sum(ws)
        attn = sum(wt * r[...] for wt, r in zip(ws, o_refs)) / den
        attn_ref[...] = attn
        lse_ref[...] = m + jnp.log(den)
        cat_ref[:, :w] = attn.astype(BF16)
        cat_ref[:, w:] = (hs_ref[...] * _gelu_terms(gr_ref[...])[0]).astype(BF16)

    blk = pl.BlockSpec((tb, w), lambda i: (i, 0))
    return pl.pallas_call(
        body, name=name, grid=(t // tb,),
        in_specs=[blk] * (2 * n + 1) + [pl.BlockSpec((tb, w), lambda i: (i, gr_col))],
        out_specs=[pl.BlockSpec((tb, 2 * w), lambda i: (i, 0)), blk, blk],
        out_shape=[jax.ShapeDtypeStruct((t, 2 * w), BF16), jax.ShapeDtypeStruct((t, w), F32),
                   jax.ShapeDtypeStruct((t, w), F32)],
        compiler_params=_params(("parallel",)),
    )(*outs, *lses, hs, proj)


def _mix_join_bwd(dcat, attn, lse, hs, proj, gr_col, *, name):
    t, w = hs.shape
    hd = w // ATTN_HEADS
    tb = _tile(t, 512, 16)

    def body(dcat_ref, attn_ref, lse_ref, hs_ref, gr_ref, stats_ref, dhs_ref, dgr_ref):
        stats_ref[...] = jnp.zeros_like(stats_ref)
        for h in range(ATTN_HEADS):
            sl = slice(h * hd, (h + 1) * hd)
            stats_ref[:, h:h + 1] = lse_ref[:, h * hd:h * hd + 1]
            stats_ref[:, ATTN_HEADS + h:ATTN_HEADS + h + 1] = jnp.sum(dcat_ref[:, sl] * attn_ref[:, sl], axis=-1, keepdims=True)
        dy = dcat_ref[:, w:]
        g, dg = _gelu_terms(gr_ref[...])
        dhs_ref[...] = dy * g
        dgr_ref[...] = (dy * hs_ref[...] * dg).astype(BF16)

    blk = pl.BlockSpec((tb, w), lambda i: (i, 0))
    return pl.pallas_call(
        body, name=name, grid=(t // tb,),
        in_specs=[pl.BlockSpec((tb, 2 * w), lambda i: (i, 0)), blk, blk, blk, pl.BlockSpec((tb, w), lambda i: (i, gr_col))],
        out_specs=[pl.BlockSpec((tb, LANES), lambda i: (i, 0)), blk, blk],
        out_shape=[jax.ShapeDtypeStruct((t, LANES), F32), jax.ShapeDtypeStruct((t, w), F32),
                   jax.ShapeDtypeStruct((t, w), BF16)],
        compiler_params=_params(("parallel",)),
    )(dcat, attn, lse, hs, proj)


def _silu_terms(x):
    s = jax.nn.sigmoid(x)
    return x * s, s * (1.0 + x * (1.0 - s))


def _dn_prep_fwd(proj, conv_w, *, name):
    t = proj.shape[0]
    w3 = conv_w.shape[1]
    w = w3 // 3
    hd = w // DN_HEADS
    tb = _tile(t, 256, 8)

    def body(x_ref, x_before_ref, cw_ref, q_ref, k_ref, v_ref, c_ref):
        c = _conv_rows(_with_rows_before(x_ref, x_before_ref), cw_ref)
        c_ref[...] = c
        s = _silu_terms(c)[0]
        v_ref[...] = s[:, 2 * w:]
        for part, ref, scale in ((0, q_ref, hd ** -0.5), (1, k_ref, 1.0)):
            for h in range(DN_HEADS):
                z = s[:, part * w + h * hd:part * w + (h + 1) * hd]
                ref[:, h * hd:(h + 1) * hd] = z * (lax.rsqrt(jnp.sum(z * z, axis=-1, keepdims=True) + 1e-6) * scale)

    main, before, _ = _row_specs(t, tb, w3, 0)
    out = pl.BlockSpec((tb, w), lambda i: (i, 0))
    return pl.pallas_call(
        body, name=name, grid=(t // tb,), in_specs=[main, before, pl.BlockSpec((CONV_K, w3), lambda i: (0, 0))],
        out_specs=[out, out, out, pl.BlockSpec((tb, w3), lambda i: (i, 0))],
        out_shape=[jax.ShapeDtypeStruct((t, w), F32)] * 3 + [jax.ShapeDtypeStruct((t, w3), F32)],
        compiler_params=_params(("parallel",)),
    )(proj, proj, conv_w)


def _dn_prep_bwd(c, dq, dk, dv, *, name):
    t, w3 = c.shape
    w = w3 // 3
    hd = w // DN_HEADS
    tb = _tile(t, 256, 8)

    def body(c_ref, dq_ref, dk_ref, dv_ref, dc_ref):
        cv = c_ref[...]
        s, ds = _silu_terms(cv)
        dc_ref[:, 2 * w:] = dv_ref[...] * ds[:, 2 * w:]
        for part, ref, scale in ((0, dq_ref, hd ** -0.5), (1, dk_ref, 1.0)):
            for h in range(DN_HEADS):
                cols = slice(part * w + h * hd, part * w + (h + 1) * hd)
                z = s[:, cols]
                rn = lax.rsqrt(jnp.sum(z * z, axis=-1, keepdims=True) + 1e-6)
                y = z * rn
                dy = ref[:, h * hd:(h + 1) * hd] * scale
                dc_ref[:, cols] = rn * (dy - y * jnp.sum(dy * y, axis=-1, keepdims=True)) * ds[:, cols]

    blk = pl.BlockSpec((tb, w), lambda i: (i, 0))
    wide = pl.BlockSpec((tb, w3), lambda i: (i, 0))
    return pl.pallas_call(
        body, name=name, grid=(t // tb,), in_specs=[wide, blk, blk, blk], out_specs=wide,
        out_shape=jax.ShapeDtypeStruct((t, w3), F32), compiler_params=_params(("parallel",)),
    )(c, dq, dk, dv)


def _dn_gate_fwd(o, proj, z_col, o_norm, *, name):
    t, w = o.shape
    hd = w // DN_HEADS
    tb = _tile(t, 512, 16)

    def body(o_ref, z_ref, g_ref, y_ref):
        for h in range(DN_HEADS):
            sl = slice(h * hd, (h + 1) * hd)
            ov = o_ref[:, sl]
            rn = lax.rsqrt(jnp.mean(ov * ov, axis=-1, keepdims=True) + NORM_EPS)
            y_ref[:, sl] = (ov * rn * g_ref[...] * _silu_terms(z_ref[:, sl])[0]).astype(BF16)

    blk = pl.BlockSpec((tb, w), lambda i: (i, 0))
    return pl.pallas_call(
        body, name=name, grid=(t // tb,),
        in_specs=[blk, pl.BlockSpec((tb, w), lambda i: (i, z_col)), pl.BlockSpec((1, hd), lambda i: (0, 0))],
        out_specs=blk, out_shape=jax.ShapeDtypeStruct((t, w), BF16), compiler_params=_params(("parallel",)),
    )(o, proj, o_norm.reshape(1, hd))


def _dn_gate_bwd(o, proj, z_col, o_norm, dy, *, name):
    t, w = o.shape
    hd = w // DN_HEADS
    tb = _tile(t, 512, 16)

    def body(o_ref, z_ref, g_ref, dy_ref, do_ref, dz_ref, dg_ref):
        gv = g_ref[...]
        dg = jnp.zeros((1, hd), F32)
        for h in range(DN_HEADS):
            sl = slice(h * hd, (h + 1) * hd)
            ov, dyv = o_ref[:, sl], dy_ref[:, sl]
            sz, dsz = _silu_terms(z_ref[:, sl])
            rn = lax.rsqrt(jnp.mean(ov * ov, axis=-1, keepdims=True) + NORM_EPS)
            nv = ov * rn
            dz_ref[:, sl] = (dyv * nv * gv * dsz).astype(BF16)
            dn = dyv * gv * sz
            do_ref[:, sl] = rn * (dn - nv * jnp.mean(dn * nv, axis=-1, keepdims=True))
            dg = dg + jnp.sum(dyv * nv * sz, axis=0, keepdims=True)

        @pl.when(pl.program_id(0) == 0)
        def _():
            dg_ref[...] = dg

        @pl.when(pl.program_id(0) > 0)
        def _():
            dg_ref[...] += dg

    blk = pl.BlockSpec((tb, w), lambda i: (i, 0))
    vec = pl.BlockSpec((1, hd), lambda i: (0, 0))
    do, dz, dg = pl.pallas_call(
        body, name=name, grid=(t // tb,), in_specs=[blk, pl.BlockSpec((tb, w), lambda i: (i, z_col)), vec, blk],
        out_specs=[blk, blk, vec],
        out_shape=[jax.ShapeDtypeStruct((t, w), F32), jax.ShapeDtypeStruct((t, w), BF16),
                   jax.ShapeDtypeStruct((1, hd), F32)],
        compiler_params=_params(("arbitrary",)),
    )(o, proj, o_norm.reshape(1, hd), dy)
    return do, dz, dg.reshape(hd)


def _band_masks(n):
    qi = lax.broadcasted_iota(jnp.int32, (ATTN_BLOCK, ATTN_BLOCK), 0)
    kj = lax.broadcasted_iota(jnp.int32, (ATTN_BLOCK, ATTN_BLOCK), 1)
    return kj <= qi, jnp.logical_and(kj >= qi, n > 0)


def _dattn_fwd(qkv, dil, *, name):
    seq, w = qkv.shape[0], qkv.shape[1] // (3 * dil)
    t = seq * dil
    hd = w // ATTN_HEADS
    assert seq % ATTN_BLOCK == 0
    nb = seq // ATTN_BLOCK
    scale = hd ** -0.5

    def body(q_ref, kp_ref, kc_ref, vp_ref, vc_ref, o_ref, lse_ref):
        mc, mp = _band_masks(pl.program_id(1))
        heads = range(ATTN_HEADS)
        sls = [slice(h * hd, (h + 1) * hd) for h in heads]
        qs = [(q_ref[:, sl] * scale).astype(BF16) for sl in sls]
        scs = [jnp.where(mc, _dot(qs[h], kc_ref[:, sls[h]].astype(BF16), 1, 1), NEG) for h in heads]
        sps = [jnp.where(mp, _dot(qs[h], kp_ref[:, sls[h]].astype(BF16), 1, 1), NEG) for h in heads]
        ms = [jnp.maximum(jnp.max(sc, axis=-1, keepdims=True), jnp.max(sp, axis=-1, keepdims=True))
              for sc, sp in zip(scs, sps)]
        pcs = [jnp.exp(sc - m) for sc, m in zip(scs, ms)]
        pps = [jnp.exp(sp - m) for sp, m in zip(sps, ms)]
        dens = [jnp.sum(pc, axis=-1, keepdims=True) + jnp.sum(pp, axis=-1, keepdims=True) for pc, pp in zip(pcs, pps)]
        outs = [_dot(pcs[h].astype(BF16), vc_ref[:, sls[h]].astype(BF16), 1, 0)
                + _dot(pps[h].astype(BF16), vp_ref[:, sls[h]].astype(BF16), 1, 0) for h in heads]
        for h in heads:
            o_ref[:, sls[h]] = outs[h] / dens[h]
            lse_ref[:, sls[h]] = jnp.broadcast_to(ms[h] + jnp.log(dens[h]), (ATTN_BLOCK, hd))

    cur = pl.BlockSpec((ATTN_BLOCK, w), lambda r, n: (n, r))
    part = lambda j, row: pl.BlockSpec((ATTN_BLOCK, w), lambda r, n: (row(n), 3 * r + j))
    here, before = (lambda n: n), (lambda n: jnp.maximum(n - 1, 0))
    o, lse = pl.pallas_call(
        body, name=name, grid=(dil, nb),
        in_specs=[part(0, here), part(1, before), part(1, here), part(2, before), part(2, here)], out_specs=[cur, cur],
        out_shape=[jax.ShapeDtypeStruct((seq, dil * w), F32)] * 2, compiler_params=_params(("parallel", "parallel")),
    )(qkv, qkv, qkv, qkv, qkv)
    return o.reshape(t, w), lse.reshape(t, w)


def _dattn_bwd(qkv, do, stats, dil, *, name):
    t, w = do.shape
    hd = w // ATTN_HEADS
    seq = t // dil
    nb = seq // ATTN_BLOCK
    scale = hd ** -0.5

    def body(qc_ref, qn_ref, doc_ref, don_ref, sc_ref, sn_ref, kp_ref, kc_ref, vp_ref, vc_ref, dqkv_ref):
        n = pl.program_id(1)
        mc, mp = _band_masks(n)
        _, mx = _band_masks(jnp.where(n + 1 < nb, 1, 0))
        heads = range(ATTN_HEADS)
        sls = [slice(h * hd, (h + 1) * hd) for h in heads]
        lse_of, delta_of = (lambda ref, h: ref[:, h:h + 1]), (lambda ref, h: ref[:, ATTN_HEADS + h:ATTN_HEADS + h + 1])
        bf = lambda ref, scl=None: [(ref[:, sl] if scl is None else ref[:, sl] * scl).astype(BF16) for sl in sls]
        qc, qn, kc, kp = bf(qc_ref, scale), bf(qn_ref, scale), bf(kc_ref), bf(kp_ref)
        vc, vp, doc, don = bf(vc_ref), bf(vp_ref), bf(doc_ref), bf(don_ref)
        p_c = [jnp.exp(jnp.where(mc, _dot(qc[h], kc[h], 1, 1), NEG) - lse_of(sc_ref, h)) for h in heads]
        p_p = [jnp.exp(jnp.where(mp, _dot(qc[h], kp[h], 1, 1), NEG) - lse_of(sc_ref, h)) for h in heads]
        p_x = [jnp.exp(jnp.where(mx, _dot(qn[h], kc[h], 1, 1), NEG) - lse_of(sn_ref, h)) for h in heads]
        ds_c = [(p_c[h] * (_dot(doc[h], vc[h], 1, 1) - delta_of(sc_ref, h))).astype(BF16) for h in heads]
        ds_p = [(p_p[h] * (_dot(doc[h], vp[h], 1, 1) - delta_of(sc_ref, h))).astype(BF16) for h in heads]
        ds_x = [(p_x[h] * (_dot(don[h], vc[h], 1, 1) - delta_of(sn_ref, h))).astype(BF16) for h in heads]
        for h in heads:
            at = lambda part: slice(part * w + h * hd, part * w + (h + 1) * hd)
            dqkv_ref[:, at(0)] = ((_dot(ds_c[h], kc[h], 1, 0) + _dot(ds_p[h], kp[h], 1, 0)) * scale).astype(BF16)
            dqkv_ref[:, at(1)] = (_dot(ds_c[h], qc[h], 0, 0) + _dot(ds_x[h], qn[h], 0, 0)).astype(BF16)
            dqkv_ref[:, at(2)] = (_dot(p_c[h].astype(BF16), doc[h], 0, 0) + _dot(p_x[h].astype(BF16), don[h], 0, 0)).astype(BF16)

    cur = pl.BlockSpec((ATTN_BLOCK, w), lambda r, n: (n, r))
    nxt = pl.BlockSpec((ATTN_BLOCK, w), lambda r, n: (jnp.minimum(n + 1, nb - 1), r))
    part = lambda j, row: pl.BlockSpec((ATTN_BLOCK, w), lambda r, n: (row(n), 3 * r + j))
    here, before, after = (lambda n: n), (lambda n: jnp.maximum(n - 1, 0)), (lambda n: jnp.minimum(n + 1, nb - 1))
    stat = lambda row: pl.BlockSpec((ATTN_BLOCK, LANES), lambda r, n: (row(n), r))
    do, stats = do.reshape(seq, dil * w), stats.reshape(seq, dil * LANES)
    dqkv = pl.pallas_call(
        body, name=name, grid=(dil, nb),
        in_specs=[part(0, here), part(0, after), cur, nxt, stat(here), stat(after), part(1, before), part(1, here),
                  part(2, before), part(2, here)], out_specs=pl.BlockSpec((ATTN_BLOCK, 3 * w), lambda r, n: (n, r)),
        out_shape=jax.ShapeDtypeStruct((seq, dil * 3 * w), BF16), compiler_params=_params(("parallel", "parallel")),
    )(qkv, qkv, do, do, stats, stats, qkv, qkv, qkv, qkv)
    return dqkv.reshape(t, 3 * w)


def _xattn_fwd(q, k, v, *, name):
    t, w = q.shape
    nm = k.shape[0]
    hd = w // XA_HEADS
    scale = hd ** -0.5
    tq = _tile(t, 512, 8)

    def body(q_ref, k_ref, v_ref, o_ref):
        heads = range(XA_HEADS)
        sls = [slice(h * hd, (h + 1) * hd) for h in heads]
        ss = [_dot((q_ref[:, sl] * scale).astype(BF16), k_ref[:, sl].astype(BF16), 1, 1) for sl in sls]
        ps = [jnp.exp(s - jnp.max(s, axis=-1, keepdims=True)) for s in ss]
        ps = [p / jnp.sum(p, axis=-1, keepdims=True) for p in ps]
        for h in heads:
            o_ref[:, sls[h]] = _dot(ps[h].astype(BF16), v_ref[:, sls[h]].astype(BF16), 1, 0).astype(BF16)

    qs = pl.BlockSpec((tq, w), lambda i: (i, 0))
    ks = pl.BlockSpec((nm, w), lambda i: (0, 0))
    return pl.pallas_call(
        body, name=name, grid=(t // tq,), in_specs=[qs, ks, ks], out_specs=qs,
        out_shape=jax.ShapeDtypeStruct((t, w), BF16), compiler_params=_params(("parallel",)),
    )(q, k, v)


def _xattn_bwd(q, k, v, do, *, name):
    t, w = q.shape
    nm = k.shape[0]
    hd = w // XA_HEADS
    scale = hd ** -0.5
    tq = _tile(t, 512, 8)

    def body(q_ref, k_ref, v_ref, do_ref, dq_ref, dk_ref, dv_ref):
        first = pl.program_id(0) == 0
        heads = range(XA_HEADS)
        sls = [slice(h * hd, (h + 1) * hd) for h in heads]
        qs_ = [(q_ref[:, sl] * scale).astype(BF16) for sl in sls]
        ks_, vs_, dos = ([ref[:, sl].astype(BF16) for sl in sls] for ref in (k_ref, v_ref, do_ref))
        ss = [_dot(qs_[h], ks_[h], 1, 1) for h in heads]
        ps = [jnp.exp(s - jnp.max(s, axis=-1, keepdims=True)) for s in ss]
        ps = [p / jnp.sum(p, axis=-1, keepdims=True) for p in ps]
        dps = [_dot(dos[h], vs_[h], 1, 1) for h in heads]
        dss = [(p * (dp - jnp.sum(p * dp, axis=-1, keepdims=True))).astype(BF16) for p, dp in zip(ps, dps)]
        for h in heads:
            sl = sls[h]
            dq_ref[:, sl] = (_dot(dss[h], ks_[h], 1, 0) * scale).astype(BF16)
            dk = _dot(dss[h], qs_[h], 0, 0)
            dv = _dot(ps[h].astype(BF16), dos[h], 0, 0)

            @pl.when(first)
            def _():
                dk_ref[:, sl] = dk
                dv_ref[:, sl] = dv

            @pl.when(jnp.logical_not(first))
            def _():
                dk_ref[:, sl] += dk
                dv_ref[:, sl] += dv

    qs = pl.BlockSpec((tq, w), lambda i: (i, 0))
    ks = pl.BlockSpec((nm, w), lambda i: (0, 0))
    return pl.pallas_call(
        body, name=name, grid=(t // tq,), in_specs=[qs, ks, ks, qs], out_specs=[qs, ks, ks],
        out_shape=[jax.ShapeDtypeStruct((t, w), BF16), jax.ShapeDtypeStruct((nm, w), F32),
                   jax.ShapeDtypeStruct((nm, w), F32)],
        compiler_params=_params(("arbitrary",)),
    )(q, k, v, do)


def _dn_head_terms(q_ref, k_ref, gc_ref, gcr_ref, b_ref, h, hd, j=0):
    c = DN_CHUNK
    sl, rs = slice(h * hd, (h + 1) * hd), slice(j * c, (j + 1) * c)
    qh, kh = q_ref[rs, sl], k_ref[rs, sl]
    gcc, gcr_h, bh = gc_ref[rs, h:h + 1], gcr_ref[j, h:h + 1, :], b_ref[rs, h:h + 1]
    row = lax.broadcasted_iota(jnp.int32, (c, c), 0)
    col = lax.broadcasted_iota(jnp.int32, (c, c), 1)
    decay = jnp.exp(jnp.where(row >= col, gcc - gcr_h, NEG))
    kb = kh * bh
    kkt = _dot(kb.astype(BF16), kh.astype(BF16), 1, 1)
    qkt = _dot(qh.astype(BF16), kh.astype(BF16), 1, 1)
    gl = gcc[c - 1:c, :]
    return dict(sl=sl, rs=rs, j=j, h=h, q=qh, k=kh, gcc=gcc, b=bh, row=row, col=col, decay=decay, kb=kb, kkt=kkt,
                qkt=qkt, e=jnp.exp(gcc), el=jnp.exp(gl), r=jnp.exp(gl - gcc))


def _dn_local_fwd(q, k, v, gc, gcr, beta, *, name):
    t, w = q.shape
    hd = w // DN_HEADS
    c = DN_CHUNK
    nch = t // c
    grp = _tile(nch, DN_LOCAL_CHUNKS, 1)

    def body(q_ref, k_ref, v_ref, gc_ref, gcr_ref, b_ref,
             u_ref, wq_ref, wqt_ref, kr_ref, krt_ref, qk_ref, qkt_ref, invt_ref):
        tm = [_dn_head_terms(q_ref, k_ref, gc_ref, gcr_ref, b_ref, h, hd, j) for j in range(grp) for h in range(DN_HEADS)]
        pw = [jnp.where(m['row'] > m['col'], m['kkt'] * m['decay'], 0.0) for m in tm]
        inv = [(m['row'] == m['col']).astype(F32) - p for m, p in zip(tm, pw)]
        for _ in range(int(math.log2(c)) - 1):
            pw = [_dot(p.astype(BF16), p.astype(BF16), 1, 0) for p in pw]
            inv = [i + _dot(i.astype(BF16), p.astype(BF16), 1, 0) for i, p in zip(inv, pw)]
        for m, iv in zip(tm, inv):
            j, h = m['j'], m['h']
            rhs = jnp.concatenate([v_ref[m['rs'], m['sl']] * m['b'], m['kb'] * m['e']], axis=1).astype(BF16)
            sol = _dot(iv.astype(BF16), rhs, 1, 0)
            u_ref[m['rs'], m['sl']] = sol[:, :hd]
            wq = jnp.concatenate([sol[:, hd:], m['q'] * m['e']], axis=0)
            kr = m['k'] * m['r']
            qk = m['qkt'] * m['decay']
            wq_ref[j, h], wqt_ref[j, h] = wq.astype(BF16), wq.T.astype(BF16)
            kr_ref[j, h], krt_ref[j, h] = kr.astype(BF16), kr.T.astype(BF16)
            qk_ref[j, h], qkt_ref[j, h] = qk.astype(BF16), qk.T.astype(BF16)
            invt_ref[j, h] = iv.T.astype(BF16)

    rows = pl.BlockSpec((grp * c, w), lambda n: (n, 0))
    cols = pl.BlockSpec((grp * c, DN_HEADS), lambda n: (n, 0))
    rowg = pl.BlockSpec((grp, DN_HEADS, c), lambda n: (n, 0, 0))
    per = lambda a, b: (pl.BlockSpec((grp, DN_HEADS, a, b), lambda n: (n, 0, 0, 0)),
                        jax.ShapeDtypeStruct((nch, DN_HEADS, a, b), BF16))
    outs = [(rows, jax.ShapeDtypeStruct((t, w), F32)), per(2 * c, hd), per(hd, 2 * c), per(c, hd), per(hd, c),
            per(c, c), per(c, c), per(c, c)]
    return pl.pallas_call(
        body, name=name, grid=(nch // grp,), in_specs=[rows, rows, rows, cols, rowg, cols],
        out_specs=[o[0] for o in outs], out_shape=[o[1] for o in outs], compiler_params=_params(("parallel",)),
    )(q, k, v, gc, gcr, beta)


def _dn_state_fwd(u, wq, krt, qk, gc, *, name):
    t, w = u.shape
    hd = w // DN_HEADS
    c = DN_CHUNK
    nch = t // c
    grp = _tile(nch, DN_STEP_CHUNKS, 1)

    def body(u_ref, wq_ref, krt_ref, qk_ref, gc_ref, o_ref, vn_ref, s_ref, state):
        @pl.when(pl.program_id(0) == 0)
        def _():
            state[...] = jnp.zeros_like(state)

        heads = range(DN_HEADS)
        sls = [slice(h * hd, (h + 1) * hd) for h in heads]
        shs = [state[h] for h in heads]
        for j in range(grp):
            rs = slice(j * c, (j + 1) * c)
            wss = [_dot(wq_ref[j, h], shs[h].astype(BF16), 1, 0) for h in heads]
            vns = [(u_ref[rs, sls[h]] - wss[h][:c]).astype(BF16) for h in heads]
            outs = [wss[h][c:] + _dot(qk_ref[j, h], vns[h], 1, 0) for h in heads]
            for h in heads:
                s_ref[j, h] = shs[h]
                vn_ref[rs, sls[h]] = vns[h]
                o_ref[rs, sls[h]] = outs[h]
            shs = [shs[h] * jnp.exp(gc_ref[(j + 1) * c - 1:(j + 1) * c, h:h + 1]) + _dot(krt_ref[j, h], vns[h], 1, 0)
                   for h in heads]
        for h in heads:
            state[h] = shs[h]

    rows = pl.BlockSpec((grp * c, w), lambda n: (n, 0))
    per = lambda a, b: pl.BlockSpec((grp, DN_HEADS, a, b), lambda n: (n, 0, 0, 0))
    return pl.pallas_call(
        body, name=name, grid=(nch // grp,),
        in_specs=[rows, per(2 * c, hd), per(hd, c), per(c, c), pl.BlockSpec((grp * c, DN_HEADS), lambda n: (n, 0))],
        out_specs=[rows, rows, per(hd, hd)],
        out_shape=[jax.ShapeDtypeStruct((t, w), F32), jax.ShapeDtypeStruct((t, w), BF16),
                   jax.ShapeDtypeStruct((nch, DN_HEADS, hd, hd), F32)],
        scratch_shapes=[pltpu.VMEM((DN_HEADS, hd, hd), F32)], compiler_params=_params(("arbitrary",)),
    )(u, wq, krt, qk, gc)


def _dn_state_bwd(do, qkt, kr, vn, states, wqt, gc, *, name):
    t, w = do.shape
    hd = w // DN_HEADS
    c = DN_CHUNK
    nch = t // c
    grp = _tile(nch, DN_STEP_CHUNKS, 1)

    def body(do_ref, qkt_ref, kr_ref, vn_ref, s_ref, wqt_ref, gc_ref, dvn_ref, dkr_ref, del_ref, dstate):
        @pl.when(pl.program_id(0) == 0)
        def _():
            dstate[...] = jnp.zeros_like(dstate)

        heads = range(DN_HEADS)
        sls = [slice(h * hd, (h + 1) * hd) for h in heads]
        dsns = [dstate[h] for h in heads]
        for j in reversed(range(grp)):
            rs = slice(j * c, (j + 1) * c)
            dsbs = [d.astype(BF16) for d in dsns]
            dobs = [do_ref[rs, sl].astype(BF16) for sl in sls]
            dvns = [(_dot(qkt_ref[j, h], dobs[h], 1, 0) + _dot(kr_ref[j, h], dsbs[h], 1, 0)).astype(BF16) for h in heads]
            dkrs = [_dot(vn_ref[rs, sls[h]], dsbs[h], 1, 1) for h in heads]
            for h in heads:
                dvn_ref[rs, sls[h]] = dvns[h]
                dkr_ref[rs, sls[h]] = dkrs[h]
                d_el = jnp.sum(jnp.sum(dsns[h] * s_ref[j, h], axis=1, keepdims=True), axis=0, keepdims=True)
                del_ref[j, h:h + 1, :] = jnp.broadcast_to(d_el, (1, LANES))
            dsns = [dsns[h] * jnp.exp(gc_ref[(j + 1) * c - 1:(j + 1) * c, h:h + 1])
                    + _dot(wqt_ref[j, h], jnp.concatenate([-dvns[h], dobs[h]], axis=0), 1, 0) for h in heads]
        for h in heads:
            dstate[h] = dsns[h]

    rev = lambda n: nch // grp - 1 - n
    rows = pl.BlockSpec((grp * c, w), lambda n: (rev(n), 0))
    per = lambda a, b: pl.BlockSpec((grp, DN_HEADS, a, b), lambda n: (rev(n), 0, 0, 0))
    return pl.pallas_call(
        body, name=name, grid=(nch // grp,),
        in_specs=[rows, per(c, c), per(c, hd), rows, per(hd, hd), per(hd, 2 * c),
                  pl.BlockSpec((grp * c, DN_HEADS), lambda n: (rev(n), 0))],
        out_specs=[rows, rows, pl.BlockSpec((grp, DN_HEADS, LANES), lambda n: (rev(n), 0, 0))],
        out_shape=[jax.ShapeDtypeStruct((t, w), BF16), jax.ShapeDtypeStruct((t, w), F32),
                   jax.ShapeDtypeStruct((nch, DN_HEADS, LANES), F32)],
        scratch_shapes=[pltpu.VMEM((DN_HEADS, hd, hd), F32)], compiler_params=_params(("arbitrary",)),
    )(do, qkt, kr, vn, states, wqt, gc)


def _dn_local_bwd(q, k, v, gc, gcr, beta, invt, u, wq, vn, states, do, dvn, dkr, d_el, *, name):
    t, w = q.shape
    hd = w // DN_HEADS
    c = DN_CHUNK
    nch = t // c
    grp = _tile(nch, DN_LOCAL_CHUNKS, 1)

    def body(q_ref, k_ref, v_ref, gc_ref, gcr_ref, b_ref, invt_ref, u_ref, wq_ref, vn_ref, s_ref, do_ref, dvn_ref,
             dkr_ref, del_ref, dq_ref, dk_ref, dv_ref, dgc_ref, dgr_ref, db_ref):
        tms = [_dn_head_terms(q_ref, k_ref, gc_ref, gcr_ref, b_ref, h, hd, j) for j in range(grp) for h in range(DN_HEADS)]
        dobs = [do_ref[m['rs'], m['sl']].astype(BF16) for m in tms]
        tss = [_dot(jnp.concatenate([dob, -dvn_ref[m['rs'], m['sl']]], axis=0), s_ref[m['j'], m['h']].astype(BF16), 1, 1)
               for m, dob in zip(tms, dobs)]
        d_qks = [_dot(dob, vn_ref[m['rs'], m['sl']], 1, 1) for m, dob in zip(tms, dobs)]
        d_rhss = [_dot(invt_ref[m['j'], m['h']],
                       jnp.concatenate([dvn_ref[m['rs'], m['sl']], ts[c:].astype(BF16)], axis=1), 1, 0)
                  for m, ts in zip(tms, tss)]
        d_as = [-_dot(d_rhs.astype(BF16), jnp.concatenate(
            [u_ref[m['rs'], m['sl']].astype(BF16), wq_ref[m['j'], m['h'], :c, :]], axis=1), 1, 1)
                for m, d_rhs in zip(tms, d_rhss)]
        for m, ts, d_qk, d_rhs, d_a in zip(tms, tss, d_qks, d_rhss, d_as):
            d_qe, j, h, rs = ts[:c], m['j'], m['h'], m['rs']
            sl, qh, kh, bh, e, r, decay = m['sl'], m['q'], m['k'], m['b'], m['e'], m['r'], m['decay']
            vh = v_ref[rs, sl]
            d_ru, d_rw = d_rhs[:, :hd], d_rhs[:, hd:]
            dv_ref[rs, sl] = d_ru * bh
            d_e = jnp.sum(d_rw * m['kb'], axis=1, keepdims=True) + jnp.sum(d_qe * qh, axis=1, keepdims=True)
            d_n = jnp.where(m['row'] > m['col'], d_a, 0.0)
            d_m, d_p = (d_n * decay).astype(BF16), (d_qk * decay).astype(BF16)
            tk = _dot(jnp.concatenate([d_m, d_p], axis=0), kh.astype(BF16), 1, 0)
            d_kb = d_rw * e + tk[:c]
            dq_ref[rs, sl] = tk[c:] + d_qe * e
            dk = _dot(d_m, m['kb'].astype(BF16), 0, 0) + _dot(d_p, qh.astype(BF16), 0, 0)
            dd = (d_n * m['kkt'] + d_qk * m['qkt']) * decay
            d_kr = dkr_ref[rs, sl]
            d_r = jnp.sum(d_kr * kh, axis=1, keepdims=True)
            d_gl = del_ref[j, h:h + 1, 0:1] * m['el'] + jnp.sum(d_r * r, axis=0, keepdims=True)
            last = lax.broadcasted_iota(jnp.int32, (c, 1), 0) == c - 1
            dgc_ref[rs, h:h + 1] = jnp.sum(dd, axis=1, keepdims=True) + d_e * e - d_r * r + jnp.where(last, d_gl, 0.0)
            dgr_ref[j, h:h + 1, :] = -jnp.sum(dd, axis=0, keepdims=True)
            dk_ref[rs, sl] = dk + d_kr * r + d_kb * bh
            db_ref[rs, h:h + 1] = jnp.sum(d_ru * vh, axis=1, keepdims=True) + jnp.sum(d_kb * kh, axis=1, keepdims=True)

    rows = pl.BlockSpec((grp * c, w), lambda n: (n, 0))
    cols = pl.BlockSpec((grp * c, DN_HEADS), lambda n: (n, 0))
    rowg = pl.BlockSpec((grp, DN_HEADS, c), lambda n: (n, 0, 0))
    per = lambda a, b: pl.BlockSpec((grp, DN_HEADS, a, b), lambda n: (n, 0, 0, 0))
    return pl.pallas_call(
        body, name=name, grid=(nch // grp,),
        in_specs=[rows, rows, rows, cols, rowg, cols, per(c, c), rows, per(2 * c, hd), rows, per(hd, hd), rows, rows,
                  rows, pl.BlockSpec((grp, DN_HEADS, LANES), lambda n: (n, 0, 0))],
        out_specs=[rows, rows, rows, cols, rowg, cols],
        out_shape=[jax.ShapeDtypeStruct((t, w), F32)] * 3
        + [jax.ShapeDtypeStruct((t, DN_HEADS), F32), jax.ShapeDtypeStruct((nch, DN_HEADS, c), F32),
           jax.ShapeDtypeStruct((t, DN_HEADS), F32)],
        compiler_params=_params(("parallel",)),
    )(q, k, v, gc, gcr, beta, invt, u, wq, vn, states, do, dvn, dkr, d_el)


def _other_chips():
    x, y = lax.axis_index("x"), lax.axis_index("y")
    return [(1 - x, y), (x, 1 - y), (1 - x, 1 - y)]


def _gather(src, *, name):
    def body(src_ref, out_ref, send_sems, recv_sems, local_sem):
        x, y, c = lax.axis_index("x"), lax.axis_index("y"), lax.axis_index("c")
        me, sibling, chips = (x, y, c), (x, y, 1 - c), _other_chips()
        slot = lambda px, py, pc: out_ref.at[4 * px + 2 * py + pc]

        def copy(k, block, to, own=False):
            return pltpu.make_async_remote_copy(
                src_ref=src_ref if own else slot(*block), dst_ref=slot(*block), send_sem=send_sems.at[k],
                recv_sem=recv_sems.at[k], device_id=to, device_id_type=MESH)

        local = pltpu.make_async_copy(src_ref, slot(*me), local_sem)
        local.start()
        first = [copy(0, me, sibling, own=True)] + [copy(1 + j, me, (*chip, c), own=True) for j, chip in enumerate(chips)]
        for cp in first:
            cp.start()
        passed = [copy(4 + j, (*chip, c), sibling) for j, chip in enumerate(chips)]
        for j, chip in enumerate(chips):
            copy(1 + j, (*chip, c), me).wait_recv()
            passed[j].start()
        copy(0, sibling, me).wait_recv()
        for j, chip in enumerate(chips):
            copy(4 + j, (*chip, 1 - c), me).wait_recv()
        for cp in first + passed:
            cp.wait_send()
        local.wait()

    return pl.pallas_call(
        body, name=name, out_shape=jax.ShapeDtypeStruct((N_DEV,) + src.shape, src.dtype),
        in_specs=[pl.BlockSpec(memory_space=pl.ANY)], out_specs=pl.BlockSpec(memory_space=pl.ANY),
        scratch_shapes=[pltpu.SemaphoreType.DMA((N_DEV - 1,)), pltpu.SemaphoreType.DMA((N_DEV - 1,)),
                        pltpu.SemaphoreType.DMA(())],
    )(src)


def _swap_with_sibling(src, *, name):
    def body(src_ref, out_ref, send_sem, recv_sem):
        x, y, c = lax.axis_index("x"), lax.axis_index("y"), lax.axis_index("c")
        cp = pltpu.make_async_remote_copy(src_ref=src_ref.at[1 - c], dst_ref=out_ref, send_sem=send_sem,
                                          recv_sem=recv_sem, device_id=(x, y, 1 - c), device_id_type=MESH)
        cp.start()
        cp.wait()

    return pl.pallas_call(
        body, name=name, out_shape=jax.ShapeDtypeStruct(src.shape[1:], src.dtype),
        in_specs=[pl.BlockSpec(memory_space=pl.ANY)], out_specs=pl.BlockSpec(memory_space=pl.ANY),
        scratch_shapes=[pltpu.SemaphoreType.DMA(()), pltpu.SemaphoreType.DMA(())],
    )(src)


def _add_own_half(src, got, *, name):
    _, n, r, c = src.shape
    tr = _tile(r, 512, 16)

    def body(s0_ref, s1_ref, got_ref, o_ref):
        own = jnp.where(lax.axis_index("c") == 0, s0_ref[0, 0], s1_ref[0, 0])
        o_ref[0] = (own.astype(F32) + got_ref[0].astype(F32)).astype(BF16)

    half = lambda h: pl.BlockSpec((1, 1, tr, c), lambda j, i: (h, j, i, 0))
    blk = pl.BlockSpec((1, tr, c), lambda j, i: (j, i, 0))
    return pl.pallas_call(
        body, name=name, grid=(n, r // tr), in_specs=[half(0), half(1), blk], out_specs=blk,
        out_shape=jax.ShapeDtypeStruct((n, r, c), BF16), compiler_params=_params(("parallel", "parallel")),
    )(src, src, got)


def _swap_between_chips(src, *, name):
    def body(src_ref, out_ref, send_sems, recv_sems, local_sem):
        x, y, c = lax.axis_index("x"), lax.axis_index("y"), lax.axis_index("c")
        mine = 2 * x + y
        local = pltpu.make_async_copy(src_ref.at[mine], out_ref.at[mine], local_sem)
        local.start()
        copies = [pltpu.make_async_remote_copy(
            src_ref=src_ref.at[2 * px + py], dst_ref=out_ref.at[mine], send_sem=send_sems.at[j],
            recv_sem=recv_sems.at[j], device_id=(px, py, c), device_id_type=MESH) for j, (px, py) in enumerate(_other_chips())]
        for cp in copies:
            cp.start()
        for j, (px, py) in enumerate(_other_chips()):
            pltpu.make_async_remote_copy(
                src_ref=src_ref.at[mine], dst_ref=out_ref.at[2 * px + py], send_sem=send_sems.at[j],
                recv_sem=recv_sems.at[j], device_id=(px, py, c), device_id_type=MESH).wait_recv()
        for cp in copies:
            cp.wait_send()
        local.wait()

    return pl.pallas_call(
        body, name=name, out_shape=jax.ShapeDtypeStruct(src.shape, src.dtype),
        in_specs=[pl.BlockSpec(memory_space=pl.ANY)], out_specs=pl.BlockSpec(memory_space=pl.ANY),
        scratch_shapes=[pltpu.SemaphoreType.DMA((3,)), pltpu.SemaphoreType.DMA((3,)), pltpu.SemaphoreType.DMA(())],
    )(src)


def _sum_adamw(parts, w, m, v, *, name):
    n_parts, r, c = parts.shape
    tr = _tile(r, 256, 8)
    c1, c2 = 1.0 - ADAM_B1 ** ADAM_STEP, 1.0 - ADAM_B2 ** ADAM_STEP

    def body(p_ref, w_ref, m_ref, v_ref, g_ref, d_ref, nm_ref, nv_ref):
        g = p_ref[0].astype(F32)
        for s in range(1, n_parts):
            g = g + p_ref[s].astype(F32)
        nm = ADAM_B1 * m_ref[...] + (1.0 - ADAM_B1) * g
        nv = ADAM_B2 * v_ref[...] + (1.0 - ADAM_B2) * (g * g)
        g_ref[...] = g
        nm_ref[...] = nm
        nv_ref[...] = nv
        d_ref[...] = -ADAM_LR * ((nm / c1) / (jnp.sqrt(nv / c2) + ADAM_EPS) + ADAM_WD * w_ref[...])

    blk = pl.BlockSpec((tr, c), lambda i: (i, 0))
    return pl.pallas_call(
        body, name=name, grid=(r // tr,), in_specs=[pl.BlockSpec((n_parts, tr, c), lambda i: (0, i, 0)), blk, blk, blk],
        out_specs=[blk] * 4, out_shape=[jax.ShapeDtypeStruct((r, c), F32)] * 4, compiler_params=_params(("parallel",)),
    )(parts, w, m, v)


def _pack_rows(n):
    return -(-n // (PACK_COLS * PACK_ROWS)) * PACK_ROWS


def _pack(blocks):
    parts, spans, at = [], [], 0
    for blk, n_lead in blocks:
        lead = blk.shape[:n_lead]
        n = math.prod(blk.shape[n_lead:])
        rows = _pack_rows(n)
        flat = blk.reshape(lead + (n,))
        flat = jnp.pad(flat, [(0, 0)] * n_lead + [(0, rows * PACK_COLS - n)])
        parts.append(flat.reshape(lead + (rows, PACK_COLS)))
        spans.append((at, rows))
        at += rows
    return jnp.concatenate(parts, axis=-2), spans


def _unpack(buf, span, shape):
    at, rows = span
    lead = buf.shape[:-2]
    flat = lax.slice_in_dim(buf, at, at + rows, axis=buf.ndim - 2).reshape(lead + (rows * PACK_COLS,))
    return lax.slice_in_dim(flat, 0, math.prod(shape), axis=len(lead)).reshape(lead + tuple(shape))


def _join_shards(g, axis):
    g = jnp.moveaxis(g, 0, axis)
    return g.reshape(g.shape[:axis] + (g.shape[axis] * g.shape[axis + 1],) + g.shape[axis + 2:])


def _split_shards(full, axis):
    s = full.shape
    g = full.reshape(s[:axis] + (N_DEV, s[axis] // N_DEV) + s[axis + 1:])
    return jnp.moveaxis(g, axis, 0)


def _residual_out(a, w, x, scale, next_gain, name, **tiles):
    out = _mm(a, w, name=name, res=x, scale=scale, norm=next_gain, **tiles)
    return out if next_gain is not None else (out, None)


def _ffn_fwd(x, h, w_in, w_out, tag, next_gain):
    gate, up, a = _mm_swiglu(h, w_in, name=tag + "_in", tm=512, tn=2816)
    xo, hn = _residual_out(a, w_out, x, 0.5, next_gain, tag + "_out", tk=2816)
    return xo, hn, (x, h, gate, up, a)


def _ffn_bwd(saved, g, w_in, w_out, dxo, tag):
    x, h, gate, up, a = saved
    d_w_out = _mm(a, dxo, name=tag + "_dwout", ta=True, scale=0.5, tm=1408, tk=2048)
    du = _mm_dswiglu(dxo, w_out, gate, up, name=tag + "_da", scale=0.5)
    d_w_in = _mm(du, h, name=tag + "_dwin", ta=True, tm=1408, tk=2048)
    dx, dg = _mm(du, w_in, name=tag + "_dh", tm=512, tk=5632, rms_bwd=(x, g, dxo))
    return dx, dg, d_w_in, d_w_out


def _block_diag(w):
    n, j, k = w.shape
    return (w[:, :, None, :] * jnp.eye(n, dtype=w.dtype)[:, None, :, None]).reshape(n * j, n * k)


def _diag_blocks(dense, n):
    j, k = dense.shape[0] // n, dense.shape[1] // n
    return jnp.stack([dense[i * j:(i + 1) * j, i * k:(i + 1) * k] for i in range(n)], axis=0)


def _attn_lru_fwd(x, h, p, tag, next_gain):
    aw = ATTN_HEADS * 64
    proj = _mm(h, p['ab_w_in'], name=tag + "_in", tb=True, tn=1280)
    qkv = proj[:, :3 * aw].astype(BF16)
    views = {dil: qkv.reshape(qkv.shape[0] // dil, dil * 3 * aw) for _, dil in DILATED_PATTERNS}
    outs, lses = [], []
    for window, dil in DILATED_PATTERNS:
        assert window // dil == ATTN_BLOCK
        o, l = _dattn_fwd(views[dil], dil, name=f"{tag}_attn{dil}")
        outs.append(o)
        lses.append(l)
    wa, wx = _block_diag(p['lru_w_a']).astype(BF16), _block_diag(p['lru_w_x']).astype(BF16)
    sp, sp_vjp = jax.vjp(lambda lam: LRU_C * jax.nn.softplus(-lam), p['lru_lambda'])
    a, b, xc = _lru_gates_fwd(proj, 3, p['lru_conv_w'], p['lru_conv_b'], wa, p['lru_b_a'], wx, p['lru_b_x'], sp,
                              name=tag + "_gates")
    hs = _scan(a, b, name=tag + "_scan")
    cat, attn, lse_all = _mix_join_fwd(outs, lses, hs, proj, 4, name=tag + "_join")
    xo, hn = _residual_out(cat, p['ab_w_out'], x, 1.0, next_gain, tag + "_out")
    return xo, hn, (x, h, proj, views, attn, lse_all, a, hs, xc, wa, wx, sp, sp_vjp, cat)


def _attn_lru_bwd(saved, p, dxo, tag):
    x, h, proj, views, attn, lse_all, a, hs, xc, wa, wx, sp, sp_vjp, cat = saved
    aw = attn.shape[1]
    g = {'ab_w_out': _mm(cat, dxo, name=tag + "_dwout", ta=True)}
    dcat = _mm(dxo, p['ab_w_out'], name=tag + "_dcat", tb=True)
    stats, dhs, dgr = _mix_join_bwd(dcat, attn, lse_all, hs, proj, 4, name=tag + "_djoin")
    a_next = jnp.concatenate([a[1:], jnp.zeros_like(a[:1])], axis=0)
    dtot = _scan(a_next, dhs, name=tag + "_dscan", reverse=True)
    h_prev = jnp.concatenate([jnp.zeros_like(hs[:1]), hs[:-1]], axis=0)
    dxc, dwa, dwx, vecs = _lru_gates_bwd(xc, dtot, h_prev, wa, p['lru_b_a'], wx, p['lru_b_x'], sp, name=tag + "_dgates")
    dxr, g['lru_conv_w'] = _conv_bwd(dxc, proj, 3, p['lru_conv_w'], name=tag + "_dconv")
    g['lru_b_a'], g['lru_b_x'], g['lru_conv_b'] = vecs[0], vecs[1], vecs[3]
    g['lru_lambda'], = sp_vjp(vecs[2])
    g['lru_w_a'], g['lru_w_x'] = _diag_blocks(dwa, LRU_BLOCKS), _diag_blocks(dwx, LRU_BLOCKS)
    dattn = dcat[:, :aw]
    dqkv = sum(_dattn_bwd(views[dil], dattn, stats, dil, name=f"{tag}_dattn{dil}").astype(F32)
               for _, dil in DILATED_PATTERNS)
    dproj = jnp.concatenate([dqkv.astype(BF16), dxr, dgr], axis=-1)
    g['ab_w_in'] = _mm(dproj, h, name=tag + "_dwin", ta=True, tm=1280)
    dx, g['mix_norm'] = _mm(dproj, p['ab_w_in'], name=tag + "_dh", tm=512, tk=2560, rms_bwd=(x, p['mix_norm'], dxo))
    return dx, g


def _dn_decay(a, b, a_log, dt_bias):
    t = a.shape[0]
    g = -jnp.exp(a_log) * jax.nn.softplus(a + dt_bias)
    gc = jnp.cumsum(g.reshape(t // DN_CHUNK, DN_CHUNK, DN_HEADS), axis=1)
    return gc.reshape(t, DN_HEADS), jnp.swapaxes(gc, 1, 2), jax.nn.sigmoid(b)


def _dn_in_width(w):
    return -(-(4 * w + 2 * DN_HEADS) // LANES) * LANES


def _deltanet_fwd(x, h, p, tag, next_gain):
    w = p['dn_w_out'].shape[0]
    proj = _mm(h, p['dn_w_in'], name=tag + "_in", tb=True, tn=1408)
    q, k, v, c = _dn_prep_fwd(proj, p['dn_conv_w'], name=tag + "_prep")
    a, b = proj[:, 4 * w:4 * w + DN_HEADS], proj[:, 4 * w + DN_HEADS:4 * w + 2 * DN_HEADS]
    (gc, gcr, beta), decay_vjp = jax.vjp(_dn_decay, a, b, p['dn_a_log'], p['dn_dt_bias'])
    prep = (q, k, v, gc, gcr, beta)
    u, wq, wqt, kr, krt, qk, qkt, invt = _dn_local_fwd(*prep, name=tag + "_local")
    o, vn, states = _dn_state_fwd(u, wq, krt, qk, gc, name=tag + "_state")
    og = _dn_gate_fwd(o, proj, 3, p['dn_o_norm'], name=tag + "_gate")
    xo, hn = _residual_out(og, p['dn_w_out'], x, 1.0, next_gain, tag + "_out")
    return xo, hn, (x, h, proj, c, o, prep, (u, wq, wqt, kr, qkt, invt, vn, states), decay_vjp, og)


def _deltanet_bwd(saved, p, dxo, tag):
    x, h, proj, c, o, prep, (u, wq, wqt, kr, qkt, invt, vn, states), decay_vjp, og = saved
    g = {'dn_w_out': _mm(og, dxo, name=tag + "_dwout", ta=True)}
    dog = _mm(dxo, p['dn_w_out'], name=tag + "_dog", tb=True)
    do, dz, g['dn_o_norm'] = _dn_gate_bwd(o, proj, 3, p['dn_o_norm'], dog, name=tag + "_dgate")
    dvn, dkr, d_el = _dn_state_bwd(do, qkt, kr, vn, states, wqt, prep[3], name=tag + "_dstate")
    dq, dk, dv, dgc, dgr, dbeta = _dn_local_bwd(*prep, invt, u, wq, vn, states, do, dvn, dkr, d_el, name=tag + "_dlocal")
    dc = _dn_prep_bwd(c, dq, dk, dv, name=tag + "_dprep")
    dqkv, g['dn_conv_w'] = _conv_bwd(dc, proj, 0, p['dn_conv_w'], name=tag + "_dconv")
    da, db, g['dn_a_log'], g['dn_dt_bias'] = decay_vjp((dgc, dgr, dbeta))
    t = x.shape[0]
    pad = jnp.zeros((t, p['dn_w_in'].shape[0] - dqkv.shape[1] - dz.shape[1] - 2 * DN_HEADS), BF16)
    groups = [("qkv", dqkv), ("z", dz), ("ab", jnp.concatenate([da.astype(BF16), db.astype(BF16), pad], axis=-1))]
    parts, dh, lo = [], None, 0
    for label, piece in groups:
        hi = lo + piece.shape[1]
        parts.append(_mm(piece, h, name=f"{tag}_dwin_{label}", ta=True, tm=1536, tk=2048))
        last = dict(tm=512, rms_bwd=(x, p['mix_norm'], dxo)) if label == groups[-1][0] else {}
        dh = _mm(piece, p['dn_w_in'][lo:hi], name=f"{tag}_dh_{label}", tk=1536, res=dh, **last)
        lo = hi
    g['dn_w_in'] = jnp.concatenate(parts, axis=0)
    dx, g['mix_norm'] = dh
    return dx, g


def _xattn_block_fwd(x, h, mem, p, tag, next_gain):
    w = x.shape[1]
    mh = _rms_fwd(mem, p['xa_mem_norm'], name=tag + "_mnorm", out_dtype=BF16)
    q = _mm(h, p['xa_wq'], name=tag + "_q")
    kv = _mm(mh, p['xa_wkv'], name=tag + "_kv", tb=True)
    k, v = kv[:, :w], kv[:, w:]
    o = _xattn_fwd(q, k, v, name=tag + "_attn").astype(BF16)
    xo, hn = _residual_out(o, p['xa_wo'], x, 1.0, next_gain, tag + "_out")
    return xo, hn, (x, h, mh, q, k, v, o)


def _xattn_block_bwd(saved, mem, p, dxo, tag):
    x, h, mh, q, k, v, o = saved
    g = {'xa_wo': _mm(o, dxo, name=tag + "_dwo", ta=True)}
    do = _mm(dxo, p['xa_wo'], name=tag + "_do", tb=True)
    dq, dk, dv = _xattn_bwd(q, k, v, do, name=tag + "_dattn")
    dq = dq.astype(BF16)
    dkv = jnp.concatenate([dk, dv], axis=-1).astype(BF16)
    g['xa_wq'] = _mm(h, dq, name=tag + "_dwq", ta=True)
    g['xa_wkv'] = _mm(dkv, mh, name=tag + "_dwkv", ta=True)
    dmh = _mm(dkv, p['xa_wkv'], name=tag + "_dmh")
    _, g['xa_mem_norm'] = _rms_bwd(mem, p['xa_mem_norm'], dmh, None, name=tag + "_dmnorm")
    dx, g['xa_norm'] = _mm(dq, p['xa_wq'], name=tag + "_dh", tb=True, tm=512, rms_bwd=(x, p['xa_norm'], dxo))
    return dx, g


def _layer_params(full, layer):
    p = {n: full[n][layer] for n in ('ffn1_norm', 'ffn1_w_in', 'ffn1_w_out', 'mix_norm', 'xa_norm', 'xa_mem_norm',
                                     'xa_wq', 'xa_wkv', 'xa_wo', 'ffn2_norm', 'ffn2_w_in', 'ffn2_w_out')}
    mixer = ('ab_w_in', 'lru_conv_w', 'lru_conv_b', 'lru_w_a', 'lru_b_a', 'lru_w_x', 'lru_b_x', 'lru_lambda', 'ab_w_out') \
        if layer % 2 == 0 else ('dn_w_in', 'dn_conv_w', 'dn_a_log', 'dn_dt_bias', 'dn_o_norm', 'dn_w_out')
    p.update({n: full[n][layer // 2] for n in mixer})
    return p


def _step(x, mem, target, full, depth):
    saved = []
    params = [_layer_params(full, layer) for layer in range(depth)]
    h = _rms_fwd(x, params[0]['ffn1_norm'], name="l0_ffn1_norm", out_dtype=BF16)
    for layer, p in enumerate(params):
        tag = f"l{layer}"
        after = params[layer + 1]['ffn1_norm'] if layer + 1 < depth else None
        x, h, s1 = _ffn_fwd(x, h, p['ffn1_w_in'], p['ffn1_w_out'], tag + "_ffn1", p['mix_norm'])
        x, h, s2 = (_attn_lru_fwd if layer % 2 == 0 else _deltanet_fwd)(x, h, p, tag + "_mix", p['xa_norm'])
        x, h, s3 = _xattn_block_fwd(x, h, mem, p, tag + "_xa", p['ffn2_norm'])
        x, h, s4 = _ffn_fwd(x, h, p['ffn2_w_in'], p['ffn2_w_out'], tag + "_ffn2", after)
        saved.append((p, s1, s2, s3, s4))
    sq, dx, d_final = _final_loss(x, full['final_norm'], target, name="final_loss")
    per_layer = []
    for layer in reversed(range(depth)):
        p, s1, s2, s3, s4 = saved[layer]
        tag = f"l{layer}"
        g = {}
        dx, g['ffn2_norm'], g['ffn2_w_in'], g['ffn2_w_out'] = _ffn_bwd(s4, p['ffn2_norm'], p['ffn2_w_in'], p['ffn2_w_out'], dx, tag + "_ffn2")
        dx, gx = _xattn_block_bwd(s3, mem, p, dx, tag + "_xa")
        dx, gm = (_attn_lru_bwd if layer % 2 == 0 else _deltanet_bwd)(s2, p, dx, tag + "_mix")
        dx, g['ffn1_norm'], g['ffn1_w_in'], g['ffn1_w_out'] = _ffn_bwd(s1, p['ffn1_norm'], p['ffn1_w_in'], p['ffn1_w_out'], dx, tag + "_ffn1")
        g.update(gx)
        g.update(gm)
        per_layer.insert(0, g)
    grads = {n: [g[n] for g in per_layer if n in g] for n in WEIGHTS[:-1]}
    grads['final_norm'] = d_final
    return sq, dx, grads


def kernel(x, mem, ffn1_norm, ffn1_w_in, ffn1_w_out, mix_norm, xa_norm, xa_mem_norm, xa_wq, xa_wkv, xa_wo, ffn2_norm, ffn2_w_in, ffn2_w_out, ab_w_in, lru_conv_w, lru_conv_b, lru_w_a, lru_b_a, lru_w_x, lru_b_x, lru_lambda, ab_w_out, dn_w_in, dn_conv_w, dn_a_log, dn_dt_bias, dn_o_norm, dn_w_out, final_norm, loss_target, m_ffn1_norm, m_ffn1_w_in, m_ffn1_w_out, m_mix_norm, m_xa_norm, m_xa_mem_norm, m_xa_wq, m_xa_wkv, m_xa_wo, m_ffn2_norm, m_ffn2_w_in, m_ffn2_w_out, m_ab_w_in, m_lru_conv_w, m_lru_conv_b, m_lru_w_a, m_lru_b_a, m_lru_w_x, m_lru_b_x, m_lru_lambda, m_ab_w_out, m_dn_w_in, m_dn_conv_w, m_dn_a_log, m_dn_dt_bias, m_dn_o_norm, m_dn_w_out, m_final_norm, v_ffn1_norm, v_ffn1_w_in, v_ffn1_w_out, v_mix_norm, v_xa_norm, v_xa_mem_norm, v_xa_wq, v_xa_wkv, v_xa_wo, v_ffn2_norm, v_ffn2_w_in, v_ffn2_w_out, v_ab_w_in, v_lru_conv_w, v_lru_conv_b, v_lru_w_a, v_lru_b_a, v_lru_w_x, v_lru_b_x, v_lru_lambda, v_ab_w_out, v_dn_w_in, v_dn_conv_w, v_dn_a_log, v_dn_dt_bias, v_dn_o_norm, v_dn_w_out, v_final_norm):
    args = dict(locals())
    flip = lambda n, a: jnp.swapaxes(a, 1, 2) if n in TRANSPOSED else a
    local = {n: flip(n, args[n]) for n in WEIGHTS}
    depth = ffn1_norm.shape[0]
    matrices = [n for n in WEIGHTS if n in SHARD_AXIS and n not in GATHER_F32]

    def entries(get):
        return [(a, n, l) for n in WEIGHTS for l, a in
                (enumerate(get(n)) if n in matrices else [(None, get(n))])]

    mats = [e for e in entries(lambda n: local[n]) if e[1] in matrices]
    send16, spans16 = _pack([(a.astype(BF16), 0) for a, _, _ in mats])
    send32, spans32 = _pack([(local[n], 0) for n in GATHER_F32])
    got16 = _gather(send16, name="gather_matrices")
    got32 = _gather(send32, name="gather_filters")
    full = {n: ([None] * local[n].shape[0] if n in matrices else local[n]) for n in WEIGHTS}
    for (a, n, l), span in zip(mats, spans16):
        full[n][l] = _unpack(got16, span, a.shape).reshape(N_DEV * a.shape[0], a.shape[1])
    for n, span in zip(GATHER_F32, spans32):
        full[n] = _join_shards(_unpack(got32, span, local[n].shape), SHARD_AXIS[n])
    dn_rows = full['dn_w_in'][0].shape[0]
    full['dn_w_in'] = [jnp.pad(w, ((0, _dn_in_width(w.shape[1]) - dn_rows), (0, 0))) for w in full['dn_w_in']]

    sq, dx, grads = _step(x[0], mem[0], loss_target[0], full, depth)
    grads['dn_w_in'] = [g[:dn_rows] for g in grads['dn_w_in']]
    loss = lax.psum(0.5 * jnp.sum(sq) / x.shape[2], ("x", "y", "c"))

    by_core = lambda z: jnp.swapaxes(z.reshape((N_DEV // 2, 2) + z.shape[1:]), 0, 1)

    def contribution(n):
        if n in matrices:
            return [by_core(g.reshape((N_DEV, g.shape[0] // N_DEV, g.shape[1]))) for g in grads[n]]
        g = grads[n] if n == 'final_norm' else jnp.stack(grads[n], axis=0)
        return by_core(_split_shards(g, SHARD_AXIS[n]) if n in SHARD_AXIS else jnp.broadcast_to(g, (N_DEV,) + g.shape))

    send, spans = _pack([(a.astype(BF16), 2) for a, _, _ in entries(contribution)])
    got = _swap_with_sibling(send, name="grads_to_sibling")
    chip_sum = _add_own_half(send, got, name="grads_chip_sum")
    parts = _swap_between_chips(chip_sum, name="grads_between_chips")
    state = [_pack([(a, 0) for a, _, _ in entries(lambda n: flip(n, args[pre + n]))])[0] for pre in ("", "m_", "v_")]
    outs = _sum_adamw(parts, *state, name="sum_adamw")
    result = []
    for o in outs:
        got_rows = {}
        for (a, n, l), span in zip(entries(lambda n: local[n]), spans):
            got_rows.setdefault(n, []).append(_unpack(o, span, a.shape))
        result += [flip(n, jnp.stack(got_rows[n], axis=0) if n in matrices else got_rows[n][0]) for n in WEIGHTS]
    return (loss, dx[None], *result)
```

```python
import math

import jax
import jax.numpy as jnp
from jax import lax
from jax.experimental import pallas as pl
from jax.experimental.pallas import tpu as pltpu

F32, BF16 = jnp.float32, jnp.bfloat16
MESH = pl.DeviceIdType.MESH
N_DEV = 8
V7X_VMEM_LIMIT = 56 << 20
LANES = 128
PACK_COLS = 1024
PACK_ROWS = 16
NEG = -1e30

NORM_EPS = 1e-6
CONV_K = 4
ATTN_HEADS = 8
DILATED_PATTERNS = ((128, 1), (512, 4), (2048, 16))
ATTN_BLOCK = 128
LRU_BLOCKS = 8
LRU_C = 8.0
DN_HEADS = 8
DN_CHUNK = 64
DN_STEP_CHUNKS = 4
DN_LOCAL_CHUNKS = 4
XA_HEADS = 4
ADAM_LR, ADAM_B1, ADAM_B2, ADAM_EPS, ADAM_WD, ADAM_STEP = 0.001, 0.9, 0.999, 1e-08, 0.01, 10

WEIGHTS = ['ffn1_norm', 'ffn1_w_in', 'ffn1_w_out', 'mix_norm', 'xa_norm', 'xa_mem_norm', 'xa_wq', 'xa_wkv', 'xa_wo',
           'ffn2_norm', 'ffn2_w_in', 'ffn2_w_out', 'ab_w_in', 'lru_conv_w', 'lru_conv_b', 'lru_w_a', 'lru_b_a',
           'lru_w_x', 'lru_b_x', 'lru_lambda', 'ab_w_out', 'dn_w_in', 'dn_conv_w', 'dn_a_log', 'dn_dt_bias',
           'dn_o_norm', 'dn_w_out', 'final_norm']
SHARD_AXIS = {'ffn1_w_in': 2, 'ffn1_w_out': 1, 'xa_wq': 1, 'xa_wkv': 2, 'xa_wo': 1, 'ffn2_w_in': 2, 'ffn2_w_out': 1,
              'ab_w_in': 2, 'lru_conv_w': 2, 'ab_w_out': 1, 'dn_w_in': 2, 'dn_conv_w': 2, 'dn_w_out': 1}
GATHER_F32 = ('lru_conv_w', 'dn_conv_w')
TRANSPOSED = ('ffn1_w_in', 'xa_wkv', 'ffn2_w_in', 'ab_w_in', 'dn_w_in')


def _params(sem=None):
    return pltpu.CompilerParams(dimension_semantics=sem, vmem_limit_bytes=V7X_VMEM_LIMIT)


def _tile(n, pref, mult):
    best = None
    t = mult
    while t <= min(n, pref):
        if n % t == 0:
            best = t
        t += mult
    return n if best is None else best


def _dot(a, b, ca, cb, prec=None):
    return lax.dot_general(a, b, (((ca,), (cb,)), ((), ())), preferred_element_type=F32, precision=prec)


def _mm(a, b, *, name, ta=False, tb=False, out_dtype=F32, res=None, scale=1.0, tm=1024, tn=1024, tk=1024,
        norm=None, rms_bwd=None):
    m, kdim = (a.shape[1], a.shape[0]) if ta else a.shape
    n = b.shape[0] if tb else b.shape[1]
    assert (b.shape[1] if tb else b.shape[0]) == kdim
    tm = _tile(m, tm, LANES if ta else 16)
    tn = _tile(n, tn, LANES)
    tk = _tile(kdim, tk, LANES)
    nk = kdim // tk
    a_spec = pl.BlockSpec((tk, tm), lambda i, j, k: (k, i)) if ta else pl.BlockSpec((tm, tk), lambda i, j, k: (i, k))
    b_spec = pl.BlockSpec((tn, tk), lambda i, j, k: (j, k)) if tb else pl.BlockSpec((tk, tn), lambda i, j, k: (k, j))
    o_spec = pl.BlockSpec((tm, tn), lambda i, j, k: (i, j))
    vec = pl.BlockSpec((1, tn), lambda i, j, k: (0, j))
    ca, cb = (0 if ta else 1), (1 if tb else 0)
    has_res = res is not None
    assert (norm is None and rms_bwd is None) or tn == n

    ins, specs = [a, b], [a_spec, b_spec]
    if has_res:
        ins.append(res)
        specs.append(o_spec)
    out_specs, out_shape = [o_spec], [jax.ShapeDtypeStruct((m, n), out_dtype)]
    if norm is not None:
        ins.append(norm.reshape(1, n))
        specs.append(vec)
        out_specs.append(o_spec)
        out_shape.append(jax.ShapeDtypeStruct((m, n), BF16))
    if rms_bwd is not None:
        ins += [rms_bwd[0], rms_bwd[1].reshape(1, n), rms_bwd[2]]
        specs += [o_spec, vec, o_spec]
        out_specs.append(vec)
        out_shape.append(jax.ShapeDtypeStruct((1, n), F32))
    n_in, n_out = len(ins), len(out_specs)

    def finish(acc, extra, outs, first_rows):
        y = acc if scale == 1.0 else acc * scale
        if has_res:
            y = y + extra[0][...]
        tail = extra[has_res:]
        if norm is not None:
            rs = lax.rsqrt(jnp.mean(y * y, axis=-1, keepdims=True) + NORM_EPS)
            outs[1][...] = (y * rs * tail[0][...]).astype(BF16)
        if rms_bwd is not None:
            x_ref, g_ref, dres_ref = tail
            xv = x_ref[...]
            rs = lax.rsqrt(jnp.mean(xv * xv, axis=-1, keepdims=True) + NORM_EPS)
            xh = xv * rs
            dgh = y * g_ref[...]
            part = jnp.sum(y * xh, axis=0, keepdims=True)
            y = dres_ref[...] + rs * (dgh - xh * jnp.mean(dgh * xh, axis=-1, keepdims=True))

            @pl.when(first_rows)
            def _():
                outs[1][...] = part

            @pl.when(jnp.logical_not(first_rows))
            def _():
                outs[1][...] += part

        outs[0][...] = y.astype(out_dtype)

    def body(*refs):
        a_ref, b_ref = refs[0], refs[1]
        extra, outs = refs[2:n_in], refs[n_in:n_in + n_out]
        p = _dot(a_ref[...].astype(BF16), b_ref[...].astype(BF16), ca, cb)
        first_rows = pl.program_id(0) == 0
        if nk == 1:
            finish(p, extra, outs, first_rows)
            return
        acc = refs[n_in + n_out]
        k = pl.program_id(2)

        @pl.when(k == 0)
        def _():
            acc[...] = p

        @pl.when(k > 0)
        def _():
            acc[...] += p

        @pl.when(k == nk - 1)
        def _():
            finish(acc[...], extra, outs, first_rows)

    out = pl.pallas_call(
        body, name=name, grid=(m // tm, n // tn, nk), in_specs=specs, out_specs=out_specs, out_shape=out_shape,
        scratch_shapes=[] if nk == 1 else [pltpu.VMEM((tm, tn), F32)],
        compiler_params=_params(("parallel" if rms_bwd is None else "arbitrary", "parallel", "arbitrary")),
    )(*ins)
    if rms_bwd is not None:
        return out[0], out[1].reshape(n)
    return out[0] if n_out == 1 else tuple(out)


def _rms_fwd(x, g, *, name, out_dtype):
    r, d = x.shape
    tr = _tile(r, 512, 16)

    def body(x_ref, g_ref, o_ref):
        xv = x_ref[...]
        rs = lax.rsqrt(jnp.mean(xv * xv, axis=-1, keepdims=True) + NORM_EPS)
        o_ref[...] = (xv * rs * g_ref[...]).astype(out_dtype)

    return pl.pallas_call(
        body, name=name, grid=(r // tr,),
        in_specs=[pl.BlockSpec((tr, d), lambda i: (i, 0)), pl.BlockSpec((1, d), lambda i: (0, 0))],
        out_specs=pl.BlockSpec((tr, d), lambda i: (i, 0)),
        out_shape=jax.ShapeDtypeStruct((r, d), out_dtype), compiler_params=_params(("parallel",)),
    )(x, g.reshape(1, d))


def _rms_bwd(x, g, dh, dres, *, name):
    r, d = x.shape
    tr = _tile(r, 512, 8)
    has_res = dres is not None

    def body(*refs):
        x_ref, g_ref, dh_ref = refs[:3]
        r_ref = refs[3] if has_res else None
        dx_ref, dg_ref = refs[3 + has_res], refs[4 + has_res]
        xv = x_ref[...]
        rs = lax.rsqrt(jnp.mean(xv * xv, axis=-1, keepdims=True) + NORM_EPS)
        xh = xv * rs
        dhv = dh_ref[...]
        dgh = dhv * g_ref[...]
        dx = rs * (dgh - xh * jnp.mean(dgh * xh, axis=-1, keepdims=True))
        if has_res:
            dx = dx + r_ref[...]
        dx_ref[...] = dx
        part = jnp.sum(dhv * xh, axis=0, keepdims=True)

        @pl.when(pl.program_id(0) == 0)
        def _():
            dg_ref[...] = part

        @pl.when(pl.program_id(0) > 0)
        def _():
            dg_ref[...] += part

    row = pl.BlockSpec((tr, d), lambda i: (i, 0))
    vec = pl.BlockSpec((1, d), lambda i: (0, 0))
    ins, specs = [x, g.reshape(1, d), dh], [row, vec, row]
    if has_res:
        ins.append(dres)
        specs.append(row)
    dx, dg = pl.pallas_call(
        body, name=name, grid=(r // tr,), in_specs=specs, out_specs=[row, vec],
        out_shape=[jax.ShapeDtypeStruct((r, d), F32), jax.ShapeDtypeStruct((1, d), F32)],
        compiler_params=_params(("arbitrary",)),
    )(*ins)
    return dx, dg.reshape(d)


def _mm_swiglu(h, w_in, *, name, tm=1024, tn=256):
    t, d = h.shape
    f = w_in.shape[0] // 2
    tm, tn = _tile(t, tm, 16), _tile(f, tn, LANES)
    nj = f // tn

    def body(h_ref, wg_ref, wu_ref, g_ref, u_ref, a_ref):
        hv = h_ref[...]
        gate, up = _dot(hv, wg_ref[...], 1, 1), _dot(hv, wu_ref[...], 1, 1)
        g_ref[...] = gate.astype(BF16)
        u_ref[...] = up.astype(BF16)
        a_ref[...] = (gate * jax.nn.sigmoid(gate) * up).astype(BF16)

    out = pl.BlockSpec((tm, tn), lambda j, i: (i, j))
    return pl.pallas_call(
        body, name=name, grid=(nj, t // tm),
        in_specs=[pl.BlockSpec((tm, d), lambda j, i: (i, 0)), pl.BlockSpec((tn, d), lambda j, i: (j, 0)),
                  pl.BlockSpec((tn, d), lambda j, i: (j + nj, 0))],
        out_specs=[out, out, out], out_shape=[jax.ShapeDtypeStruct((t, f), BF16)] * 3,
        compiler_params=_params(("parallel", "parallel")),
    )(h, w_in, w_in)


def _mm_dswiglu(dy, w_out, gate, up, *, name, scale, tm=512):
    t, d = dy.shape
    f = w_out.shape[0]
    tm = _tile(t, tm, 16)

    def body(dy_ref, w_ref, g_ref, u_ref, o_ref):
        da = _dot(dy_ref[...].astype(BF16), w_ref[...], 1, 1) * scale
        gv = g_ref[...].astype(F32)
        s = jax.nn.sigmoid(gv)
        o_ref[:, :f] = (da * u_ref[...].astype(F32) * (s * (1.0 + gv * (1.0 - s)))).astype(BF16)
        o_ref[:, f:] = (da * gv * s).astype(BF16)

    blk = pl.BlockSpec((tm, f), lambda i: (i, 0))
    return pl.pallas_call(
        body, name=name, grid=(t // tm,),
        in_specs=[pl.BlockSpec((tm, d), lambda i: (i, 0)), pl.BlockSpec((f, d), lambda i: (0, 0)), blk, blk],
        out_specs=pl.BlockSpec((tm, 2 * f), lambda i: (i, 0)), out_shape=jax.ShapeDtypeStruct((t, 2 * f), BF16),
        compiler_params=_params(("parallel",)),
    )(dy, w_out, gate, up)


def _final_loss(x, g, target, *, name):
    r, d = x.shape
    tr = _tile(r, 512, 8)

    def body(x_ref, g_ref, t_ref, sq_ref, dx_ref, dg_ref):
        xv = x_ref[...]
        gv = g_ref[...]
        rs = lax.rsqrt(jnp.mean(xv * xv, axis=-1, keepdims=True) + NORM_EPS)
        xh = xv * rs
        err = xh * gv - t_ref[...]
        dy = err * (1.0 / d)
        dgh = dy * gv
        dx_ref[...] = rs * (dgh - xh * jnp.mean(dgh * xh, axis=-1, keepdims=True))
        sq = jnp.sum(err * err, axis=0, keepdims=True)
        part = jnp.sum(dy * xh, axis=0, keepdims=True)

        @pl.when(pl.program_id(0) == 0)
        def _():
            sq_ref[...] = sq
            dg_ref[...] = part

        @pl.when(pl.program_id(0) > 0)
        def _():
            sq_ref[...] += sq
            dg_ref[...] += part

    row = pl.BlockSpec((tr, d), lambda i: (i, 0))
    vec = pl.BlockSpec((1, d), lambda i: (0, 0))
    sq, dx, dg = pl.pallas_call(
        body, name=name, grid=(r // tr,), in_specs=[row, vec, row], out_specs=[vec, row, vec],
        out_shape=[jax.ShapeDtypeStruct((1, d), F32), jax.ShapeDtypeStruct((r, d), F32),
                   jax.ShapeDtypeStruct((1, d), F32)],
        compiler_params=_params(("arbitrary",)),
    )(x, g.reshape(1, d), target)
    return sq.reshape(d), dx, dg.reshape(d)


def _scan(a, b, *, name, reverse=False):
    t, w = a.shape
    tb = _tile(t, 1024, 8)
    nblk, ngrp = t // tb, tb // 8

    def body(a_ref, b_ref, h_ref, carry):
        @pl.when(pl.program_id(0) == 0)
        def _():
            carry[...] = jnp.zeros_like(carry)

        row = lax.broadcasted_iota(jnp.int32, (8, w), 0)

        def group(i, c):
            r0 = pl.multiple_of((ngrp - 1 - i if reverse else i) * 8, 8)
            av, bv = a_ref[pl.ds(r0, 8), :], b_ref[pl.ds(r0, 8), :]
            for s in (1, 2, 4):
                keep = row < 8 - s if reverse else row >= s
                shift = 8 - s if reverse else s
                bv = jnp.where(keep, bv + av * pltpu.roll(bv, shift, axis=0), bv)
                av = jnp.where(keep, av * pltpu.roll(av, shift, axis=0), av)
            hv = bv + av * c
            h_ref[pl.ds(r0, 8), :] = hv
            return hv[0:1, :] if reverse else hv[7:8, :]

        carry[0:1, :] = lax.fori_loop(0, ngrp, group, carry[0:1, :])

    blk = pl.BlockSpec((tb, w), lambda i: (nblk - 1 - i if reverse else i, 0))
    return pl.pallas_call(
        body, name=name, grid=(t // tb,), in_specs=[blk, blk], out_specs=blk,
        out_shape=jax.ShapeDtypeStruct((t, w), F32), scratch_shapes=[pltpu.VMEM((8, w), F32)],
        compiler_params=_params(("arbitrary",)),
    )(a, b)


def _row_specs(t, tb, width, col):
    per = tb // 8
    main = pl.BlockSpec((tb, width), lambda i: (i, col))
    before = pl.BlockSpec((8, width), lambda i: (jnp.maximum(i * per - 1, 0), col))
    after = pl.BlockSpec((8, width), lambda i: (jnp.minimum((i + 1) * per, t // 8 - 1), col))
    return main, before, after


def _with_rows_before(x_ref, before_ref):
    return jnp.concatenate([jnp.where(pl.program_id(0) > 0, before_ref[...], 0.0), x_ref[...]], axis=0)


def _tap(xe, s):
    return xe[8:] if s == 0 else pltpu.roll(xe, s, axis=0)[8:]


def _conv_rows(xe, w_ref):
    return sum(_tap(xe, CONV_K - 1 - k) * w_ref[k:k + 1, :] for k in range(CONV_K))


def _conv_bwd(dy, x, col, w, *, name):
    t, width = dy.shape
    tb = _tile(t, 256, 8)
    nblk = t // tb

    def body(dy_ref, dy_after_ref, x_ref, x_before_ref, w_ref, dx_ref, dw_ref):
        i = pl.program_id(0)
        dyv = dy_ref[...]
        dye = jnp.concatenate([dyv, jnp.where(i < nblk - 1, dy_after_ref[...], 0.0)], axis=0)
        dx = dyv * w_ref[CONV_K - 1:CONV_K, :]
        for s in range(1, CONV_K):
            dx = dx + pltpu.roll(dye, tb + 8 - s, axis=0)[:tb] * w_ref[CONV_K - 1 - s:CONV_K - s, :]
        dx_ref[...] = dx.astype(BF16)
        xe = _with_rows_before(x_ref, x_before_ref)

        @pl.when(i == 0)
        def _():
            dw_ref[...] = jnp.zeros_like(dw_ref)

        for k in range(CONV_K):
            dw_ref[k:k + 1, :] += jnp.sum(dyv * _tap(xe, CONV_K - 1 - k), axis=0, keepdims=True)

    main, _, after = _row_specs(t, tb, width, 0)
    xmain, xbefore, _ = _row_specs(t, tb, width, col)
    dx, dw = pl.pallas_call(
        body, name=name, grid=(nblk,),
        in_specs=[main, after, xmain, xbefore, pl.BlockSpec((CONV_K, width), lambda i: (0, 0))],
        out_specs=[main, pl.BlockSpec((8, width), lambda i: (0, 0))],
        out_shape=[jax.ShapeDtypeStruct((t, width), BF16), jax.ShapeDtypeStruct((8, width), F32)],
        compiler_params=_params(("arbitrary",)),
    )(dy, dy, x, x, w)
    return dx, dw[:CONV_K]


def _expm1(x):
    small = x * (1.0 + x * (0.5 + x * (1.0 / 6.0 + x * (1.0 / 24.0 + x * (1.0 / 120.0 + x * (1.0 / 720.0))))))
    return jnp.where(jnp.abs(x) < 0.1, small, jnp.exp(x) - 1.0)


def _lru_gate_terms(xc, wa_ref, ba_ref, wx_ref, bx_ref, sp_ref):
    xb = xc.astype(BF16)
    r = jax.nn.sigmoid(_dot(xb, wa_ref[...], 1, 0) + ba_ref[...])
    i = jax.nn.sigmoid(_dot(xb, wx_ref[...], 1, 0) + bx_ref[...])
    log_a = -r * sp_ref[...]
    return r, i, jnp.exp(log_a), jnp.sqrt(-_expm1(2.0 * log_a))


def _lru_gates_fwd(proj, col, conv_w, conv_b, wa, ba, wx, bx, sp, *, name):
    t = proj.shape[0]
    width = conv_w.shape[1]
    tb = _tile(t, 512, 8)

    def body(x_ref, x_before_ref, cw_ref, cb_ref, wa_ref, ba_ref, wx_ref, bx_ref, sp_ref, a_ref, b_ref, xc_ref):
        xc = _conv_rows(_with_rows_before(x_ref, x_before_ref), cw_ref) + cb_ref[...]
        r, i, a, mult = _lru_gate_terms(xc, wa_ref, ba_ref, wx_ref, bx_ref, sp_ref)
        a_ref[...] = a
        b_ref[...] = mult * i * xc
        xc_ref[...] = xc

    main, before, _ = _row_specs(t, tb, width, col)
    out = pl.BlockSpec((tb, width), lambda i: (i, 0))
    vec = pl.BlockSpec((1, width), lambda i: (0, 0))
    mat = pl.BlockSpec((width, width), lambda i: (0, 0))
    return pl.pallas_call(
        body, name=name, grid=(t // tb,),
        in_specs=[main, before, pl.BlockSpec((CONV_K, width), lambda i: (0, 0)), vec, mat, vec, mat, vec, vec],
        out_specs=[out] * 3, out_shape=[jax.ShapeDtypeStruct((t, width), F32)] * 3,
        compiler_params=_params(("parallel",)),
    )(proj, proj, conv_w, conv_b.reshape(1, -1), wa, ba.reshape(1, -1), wx, bx.reshape(1, -1), sp.reshape(1, -1))


def _lru_gates_bwd(xc, dtot, h_prev, wa, ba, wx, bx, sp, *, name):
    t, width = xc.shape
    tb = _tile(t, 512, 8)

    def body(xc_ref, dt_ref, hp_ref, wa_ref, ba_ref, wx_ref, bx_ref, sp_ref, dxc_ref, dwa_ref, dwx_ref, vec_ref):
        xc = xc_ref[...]
        r, i, a, mult = _lru_gate_terms(xc, wa_ref, ba_ref, wx_ref, bx_ref, sp_ref)
        db = dt_ref[...]
        d_la = db * hp_ref[...] * a - db * i * xc * (a * a / mult)
        d_pa = (-d_la * sp_ref[...]) * r * (1.0 - r)
        d_pi = db * mult * xc * i * (1.0 - i)
        dab, dib = d_pa.astype(BF16), d_pi.astype(BF16)
        dxc = db * mult * i + _dot(dab, wa_ref[...], 1, 1) + _dot(dib, wx_ref[...], 1, 1)
        dxc_ref[...] = dxc
        xb = xc.astype(BF16)
        rows = [jnp.sum(z, axis=0, keepdims=True) for z in (d_pa, d_pi, -d_la * r, dxc)]

        @pl.when(pl.program_id(0) == 0)
        def _():
            dwa_ref[...] = jnp.zeros_like(dwa_ref)
            dwx_ref[...] = jnp.zeros_like(dwx_ref)
            vec_ref[...] = jnp.zeros_like(vec_ref)

        dwa_ref[...] += _dot(xb, dab, 0, 0)
        dwx_ref[...] += _dot(xb, dib, 0, 0)
        for j, z in enumerate(rows):
            vec_ref[j:j + 1, :] += z

    blk = pl.BlockSpec((tb, width), lambda i: (i, 0))
    vec = pl.BlockSpec((1, width), lambda i: (0, 0))
    mat = pl.BlockSpec((width, width), lambda i: (0, 0))
    dxc, dwa, dwx, vecs = pl.pallas_call(
        body, name=name, grid=(t // tb,), in_specs=[blk, blk, blk, mat, vec, mat, vec, vec],
        out_specs=[blk, mat, mat, pl.BlockSpec((8, width), lambda i: (0, 0))],
        out_shape=[jax.ShapeDtypeStruct((t, width), F32), jax.ShapeDtypeStruct((width, width), F32),
                   jax.ShapeDtypeStruct((width, width), F32), jax.ShapeDtypeStruct((8, width), F32)],
        compiler_params=_params(("arbitrary",)),
    )(xc, dtot, h_prev, wa, ba.reshape(1, -1), wx, bx.reshape(1, -1), sp.reshape(1, -1))
    return dxc, dwa, dwx, vecs[:4]


GELU_C = math.sqrt(2.0 / math.pi)


def _gelu_terms(x):
    th = jnp.tanh(GELU_C * (x + 0.044715 * x * x * x))
    return 0.5 * x * (1.0 + th), 0.5 * (1.0 + th) + 0.5 * x * (1.0 - th * th) * GELU_C * (1.0 + 3 * 0.044715 * x * x)


def _mix_join_fwd(outs, lses, hs, proj, gr_col, *, name):
    t, w = hs.shape
    tb = _tile(t, 512, 16)
    n = len(outs)

    def body(*refs):
        o_refs, l_refs = refs[:n], refs[n:2 * n]
        hs_ref, gr_ref, cat_ref, attn_ref, lse_ref = refs[2 * n:]
        ls = [r[...] for r in l_refs]
        m = ls[0]
        for l in ls[1:]:
            m = jnp.maximum(m, l)
        ws = [jnp.exp(l - m) for l in ls]
        den = sum(ws)
        attn = sum(wt * r[...] for wt, r in zip(ws, o_refs)) / den
        attn_ref[...] = attn
        lse_ref[...] = m + jnp.log(den)
        cat_ref[:, :w] = attn.astype(BF16)
        cat_ref[:, w:] = (hs_ref[...] * _gelu_terms(gr_ref[...])[0]).astype(BF16)

    blk = pl.BlockSpec((tb, w), lambda i: (i, 0))
    return pl.pallas_call(
        body, name=name, grid=(t // tb,),
        in_specs=[blk] * (2 * n + 1) + [pl.BlockSpec((tb, w), lambda i: (i, gr_col))],
        out_specs=[pl.BlockSpec((tb, 2 * w), lambda i: (i, 0)), blk, blk],
        out_shape=[jax.ShapeDtypeStruct((t, 2 * w), BF16), jax.ShapeDtypeStruct((t, w), F32),
                   jax.ShapeDtypeStruct((t, w), F32)],
        compiler_params=_params(("parallel",)),
    )(*outs, *lses, hs, proj)


def _mix_join_bwd(dcat, attn, lse, hs, proj, gr_col, *, name):
    t, w = hs.shape
    hd = w // ATTN_HEADS
    tb = _tile(t, 512, 16)

    def body(dcat_ref, attn_ref, lse_ref, hs_ref, gr_ref, stats_ref, dhs_ref, dgr_ref):
        stats_ref[...] = jnp.zeros_like(stats_ref)
        for h in range(ATTN_HEADS):
            sl = slice(h * hd, (h + 1) * hd)
            stats_ref[:, h:h + 1] = lse_ref[:, h * hd:h * hd + 1]
            stats_ref[:, ATTN_HEADS + h:ATTN_HEADS + h + 1] = jnp.sum(dcat_ref[:, sl] * attn_ref[:, sl], axis=-1, keepdims=True)
        dy = dcat_ref[:, w:]
        g, dg = _gelu_terms(gr_ref[...])
        dhs_ref[...] = dy * g
        dgr_ref[...] = (dy * hs_ref[...] * dg).astype(BF16)

    blk = pl.BlockSpec((tb, w), lambda i: (i, 0))
    return pl.pallas_call(
        body, name=name, grid=(t // tb,),
        in_specs=[pl.BlockSpec((tb, 2 * w), lambda i: (i, 0)), blk, blk, blk, pl.BlockSpec((tb, w), lambda i: (i, gr_col))],
        out_specs=[pl.BlockSpec((tb, LANES), lambda i: (i, 0)), blk, blk],
        out_shape=[jax.ShapeDtypeStruct((t, LANES), F32), jax.ShapeDtypeStruct((t, w), F32),
                   jax.ShapeDtypeStruct((t, w), BF16)],
        compiler_params=_params(("parallel",)),
    )(dcat, attn, lse, hs, proj)


def _silu_terms(x):
    s = jax.nn.sigmoid(x)
    return x * s, s * (1.0 + x * (1.0 - s))


def _dn_prep_fwd(proj, conv_w, *, name):
    t = proj.shape[0]
    w3 = conv_w.shape[1]
    w = w3 // 3
    hd = w // DN_HEADS
    tb = _tile(t, 256, 8)

    def body(x_ref, x_before_ref, cw_ref, q_ref, k_ref, v_ref, c_ref):
        c = _conv_rows(_with_rows_before(x_ref, x_before_ref), cw_ref)
        c_ref[...] = c
        s = _silu_terms(c)[0]
        v_ref[...] = s[:, 2 * w:]
        for part, ref, scale in ((0, q_ref, hd ** -0.5), (1, k_ref, 1.0)):
            for h in range(DN_HEADS):
                z = s[:, part * w + h * hd:part * w + (h + 1) * hd]
                ref[:, h * hd:(h + 1) * hd] = z * (lax.rsqrt(jnp.sum(z * z, axis=-1, keepdims=True) + 1e-6) * scale)

    main, before, _ = _row_specs(t, tb, w3, 0)
    out = pl.BlockSpec((tb, w), lambda i: (i, 0))
    return pl.pallas_call(
        body, name=name, grid=(t // tb,), in_specs=[main, before, pl.BlockSpec((CONV_K, w3), lambda i: (0, 0))],
        out_specs=[out, out, out, pl.BlockSpec((tb, w3), lambda i: (i, 0))],
        out_shape=[jax.ShapeDtypeStruct((t, w), F32)] * 3 + [jax.ShapeDtypeStruct((t, w3), F32)],
        compiler_params=_params(("parallel",)),
    )(proj, proj, conv_w)


def _dn_prep_bwd(c, dq, dk, dv, *, name):
    t, w3 = c.shape
    w = w3 // 3
    hd = w // DN_HEADS
    tb = _tile(t, 256, 8)

    def body(c_ref, dq_ref, dk_ref, dv_ref, dc_ref):
        cv = c_ref[...]
        s, ds = _silu_terms(cv)
        dc_ref[:, 2 * w:] = dv_ref[...] * ds[:, 2 * w:]
        for part, ref, scale in ((0, dq_ref, hd ** -0.5), (1, dk_ref, 1.0)):
            for h in range(DN_HEADS):
                cols = slice(part * w + h * hd, part * w + (h + 1) * hd)
                z = s[:, cols]
                rn = lax.rsqrt(jnp.sum(z * z, axis=-1, keepdims=True) + 1e-6)
                y = z * rn
                dy = ref[:, h * hd:(h + 1) * hd] * scale
                dc_ref[:, cols] = rn * (dy - y * jnp.sum(dy * y, axis=-1, keepdims=True)) * ds[:, cols]

    blk = pl.BlockSpec((tb, w), lambda i: (i, 0))
    wide = pl.BlockSpec((tb, w3), lambda i: (i, 0))
    return pl.pallas_call(
        body, name=name, grid=(t // tb,), in_specs=[wide, blk, blk, blk], out_specs=wide,
        out_shape=jax.ShapeDtypeStruct((t, w3), F32), compiler_params=_params(("parallel",)),
    )(c, dq, dk, dv)


def _dn_gate_fwd(o, proj, z_col, o_norm, *, name):
    t, w = o.shape
    hd = w // DN_HEADS
    tb = _tile(t, 512, 16)

    def body(o_ref, z_ref, g_ref, y_ref):
        for h in range(DN_HEADS):
            sl = slice(h * hd, (h + 1) * hd)
            ov = o_ref[:, sl]
            rn = lax.rsqrt(jnp.mean(ov * ov, axis=-1, keepdims=True) + NORM_EPS)
            y_ref[:, sl] = (ov * rn * g_ref[...] * _silu_terms(z_ref[:, sl])[0]).astype(BF16)

    blk = pl.BlockSpec((tb, w), lambda i: (i, 0))
    return pl.pallas_call(
        body, name=name, grid=(t // tb,),
        in_specs=[blk, pl.BlockSpec((tb, w), lambda i: (i, z_col)), pl.BlockSpec((1, hd), lambda i: (0, 0))],
        out_specs=blk, out_shape=jax.ShapeDtypeStruct((t, w), BF16), compiler_params=_params(("parallel",)),
    )(o, proj, o_norm.reshape(1, hd))


def _dn_gate_bwd(o, proj, z_col, o_norm, dy, *, name):
    t, w = o.shape
    hd = w // DN_HEADS
    tb = _tile(t, 512, 16)

    def body(o_ref, z_ref, g_ref, dy_ref, do_ref, dz_ref, dg_ref):
        gv = g_ref[...]
        dg = jnp.zeros((1, hd), F32)
        for h in range(DN_HEADS):
            sl = slice(h * hd, (h + 1) * hd)
            ov, dyv = o_ref[:, sl], dy_ref[:, sl]
            sz, dsz = _silu_terms(z_ref[:, sl])
            rn = lax.rsqrt(jnp.mean(ov * ov, axis=-1, keepdims=True) + NORM_EPS)
            nv = ov * rn
            dz_ref[:, sl] = (dyv * nv * gv * dsz).astype(BF16)
            dn = dyv * gv * sz
            do_ref[:, sl] = rn * (dn - nv * jnp.mean(dn * nv, axis=-1, keepdims=True))
            dg = dg + jnp.sum(dyv * nv * sz, axis=0, keepdims=True)

        @pl.when(pl.program_id(0) == 0)
        def _():
            dg_ref[...] = dg

        @pl.when(pl.program_id(0) > 0)
        def _():
            dg_ref[...] += dg

    blk = pl.BlockSpec((tb, w), lambda i: (i, 0))
    vec = pl.BlockSpec((1, hd), lambda i: (0, 0))
    do, dz, dg = pl.pallas_call(
        body, name=name, grid=(t // tb,), in_specs=[blk, pl.BlockSpec((tb, w), lambda i: (i, z_col)), vec, blk],
        out_specs=[blk, blk, vec],
        out_shape=[jax.ShapeDtypeStruct((t, w), F32), jax.ShapeDtypeStruct((t, w), BF16),
                   jax.ShapeDtypeStruct((1, hd), F32)],
        compiler_params=_params(("arbitrary",)),
    )(o, proj, o_norm.reshape(1, hd), dy)
    return do, dz, dg.reshape(hd)


def _band_masks(n):
    qi = lax.broadcasted_iota(jnp.int32, (ATTN_BLOCK, ATTN_BLOCK), 0)
    kj = lax.broadcasted_iota(jnp.int32, (ATTN_BLOCK, ATTN_BLOCK), 1)
    return kj <= qi, jnp.logical_and(kj >= qi, n > 0)


def _dattn_fwd(qkv, dil, *, name):
    seq, w = qkv.shape[0], qkv.shape[1] // (3 * dil)
    t = seq * dil
    hd = w // ATTN_HEADS
    assert seq % ATTN_BLOCK == 0
    nb = seq // ATTN_BLOCK
    scale = hd ** -0.5

    def body(q_ref, kp_ref, kc_ref, vp_ref, vc_ref, o_ref, lse_ref):
        mc, mp = _band_masks(pl.program_id(1))
        heads = range(ATTN_HEADS)
        sls = [slice(h * hd, (h + 1) * hd) for h in heads]
        qs = [(q_ref[:, sl] * scale).astype(BF16) for sl in sls]
        scs = [jnp.where(mc, _dot(qs[h], kc_ref[:, sls[h]].astype(BF16), 1, 1), NEG) for h in heads]
        sps = [jnp.where(mp, _dot(qs[h], kp_ref[:, sls[h]].astype(BF16), 1, 1), NEG) for h in heads]
        ms = [jnp.maximum(jnp.max(sc, axis=-1, keepdims=True), jnp.max(sp, axis=-1, keepdims=True))
              for sc, sp in zip(scs, sps)]
        pcs = [jnp.exp(sc - m) for sc, m in zip(scs, ms)]
        pps = [jnp.exp(sp - m) for sp, m in zip(sps, ms)]
        dens = [jnp.sum(pc, axis=-1, keepdims=True) + jnp.sum(pp, axis=-1, keepdims=True) for pc, pp in zip(pcs, pps)]
        outs = [_dot(pcs[h].astype(BF16), vc_ref[:, sls[h]].astype(BF16), 1, 0)
                + _dot(pps[h].astype(BF16), vp_ref[:, sls[h]].astype(BF16), 1, 0) for h in heads]
        for h in heads:
            o_ref[:, sls[h]] = outs[h] / dens[h]
            lse_ref[:, sls[h]] = jnp.broadcast_to(ms[h] + jnp.log(dens[h]), (ATTN_BLOCK, hd))

    cur = pl.BlockSpec((ATTN_BLOCK, w), lambda r, n: (n, r))
    part = lambda j, row: pl.BlockSpec((ATTN_BLOCK, w), lambda r, n: (row(n), 3 * r + j))
    here, before = (lambda n: n), (lambda n: jnp.maximum(n - 1, 0))
    o, lse = pl.pallas_call(
        body, name=name, grid=(dil, nb),
        in_specs=[part(0, here), part(1, before), part(1, here), part(2, before), part(2, here)], out_specs=[cur, cur],
        out_shape=[jax.ShapeDtypeStruct((seq, dil * w), F32)] * 2, compiler_params=_params(("parallel", "parallel")),
    )(qkv, qkv, qkv, qkv, qkv)
    return o.reshape(t, w), lse.reshape(t, w)


def _dattn_bwd(qkv, do, stats, dil, *, name):
    t, w = do.shape
    hd = w // ATTN_HEADS
    seq = t // dil
    nb = seq // ATTN_BLOCK
    scale = hd ** -0.5

    def body(qc_ref, qn_ref, doc_ref, don_ref, sc_ref, sn_ref, kp_ref, kc_ref, vp_ref, vc_ref, dqkv_ref):
        n = pl.program_id(1)
        mc, mp = _band_masks(n)
        _, mx = _band_masks(jnp.where(n + 1 < nb, 1, 0))
        heads = range(ATTN_HEADS)
        sls = [slice(h * hd, (h + 1) * hd) for h in heads]
        lse_of, delta_of = (lambda ref, h: ref[:, h:h + 1]), (lambda ref, h: ref[:, ATTN_HEADS + h:ATTN_HEADS + h + 1])
        bf = lambda ref, scl=None: [(ref[:, sl] if scl is None else ref[:, sl] * scl).astype(BF16) for sl in sls]
        qc, qn, kc, kp = bf(qc_ref, scale), bf(qn_ref, scale), bf(kc_ref), bf(kp_ref)
        vc, vp, doc, don = bf(vc_ref), bf(vp_ref), bf(doc_ref), bf(don_ref)
        p_c = [jnp.exp(jnp.where(mc, _dot(qc[h], kc[h], 1, 1), NEG) - lse_of(sc_ref, h)) for h in heads]
        p_p = [jnp.exp(jnp.where(mp, _dot(qc[h], kp[h], 1, 1), NEG) - lse_of(sc_ref, h)) for h in heads]
        p_x = [jnp.exp(jnp.where(mx, _dot(qn[h], kc[h], 1, 1), NEG) - lse_of(sn_ref, h)) for h in heads]
        ds_c = [(p_c[h] * (_dot(doc[h], vc[h], 1, 1) - delta_of(sc_ref, h))).astype(BF16) for h in heads]
        ds_p = [(p_p[h] * (_dot(doc[h], vp[h], 1, 1) - delta_of(sc_ref, h))).astype(BF16) for h in heads]
        ds_x = [(p_x[h] * (_dot(don[h], vc[h], 1, 1) - delta_of(sn_ref, h))).astype(BF16) for h in heads]
        for h in heads:
            at = lambda part: slice(part * w + h * hd, part * w + (h + 1) * hd)
            dqkv_ref[:, at(0)] = ((_dot(ds_c[h], kc[h], 1, 0) + _dot(ds_p[h], kp[h], 1, 0)) * scale).astype(BF16)
            dqkv_ref[:, at(1)] = (_dot(ds_c[h], qc[h], 0, 0) + _dot(ds_x[h], qn[h], 0, 0)).astype(BF16)
            dqkv_ref[:, at(2)] = (_dot(p_c[h].astype(BF16), doc[h], 0, 0) + _dot(p_x[h].astype(BF16), don[h], 0, 0)).astype(BF16)

    cur = pl.BlockSpec((ATTN_BLOCK, w), lambda r, n: (n, r))
    nxt = pl.BlockSpec((ATTN_BLOCK, w), lambda r, n: (jnp.minimum(n + 1, nb - 1), r))
    part = lambda j, row: pl.BlockSpec((ATTN_BLOCK, w), lambda r, n: (row(n), 3 * r + j))
    here, before, after = (lambda n: n), (lambda n: jnp.maximum(n - 1, 0)), (lambda n: jnp.minimum(n + 1, nb - 1))
    stat = lambda row: pl.BlockSpec((ATTN_BLOCK, LANES), lambda r, n: (row(n), r))
    do, stats = do.reshape(seq, dil * w), stats.reshape(seq, dil * LANES)
    dqkv = pl.pallas_call(
        body, name=name, grid=(dil, nb),
        in_specs=[part(0, here), part(0, after), cur, nxt, stat(here), stat(after), part(1, before), part(1, here),
                  part(2, before), part(2, here)], out_specs=pl.BlockSpec((ATTN_BLOCK, 3 * w), lambda r, n: (n, r)),
        out_shape=jax.ShapeDtypeStruct((seq, dil * 3 * w), BF16), compiler_params=_params(("parallel", "parallel")),
    )(qkv, qkv, do, do, stats, stats, qkv, qkv, qkv, qkv)
    return dqkv.reshape(t, 3 * w)


def _xattn_fwd(q, k, v, *, name):
    t, w = q.shape
    nm = k.shape[0]
    hd = w // XA_HEADS
    scale = hd ** -0.5
    tq = _tile(t, 512, 8)

    def body(q_ref, k_ref, v_ref, o_ref):
        heads = range(XA_HEADS)
        sls = [slice(h * hd, (h + 1) * hd) for h in heads]
        ss = [_dot((q_ref[:, sl] * scale).astype(BF16), k_ref[:, sl].astype(BF16), 1, 1) for sl in sls]
        ps = [jnp.exp(s - jnp.max(s, axis=-1, keepdims=True)) for s in ss]
        ps = [p / jnp.sum(p, axis=-1, keepdims=True) for p in ps]
        for h in heads:
            o_ref[:, sls[h]] = _dot(ps[h].astype(BF16), v_ref[:, sls[h]].astype(BF16), 1, 0).astype(BF16)

    qs = pl.BlockSpec((tq, w), lambda i: (i, 0))
    ks = pl.BlockSpec((nm, w), lambda i: (0, 0))
    return pl.pallas_call(
        body, name=name, grid=(t // tq,), in_specs=[qs, ks, ks], out_specs=qs,
        out_shape=jax.ShapeDtypeStruct((t, w), BF16), compiler_params=_params(("parallel",)),
    )(q, k, v)


def _xattn_bwd(q, k, v, do, *, name):
    t, w = q.shape
    nm = k.shape[0]
    hd = w // XA_HEADS
    scale = hd ** -0.5
    tq = _tile(t, 512, 8)

    def body(q_ref, k_ref, v_ref, do_ref, dq_ref, dk_ref, dv_ref):
        first = pl.program_id(0) == 0
        heads = range(XA_HEADS)
        sls = [slice(h * hd, (h + 1) * hd) for h in heads]
        qs_ = [(q_ref[:, sl] * scale).astype(BF16) for sl in sls]
        ks_, vs_, dos = ([ref[:, sl].astype(BF16) for sl in sls] for ref in (k_ref, v_ref, do_ref))
        ss = [_dot(qs_[h], ks_[h], 1, 1) for h in heads]
        ps = [jnp.exp(s - jnp.max(s, axis=-1, keepdims=True)) for s in ss]
        ps = [p / jnp.sum(p, axis=-1, keepdims=True) for p in ps]
        dps = [_dot(dos[h], vs_[h], 1, 1) for h in heads]
        dss = [(p * (dp - jnp.sum(p * dp, axis=-1, keepdims=True))).astype(BF16) for p, dp in zip(ps, dps)]
        for h in heads:
            sl = sls[h]
            dq_ref[:, sl] = (_dot(dss[h], ks_[h], 1, 0) * scale).astype(BF16)
            dk = _dot(dss[h], qs_[h], 0, 0)
            dv = _dot(ps[h].astype(BF16), dos[h], 0, 0)

            @pl.when(first)
            def _():
                dk_ref[:, sl] = dk
                dv_ref[:, sl] = dv

            @pl.when(jnp.logical_not(first))
            def _():
                dk_ref[:, sl] += dk
                dv_ref[:, sl] += dv

    qs = pl.BlockSpec((tq, w), lambda i: (i, 0))
    ks = pl.BlockSpec((nm, w), lambda i: (0, 0))
    return pl.pallas_call(
        body, name=name, grid=(t // tq,), in_specs=[qs, ks, ks, qs], out_specs=[qs, ks, ks],
        out_shape=[jax.ShapeDtypeStruct((t, w), BF16), jax.ShapeDtypeStruct((nm, w), F32),
                   jax.ShapeDtypeStruct((nm, w), F32)],
        compiler_params=_params(("arbitrary",)),
    )(q, k, v, do)


def _dn_head_terms(q_ref, k_ref, gc_ref, gcr_ref, b_ref, h, hd, j=0):
    c = DN_CHUNK
    sl, rs = slice(h * hd, (h + 1) * hd), slice(j * c, (j + 1) * c)
    qh, kh = q_ref[rs, sl], k_ref[rs, sl]
    gcc, gcr_h, bh = gc_ref[rs, h:h + 1], gcr_ref[j, h:h + 1, :], b_ref[rs, h:h + 1]
    row = lax.broadcasted_iota(jnp.int32, (c, c), 0)
    col = lax.broadcasted_iota(jnp.int32, (c, c), 1)
    decay = jnp.exp(jnp.where(row >= col, gcc - gcr_h, NEG))
    kb = kh * bh
    kkt = _dot(kb.astype(BF16), kh.astype(BF16), 1, 1)
    qkt = _dot(qh.astype(BF16), kh.astype(BF16), 1, 1)
    gl = gcc[c - 1:c, :]
    return dict(sl=sl, rs=rs, j=j, h=h, q=qh, k=kh, gcc=gcc, b=bh, row=row, col=col, decay=decay, kb=kb, kkt=kkt,
                qkt=qkt, e=jnp.exp(gcc), el=jnp.exp(gl), r=jnp.exp(gl - gcc))


def _dn_local_fwd(q, k, v, gc, gcr, beta, *, name):
    t, w = q.shape
    hd = w // DN_HEADS
    c = DN_CHUNK
    nch = t // c
    grp = _tile(nch, DN_LOCAL_CHUNKS, 1)

    def body(q_ref, k_ref, v_ref, gc_ref, gcr_ref, b_ref,
             u_ref, wq_ref, wqt_ref, kr_ref, krt_ref, qk_ref, qkt_ref, invt_ref):
        tm = [_dn_head_terms(q_ref, k_ref, gc_ref, gcr_ref, b_ref, h, hd, j) for j in range(grp) for h in range(DN_HEADS)]
        pw = [jnp.where(m['row'] > m['col'], m['kkt'] * m['decay'], 0.0) for m in tm]
        inv = [(m['row'] == m['col']).astype(F32) - p for m, p in zip(tm, pw)]
        for _ in range(int(math.log2(c)) - 1):
            pw = [_dot(p.astype(BF16), p.astype(BF16), 1, 0) for p in pw]
            inv = [i + _dot(i.astype(BF16), p.astype(BF16), 1, 0) for i, p in zip(inv, pw)]
        for m, iv in zip(tm, inv):
            j, h = m['j'], m['h']
            rhs = jnp.concatenate([v_ref[m['rs'], m['sl']] * m['b'], m['kb'] * m['e']], axis=1).astype(BF16)
            sol = _dot(iv.astype(BF16), rhs, 1, 0)
            u_ref[m['rs'], m['sl']] = sol[:, :hd]
            wq = jnp.concatenate([sol[:, hd:], m['q'] * m['e']], axis=0)
            kr = m['k'] * m['r']
            qk = m['qkt'] * m['decay']
            wq_ref[j, h], wqt_ref[j, h] = wq.astype(BF16), wq.T.astype(BF16)
            kr_ref[j, h], krt_ref[j, h] = kr.astype(BF16), kr.T.astype(BF16)
            qk_ref[j, h], qkt_ref[j, h] = qk.astype(BF16), qk.T.astype(BF16)
            invt_ref[j, h] = iv.T.astype(BF16)

    rows = pl.BlockSpec((grp * c, w), lambda n: (n, 0))
    cols = pl.BlockSpec((grp * c, DN_HEADS), lambda n: (n, 0))
    rowg = pl.BlockSpec((grp, DN_HEADS, c), lambda n: (n, 0, 0))
    per = lambda a, b: (pl.BlockSpec((grp, DN_HEADS, a, b), lambda n: (n, 0, 0, 0)),
                        jax.ShapeDtypeStruct((nch, DN_HEADS, a, b), BF16))
    outs = [(rows, jax.ShapeDtypeStruct((t, w), F32)), per(2 * c, hd), per(hd, 2 * c), per(c, hd), per(hd, c),
            per(c, c), per(c, c), per(c, c)]
    return pl.pallas_call(
        body, name=name, grid=(nch // grp,), in_specs=[rows, rows, rows, cols, rowg, cols],
        out_specs=[o[0] for o in outs], out_shape=[o[1] for o in outs], compiler_params=_params(("parallel",)),
    )(q, k, v, gc, gcr, beta)


def _dn_state_fwd(u, wq, krt, qk, gc, *, name):
    t, w = u.shape
    hd = w // DN_HEADS
    c = DN_CHUNK
    nch = t // c
    grp = _tile(nch, DN_STEP_CHUNKS, 1)

    def body(u_ref, wq_ref, krt_ref, qk_ref, gc_ref, o_ref, vn_ref, s_ref, state):
        @pl.when(pl.program_id(0) == 0)
        def _():
            state[...] = jnp.zeros_like(state)

        heads = range(DN_HEADS)
        sls = [slice(h * hd, (h + 1) * hd) for h in heads]
        shs = [state[h] for h in heads]
        for j in range(grp):
            rs = slice(j * c, (j + 1) * c)
            wss = [_dot(wq_ref[j, h], shs[h].astype(BF16), 1, 0) for h in heads]
            vns = [(u_ref[rs, sls[h]] - wss[h][:c]).astype(BF16) for h in heads]
            outs = [wss[h][c:] + _dot(qk_ref[j, h], vns[h], 1, 0) for h in heads]
            for h in heads:
                s_ref[j, h] = shs[h].astype(BF16)
                vn_ref[rs, sls[h]] = vns[h]
                o_ref[rs, sls[h]] = outs[h]
            shs = [shs[h] * jnp.exp(gc_ref[(j + 1) * c - 1:(j + 1) * c, h:h + 1]) + _dot(krt_ref[j, h], vns[h], 1, 0)
                   for h in heads]
        for h in heads:
            state[h] = shs[h]

    rows = pl.BlockSpec((grp * c, w), lambda n: (n, 0))
    per = lambda a, b: pl.BlockSpec((grp, DN_HEADS, a, b), lambda n: (n, 0, 0, 0))
    return pl.pallas_call(
        body, name=name, grid=(nch // grp,),
        in_specs=[rows, per(2 * c, hd), per(hd, c), per(c, c), pl.BlockSpec((grp * c, DN_HEADS), lambda n: (n, 0))],
        out_specs=[rows, rows, per(hd, hd)],
        out_shape=[jax.ShapeDtypeStruct((t, w), F32), jax.ShapeDtypeStruct((t, w), BF16),
                   jax.ShapeDtypeStruct((nch, DN_HEADS, hd, hd), BF16)],
        scratch_shapes=[pltpu.VMEM((DN_HEADS, hd, hd), F32)], compiler_params=_params(("arbitrary",)),
    )(u, wq, krt, qk, gc)


def _dn_state_bwd(do, qkt, kr, vn, states, wqt, gc, *, name):
    t, w = do.shape
    hd = w // DN_HEADS
    c = DN_CHUNK
    nch = t // c
    grp = _tile(nch, DN_STEP_CHUNKS, 1)

    def body(do_ref, qkt_ref, kr_ref, vn_ref, s_ref, wqt_ref, gc_ref, dvn_ref, dkr_ref, del_ref, dstate):
        @pl.when(pl.program_id(0) == 0)
        def _():
            dstate[...] = jnp.zeros_like(dstate)

        heads = range(DN_HEADS)
        sls = [slice(h * hd, (h + 1) * hd) for h in heads]
        dsns = [dstate[h] for h in heads]
        for j in reversed(range(grp)):
            rs = slice(j * c, (j + 1) * c)
            dsbs = [d.astype(BF16) for d in dsns]
            dobs = [do_ref[rs, sl].astype(BF16) for sl in sls]
            dvns = [(_dot(qkt_ref[j, h], dobs[h], 1, 0) + _dot(kr_ref[j, h], dsbs[h], 1, 0)).astype(BF16) for h in heads]
            dkrs = [_dot(vn_ref[rs, sls[h]], dsbs[h], 1, 1) for h in heads]
            for h in heads:
                dvn_ref[rs, sls[h]] = dvns[h]
                dkr_ref[rs, sls[h]] = dkrs[h]
                d_el = jnp.sum(jnp.sum(dsns[h] * s_ref[j, h].astype(F32), axis=1, keepdims=True), axis=0, keepdims=True)
                del_ref[j, h:h + 1, :] = jnp.broadcast_to(d_el, (1, LANES))
            dsns = [dsns[h] * jnp.exp(gc_ref[(j + 1) * c - 1:(j + 1) * c, h:h + 1])
                    + _dot(wqt_ref[j, h], jnp.concatenate([-dvns[h], dobs[h]], axis=0), 1, 0) for h in heads]
        for h in heads:
            dstate[h] = dsns[h]

    rev = lambda n: nch // grp - 1 - n
    rows = pl.BlockSpec((grp * c, w), lambda n: (rev(n), 0))
    per = lambda a, b: pl.BlockSpec((grp, DN_HEADS, a, b), lambda n: (rev(n), 0, 0, 0))
    return pl.pallas_call(
        body, name=name, grid=(nch // grp,),
        in_specs=[rows, per(c, c), per(c, hd), rows, per(hd, hd), per(hd, 2 * c),
                  pl.BlockSpec((grp * c, DN_HEADS), lambda n: (rev(n), 0))],
        out_specs=[rows, rows, pl.BlockSpec((grp, DN_HEADS, LANES), lambda n: (rev(n), 0, 0))],
        out_shape=[jax.ShapeDtypeStruct((t, w), BF16), jax.ShapeDtypeStruct((t, w), F32),
                   jax.ShapeDtypeStruct((nch, DN_HEADS, LANES), F32)],
        scratch_shapes=[pltpu.VMEM((DN_HEADS, hd, hd), F32)], compiler_params=_params(("arbitrary",)),
    )(do, qkt, kr, vn, states, wqt, gc)


def _dn_local_bwd(q, k, v, gc, gcr, beta, invt, u, wq, vn, states, do, dvn, dkr, d_el, *, name):
    t, w = q.shape
    hd = w // DN_HEADS
    c = DN_CHUNK
    nch = t // c
    grp = _tile(nch, DN_LOCAL_CHUNKS, 1)

    def body(q_ref, k_ref, v_ref, gc_ref, gcr_ref, b_ref, invt_ref, u_ref, wq_ref, vn_ref, s_ref, do_ref, dvn_ref,
             dkr_ref, del_ref, dq_ref, dk_ref, dv_ref, dgc_ref, dgr_ref, db_ref):
        tms = [_dn_head_terms(q_ref, k_ref, gc_ref, gcr_ref, b_ref, h, hd, j) for j in range(grp) for h in range(DN_HEADS)]
        dobs = [do_ref[m['rs'], m['sl']].astype(BF16) for m in tms]
        tss = [_dot(jnp.concatenate([dob, -dvn_ref[m['rs'], m['sl']]], axis=0), s_ref[m['j'], m['h']].astype(BF16), 1, 1)
               for m, dob in zip(tms, dobs)]
        d_qks = [_dot(dob, vn_ref[m['rs'], m['sl']], 1, 1) for m, dob in zip(tms, dobs)]
        d_rhss = [_dot(invt_ref[m['j'], m['h']],
                       jnp.concatenate([dvn_ref[m['rs'], m['sl']], ts[c:].astype(BF16)], axis=1), 1, 0)
                  for m, ts in zip(tms, tss)]
        d_as = [-_dot(d_rhs.astype(BF16), jnp.concatenate(
            [u_ref[m['rs'], m['sl']].astype(BF16), wq_ref[m['j'], m['h'], :c, :]], axis=1), 1, 1)
                for m, d_rhs in zip(tms, d_rhss)]
        for m, ts, d_qk, d_rhs, d_a in zip(tms, tss, d_qks, d_rhss, d_as):
            d_qe, j, h, rs = ts[:c], m['j'], m['h'], m['rs']
            sl, qh, kh, bh, e, r, decay = m['sl'], m['q'], m['k'], m['b'], m['e'], m['r'], m['decay']
            vh = v_ref[rs, sl]
            d_ru, d_rw = d_rhs[:, :hd], d_rhs[:, hd:]
            dv_ref[rs, sl] = d_ru * bh
            d_e = jnp.sum(d_rw * m['kb'], axis=1, keepdims=True) + jnp.sum(d_qe * qh, axis=1, keepdims=True)
            d_n = jnp.where(m['row'] > m['col'], d_a, 0.0)
            d_m, d_p = (d_n * decay).astype(BF16), (d_qk * decay).astype(BF16)
            tk = _dot(jnp.concatenate([d_m, d_p], axis=0), kh.astype(BF16), 1, 0)
            d_kb = d_rw * e + tk[:c]
            dq_ref[rs, sl] = tk[c:] + d_qe * e
            dk = _dot(d_m, m['kb'].astype(BF16), 0, 0) + _dot(d_p, qh.astype(BF16), 0, 0)
            dd = (d_n * m['kkt'] + d_qk * m['qkt']) * decay
            d_kr = dkr_ref[rs, sl]
            d_r = jnp.sum(d_kr * kh, axis=1, keepdims=True)
            d_gl = del_ref[j, h:h + 1, 0:1] * m['el'] + jnp.sum(d_r * r, axis=0, keepdims=True)
            last = lax.broadcasted_iota(jnp.int32, (c, 1), 0) == c - 1
            dgc_ref[rs, h:h + 1] = jnp.sum(dd, axis=1, keepdims=True) + d_e * e - d_r * r + jnp.where(last, d_gl, 0.0)
            dgr_ref[j, h:h + 1, :] = -jnp.sum(dd, axis=0, keepdims=True)
            dk_ref[rs, sl] = dk + d_kr * r + d_kb * bh
            db_ref[rs, h:h + 1] = jnp.sum(d_ru * vh, axis=1, keepdims=True) + jnp.sum(d_kb * kh, axis=1, keepdims=True)

    rows = pl.BlockSpec((grp * c, w), lambda n: (n, 0))
    cols = pl.BlockSpec((grp * c, DN_HEADS), lambda n: (n, 0))
    rowg = pl.BlockSpec((grp, DN_HEADS, c), lambda n: (n, 0, 0))
    per = lambda a, b: pl.BlockSpec((grp, DN_HEADS, a, b), lambda n: (n, 0, 0, 0))
    return pl.pallas_call(
        body, name=name, grid=(nch // grp,),
        in_specs=[rows, rows, rows, cols, rowg, cols, per(c, c), rows, per(2 * c, hd), rows, per(hd, hd), rows, rows,
                  rows, pl.BlockSpec((grp, DN_HEADS, LANES), lambda n: (n, 0, 0))],
        out_specs=[rows, rows, rows, cols, rowg, cols],
        out_shape=[jax.ShapeDtypeStruct((t, w), F32)] * 3
        + [jax.ShapeDtypeStruct((t, DN_HEADS), F32), jax.ShapeDtypeStruct((nch, DN_HEADS, c), F32),
           jax.ShapeDtypeStruct((t, DN_HEADS), F32)],
        compiler_params=_params(("parallel",)),
    )(q, k, v, gc, gcr, beta, invt, u, wq, vn, states, do, dvn, dkr, d_el)


def _other_chips():
    x, y = lax.axis_index("x"), lax.axis_index("y")
    return [(1 - x, y), (x, 1 - y), (1 - x, 1 - y)]


def _gather(src, *, name):
    def body(src_ref, out_ref, send_sems, recv_sems, local_sem):
        x, y, c = lax.axis_index("x"), lax.axis_index("y"), lax.axis_index("c")
        me, sibling, chips = (x, y, c), (x, y, 1 - c), _other_chips()
        slot = lambda px, py, pc: out_ref.at[4 * px + 2 * py + pc]

        def copy(k, block, to, own=False):
            return pltpu.make_async_remote_copy(
                src_ref=src_ref if own else slot(*block), dst_ref=slot(*block), send_sem=send_sems.at[k],
                recv_sem=recv_sems.at[k], device_id=to, device_id_type=MESH)

        local = pltpu.make_async_copy(src_ref, slot(*me), local_sem)
        local.start()
        first = [copy(0, me, sibling, own=True)] + [copy(1 + j, me, (*chip, c), own=True) for j, chip in enumerate(chips)]
        for cp in first:
            cp.start()
        passed = [copy(4 + j, (*chip, c), sibling) for j, chip in enumerate(chips)]
        for j, chip in enumerate(chips):
            copy(1 + j, (*chip, c), me).wait_recv()
            passed[j].start()
        copy(0, sibling, me).wait_recv()
        for j, chip in enumerate(chips):
            copy(4 + j, (*chip, 1 - c), me).wait_recv()
        for cp in first + passed:
            cp.wait_send()
        local.wait()

    return pl.pallas_call(
        body, name=name, out_shape=jax.ShapeDtypeStruct((N_DEV,) + src.shape, src.dtype),
        in_specs=[pl.BlockSpec(memory_space=pl.ANY)], out_specs=pl.BlockSpec(memory_space=pl.ANY),
        scratch_shapes=[pltpu.SemaphoreType.DMA((N_DEV - 1,)), pltpu.SemaphoreType.DMA((N_DEV - 1,)),
                        pltpu.SemaphoreType.DMA(())],
    )(src)


def _swap_with_sibling(src, *, name):
    def body(src_ref, out_ref, send_sem, recv_sem):
        x, y, c = lax.axis_index("x"), lax.axis_index("y"), lax.axis_index("c")
        cp = pltpu.make_async_remote_copy(src_ref=src_ref.at[1 - c], dst_ref=out_ref, send_sem=send_sem,
                                          recv_sem=recv_sem, device_id=(x, y, 1 - c), device_id_type=MESH)
        cp.start()
        cp.wait()

    return pl.pallas_call(
        body, name=name, out_shape=jax.ShapeDtypeStruct(src.shape[1:], src.dtype),
        in_specs=[pl.BlockSpec(memory_space=pl.ANY)], out_specs=pl.BlockSpec(memory_space=pl.ANY),
        scratch_shapes=[pltpu.SemaphoreType.DMA(()), pltpu.SemaphoreType.DMA(())],
    )(src)


def _add_own_half(src, got, *, name):
    _, n, r, c = src.shape
    tr = _tile(r, 512, 16)

    def body(s0_ref, s1_ref, got_ref, o_ref):
        own = jnp.where(lax.axis_index("c") == 0, s0_ref[0, 0], s1_ref[0, 0])
        o_ref[0] = (own.astype(F32) + got_ref[0].astype(F32)).astype(BF16)

    half = lambda h: pl.BlockSpec((1, 1, tr, c), lambda j, i: (h, j, i, 0))
    blk = pl.BlockSpec((1, tr, c), lambda j, i: (j, i, 0))
    return pl.pallas_call(
        body, name=name, grid=(n, r // tr), in_specs=[half(0), half(1), blk], out_specs=blk,
        out_shape=jax.ShapeDtypeStruct((n, r, c), BF16), compiler_params=_params(("parallel", "parallel")),
    )(src, src, got)


def _swap_between_chips(src, *, name):
    def body(src_ref, out_ref, send_sems, recv_sems, local_sem):
        x, y, c = lax.axis_index("x"), lax.axis_index("y"), lax.axis_index("c")
        mine = 2 * x + y
        local = pltpu.make_async_copy(src_ref.at[mine], out_ref.at[mine], local_sem)
        local.start()
        copies = [pltpu.make_async_remote_copy(
            src_ref=src_ref.at[2 * px + py], dst_ref=out_ref.at[mine], send_sem=send_sems.at[j],
            recv_sem=recv_sems.at[j], device_id=(px, py, c), device_id_type=MESH) for j, (px, py) in enumerate(_other_chips())]
        for cp in copies:
            cp.start()
        for j, (px, py) in enumerate(_other_chips()):
            pltpu.make_async_remote_copy(
                src_ref=src_ref.at[mine], dst_ref=out_ref.at[2 * px + py], send_sem=send_sems.at[j],
                recv_sem=recv_sems.at[j], device_id=(px, py, c), device_id_type=MESH).wait_recv()
        for cp in copies:
            cp.wait_send()
        local.wait()

    return pl.pallas_call(
        body, name=name, out_shape=jax.ShapeDtypeStruct(src.shape, src.dtype),
        in_specs=[pl.BlockSpec(memory_space=pl.ANY)], out_specs=pl.BlockSpec(memory_space=pl.ANY),
        scratch_shapes=[pltpu.SemaphoreType.DMA((3,)), pltpu.SemaphoreType.DMA((3,)), pltpu.SemaphoreType.DMA(())],
    )(src)


def _sum_adamw(parts, w, m, v, *, name):
    n_parts, r, c = parts.shape
    tr = _tile(r, 256, 8)
    c1, c2 = 1.0 - ADAM_B1 ** ADAM_STEP, 1.0 - ADAM_B2 ** ADAM_STEP

    def body(p_ref, w_ref, m_ref, v_ref, g_ref, d_ref, nm_ref, nv_ref):
        g = p_ref[0].astype(F32)
        for s in range(1, n_parts):
            g = g + p_ref[s].astype(F32)
        nm = ADAM_B1 * m_ref[...] + (1.0 - ADAM_B1) * g
        nv = ADAM_B2 * v_ref[...] + (1.0 - ADAM_B2) * (g * g)
        g_ref[...] = g
        nm_ref[...] = nm
        nv_ref[...] = nv
        d_ref[...] = -ADAM_LR * ((nm / c1) / (jnp.sqrt(nv / c2) + ADAM_EPS) + ADAM_WD * w_ref[...])

    blk = pl.BlockSpec((tr, c), lambda i: (i, 0))
    return pl.pallas_call(
        body, name=name, grid=(r // tr,), in_specs=[pl.BlockSpec((n_parts, tr, c), lambda i: (0, i, 0)), blk, blk, blk],
        out_specs=[blk] * 4, out_shape=[jax.ShapeDtypeStruct((r, c), F32)] * 4, compiler_params=_params(("parallel",)),
    )(parts, w, m, v)


def _pack_rows(n):
    return -(-n // (PACK_COLS * PACK_ROWS)) * PACK_ROWS


def _pack(blocks):
    parts, spans, at = [], [], 0
    for blk, n_lead in blocks:
        lead = blk.shape[:n_lead]
        n = math.prod(blk.shape[n_lead:])
        rows = _pack_rows(n)
        flat = blk.reshape(lead + (n,))
        flat = jnp.pad(flat, [(0, 0)] * n_lead + [(0, rows * PACK_COLS - n)])
        parts.append(flat.reshape(lead + (rows, PACK_COLS)))
        spans.append((at, rows))
        at += rows
    return jnp.concatenate(parts, axis=-2), spans


def _unpack(buf, span, shape):
    at, rows = span
    lead = buf.shape[:-2]
    flat = lax.slice_in_dim(buf, at, at + rows, axis=buf.ndim - 2).reshape(lead + (rows * PACK_COLS,))
    return lax.slice_in_dim(flat, 0, math.prod(shape), axis=len(lead)).reshape(lead + tuple(shape))


def _join_shards(g, axis):
    g = jnp.moveaxis(g, 0, axis)
    return g.reshape(g.shape[:axis] + (g.shape[axis] * g.shape[axis + 1],) + g.shape[axis + 2:])


def _split_shards(full, axis):
    s = full.shape
    g = full.reshape(s[:axis] + (N_DEV, s[axis] // N_DEV) + s[axis + 1:])
    return jnp.moveaxis(g, axis, 0)


def _residual_out(a, w, x, scale, next_gain, name, **tiles):
    out = _mm(a, w, name=name, res=x, scale=scale, norm=next_gain, **tiles)
    return out if next_gain is not None else (out, None)


def _ffn_fwd(x, h, w_in, w_out, tag, next_gain):
    gate, up, a = _mm_swiglu(h, w_in, name=tag + "_in", tm=512, tn=2816)
    xo, hn = _residual_out(a, w_out, x, 0.5, next_gain, tag + "_out", tk=2816)
    return xo, hn, (x, h, gate, up, a)


def _ffn_bwd(saved, g, w_in, w_out, dxo, tag):
    x, h, gate, up, a = saved
    d_w_out = _mm(a, dxo, name=tag + "_dwout", ta=True, scale=0.5, tm=1408, tk=2048)
    du = _mm_dswiglu(dxo, w_out, gate, up, name=tag + "_da", scale=0.5)
    d_w_in = _mm(du, h, name=tag + "_dwin", ta=True, tm=1408, tk=2048)
    dx, dg = _mm(du, w_in, name=tag + "_dh", tm=512, tk=5632, rms_bwd=(x, g, dxo))
    return dx, dg, d_w_in, d_w_out


def _block_diag(w):
    n, j, k = w.shape
    return (w[:, :, None, :] * jnp.eye(n, dtype=w.dtype)[:, None, :, None]).reshape(n * j, n * k)


def _diag_blocks(dense, n):
    j, k = dense.shape[0] // n, dense.shape[1] // n
    return jnp.stack([dense[i * j:(i + 1) * j, i * k:(i + 1) * k] for i in range(n)], axis=0)


def _attn_lru_fwd(x, h, p, tag, next_gain):
    aw = ATTN_HEADS * 64
    proj = _mm(h, p['ab_w_in'], name=tag + "_in", tb=True, tn=1280)
    qkv = proj[:, :3 * aw].astype(BF16)
    views = {dil: qkv.reshape(qkv.shape[0] // dil, dil * 3 * aw) for _, dil in DILATED_PATTERNS}
    outs, lses = [], []
    for window, dil in DILATED_PATTERNS:
        assert window // dil == ATTN_BLOCK
        o, l = _dattn_fwd(views[dil], dil, name=f"{tag}_attn{dil}")
        outs.append(o)
        lses.append(l)
    wa, wx = _block_diag(p['lru_w_a']).astype(BF16), _block_diag(p['lru_w_x']).astype(BF16)
    sp, sp_vjp = jax.vjp(lambda lam: LRU_C * jax.nn.softplus(-lam), p['lru_lambda'])
    a, b, xc = _lru_gates_fwd(proj, 3, p['lru_conv_w'], p['lru_conv_b'], wa, p['lru_b_a'], wx, p['lru_b_x'], sp,
                              name=tag + "_gates")
    hs = _scan(a, b, name=tag + "_scan")
    cat, attn, lse_all = _mix_join_fwd(outs, lses, hs, proj, 4, name=tag + "_join")
    xo, hn = _residual_out(cat, p['ab_w_out'], x, 1.0, next_gain, tag + "_out")
    return xo, hn, (x, h, proj, views, attn, lse_all, a, hs, xc, wa, wx, sp, sp_vjp, cat)


def _attn_lru_bwd(saved, p, dxo, tag):
    x, h, proj, views, attn, lse_all, a, hs, xc, wa, wx, sp, sp_vjp, cat = saved
    aw = attn.shape[1]
    g = {'ab_w_out': _mm(cat, dxo, name=tag + "_dwout", ta=True)}
    dcat = _mm(dxo, p['ab_w_out'], name=tag + "_dcat", tb=True)
    stats, dhs, dgr = _mix_join_bwd(dcat, attn, lse_all, hs, proj, 4, name=tag + "_djoin")
    a_next = jnp.concatenate([a[1:], jnp.zeros_like(a[:1])], axis=0)
    dtot = _scan(a_next, dhs, name=tag + "_dscan", reverse=True)
    h_prev = jnp.concatenate([jnp.zeros_like(hs[:1]), hs[:-1]], axis=0)
    dxc, dwa, dwx, vecs = _lru_gates_bwd(xc, dtot, h_prev, wa, p['lru_b_a'], wx, p['lru_b_x'], sp, name=tag + "_dgates")
    dxr, g['lru_conv_w'] = _conv_bwd(dxc, proj, 3, p['lru_conv_w'], name=tag + "_dconv")
    g['lru_b_a'], g['lru_b_x'], g['lru_conv_b'] = vecs[0], vecs[1], vecs[3]
    g['lru_lambda'], = sp_vjp(vecs[2])
    g['lru_w_a'], g['lru_w_x'] = _diag_blocks(dwa, LRU_BLOCKS), _diag_blocks(dwx, LRU_BLOCKS)
    dattn = dcat[:, :aw]
    dqkv = sum(_dattn_bwd(views[dil], dattn, stats, dil, name=f"{tag}_dattn{dil}").astype(F32)
               for _, dil in DILATED_PATTERNS)
    dproj = jnp.concatenate([dqkv.astype(BF16), dxr, dgr], axis=-1)
    g['ab_w_in'] = _mm(dproj, h, name=tag + "_dwin", ta=True, tm=1280)
    dx, g['mix_norm'] = _mm(dproj, p['ab_w_in'], name=tag + "_dh", tm=512, tk=2560, rms_bwd=(x, p['mix_norm'], dxo))
    return dx, g


def _dn_decay(a, b, a_log, dt_bias):
    t = a.shape[0]
    g = -jnp.exp(a_log) * jax.nn.softplus(a + dt_bias)
    gc = jnp.cumsum(g.reshape(t // DN_CHUNK, DN_CHUNK, DN_HEADS), axis=1)
    return gc.reshape(t, DN_HEADS), jnp.swapaxes(gc, 1, 2), jax.nn.sigmoid(b)


def _dn_in_width(w):
    return -(-(4 * w + 2 * DN_HEADS) // LANES) * LANES


def _deltanet_fwd(x, h, p, tag, next_gain):
    w = p['dn_w_out'].shape[0]
    proj = _mm(h, p['dn_w_in'], name=tag + "_in", tb=True, tn=1408)
    q, k, v, c = _dn_prep_fwd(proj, p['dn_conv_w'], name=tag + "_prep")
    a, b = proj[:, 4 * w:4 * w + DN_HEADS], proj[:, 4 * w + DN_HEADS:4 * w + 2 * DN_HEADS]
    (gc, gcr, beta), decay_vjp = jax.vjp(_dn_decay, a, b, p['dn_a_log'], p['dn_dt_bias'])
    prep = (q, k, v, gc, gcr, beta)
    u, wq, wqt, kr, krt, qk, qkt, invt = _dn_local_fwd(*prep, name=tag + "_local")
    o, vn, states = _dn_state_fwd(u, wq, krt, qk, gc, name=tag + "_state")
    og = _dn_gate_fwd(o, proj, 3, p['dn_o_norm'], name=tag + "_gate")
    xo, hn = _residual_out(og, p['dn_w_out'], x, 1.0, next_gain, tag + "_out")
    return xo, hn, (x, h, proj, c, o, prep, (u, wq, wqt, kr, qkt, invt, vn, states), decay_vjp, og)


def _deltanet_bwd(saved, p, dxo, tag):
    x, h, proj, c, o, prep, (u, wq, wqt, kr, qkt, invt, vn, states), decay_vjp, og = saved
    g = {'dn_w_out': _mm(og, dxo, name=tag + "_dwout", ta=True)}
    dog = _mm(dxo, p['dn_w_out'], name=tag + "_dog", tb=True)
    do, dz, g['dn_o_norm'] = _dn_gate_bwd(o, proj, 3, p['dn_o_norm'], dog, name=tag + "_dgate")
    dvn, dkr, d_el = _dn_state_bwd(do, qkt, kr, vn, states, wqt, prep[3], name=tag + "_dstate")
    dq, dk, dv, dgc, dgr, dbeta = _dn_local_bwd(*prep, invt, u, wq, vn, states, do, dvn, dkr, d_el, name=tag + "_dlocal")
    dc = _dn_prep_bwd(c, dq, dk, dv, name=tag + "_dprep")
    dqkv, g['dn_conv_w'] = _conv_bwd(dc, proj, 0, p['dn_conv_w'], name=tag + "_dconv")
    da, db, g['dn_a_log'], g['dn_dt_bias'] = decay_vjp((dgc, dgr, dbeta))
    t = x.shape[0]
    pad = jnp.zeros((t, p['dn_w_in'].shape[0] - dqkv.shape[1] - dz.shape[1] - 2 * DN_HEADS), BF16)
    groups = [("qkv", dqkv), ("z", dz), ("ab", jnp.concatenate([da.astype(BF16), db.astype(BF16), pad], axis=-1))]
    parts, dh, lo = [], None, 0
    for label, piece in groups:
        hi = lo + piece.shape[1]
        parts.append(_mm(piece, h, name=f"{tag}_dwin_{label}", ta=True, tm=1536, tk=2048))
        last = dict(tm=512, rms_bwd=(x, p['mix_norm'], dxo)) if label == groups[-1][0] else {}
        dh = _mm(piece, p['dn_w_in'][lo:hi], name=f"{tag}_dh_{label}", tk=1536, res=dh, **last)
        lo = hi
    g['dn_w_in'] = jnp.concatenate(parts, axis=0)
    dx, g['mix_norm'] = dh
    return dx, g


def _xattn_block_fwd(x, h, mem, p, tag, next_gain):
    w = x.shape[1]
    mh = _rms_fwd(mem, p['xa_mem_norm'], name=tag + "_mnorm", out_dtype=BF16)
    q = _mm(h, p['xa_wq'], name=tag + "_q")
    kv = _mm(mh, p['xa_wkv'], name=tag + "_kv", tb=True)
    k, v = kv[:, :w], kv[:, w:]
    o = _xattn_fwd(q, k, v, name=tag + "_attn").astype(BF16)
    xo, hn = _residual_out(o, p['xa_wo'], x, 1.0, next_gain, tag + "_out")
    return xo, hn, (x, h, mh, q, k, v, o)


def _xattn_block_bwd(saved, mem, p, dxo, tag):
    x, h, mh, q, k, v, o = saved
    g = {'xa_wo': _mm(o, dxo, name=tag + "_dwo", ta=True)}
    do = _mm(dxo, p['xa_wo'], name=tag + "_do", tb=True)
    dq, dk, dv = _xattn_bwd(q, k, v, do, name=tag + "_dattn")
    dq = dq.astype(BF16)
    dkv = jnp.concatenate([dk, dv], axis=-1).astype(BF16)
    g['xa_wq'] = _mm(h, dq, name=tag + "_dwq", ta=True)
    g['xa_wkv'] = _mm(dkv, mh, name=tag + "_dwkv", ta=True)
    dmh = _mm(dkv, p['xa_wkv'], name=tag + "_dmh")
    _, g['xa_mem_norm'] = _rms_bwd(mem, p['xa_mem_norm'], dmh, None, name=tag + "_dmnorm")
    dx, g['xa_norm'] = _mm(dq, p['xa_wq'], name=tag + "_dh", tb=True, tm=512, rms_bwd=(x, p['xa_norm'], dxo))
    return dx, g


def _layer_params(full, layer):
    p = {n: full[n][layer] for n in ('ffn1_norm', 'ffn1_w_in', 'ffn1_w_out', 'mix_norm', 'xa_norm', 'xa_mem_norm',
                                     'xa_wq', 'xa_wkv', 'xa_wo', 'ffn2_norm', 'ffn2_w_in', 'ffn2_w_out')}
    mixer = ('ab_w_in', 'lru_conv_w', 'lru_conv_b', 'lru_w_a', 'lru_b_a', 'lru_w_x', 'lru_b_x', 'lru_lambda', 'ab_w_out') \
        if layer % 2 == 0 else ('dn_w_in', 'dn_conv_w', 'dn_a_log', 'dn_dt_bias', 'dn_o_norm', 'dn_w_out')
    p.update({n: full[n][layer // 2] for n in mixer})
    return p


def _step(x, mem, target, full, depth):
    saved = []
    params = [_layer_params(full, layer) for layer in range(depth)]
    h = _rms_fwd(x, params[0]['ffn1_norm'], name="l0_ffn1_norm", out_dtype=BF16)
    for layer, p in enumerate(params):
        tag = f"l{layer}"
        after = params[layer + 1]['ffn1_norm'] if layer + 1 < depth else None
        x, h, s1 = _ffn_fwd(x, h, p['ffn1_w_in'], p['ffn1_w_out'], tag + "_ffn1", p['mix_norm'])
        x, h, s2 = (_attn_lru_fwd if layer % 2 == 0 else _deltanet_fwd)(x, h, p, tag + "_mix", p['xa_norm'])
        x, h, s3 = _xattn_block_fwd(x, h, mem, p, tag + "_xa", p['ffn2_norm'])
        x, h, s4 = _ffn_fwd(x, h, p['ffn2_w_in'], p['ffn2_w_out'], tag + "_ffn2", after)
        saved.append((p, s1, s2, s3, s4))
    sq, dx, d_final = _final_loss(x, full['final_norm'], target, name="final_loss")
    per_layer = []
    for layer in reversed(range(depth)):
        p, s1, s2, s3, s4 = saved[layer]
        tag = f"l{layer}"
        g = {}
        dx, g['ffn2_norm'], g['ffn2_w_in'], g['ffn2_w_out'] = _ffn_bwd(s4, p['ffn2_norm'], p['ffn2_w_in'], p['ffn2_w_out'], dx, tag + "_ffn2")
        dx, gx = _xattn_block_bwd(s3, mem, p, dx, tag + "_xa")
        dx, gm = (_attn_lru_bwd if layer % 2 == 0 else _deltanet_bwd)(s2, p, dx, tag + "_mix")
        dx, g['ffn1_norm'], g['ffn1_w_in'], g['ffn1_w_out'] = _ffn_bwd(s1, p['ffn1_norm'], p['ffn1_w_in'], p['ffn1_w_out'], dx, tag + "_ffn1")
        g.update(gx)
        g.update(gm)
        per_layer.insert(0, g)
    grads = {n: [g[n] for g in per_layer if n in g] for n in WEIGHTS[:-1]}
    grads['final_norm'] = d_final
    return sq, dx, grads


def kernel(x, mem, ffn1_norm, ffn1_w_in, ffn1_w_out, mix_norm, xa_norm, xa_mem_norm, xa_wq, xa_wkv, xa_wo, ffn2_norm, ffn2_w_in, ffn2_w_out, ab_w_in, lru_conv_w, lru_conv_b, lru_w_a, lru_b_a, lru_w_x, lru_b_x, lru_lambda, ab_w_out, dn_w_in, dn_conv_w, dn_a_log, dn_dt_bias, dn_o_norm, dn_w_out, final_norm, loss_target, m_ffn1_norm, m_ffn1_w_in, m_ffn1_w_out, m_mix_norm, m_xa_norm, m_xa_mem_norm, m_xa_wq, m_xa_wkv, m_xa_wo, m_ffn2_norm, m_ffn2_w_in, m_ffn2_w_out, m_ab_w_in, m_lru_conv_w, m_lru_conv_b, m_lru_w_a, m_lru_b_a, m_lru_w_x, m_lru_b_x, m_lru_lambda, m_ab_w_out, m_dn_w_in, m_dn_conv_w, m_dn_a_log, m_dn_dt_bias, m_dn_o_norm, m_dn_w_out, m_final_norm, v_ffn1_norm, v_ffn1_w_in, v_ffn1_w_out, v_mix_norm, v_xa_norm, v_xa_mem_norm, v_xa_wq, v_xa_wkv, v_xa_wo, v_ffn2_norm, v_ffn2_w_in, v_ffn2_w_out, v_ab_w_in, v_lru_conv_w, v_lru_conv_b, v_lru_w_a, v_lru_b_a, v_lru_w_x, v_lru_b_x, v_lru_lambda, v_ab_w_out, v_dn_w_in, v_dn_conv_w, v_dn_a_log, v_dn_dt_bias, v_dn_o_norm, v_dn_w_out, v_final_norm):
    args = dict(locals())
    flip = lambda n, a: jnp.swapaxes(a, 1, 2) if n in TRANSPOSED else a
    local = {n: flip(n, args[n]) for n in WEIGHTS}
    depth = ffn1_norm.shape[0]
    matrices = [n for n in WEIGHTS if n in SHARD_AXIS and n not in GATHER_F32]

    def entries(get):
        return [(a, n, l) for n in WEIGHTS for l, a in
                (enumerate(get(n)) if n in matrices else [(None, get(n))])]

    mats = [e for e in entries(lambda n: local[n]) if e[1] in matrices]
    send16, spans16 = _pack([(a.astype(BF16), 0) for a, _, _ in mats])
    send32, spans32 = _pack([(local[n], 0) for n in GATHER_F32])
    got16 = _gather(send16, name="gather_matrices")
    got32 = _gather(send32, name="gather_filters")
    full = {n: ([None] * local[n].shape[0] if n in matrices else local[n]) for n in WEIGHTS}
    for (a, n, l), span in zip(mats, spans16):
        full[n][l] = _unpack(got16, span, a.shape).reshape(N_DEV * a.shape[0], a.shape[1])
    for n, span in zip(GATHER_F32, spans32):
        full[n] = _join_shards(_unpack(got32, span, local[n].shape), SHARD_AXIS[n])
    dn_rows = full['dn_w_in'][0].shape[0]
    full['dn_w_in'] = [jnp.pad(w, ((0, _dn_in_width(w.shape[1]) - dn_rows), (0, 0))) for w in full['dn_w_in']]

    sq, dx, grads = _step(x[0], mem[0], loss_target[0], full, depth)
    grads['dn_w_in'] = [g[:dn_rows] for g in grads['dn_w_in']]
    loss = lax.psum(0.5 * jnp.sum(sq) / x.shape[2], ("x", "y", "c"))

    by_core = lambda z: jnp.swapaxes(z.reshape((N_DEV // 2, 2) + z.shape[1:]), 0, 1)

    def contribution(n):
        if n in matrices:
            return [by_core(g.reshape((N_DEV, g.shape[0] // N_DEV, g.shape[1]))) for g in grads[n]]
        g = grads[n] if n == 'final_norm' else jnp.stack(grads[n], axis=0)
        return by_core(_split_shards(g, SHARD_AXIS[n]) if n in SHARD_AXIS else jnp.broadcast_to(g, (N_DEV,) + g.shape))

    send, spans = _pack([(a.astype(BF16), 2) for a, _, _ in entries(contribution)])
    got = _swap_with_sibling(send, name="grads_to_sibling")
    chip_sum = _add_own_half(send, got, name="grads_chip_sum")
    parts = _swap_between_chips(chip_sum, name="grads_between_chips")
    state = [_pack([(a, 0) for a, _, _ in entries(lambda n: flip(n, args[pre + n]))])[0] for pre in ("", "m_", "v_")]
    outs = _sum_adamw(parts, *state, name="sum_adamw")
    result = []
    for o in outs:
        got_rows = {}
        for (a, n, l), span in zip(entries(lambda n: local[n]), spans):
            got_rows.setdefault(n, []).append(_unpack(o, span, a.shape))
        result += [flip(n, jnp.stack(got_rows[n], axis=0) if n in matrices else got_rows[n][0]) for n in WEIGHTS]
    return (loss, dx[None], *result)
```
